```python
import jax, jax.numpy as jnp
from jax import lax
import numpy as np

D_MODEL = 1024
BATCH = 8
SEQ = 2048
DEPTH = 1

GRID_W = 64
CTX_LEN = 256
D_RNN = 512
RNN_HEADS = 8
RNN_HEAD_DIM = D_RNN // RNN_HEADS
RNN_CONV_W = 4
RNN_CONV_LEFT = 2
RG_C = 8.0
D_CONV = 512
CONV_GROUPS = 8
CONV_W = 3
D_CONV_H = D_CONV // 2
D_MIX = D_RNN + D_CONV
D_IN = 2 * D_RNN + 3 * D_CONV
N_GROUPS = 4
EXPERTS_PER_GROUP = 8
N_EXPERTS = N_GROUPS * EXPERTS_PER_GROUP
TOP_K = 2
D_EXPERT = 512
MOE_BLOCK = 128
NORM_EPS = 1e-6

kernel_name = "hybrid_rglru_shortconv_hiermoe_dit"


def rmsnorm(x, g):
    xf = x.astype(jnp.float32)
    y = xf * lax.rsqrt(jnp.mean(xf * xf, axis=-1, keepdims=True) + NORM_EPS)
    return (y * g.astype(jnp.float32)).astype(x.dtype)


def shift_conv(x, w, axis, left):
    k_w = w.shape[0]
    pad = [(0, 0)] * x.ndim
    pad[axis] = (left, k_w - 1 - left)
    xp = jnp.pad(x, pad)
    n = x.shape[axis]
    return sum(lax.slice_in_dim(xp, k, k + n, axis=axis) * w[k] for k in range(k_w))


def grid_short_conv(u, w, rows):
    bn, n, ch = u.shape
    ug = u.reshape(bn, rows, GRID_W, ch)
    uh = shift_conv(ug[..., :D_CONV_H], w[:, :D_CONV_H], axis=2, left=1)
    uv = shift_conv(ug[..., D_CONV_H:], w[:, D_CONV_H:], axis=1, left=1)
    return jnp.concatenate([uh, uv], axis=-1).reshape(bn, n, ch)


def rglru_coeffs(xc, wa, ba, wx, bx, lam):
    bn, n, _ = xc.shape
    xh = xc.reshape(bn, n, RNN_HEADS, RNN_HEAD_DIM)
    r = jax.nn.sigmoid(jnp.einsum('blhi,hij->blhj', xh, wa).reshape(bn, n, D_RNN) + ba)
    i = jax.nn.sigmoid(jnp.einsum('blhi,hij->blhj', xh, wx).reshape(bn, n, D_RNN) + bx)
    log_a = (-RG_C * r.astype(jnp.float32)) * jax.nn.softplus(-lam.astype(jnp.float32))
    a = jnp.exp(log_a)
    b = jnp.sqrt(-jnp.expm1(2.0 * log_a)) * (i * xc).astype(jnp.float32)
    return a, b


def _combine(e1, e2):
    a1, b1 = e1
    a2, b2 = e2
    return a1 * a2, a2 * b1 + b2


def linear_scan(a, b, h0, reverse):
    a_cum, h = lax.associative_scan(_combine, (a, b), axis=1, reverse=reverse)
    return h + a_cum * h0[:, None]


def hier_moe(xt, wg, bg, we, be, w_gate, w_up, w_down):
    t_n, d = xt.shape
    xf = xt.astype(jnp.float32)
    g_logits = xf @ wg.astype(jnp.float32) + bg.astype(jnp.float32)
    p_group = jax.nn.softmax(g_logits, axis=-1)
    grp = jnp.argmax(g_logits, axis=-1)
    p_g = jnp.max(p_group, axis=-1, keepdims=True)
    e_logits = (xf @ we.astype(jnp.float32) + be.astype(jnp.float32)).reshape(t_n, N_GROUPS, EXPERTS_PER_GROUP)
    e_in_grp = e_logits[jnp.arange(t_n), grp]
    top_p, top_local = lax.top_k(jax.nn.softmax(e_in_grp, axis=-1), TOP_K)
    gates = p_g * top_p / jnp.sum(top_p, axis=-1, keepdims=True)
    top_e = grp[:, None] * EXPERTS_PER_GROUP + top_local

    n_assign = t_n * TOP_K
    n_blocks = (n_assign + N_EXPERTS * (MOE_BLOCK - 1) + MOE_BLOCK - 1) // MOE_BLOCK
    flat_e = top_e.reshape(n_assign).astype(jnp.int32)
    order = jnp.argsort(flat_e)
    sorted_e = flat_e[order]
    counts = jnp.bincount(flat_e, length=N_EXPERTS)
    starts = jnp.cumsum(counts) - counts
    pcounts = (counts + MOE_BLOCK - 1) // MOE_BLOCK * MOE_BLOCK
    pends = jnp.cumsum(pcounts)
    pstarts = pends - pcounts
    dest = pstarts[sorted_e] + jnp.arange(n_assign) - starts[sorted_e]
    slot_tok = jnp.full((n_blocks * MOE_BLOCK,), t_n, jnp.int32).at[dest].set((order // TOP_K).astype(jnp.int32))
    x_pad = jnp.concatenate([xt, jnp.zeros((1, d), xt.dtype)], axis=0)
    xb = x_pad[slot_tok].reshape(n_blocks, MOE_BLOCK, d)
    block_e = jnp.minimum(jnp.sum(jnp.arange(n_blocks)[:, None] * MOE_BLOCK >= pends[None, :], axis=1), N_EXPERTS - 1)

    def run_block(args):
        xblk, e = args
        return (jax.nn.silu(xblk @ w_gate[e]) * (xblk @ w_up[e])) @ w_down[e]

    yb = lax.map(run_block, (xb, block_e)).reshape(n_blocks * MOE_BLOCK, d)
    y_assign = jnp.zeros((n_assign, d), yb.dtype).at[order].set(yb[dest])
    return jnp.einsum('tkd,tk->td', y_assign.reshape(t_n, TOP_K, d), gates.astype(yb.dtype))


def block(x, ctx_s, c, c_ctx, rows, last, ada_w, ada_b, norm1_g, norm2_g, w_in, rnn_conv_w, rnn_conv_b,
          rg_wa, rg_ba, rg_wx, rg_bx, rg_lambda, sc_conv_w, w_out, router_group_w, router_group_b,
          router_exp_w, router_exp_b, exp_w_gate, exp_w_up, exp_w_down):
    bn = x.shape[0]
    d = x.shape[-1]
    mod = jax.nn.silu(c) @ ada_w + ada_b
    sh1, sc1, g1, sh2, sc2, g2 = [m[:, None] for m in jnp.split(mod, 6, axis=-1)]
    mod_c = jax.nn.silu(c_ctx) @ ada_w + ada_b
    csh1, csc1, cg1, csh2, csc2, cg2 = jnp.split(mod_c, 6, axis=-1)

    splits = [D_RNN, 2 * D_RNN, 2 * D_RNN + D_CONV, 2 * D_RNN + 2 * D_CONV]
    h_lat = rmsnorm(x, norm1_g) * (1 + sc1) + sh1
    h_ctx = rmsnorm(ctx_s, norm1_g) * (1 + csc1) + csh1
    xr_l, gr_l, v_l, bgt_l, cgt_l = jnp.split(h_lat @ w_in, splits, axis=-1)
    xr_c, gr_c, v_c, bgt_c, cgt_c = jnp.split(h_ctx @ w_in, splits, axis=-1)

    xc_l = shift_conv(xr_l, rnn_conv_w, 1, RNN_CONV_LEFT) + rnn_conv_b
    xc_c = shift_conv(xr_c, rnn_conv_w, 1, RNN_CONV_LEFT) + rnn_conv_b
    hsum_l = jnp.zeros(xr_l.shape, jnp.float32)
    hsum_c = jnp.zeros(xr_c.shape, jnp.float32)
    for di, reverse in enumerate((False, True)):
        a_c, b_c = rglru_coeffs(xc_c, rg_wa[di], rg_ba[di], rg_wx[di], rg_bx[di], rg_lambda[di])
        h_c = linear_scan(a_c, b_c, jnp.zeros((bn, D_RNN), jnp.float32), reverse)
        h0 = h_c[:, 0] if reverse else h_c[:, -1]
        a_l, b_l = rglru_coeffs(xc_l, rg_wa[di], rg_ba[di], rg_wx[di], rg_bx[di], rg_lambda[di])
        hsum_l = hsum_l + linear_scan(a_l, b_l, h0, reverse)
        if not last:
            hsum_c = hsum_c + h_c

    y_rnn_l = jax.nn.gelu(gr_l, approximate=True) * hsum_l.astype(x.dtype)
    y_conv_l = bgt_l * grid_short_conv(cgt_l * v_l, sc_conv_w, rows)
    x = x + g1 * (jnp.concatenate([y_rnn_l, y_conv_l], axis=-1) @ w_out)
    if not last:
        y_rnn_c = jax.nn.gelu(gr_c, approximate=True) * hsum_c.astype(ctx_s.dtype)
        y_conv_c = bgt_c * shift_conv(cgt_c * v_c, sc_conv_w, 1, 1)
        ctx_s = ctx_s + cg1 * (jnp.concatenate([y_rnn_c, y_conv_c], axis=-1) @ w_out)

    m_l = (rmsnorm(x, norm2_g) * (1 + sc2) + sh2).reshape(-1, d)
    moe_args = (router_group_w, router_group_b, router_exp_w, router_exp_b, exp_w_gate, exp_w_up, exp_w_down)
    if last:
        x = x + g2 * hier_moe(m_l, *moe_args).reshape(x.shape)
        return x, None
    m_c = (rmsnorm(ctx_s, norm2_g) * (1 + csc2) + csh2).reshape(-1, d)
    n_ctx_tok = m_c.shape[0]
    y_all = hier_moe(jnp.concatenate([m_c, m_l], axis=0), *moe_args)
    ctx_s = ctx_s + cg2 * y_all[:n_ctx_tok].reshape(ctx_s.shape)
    x = x + g2 * y_all[n_ctx_tok:].reshape(x.shape)
    return x, ctx_s


def setup_inputs(seed: int = 0) -> dict:
    key = jax.random.key(seed)
    ks = jax.random.split(key, 32)
    f32 = jnp.float32
    nrm = lambda k, shape, s: jax.random.normal(k, shape, f32) * s
    a0 = jax.random.uniform(ks[12], (DEPTH, 2, D_RNN), f32, minval=0.9, maxval=0.999)
    return {
        "x": nrm(ks[0], (BATCH, SEQ, D_MODEL), 1.0),
        "c": nrm(ks[1], (BATCH, D_MODEL), 1.0),
        "ctx": nrm(ks[2], (BATCH, CTX_LEN, D_MODEL), 1.0),
        "c_ctx": nrm(ks[3], (D_MODEL,), 1.0),
        "ada_w": nrm(ks[4], (DEPTH, D_MODEL, 6 * D_MODEL), 0.5 * D_MODEL ** -0.5),
        "ada_b": nrm(ks[5], (DEPTH, 6 * D_MODEL), 0.01),
        "norm1_g": 1.0 + nrm(ks[6], (DEPTH, D_MODEL), 0.02),
        "norm2_g": 1.0 + nrm(ks[7], (DEPTH, D_MODEL), 0.02),
        "w_in": nrm(ks[8], (DEPTH, D_MODEL, D_IN), D_MODEL ** -0.5),
        "rnn_conv_w": nrm(ks[9], (DEPTH, RNN_CONV_W, D_RNN), RNN_CONV_W ** -0.5),
        "rnn_conv_b": nrm(ks[10], (DEPTH, D_RNN), 0.01),
        "rg_wa": nrm(ks[11], (DEPTH, 2, RNN_HEADS, RNN_HEAD_DIM, RNN_HEAD_DIM), RNN_HEAD_DIM ** -0.5),
        "rg_ba": nrm(ks[13], (DEPTH, 2, D_RNN), 0.01),
        "rg_wx": nrm(ks[14], (DEPTH, 2, RNN_HEADS, RNN_HEAD_DIM, RNN_HEAD_DIM), RNN_HEAD_DIM ** -0.5),
        "rg_bx": nrm(ks[15], (DEPTH, 2, D_RNN), 0.01),
        "rg_lambda": jnp.log(a0) - jnp.log1p(-a0),
        "sc_conv_w": nrm(ks[16], (DEPTH, CONV_W, D_CONV), CONV_W ** -0.5),
        "w_out": nrm(ks[17], (DEPTH, D_MIX, D_MODEL), D_MIX ** -0.5),
        "router_group_w": nrm(ks[18], (DEPTH, D_MODEL, N_GROUPS), D_MODEL ** -0.5),
        "router_group_b": nrm(ks[19], (DEPTH, N_GROUPS), 0.01),
        "router_exp_w": nrm(ks[20], (DEPTH, D_MODEL, N_EXPERTS), D_MODEL ** -0.5),
        "router_exp_b": nrm(ks[21], (DEPTH, N_EXPERTS), 0.01),
        "exp_w_gate": nrm(ks[22], (DEPTH, N_EXPERTS, D_MODEL, D_EXPERT), D_MODEL ** -0.5),
        "exp_w_up": nrm(ks[23], (DEPTH, N_EXPERTS, D_MODEL, D_EXPERT), D_MODEL ** -0.5),
        "exp_w_down": nrm(ks[24], (DEPTH, N_EXPERTS, D_EXPERT, D_MODEL), D_EXPERT ** -0.5),
        "final_norm_g": 1.0 + nrm(ks[25], (D_MODEL,), 0.02),
    }


def reference(x, c, ctx, c_ctx, ada_w, ada_b, norm1_g, norm2_g, w_in, rnn_conv_w, rnn_conv_b,
              rg_wa, rg_ba, rg_wx, rg_bx, rg_lambda, sc_conv_w, w_out, router_group_w, router_group_b,
              router_exp_w, router_exp_b, exp_w_gate, exp_w_up, exp_w_down, final_norm_g):
    rows = x.shape[1] // GRID_W
    ctx_s = ctx
    for l in range(DEPTH):
        x, ctx_s = block(x, ctx_s, c, c_ctx, rows, l == DEPTH - 1,
                         ada_w[l], ada_b[l], norm1_g[l], norm2_g[l], w_in[l], rnn_conv_w[l], rnn_conv_b[l],
                         rg_wa[l], rg_ba[l], rg_wx[l], rg_bx[l], rg_lambda[l], sc_conv_w[l], w_out[l],
                         router_group_w[l], router_group_b[l], router_exp_w[l], router_exp_b[l],
                         exp_w_gate[l], exp_w_up[l], exp_w_down[l])
    return rmsnorm(x, final_norm_g)
```

```python
import functools

import jax
import jax.numpy as jnp
from jax import lax
from jax.experimental import pallas as pl
from jax.experimental.pallas import tpu as pltpu

F32 = jnp.float32
BF16 = jnp.bfloat16

D_MODEL = 1024
D_RNN = 512
D_CONV = 512
D_CONV_H = D_CONV // 2
RNN_HEADS = 8
RNN_HEAD_DIM = D_RNN // RNN_HEADS
GRID_W = 64
RG_C = 8.0
N_GROUPS = 4
EXPERTS_PER_GROUP = 8
N_EXPERTS = N_GROUPS * EXPERTS_PER_GROUP
TOP_K = 2
D_EXPERT = 512
NORM_EPS = 1e-6

LANES = 128
SUBLANES = 8
ROW_TILES = D_MODEL // LANES
N_LANE_GROUPS = D_RNN // LANES
EXPERT_LANE0 = N_GROUPS

MOD_ROWS = 16
MOD_TN = 768
TOK_TILE = 512
MOE_BLK = 256
DISPATCH_TILE = 512
COMBINE_TILE = 256
DMA_UNROLL = 16
VMEM_LIMIT = 48 * 1024 * 1024


def _dot(a, b):
    return jnp.dot(a, b, preferred_element_type=F32)


def _split_bf16(x):
    hi = x.astype(BF16)
    lo = (x - hi.astype(F32)).astype(BF16)
    return hi, lo


def _mod_kernel(cc_ref, w_ref, b_ref, o_ref):
    s = cc_ref[...]
    s = s * jax.nn.sigmoid(s)
    s_hi, s_lo = _split_bf16(s)
    w_hi, w_lo = _split_bf16(w_ref[...])
    o_ref[...] = _dot(s_hi, w_hi) + _dot(s_lo, w_hi) + _dot(s_hi, w_lo) + b_ref[...]


def _modulation(cc, ada_w, ada_b):
    n = ada_w.shape[1]
    return pl.pallas_call(
        _mod_kernel,
        grid=(n // MOD_TN,),
        in_specs=[
            pl.BlockSpec((MOD_ROWS, D_MODEL), lambda j: (0, 0)),
            pl.BlockSpec((D_MODEL, MOD_TN), lambda j: (0, j)),
            pl.BlockSpec((1, MOD_TN), lambda j: (0, j)),
        ],
        out_specs=pl.BlockSpec((MOD_ROWS, MOD_TN), lambda j: (0, j)),
        out_shape=jax.ShapeDtypeStruct((MOD_ROWS, n), F32),
        compiler_params=pltpu.CompilerParams(vmem_limit_bytes=VMEM_LIMIT),
        name="mod",
    )(cc, ada_w, ada_b.reshape(1, n))


def _norm_mod(x, g, scale, shift):
    ms = jnp.mean(x * x, axis=-1, keepdims=True)
    y = x * lax.rsqrt(ms + NORM_EPS) * g
    return y * (1.0 + scale) + shift


def _inproj_kernel(x_ref, mod_ref, g_ref, w_ref, *out_refs, latent):
    h = _norm_mod(x_ref[0], g_ref[...], mod_ref[0, 1:2, :], mod_ref[0, 0:1, :])
    hb = h.astype(BF16)
    xr = _dot(hb, w_ref[:, 0:D_RNN])
    out_refs[0][0] = xr
    if latent:
        o = D_RNN
        out_refs[1][0] = _dot(hb, w_ref[:, o:o + D_RNN])
        o += D_RNN
        v = _dot(hb, w_ref[:, o:o + D_CONV])
        out_refs[3][0] = _dot(hb, w_ref[:, o + D_CONV:o + 2 * D_CONV])
        cg = _dot(hb, w_ref[:, o + 2 * D_CONV:o + 3 * D_CONV])
        out_refs[2][0] = cg * v


def _inproj(x, mod3, mod_row, norm_g, w_bf16, latent):
    bn, n, d = x.shape
    tm = min(TOK_TILE, n)
    n_out = 4 if latent else 1
    width = w_bf16.shape[1]
    mod_map = (lambda b, i: (b, 0, 0)) if mod_row is None else (lambda b, i: (mod_row, 0, 0))
    return pl.pallas_call(
        functools.partial(_inproj_kernel, latent=latent),
        grid=(bn, n // tm),
        in_specs=[
            pl.BlockSpec((1, tm, d), lambda b, i: (b, i, 0)),
            pl.BlockSpec((1, 6, d), mod_map),
            pl.BlockSpec((1, d), lambda b, i: (0, 0)),
            pl.BlockSpec((d, width), lambda b, i: (0, 0)),
        ],
        out_specs=[pl.BlockSpec((1, tm, D_RNN), lambda b, i: (b, i, 0))] * n_out,
        out_shape=[jax.ShapeDtypeStruct((bn, n, D_RNN), F32)] * n_out,
        compiler_params=pltpu.CompilerParams(vmem_limit_bytes=VMEM_LIMIT),
        name="inproj_lat" if latent else "inproj_ctx",
    )(x, mod3, norm_g.reshape(1, d), w_bf16)


def _shift_rows(x, k):
    n = x.shape[0]
    row = lax.broadcasted_iota(jnp.int32, x.shape, 0)
    rolled = pltpu.roll(x, k % n, axis=0)
    valid = (row >= k) if k > 0 else (row < n + k)
    return jnp.where(valid, rolled, 0.0)


def _round8(v):
    return (v + SUBLANES - 1) // SUBLANES * SUBLANES


def _scan_work_rows(n):
    rows = 0
    while n > SUBLANES:
        g = n // SUBLANES
        rows += 2 * n + 2 * _round8(g) + _round8(g) + 2 * SUBLANES
        n = g
    return rows


def _linear_scan(a_ref, a_off, b_ref, b_off, h_ref, h_off, n, h0, reverse, work, w_off):
    if n <= SUBLANES:
        h = h0
        for r in (range(n - 1, -1, -1) if reverse else range(n)):
            h = a_ref[pl.ds(a_off + r, 1), :] * h + b_ref[pl.ds(b_off + r, 1), :]
            h_ref[pl.ds(h_off + r, 1), :] = h
        return
    g = n // SUBLANES
    gp = _round8(g)
    ca_off = w_off
    cb_off = ca_off + n
    a2_off = cb_off + n
    b2_off = a2_off + gp
    hs_off = b2_off + gp
    next_off = hs_off + gp + 2 * SUBLANES

    a = a_ref[pl.ds(a_off, n), :].reshape(g, SUBLANES, LANES)
    b = b_ref[pl.ds(b_off, n), :].reshape(g, SUBLANES, LANES)
    row = lax.broadcasted_iota(jnp.int32, (g, SUBLANES, LANES), 1)
    for s in (1, 2, 4):
        if reverse:
            a_s = pltpu.roll(a, SUBLANES - s, axis=1)
            b_s = pltpu.roll(b, SUBLANES - s, axis=1)
            valid = row < SUBLANES - s
        else:
            a_s = pltpu.roll(a, s, axis=1)
            b_s = pltpu.roll(b, s, axis=1)
            valid = row >= s
        b = jnp.where(valid, a * b_s + b, b)
        a = jnp.where(valid, a * a_s, a)
    work[pl.ds(ca_off, n), :] = a.reshape(n, LANES)
    work[pl.ds(cb_off, n), :] = b.reshape(n, LANES)

    last = 0 if reverse else SUBLANES - 1
    work[pl.ds(a2_off, g), :] = work[pl.ds(ca_off + last, g, stride=SUBLANES), :]
    work[pl.ds(b2_off, g), :] = work[pl.ds(cb_off + last, g, stride=SUBLANES), :]
    _linear_scan(work, a2_off, work, b2_off, work, hs_off + SUBLANES, g, h0, reverse, work, next_off)
    if reverse:
        work[pl.ds(hs_off + SUBLANES + g, 1), :] = h0
        in_off = hs_off + SUBLANES + 1
    else:
        work[pl.ds(hs_off + SUBLANES - 1, 1), :] = h0
        in_off = hs_off + SUBLANES - 1

    def apply_group(gi, r0):
        h_in = work[pl.ds(in_off + gi, 1), :]
        h_ref[pl.ds(h_off + r0, SUBLANES), :] = (
            work[pl.ds(ca_off + r0, SUBLANES), :] * h_in + work[pl.ds(cb_off + r0, SUBLANES), :])

    if g <= 32:
        for gi in range(g):
            apply_group(gi, gi * SUBLANES)
    else:
        def body(go, carry):
            for u in range(SUBLANES):
                gi = go * SUBLANES + u
                apply_group(gi, pl.multiple_of(gi * SUBLANES, SUBLANES))
            return carry
        lax.fori_loop(0, g // SUBLANES, body, 0)


def _rnn_kernel(xr_ref, xrc_ref, gr_ref, cw_ref, cb_ref, wg_ref, bg_ref, lam_ref, y_ref,
                a_s, b_s, h_s, hsum_s, work, *, n_lat, n_ctx):
    nl = -lam_ref[...]
    sp = jnp.maximum(nl, 0.0) + jnp.log1p(jnp.exp(-jnp.abs(nl)))
    cw = cw_ref[...]
    bias = cb_ref[...]
    wg = wg_ref[0]
    bg = bg_ref[0]

    def conv_gates(x):
        xc = (cw[0:1] * _shift_rows(x, 2) + cw[1:2] * _shift_rows(x, 1) + cw[2:3] * x
              + cw[3:4] * _shift_rows(x, -1)) + bias
        return xc, _dot(xc.astype(BF16), wg) + bg

    def coeffs(xc, gates, d):
        r = jax.nn.sigmoid(gates[:, (2 * d) * LANES:(2 * d + 1) * LANES])
        i = jax.nn.sigmoid(gates[:, (2 * d + 1) * LANES:(2 * d + 2) * LANES])
        log_a = (-RG_C * r) * sp[d:d + 1]
        a = jnp.exp(log_a)
        b = jnp.sqrt(-jnp.tanh(log_a) * (a * a + 1.0)) * (i * xc)
        return a, b

    zero = jnp.zeros((1, LANES), F32)
    xc_c, gates_c = conv_gates(xrc_ref[0])
    h0 = []
    for d in range(2):
        a, b = coeffs(xc_c, gates_c, d)
        a_s[pl.ds(0, n_ctx), :] = a
        b_s[pl.ds(0, n_ctx), :] = b
        _linear_scan(a_s, 0, b_s, 0, h_s, 0, n_ctx, zero, d == 1, work, 0)
        h0.append(h_s[pl.ds(0 if d == 1 else n_ctx - 1, 1), :])

    xc_l, gates_l = conv_gates(xr_ref[0])
    for d in range(2):
        a, b = coeffs(xc_l, gates_l, d)
        a_s[...] = a
        b_s[...] = b
        _linear_scan(a_s, 0, b_s, 0, h_s if d == 1 else hsum_s, 0, n_lat, h0[d], d == 1, work, 0)
    y_ref[0] = jax.nn.gelu(gr_ref[0], approximate=True) * (hsum_s[...] + h_s[...])


def _rnn(xr, xr_c, gr, conv_w, conv_b, wg, bgate, lam):
    bn, n, _ = xr.shape
    n_ctx = xr_c.shape[1]
    seq_spec = pl.BlockSpec((1, n, LANES), lambda b, p: (b, 0, p))
    return pl.pallas_call(
        functools.partial(_rnn_kernel, n_lat=n, n_ctx=n_ctx),
        grid=(bn, N_LANE_GROUPS),
        in_specs=[
            seq_spec,
            pl.BlockSpec((1, n_ctx, LANES), lambda b, p: (b, 0, p)),
            seq_spec,
            pl.BlockSpec((4, LANES), lambda b, p: (0, p)),
            pl.BlockSpec((1, LANES), lambda b, p: (0, p)),
            pl.BlockSpec((1, LANES, 4 * LANES), lambda b, p: (p, 0, 0)),
            pl.BlockSpec((1, 1, 4 * LANES), lambda b, p: (p, 0, 0)),
            pl.BlockSpec((2, LANES), lambda b, p: (0, p)),
        ],
        out_specs=seq_spec,
        out_shape=jax.ShapeDtypeStruct((bn, n, D_RNN), F32),
        scratch_shapes=[pltpu.VMEM((n, LANES), F32)] * 4
        + [pltpu.VMEM((_scan_work_rows(n), LANES), F32)],
        compiler_params=pltpu.CompilerParams(vmem_limit_bytes=VMEM_LIMIT),
        name="rnn",
    )(xr, xr_c, gr, conv_w, conv_b.reshape(1, D_RNN), wg, bgate, lam)


def _gate_weights(rg_wa, rg_ba, rg_wx, rg_bx):
    eye = jnp.eye(2, dtype=F32)
    blocks, biases = [], []
    for d in range(2):
        for w, bvec in ((rg_wa[d], rg_ba[d]), (rg_wx[d], rg_bx[d])):
            w4 = w.reshape(N_LANE_GROUPS, 2, RNN_HEAD_DIM, RNN_HEAD_DIM)
            bd = jnp.einsum("paij,ac->paicj", w4, eye).reshape(N_LANE_GROUPS, LANES, LANES)
            blocks.append(bd)
            biases.append(bvec.reshape(N_LANE_GROUPS, 1, LANES))
    return jnp.concatenate(blocks, axis=-1).astype(BF16), jnp.concatenate(biases, axis=-1)


def _gconv_kernel(u_ref, bg_ref, w_ref, y_ref, *, n):
    p = pl.program_id(1)
    u = u_ref[0]
    w = w_ref[...]

    @pl.when(p < D_CONV_H // LANES)
    def _():
        col = lax.broadcasted_iota(jnp.int32, u.shape, 0) % GRID_W
        left = jnp.where(col > 0, _shift_rows(u, 1), 0.0)
        right = jnp.where(col < GRID_W - 1, _shift_rows(u, -1), 0.0)
        y_ref[0] = bg_ref[0] * (w[0:1] * left + w[1:2] * u + w[2:3] * right)

    @pl.when(p >= D_CONV_H // LANES)
    def _():
        y_ref[0] = bg_ref[0] * (w[0:1] * _shift_rows(u, GRID_W) + w[1:2] * u
                                + w[2:3] * _shift_rows(u, -GRID_W))


def _gconv(u, bg, w):
    bn, n, _ = u.shape
    seq_spec = pl.BlockSpec((1, n, LANES), lambda b, p: (b, 0, p))
    return pl.pallas_call(
        functools.partial(_gconv_kernel, n=n),
        grid=(bn, D_CONV // LANES),
        in_specs=[seq_spec, seq_spec, pl.BlockSpec((3, LANES), lambda b, p: (0, p))],
        out_specs=seq_spec,
        out_shape=jax.ShapeDtypeStruct((bn, n, D_CONV), F32),
        compiler_params=pltpu.CompilerParams(vmem_limit_bytes=VMEM_LIMIT),
        name="gconv",
    )(u, bg, w)


def _lane_max(x, mask):
    return jnp.max(jnp.where(mask, x, -jnp.inf), axis=-1, keepdims=True)


def _first_lane(cond, lane):
    return jnp.min(jnp.where(cond, lane, float(LANES)), axis=-1, keepdims=True)


def _outproj_kernel(x_ref, yr_ref, yc_ref, w_ref, mod_ref, g_ref, wr_hi_ref, wr_lo_ref, br_ref,
                    x1_ref, mt_ref, route_ref, cnt_ref, carry, *, tm):
    first = (pl.program_id(0) == 0) & (pl.program_id(1) == 0)

    @pl.when(first)
    def _():
        carry[...] = jnp.zeros_like(carry)

    mix = _dot(yr_ref[0].astype(BF16), w_ref[0:D_RNN, :]) + _dot(yc_ref[0].astype(BF16), w_ref[D_RNN:, :])
    x1 = x_ref[0] + mod_ref[0, 2:3, :] * mix
    x1_ref[0] = x1
    m = _norm_mod(x1, g_ref[...], mod_ref[0, 4:5, :], mod_ref[0, 3:4, :])
    for s in range(ROW_TILES):
        mt_ref[pl.ds(s, tm, stride=ROW_TILES), :] = m[:, s * LANES:(s + 1) * LANES]

    m_hi, m_lo = _split_bf16(m)
    logits = (_dot(m_hi, wr_hi_ref[...]) + _dot(m_lo, wr_hi_ref[...]) + _dot(m_hi, wr_lo_ref[...])
              + br_ref[...])
    lane_i = lax.broadcasted_iota(jnp.int32, logits.shape, 1)
    lane = lane_i.astype(F32)
    is_grp = lane_i < N_GROUPS
    g_max = _lane_max(logits, is_grp)
    grp = _first_lane(is_grp & (logits == g_max), lane)
    p_g = 1.0 / jnp.sum(jnp.where(is_grp, jnp.exp(logits - g_max), 0.0), axis=-1, keepdims=True)
    lo_lane = EXPERT_LANE0 + grp * EXPERTS_PER_GROUP
    in_grp = (lane >= lo_lane) & (lane < lo_lane + EXPERTS_PER_GROUP)
    l1 = _lane_max(logits, in_grp)
    i1 = _first_lane(in_grp & (logits == l1), lane)
    rest = in_grp & (lane != i1)
    l2 = _lane_max(logits, rest)
    i2 = _first_lane(rest & (logits == l2), lane)
    r21 = jnp.exp(l2 - l1)
    gate1 = p_g / (1.0 + r21)
    gate2 = gate1 * r21

    oh1 = jnp.where(lane == i1, 1.0, 0.0)
    oh2 = jnp.where(lane == i2, 1.0, 0.0)
    both = (oh1 + oh2).astype(BF16)
    ti = lax.broadcasted_iota(jnp.int32, (tm, tm), 0)
    tj = lax.broadcasted_iota(jnp.int32, (tm, tm), 1)
    tri = jnp.where(tj < ti, 1.0, 0.0).astype(BF16)
    before = _dot(tri, both) + carry[...]
    rank1 = jnp.sum(oh1 * before, axis=-1, keepdims=True)
    rank2 = jnp.sum(oh2 * before, axis=-1, keepdims=True)
    total = carry[...] + jnp.sum(oh1 + oh2, axis=0, keepdims=True)
    carry[...] = total
    cnt_ref[...] = total

    e1 = i1 - EXPERT_LANE0
    e2 = i2 - EXPERT_LANE0
    out = jnp.zeros(logits.shape, F32)
    for k, val in enumerate((e1, e2, gate1, gate2, rank1, rank2)):
        out = jnp.where(lane_i == k, val, out)
    route_ref[...] = out


def _outproj(x, y_rnn, y_conv, w_out_bf16, mod3, norm_g, wr_hi, wr_lo, br):
    bn, n, d = x.shape
    tm = min(TOK_TILE, n)
    nt = n // tm
    t_all = bn * n
    return pl.pallas_call(
        functools.partial(_outproj_kernel, tm=tm),
        grid=(bn, nt),
        in_specs=[
            pl.BlockSpec((1, tm, d), lambda b, i: (b, i, 0)),
            pl.BlockSpec((1, tm, D_RNN), lambda b, i: (b, i, 0)),
            pl.BlockSpec((1, tm, D_CONV), lambda b, i: (b, i, 0)),
            pl.BlockSpec((D_RNN + D_CONV, d), lambda b, i: (0, 0)),
            pl.BlockSpec((1, 6, d), lambda b, i: (b, 0, 0)),
            pl.BlockSpec((1, d), lambda b, i: (0, 0)),
            pl.BlockSpec((d, LANES), lambda b, i: (0, 0)),
            pl.BlockSpec((d, LANES), lambda b, i: (0, 0)),
            pl.BlockSpec((1, LANES), lambda b, i: (0, 0)),
        ],
        out_specs=[
            pl.BlockSpec((1, tm, d), lambda b, i: (b, i, 0)),
            pl.BlockSpec((tm * ROW_TILES, LANES), lambda b, i: (b * nt + i, 0)),
            pl.BlockSpec((tm, LANES), lambda b, i: (b * nt + i, 0)),
            pl.BlockSpec((1, LANES), lambda b, i: (0, 0)),
        ],
        out_shape=[
            jax.ShapeDtypeStruct((bn, n, d), F32),
            jax.ShapeDtypeStruct((t_all * ROW_TILES, LANES), F32),
            jax.ShapeDtypeStruct((t_all, LANES), F32),
            jax.ShapeDtypeStruct((1, LANES), F32),
        ],
        scratch_shapes=[pltpu.VMEM((1, LANES), F32)],
        compiler_params=pltpu.CompilerParams(
            dimension_semantics=("arbitrary", "arbitrary"), vmem_limit_bytes=VMEM_LIMIT),
        name="outproj",
    )(x, y_rnn, y_conv, w_out_bf16, mod3, norm_g.reshape(1, d), wr_hi, wr_lo, br)


def _dispatch_kernel(dest_ref, pend_ref, cnt_ref, mt_hbm, xb_hbm, zbuf, zsem, sem, *, td, n_blocks):
    i = pl.program_id(0)

    def zero_block(start):
        return pltpu.make_async_copy(zbuf, xb_hbm.at[pl.ds(pl.multiple_of(start, MOE_BLK), MOE_BLK)], zsem)

    @pl.when(i == 0)
    def _():
        zbuf[...] = jnp.zeros_like(zbuf)
        n_used = pend_ref[N_EXPERTS - 1] // MOE_BLK
        for e in range(N_EXPERTS):
            @pl.when(cnt_ref[e] > 0)
            def _():
                zero_block(pend_ref[e] - MOE_BLK).start()
        lax.fori_loop(n_used, n_blocks, lambda j, c: (zero_block(j * MOE_BLK).start(), c)[1], 0)
        for e in range(N_EXPERTS):
            @pl.when(cnt_ref[e] > 0)
            def _():
                zero_block(0).wait()
        lax.fori_loop(n_used, n_blocks, lambda j, c: (zero_block(0).wait(), c)[1], 0)

    base = i * td

    def row_copy(t, d):
        return pltpu.make_async_copy(mt_hbm.at[t], xb_hbm.at[d], sem)

    def issue(c, carry):
        for u in range(DMA_UNROLL):
            t = base + c * DMA_UNROLL + u
            for k in range(TOP_K):
                row_copy(t, dest_ref[TOP_K * t + k]).start()
        return carry

    def drain(c, carry):
        for u in range(DMA_UNROLL * TOP_K):
            row_copy(0, 0).wait()
        return carry

    lax.fori_loop(0, td // DMA_UNROLL, issue, 0)
    lax.fori_loop(0, td // DMA_UNROLL, drain, 0)


def _dispatch(dest, pend, cnt, mt3, n_slots):
    t_all = mt3.shape[0]
    td = min(DISPATCH_TILE, t_all)
    return pl.pallas_call(
        functools.partial(_dispatch_kernel, td=td, n_blocks=n_slots // MOE_BLK),
        grid_spec=pltpu.PrefetchScalarGridSpec(
            num_scalar_prefetch=3,
            grid=(t_all // td,),
            in_specs=[pl.BlockSpec(memory_space=pl.ANY)],
            out_specs=pl.BlockSpec(memory_space=pl.ANY),
            scratch_shapes=[
                pltpu.VMEM((MOE_BLK, ROW_TILES, LANES), F32),
                pltpu.SemaphoreType.DMA,
                pltpu.SemaphoreType.DMA,
            ],
        ),
        out_shape=jax.ShapeDtypeStruct((n_slots, ROW_TILES, LANES), F32),
        compiler_params=pltpu.CompilerParams(
            dimension_semantics=("arbitrary",), vmem_limit_bytes=VMEM_LIMIT),
        name="dispatch",
    )(dest, pend, cnt, mt3)


def _expert_kernel(be_ref, nu_ref, xb_ref, wg_ref, wu_ref, wd_ref, yb_ref, wg_s, wu_s, wd_s):
    j = pl.program_id(0)

    @pl.when(j >= nu_ref[0])
    def _():
        yb_ref[...] = jnp.zeros_like(yb_ref)

    @pl.when(j < nu_ref[0])
    def _():
        e = be_ref[j]
        prev = be_ref[jnp.maximum(j - 1, 0)]

        @pl.when((j == 0) | (e != prev))
        def _():
            wg_s[...] = wg_ref[0].astype(BF16)
            wu_s[...] = wu_ref[0].astype(BF16)
            wd_s[...] = wd_ref[0].astype(BF16)

        x = jnp.concatenate(
            [xb_ref[pl.ds(s, MOE_BLK, stride=ROW_TILES), :] for s in range(ROW_TILES)], axis=-1)
        xb16 = x.astype(BF16)
        gate = _dot(xb16, wg_s[...])
        up = _dot(xb16, wu_s[...])
        h = (gate * jax.nn.sigmoid(gate)) * up
        y = _dot(h.astype(BF16), wd_s[...])
        for s in range(ROW_TILES):
            yb_ref[pl.ds(s, MOE_BLK, stride=ROW_TILES), :] = y[:, s * LANES:(s + 1) * LANES]


def _experts(block_e, n_used, xb2, w_gate, w_up, w_down, n_blocks):
    def blk(j, be, nu):
        return (jnp.minimum(j, nu[0] - 1), 0)

    def wsel(j, be, nu):
        return (be[jnp.minimum(j, nu[0] - 1)], 0, 0)

    return pl.pallas_call(
        _expert_kernel,
        grid_spec=pltpu.PrefetchScalarGridSpec(
            num_scalar_prefetch=2,
            grid=(n_blocks,),
            in_specs=[
                pl.BlockSpec((MOE_BLK * ROW_TILES, LANES), blk),
                pl.BlockSpec((1, D_MODEL, D_EXPERT), wsel),
                pl.BlockSpec((1, D_MODEL, D_EXPERT), wsel),
                pl.BlockSpec((1, D_EXPERT, D_MODEL), wsel),
            ],
            out_specs=pl.BlockSpec((MOE_BLK * ROW_TILES, LANES), lambda j, be, nu: (j, 0)),
            scratch_shapes=[
                pltpu.VMEM((D_MODEL, D_EXPERT), BF16),
                pltpu.VMEM((D_MODEL, D_EXPERT), BF16),
                pltpu.VMEM((D_EXPERT, D_MODEL), BF16),
            ],
        ),
        out_shape=jax.ShapeDtypeStruct(xb2.shape, F32),
        compiler_params=pltpu.CompilerParams(
            dimension_semantics=("arbitrary",), vmem_limit_bytes=VMEM_LIMIT),
        name="expert",
    )(block_e, n_used, xb2, w_gate, w_up, w_down)


def _combine_kernel(dest_ref, yb_hbm, x1_ref, route_ref, mod_ref, g_ref, o_ref, ybuf, sem, *, tc):
    i = pl.program_id(0)
    base = i * tc

    def row_copy(d, k, r):
        return pltpu.make_async_copy(
            yb_hbm.at[d], ybuf.at[k, pl.ds(pl.multiple_of(r * ROW_TILES, ROW_TILES), ROW_TILES)], sem)

    def issue(c, carry):
        for u in range(DMA_UNROLL):
            r = c * DMA_UNROLL + u
            for k in range(TOP_K):
                row_copy(dest_ref[TOP_K * (base + r) + k], k, r).start()
        return carry

    def drain(c, carry):
        for u in range(DMA_UNROLL * TOP_K):
            row_copy(0, 0, 0).wait()
        return carry

    lax.fori_loop(0, tc // DMA_UNROLL, issue, 0)
    lax.fori_loop(0, tc // DMA_UNROLL, drain, 0)

    def rows(k):
        return jnp.concatenate(
            [ybuf[k, pl.ds(s, tc, stride=ROW_TILES), :] for s in range(ROW_TILES)], axis=-1)

    route = route_ref[...]
    y = route[:, 2:3] * rows(0) + route[:, 3:4] * rows(1)
    x2 = x1_ref[...] + mod_ref[0, 5:6, :] * y
    ms = jnp.mean(x2 * x2, axis=-1, keepdims=True)
    o_ref[...] = x2 * lax.rsqrt(ms + NORM_EPS) * g_ref[...]


def _combine(dest, yb3, x1_2d, route, mod3, final_g, seq):
    t_all, d = x1_2d.shape
    tc = min(COMBINE_TILE, seq)
    per_seq = seq // tc
    return pl.pallas_call(
        functools.partial(_combine_kernel, tc=tc),
        grid_spec=pltpu.PrefetchScalarGridSpec(
            num_scalar_prefetch=1,
            grid=(t_all // tc,),
            in_specs=[
                pl.BlockSpec(memory_space=pl.ANY),
                pl.BlockSpec((tc, d), lambda i, dest: (i, 0)),
                pl.BlockSpec((tc, LANES), lambda i, dest: (i, 0)),
                pl.BlockSpec((1, 6, d), lambda i, dest: (i // per_seq, 0, 0)),
                pl.BlockSpec((1, d), lambda i, dest: (0, 0)),
            ],
            out_specs=pl.BlockSpec((tc, d), lambda i, dest: (i, 0)),
            scratch_shapes=[
                pltpu.VMEM((TOP_K, tc * ROW_TILES, LANES), F32),
                pltpu.SemaphoreType.DMA,
            ],
        ),
        out_shape=jax.ShapeDtypeStruct((t_all, d), F32),
        compiler_params=pltpu.CompilerParams(
            dimension_semantics=("arbitrary",), vmem_limit_bytes=VMEM_LIMIT),
        name="combine",
    )(dest, yb3, x1_2d, route, mod3, final_g.reshape(1, d))


def kernel(x, c, ctx, c_ctx, ada_w, ada_b, norm1_g, norm2_g, w_in, rnn_conv_w, rnn_conv_b, rg_wa, rg_ba,
           rg_wx, rg_bx, rg_lambda, sc_conv_w, w_out, router_group_w, router_group_b, router_exp_w,
           router_exp_b, exp_w_gate, exp_w_up, exp_w_down, final_norm_g):
    bn, seq, d = x.shape
    assert d == D_MODEL and bn < MOD_ROWS and ada_w.shape[0] == 1
    t_all = bn * seq

    cc = jnp.concatenate([c, c_ctx[None], jnp.zeros((MOD_ROWS - bn - 1, d), F32)], axis=0)
    mod3 = _modulation(cc, ada_w[0], ada_b[0]).reshape(MOD_ROWS, 6, d)

    w_in_b = w_in[0].astype(BF16)
    xr, gr, u, bg = _inproj(x, mod3, None, norm1_g[0], w_in_b, latent=True)
    (xr_c,) = _inproj(ctx, mod3, bn, norm1_g[0], w_in_b[:, :D_RNN], latent=False)

    wg, bgate = _gate_weights(rg_wa[0], rg_ba[0], rg_wx[0], rg_bx[0])
    y_rnn = _rnn(xr, xr_c, gr, rnn_conv_w[0], rnn_conv_b[0], wg, bgate, rg_lambda[0])
    y_conv = _gconv(u, bg, sc_conv_w[0])

    wr = jnp.zeros((d, LANES), F32)
    wr = wr.at[:, :N_GROUPS].set(router_group_w[0]).at[:, EXPERT_LANE0:EXPERT_LANE0 + N_EXPERTS].set(router_exp_w[0])
    br = jnp.zeros((1, LANES), F32)
    br = br.at[0, :N_GROUPS].set(router_group_b[0]).at[0, EXPERT_LANE0:EXPERT_LANE0 + N_EXPERTS].set(router_exp_b[0])
    wr_hi, wr_lo = _split_bf16(wr)
    x1, mt, route, cnt = _outproj(x, y_rnn, y_conv, w_out[0].astype(BF16), mod3, norm2_g[0], wr_hi, wr_lo, br)

    n_assign = t_all * TOP_K
    n_blocks = (n_assign + N_EXPERTS * (MOE_BLK - 1) + MOE_BLK - 1) // MOE_BLK
    counts = cnt[0, EXPERT_LANE0:EXPERT_LANE0 + N_EXPERTS].astype(jnp.int32)
    pcounts = (counts + MOE_BLK - 1) // MOE_BLK * MOE_BLK
    pends = jnp.cumsum(pcounts)
    pstarts = pends - pcounts
    experts = route[:, 0:TOP_K].astype(jnp.int32)
    ranks = route[:, 4:4 + TOP_K].astype(jnp.int32)
    onehot = experts[:, :, None] == jnp.arange(N_EXPERTS, dtype=jnp.int32)
    dest = (ranks + jnp.sum(jnp.where(onehot, pstarts, 0), axis=-1)).reshape(n_assign)
    n_used = (pends[-1] // MOE_BLK).astype(jnp.int32)
    blk_start = jnp.arange(n_blocks, dtype=jnp.int32) * MOE_BLK
    block_e = jnp.minimum(jnp.sum(blk_start[:, None] >= pends[None, :], axis=1), N_EXPERTS - 1)
    last_e = jnp.max(jnp.where(counts > 0, jnp.arange(N_EXPERTS, dtype=jnp.int32), 0))
    block_e = jnp.where(blk_start < pends[-1], block_e, last_e).astype(jnp.int32)

    n_slots = n_blocks * MOE_BLK
    xb3 = _dispatch(dest, pends.astype(jnp.int32), counts, mt.reshape(t_all, ROW_TILES, LANES), n_slots)
    yb2 = _experts(block_e, n_used.reshape(1), xb3.reshape(n_slots * ROW_TILES, LANES),
                   exp_w_gate[0], exp_w_up[0], exp_w_down[0], n_blocks)
    out = _combine(dest, yb2.reshape(n_slots, ROW_TILES, LANES), x1.reshape(t_all, d), route, mod3,
                   final_norm_g, seq)
    return out.reshape(bn, seq, d)
```

```python
import functools

import jax
import jax.numpy as jnp
from jax import lax
from jax.experimental import pallas as pl
from jax.experimental.pallas import tpu as pltpu

F32 = jnp.float32
BF16 = jnp.bfloat16

D_MODEL = 1024
D_RNN = 512
D_CONV = 512
D_CONV_H = D_CONV // 2
RNN_HEADS = 8
RNN_HEAD_DIM = D_RNN // RNN_HEADS
GRID_W = 64
RG_C = 8.0
N_GROUPS = 4
EXPERTS_PER_GROUP = 8
N_EXPERTS = N_GROUPS * EXPERTS_PER_GROUP
TOP_K = 2
D_EXPERT = 512
NORM_EPS = 1e-6

LANES = 128
SUBLANES = 8
ROW_TILES = D_MODEL // LANES
N_LANE_GROUPS = D_RNN // LANES
EXPERT_LANE0 = N_GROUPS

MOD_ROWS = 16
MOD_TN = 768
TOK_TILE = 512
MOE_BLK = 256
COMBINE_TILE = 256
DMA_UNROLL = 16
VMEM_LIMIT = 48 * 1024 * 1024


def _dot(a, b):
    return jnp.dot(a, b, preferred_element_type=F32)


def _split_bf16(x):
    hi = x.astype(BF16)
    lo = (x - hi.astype(F32)).astype(BF16)
    return hi, lo


def _mod_kernel(cc_ref, w_ref, b_ref, o_ref):
    s = cc_ref[...]
    s = s * jax.nn.sigmoid(s)
    s_hi, s_lo = _split_bf16(s)
    w_hi, w_lo = _split_bf16(w_ref[...])
    o_ref[...] = _dot(s_hi, w_hi) + _dot(s_lo, w_hi) + _dot(s_hi, w_lo) + b_ref[...]


def _modulation(cc, ada_w, ada_b):
    n = ada_w.shape[1]
    return pl.pallas_call(
        _mod_kernel,
        grid=(n // MOD_TN,),
        in_specs=[
            pl.BlockSpec((MOD_ROWS, D_MODEL), lambda j: (0, 0)),
            pl.BlockSpec((D_MODEL, MOD_TN), lambda j: (0, j)),
            pl.BlockSpec((1, MOD_TN), lambda j: (0, j)),
        ],
        out_specs=pl.BlockSpec((MOD_ROWS, MOD_TN), lambda j: (0, j)),
        out_shape=jax.ShapeDtypeStruct((MOD_ROWS, n), F32),
        compiler_params=pltpu.CompilerParams(vmem_limit_bytes=VMEM_LIMIT),
        name="mod",
    )(cc, ada_w, ada_b.reshape(1, n))


def _norm_mod(x, g, scale, shift):
    ms = jnp.mean(x * x, axis=-1, keepdims=True)
    y = x * lax.rsqrt(ms + NORM_EPS) * g
    return y * (1.0 + scale) + shift


def _inproj_kernel(x_ref, mod_ref, g_ref, w_ref, *out_refs, latent):
    h = _norm_mod(x_ref[0], g_ref[...], mod_ref[0, 1:2, :], mod_ref[0, 0:1, :])
    hb = h.astype(BF16)
    xr = _dot(hb, w_ref[:, 0:D_RNN])
    out_refs[0][0] = xr
    if latent:
        o = D_RNN
        out_refs[1][0] = _dot(hb, w_ref[:, o:o + D_RNN])
        o += D_RNN
        v = _dot(hb, w_ref[:, o:o + D_CONV])
        out_refs[3][0] = _dot(hb, w_ref[:, o + D_CONV:o + 2 * D_CONV])
        cg = _dot(hb, w_ref[:, o + 2 * D_CONV:o + 3 * D_CONV])
        out_refs[2][0] = cg * v


def _inproj(x, mod3, mod_row, norm_g, w_bf16, latent):
    bn, n, d = x.shape
    tm = min(TOK_TILE, n)
    n_out = 4 if latent else 1
    width = w_bf16.shape[1]
    mod_map = (lambda b, i: (b, 0, 0)) if mod_row is None else (lambda b, i: (mod_row, 0, 0))
    return pl.pallas_call(
        functools.partial(_inproj_kernel, latent=latent),
        grid=(bn, n // tm),
        in_specs=[
            pl.BlockSpec((1, tm, d), lambda b, i: (b, i, 0)),
            pl.BlockSpec((1, 6, d), mod_map),
            pl.BlockSpec((1, d), lambda b, i: (0, 0)),
            pl.BlockSpec((d, width), lambda b, i: (0, 0)),
        ],
        out_specs=[pl.BlockSpec((1, tm, D_RNN), lambda b, i: (b, i, 0))] * n_out,
        out_shape=[jax.ShapeDtypeStruct((bn, n, D_RNN), F32)] * n_out,
        compiler_params=pltpu.CompilerParams(vmem_limit_bytes=VMEM_LIMIT),
        name="inproj_lat" if latent else "inproj_ctx",
    )(x, mod3, norm_g.reshape(1, d), w_bf16)


def _shift_rows(x, k):
    n = x.shape[0]
    row = lax.broadcasted_iota(jnp.int32, x.shape, 0)
    rolled = pltpu.roll(x, k % n, axis=0)
    valid = (row >= k) if k > 0 else (row < n + k)
    return jnp.where(valid, rolled, 0.0)


def _round8(v):
    return (v + SUBLANES - 1) // SUBLANES * SUBLANES


def _scan_work_rows(n):
    rows = 0
    while n > SUBLANES:
        g = n // SUBLANES
        rows += 2 * n + 2 * _round8(g) + _round8(g) + 2 * SUBLANES
        n = g
    return rows


def _linear_scan(a_ref, a_off, b_ref, b_off, h_ref, h_off, n, h0, reverse, work, w_off):
    if n <= SUBLANES:
        h = h0
        for r in (range(n - 1, -1, -1) if reverse else range(n)):
            h = a_ref[pl.ds(a_off + r, 1), :] * h + b_ref[pl.ds(b_off + r, 1), :]
            h_ref[pl.ds(h_off + r, 1), :] = h
        return
    g = n // SUBLANES
    gp = _round8(g)
    ca_off = w_off
    cb_off = ca_off + n
    a2_off = cb_off + n
    b2_off = a2_off + gp
    hs_off = b2_off + gp
    next_off = hs_off + gp + 2 * SUBLANES

    a = a_ref[pl.ds(a_off, n), :].reshape(g, SUBLANES, LANES)
    b = b_ref[pl.ds(b_off, n), :].reshape(g, SUBLANES, LANES)
    row = lax.broadcasted_iota(jnp.int32, (g, SUBLANES, LANES), 1)
    for s in (1, 2, 4):
        if reverse:
            a_s = pltpu.roll(a, SUBLANES - s, axis=1)
            b_s = pltpu.roll(b, SUBLANES - s, axis=1)
            valid = row < SUBLANES - s
        else:
            a_s = pltpu.roll(a, s, axis=1)
            b_s = pltpu.roll(b, s, axis=1)
            valid = row >= s
        b = jnp.where(valid, a * b_s + b, b)
        a = jnp.where(valid, a * a_s, a)
    work[pl.ds(ca_off, n), :] = a.reshape(n, LANES)
    work[pl.ds(cb_off, n), :] = b.reshape(n, LANES)

    last = 0 if reverse else SUBLANES - 1
    work[pl.ds(a2_off, g), :] = work[pl.ds(ca_off + last, g, stride=SUBLANES), :]
    work[pl.ds(b2_off, g), :] = work[pl.ds(cb_off + last, g, stride=SUBLANES), :]
    _linear_scan(work, a2_off, work, b2_off, work, hs_off + SUBLANES, g, h0, reverse, work, next_off)
    if reverse:
        work[pl.ds(hs_off + SUBLANES + g, 1), :] = h0
        in_off = hs_off + SUBLANES + 1
    else:
        work[pl.ds(hs_off + SUBLANES - 1, 1), :] = h0
        in_off = hs_off + SUBLANES - 1

    def apply_group(gi, r0):
        h_in = work[pl.ds(in_off + gi, 1), :]
        h_ref[pl.ds(h_off + r0, SUBLANES), :] = (
            work[pl.ds(ca_off + r0, SUBLANES), :] * h_in + work[pl.ds(cb_off + r0, SUBLANES), :])

    if g <= 32:
        for gi in range(g):
            apply_group(gi, gi * SUBLANES)
    else:
        def body(go, carry):
            for u in range(SUBLANES):
                gi = go * SUBLANES + u
                apply_group(gi, pl.multiple_of(gi * SUBLANES, SUBLANES))
            return carry
        lax.fori_loop(0, g // SUBLANES, body, 0)


def _rnn_kernel(xr_ref, xrc_ref, gr_ref, cw_ref, cb_ref, wg_ref, bg_ref, lam_ref, y_ref,
                a_s, b_s, h_s, hsum_s, work, *, n_lat, n_ctx):
    nl = -lam_ref[...]
    sp = jnp.maximum(nl, 0.0) + jnp.log1p(jnp.exp(-jnp.abs(nl)))
    cw = cw_ref[...]
    bias = cb_ref[...]
    wg = wg_ref[0]
    bg = bg_ref[0]

    def conv_gates(x):
        xc = (cw[0:1] * _shift_rows(x, 2) + cw[1:2] * _shift_rows(x, 1) + cw[2:3] * x
              + cw[3:4] * _shift_rows(x, -1)) + bias
        return xc, _dot(xc.astype(BF16), wg) + bg

    def coeffs(xc, gates, d):
        r = jax.nn.sigmoid(gates[:, (2 * d) * LANES:(2 * d + 1) * LANES])
        i = jax.nn.sigmoid(gates[:, (2 * d + 1) * LANES:(2 * d + 2) * LANES])
        log_a = (-RG_C * r) * sp[d:d + 1]
        a = jnp.exp(log_a)
        b = jnp.sqrt(-jnp.tanh(log_a) * (a * a + 1.0)) * (i * xc)
        return a, b

    zero = jnp.zeros((1, LANES), F32)
    xc_c, gates_c = conv_gates(xrc_ref[0])
    h0 = []
    for d in range(2):
        a, b = coeffs(xc_c, gates_c, d)
        a_s[pl.ds(0, n_ctx), :] = a
        b_s[pl.ds(0, n_ctx), :] = b
        _linear_scan(a_s, 0, b_s, 0, h_s, 0, n_ctx, zero, d == 1, work, 0)
        h0.append(h_s[pl.ds(0 if d == 1 else n_ctx - 1, 1), :])

    xc_l, gates_l = conv_gates(xr_ref[0])
    for d in range(2):
        a, b = coeffs(xc_l, gates_l, d)
        a_s[...] = a
        b_s[...] = b
        _linear_scan(a_s, 0, b_s, 0, h_s if d == 1 else hsum_s, 0, n_lat, h0[d], d == 1, work, 0)
    y_ref[0] = jax.nn.gelu(gr_ref[0], approximate=True) * (hsum_s[...] + h_s[...])


def _rnn(xr, xr_c, gr, conv_w, conv_b, wg, bgate, lam):
    bn, n, _ = xr.shape
    n_ctx = xr_c.shape[1]
    seq_spec = pl.BlockSpec((1, n, LANES), lambda b, p: (b, 0, p))
    return pl.pallas_call(
        functools.partial(_rnn_kernel, n_lat=n, n_ctx=n_ctx),
        grid=(bn, N_LANE_GROUPS),
        in_specs=[
            seq_spec,
            pl.BlockSpec((1, n_ctx, LANES), lambda b, p: (b, 0, p)),
            seq_spec,
            pl.BlockSpec((4, LANES), lambda b, p: (0, p)),
            pl.BlockSpec((1, LANES), lambda b, p: (0, p)),
            pl.BlockSpec((1, LANES, 4 * LANES), lambda b, p: (p, 0, 0)),
            pl.BlockSpec((1, 1, 4 * LANES), lambda b, p: (p, 0, 0)),
            pl.BlockSpec((2, LANES), lambda b, p: (0, p)),
        ],
        out_specs=seq_spec,
        out_shape=jax.ShapeDtypeStruct((bn, n, D_RNN), F32),
        scratch_shapes=[pltpu.VMEM((n, LANES), F32)] * 4
        + [pltpu.VMEM((_scan_work_rows(n), LANES), F32)],
        compiler_params=pltpu.CompilerParams(vmem_limit_bytes=VMEM_LIMIT),
        name="rnn",
    )(xr, xr_c, gr, conv_w, conv_b.reshape(1, D_RNN), wg, bgate, lam)


def _gate_weights(rg_wa, rg_ba, rg_wx, rg_bx):
    eye = jnp.eye(2, dtype=F32)
    blocks, biases = [], []
    for d in range(2):
        for w, bvec in ((rg_wa[d], rg_ba[d]), (rg_wx[d], rg_bx[d])):
            w4 = w.reshape(N_LANE_GROUPS, 2, RNN_HEAD_DIM, RNN_HEAD_DIM)
            bd = jnp.einsum("paij,ac->paicj", w4, eye).reshape(N_LANE_GROUPS, LANES, LANES)
            blocks.append(bd)
            biases.append(bvec.reshape(N_LANE_GROUPS, 1, LANES))
    return jnp.concatenate(blocks, axis=-1).astype(BF16), jnp.concatenate(biases, axis=-1)


def _gconv_kernel(u_ref, bg_ref, w_ref, y_ref, *, n):
    p = pl.program_id(1)
    u = u_ref[0]
    w = w_ref[...]

    @pl.when(p < D_CONV_H // LANES)
    def _():
        col = lax.broadcasted_iota(jnp.int32, u.shape, 0) % GRID_W
        left = jnp.where(col > 0, _shift_rows(u, 1), 0.0)
        right = jnp.where(col < GRID_W - 1, _shift_rows(u, -1), 0.0)
        y_ref[0] = bg_ref[0] * (w[0:1] * left + w[1:2] * u + w[2:3] * right)

    @pl.when(p >= D_CONV_H // LANES)
    def _():
        y_ref[0] = bg_ref[0] * (w[0:1] * _shift_rows(u, GRID_W) + w[1:2] * u
                                + w[2:3] * _shift_rows(u, -GRID_W))


def _gconv(u, bg, w):
    bn, n, _ = u.shape
    seq_spec = pl.BlockSpec((1, n, LANES), lambda b, p: (b, 0, p))
    return pl.pallas_call(
        functools.partial(_gconv_kernel, n=n),
        grid=(bn, D_CONV // LANES),
        in_specs=[seq_spec, seq_spec, pl.BlockSpec((3, LANES), lambda b, p: (0, p))],
        out_specs=seq_spec,
        out_shape=jax.ShapeDtypeStruct((bn, n, D_CONV), F32),
        compiler_params=pltpu.CompilerParams(vmem_limit_bytes=VMEM_LIMIT),
        name="gconv",
    )(u, bg, w)


def _lane_max(x, mask):
    return jnp.max(jnp.where(mask, x, -jnp.inf), axis=-1, keepdims=True)


def _first_lane(cond, lane):
    return jnp.min(jnp.where(cond, lane, float(LANES)), axis=-1, keepdims=True)


def _outproj_kernel(x_ref, yr_ref, yc_ref, w_ref, mod_ref, g_ref, wr_hi_ref, wr_lo_ref, br_ref,
                    x1_ref, mt_ref, route_ref, cnt_ref, carry, *, tm):
    first = (pl.program_id(0) == 0) & (pl.program_id(1) == 0)

    @pl.when(first)
    def _():
        carry[...] = jnp.zeros_like(carry)

    mix = _dot(yr_ref[0].astype(BF16), w_ref[0:D_RNN, :]) + _dot(yc_ref[0].astype(BF16), w_ref[D_RNN:, :])
    x1 = x_ref[0] + mod_ref[0, 2:3, :] * mix
    x1_ref[0] = x1
    m = _norm_mod(x1, g_ref[...], mod_ref[0, 4:5, :], mod_ref[0, 3:4, :])
    for s in range(ROW_TILES):
        mt_ref[pl.ds(s, tm, stride=ROW_TILES), :] = m[:, s * LANES:(s + 1) * LANES]

    m_hi, m_lo = _split_bf16(m)
    logits = (_dot(m_hi, wr_hi_ref[...]) + _dot(m_lo, wr_hi_ref[...]) + _dot(m_hi, wr_lo_ref[...])
              + br_ref[...])
    lane_i = lax.broadcasted_iota(jnp.int32, logits.shape, 1)
    lane = lane_i.astype(F32)
    is_grp = lane_i < N_GROUPS
    g_max = _lane_max(logits, is_grp)
    grp = _first_lane(is_grp & (logits == g_max), lane)
    p_g = 1.0 / jnp.sum(jnp.where(is_grp, jnp.exp(logits - g_max), 0.0), axis=-1, keepdims=True)
    lo_lane = EXPERT_LANE0 + grp * EXPERTS_PER_GROUP
    in_grp = (lane >= lo_lane) & (lane < lo_lane + EXPERTS_PER_GROUP)
    l1 = _lane_max(logits, in_grp)
    i1 = _first_lane(in_grp & (logits == l1), lane)
    rest = in_grp & (lane != i1)
    l2 = _lane_max(logits, rest)
    i2 = _first_lane(rest & (logits == l2), lane)
    r21 = jnp.exp(l2 - l1)
    gate1 = p_g / (1.0 + r21)
    gate2 = gate1 * r21

    oh1 = jnp.where(lane == i1, 1.0, 0.0)
    oh2 = jnp.where(lane == i2, 1.0, 0.0)
    both = (oh1 + oh2).astype(BF16)
    ti = lax.broadcasted_iota(jnp.int32, (tm, tm), 0)
    tj = lax.broadcasted_iota(jnp.int32, (tm, tm), 1)
    tri = jnp.where(tj < ti, 1.0, 0.0).astype(BF16)
    before = _dot(tri, both) + carry[...]
    rank1 = jnp.sum(oh1 * before, axis=-1, keepdims=True)
    rank2 = jnp.sum(oh2 * before, axis=-1, keepdims=True)
    total = carry[...] + jnp.sum(oh1 + oh2, axis=0, keepdims=True)
    carry[...] = total
    cnt_ref[...] = total

    e1 = i1 - EXPERT_LANE0
    e2 = i2 - EXPERT_LANE0
    out = jnp.zeros(logits.shape, F32)
    for k, val in enumerate((e1, e2, gate1, gate2, rank1, rank2)):
        out = jnp.where(lane_i == k, val, out)
    route_ref[...] = out


def _outproj(x, y_rnn, y_conv, w_out_bf16, mod3, norm_g, wr_hi, wr_lo, br):
    bn, n, d = x.shape
    tm = min(TOK_TILE, n)
    nt = n // tm
    t_all = bn * n
    return pl.pallas_call(
        functools.partial(_outproj_kernel, tm=tm),
        grid=(bn, nt),
        in_specs=[
            pl.BlockSpec((1, tm, d), lambda b, i: (b, i, 0)),
            pl.BlockSpec((1, tm, D_RNN), lambda b, i: (b, i, 0)),
            pl.BlockSpec((1, tm, D_CONV), lambda b, i: (b, i, 0)),
            pl.BlockSpec((D_RNN + D_CONV, d), lambda b, i: (0, 0)),
            pl.BlockSpec((1, 6, d), lambda b, i: (b, 0, 0)),
            pl.BlockSpec((1, d), lambda b, i: (0, 0)),
            pl.BlockSpec((d, LANES), lambda b, i: (0, 0)),
            pl.BlockSpec((d, LANES), lambda b, i: (0, 0)),
            pl.BlockSpec((1, LANES), lambda b, i: (0, 0)),
        ],
        out_specs=[
            pl.BlockSpec((1, tm, d), lambda b, i: (b, i, 0)),
            pl.BlockSpec((tm * ROW_TILES, LANES), lambda b, i: (b * nt + i, 0)),
            pl.BlockSpec((tm, LANES), lambda b, i: (b * nt + i, 0)),
            pl.BlockSpec((1, LANES), lambda b, i: (0, 0)),
        ],
        out_shape=[
            jax.ShapeDtypeStruct((bn, n, d), F32),
            jax.ShapeDtypeStruct((t_all * ROW_TILES, LANES), F32),
            jax.ShapeDtypeStruct((t_all, LANES), F32),
            jax.ShapeDtypeStruct((1, LANES), F32),
        ],
        scratch_shapes=[pltpu.VMEM((1, LANES), F32)],
        compiler_params=pltpu.CompilerParams(
            dimension_semantics=("arbitrary", "arbitrary"), vmem_limit_bytes=VMEM_LIMIT),
        name="outproj",
    )(x, y_rnn, y_conv, w_out_bf16, mod3, norm_g.reshape(1, d), wr_hi, wr_lo, br)


def _expert_kernel(be_ref, nu_ref, tok_ref, mt_hbm, wg_ref, wu_ref, wd_ref, yb_ref,
                   xbuf, sems, wg_s, wu_s, wd_s):
    j = pl.program_id(0)
    n_used = nu_ref[0]

    def row_copy(tok, slot, r):
        dst = xbuf.at[slot, pl.ds(pl.multiple_of(r * ROW_TILES, ROW_TILES), ROW_TILES)]
        return pltpu.make_async_copy(mt_hbm.at[tok], dst, sems.at[slot])

    def gather(blk, slot):
        def issue(c, carry):
            for u in range(DMA_UNROLL):
                r = c * DMA_UNROLL + u
                row_copy(tok_ref[blk * MOE_BLK + r], slot, r).start()
            return carry
        lax.fori_loop(0, MOE_BLK // DMA_UNROLL, issue, 0)

    @pl.when(j >= n_used)
    def _():
        yb_ref[...] = jnp.zeros_like(yb_ref)

    @pl.when(j < n_used)
    def _():
        slot = j % 2

        @pl.when(j == 0)
        def _():
            gather(0, 0)

        @pl.when(j + 1 < n_used)
        def _():
            gather(j + 1, 1 - slot)

        e = be_ref[j]
        prev = be_ref[jnp.maximum(j - 1, 0)]

        @pl.when((j == 0) | (e != prev))
        def _():
            wg_s[...] = wg_ref[0].astype(BF16)
            wu_s[...] = wu_ref[0].astype(BF16)
            wd_s[...] = wd_ref[0].astype(BF16)

        def drain(c, carry):
            for u in range(DMA_UNROLL):
                row_copy(0, slot, 0).wait()
            return carry
        lax.fori_loop(0, MOE_BLK // DMA_UNROLL, drain, 0)

        x = jnp.concatenate(
            [xbuf[slot, pl.ds(s, MOE_BLK, stride=ROW_TILES), :] for s in range(ROW_TILES)], axis=-1)
        xb16 = x.astype(BF16)
        gate = _dot(xb16, wg_s[...])
        up = _dot(xb16, wu_s[...])
        h = (gate * jax.nn.sigmoid(gate)) * up
        y = _dot(h.astype(BF16), wd_s[...])
        for s in range(ROW_TILES):
            yb_ref[pl.ds(s, MOE_BLK, stride=ROW_TILES), :] = y[:, s * LANES:(s + 1) * LANES]


def _experts(block_e, n_used, slot_tok, mt3, w_gate, w_up, w_down, n_blocks):
    def wsel(j, be, nu, tok):
        return (be[jnp.minimum(j, nu[0] - 1)], 0, 0)

    return pl.pallas_call(
        _expert_kernel,
        grid_spec=pltpu.PrefetchScalarGridSpec(
            num_scalar_prefetch=3,
            grid=(n_blocks,),
            in_specs=[
                pl.BlockSpec(memory_space=pl.ANY),
                pl.BlockSpec((1, D_MODEL, D_EXPERT), wsel),
                pl.BlockSpec((1, D_MODEL, D_EXPERT), wsel),
                pl.BlockSpec((1, D_EXPERT, D_MODEL), wsel),
            ],
            out_specs=pl.BlockSpec((MOE_BLK * ROW_TILES, LANES), lambda j, be, nu, tok: (j, 0)),
            scratch_shapes=[
                pltpu.VMEM((2, MOE_BLK * ROW_TILES, LANES), F32),
                pltpu.SemaphoreType.DMA((2,)),
                pltpu.VMEM((D_MODEL, D_EXPERT), BF16),
                pltpu.VMEM((D_MODEL, D_EXPERT), BF16),
                pltpu.VMEM((D_EXPERT, D_MODEL), BF16),
            ],
        ),
        out_shape=jax.ShapeDtypeStruct((n_blocks * MOE_BLK * ROW_TILES, LANES), F32),
        compiler_params=pltpu.CompilerParams(
            dimension_semantics=("arbitrary",), vmem_limit_bytes=VMEM_LIMIT),
        name="expert",
    )(block_e, n_used, slot_tok, mt3, w_gate, w_up, w_down)


def _combine_kernel(dest_ref, yb_hbm, x1_ref, route_ref, mod_ref, g_ref, o_ref, ybuf, sem, *, tc):
    i = pl.program_id(0)
    base = i * tc

    def row_copy(d, k, r):
        return pltpu.make_async_copy(
            yb_hbm.at[d], ybuf.at[k, pl.ds(pl.multiple_of(r * ROW_TILES, ROW_TILES), ROW_TILES)], sem)

    def issue(c, carry):
        for u in range(DMA_UNROLL):
            r = c * DMA_UNROLL + u
            for k in range(TOP_K):
                row_copy(dest_ref[TOP_K * (base + r) + k], k, r).start()
        return carry

    def drain(c, carry):
        for u in range(DMA_UNROLL * TOP_K):
            row_copy(0, 0, 0).wait()
        return carry

    lax.fori_loop(0, tc // DMA_UNROLL, issue, 0)
    lax.fori_loop(0, tc // DMA_UNROLL, drain, 0)

    def rows(k):
        return jnp.concatenate(
            [ybuf[k, pl.ds(s, tc, stride=ROW_TILES), :] for s in range(ROW_TILES)], axis=-1)

    route = route_ref[...]
    y = route[:, 2:3] * rows(0) + route[:, 3:4] * rows(1)
    x2 = x1_ref[...] + mod_ref[0, 5:6, :] * y
    ms = jnp.mean(x2 * x2, axis=-1, keepdims=True)
    o_ref[...] = x2 * lax.rsqrt(ms + NORM_EPS) * g_ref[...]


def _combine(dest, yb3, x1_2d, route, mod3, final_g, seq):
    t_all, d = x1_2d.shape
    tc = min(COMBINE_TILE, seq)
    per_seq = seq // tc
    return pl.pallas_call(
        functools.partial(_combine_kernel, tc=tc),
        grid_spec=pltpu.PrefetchScalarGridSpec(
            num_scalar_prefetch=1,
            grid=(t_all // tc,),
            in_specs=[
                pl.BlockSpec(memory_space=pl.ANY),
                pl.BlockSpec((tc, d), lambda i, dest: (i, 0)),
                pl.BlockSpec((tc, LANES), lambda i, dest: (i, 0)),
                pl.BlockSpec((1, 6, d), lambda i, dest: (i // per_seq, 0, 0)),
                pl.BlockSpec((1, d), lambda i, dest: (0, 0)),
            ],
            out_specs=pl.BlockSpec((tc, d), lambda i, dest: (i, 0)),
            scratch_shapes=[
                pltpu.VMEM((TOP_K, tc * ROW_TILES, LANES), F32),
                pltpu.SemaphoreType.DMA,
            ],
        ),
        out_shape=jax.ShapeDtypeStruct((t_all, d), F32),
        compiler_params=pltpu.CompilerParams(
            dimension_semantics=("arbitrary",), vmem_limit_bytes=VMEM_LIMIT),
        name="combine",
    )(dest, yb3, x1_2d, route, mod3, final_g.reshape(1, d))


def kernel(x, c, ctx, c_ctx, ada_w, ada_b, norm1_g, norm2_g, w_in, rnn_conv_w, rnn_conv_b, rg_wa, rg_ba,
           rg_wx, rg_bx, rg_lambda, sc_conv_w, w_out, router_group_w, router_group_b, router_exp_w,
           router_exp_b, exp_w_gate, exp_w_up, exp_w_down, final_norm_g):
    bn, seq, d = x.shape
    assert d == D_MODEL and bn < MOD_ROWS and ada_w.shape[0] == 1
    t_all = bn * seq

    cc = jnp.concatenate([c, c_ctx[None], jnp.zeros((MOD_ROWS - bn - 1, d), F32)], axis=0)
    mod3 = _modulation(cc, ada_w[0], ada_b[0]).reshape(MOD_ROWS, 6, d)

    w_in_b = w_in[0].astype(BF16)
    xr, gr, u, bg = _inproj(x, mod3, None, norm1_g[0], w_in_b, latent=True)
    (xr_c,) = _inproj(ctx, mod3, bn, norm1_g[0], w_in_b[:, :D_RNN], latent=False)

    wg, bgate = _gate_weights(rg_wa[0], rg_ba[0], rg_wx[0], rg_bx[0])
    y_rnn = _rnn(xr, xr_c, gr, rnn_conv_w[0], rnn_conv_b[0], wg, bgate, rg_lambda[0])
    y_conv = _gconv(u, bg, sc_conv_w[0])

    wr = jnp.zeros((d, LANES), F32)
    wr = wr.at[:, :N_GROUPS].set(router_group_w[0]).at[:, EXPERT_LANE0:EXPERT_LANE0 + N_EXPERTS].set(router_exp_w[0])
    br = jnp.zeros((1, LANES), F32)
    br = br.at[0, :N_GROUPS].set(router_group_b[0]).at[0, EXPERT_LANE0:EXPERT_LANE0 + N_EXPERTS].set(router_exp_b[0])
    wr_hi, wr_lo = _split_bf16(wr)
    x1, mt, route, cnt = _outproj(x, y_rnn, y_conv, w_out[0].astype(BF16), mod3, norm2_g[0], wr_hi, wr_lo, br)

    n_assign = t_all * TOP_K
    n_blocks = (n_assign + N_EXPERTS * (MOE_BLK - 1) + MOE_BLK - 1) // MOE_BLK
    counts = cnt[0, EXPERT_LANE0:EXPERT_LANE0 + N_EXPERTS].astype(jnp.int32)
    pcounts = (counts + MOE_BLK - 1) // MOE_BLK * MOE_BLK
    pends = jnp.cumsum(pcounts)
    pstarts = pends - pcounts
    experts = route[:, 0:TOP_K].astype(jnp.int32)
    ranks = route[:, 4:4 + TOP_K].astype(jnp.int32)
    onehot = experts[:, :, None] == jnp.arange(N_EXPERTS, dtype=jnp.int32)
    dest = (ranks + jnp.sum(jnp.where(onehot, pstarts, 0), axis=-1)).reshape(n_assign)
    n_used = (pends[-1] // MOE_BLK).astype(jnp.int32)
    blk_start = jnp.arange(n_blocks, dtype=jnp.int32) * MOE_BLK
    block_e = jnp.minimum(jnp.sum(blk_start[:, None] >= pends[None, :], axis=1), N_EXPERTS - 1)
    last_e = jnp.max(jnp.where(counts > 0, jnp.arange(N_EXPERTS, dtype=jnp.int32), 0))
    block_e = jnp.where(blk_start < pends[-1], block_e, last_e).astype(jnp.int32)

    n_slots = n_blocks * MOE_BLK
    tok_of_assign = jnp.arange(n_assign, dtype=jnp.int32) // TOP_K
    slot_tok = jnp.zeros((n_slots,), jnp.int32).at[dest].set(tok_of_assign, unique_indices=True)
    yb2 = _experts(block_e, n_used.reshape(1), slot_tok, mt.reshape(t_all, ROW_TILES, LANES),
                   exp_w_gate[0], exp_w_up[0], exp_w_down[0], n_blocks)
    out = _combine(dest, yb2.reshape(n_slots, ROW_TILES, LANES), x1.reshape(t_all, d), route, mod3,
                   final_norm_g, seq)
    return out.reshape(bn, seq, d)
```

```python
import functools

import jax
import jax.numpy as jnp
from jax import lax
from jax.experimental import pallas as pl
from jax.experimental.pallas import tpu as pltpu

F32 = jnp.float32
BF16 = jnp.bfloat16

D_MODEL = 1024
D_RNN = 512
D_CONV = 512
D_CONV_H = D_CONV // 2
RNN_HEADS = 8
RNN_HEAD_DIM = D_RNN // RNN_HEADS
GRID_W = 64
RG_C = 8.0
N_GROUPS = 4
EXPERTS_PER_GROUP = 8
N_EXPERTS = N_GROUPS * EXPERTS_PER_GROUP
TOP_K = 2
D_EXPERT = 512
NORM_EPS = 1e-6

LANES = 128
SUBLANES = 8
ROW_TILES = D_MODEL // LANES
N_LANE_GROUPS = D_RNN // LANES
EXPERT_LANE0 = N_GROUPS

MOD_ROWS = 16
MOD_TN = 768
TOK_TILE = 512
MOE_BLK = 256
COMBINE_TILE = 256
DMA_UNROLL = 16
VMEM_LIMIT = 48 * 1024 * 1024


def _dot(a, b):
    return jnp.dot(a, b, preferred_element_type=F32)


def _split_bf16(x):
    hi = x.astype(BF16)
    lo = (x - hi.astype(F32)).astype(BF16)
    return hi, lo


def _mod_kernel(cc_ref, w_ref, b_ref, o_ref):
    s = cc_ref[...]
    s = s * jax.nn.sigmoid(s)
    s_hi, s_lo = _split_bf16(s)
    w_hi, w_lo = _split_bf16(w_ref[...])
    o_ref[...] = _dot(s_hi, w_hi) + _dot(s_lo, w_hi) + _dot(s_hi, w_lo) + b_ref[...]


def _modulation(cc, ada_w, ada_b):
    n = ada_w.shape[1]
    return pl.pallas_call(
        _mod_kernel,
        grid=(n // MOD_TN,),
        in_specs=[
            pl.BlockSpec((MOD_ROWS, D_MODEL), lambda j: (0, 0)),
            pl.BlockSpec((D_MODEL, MOD_TN), lambda j: (0, j)),
            pl.BlockSpec((1, MOD_TN), lambda j: (0, j)),
        ],
        out_specs=pl.BlockSpec((MOD_ROWS, MOD_TN), lambda j: (0, j)),
        out_shape=jax.ShapeDtypeStruct((MOD_ROWS, n), F32),
        compiler_params=pltpu.CompilerParams(vmem_limit_bytes=VMEM_LIMIT),
        name="mod",
    )(cc, ada_w, ada_b.reshape(1, n))


def _norm_mod(x, g, scale, shift):
    ms = jnp.mean(x * x, axis=-1, keepdims=True)
    y = x * lax.rsqrt(ms + NORM_EPS) * g
    return y * (1.0 + scale) + shift


def _inproj_kernel(x_ref, mod_ref, g_ref, w_ref, *out_refs, latent):
    h = _norm_mod(x_ref[0], g_ref[...], mod_ref[0, 1:2, :], mod_ref[0, 0:1, :])
    hb = h.astype(BF16)
    xr = _dot(hb, w_ref[:, 0:D_RNN])
    out_refs[0][0] = xr
    if latent:
        o = D_RNN
        out_refs[1][0] = _dot(hb, w_ref[:, o:o + D_RNN])
        o += D_RNN
        v = _dot(hb, w_ref[:, o:o + D_CONV])
        out_refs[3][0] = _dot(hb, w_ref[:, o + D_CONV:o + 2 * D_CONV])
        cg = _dot(hb, w_ref[:, o + 2 * D_CONV:o + 3 * D_CONV])
        out_refs[2][0] = cg * v


def _inproj(x, mod3, mod_row, norm_g, w_bf16, latent):
    bn, n, d = x.shape
    tm = min(TOK_TILE, n)
    n_out = 4 if latent else 1
    width = w_bf16.shape[1]
    mod_map = (lambda b, i: (b, 0, 0)) if mod_row is None else (lambda b, i: (mod_row, 0, 0))
    return pl.pallas_call(
        functools.partial(_inproj_kernel, latent=latent),
        grid=(bn, n // tm),
        in_specs=[
            pl.BlockSpec((1, tm, d), lambda b, i: (b, i, 0)),
            pl.BlockSpec((1, 6, d), mod_map),
            pl.BlockSpec((1, d), lambda b, i: (0, 0)),
            pl.BlockSpec((d, width), lambda b, i: (0, 0)),
        ],
        out_specs=[pl.BlockSpec((1, tm, D_RNN), lambda b, i: (b, i, 0))] * n_out,
        out_shape=[jax.ShapeDtypeStruct((bn, n, D_RNN), F32)] * n_out,
        compiler_params=pltpu.CompilerParams(vmem_limit_bytes=VMEM_LIMIT),
        name="inproj_lat" if latent else "inproj_ctx",
    )(x, mod3, norm_g.reshape(1, d), w_bf16)


def _shift_rows(x, k):
    n = x.shape[0]
    row = lax.broadcasted_iota(jnp.int32, x.shape, 0)
    rolled = pltpu.roll(x, k % n, axis=0)
    valid = (row >= k) if k > 0 else (row < n + k)
    return jnp.where(valid, rolled, 0.0)


def _round8(v):
    return (v + SUBLANES - 1) // SUBLANES * SUBLANES


def _scan_work_rows(n):
    rows = 0
    while n > SUBLANES:
        g = n // SUBLANES
        rows += 2 * n + 2 * _round8(g) + _round8(g) + 2 * SUBLANES
        n = g
    return rows


def _linear_scan(a_ref, a_off, b_ref, b_off, h_ref, h_off, n, h0, reverse, work, w_off):
    if n <= SUBLANES:
        h = h0
        for r in (range(n - 1, -1, -1) if reverse else range(n)):
            h = a_ref[pl.ds(a_off + r, 1), :] * h + b_ref[pl.ds(b_off + r, 1), :]
            h_ref[pl.ds(h_off + r, 1), :] = h
        return
    g = n // SUBLANES
    gp = _round8(g)
    ca_off = w_off
    cb_off = ca_off + n
    a2_off = cb_off + n
    b2_off = a2_off + gp
    hs_off = b2_off + gp
    next_off = hs_off + gp + 2 * SUBLANES

    a = a_ref[pl.ds(a_off, n), :].reshape(g, SUBLANES, LANES)
    b = b_ref[pl.ds(b_off, n), :].reshape(g, SUBLANES, LANES)
    row = lax.broadcasted_iota(jnp.int32, (g, SUBLANES, LANES), 1)
    for s in (1, 2, 4):
        if reverse:
            a_s = pltpu.roll(a, SUBLANES - s, axis=1)
            b_s = pltpu.roll(b, SUBLANES - s, axis=1)
            valid = row < SUBLANES - s
        else:
            a_s = pltpu.roll(a, s, axis=1)
            b_s = pltpu.roll(b, s, axis=1)
            valid = row >= s
        b = jnp.where(valid, a * b_s + b, b)
        a = jnp.where(valid, a * a_s, a)
    work[pl.ds(ca_off, n), :] = a.reshape(n, LANES)
    work[pl.ds(cb_off, n), :] = b.reshape(n, LANES)

    last = 0 if reverse else SUBLANES - 1
    work[pl.ds(a2_off, g), :] = work[pl.ds(ca_off + last, g, stride=SUBLANES), :]
    work[pl.ds(b2_off, g), :] = work[pl.ds(cb_off + last, g, stride=SUBLANES), :]
    _linear_scan(work, a2_off, work, b2_off, work, hs_off + SUBLANES, g, h0, reverse, work, next_off)
    if reverse:
        work[pl.ds(hs_off + SUBLANES + g, 1), :] = h0
        in_off = hs_off + SUBLANES + 1
    else:
        work[pl.ds(hs_off + SUBLANES - 1, 1), :] = h0
        in_off = hs_off + SUBLANES - 1

    def apply_group(gi, r0):
        h_in = work[pl.ds(in_off + gi, 1), :]
        h_ref[pl.ds(h_off + r0, SUBLANES), :] = (
            work[pl.ds(ca_off + r0, SUBLANES), :] * h_in + work[pl.ds(cb_off + r0, SUBLANES), :])

    if g <= 32:
        for gi in range(g):
            apply_group(gi, gi * SUBLANES)
    else:
        def body(go, carry):
            for u in range(SUBLANES):
                gi = go * SUBLANES + u
                apply_group(gi, pl.multiple_of(gi * SUBLANES, SUBLANES))
            return carry
        lax.fori_loop(0, g // SUBLANES, body, 0)


def _rnn_kernel(xr_ref, xrc_ref, gr_ref, cw_ref, cb_ref, wg_ref, bg_ref, lam_ref, y_ref,
                a_s, b_s, h_s, hsum_s, work, *, n_lat, n_ctx):
    nl = -lam_ref[...]
    sp = jnp.maximum(nl, 0.0) + jnp.log1p(jnp.exp(-jnp.abs(nl)))
    cw = cw_ref[...]
    bias = cb_ref[...]
    wg = wg_ref[0]
    bg = bg_ref[0]

    def conv_gates(x):
        xc = (cw[0:1] * _shift_rows(x, 2) + cw[1:2] * _shift_rows(x, 1) + cw[2:3] * x
              + cw[3:4] * _shift_rows(x, -1)) + bias
        return xc, _dot(xc.astype(BF16), wg) + bg

    def coeffs(xc, gates, d):
        r = jax.nn.sigmoid(gates[:, (2 * d) * LANES:(2 * d + 1) * LANES])
        i = jax.nn.sigmoid(gates[:, (2 * d + 1) * LANES:(2 * d + 2) * LANES])
        log_a = (-RG_C * r) * sp[d:d + 1]
        a = jnp.exp(log_a)
        b = jnp.sqrt(-jnp.tanh(log_a) * (a * a + 1.0)) * (i * xc)
        return a, b

    zero = jnp.zeros((1, LANES), F32)
    xc_c, gates_c = conv_gates(xrc_ref[0])
    h0 = []
    for d in range(2):
        a, b = coeffs(xc_c, gates_c, d)
        a_s[pl.ds(0, n_ctx), :] = a
        b_s[pl.ds(0, n_ctx), :] = b
        _linear_scan(a_s, 0, b_s, 0, h_s, 0, n_ctx, zero, d == 1, work, 0)
        h0.append(h_s[pl.ds(0 if d == 1 else n_ctx - 1, 1), :])

    xc_l, gates_l = conv_gates(xr_ref[0])
    for d in range(2):
        a, b = coeffs(xc_l, gates_l, d)
        a_s[...] = a
        b_s[...] = b
        _linear_scan(a_s, 0, b_s, 0, h_s if d == 1 else hsum_s, 0, n_lat, h0[d], d == 1, work, 0)
    y_ref[0] = jax.nn.gelu(gr_ref[0], approximate=True) * (hsum_s[...] + h_s[...])


def _rnn(xr, xr_c, gr, conv_w, conv_b, wg, bgate, lam):
    bn, n, _ = xr.shape
    n_ctx = xr_c.shape[1]
    seq_spec = pl.BlockSpec((1, n, LANES), lambda b, p: (b, 0, p))
    return pl.pallas_call(
        functools.partial(_rnn_kernel, n_lat=n, n_ctx=n_ctx),
        grid=(bn, N_LANE_GROUPS),
        in_specs=[
            seq_spec,
            pl.BlockSpec((1, n_ctx, LANES), lambda b, p: (b, 0, p)),
            seq_spec,
            pl.BlockSpec((4, LANES), lambda b, p: (0, p)),
            pl.BlockSpec((1, LANES), lambda b, p: (0, p)),
            pl.BlockSpec((1, LANES, 4 * LANES), lambda b, p: (p, 0, 0)),
            pl.BlockSpec((1, 1, 4 * LANES), lambda b, p: (p, 0, 0)),
            pl.BlockSpec((2, LANES), lambda b, p: (0, p)),
        ],
        out_specs=seq_spec,
        out_shape=jax.ShapeDtypeStruct((bn, n, D_RNN), F32),
        scratch_shapes=[pltpu.VMEM((n, LANES), F32)] * 4
        + [pltpu.VMEM((_scan_work_rows(n), LANES), F32)],
        compiler_params=pltpu.CompilerParams(vmem_limit_bytes=VMEM_LIMIT),
        name="rnn",
    )(xr, xr_c, gr, conv_w, conv_b.reshape(1, D_RNN), wg, bgate, lam)


def _gate_weights(rg_wa, rg_ba, rg_wx, rg_bx):
    eye = jnp.eye(2, dtype=F32)
    blocks, biases = [], []
    for d in range(2):
        for w, bvec in ((rg_wa[d], rg_ba[d]), (rg_wx[d], rg_bx[d])):
            w4 = w.reshape(N_LANE_GROUPS, 2, RNN_HEAD_DIM, RNN_HEAD_DIM)
            bd = jnp.einsum("paij,ac->paicj", w4, eye).reshape(N_LANE_GROUPS, LANES, LANES)
            blocks.append(bd)
            biases.append(bvec.reshape(N_LANE_GROUPS, 1, LANES))
    return jnp.concatenate(blocks, axis=-1).astype(BF16), jnp.concatenate(biases, axis=-1)


def _gconv_kernel(u_ref, bg_ref, w_ref, y_ref, *, n):
    p = pl.program_id(1)
    u = u_ref[0]
    w = w_ref[...]

    @pl.when(p < D_CONV_H // LANES)
    def _():
        col = lax.broadcasted_iota(jnp.int32, u.shape, 0) % GRID_W
        left = jnp.where(col > 0, _shift_rows(u, 1), 0.0)
        right = jnp.where(col < GRID_W - 1, _shift_rows(u, -1), 0.0)
        y_ref[0] = bg_ref[0] * (w[0:1] * left + w[1:2] * u + w[2:3] * right)

    @pl.when(p >= D_CONV_H // LANES)
    def _():
        y_ref[0] = bg_ref[0] * (w[0:1] * _shift_rows(u, GRID_W) + w[1:2] * u
                                + w[2:3] * _shift_rows(u, -GRID_W))


def _gconv(u, bg, w):
    bn, n, _ = u.shape
    seq_spec = pl.BlockSpec((1, n, LANES), lambda b, p: (b, 0, p))
    return pl.pallas_call(
        functools.partial(_gconv_kernel, n=n),
        grid=(bn, D_CONV // LANES),
        in_specs=[seq_spec, seq_spec, pl.BlockSpec((3, LANES), lambda b, p: (0, p))],
        out_specs=seq_spec,
        out_shape=jax.ShapeDtypeStruct((bn, n, D_CONV), F32),
        compiler_params=pltpu.CompilerParams(vmem_limit_bytes=VMEM_LIMIT),
        name="gconv",
    )(u, bg, w)


def _lane_max(x, mask):
    return jnp.max(jnp.where(mask, x, -jnp.inf), axis=-1, keepdims=True)


def _first_lane(cond, lane):
    return jnp.min(jnp.where(cond, lane, float(LANES)), axis=-1, keepdims=True)


def _outproj_kernel(x_ref, yr_ref, yc_ref, w_ref, mod_ref, g_ref, wr_hi_ref, wr_lo_ref, br_ref,
                    x1_ref, mt_ref, route_ref, cnt_ref, carry, *, tm):
    first = (pl.program_id(0) == 0) & (pl.program_id(1) == 0)

    @pl.when(first)
    def _():
        carry[...] = jnp.zeros_like(carry)

    mix = _dot(yr_ref[0].astype(BF16), w_ref[0:D_RNN, :]) + _dot(yc_ref[0].astype(BF16), w_ref[D_RNN:, :])
    x1 = x_ref[0] + mod_ref[0, 2:3, :] * mix
    x1_ref[0] = x1
    m = _norm_mod(x1, g_ref[...], mod_ref[0, 4:5, :], mod_ref[0, 3:4, :])
    for s in range(ROW_TILES):
        mt_ref[pl.ds(s, tm, stride=ROW_TILES), :] = m[:, s * LANES:(s + 1) * LANES]

    m_hi, m_lo = _split_bf16(m)
    logits = (_dot(m_hi, wr_hi_ref[...]) + _dot(m_lo, wr_hi_ref[...]) + _dot(m_hi, wr_lo_ref[...])
              + br_ref[...])
    lane_i = lax.broadcasted_iota(jnp.int32, logits.shape, 1)
    lane = lane_i.astype(F32)
    is_grp = lane_i < N_GROUPS
    g_max = _lane_max(logits, is_grp)
    grp = _first_lane(is_grp & (logits == g_max), lane)
    p_g = 1.0 / jnp.sum(jnp.where(is_grp, jnp.exp(logits - g_max), 0.0), axis=-1, keepdims=True)
    lo_lane = EXPERT_LANE0 + grp * EXPERTS_PER_GROUP
    in_grp = (lane >= lo_lane) & (lane < lo_lane + EXPERTS_PER_GROUP)
    l1 = _lane_max(logits, in_grp)
    i1 = _first_lane(in_grp & (logits == l1), lane)
    rest = in_grp & (lane != i1)
    l2 = _lane_max(logits, rest)
    i2 = _first_lane(rest & (logits == l2), lane)
    r21 = jnp.exp(l2 - l1)
    gate1 = p_g / (1.0 + r21)
    gate2 = gate1 * r21

    oh1 = jnp.where(lane == i1, 1.0, 0.0)
    oh2 = jnp.where(lane == i2, 1.0, 0.0)
    both = (oh1 + oh2).astype(BF16)
    ti = lax.broadcasted_iota(jnp.int32, (tm, tm), 0)
    tj = lax.broadcasted_iota(jnp.int32, (tm, tm), 1)
    tri = jnp.where(tj < ti, 1.0, 0.0).astype(BF16)
    before = _dot(tri, both) + carry[...]
    rank1 = jnp.sum(oh1 * before, axis=-1, keepdims=True)
    rank2 = jnp.sum(oh2 * before, axis=-1, keepdims=True)
    total = carry[...] + jnp.sum(oh1 + oh2, axis=0, keepdims=True)
    carry[...] = total
    cnt_ref[...] = total

    e1 = i1 - EXPERT_LANE0
    e2 = i2 - EXPERT_LANE0
    out = jnp.zeros(logits.shape, F32)
    for k, val in enumerate((e1, e2, gate1, gate2, rank1, rank2)):
        out = jnp.where(lane_i == k, val, out)
    route_ref[...] = out


def _outproj(x, y_rnn, y_conv, w_out_bf16, mod3, norm_g, wr_hi, wr_lo, br):
    bn, n, d = x.shape
    tm = min(TOK_TILE, n)
    nt = n // tm
    t_all = bn * n
    return pl.pallas_call(
        functools.partial(_outproj_kernel, tm=tm),
        grid=(bn, nt),
        in_specs=[
            pl.BlockSpec((1, tm, d), lambda b, i: (b, i, 0)),
            pl.BlockSpec((1, tm, D_RNN), lambda b, i: (b, i, 0)),
            pl.BlockSpec((1, tm, D_CONV), lambda b, i: (b, i, 0)),
            pl.BlockSpec((D_RNN + D_CONV, d), lambda b, i: (0, 0)),
            pl.BlockSpec((1, 6, d), lambda b, i: (b, 0, 0)),
            pl.BlockSpec((1, d), lambda b, i: (0, 0)),
            pl.BlockSpec((d, LANES), lambda b, i: (0, 0)),
            pl.BlockSpec((d, LANES), lambda b, i: (0, 0)),
            pl.BlockSpec((1, LANES), lambda b, i: (0, 0)),
        ],
        out_specs=[
            pl.BlockSpec((1, tm, d), lambda b, i: (b, i, 0)),
            pl.BlockSpec((tm * ROW_TILES, LANES), lambda b, i: (b * nt + i, 0)),
            pl.BlockSpec((tm, LANES), lambda b, i: (b * nt + i, 0)),
            pl.BlockSpec((1, LANES), lambda b, i: (0, 0)),
        ],
        out_shape=[
            jax.ShapeDtypeStruct((bn, n, d), F32),
            jax.ShapeDtypeStruct((t_all * ROW_TILES, LANES), F32),
            jax.ShapeDtypeStruct((t_all, LANES), F32),
            jax.ShapeDtypeStruct((1, LANES), F32),
        ],
        scratch_shapes=[pltpu.VMEM((1, LANES), F32)],
        compiler_params=pltpu.CompilerParams(
            dimension_semantics=("arbitrary", "arbitrary"), vmem_limit_bytes=VMEM_LIMIT),
        name="outproj",
    )(x, y_rnn, y_conv, w_out_bf16, mod3, norm_g.reshape(1, d), wr_hi, wr_lo, br)


def _row_tile(ref, row):
    return ref.at[pl.ds(pl.multiple_of(row * ROW_TILES, ROW_TILES), ROW_TILES)]


def _slotmap_kernel(dest_ref, asg_ref):
    def init(c, carry):
        for u in range(DMA_UNROLL):
            asg_ref[c * DMA_UNROLL + u] = 0
        return carry
    lax.fori_loop(0, asg_ref.shape[0] // DMA_UNROLL, init, 0)

    def body(c, carry):
        for u in range(DMA_UNROLL):
            a = c * DMA_UNROLL + u
            asg_ref[dest_ref[a]] = a
        return carry
    lax.fori_loop(0, dest_ref.shape[0] // DMA_UNROLL, body, 0)


def _slotmap(dest, n_slots):
    return pl.pallas_call(
        _slotmap_kernel,
        in_specs=[pl.BlockSpec(memory_space=pltpu.SMEM)],
        out_specs=pl.BlockSpec(memory_space=pltpu.SMEM),
        out_shape=jax.ShapeDtypeStruct((n_slots,), jnp.int32),
        name="slotmap",
    )(dest)


def _expert_kernel(be_ref, nu_ref, asg_ref, mt_hbm, wg_ref, wu_ref, wd_ref, yb_ref,
                   xbuf, sems, wg_s, wu_s, wd_s):
    j = pl.program_id(0)
    n_used = nu_ref[0]

    def row_copy(tok, slot, r):
        return pltpu.make_async_copy(_row_tile(mt_hbm, tok), _row_tile(xbuf.at[slot], r), sems.at[slot])

    def gather(blk, slot):
        def issue(c, carry):
            for u in range(DMA_UNROLL):
                r = c * DMA_UNROLL + u
                tok = lax.shift_right_logical(asg_ref[blk * MOE_BLK + r], 1)
                row_copy(tok, slot, r).start(priority=u % 2)
            return carry
        lax.fori_loop(0, MOE_BLK // DMA_UNROLL, issue, 0)

    @pl.when(j >= n_used)
    def _():
        yb_ref[...] = jnp.zeros_like(yb_ref)

    @pl.when(j < n_used)
    def _():
        slot = j % 2

        @pl.when(j == 0)
        def _():
            gather(0, 0)

        @pl.when(j + 1 < n_used)
        def _():
            gather(j + 1, 1 - slot)

        e = be_ref[j]
        prev = be_ref[jnp.maximum(j - 1, 0)]

        @pl.when((j == 0) | (e != prev))
        def _():
            wg_s[...] = wg_ref[0].astype(BF16)
            wu_s[...] = wu_ref[0].astype(BF16)
            wd_s[...] = wd_ref[0].astype(BF16)

        def drain(c, carry):
            for u in range(DMA_UNROLL):
                row_copy(0, slot, 0).wait()
            return carry
        lax.fori_loop(0, MOE_BLK // DMA_UNROLL, drain, 0)

        x = jnp.concatenate(
            [xbuf[slot, pl.ds(s, MOE_BLK, stride=ROW_TILES), :] for s in range(ROW_TILES)], axis=-1)
        xb16 = x.astype(BF16)
        gate = _dot(xb16, wg_s[...])
        up = _dot(xb16, wu_s[...])
        h = (gate * jax.nn.sigmoid(gate)) * up
        y = _dot(h.astype(BF16), wd_s[...])
        for s in range(ROW_TILES):
            yb_ref[pl.ds(s, MOE_BLK, stride=ROW_TILES), :] = y[:, s * LANES:(s + 1) * LANES]


def _experts(block_e, n_used, slot_asg, mt, w_gate, w_up, w_down, n_blocks):
    def wsel(j, be, nu, asg):
        return (be[jnp.minimum(j, nu[0] - 1)], 0, 0)

    return pl.pallas_call(
        _expert_kernel,
        grid_spec=pltpu.PrefetchScalarGridSpec(
            num_scalar_prefetch=3,
            grid=(n_blocks,),
            in_specs=[
                pl.BlockSpec(memory_space=pl.ANY),
                pl.BlockSpec((1, D_MODEL, D_EXPERT), wsel),
                pl.BlockSpec((1, D_MODEL, D_EXPERT), wsel),
                pl.BlockSpec((1, D_EXPERT, D_MODEL), wsel),
            ],
            out_specs=pl.BlockSpec((MOE_BLK * ROW_TILES, LANES), lambda j, be, nu, asg: (j, 0)),
            scratch_shapes=[
                pltpu.VMEM((2, MOE_BLK * ROW_TILES, LANES), F32),
                pltpu.SemaphoreType.DMA((2,)),
                pltpu.VMEM((D_MODEL, D_EXPERT), BF16),
                pltpu.VMEM((D_MODEL, D_EXPERT), BF16),
                pltpu.VMEM((D_EXPERT, D_MODEL), BF16),
            ],
        ),
        out_shape=jax.ShapeDtypeStruct((n_blocks * MOE_BLK * ROW_TILES, LANES), F32),
        compiler_params=pltpu.CompilerParams(
            dimension_semantics=("arbitrary",), vmem_limit_bytes=VMEM_LIMIT),
        name="expert",
    )(block_e, n_used, slot_asg, mt, w_gate, w_up, w_down)


def _combine_kernel(dest_ref, yb_hbm, x1_ref, route_ref, mod_ref, g_ref, o_ref, ybuf, sems, *, tc):
    i = pl.program_id(0)
    slot = i % 2

    def row_copy(d, sl, k, r):
        return pltpu.make_async_copy(_row_tile(yb_hbm, d), _row_tile(ybuf.at[sl, k], r), sems.at[sl])

    def gather(step, sl):
        def issue(c, carry):
            for u in range(DMA_UNROLL):
                r = c * DMA_UNROLL + u
                for k in range(TOP_K):
                    row_copy(dest_ref[TOP_K * (step * tc + r) + k], sl, k, r).start(priority=k)
            return carry
        lax.fori_loop(0, tc // DMA_UNROLL, issue, 0)

    @pl.when(i == 0)
    def _():
        gather(0, 0)

    @pl.when(i + 1 < pl.num_programs(0))
    def _():
        gather(i + 1, 1 - slot)

    def drain(c, carry):
        for u in range(DMA_UNROLL * TOP_K):
            row_copy(0, slot, 0, 0).wait()
        return carry
    lax.fori_loop(0, tc // DMA_UNROLL, drain, 0)

    def rows(k):
        return jnp.concatenate(
            [ybuf[slot, k, pl.ds(s, tc, stride=ROW_TILES), :] for s in range(ROW_TILES)], axis=-1)

    route = route_ref[...]
    y = route[:, 2:3] * rows(0) + route[:, 3:4] * rows(1)
    x2 = x1_ref[...] + mod_ref[0, 5:6, :] * y
    ms = jnp.mean(x2 * x2, axis=-1, keepdims=True)
    o_ref[...] = x2 * lax.rsqrt(ms + NORM_EPS) * g_ref[...]


def _combine(dest, yb, x1_2d, route, mod3, final_g, seq):
    t_all, d = x1_2d.shape
    tc = min(COMBINE_TILE, seq)
    per_seq = seq // tc
    return pl.pallas_call(
        functools.partial(_combine_kernel, tc=tc),
        grid_spec=pltpu.PrefetchScalarGridSpec(
            num_scalar_prefetch=1,
            grid=(t_all // tc,),
            in_specs=[
                pl.BlockSpec(memory_space=pl.ANY),
                pl.BlockSpec((tc, d), lambda i, dest: (i, 0)),
                pl.BlockSpec((tc, LANES), lambda i, dest: (i, 0)),
                pl.BlockSpec((1, 6, d), lambda i, dest: (i // per_seq, 0, 0)),
                pl.BlockSpec((1, d), lambda i, dest: (0, 0)),
            ],
            out_specs=pl.BlockSpec((tc, d), lambda i, dest: (i, 0)),
            scratch_shapes=[
                pltpu.VMEM((2, TOP_K, tc * ROW_TILES, LANES), F32),
                pltpu.SemaphoreType.DMA((2,)),
            ],
        ),
        out_shape=jax.ShapeDtypeStruct((t_all, d), F32),
        compiler_params=pltpu.CompilerParams(
            dimension_semantics=("arbitrary",), vmem_limit_bytes=VMEM_LIMIT),
        name="combine",
    )(dest, yb, x1_2d, route, mod3, final_g.reshape(1, d))


def kernel(x, c, ctx, c_ctx, ada_w, ada_b, norm1_g, norm2_g, w_in, rnn_conv_w, rnn_conv_b, rg_wa, rg_ba,
           rg_wx, rg_bx, rg_lambda, sc_conv_w, w_out, router_group_w, router_group_b, router_exp_w,
           router_exp_b, exp_w_gate, exp_w_up, exp_w_down, final_norm_g):
    bn, seq, d = x.shape
    assert d == D_MODEL and bn < MOD_ROWS and ada_w.shape[0] == 1
    t_all = bn * seq

    cc = jnp.concatenate([c, c_ctx[None], jnp.zeros((MOD_ROWS - bn - 1, d), F32)], axis=0)
    mod3 = _modulation(cc, ada_w[0], ada_b[0]).reshape(MOD_ROWS, 6, d)

    w_in_b = w_in[0].astype(BF16)
    xr, gr, u, bg = _inproj(x, mod3, None, norm1_g[0], w_in_b, latent=True)
    (xr_c,) = _inproj(ctx, mod3, bn, norm1_g[0], w_in_b[:, :D_RNN], latent=False)

    wg, bgate = _gate_weights(rg_wa[0], rg_ba[0], rg_wx[0], rg_bx[0])
    y_rnn = _rnn(xr, xr_c, gr, rnn_conv_w[0], rnn_conv_b[0], wg, bgate, rg_lambda[0])
    y_conv = _gconv(u, bg, sc_conv_w[0])

    wr = jnp.zeros((d, LANES), F32)
    wr = wr.at[:, :N_GROUPS].set(router_group_w[0]).at[:, EXPERT_LANE0:EXPERT_LANE0 + N_EXPERTS].set(router_exp_w[0])
    br = jnp.zeros((1, LANES), F32)
    br = br.at[0, :N_GROUPS].set(router_group_b[0]).at[0, EXPERT_LANE0:EXPERT_LANE0 + N_EXPERTS].set(router_exp_b[0])
    wr_hi, wr_lo = _split_bf16(wr)
    x1, mt, route, cnt = _outproj(x, y_rnn, y_conv, w_out[0].astype(BF16), mod3, norm2_g[0], wr_hi, wr_lo, br)

    n_assign = t_all * TOP_K
    n_blocks = (n_assign + N_EXPERTS * (MOE_BLK - 1) + MOE_BLK - 1) // MOE_BLK
    counts = cnt[0, EXPERT_LANE0:EXPERT_LANE0 + N_EXPERTS].astype(jnp.int32)
    pcounts = (counts + MOE_BLK - 1) // MOE_BLK * MOE_BLK
    pends = jnp.cumsum(pcounts)
    pstarts = pends - pcounts
    experts = route[:, 0:TOP_K].astype(jnp.int32)
    ranks = route[:, 4:4 + TOP_K].astype(jnp.int32)
    onehot = experts[:, :, None] == jnp.arange(N_EXPERTS, dtype=jnp.int32)
    dest = (ranks + jnp.sum(jnp.where(onehot, pstarts, 0), axis=-1)).reshape(n_assign)
    n_used = (pends[-1] // MOE_BLK).astype(jnp.int32)
    blk_start = jnp.arange(n_blocks, dtype=jnp.int32) * MOE_BLK
    block_e = jnp.minimum(jnp.sum(blk_start[:, None] >= pends[None, :], axis=1), N_EXPERTS - 1)
    last_e = jnp.max(jnp.where(counts > 0, jnp.arange(N_EXPERTS, dtype=jnp.int32), 0))
    block_e = jnp.where(blk_start < pends[-1], block_e, last_e).astype(jnp.int32)

    n_slots = n_blocks * MOE_BLK
    slot_asg = _slotmap(dest, n_slots)
    yb = _experts(block_e, n_used.reshape(1), slot_asg, mt, exp_w_gate[0], exp_w_up[0], exp_w_down[0],
                  n_blocks)
    out = _combine(dest, yb, x1.reshape(t_all, d), route, mod3, final_norm_g, seq)
    return out.reshape(bn, seq, d)
```

```python
import functools

import jax
import jax.numpy as jnp
from jax import lax
from jax.experimental import pallas as pl
from jax.experimental.pallas import tpu as pltpu

F32 = jnp.float32
BF16 = jnp.bfloat16

D_MODEL = 1024
D_RNN = 512
D_CONV = 512
D_CONV_H = D_CONV // 2
RNN_HEADS = 8
RNN_HEAD_DIM = D_RNN // RNN_HEADS
GRID_W = 64
RG_C = 8.0
N_GROUPS = 4
EXPERTS_PER_GROUP = 8
N_EXPERTS = N_GROUPS * EXPERTS_PER_GROUP
TOP_K = 2
D_EXPERT = 512
NORM_EPS = 1e-6

LANES = 128
SUBLANES = 8
ROW_TILES = D_MODEL // LANES
N_LANE_GROUPS = D_RNN // LANES
EXPERT_LANE0 = N_GROUPS

MOD_ROWS = 16
MOD_TN = 768
TOK_TILE = 512
MOE_BLK = 256
EXPERT_PIECES = 8
COMBINE_TILE = 256
DMA_UNROLL = 16
VMEM_LIMIT = 48 * 1024 * 1024


def _dot(a, b):
    return jnp.dot(a, b, preferred_element_type=F32)


def _split_bf16(x):
    hi = x.astype(BF16)
    lo = (x - hi.astype(F32)).astype(BF16)
    return hi, lo


def _mod_kernel(cc_ref, w_ref, b_ref, o_ref):
    s = cc_ref[...]
    s = s * jax.nn.sigmoid(s)
    s_hi, s_lo = _split_bf16(s)
    w_hi, w_lo = _split_bf16(w_ref[...])
    o_ref[...] = _dot(s_hi, w_hi) + _dot(s_lo, w_hi) + _dot(s_hi, w_lo) + b_ref[...]


def _modulation(cc, ada_w, ada_b):
    n = ada_w.shape[1]
    return pl.pallas_call(
        _mod_kernel,
        grid=(n // MOD_TN,),
        in_specs=[
            pl.BlockSpec((MOD_ROWS, D_MODEL), lambda j: (0, 0)),
            pl.BlockSpec((D_MODEL, MOD_TN), lambda j: (0, j)),
            pl.BlockSpec((1, MOD_TN), lambda j: (0, j)),
        ],
        out_specs=pl.BlockSpec((MOD_ROWS, MOD_TN), lambda j: (0, j)),
        out_shape=jax.ShapeDtypeStruct((MOD_ROWS, n), F32),
        compiler_params=pltpu.CompilerParams(vmem_limit_bytes=VMEM_LIMIT),
        name="mod",
    )(cc, ada_w, ada_b.reshape(1, n))


def _norm_mod(x, g, scale, shift):
    ms = jnp.mean(x * x, axis=-1, keepdims=True)
    y = x * lax.rsqrt(ms + NORM_EPS) * g
    return y * (1.0 + scale) + shift


def _inproj_kernel(x_ref, mod_ref, g_ref, w_ref, *out_refs, latent):
    h = _norm_mod(x_ref[0], g_ref[...], mod_ref[0, 1:2, :], mod_ref[0, 0:1, :])
    hb = h.astype(BF16)
    xr = _dot(hb, w_ref[:, 0:D_RNN])
    out_refs[0][0] = xr
    if latent:
        o = D_RNN
        out_refs[1][0] = _dot(hb, w_ref[:, o:o + D_RNN])
        o += D_RNN
        v = _dot(hb, w_ref[:, o:o + D_CONV])
        out_refs[3][0] = _dot(hb, w_ref[:, o + D_CONV:o + 2 * D_CONV])
        cg = _dot(hb, w_ref[:, o + 2 * D_CONV:o + 3 * D_CONV])
        out_refs[2][0] = cg * v


def _inproj(x, mod3, mod_row, norm_g, w_bf16, latent):
    bn, n, d = x.shape
    tm = min(TOK_TILE, n)
    n_out = 4 if latent else 1
    width = w_bf16.shape[1]
    mod_map = (lambda b, i: (b, 0, 0)) if mod_row is None else (lambda b, i: (mod_row, 0, 0))
    return pl.pallas_call(
        functools.partial(_inproj_kernel, latent=latent),
        grid=(bn, n // tm),
        in_specs=[
            pl.BlockSpec((1, tm, d), lambda b, i: (b, i, 0)),
            pl.BlockSpec((1, 6, d), mod_map),
            pl.BlockSpec((1, d), lambda b, i: (0, 0)),
            pl.BlockSpec((d, width), lambda b, i: (0, 0)),
        ],
        out_specs=[pl.BlockSpec((1, tm, D_RNN), lambda b, i: (b, i, 0))] * n_out,
        out_shape=[jax.ShapeDtypeStruct((bn, n, D_RNN), F32)] * n_out,
        compiler_params=pltpu.CompilerParams(vmem_limit_bytes=VMEM_LIMIT),
        name="inproj_lat" if latent else "inproj_ctx",
    )(x, mod3, norm_g.reshape(1, d), w_bf16)


def _shift_rows(x, k):
    n = x.shape[0]
    row = lax.broadcasted_iota(jnp.int32, x.shape, 0)
    rolled = pltpu.roll(x, k % n, axis=0)
    valid = (row >= k) if k > 0 else (row < n + k)
    return jnp.where(valid, rolled, 0.0)


def _round8(v):
    return (v + SUBLANES - 1) // SUBLANES * SUBLANES


def _scan_work_rows(n):
    rows = 0
    while n > SUBLANES:
        g = n // SUBLANES
        rows += 2 * n + 2 * _round8(g) + _round8(g) + 2 * SUBLANES
        n = g
    return rows


def _linear_scan(a_ref, a_off, b_ref, b_off, h_ref, h_off, n, h0, reverse, work, w_off):
    if n <= SUBLANES:
        h = h0
        for r in (range(n - 1, -1, -1) if reverse else range(n)):
            h = a_ref[pl.ds(a_off + r, 1), :] * h + b_ref[pl.ds(b_off + r, 1), :]
            h_ref[pl.ds(h_off + r, 1), :] = h
        return
    g = n // SUBLANES
    gp = _round8(g)
    ca_off = w_off
    cb_off = ca_off + n
    a2_off = cb_off + n
    b2_off = a2_off + gp
    hs_off = b2_off + gp
    next_off = hs_off + gp + 2 * SUBLANES

    a = a_ref[pl.ds(a_off, n), :].reshape(g, SUBLANES, LANES)
    b = b_ref[pl.ds(b_off, n), :].reshape(g, SUBLANES, LANES)
    row = lax.broadcasted_iota(jnp.int32, (g, SUBLANES, LANES), 1)
    for s in (1, 2, 4):
        if reverse:
            a_s = pltpu.roll(a, SUBLANES - s, axis=1)
            b_s = pltpu.roll(b, SUBLANES - s, axis=1)
            valid = row < SUBLANES - s
        else:
            a_s = pltpu.roll(a, s, axis=1)
            b_s = pltpu.roll(b, s, axis=1)
            valid = row >= s
        b = jnp.where(valid, a * b_s + b, b)
        a = jnp.where(valid, a * a_s, a)
    work[pl.ds(ca_off, n), :] = a.reshape(n, LANES)
    work[pl.ds(cb_off, n), :] = b.reshape(n, LANES)

    last = 0 if reverse else SUBLANES - 1
    work[pl.ds(a2_off, g), :] = work[pl.ds(ca_off + last, g, stride=SUBLANES), :]
    work[pl.ds(b2_off, g), :] = work[pl.ds(cb_off + last, g, stride=SUBLANES), :]
    _linear_scan(work, a2_off, work, b2_off, work, hs_off + SUBLANES, g, h0, reverse, work, next_off)
    if reverse:
        work[pl.ds(hs_off + SUBLANES + g, 1), :] = h0
        in_off = hs_off + SUBLANES + 1
    else:
        work[pl.ds(hs_off + SUBLANES - 1, 1), :] = h0
        in_off = hs_off + SUBLANES - 1

    def apply_group(gi, r0):
        h_in = work[pl.ds(in_off + gi, 1), :]
        h_ref[pl.ds(h_off + r0, SUBLANES), :] = (
            work[pl.ds(ca_off + r0, SUBLANES), :] * h_in + work[pl.ds(cb_off + r0, SUBLANES), :])

    if g <= 32:
        for gi in range(g):
            apply_group(gi, gi * SUBLANES)
    else:
        def body(go, carry):
            for u in range(SUBLANES):
                gi = go * SUBLANES + u
                apply_group(gi, pl.multiple_of(gi * SUBLANES, SUBLANES))
            return carry
        lax.fori_loop(0, g // SUBLANES, body, 0)


def _rnn_kernel(xr_ref, xrc_ref, gr_ref, cw_ref, cb_ref, wg_ref, bg_ref, lam_ref, y_ref,
                a_s, b_s, h_s, hsum_s, work, *, n_lat, n_ctx):
    nl = -lam_ref[...]
    sp = jnp.maximum(nl, 0.0) + jnp.log1p(jnp.exp(-jnp.abs(nl)))
    cw = cw_ref[...]
    bias = cb_ref[...]
    wg = wg_ref[0]
    bg = bg_ref[0]

    def conv_gates(x):
        xc = (cw[0:1] * _shift_rows(x, 2) + cw[1:2] * _shift_rows(x, 1) + cw[2:3] * x
              + cw[3:4] * _shift_rows(x, -1)) + bias
        return xc, _dot(xc.astype(BF16), wg) + bg

    def coeffs(xc, gates, d):
        r = jax.nn.sigmoid(gates[:, (2 * d) * LANES:(2 * d + 1) * LANES])
        i = jax.nn.sigmoid(gates[:, (2 * d + 1) * LANES:(2 * d + 2) * LANES])
        log_a = (-RG_C * r) * sp[d:d + 1]
        a = jnp.exp(log_a)
        b = jnp.sqrt(-jnp.tanh(log_a) * (a * a + 1.0)) * (i * xc)
        return a, b

    zero = jnp.zeros((1, LANES), F32)
    xc_c, gates_c = conv_gates(xrc_ref[0])
    h0 = []
    for d in range(2):
        a, b = coeffs(xc_c, gates_c, d)
        a_s[pl.ds(0, n_ctx), :] = a
        b_s[pl.ds(0, n_ctx), :] = b
        _linear_scan(a_s, 0, b_s, 0, h_s, 0, n_ctx, zero, d == 1, work, 0)
        h0.append(h_s[pl.ds(0 if d == 1 else n_ctx - 1, 1), :])

    xc_l, gates_l = conv_gates(xr_ref[0])
    for d in range(2):
        a, b = coeffs(xc_l, gates_l, d)
        a_s[...] = a
        b_s[...] = b
        _linear_scan(a_s, 0, b_s, 0, h_s if d == 1 else hsum_s, 0, n_lat, h0[d], d == 1, work, 0)
    y_ref[0] = jax.nn.gelu(gr_ref[0], approximate=True) * (hsum_s[...] + h_s[...])


def _rnn(xr, xr_c, gr, conv_w, conv_b, wg, bgate, lam):
    bn, n, _ = xr.shape
    n_ctx = xr_c.shape[1]
    seq_spec = pl.BlockSpec((1, n, LANES), lambda b, p: (b, 0, p))
    return pl.pallas_call(
        functools.partial(_rnn_kernel, n_lat=n, n_ctx=n_ctx),
        grid=(bn, N_LANE_GROUPS),
        in_specs=[
            seq_spec,
            pl.BlockSpec((1, n_ctx, LANES), lambda b, p: (b, 0, p)),
            seq_spec,
            pl.BlockSpec((4, LANES), lambda b, p: (0, p)),
            pl.BlockSpec((1, LANES), lambda b, p: (0, p)),
            pl.BlockSpec((1, LANES, 4 * LANES), lambda b, p: (p, 0, 0)),
            pl.BlockSpec((1, 1, 4 * LANES), lambda b, p: (p, 0, 0)),
            pl.BlockSpec((2, LANES), lambda b, p: (0, p)),
        ],
        out_specs=seq_spec,
        out_shape=jax.ShapeDtypeStruct((bn, n, D_RNN), F32),
        scratch_shapes=[pltpu.VMEM((n, LANES), F32)] * 4
        + [pltpu.VMEM((_scan_work_rows(n), LANES), F32)],
        compiler_params=pltpu.CompilerParams(vmem_limit_bytes=VMEM_LIMIT),
        name="rnn",
    )(xr, xr_c, gr, conv_w, conv_b.reshape(1, D_RNN), wg, bgate, lam)


def _gate_weights(rg_wa, rg_ba, rg_wx, rg_bx):
    eye = jnp.eye(2, dtype=F32)
    blocks, biases = [], []
    for d in range(2):
        for w, bvec in ((rg_wa[d], rg_ba[d]), (rg_wx[d], rg_bx[d])):
            w4 = w.reshape(N_LANE_GROUPS, 2, RNN_HEAD_DIM, RNN_HEAD_DIM)
            bd = jnp.einsum("paij,ac->paicj", w4, eye).reshape(N_LANE_GROUPS, LANES, LANES)
            blocks.append(bd)
            biases.append(bvec.reshape(N_LANE_GROUPS, 1, LANES))
    return jnp.concatenate(blocks, axis=-1).astype(BF16), jnp.concatenate(biases, axis=-1)


def _gconv_kernel(u_ref, bg_ref, w_ref, y_ref, *, n):
    p = pl.program_id(1)
    u = u_ref[0]
    w = w_ref[...]

    @pl.when(p < D_CONV_H // LANES)
    def _():
        col = lax.broadcasted_iota(jnp.int32, u.shape, 0) % GRID_W
        left = jnp.where(col > 0, _shift_rows(u, 1), 0.0)
        right = jnp.where(col < GRID_W - 1, _shift_rows(u, -1), 0.0)
        y_ref[0] = bg_ref[0] * (w[0:1] * left + w[1:2] * u + w[2:3] * right)

    @pl.when(p >= D_CONV_H // LANES)
    def _():
        y_ref[0] = bg_ref[0] * (w[0:1] * _shift_rows(u, GRID_W) + w[1:2] * u
                                + w[2:3] * _shift_rows(u, -GRID_W))


def _gconv(u, bg, w):
    bn, n, _ = u.shape
    seq_spec = pl.BlockSpec((1, n, LANES), lambda b, p: (b, 0, p))
    return pl.pallas_call(
        functools.partial(_gconv_kernel, n=n),
        grid=(bn, D_CONV // LANES),
        in_specs=[seq_spec, seq_spec, pl.BlockSpec((3, LANES), lambda b, p: (0, p))],
        out_specs=seq_spec,
        out_shape=jax.ShapeDtypeStruct((bn, n, D_CONV), F32),
        compiler_params=pltpu.CompilerParams(vmem_limit_bytes=VMEM_LIMIT),
        name="gconv",
    )(u, bg, w)


def _lane_max(x, mask):
    return jnp.max(jnp.where(mask, x, -jnp.inf), axis=-1, keepdims=True)


def _first_lane(cond, lane):
    return jnp.min(jnp.where(cond, lane, float(LANES)), axis=-1, keepdims=True)


def _outproj_kernel(x_ref, yr_ref, yc_ref, w_ref, mod_ref, g_ref, wr_hi_ref, wr_lo_ref, br_ref,
                    x1_ref, mt_ref, route_ref, cnt_ref, carry, *, tm):
    first = (pl.program_id(0) == 0) & (pl.program_id(1) == 0)

    @pl.when(first)
    def _():
        carry[...] = jnp.zeros_like(carry)

    mix = _dot(yr_ref[0].astype(BF16), w_ref[0:D_RNN, :]) + _dot(yc_ref[0].astype(BF16), w_ref[D_RNN:, :])
    x1 = x_ref[0] + mod_ref[0, 2:3, :] * mix
    x1_ref[0] = x1
    m = _norm_mod(x1, g_ref[...], mod_ref[0, 4:5, :], mod_ref[0, 3:4, :])
    for s in range(ROW_TILES):
        mt_ref[pl.ds(s, tm, stride=ROW_TILES), :] = m[:, s * LANES:(s + 1) * LANES]

    m_hi, m_lo = _split_bf16(m)
    logits = (_dot(m_hi, wr_hi_ref[...]) + _dot(m_lo, wr_hi_ref[...]) + _dot(m_hi, wr_lo_ref[...])
              + br_ref[...])
    lane_i = lax.broadcasted_iota(jnp.int32, logits.shape, 1)
    lane = lane_i.astype(F32)
    is_grp = lane_i < N_GROUPS
    g_max = _lane_max(logits, is_grp)
    grp = _first_lane(is_grp & (logits == g_max), lane)
    p_g = 1.0 / jnp.sum(jnp.where(is_grp, jnp.exp(logits - g_max), 0.0), axis=-1, keepdims=True)
    lo_lane = EXPERT_LANE0 + grp * EXPERTS_PER_GROUP
    in_grp = (lane >= lo_lane) & (lane < lo_lane + EXPERTS_PER_GROUP)
    l1 = _lane_max(logits, in_grp)
    i1 = _first_lane(in_grp & (logits == l1), lane)
    rest = in_grp & (lane != i1)
    l2 = _lane_max(logits, rest)
    i2 = _first_lane(rest & (logits == l2), lane)
    r21 = jnp.exp(l2 - l1)
    gate1 = p_g / (1.0 + r21)
    gate2 = gate1 * r21

    oh1 = jnp.where(lane == i1, 1.0, 0.0)
    oh2 = jnp.where(lane == i2, 1.0, 0.0)
    both = (oh1 + oh2).astype(BF16)
    ti = lax.broadcasted_iota(jnp.int32, (tm, tm), 0)
    tj = lax.broadcasted_iota(jnp.int32, (tm, tm), 1)
    tri = jnp.where(tj < ti, 1.0, 0.0).astype(BF16)
    before = _dot(tri, both) + carry[...]
    rank1 = jnp.sum(oh1 * before, axis=-1, keepdims=True)
    rank2 = jnp.sum(oh2 * before, axis=-1, keepdims=True)
    total = carry[...] + jnp.sum(oh1 + oh2, axis=0, keepdims=True)
    carry[...] = total
    cnt_ref[...] = total

    e1 = i1 - EXPERT_LANE0
    e2 = i2 - EXPERT_LANE0
    out = jnp.zeros(logits.shape, F32)
    for k, val in enumerate((e1, e2, gate1, gate2, rank1, rank2)):
        out = jnp.where(lane_i == k, val, out)
    route_ref[...] = out


def _outproj(x, y_rnn, y_conv, w_out_bf16, mod3, norm_g, wr_hi, wr_lo, br):
    bn, n, d = x.shape
    tm = min(TOK_TILE, n)
    nt = n // tm
    t_all = bn * n
    return pl.pallas_call(
        functools.partial(_outproj_kernel, tm=tm),
        grid=(bn, nt),
        in_specs=[
            pl.BlockSpec((1, tm, d), lambda b, i: (b, i, 0)),
            pl.BlockSpec((1, tm, D_RNN), lambda b, i: (b, i, 0)),
            pl.BlockSpec((1, tm, D_CONV), lambda b, i: (b, i, 0)),
            pl.BlockSpec((D_RNN + D_CONV, d), lambda b, i: (0, 0)),
            pl.BlockSpec((1, 6, d), lambda b, i: (b, 0, 0)),
            pl.BlockSpec((1, d), lambda b, i: (0, 0)),
            pl.BlockSpec((d, LANES), lambda b, i: (0, 0)),
            pl.BlockSpec((d, LANES), lambda b, i: (0, 0)),
            pl.BlockSpec((1, LANES), lambda b, i: (0, 0)),
        ],
        out_specs=[
            pl.BlockSpec((1, tm, d), lambda b, i: (b, i, 0)),
            pl.BlockSpec((tm * ROW_TILES, LANES), lambda b, i: (b * nt + i, 0)),
            pl.BlockSpec((tm, LANES), lambda b, i: (b * nt + i, 0)),
            pl.BlockSpec((1, LANES), lambda b, i: (0, 0)),
        ],
        out_shape=[
            jax.ShapeDtypeStruct((bn, n, d), F32),
            jax.ShapeDtypeStruct((t_all * ROW_TILES, LANES), F32),
            jax.ShapeDtypeStruct((t_all, LANES), F32),
            jax.ShapeDtypeStruct((1, LANES), F32),
        ],
        scratch_shapes=[pltpu.VMEM((1, LANES), F32)],
        compiler_params=pltpu.CompilerParams(
            dimension_semantics=("arbitrary", "arbitrary"), vmem_limit_bytes=VMEM_LIMIT),
        name="outproj",
    )(x, y_rnn, y_conv, w_out_bf16, mod3, norm_g.reshape(1, d), wr_hi, wr_lo, br)


def _row_tile(ref, row):
    return ref.at[pl.ds(pl.multiple_of(row * ROW_TILES, ROW_TILES), ROW_TILES)]


def _slotmap_kernel(dest_ref, asg_ref):
    def init(c, carry):
        for u in range(DMA_UNROLL):
            asg_ref[c * DMA_UNROLL + u] = 0
        return carry
    lax.fori_loop(0, asg_ref.shape[0] // DMA_UNROLL, init, 0)

    def body(c, carry):
        for u in range(DMA_UNROLL):
            a = c * DMA_UNROLL + u
            asg_ref[dest_ref[a]] = a
        return carry
    lax.fori_loop(0, dest_ref.shape[0] // DMA_UNROLL, body, 0)


def _slotmap(dest, n_slots):
    return pl.pallas_call(
        _slotmap_kernel,
        in_specs=[pl.BlockSpec(memory_space=pltpu.SMEM)],
        out_specs=pl.BlockSpec(memory_space=pltpu.SMEM),
        out_shape=jax.ShapeDtypeStruct((n_slots,), jnp.int32),
        name="slotmap",
    )(dest)


def _expert_kernel(be_ref, nu_ref, asg_ref, mt_hbm, wg_ref, wu_ref, wd_ref, yb_ref,
                   xbuf, sems, wg_s, wu_s, wd_s):
    j = pl.program_id(0)
    n_used = nu_ref[0]

    def row_copy(tok, slot, r):
        return pltpu.make_async_copy(_row_tile(mt_hbm, tok), _row_tile(xbuf.at[slot], r), sems.at[slot])

    def issue_rows(blk, slot, r0, n):
        for u in range(n):
            tok = lax.shift_right_logical(asg_ref[blk * MOE_BLK + r0 + u], 1)
            row_copy(tok, slot, r0 + u).start(priority=u % 2)

    def drain(slot):
        def body(c, carry):
            for u in range(DMA_UNROLL):
                row_copy(0, slot, 0).wait()
            return carry
        lax.fori_loop(0, MOE_BLK // DMA_UNROLL, body, 0)

    @pl.when(j >= n_used)
    def _():
        yb_ref[...] = jnp.zeros_like(yb_ref)

    @pl.when(j < n_used)
    def _():
        slot = j % 2

        @pl.when(j == 0)
        def _():
            def body(c, carry):
                issue_rows(0, 0, c * DMA_UNROLL, DMA_UNROLL)
                return carry
            lax.fori_loop(0, MOE_BLK // DMA_UNROLL, body, 0)

        e = be_ref[j]
        prev = be_ref[jnp.maximum(j - 1, 0)]

        @pl.when((j == 0) | (e != prev))
        def _():
            wg_s[...] = wg_ref[0].astype(BF16)
            wu_s[...] = wu_ref[0].astype(BF16)
            wd_s[...] = wd_ref[0].astype(BF16)

        drain(slot)

        nxt = jnp.minimum(j + 1, n_used - 1)
        piece = MOE_BLK // EXPERT_PIECES

        x = jnp.concatenate(
            [xbuf[slot, pl.ds(s, MOE_BLK, stride=ROW_TILES), :] for s in range(ROW_TILES)], axis=-1)
        xb16 = x.astype(BF16)
        half = D_EXPERT // 2
        acts = []
        for q in range(2):
            gate = _dot(xb16, wg_s[:, q * half:(q + 1) * half])
            issue_rows(nxt, 1 - slot, (2 * q) * piece, piece)
            up = _dot(xb16, wu_s[:, q * half:(q + 1) * half])
            issue_rows(nxt, 1 - slot, (2 * q + 1) * piece, piece)
            acts.append(((gate * jax.nn.sigmoid(gate)) * up).astype(BF16))
        h = jnp.concatenate(acts, axis=-1)
        cols = D_MODEL // 4
        for q in range(4):
            y = _dot(h, wd_s[:, q * cols:(q + 1) * cols])
            issue_rows(nxt, 1 - slot, (4 + q) * piece, piece)
            for s in range(cols // LANES):
                yb_ref[pl.ds(q * (cols // LANES) + s, MOE_BLK, stride=ROW_TILES), :] = (
                    y[:, s * LANES:(s + 1) * LANES])

        @pl.when(j == n_used - 1)
        def _():
            drain(1 - slot)


def _experts(block_e, n_used, slot_asg, mt, w_gate, w_up, w_down, n_blocks):
    def wsel(j, be, nu, asg):
        return (be[jnp.minimum(j, nu[0] - 1)], 0, 0)

    return pl.pallas_call(
        _expert_kernel,
        grid_spec=pltpu.PrefetchScalarGridSpec(
            num_scalar_prefetch=3,
            grid=(n_blocks,),
            in_specs=[
                pl.BlockSpec(memory_space=pl.ANY),
                pl.BlockSpec((1, D_MODEL, D_EXPERT), wsel),
                pl.BlockSpec((1, D_MODEL, D_EXPERT), wsel),
                pl.BlockSpec((1, D_EXPERT, D_MODEL), wsel),
            ],
            out_specs=pl.BlockSpec((MOE_BLK * ROW_TILES, LANES), lambda j, be, nu, asg: (j, 0)),
            scratch_shapes=[
                pltpu.VMEM((2, MOE_BLK * ROW_TILES, LANES), F32),
                pltpu.SemaphoreType.DMA((2,)),
                pltpu.VMEM((D_MODEL, D_EXPERT), BF16),
                pltpu.VMEM((D_MODEL, D_EXPERT), BF16),
                pltpu.VMEM((D_EXPERT, D_MODEL), BF16),
            ],
        ),
        out_shape=jax.ShapeDtypeStruct((n_blocks * MOE_BLK * ROW_TILES, LANES), F32),
        compiler_params=pltpu.CompilerParams(
            dimension_semantics=("arbitrary",), vmem_limit_bytes=VMEM_LIMIT),
        name="expert",
    )(block_e, n_used, slot_asg, mt, w_gate, w_up, w_down)


def _combine_kernel(dest_ref, yb_hbm, x1_ref, route_ref, mod_ref, g_ref, o_ref, ybuf, sems, *, tc):
    i = pl.program_id(0)
    slot = i % 2

    def row_copy(d, sl, k, r):
        return pltpu.make_async_copy(_row_tile(yb_hbm, d), _row_tile(ybuf.at[sl, k], r), sems.at[sl])

    def gather(step, sl):
        def issue(c, carry):
            for u in range(DMA_UNROLL):
                r = c * DMA_UNROLL + u
                for k in range(TOP_K):
                    row_copy(dest_ref[TOP_K * (step * tc + r) + k], sl, k, r).start(priority=k)
            return carry
        lax.fori_loop(0, tc // DMA_UNROLL, issue, 0)

    @pl.when(i == 0)
    def _():
        gather(0, 0)

    @pl.when(i + 1 < pl.num_programs(0))
    def _():
        gather(i + 1, 1 - slot)

    def drain(c, carry):
        for u in range(DMA_UNROLL * TOP_K):
            row_copy(0, slot, 0, 0).wait()
        return carry
    lax.fori_loop(0, tc // DMA_UNROLL, drain, 0)

    def rows(k):
        return jnp.concatenate(
            [ybuf[slot, k, pl.ds(s, tc, stride=ROW_TILES), :] for s in range(ROW_TILES)], axis=-1)

    route = route_ref[...]
    y = route[:, 2:3] * rows(0) + route[:, 3:4] * rows(1)
    x2 = x1_ref[...] + mod_ref[0, 5:6, :] * y
    ms = jnp.mean(x2 * x2, axis=-1, keepdims=True)
    o_ref[...] = x2 * lax.rsqrt(ms + NORM_EPS) * g_ref[...]


def _combine(dest, yb, x1_2d, route, mod3, final_g, seq):
    t_all, d = x1_2d.shape
    tc = min(COMBINE_TILE, seq)
    per_seq = seq // tc
    return pl.pallas_call(
        functools.partial(_combine_kernel, tc=tc),
        grid_spec=pltpu.PrefetchScalarGridSpec(
            num_scalar_prefetch=1,
            grid=(t_all // tc,),
            in_specs=[
                pl.BlockSpec(memory_space=pl.ANY),
                pl.BlockSpec((tc, d), lambda i, dest: (i, 0)),
                pl.BlockSpec((tc, LANES), lambda i, dest: (i, 0)),
                pl.BlockSpec((1, 6, d), lambda i, dest: (i // per_seq, 0, 0)),
                pl.BlockSpec((1, d), lambda i, dest: (0, 0)),
            ],
            out_specs=pl.BlockSpec((tc, d), lambda i, dest: (i, 0)),
            scratch_shapes=[
                pltpu.VMEM((2, TOP_K, tc * ROW_TILES, LANES), F32),
                pltpu.SemaphoreType.DMA((2,)),
            ],
        ),
        out_shape=jax.ShapeDtypeStruct((t_all, d), F32),
        compiler_params=pltpu.CompilerParams(
            dimension_semantics=("arbitrary",), vmem_limit_bytes=VMEM_LIMIT),
        name="combine",
    )(dest, yb, x1_2d, route, mod3, final_g.reshape(1, d))


def kernel(x, c, ctx, c_ctx, ada_w, ada_b, norm1_g, norm2_g, w_in, rnn_conv_w, rnn_conv_b, rg_wa, rg_ba,
           rg_wx, rg_bx, rg_lambda, sc_conv_w, w_out, router_group_w, router_group_b, router_exp_w,
           router_exp_b, exp_w_gate, exp_w_up, exp_w_down, final_norm_g):
    bn, seq, d = x.shape
    assert d == D_MODEL and bn < MOD_ROWS and ada_w.shape[0] == 1
    t_all = bn * seq

    cc = jnp.concatenate([c, c_ctx[None], jnp.zeros((MOD_ROWS - bn - 1, d), F32)], axis=0)
    mod3 = _modulation(cc, ada_w[0], ada_b[0]).reshape(MOD_ROWS, 6, d)

    w_in_b = w_in[0].astype(BF16)
    xr, gr, u, bg = _inproj(x, mod3, None, norm1_g[0], w_in_b, latent=True)
    (xr_c,) = _inproj(ctx, mod3, bn, norm1_g[0], w_in_b[:, :D_RNN], latent=False)

    wg, bgate = _gate_weights(rg_wa[0], rg_ba[0], rg_wx[0], rg_bx[0])
    y_rnn = _rnn(xr, xr_c, gr, rnn_conv_w[0], rnn_conv_b[0], wg, bgate, rg_lambda[0])
    y_conv = _gconv(u, bg, sc_conv_w[0])

    wr = jnp.zeros((d, LANES), F32)
    wr = wr.at[:, :N_GROUPS].set(router_group_w[0]).at[:, EXPERT_LANE0:EXPERT_LANE0 + N_EXPERTS].set(router_exp_w[0])
    br = jnp.zeros((1, LANES), F32)
    br = br.at[0, :N_GROUPS].set(router_group_b[0]).at[0, EXPERT_LANE0:EXPERT_LANE0 + N_EXPERTS].set(router_exp_b[0])
    wr_hi, wr_lo = _split_bf16(wr)
    x1, mt, route, cnt = _outproj(x, y_rnn, y_conv, w_out[0].astype(BF16), mod3, norm2_g[0], wr_hi, wr_lo, br)

    n_assign = t_all * TOP_K
    n_blocks = (n_assign + N_EXPERTS * (MOE_BLK - 1) + MOE_BLK - 1) // MOE_BLK
    counts = cnt[0, EXPERT_LANE0:EXPERT_LANE0 + N_EXPERTS].astype(jnp.int32)
    pcounts = (counts + MOE_BLK - 1) // MOE_BLK * MOE_BLK
    pends = jnp.cumsum(pcounts)
    pstarts = pends - pcounts
    experts = route[:, 0:TOP_K].astype(jnp.int32)
    ranks = route[:, 4:4 + TOP_K].astype(jnp.int32)
    onehot = experts[:, :, None] == jnp.arange(N_EXPERTS, dtype=jnp.int32)
    dest = (ranks + jnp.sum(jnp.where(onehot, pstarts, 0), axis=-1)).reshape(n_assign)
    n_used = (pends[-1] // MOE_BLK).astype(jnp.int32)
    blk_start = jnp.arange(n_blocks, dtype=jnp.int32) * MOE_BLK
    block_e = jnp.minimum(jnp.sum(blk_start[:, None] >= pends[None, :], axis=1), N_EXPERTS - 1)
    last_e = jnp.max(jnp.where(counts > 0, jnp.arange(N_EXPERTS, dtype=jnp.int32), 0))
    block_e = jnp.where(blk_start < pends[-1], block_e, last_e).astype(jnp.int32)

    n_slots = n_blocks * MOE_BLK
    slot_asg = _slotmap(dest, n_slots)
    yb = _experts(block_e, n_used.reshape(1), slot_asg, mt, exp_w_gate[0], exp_w_up[0], exp_w_down[0],
                  n_blocks)
    out = _combine(dest, yb, x1.reshape(t_all, d), route, mod3, final_norm_g, seq)
    return out.reshape(bn, seq, d)
```

```python
import functools

import jax
import jax.numpy as jnp
from jax import lax
from jax.experimental import pallas as pl
from jax.experimental.pallas import tpu as pltpu

F32 = jnp.float32
BF16 = jnp.bfloat16

D_MODEL = 1024
D_RNN = 512
D_CONV = 512
D_CONV_H = D_CONV // 2
RNN_HEADS = 8
RNN_HEAD_DIM = D_RNN // RNN_HEADS
GRID_W = 64
RG_C = 8.0
N_GROUPS = 4
EXPERTS_PER_GROUP = 8
N_EXPERTS = N_GROUPS * EXPERTS_PER_GROUP
TOP_K = 2
D_EXPERT = 512
NORM_EPS = 1e-6

LANES = 128
SUBLANES = 8
ROW_TILES = D_MODEL // LANES
N_LANE_GROUPS = D_RNN // LANES
EXPERT_LANE0 = N_GROUPS

MOD_ROWS = 16
MOD_TN = 768
TOK_TILE = 512
MOE_BLK = 256
EXPERT_PIECES = 8
COMBINE_TILE = 256
DMA_UNROLL = 16
VMEM_LIMIT = 48 * 1024 * 1024


def _dot(a, b):
    return jnp.dot(a, b, preferred_element_type=F32)


def _split_bf16(x):
    hi = x.astype(BF16)
    lo = (x - hi.astype(F32)).astype(BF16)
    return hi, lo


def _mod_kernel(cc_ref, w_ref, b_ref, o_ref):
    s = cc_ref[...]
    s = s * jax.nn.sigmoid(s)
    s_hi, s_lo = _split_bf16(s)
    w_hi, w_lo = _split_bf16(w_ref[...])
    o_ref[...] = _dot(s_hi, w_hi) + _dot(s_lo, w_hi) + _dot(s_hi, w_lo) + b_ref[...]


def _modulation(cc, ada_w, ada_b):
    n = ada_w.shape[1]
    return pl.pallas_call(
        _mod_kernel,
        grid=(n // MOD_TN,),
        in_specs=[
            pl.BlockSpec((MOD_ROWS, D_MODEL), lambda j: (0, 0)),
            pl.BlockSpec((D_MODEL, MOD_TN), lambda j: (0, j)),
            pl.BlockSpec((1, MOD_TN), lambda j: (0, j)),
        ],
        out_specs=pl.BlockSpec((MOD_ROWS, MOD_TN), lambda j: (0, j)),
        out_shape=jax.ShapeDtypeStruct((MOD_ROWS, n), F32),
        compiler_params=pltpu.CompilerParams(vmem_limit_bytes=VMEM_LIMIT),
        name="mod",
    )(cc, ada_w, ada_b.reshape(1, n))


def _norm_mod(x, g, scale, shift):
    ms = jnp.mean(x * x, axis=-1, keepdims=True)
    y = x * lax.rsqrt(ms + NORM_EPS) * g
    return y * (1.0 + scale) + shift


def _inproj_kernel(x_ref, mod_ref, g_ref, w_ref, *out_refs, latent):
    h = _norm_mod(x_ref[0], g_ref[...], mod_ref[0, 1:2, :], mod_ref[0, 0:1, :])
    hb = h.astype(BF16)
    xr = _dot(hb, w_ref[:, 0:D_RNN])
    out_refs[0][0] = xr
    if latent:
        o = D_RNN
        out_refs[1][0] = _dot(hb, w_ref[:, o:o + D_RNN])
        o += D_RNN
        v = _dot(hb, w_ref[:, o:o + D_CONV])
        out_refs[3][0] = _dot(hb, w_ref[:, o + D_CONV:o + 2 * D_CONV])
        cg = _dot(hb, w_ref[:, o + 2 * D_CONV:o + 3 * D_CONV])
        out_refs[2][0] = cg * v


def _inproj(x, mod3, mod_row, norm_g, w_bf16, latent):
    bn, n, d = x.shape
    tm = min(TOK_TILE, n)
    n_out = 4 if latent else 1
    width = w_bf16.shape[1]
    mod_map = (lambda b, i: (b, 0, 0)) if mod_row is None else (lambda b, i: (mod_row, 0, 0))
    return pl.pallas_call(
        functools.partial(_inproj_kernel, latent=latent),
        grid=(bn, n // tm),
        in_specs=[
            pl.BlockSpec((1, tm, d), lambda b, i: (b, i, 0)),
            pl.BlockSpec((1, 6, d), mod_map),
            pl.BlockSpec((1, d), lambda b, i: (0, 0)),
            pl.BlockSpec((d, width), lambda b, i: (0, 0)),
        ],
        out_specs=[pl.BlockSpec((1, tm, D_RNN), lambda b, i: (b, i, 0))] * n_out,
        out_shape=[jax.ShapeDtypeStruct((bn, n, D_RNN), F32)] * n_out,
        compiler_params=pltpu.CompilerParams(vmem_limit_bytes=VMEM_LIMIT),
        name="inproj_lat" if latent else "inproj_ctx",
    )(x, mod3, norm_g.reshape(1, d), w_bf16)


def _shift_rows(x, k):
    n = x.shape[0]
    row = lax.broadcasted_iota(jnp.int32, x.shape, 0)
    rolled = pltpu.roll(x, k % n, axis=0)
    valid = (row >= k) if k > 0 else (row < n + k)
    return jnp.where(valid, rolled, 0.0)


def _round8(v):
    return (v + SUBLANES - 1) // SUBLANES * SUBLANES


def _scan_work_rows(n):
    rows = 0
    while n > SUBLANES:
        g = n // SUBLANES
        rows += 2 * n + 2 * _round8(g) + _round8(g) + 2 * SUBLANES
        n = g
    return rows


def _linear_scan(a_ref, a_off, b_ref, b_off, h_ref, h_off, n, h0, reverse, work, w_off):
    if n <= SUBLANES:
        h = h0
        for r in (range(n - 1, -1, -1) if reverse else range(n)):
            h = a_ref[pl.ds(a_off + r, 1), :] * h + b_ref[pl.ds(b_off + r, 1), :]
            h_ref[pl.ds(h_off + r, 1), :] = h
        return
    g = n // SUBLANES
    gp = _round8(g)
    ca_off = w_off
    cb_off = ca_off + n
    a2_off = cb_off + n
    b2_off = a2_off + gp
    hs_off = b2_off + gp
    next_off = hs_off + gp + 2 * SUBLANES

    a = a_ref[pl.ds(a_off, n), :].reshape(g, SUBLANES, LANES)
    b = b_ref[pl.ds(b_off, n), :].reshape(g, SUBLANES, LANES)
    row = lax.broadcasted_iota(jnp.int32, (g, SUBLANES, LANES), 1)
    for s in (1, 2, 4):
        if reverse:
            a_s = pltpu.roll(a, SUBLANES - s, axis=1)
            b_s = pltpu.roll(b, SUBLANES - s, axis=1)
            valid = row < SUBLANES - s
        else:
            a_s = pltpu.roll(a, s, axis=1)
            b_s = pltpu.roll(b, s, axis=1)
            valid = row >= s
        b = jnp.where(valid, a * b_s + b, b)
        a = jnp.where(valid, a * a_s, a)
    work[pl.ds(ca_off, n), :] = a.reshape(n, LANES)
    work[pl.ds(cb_off, n), :] = b.reshape(n, LANES)

    last = 0 if reverse else SUBLANES - 1
    work[pl.ds(a2_off, g), :] = work[pl.ds(ca_off + last, g, stride=SUBLANES), :]
    work[pl.ds(b2_off, g), :] = work[pl.ds(cb_off + last, g, stride=SUBLANES), :]
    _linear_scan(work, a2_off, work, b2_off, work, hs_off + SUBLANES, g, h0, reverse, work, next_off)
    if reverse:
        work[pl.ds(hs_off + SUBLANES + g, 1), :] = h0
        in_off = hs_off + SUBLANES + 1
    else:
        work[pl.ds(hs_off + SUBLANES - 1, 1), :] = h0
        in_off = hs_off + SUBLANES - 1

    def apply_group(gi, r0):
        h_in = work[pl.ds(in_off + gi, 1), :]
        h_ref[pl.ds(h_off + r0, SUBLANES), :] = (
            work[pl.ds(ca_off + r0, SUBLANES), :] * h_in + work[pl.ds(cb_off + r0, SUBLANES), :])

    if g <= 32:
        for gi in range(g):
            apply_group(gi, gi * SUBLANES)
    else:
        def body(go, carry):
            for u in range(SUBLANES):
                gi = go * SUBLANES + u
                apply_group(gi, pl.multiple_of(gi * SUBLANES, SUBLANES))
            return carry
        lax.fori_loop(0, g // SUBLANES, body, 0)


def _rnn_kernel(xr_ref, xrc_ref, gr_ref, cw_ref, cb_ref, wg_ref, bg_ref, lam_ref, y_ref,
                a_s, b_s, h_s, hsum_s, work, *, n_lat, n_ctx):
    nl = -lam_ref[...]
    sp = jnp.maximum(nl, 0.0) + jnp.log1p(jnp.exp(-jnp.abs(nl)))
    cw = cw_ref[...]
    bias = cb_ref[...]
    wg = wg_ref[0]
    bg = bg_ref[0]

    def conv_gates(x):
        xc = (cw[0:1] * _shift_rows(x, 2) + cw[1:2] * _shift_rows(x, 1) + cw[2:3] * x
              + cw[3:4] * _shift_rows(x, -1)) + bias
        return xc, _dot(xc.astype(BF16), wg) + bg

    def coeffs(xc, gates, d):
        r = jax.nn.sigmoid(gates[:, (2 * d) * LANES:(2 * d + 1) * LANES])
        i = jax.nn.sigmoid(gates[:, (2 * d + 1) * LANES:(2 * d + 2) * LANES])
        log_a = (-RG_C * r) * sp[d:d + 1]
        a = jnp.exp(log_a)
        b = jnp.sqrt(-jnp.tanh(log_a) * (a * a + 1.0)) * (i * xc)
        return a, b

    zero = jnp.zeros((1, LANES), F32)
    xc_c, gates_c = conv_gates(xrc_ref[0])
    h0 = []
    for d in range(2):
        a, b = coeffs(xc_c, gates_c, d)
        a_s[pl.ds(0, n_ctx), :] = a
        b_s[pl.ds(0, n_ctx), :] = b
        _linear_scan(a_s, 0, b_s, 0, h_s, 0, n_ctx, zero, d == 1, work, 0)
        h0.append(h_s[pl.ds(0 if d == 1 else n_ctx - 1, 1), :])

    xc_l, gates_l = conv_gates(xr_ref[0])
    for d in range(2):
        a, b = coeffs(xc_l, gates_l, d)
        a_s[...] = a
        b_s[...] = b
        _linear_scan(a_s, 0, b_s, 0, h_s if d == 1 else hsum_s, 0, n_lat, h0[d], d == 1, work, 0)
    y_ref[0] = jax.nn.gelu(gr_ref[0], approximate=True) * (hsum_s[...] + h_s[...])


def _rnn(xr, xr_c, gr, conv_w, conv_b, wg, bgate, lam):
    bn, n, _ = xr.shape
    n_ctx = xr_c.shape[1]
    seq_spec = pl.BlockSpec((1, n, LANES), lambda b, p: (b, 0, p))
    return pl.pallas_call(
        functools.partial(_rnn_kernel, n_lat=n, n_ctx=n_ctx),
        grid=(bn, N_LANE_GROUPS),
        in_specs=[
            seq_spec,
            pl.BlockSpec((1, n_ctx, LANES), lambda b, p: (b, 0, p)),
            seq_spec,
            pl.BlockSpec((4, LANES), lambda b, p: (0, p)),
            pl.BlockSpec((1, LANES), lambda b, p: (0, p)),
            pl.BlockSpec((1, LANES, 4 * LANES), lambda b, p: (p, 0, 0)),
            pl.BlockSpec((1, 1, 4 * LANES), lambda b, p: (p, 0, 0)),
            pl.BlockSpec((2, LANES), lambda b, p: (0, p)),
        ],
        out_specs=seq_spec,
        out_shape=jax.ShapeDtypeStruct((bn, n, D_RNN), F32),
        scratch_shapes=[pltpu.VMEM((n, LANES), F32)] * 4
        + [pltpu.VMEM((_scan_work_rows(n), LANES), F32)],
        compiler_params=pltpu.CompilerParams(vmem_limit_bytes=VMEM_LIMIT),
        name="rnn",
    )(xr, xr_c, gr, conv_w, conv_b.reshape(1, D_RNN), wg, bgate, lam)


def _gate_weights(rg_wa, rg_ba, rg_wx, rg_bx):
    eye = jnp.eye(2, dtype=F32)
    blocks, biases = [], []
    for d in range(2):
        for w, bvec in ((rg_wa[d], rg_ba[d]), (rg_wx[d], rg_bx[d])):
            w4 = w.reshape(N_LANE_GROUPS, 2, RNN_HEAD_DIM, RNN_HEAD_DIM)
            bd = jnp.einsum("paij,ac->paicj", w4, eye).reshape(N_LANE_GROUPS, LANES, LANES)
            blocks.append(bd)
            biases.append(bvec.reshape(N_LANE_GROUPS, 1, LANES))
    return jnp.concatenate(blocks, axis=-1).astype(BF16), jnp.concatenate(biases, axis=-1)


def _gconv_kernel(u_ref, bg_ref, w_ref, y_ref, *, n):
    p = pl.program_id(1)
    u = u_ref[0]
    w = w_ref[...]

    @pl.when(p < D_CONV_H // LANES)
    def _():
        col = lax.broadcasted_iota(jnp.int32, u.shape, 0) % GRID_W
        left = jnp.where(col > 0, _shift_rows(u, 1), 0.0)
        right = jnp.where(col < GRID_W - 1, _shift_rows(u, -1), 0.0)
        y_ref[0] = bg_ref[0] * (w[0:1] * left + w[1:2] * u + w[2:3] * right)

    @pl.when(p >= D_CONV_H // LANES)
    def _():
        y_ref[0] = bg_ref[0] * (w[0:1] * _shift_rows(u, GRID_W) + w[1:2] * u
                                + w[2:3] * _shift_rows(u, -GRID_W))


def _gconv(u, bg, w):
    bn, n, _ = u.shape
    seq_spec = pl.BlockSpec((1, n, LANES), lambda b, p: (b, 0, p))
    return pl.pallas_call(
        functools.partial(_gconv_kernel, n=n),
        grid=(bn, D_CONV // LANES),
        in_specs=[seq_spec, seq_spec, pl.BlockSpec((3, LANES), lambda b, p: (0, p))],
        out_specs=seq_spec,
        out_shape=jax.ShapeDtypeStruct((bn, n, D_CONV), F32),
        compiler_params=pltpu.CompilerParams(vmem_limit_bytes=VMEM_LIMIT),
        name="gconv",
    )(u, bg, w)


def _lane_max(x, mask):
    return jnp.max(jnp.where(mask, x, -jnp.inf), axis=-1, keepdims=True)


def _first_lane(cond, lane):
    return jnp.min(jnp.where(cond, lane, float(LANES)), axis=-1, keepdims=True)


def _outproj_kernel(x_ref, yr_ref, yc_ref, w_ref, mod_ref, g_ref, wr_hi_ref, wr_lo_ref, br_ref,
                    x1_ref, mt_ref, route_ref, cnt_ref, carry, *, tm):
    first = (pl.program_id(0) == 0) & (pl.program_id(1) == 0)

    @pl.when(first)
    def _():
        carry[...] = jnp.zeros_like(carry)

    mix = _dot(yr_ref[0].astype(BF16), w_ref[0:D_RNN, :]) + _dot(yc_ref[0].astype(BF16), w_ref[D_RNN:, :])
    x1 = x_ref[0] + mod_ref[0, 2:3, :] * mix
    x1_ref[0] = x1
    m = _norm_mod(x1, g_ref[...], mod_ref[0, 4:5, :], mod_ref[0, 3:4, :])
    for s in range(ROW_TILES):
        mt_ref[pl.ds(s, tm, stride=ROW_TILES), :] = m[:, s * LANES:(s + 1) * LANES]

    m_hi, m_lo = _split_bf16(m)
    logits = (_dot(m_hi, wr_hi_ref[...]) + _dot(m_lo, wr_hi_ref[...]) + _dot(m_hi, wr_lo_ref[...])
              + br_ref[...])
    lane_i = lax.broadcasted_iota(jnp.int32, logits.shape, 1)
    lane = lane_i.astype(F32)
    is_grp = lane_i < N_GROUPS
    g_max = _lane_max(logits, is_grp)
    grp = _first_lane(is_grp & (logits == g_max), lane)
    p_g = 1.0 / jnp.sum(jnp.where(is_grp, jnp.exp(logits - g_max), 0.0), axis=-1, keepdims=True)
    lo_lane = EXPERT_LANE0 + grp * EXPERTS_PER_GROUP
    in_grp = (lane >= lo_lane) & (lane < lo_lane + EXPERTS_PER_GROUP)
    l1 = _lane_max(logits, in_grp)
    i1 = _first_lane(in_grp & (logits == l1), lane)
    rest = in_grp & (lane != i1)
    l2 = _lane_max(logits, rest)
    i2 = _first_lane(rest & (logits == l2), lane)
    r21 = jnp.exp(l2 - l1)
    gate1 = p_g / (1.0 + r21)
    gate2 = gate1 * r21

    oh1 = jnp.where(lane == i1, 1.0, 0.0)
    oh2 = jnp.where(lane == i2, 1.0, 0.0)
    both = (oh1 + oh2).astype(BF16)
    ti = lax.broadcasted_iota(jnp.int32, (tm, tm), 0)
    tj = lax.broadcasted_iota(jnp.int32, (tm, tm), 1)
    tri = jnp.where(tj < ti, 1.0, 0.0).astype(BF16)
    before = _dot(tri, both) + carry[...]
    rank1 = jnp.sum(oh1 * before, axis=-1, keepdims=True)
    rank2 = jnp.sum(oh2 * before, axis=-1, keepdims=True)
    total = carry[...] + jnp.sum(oh1 + oh2, axis=0, keepdims=True)
    carry[...] = total
    cnt_ref[...] = total

    e1 = i1 - EXPERT_LANE0
    e2 = i2 - EXPERT_LANE0
    out = jnp.zeros(logits.shape, F32)
    for k, val in enumerate((e1, e2, gate1, gate2, rank1, rank2)):
        out = jnp.where(lane_i == k, val, out)
    route_ref[...] = out


def _outproj(x, y_rnn, y_conv, w_out_bf16, mod3, norm_g, wr_hi, wr_lo, br):
    bn, n, d = x.shape
    tm = min(TOK_TILE, n)
    nt = n // tm
    t_all = bn * n
    return pl.pallas_call(
        functools.partial(_outproj_kernel, tm=tm),
        grid=(bn, nt),
        in_specs=[
            pl.BlockSpec((1, tm, d), lambda b, i: (b, i, 0)),
            pl.BlockSpec((1, tm, D_RNN), lambda b, i: (b, i, 0)),
            pl.BlockSpec((1, tm, D_CONV), lambda b, i: (b, i, 0)),
            pl.BlockSpec((D_RNN + D_CONV, d), lambda b, i: (0, 0)),
            pl.BlockSpec((1, 6, d), lambda b, i: (b, 0, 0)),
            pl.BlockSpec((1, d), lambda b, i: (0, 0)),
            pl.BlockSpec((d, LANES), lambda b, i: (0, 0)),
            pl.BlockSpec((d, LANES), lambda b, i: (0, 0)),
            pl.BlockSpec((1, LANES), lambda b, i: (0, 0)),
        ],
        out_specs=[
            pl.BlockSpec((1, tm, d), lambda b, i: (b, i, 0)),
            pl.BlockSpec((tm * ROW_TILES, LANES), lambda b, i: (b * nt + i, 0)),
            pl.BlockSpec((tm, LANES), lambda b, i: (b * nt + i, 0)),
            pl.BlockSpec((1, LANES), lambda b, i: (0, 0)),
        ],
        out_shape=[
            jax.ShapeDtypeStruct((bn, n, d), F32),
            jax.ShapeDtypeStruct((t_all * ROW_TILES, LANES), F32),
            jax.ShapeDtypeStruct((t_all, LANES), F32),
            jax.ShapeDtypeStruct((1, LANES), F32),
        ],
        scratch_shapes=[pltpu.VMEM((1, LANES), F32)],
        compiler_params=pltpu.CompilerParams(
            dimension_semantics=("arbitrary", "arbitrary"), vmem_limit_bytes=VMEM_LIMIT),
        name="outproj",
    )(x, y_rnn, y_conv, w_out_bf16, mod3, norm_g.reshape(1, d), wr_hi, wr_lo, br)


def _row_tile(ref, row):
    return ref.at[pl.ds(pl.multiple_of(row * ROW_TILES, ROW_TILES), ROW_TILES)]


def _slotmap_kernel(dest_ref, asg_ref):
    def init(c, carry):
        for u in range(DMA_UNROLL):
            asg_ref[c * DMA_UNROLL + u] = 0
        return carry
    lax.fori_loop(0, asg_ref.shape[0] // DMA_UNROLL, init, 0)

    def body(c, carry):
        for u in range(DMA_UNROLL):
            a = c * DMA_UNROLL + u
            asg_ref[dest_ref[a]] = a
        return carry
    lax.fori_loop(0, dest_ref.shape[0] // DMA_UNROLL, body, 0)


def _slotmap(dest, n_slots):
    return pl.pallas_call(
        _slotmap_kernel,
        in_specs=[pl.BlockSpec(memory_space=pltpu.SMEM)],
        out_specs=pl.BlockSpec(memory_space=pltpu.SMEM),
        out_shape=jax.ShapeDtypeStruct((n_slots,), jnp.int32),
        name="slotmap",
    )(dest)


def _expert_kernel(be_ref, nu_ref, asg_hbm, mt_hbm, wg_ref, wu_ref, wd_ref, yb_ref,
                   xbuf, sems, idx, isems, wg_s, wu_s, wd_s):
    j = pl.program_id(0)
    n_used = nu_ref[0]
    last = n_used - 1

    def idx_copy(blk, sl):
        return pltpu.make_async_copy(asg_hbm.at[blk], idx.at[sl], isems.at[sl])

    def row_copy(tok, slot, r):
        return pltpu.make_async_copy(_row_tile(mt_hbm, tok), _row_tile(xbuf.at[slot], r), sems.at[slot])

    def issue_rows(slot, r0, n):
        for u in range(n):
            tok = lax.shift_right_logical(idx[slot, r0 + u], 1)
            row_copy(tok, slot, r0 + u).start(priority=u % 2)

    def drain(slot):
        def body(c, carry):
            for u in range(DMA_UNROLL):
                row_copy(0, slot, 0).wait()
            return carry
        lax.fori_loop(0, MOE_BLK // DMA_UNROLL, body, 0)

    @pl.when(j >= n_used)
    def _():
        yb_ref[...] = jnp.zeros_like(yb_ref)

    @pl.when(j < n_used)
    def _():
        slot = j % 2
        other = 1 - slot

        @pl.when(j == 0)
        def _():
            idx_copy(0, 0).start()
            idx_copy(0, 0).wait()

            def body(c, carry):
                issue_rows(0, c * DMA_UNROLL, DMA_UNROLL)
                return carry
            lax.fori_loop(0, MOE_BLK // DMA_UNROLL, body, 0)
            idx_copy(jnp.minimum(1, last), 1).start()

        e = be_ref[j]
        prev = be_ref[jnp.maximum(j - 1, 0)]

        @pl.when((j == 0) | (e != prev))
        def _():
            wg_s[...] = wg_ref[0].astype(BF16)
            wu_s[...] = wu_ref[0].astype(BF16)
            wd_s[...] = wd_ref[0].astype(BF16)

        drain(slot)
        idx_copy(0, other).wait()

        piece = MOE_BLK // EXPERT_PIECES

        x = jnp.concatenate(
            [xbuf[slot, pl.ds(s, MOE_BLK, stride=ROW_TILES), :] for s in range(ROW_TILES)], axis=-1)
        xb16 = x.astype(BF16)
        half = D_EXPERT // 2
        acts = []
        for q in range(2):
            gate = _dot(xb16, wg_s[:, q * half:(q + 1) * half])
            issue_rows(other, (2 * q) * piece, piece)
            up = _dot(xb16, wu_s[:, q * half:(q + 1) * half])
            issue_rows(other, (2 * q + 1) * piece, piece)
            acts.append(((gate * jax.nn.sigmoid(gate)) * up).astype(BF16))
        h = jnp.concatenate(acts, axis=-1)
        cols = D_MODEL // 4
        for q in range(4):
            y = _dot(h, wd_s[:, q * cols:(q + 1) * cols])
            issue_rows(other, (4 + q) * piece, piece)
            for s in range(cols // LANES):
                yb_ref[pl.ds(q * (cols // LANES) + s, MOE_BLK, stride=ROW_TILES), :] = (
                    y[:, s * LANES:(s + 1) * LANES])

        @pl.when(j < last)
        def _():
            idx_copy(jnp.minimum(j + 2, last), slot).start()

        @pl.when(j == last)
        def _():
            drain(other)


def _experts(block_e, n_used, slot_asg, mt, w_gate, w_up, w_down, n_blocks):
    def wsel(j, be, nu):
        return (be[jnp.minimum(j, nu[0] - 1)], 0, 0)

    return pl.pallas_call(
        _expert_kernel,
        grid_spec=pltpu.PrefetchScalarGridSpec(
            num_scalar_prefetch=2,
            grid=(n_blocks,),
            in_specs=[
                pl.BlockSpec(memory_space=pl.ANY),
                pl.BlockSpec(memory_space=pl.ANY),
                pl.BlockSpec((1, D_MODEL, D_EXPERT), wsel),
                pl.BlockSpec((1, D_MODEL, D_EXPERT), wsel),
                pl.BlockSpec((1, D_EXPERT, D_MODEL), wsel),
            ],
            out_specs=pl.BlockSpec((MOE_BLK * ROW_TILES, LANES), lambda j, be, nu: (j, 0)),
            scratch_shapes=[
                pltpu.VMEM((2, MOE_BLK * ROW_TILES, LANES), F32),
                pltpu.SemaphoreType.DMA((2,)),
                pltpu.SMEM((2, MOE_BLK), jnp.int32),
                pltpu.SemaphoreType.DMA((2,)),
                pltpu.VMEM((D_MODEL, D_EXPERT), BF16),
                pltpu.VMEM((D_MODEL, D_EXPERT), BF16),
                pltpu.VMEM((D_EXPERT, D_MODEL), BF16),
            ],
        ),
        out_shape=jax.ShapeDtypeStruct((n_blocks * MOE_BLK * ROW_TILES, LANES), F32),
        compiler_params=pltpu.CompilerParams(
            dimension_semantics=("arbitrary",), vmem_limit_bytes=VMEM_LIMIT),
        name="expert",
    )(block_e, n_used, slot_asg.reshape(n_blocks, MOE_BLK), mt, w_gate, w_up, w_down)


def _combine_kernel(dest_ref, yb_hbm, x1_ref, route_ref, mod_ref, g_ref, o_ref, ybuf, sems, *, tc):
    i = pl.program_id(0)
    slot = i % 2

    def row_copy(d, sl, k, r):
        return pltpu.make_async_copy(_row_tile(yb_hbm, d), _row_tile(ybuf.at[sl, k], r), sems.at[sl])

    def gather(step, sl):
        def issue(c, carry):
            for u in range(DMA_UNROLL):
                r = c * DMA_UNROLL + u
                for k in range(TOP_K):
                    row_copy(dest_ref[TOP_K * (step * tc + r) + k], sl, k, r).start(priority=k)
            return carry
        lax.fori_loop(0, tc // DMA_UNROLL, issue, 0)

    @pl.when(i == 0)
    def _():
        gather(0, 0)

    @pl.when(i + 1 < pl.num_programs(0))
    def _():
        gather(i + 1, 1 - slot)

    def drain(c, carry):
        for u in range(DMA_UNROLL * TOP_K):
            row_copy(0, slot, 0, 0).wait()
        return carry
    lax.fori_loop(0, tc // DMA_UNROLL, drain, 0)

    def rows(k):
        return jnp.concatenate(
            [ybuf[slot, k, pl.ds(s, tc, stride=ROW_TILES), :] for s in range(ROW_TILES)], axis=-1)

    route = route_ref[...]
    y = route[:, 2:3] * rows(0) + route[:, 3:4] * rows(1)
    x2 = x1_ref[...] + mod_ref[0, 5:6, :] * y
    ms = jnp.mean(x2 * x2, axis=-1, keepdims=True)
    o_ref[...] = x2 * lax.rsqrt(ms + NORM_EPS) * g_ref[...]


def _combine(dest, yb, x1_2d, route, mod3, final_g, seq):
    t_all, d = x1_2d.shape
    tc = min(COMBINE_TILE, seq)
    per_seq = seq // tc
    return pl.pallas_call(
        functools.partial(_combine_kernel, tc=tc),
        grid_spec=pltpu.PrefetchScalarGridSpec(
            num_scalar_prefetch=1,
            grid=(t_all // tc,),
            in_specs=[
                pl.BlockSpec(memory_space=pl.ANY),
                pl.BlockSpec((tc, d), lambda i, dest: (i, 0)),
                pl.BlockSpec((tc, LANES), lambda i, dest: (i, 0)),
                pl.BlockSpec((1, 6, d), lambda i, dest: (i // per_seq, 0, 0)),
                pl.BlockSpec((1, d), lambda i, dest: (0, 0)),
            ],
            out_specs=pl.BlockSpec((tc, d), lambda i, dest: (i, 0)),
            scratch_shapes=[
                pltpu.VMEM((2, TOP_K, tc * ROW_TILES, LANES), F32),
                pltpu.SemaphoreType.DMA((2,)),
            ],
        ),
        out_shape=jax.ShapeDtypeStruct((t_all, d), F32),
        compiler_params=pltpu.CompilerParams(
            dimension_semantics=("arbitrary",), vmem_limit_bytes=VMEM_LIMIT),
        name="combine",
    )(dest, yb, x1_2d, route, mod3, final_g.reshape(1, d))


def kernel(x, c, ctx, c_ctx, ada_w, ada_b, norm1_g, norm2_g, w_in, rnn_conv_w, rnn_conv_b, rg_wa, rg_ba,
           rg_wx, rg_bx, rg_lambda, sc_conv_w, w_out, router_group_w, router_group_b, router_exp_w,
           router_exp_b, exp_w_gate, exp_w_up, exp_w_down, final_norm_g):
    bn, seq, d = x.shape
    assert d == D_MODEL and bn < MOD_ROWS and ada_w.shape[0] == 1
    t_all = bn * seq

    cc = jnp.concatenate([c, c_ctx[None], jnp.zeros((MOD_ROWS - bn - 1, d), F32)], axis=0)
    mod3 = _modulation(cc, ada_w[0], ada_b[0]).reshape(MOD_ROWS, 6, d)

    w_in_b = w_in[0].astype(BF16)
    xr, gr, u, bg = _inproj(x, mod3, None, norm1_g[0], w_in_b, latent=True)
    (xr_c,) = _inproj(ctx, mod3, bn, norm1_g[0], w_in_b[:, :D_RNN], latent=False)

    wg, bgate = _gate_weights(rg_wa[0], rg_ba[0], rg_wx[0], rg_bx[0])
    y_rnn = _rnn(xr, xr_c, gr, rnn_conv_w[0], rnn_conv_b[0], wg, bgate, rg_lambda[0])
    y_conv = _gconv(u, bg, sc_conv_w[0])

    wr = jnp.zeros((d, LANES), F32)
    wr = wr.at[:, :N_GROUPS].set(router_group_w[0]).at[:, EXPERT_LANE0:EXPERT_LANE0 + N_EXPERTS].set(router_exp_w[0])
    br = jnp.zeros((1, LANES), F32)
    br = br.at[0, :N_GROUPS].set(router_group_b[0]).at[0, EXPERT_LANE0:EXPERT_LANE0 + N_EXPERTS].set(router_exp_b[0])
    wr_hi, wr_lo = _split_bf16(wr)
    x1, mt, route, cnt = _outproj(x, y_rnn, y_conv, w_out[0].astype(BF16), mod3, norm2_g[0], wr_hi, wr_lo, br)

    n_assign = t_all * TOP_K
    n_blocks = (n_assign + N_EXPERTS * (MOE_BLK - 1) + MOE_BLK - 1) // MOE_BLK
    counts = cnt[0, EXPERT_LANE0:EXPERT_LANE0 + N_EXPERTS].astype(jnp.int32)
    pcounts = (counts + MOE_BLK - 1) // MOE_BLK * MOE_BLK
    pends = jnp.cumsum(pcounts)
    pstarts = pends - pcounts
    experts = route[:, 0:TOP_K].astype(jnp.int32)
    ranks = route[:, 4:4 + TOP_K].astype(jnp.int32)
    onehot = experts[:, :, None] == jnp.arange(N_EXPERTS, dtype=jnp.int32)
    dest = (ranks + jnp.sum(jnp.where(onehot, pstarts, 0), axis=-1)).reshape(n_assign)
    n_used = (pends[-1] // MOE_BLK).astype(jnp.int32)
    blk_start = jnp.arange(n_blocks, dtype=jnp.int32) * MOE_BLK
    block_e = jnp.minimum(jnp.sum(blk_start[:, None] >= pends[None, :], axis=1), N_EXPERTS - 1)
    last_e = jnp.max(jnp.where(counts > 0, jnp.arange(N_EXPERTS, dtype=jnp.int32), 0))
    block_e = jnp.where(blk_start < pends[-1], block_e, last_e).astype(jnp.int32)

    n_slots = n_blocks * MOE_BLK
    slot_asg = _slotmap(dest, n_slots)
    yb = _experts(block_e, n_used.reshape(1), slot_asg, mt, exp_w_gate[0], exp_w_up[0], exp_w_down[0],
                  n_blocks)
    out = _combine(dest, yb, x1.reshape(t_all, d), route, mod3, final_norm_g, seq)
    return out.reshape(bn, seq, d)
```

```python
import functools

import jax
import jax.numpy as jnp
from jax import lax
from jax.experimental import pallas as pl
from jax.experimental.pallas import tpu as pltpu

F32 = jnp.float32
BF16 = jnp.bfloat16

D_MODEL = 1024
D_RNN = 512
D_CONV = 512
D_CONV_H = D_CONV // 2
RNN_HEADS = 8
RNN_HEAD_DIM = D_RNN // RNN_HEADS
GRID_W = 64
RG_C = 8.0
N_GROUPS = 4
EXPERTS_PER_GROUP = 8
N_EXPERTS = N_GROUPS * EXPERTS_PER_GROUP
TOP_K = 2
D_EXPERT = 512
NORM_EPS = 1e-6

LANES = 128
SUBLANES = 8
ROW_TILES = D_MODEL // LANES
N_LANE_GROUPS = D_RNN // LANES
EXPERT_LANE0 = N_GROUPS

MOD_ROWS = 16
MOD_TN = 768
TOK_TILE = 512
MOE_BLK = 256
EXPERT_PIECES = 8
EXPERT_RING = 3
COMBINE_TILE = 256
DMA_UNROLL = 16
VMEM_LIMIT = 48 * 1024 * 1024


def _dot(a, b):
    return jnp.dot(a, b, preferred_element_type=F32)


def _split_bf16(x):
    hi = x.astype(BF16)
    lo = (x - hi.astype(F32)).astype(BF16)
    return hi, lo


def _mod_kernel(cc_ref, w_ref, b_ref, o_ref):
    s = cc_ref[...]
    s = s * jax.nn.sigmoid(s)
    s_hi, s_lo = _split_bf16(s)
    w_hi, w_lo = _split_bf16(w_ref[...])
    o_ref[...] = _dot(s_hi, w_hi) + _dot(s_lo, w_hi) + _dot(s_hi, w_lo) + b_ref[...]


def _modulation(cc, ada_w, ada_b):
    n = ada_w.shape[1]
    return pl.pallas_call(
        _mod_kernel,
        grid=(n // MOD_TN,),
        in_specs=[
            pl.BlockSpec((MOD_ROWS, D_MODEL), lambda j: (0, 0)),
            pl.BlockSpec((D_MODEL, MOD_TN), lambda j: (0, j)),
            pl.BlockSpec((1, MOD_TN), lambda j: (0, j)),
        ],
        out_specs=pl.BlockSpec((MOD_ROWS, MOD_TN), lambda j: (0, j)),
        out_shape=jax.ShapeDtypeStruct((MOD_ROWS, n), F32),
        compiler_params=pltpu.CompilerParams(vmem_limit_bytes=VMEM_LIMIT),
        name="mod",
    )(cc, ada_w, ada_b.reshape(1, n))


def _norm_mod(x, g, scale, shift):
    ms = jnp.mean(x * x, axis=-1, keepdims=True)
    y = x * lax.rsqrt(ms + NORM_EPS) * g
    return y * (1.0 + scale) + shift


def _inproj_kernel(x_ref, mod_ref, g_ref, w_ref, *out_refs, latent):
    h = _norm_mod(x_ref[0], g_ref[...], mod_ref[0, 1:2, :], mod_ref[0, 0:1, :])
    hb = h.astype(BF16)
    xr = _dot(hb, w_ref[:, 0:D_RNN])
    out_refs[0][0] = xr
    if latent:
        o = D_RNN
        out_refs[1][0] = _dot(hb, w_ref[:, o:o + D_RNN])
        o += D_RNN
        v = _dot(hb, w_ref[:, o:o + D_CONV])
        out_refs[3][0] = _dot(hb, w_ref[:, o + D_CONV:o + 2 * D_CONV])
        cg = _dot(hb, w_ref[:, o + 2 * D_CONV:o + 3 * D_CONV])
        out_refs[2][0] = cg * v


def _inproj(x, mod3, mod_row, norm_g, w_bf16, latent):
    bn, n, d = x.shape
    tm = min(TOK_TILE, n)
    n_out = 4 if latent else 1
    width = w_bf16.shape[1]
    mod_map = (lambda b, i: (b, 0, 0)) if mod_row is None else (lambda b, i: (mod_row, 0, 0))
    return pl.pallas_call(
        functools.partial(_inproj_kernel, latent=latent),
        grid=(bn, n // tm),
        in_specs=[
            pl.BlockSpec((1, tm, d), lambda b, i: (b, i, 0)),
            pl.BlockSpec((1, 6, d), mod_map),
            pl.BlockSpec((1, d), lambda b, i: (0, 0)),
            pl.BlockSpec((d, width), lambda b, i: (0, 0)),
        ],
        out_specs=[pl.BlockSpec((1, tm, D_RNN), lambda b, i: (b, i, 0))] * n_out,
        out_shape=[jax.ShapeDtypeStruct((bn, n, D_RNN), F32)] * n_out,
        compiler_params=pltpu.CompilerParams(vmem_limit_bytes=VMEM_LIMIT),
        name="inproj_lat" if latent else "inproj_ctx",
    )(x, mod3, norm_g.reshape(1, d), w_bf16)


def _shift_rows(x, k):
    n = x.shape[0]
    row = lax.broadcasted_iota(jnp.int32, x.shape, 0)
    rolled = pltpu.roll(x, k % n, axis=0)
    valid = (row >= k) if k > 0 else (row < n + k)
    return jnp.where(valid, rolled, 0.0)


def _round8(v):
    return (v + SUBLANES - 1) // SUBLANES * SUBLANES


def _scan_work_rows(n):
    rows = 0
    while n > SUBLANES:
        g = n // SUBLANES
        rows += 2 * n + 2 * _round8(g) + _round8(g) + 2 * SUBLANES
        n = g
    return rows


def _linear_scan(a_ref, a_off, b_ref, b_off, h_ref, h_off, n, h0, reverse, work, w_off):
    if n <= SUBLANES:
        h = h0
        for r in (range(n - 1, -1, -1) if reverse else range(n)):
            h = a_ref[pl.ds(a_off + r, 1), :] * h + b_ref[pl.ds(b_off + r, 1), :]
            h_ref[pl.ds(h_off + r, 1), :] = h
        return
    g = n // SUBLANES
    gp = _round8(g)
    ca_off = w_off
    cb_off = ca_off + n
    a2_off = cb_off + n
    b2_off = a2_off + gp
    hs_off = b2_off + gp
    next_off = hs_off + gp + 2 * SUBLANES

    a = a_ref[pl.ds(a_off, n), :].reshape(g, SUBLANES, LANES)
    b = b_ref[pl.ds(b_off, n), :].reshape(g, SUBLANES, LANES)
    row = lax.broadcasted_iota(jnp.int32, (g, SUBLANES, LANES), 1)
    for s in (1, 2, 4):
        if reverse:
            a_s = pltpu.roll(a, SUBLANES - s, axis=1)
            b_s = pltpu.roll(b, SUBLANES - s, axis=1)
            valid = row < SUBLANES - s
        else:
            a_s = pltpu.roll(a, s, axis=1)
            b_s = pltpu.roll(b, s, axis=1)
            valid = row >= s
        b = jnp.where(valid, a * b_s + b, b)
        a = jnp.where(valid, a * a_s, a)
    work[pl.ds(ca_off, n), :] = a.reshape(n, LANES)
    work[pl.ds(cb_off, n), :] = b.reshape(n, LANES)

    last = 0 if reverse else SUBLANES - 1
    work[pl.ds(a2_off, g), :] = work[pl.ds(ca_off + last, g, stride=SUBLANES), :]
    work[pl.ds(b2_off, g), :] = work[pl.ds(cb_off + last, g, stride=SUBLANES), :]
    _linear_scan(work, a2_off, work, b2_off, work, hs_off + SUBLANES, g, h0, reverse, work, next_off)
    if reverse:
        work[pl.ds(hs_off + SUBLANES + g, 1), :] = h0
        in_off = hs_off + SUBLANES + 1
    else:
        work[pl.ds(hs_off + SUBLANES - 1, 1), :] = h0
        in_off = hs_off + SUBLANES - 1

    def apply_group(gi, r0):
        h_in = work[pl.ds(in_off + gi, 1), :]
        h_ref[pl.ds(h_off + r0, SUBLANES), :] = (
            work[pl.ds(ca_off + r0, SUBLANES), :] * h_in + work[pl.ds(cb_off + r0, SUBLANES), :])

    if g <= 32:
        for gi in range(g):
            apply_group(gi, gi * SUBLANES)
    else:
        def body(go, carry):
            for u in range(SUBLANES):
                gi = go * SUBLANES + u
                apply_group(gi, pl.multiple_of(gi * SUBLANES, SUBLANES))
            return carry
        lax.fori_loop(0, g // SUBLANES, body, 0)


def _rnn_kernel(xr_ref, xrc_ref, gr_ref, cw_ref, cb_ref, wg_ref, bg_ref, lam_ref, y_ref,
                a_s, b_s, h_s, hsum_s, work, *, n_lat, n_ctx):
    nl = -lam_ref[...]
    sp = jnp.maximum(nl, 0.0) + jnp.log1p(jnp.exp(-jnp.abs(nl)))
    cw = cw_ref[...]
    bias = cb_ref[...]
    wg = wg_ref[0]
    bg = bg_ref[0]

    def conv_gates(x):
        xc = (cw[0:1] * _shift_rows(x, 2) + cw[1:2] * _shift_rows(x, 1) + cw[2:3] * x
              + cw[3:4] * _shift_rows(x, -1)) + bias
        return xc, _dot(xc.astype(BF16), wg) + bg

    def coeffs(xc, gates, d):
        r = jax.nn.sigmoid(gates[:, (2 * d) * LANES:(2 * d + 1) * LANES])
        i = jax.nn.sigmoid(gates[:, (2 * d + 1) * LANES:(2 * d + 2) * LANES])
        log_a = (-RG_C * r) * sp[d:d + 1]
        a = jnp.exp(log_a)
        b = jnp.sqrt(-jnp.tanh(log_a) * (a * a + 1.0)) * (i * xc)
        return a, b

    zero = jnp.zeros((1, LANES), F32)
    xc_c, gates_c = conv_gates(xrc_ref[0])
    h0 = []
    for d in range(2):
        a, b = coeffs(xc_c, gates_c, d)
        a_s[pl.ds(0, n_ctx), :] = a
        b_s[pl.ds(0, n_ctx), :] = b
        _linear_scan(a_s, 0, b_s, 0, h_s, 0, n_ctx, zero, d == 1, work, 0)
        h0.append(h_s[pl.ds(0 if d == 1 else n_ctx - 1, 1), :])

    xc_l, gates_l = conv_gates(xr_ref[0])
    for d in range(2):
        a, b = coeffs(xc_l, gates_l, d)
        a_s[...] = a
        b_s[...] = b
        _linear_scan(a_s, 0, b_s, 0, h_s if d == 1 else hsum_s, 0, n_lat, h0[d], d == 1, work, 0)
    y_ref[0] = jax.nn.gelu(gr_ref[0], approximate=True) * (hsum_s[...] + h_s[...])


def _rnn(xr, xr_c, gr, conv_w, conv_b, wg, bgate, lam):
    bn, n, _ = xr.shape
    n_ctx = xr_c.shape[1]
    seq_spec = pl.BlockSpec((1, n, LANES), lambda b, p: (b, 0, p))
    return pl.pallas_call(
        functools.partial(_rnn_kernel, n_lat=n, n_ctx=n_ctx),
        grid=(bn, N_LANE_GROUPS),
        in_specs=[
            seq_spec,
            pl.BlockSpec((1, n_ctx, LANES), lambda b, p: (b, 0, p)),
            seq_spec,
            pl.BlockSpec((4, LANES), lambda b, p: (0, p)),
            pl.BlockSpec((1, LANES), lambda b, p: (0, p)),
            pl.BlockSpec((1, LANES, 4 * LANES), lambda b, p: (p, 0, 0)),
            pl.BlockSpec((1, 1, 4 * LANES), lambda b, p: (p, 0, 0)),
            pl.BlockSpec((2, LANES), lambda b, p: (0, p)),
        ],
        out_specs=seq_spec,
        out_shape=jax.ShapeDtypeStruct((bn, n, D_RNN), F32),
        scratch_shapes=[pltpu.VMEM((n, LANES), F32)] * 4
        + [pltpu.VMEM((_scan_work_rows(n), LANES), F32)],
        compiler_params=pltpu.CompilerParams(vmem_limit_bytes=VMEM_LIMIT),
        name="rnn",
    )(xr, xr_c, gr, conv_w, conv_b.reshape(1, D_RNN), wg, bgate, lam)


def _gate_weights(rg_wa, rg_ba, rg_wx, rg_bx):
    eye = jnp.eye(2, dtype=F32)
    blocks, biases = [], []
    for d in range(2):
        for w, bvec in ((rg_wa[d], rg_ba[d]), (rg_wx[d], rg_bx[d])):
            w4 = w.reshape(N_LANE_GROUPS, 2, RNN_HEAD_DIM, RNN_HEAD_DIM)
            bd = jnp.einsum("paij,ac->paicj", w4, eye).reshape(N_LANE_GROUPS, LANES, LANES)
            blocks.append(bd)
            biases.append(bvec.reshape(N_LANE_GROUPS, 1, LANES))
    return jnp.concatenate(blocks, axis=-1).astype(BF16), jnp.concatenate(biases, axis=-1)


def _gconv_kernel(u_ref, bg_ref, w_ref, y_ref, *, n):
    p = pl.program_id(1)
    u = u_ref[0]
    w = w_ref[...]

    @pl.when(p < D_CONV_H // LANES)
    def _():
        col = lax.broadcasted_iota(jnp.int32, u.shape, 0) % GRID_W
        left = jnp.where(col > 0, _shift_rows(u, 1), 0.0)
        right = jnp.where(col < GRID_W - 1, _shift_rows(u, -1), 0.0)
        y_ref[0] = bg_ref[0] * (w[0:1] * left + w[1:2] * u + w[2:3] * right)

    @pl.when(p >= D_CONV_H // LANES)
    def _():
        y_ref[0] = bg_ref[0] * (w[0:1] * _shift_rows(u, GRID_W) + w[1:2] * u
                                + w[2:3] * _shift_rows(u, -GRID_W))


def _gconv(u, bg, w):
    bn, n, _ = u.shape
    seq_spec = pl.BlockSpec((1, n, LANES), lambda b, p: (b, 0, p))
    return pl.pallas_call(
        functools.partial(_gconv_kernel, n=n),
        grid=(bn, D_CONV // LANES),
        in_specs=[seq_spec, seq_spec, pl.BlockSpec((3, LANES), lambda b, p: (0, p))],
        out_specs=seq_spec,
        out_shape=jax.ShapeDtypeStruct((bn, n, D_CONV), F32),
        compiler_params=pltpu.CompilerParams(vmem_limit_bytes=VMEM_LIMIT),
        name="gconv",
    )(u, bg, w)


def _lane_max(x, mask):
    return jnp.max(jnp.where(mask, x, -jnp.inf), axis=-1, keepdims=True)


def _first_lane(cond, lane):
    return jnp.min(jnp.where(cond, lane, float(LANES)), axis=-1, keepdims=True)


def _outproj_kernel(x_ref, yr_ref, yc_ref, w_ref, mod_ref, g_ref, wr_hi_ref, wr_lo_ref, br_ref,
                    x1_ref, mt_ref, route_ref, cnt_ref, carry, *, tm):
    first = (pl.program_id(0) == 0) & (pl.program_id(1) == 0)

    @pl.when(first)
    def _():
        carry[...] = jnp.zeros_like(carry)

    mix = _dot(yr_ref[0].astype(BF16), w_ref[0:D_RNN, :]) + _dot(yc_ref[0].astype(BF16), w_ref[D_RNN:, :])
    x1 = x_ref[0] + mod_ref[0, 2:3, :] * mix
    x1_ref[0] = x1
    m = _norm_mod(x1, g_ref[...], mod_ref[0, 4:5, :], mod_ref[0, 3:4, :])
    for s in range(ROW_TILES):
        mt_ref[pl.ds(s, tm, stride=ROW_TILES), :] = m[:, s * LANES:(s + 1) * LANES]

    m_hi, m_lo = _split_bf16(m)
    logits = (_dot(m_hi, wr_hi_ref[...]) + _dot(m_lo, wr_hi_ref[...]) + _dot(m_hi, wr_lo_ref[...])
              + br_ref[...])
    lane_i = lax.broadcasted_iota(jnp.int32, logits.shape, 1)
    lane = lane_i.astype(F32)
    is_grp = lane_i < N_GROUPS
    g_max = _lane_max(logits, is_grp)
    grp = _first_lane(is_grp & (logits == g_max), lane)
    p_g = 1.0 / jnp.sum(jnp.where(is_grp, jnp.exp(logits - g_max), 0.0), axis=-1, keepdims=True)
    lo_lane = EXPERT_LANE0 + grp * EXPERTS_PER_GROUP
    in_grp = (lane >= lo_lane) & (lane < lo_lane + EXPERTS_PER_GROUP)
    l1 = _lane_max(logits, in_grp)
    i1 = _first_lane(in_grp & (logits == l1), lane)
    rest = in_grp & (lane != i1)
    l2 = _lane_max(logits, rest)
    i2 = _first_lane(rest & (logits == l2), lane)
    r21 = jnp.exp(l2 - l1)
    gate1 = p_g / (1.0 + r21)
    gate2 = gate1 * r21

    oh1 = jnp.where(lane == i1, 1.0, 0.0)
    oh2 = jnp.where(lane == i2, 1.0, 0.0)
    both = (oh1 + oh2).astype(BF16)
    ti = lax.broadcasted_iota(jnp.int32, (tm, tm), 0)
    tj = lax.broadcasted_iota(jnp.int32, (tm, tm), 1)
    tri = jnp.where(tj < ti, 1.0, 0.0).astype(BF16)
    before = _dot(tri, both) + carry[...]
    rank1 = jnp.sum(oh1 * before, axis=-1, keepdims=True)
    rank2 = jnp.sum(oh2 * before, axis=-1, keepdims=True)
    total = carry[...] + jnp.sum(oh1 + oh2, axis=0, keepdims=True)
    carry[...] = total
    cnt_ref[...] = total

    e1 = i1 - EXPERT_LANE0
    e2 = i2 - EXPERT_LANE0
    out = jnp.zeros(logits.shape, F32)
    for k, val in enumerate((e1, e2, gate1, gate2, rank1, rank2)):
        out = jnp.where(lane_i == k, val, out)
    route_ref[...] = out


def _outproj(x, y_rnn, y_conv, w_out_bf16, mod3, norm_g, wr_hi, wr_lo, br):
    bn, n, d = x.shape
    tm = min(TOK_TILE, n)
    nt = n // tm
    t_all = bn * n
    return pl.pallas_call(
        functools.partial(_outproj_kernel, tm=tm),
        grid=(bn, nt),
        in_specs=[
            pl.BlockSpec((1, tm, d), lambda b, i: (b, i, 0)),
            pl.BlockSpec((1, tm, D_RNN), lambda b, i: (b, i, 0)),
            pl.BlockSpec((1, tm, D_CONV), lambda b, i: (b, i, 0)),
            pl.BlockSpec((D_RNN + D_CONV, d), lambda b, i: (0, 0)),
            pl.BlockSpec((1, 6, d), lambda b, i: (b, 0, 0)),
            pl.BlockSpec((1, d), lambda b, i: (0, 0)),
            pl.BlockSpec((d, LANES), lambda b, i: (0, 0)),
            pl.BlockSpec((d, LANES), lambda b, i: (0, 0)),
            pl.BlockSpec((1, LANES), lambda b, i: (0, 0)),
        ],
        out_specs=[
            pl.BlockSpec((1, tm, d), lambda b, i: (b, i, 0)),
            pl.BlockSpec((tm * ROW_TILES, LANES), lambda b, i: (b * nt + i, 0)),
            pl.BlockSpec((tm, LANES), lambda b, i: (b * nt + i, 0)),
            pl.BlockSpec((1, LANES), lambda b, i: (0, 0)),
        ],
        out_shape=[
            jax.ShapeDtypeStruct((bn, n, d), F32),
            jax.ShapeDtypeStruct((t_all * ROW_TILES, LANES), F32),
            jax.ShapeDtypeStruct((t_all, LANES), F32),
            jax.ShapeDtypeStruct((1, LANES), F32),
        ],
        scratch_shapes=[pltpu.VMEM((1, LANES), F32)],
        compiler_params=pltpu.CompilerParams(
            dimension_semantics=("arbitrary", "arbitrary"), vmem_limit_bytes=VMEM_LIMIT),
        name="outproj",
    )(x, y_rnn, y_conv, w_out_bf16, mod3, norm_g.reshape(1, d), wr_hi, wr_lo, br)


def _row_tile(ref, row):
    return ref.at[pl.ds(pl.multiple_of(row * ROW_TILES, ROW_TILES), ROW_TILES)]


def _slotmap_kernel(dest_ref, asg_ref):
    def init(c, carry):
        for u in range(DMA_UNROLL):
            asg_ref[c * DMA_UNROLL + u] = 0
        return carry
    lax.fori_loop(0, asg_ref.shape[0] // DMA_UNROLL, init, 0)

    def body(c, carry):
        for u in range(DMA_UNROLL):
            a = c * DMA_UNROLL + u
            asg_ref[dest_ref[a]] = a
        return carry
    lax.fori_loop(0, dest_ref.shape[0] // DMA_UNROLL, body, 0)


def _slotmap(dest, n_slots):
    return pl.pallas_call(
        _slotmap_kernel,
        in_specs=[pl.BlockSpec(memory_space=pltpu.SMEM)],
        out_specs=pl.BlockSpec(memory_space=pltpu.SMEM),
        out_shape=jax.ShapeDtypeStruct((n_slots,), jnp.int32),
        name="slotmap",
    )(dest)


def _expert_kernel(be_ref, nu_ref, asg_hbm, mt_hbm, wg_ref, wu_ref, wd_ref, yb_ref,
                   xbuf, sems, idx, isems, wg_s, wu_s, wd_s):
    j = pl.program_id(0)
    n_used = nu_ref[0]
    last = n_used - 1

    def idx_copy(blk, sl):
        return pltpu.make_async_copy(asg_hbm.at[blk], idx.at[sl], isems.at[sl])

    def row_copy(tok, slot, r):
        return pltpu.make_async_copy(_row_tile(mt_hbm, tok), _row_tile(xbuf.at[slot], r), sems.at[slot])

    def issue_rows(slot, r0, n):
        for u in range(n):
            tok = lax.shift_right_logical(idx[slot, 0, r0 + u], 1)
            row_copy(tok, slot, r0 + u).start(priority=u % 2)

    def drain(slot):
        def body(c, carry):
            for u in range(DMA_UNROLL):
                row_copy(0, slot, 0).wait()
            return carry
        lax.fori_loop(0, MOE_BLK // DMA_UNROLL, body, 0)

    @pl.when(j >= n_used)
    def _():
        yb_ref[...] = jnp.zeros_like(yb_ref)

    @pl.when(j < n_used)
    def _():
        slot = j % EXPERT_RING
        ahead = (j + EXPERT_RING - 1) % EXPERT_RING

        @pl.when(j == 0)
        def _():
            for b in range(EXPERT_RING - 1):
                idx_copy(jnp.minimum(b, last), b).start()
                idx_copy(0, b).wait()

                def body(c, carry, b=b):
                    issue_rows(b, c * DMA_UNROLL, DMA_UNROLL)
                    return carry
                lax.fori_loop(0, MOE_BLK // DMA_UNROLL, body, 0)
            idx_copy(jnp.minimum(EXPERT_RING - 1, last), EXPERT_RING - 1).start()

        e = be_ref[j]
        prev = be_ref[jnp.maximum(j - 1, 0)]

        @pl.when((j == 0) | (e != prev))
        def _():
            wg_s[...] = wg_ref[0].astype(BF16)
            wu_s[...] = wu_ref[0].astype(BF16)
            wd_s[...] = wd_ref[0].astype(BF16)

        drain(slot)
        idx_copy(0, ahead).wait()

        piece = MOE_BLK // EXPERT_PIECES

        x = jnp.concatenate(
            [xbuf[slot, pl.ds(s, MOE_BLK, stride=ROW_TILES), :] for s in range(ROW_TILES)], axis=-1)
        xb16 = x.astype(BF16)
        half = D_EXPERT // 2
        acts = []
        for q in range(2):
            gate = _dot(xb16, wg_s[:, q * half:(q + 1) * half])
            issue_rows(ahead, (2 * q) * piece, piece)
            up = _dot(xb16, wu_s[:, q * half:(q + 1) * half])
            issue_rows(ahead, (2 * q + 1) * piece, piece)
            acts.append(((gate * jax.nn.sigmoid(gate)) * up).astype(BF16))
        h = jnp.concatenate(acts, axis=-1)
        cols = D_MODEL // 4
        for q in range(4):
            y = _dot(h, wd_s[:, q * cols:(q + 1) * cols])
            issue_rows(ahead, (4 + q) * piece, piece)
            for s in range(cols // LANES):
                yb_ref[pl.ds(q * (cols // LANES) + s, MOE_BLK, stride=ROW_TILES), :] = (
                    y[:, s * LANES:(s + 1) * LANES])

        @pl.when(j < last)
        def _():
            idx_copy(jnp.minimum(j + EXPERT_RING, last), slot).start()

        @pl.when(j == last)
        def _():
            for b in range(1, EXPERT_RING):
                drain((j + b) % EXPERT_RING)


def _experts(block_e, n_used, slot_asg, mt, w_gate, w_up, w_down, n_blocks):
    def wsel(j, be, nu):
        return (be[jnp.minimum(j, nu[0] - 1)], 0, 0)

    return pl.pallas_call(
        _expert_kernel,
        grid_spec=pltpu.PrefetchScalarGridSpec(
            num_scalar_prefetch=2,
            grid=(n_blocks,),
            in_specs=[
                pl.BlockSpec(memory_space=pl.ANY),
                pl.BlockSpec(memory_space=pl.ANY),
                pl.BlockSpec((1, D_MODEL, D_EXPERT), wsel),
                pl.BlockSpec((1, D_MODEL, D_EXPERT), wsel),
                pl.BlockSpec((1, D_EXPERT, D_MODEL), wsel),
            ],
            out_specs=pl.BlockSpec((MOE_BLK * ROW_TILES, LANES), lambda j, be, nu: (j, 0)),
            scratch_shapes=[
                pltpu.VMEM((EXPERT_RING, MOE_BLK * ROW_TILES, LANES), F32),
                pltpu.SemaphoreType.DMA((EXPERT_RING,)),
                pltpu.SMEM((EXPERT_RING, 1, MOE_BLK), jnp.int32),
                pltpu.SemaphoreType.DMA((EXPERT_RING,)),
                pltpu.VMEM((D_MODEL, D_EXPERT), BF16),
                pltpu.VMEM((D_MODEL, D_EXPERT), BF16),
                pltpu.VMEM((D_EXPERT, D_MODEL), BF16),
            ],
        ),
        out_shape=jax.ShapeDtypeStruct((n_blocks * MOE_BLK * ROW_TILES, LANES), F32),
        compiler_params=pltpu.CompilerParams(
            dimension_semantics=("arbitrary",), vmem_limit_bytes=VMEM_LIMIT),
        name="expert",
    )(block_e, n_used, slot_asg.reshape(n_blocks, 1, MOE_BLK), mt, w_gate, w_up, w_down)


def _combine_kernel(dest_ref, yb_hbm, x1_ref, route_ref, mod_ref, g_ref, o_ref, ybuf, sems, *, tc):
    i = pl.program_id(0)
    slot = i % 2

    def row_copy(d, sl, k, r):
        return pltpu.make_async_copy(_row_tile(yb_hbm, d), _row_tile(ybuf.at[sl, k], r), sems.at[sl])

    def gather(step, sl):
        def issue(c, carry):
            for u in range(DMA_UNROLL):
                r = c * DMA_UNROLL + u
                for k in range(TOP_K):
                    row_copy(dest_ref[TOP_K * (step * tc + r) + k], sl, k, r).start(priority=k)
            return carry
        lax.fori_loop(0, tc // DMA_UNROLL, issue, 0)

    @pl.when(i == 0)
    def _():
        gather(0, 0)

    @pl.when(i + 1 < pl.num_programs(0))
    def _():
        gather(i + 1, 1 - slot)

    def drain(c, carry):
        for u in range(DMA_UNROLL * TOP_K):
            row_copy(0, slot, 0, 0).wait()
        return carry
    lax.fori_loop(0, tc // DMA_UNROLL, drain, 0)

    def rows(k):
        return jnp.concatenate(
            [ybuf[slot, k, pl.ds(s, tc, stride=ROW_TILES), :] for s in range(ROW_TILES)], axis=-1)

    route = route_ref[...]
    y = route[:, 2:3] * rows(0) + route[:, 3:4] * rows(1)
    x2 = x1_ref[...] + mod_ref[0, 5:6, :] * y
    ms = jnp.mean(x2 * x2, axis=-1, keepdims=True)
    o_ref[...] = x2 * lax.rsqrt(ms + NORM_EPS) * g_ref[...]


def _combine(dest, yb, x1_2d, route, mod3, final_g, seq):
    t_all, d = x1_2d.shape
    tc = min(COMBINE_TILE, seq)
    per_seq = seq // tc
    return pl.pallas_call(
        functools.partial(_combine_kernel, tc=tc),
        grid_spec=pltpu.PrefetchScalarGridSpec(
            num_scalar_prefetch=1,
            grid=(t_all // tc,),
            in_specs=[
                pl.BlockSpec(memory_space=pl.ANY),
                pl.BlockSpec((tc, d), lambda i, dest: (i, 0)),
                pl.BlockSpec((tc, LANES), lambda i, dest: (i, 0)),
                pl.BlockSpec((1, 6, d), lambda i, dest: (i // per_seq, 0, 0)),
                pl.BlockSpec((1, d), lambda i, dest: (0, 0)),
            ],
            out_specs=pl.BlockSpec((tc, d), lambda i, dest: (i, 0)),
            scratch_shapes=[
                pltpu.VMEM((2, TOP_K, tc * ROW_TILES, LANES), F32),
                pltpu.SemaphoreType.DMA((2,)),
            ],
        ),
        out_shape=jax.ShapeDtypeStruct((t_all, d), F32),
        compiler_params=pltpu.CompilerParams(
            dimension_semantics=("arbitrary",), vmem_limit_bytes=VMEM_LIMIT),
        name="combine",
    )(dest, yb, x1_2d, route, mod3, final_g.reshape(1, d))


def kernel(x, c, ctx, c_ctx, ada_w, ada_b, norm1_g, norm2_g, w_in, rnn_conv_w, rnn_conv_b, rg_wa, rg_ba,
           rg_wx, rg_bx, rg_lambda, sc_conv_w, w_out, router_group_w, router_group_b, router_exp_w,
           router_exp_b, exp_w_gate, exp_w_up, exp_w_down, final_norm_g):
    bn, seq, d = x.shape
    assert d == D_MODEL and bn < MOD_ROWS and ada_w.shape[0] == 1
    t_all = bn * seq

    cc = jnp.concatenate([c, c_ctx[None], jnp.zeros((MOD_ROWS - bn - 1, d), F32)], axis=0)
    mod3 = _modulation(cc, ada_w[0], ada_b[0]).reshape(MOD_ROWS, 6, d)

    w_in_b = w_in[0].astype(BF16)
    xr, gr, u, bg = _inproj(x, mod3, None, norm1_g[0], w_in_b, latent=True)
    (xr_c,) = _inproj(ctx, mod3, bn, norm1_g[0], w_in_b[:, :D_RNN], latent=False)

    wg, bgate = _gate_weights(rg_wa[0], rg_ba[0], rg_wx[0], rg_bx[0])
    y_rnn = _rnn(xr, xr_c, gr, rnn_conv_w[0], rnn_conv_b[0], wg, bgate, rg_lambda[0])
    y_conv = _gconv(u, bg, sc_conv_w[0])

    wr = jnp.zeros((d, LANES), F32)
    wr = wr.at[:, :N_GROUPS].set(router_group_w[0]).at[:, EXPERT_LANE0:EXPERT_LANE0 + N_EXPERTS].set(router_exp_w[0])
    br = jnp.zeros((1, LANES), F32)
    br = br.at[0, :N_GROUPS].set(router_group_b[0]).at[0, EXPERT_LANE0:EXPERT_LANE0 + N_EXPERTS].set(router_exp_b[0])
    wr_hi, wr_lo = _split_bf16(wr)
    x1, mt, route, cnt = _outproj(x, y_rnn, y_conv, w_out[0].astype(BF16), mod3, norm2_g[0], wr_hi, wr_lo, br)

    n_assign = t_all * TOP_K
    n_blocks = (n_assign + N_EXPERTS * (MOE_BLK - 1) + MOE_BLK - 1) // MOE_BLK
    counts = cnt[0, EXPERT_LANE0:EXPERT_LANE0 + N_EXPERTS].astype(jnp.int32)
    pcounts = (counts + MOE_BLK - 1) // MOE_BLK * MOE_BLK
    pends = jnp.cumsum(pcounts)
    pstarts = pends - pcounts
    experts = route[:, 0:TOP_K].astype(jnp.int32)
    ranks = route[:, 4:4 + TOP_K].astype(jnp.int32)
    onehot = experts[:, :, None] == jnp.arange(N_EXPERTS, dtype=jnp.int32)
    dest = (ranks + jnp.sum(jnp.where(onehot, pstarts, 0), axis=-1)).reshape(n_assign)
    n_used = (pends[-1] // MOE_BLK).astype(jnp.int32)
    blk_start = jnp.arange(n_blocks, dtype=jnp.int32) * MOE_BLK
    block_e = jnp.minimum(jnp.sum(blk_start[:, None] >= pends[None, :], axis=1), N_EXPERTS - 1)
    last_e = jnp.max(jnp.where(counts > 0, jnp.arange(N_EXPERTS, dtype=jnp.int32), 0))
    block_e = jnp.where(blk_start < pends[-1], block_e, last_e).astype(jnp.int32)

    n_slots = n_blocks * MOE_BLK
    slot_asg = _slotmap(dest, n_slots)
    yb = _experts(block_e, n_used.reshape(1), slot_asg, mt, exp_w_gate[0], exp_w_up[0], exp_w_down[0],
                  n_blocks)
    out = _combine(dest, yb, x1.reshape(t_all, d), route, mod3, final_norm_g, seq)
    return out.reshape(bn, seq, d)
```

```python
import functools

import jax
import jax.numpy as jnp
from jax import lax
from jax.experimental import pallas as pl
from jax.experimental.pallas import tpu as pltpu

F32 = jnp.float32
BF16 = jnp.bfloat16

D_MODEL = 1024
D_RNN = 512
D_CONV = 512
D_CONV_H = D_CONV // 2
RNN_HEADS = 8
RNN_HEAD_DIM = D_RNN // RNN_HEADS
GRID_W = 64
RG_C = 8.0
N_GROUPS = 4
EXPERTS_PER_GROUP = 8
N_EXPERTS = N_GROUPS * EXPERTS_PER_GROUP
TOP_K = 2
D_EXPERT = 512
NORM_EPS = 1e-6

LANES = 128
SUBLANES = 8
ROW_TILES = D_MODEL // LANES
N_LANE_GROUPS = D_RNN // LANES
EXPERT_LANE0 = N_GROUPS

MOD_ROWS = 16
MOD_TN = 768
TOK_TILE = 512
MOE_BLK = 256
MXU_TILE = 256
EXPERT_RING = 3
COMBINE_TILE = 256
DMA_UNROLL = 16
VMEM_LIMIT = 48 * 1024 * 1024


def _dot(a, b):
    return jnp.dot(a, b, preferred_element_type=F32)


def _split_bf16(x):
    hi = x.astype(BF16)
    lo = (x - hi.astype(F32)).astype(BF16)
    return hi, lo


def _mod_kernel(cc_ref, w_ref, b_ref, o_ref):
    s = cc_ref[...]
    s = s * jax.nn.sigmoid(s)
    s_hi, s_lo = _split_bf16(s)
    w_hi, w_lo = _split_bf16(w_ref[...])
    o_ref[...] = _dot(s_hi, w_hi) + _dot(s_lo, w_hi) + _dot(s_hi, w_lo) + b_ref[...]


def _modulation(cc, ada_w, ada_b):
    n = ada_w.shape[1]
    return pl.pallas_call(
        _mod_kernel,
        grid=(n // MOD_TN,),
        in_specs=[
            pl.BlockSpec((MOD_ROWS, D_MODEL), lambda j: (0, 0)),
            pl.BlockSpec((D_MODEL, MOD_TN), lambda j: (0, j)),
            pl.BlockSpec((1, MOD_TN), lambda j: (0, j)),
        ],
        out_specs=pl.BlockSpec((MOD_ROWS, MOD_TN), lambda j: (0, j)),
        out_shape=jax.ShapeDtypeStruct((MOD_ROWS, n), F32),
        compiler_params=pltpu.CompilerParams(vmem_limit_bytes=VMEM_LIMIT),
        name="mod",
    )(cc, ada_w, ada_b.reshape(1, n))


def _norm_mod(x, g, scale, shift):
    ms = jnp.mean(x * x, axis=-1, keepdims=True)
    y = x * lax.rsqrt(ms + NORM_EPS) * g
    return y * (1.0 + scale) + shift


def _inproj_kernel(x_ref, mod_ref, g_ref, w_ref, *out_refs, latent):
    h = _norm_mod(x_ref[0], g_ref[...], mod_ref[0, 1:2, :], mod_ref[0, 0:1, :])
    hb = h.astype(BF16)
    xr = _dot(hb, w_ref[:, 0:D_RNN])
    out_refs[0][0] = xr
    if latent:
        o = D_RNN
        out_refs[1][0] = _dot(hb, w_ref[:, o:o + D_RNN])
        o += D_RNN
        v = _dot(hb, w_ref[:, o:o + D_CONV])
        out_refs[3][0] = _dot(hb, w_ref[:, o + D_CONV:o + 2 * D_CONV])
        cg = _dot(hb, w_ref[:, o + 2 * D_CONV:o + 3 * D_CONV])
        out_refs[2][0] = cg * v


def _inproj(x, mod3, mod_row, norm_g, w_bf16, latent):
    bn, n, d = x.shape
    tm = min(TOK_TILE, n)
    n_out = 4 if latent else 1
    width = w_bf16.shape[1]
    mod_map = (lambda b, i: (b, 0, 0)) if mod_row is None else (lambda b, i: (mod_row, 0, 0))
    return pl.pallas_call(
        functools.partial(_inproj_kernel, latent=latent),
        grid=(bn, n // tm),
        in_specs=[
            pl.BlockSpec((1, tm, d), lambda b, i: (b, i, 0)),
            pl.BlockSpec((1, 6, d), mod_map),
            pl.BlockSpec((1, d), lambda b, i: (0, 0)),
            pl.BlockSpec((d, width), lambda b, i: (0, 0)),
        ],
        out_specs=[pl.BlockSpec((1, tm, D_RNN), lambda b, i: (b, i, 0))] * n_out,
        out_shape=[jax.ShapeDtypeStruct((bn, n, D_RNN), F32)] * n_out,
        compiler_params=pltpu.CompilerParams(vmem_limit_bytes=VMEM_LIMIT),
        name="inproj_lat" if latent else "inproj_ctx",
    )(x, mod3, norm_g.reshape(1, d), w_bf16)


def _shift_rows(x, k):
    n = x.shape[0]
    row = lax.broadcasted_iota(jnp.int32, x.shape, 0)
    rolled = pltpu.roll(x, k % n, axis=0)
    valid = (row >= k) if k > 0 else (row < n + k)
    return jnp.where(valid, rolled, 0.0)


def _round8(v):
    return (v + SUBLANES - 1) // SUBLANES * SUBLANES


def _scan_work_rows(n):
    rows = 0
    while n > SUBLANES:
        g = n // SUBLANES
        rows += 2 * n + 2 * _round8(g) + _round8(g) + 2 * SUBLANES
        n = g
    return rows


def _linear_scan(a_ref, a_off, b_ref, b_off, h_ref, h_off, n, h0, reverse, work, w_off):
    if n <= SUBLANES:
        h = h0
        for r in (range(n - 1, -1, -1) if reverse else range(n)):
            h = a_ref[pl.ds(a_off + r, 1), :] * h + b_ref[pl.ds(b_off + r, 1), :]
            h_ref[pl.ds(h_off + r, 1), :] = h
        return
    g = n // SUBLANES
    gp = _round8(g)
    ca_off = w_off
    cb_off = ca_off + n
    a2_off = cb_off + n
    b2_off = a2_off + gp
    hs_off = b2_off + gp
    next_off = hs_off + gp + 2 * SUBLANES

    a = a_ref[pl.ds(a_off, n), :].reshape(g, SUBLANES, LANES)
    b = b_ref[pl.ds(b_off, n), :].reshape(g, SUBLANES, LANES)
    row = lax.broadcasted_iota(jnp.int32, (g, SUBLANES, LANES), 1)
    for s in (1, 2, 4):
        if reverse:
            a_s = pltpu.roll(a, SUBLANES - s, axis=1)
            b_s = pltpu.roll(b, SUBLANES - s, axis=1)
            valid = row < SUBLANES - s
        else:
            a_s = pltpu.roll(a, s, axis=1)
            b_s = pltpu.roll(b, s, axis=1)
            valid = row >= s
        b = jnp.where(valid, a * b_s + b, b)
        a = jnp.where(valid, a * a_s, a)
    work[pl.ds(ca_off, n), :] = a.reshape(n, LANES)
    work[pl.ds(cb_off, n), :] = b.reshape(n, LANES)

    last = 0 if reverse else SUBLANES - 1
    work[pl.ds(a2_off, g), :] = work[pl.ds(ca_off + last, g, stride=SUBLANES), :]
    work[pl.ds(b2_off, g), :] = work[pl.ds(cb_off + last, g, stride=SUBLANES), :]
    _linear_scan(work, a2_off, work, b2_off, work, hs_off + SUBLANES, g, h0, reverse, work, next_off)
    if reverse:
        work[pl.ds(hs_off + SUBLANES + g, 1), :] = h0
        in_off = hs_off + SUBLANES + 1
    else:
        work[pl.ds(hs_off + SUBLANES - 1, 1), :] = h0
        in_off = hs_off + SUBLANES - 1

    def apply_group(gi, r0):
        h_in = work[pl.ds(in_off + gi, 1), :]
        h_ref[pl.ds(h_off + r0, SUBLANES), :] = (
            work[pl.ds(ca_off + r0, SUBLANES), :] * h_in + work[pl.ds(cb_off + r0, SUBLANES), :])

    if g <= 32:
        for gi in range(g):
            apply_group(gi, gi * SUBLANES)
    else:
        def body(go, carry):
            for u in range(SUBLANES):
                gi = go * SUBLANES + u
                apply_group(gi, pl.multiple_of(gi * SUBLANES, SUBLANES))
            return carry
        lax.fori_loop(0, g // SUBLANES, body, 0)


def _rnn_kernel(xr_ref, xrc_ref, gr_ref, cw_ref, cb_ref, wg_ref, bg_ref, lam_ref, y_ref,
                a_s, b_s, h_s, hsum_s, work, *, n_lat, n_ctx):
    nl = -lam_ref[...]
    sp = jnp.maximum(nl, 0.0) + jnp.log1p(jnp.exp(-jnp.abs(nl)))
    cw = cw_ref[...]
    bias = cb_ref[...]
    wg = wg_ref[0]
    bg = bg_ref[0]

    def conv_gates(x):
        xc = (cw[0:1] * _shift_rows(x, 2) + cw[1:2] * _shift_rows(x, 1) + cw[2:3] * x
              + cw[3:4] * _shift_rows(x, -1)) + bias
        return xc, _dot(xc.astype(BF16), wg) + bg

    def coeffs(xc, gates, d):
        r = jax.nn.sigmoid(gates[:, (2 * d) * LANES:(2 * d + 1) * LANES])
        i = jax.nn.sigmoid(gates[:, (2 * d + 1) * LANES:(2 * d + 2) * LANES])
        log_a = (-RG_C * r) * sp[d:d + 1]
        a = jnp.exp(log_a)
        b = jnp.sqrt(-jnp.tanh(log_a) * (a * a + 1.0)) * (i * xc)
        return a, b

    zero = jnp.zeros((1, LANES), F32)
    xc_c, gates_c = conv_gates(xrc_ref[0])
    h0 = []
    for d in range(2):
        a, b = coeffs(xc_c, gates_c, d)
        a_s[pl.ds(0, n_ctx), :] = a
        b_s[pl.ds(0, n_ctx), :] = b
        _linear_scan(a_s, 0, b_s, 0, h_s, 0, n_ctx, zero, d == 1, work, 0)
        h0.append(h_s[pl.ds(0 if d == 1 else n_ctx - 1, 1), :])

    xc_l, gates_l = conv_gates(xr_ref[0])
    for d in range(2):
        a, b = coeffs(xc_l, gates_l, d)
        a_s[...] = a
        b_s[...] = b
        _linear_scan(a_s, 0, b_s, 0, h_s if d == 1 else hsum_s, 0, n_lat, h0[d], d == 1, work, 0)
    y_ref[0] = jax.nn.gelu(gr_ref[0], approximate=True) * (hsum_s[...] + h_s[...])


def _rnn(xr, xr_c, gr, conv_w, conv_b, wg, bgate, lam):
    bn, n, _ = xr.shape
    n_ctx = xr_c.shape[1]
    seq_spec = pl.BlockSpec((1, n, LANES), lambda b, p: (b, 0, p))
    return pl.pallas_call(
        functools.partial(_rnn_kernel, n_lat=n, n_ctx=n_ctx),
        grid=(bn, N_LANE_GROUPS),
        in_specs=[
            seq_spec,
            pl.BlockSpec((1, n_ctx, LANES), lambda b, p: (b, 0, p)),
            seq_spec,
            pl.BlockSpec((4, LANES), lambda b, p: (0, p)),
            pl.BlockSpec((1, LANES), lambda b, p: (0, p)),
            pl.BlockSpec((1, LANES, 4 * LANES), lambda b, p: (p, 0, 0)),
            pl.BlockSpec((1, 1, 4 * LANES), lambda b, p: (p, 0, 0)),
            pl.BlockSpec((2, LANES), lambda b, p: (0, p)),
        ],
        out_specs=seq_spec,
        out_shape=jax.ShapeDtypeStruct((bn, n, D_RNN), F32),
        scratch_shapes=[pltpu.VMEM((n, LANES), F32)] * 4
        + [pltpu.VMEM((_scan_work_rows(n), LANES), F32)],
        compiler_params=pltpu.CompilerParams(vmem_limit_bytes=VMEM_LIMIT),
        name="rnn",
    )(xr, xr_c, gr, conv_w, conv_b.reshape(1, D_RNN), wg, bgate, lam)


def _gate_weights(rg_wa, rg_ba, rg_wx, rg_bx):
    eye = jnp.eye(2, dtype=F32)
    blocks, biases = [], []
    for d in range(2):
        for w, bvec in ((rg_wa[d], rg_ba[d]), (rg_wx[d], rg_bx[d])):
            w4 = w.reshape(N_LANE_GROUPS, 2, RNN_HEAD_DIM, RNN_HEAD_DIM)
            bd = jnp.einsum("paij,ac->paicj", w4, eye).reshape(N_LANE_GROUPS, LANES, LANES)
            blocks.append(bd)
            biases.append(bvec.reshape(N_LANE_GROUPS, 1, LANES))
    return jnp.concatenate(blocks, axis=-1).astype(BF16), jnp.concatenate(biases, axis=-1)


def _gconv_kernel(u_ref, bg_ref, w_ref, y_ref, *, n):
    p = pl.program_id(1)
    u = u_ref[0]
    w = w_ref[...]

    @pl.when(p < D_CONV_H // LANES)
    def _():
        col = lax.broadcasted_iota(jnp.int32, u.shape, 0) % GRID_W
        left = jnp.where(col > 0, _shift_rows(u, 1), 0.0)
        right = jnp.where(col < GRID_W - 1, _shift_rows(u, -1), 0.0)
        y_ref[0] = bg_ref[0] * (w[0:1] * left + w[1:2] * u + w[2:3] * right)

    @pl.when(p >= D_CONV_H // LANES)
    def _():
        y_ref[0] = bg_ref[0] * (w[0:1] * _shift_rows(u, GRID_W) + w[1:2] * u
                                + w[2:3] * _shift_rows(u, -GRID_W))


def _gconv(u, bg, w):
    bn, n, _ = u.shape
    seq_spec = pl.BlockSpec((1, n, LANES), lambda b, p: (b, 0, p))
    return pl.pallas_call(
        functools.partial(_gconv_kernel, n=n),
        grid=(bn, D_CONV // LANES),
        in_specs=[seq_spec, seq_spec, pl.BlockSpec((3, LANES), lambda b, p: (0, p))],
        out_specs=seq_spec,
        out_shape=jax.ShapeDtypeStruct((bn, n, D_CONV), F32),
        compiler_params=pltpu.CompilerParams(vmem_limit_bytes=VMEM_LIMIT),
        name="gconv",
    )(u, bg, w)


def _lane_max(x, mask):
    return jnp.max(jnp.where(mask, x, -jnp.inf), axis=-1, keepdims=True)


def _first_lane(cond, lane):
    return jnp.min(jnp.where(cond, lane, float(LANES)), axis=-1, keepdims=True)


def _outproj_kernel(x_ref, yr_ref, yc_ref, w_ref, mod_ref, g_ref, wr_hi_ref, wr_lo_ref, br_ref,
                    x1_ref, mt_ref, route_ref, cnt_ref, carry, *, tm):
    first = (pl.program_id(0) == 0) & (pl.program_id(1) == 0)

    @pl.when(first)
    def _():
        carry[...] = jnp.zeros_like(carry)

    mix = _dot(yr_ref[0].astype(BF16), w_ref[0:D_RNN, :]) + _dot(yc_ref[0].astype(BF16), w_ref[D_RNN:, :])
    x1 = x_ref[0] + mod_ref[0, 2:3, :] * mix
    x1_ref[0] = x1
    m = _norm_mod(x1, g_ref[...], mod_ref[0, 4:5, :], mod_ref[0, 3:4, :])
    for s in range(ROW_TILES):
        mt_ref[pl.ds(s, tm, stride=ROW_TILES), :] = m[:, s * LANES:(s + 1) * LANES]

    m_hi, m_lo = _split_bf16(m)
    logits = (_dot(m_hi, wr_hi_ref[...]) + _dot(m_lo, wr_hi_ref[...]) + _dot(m_hi, wr_lo_ref[...])
              + br_ref[...])
    lane_i = lax.broadcasted_iota(jnp.int32, logits.shape, 1)
    lane = lane_i.astype(F32)
    is_grp = lane_i < N_GROUPS
    g_max = _lane_max(logits, is_grp)
    grp = _first_lane(is_grp & (logits == g_max), lane)
    p_g = 1.0 / jnp.sum(jnp.where(is_grp, jnp.exp(logits - g_max), 0.0), axis=-1, keepdims=True)
    lo_lane = EXPERT_LANE0 + grp * EXPERTS_PER_GROUP
    in_grp = (lane >= lo_lane) & (lane < lo_lane + EXPERTS_PER_GROUP)
    l1 = _lane_max(logits, in_grp)
    i1 = _first_lane(in_grp & (logits == l1), lane)
    rest = in_grp & (lane != i1)
    l2 = _lane_max(logits, rest)
    i2 = _first_lane(rest & (logits == l2), lane)
    r21 = jnp.exp(l2 - l1)
    gate1 = p_g / (1.0 + r21)
    gate2 = gate1 * r21

    oh1 = jnp.where(lane == i1, 1.0, 0.0)
    oh2 = jnp.where(lane == i2, 1.0, 0.0)
    both = (oh1 + oh2).astype(BF16)
    ti = lax.broadcasted_iota(jnp.int32, (tm, tm), 0)
    tj = lax.broadcasted_iota(jnp.int32, (tm, tm), 1)
    tri = jnp.where(tj < ti, 1.0, 0.0).astype(BF16)
    before = _dot(tri, both) + carry[...]
    rank1 = jnp.sum(oh1 * before, axis=-1, keepdims=True)
    rank2 = jnp.sum(oh2 * before, axis=-1, keepdims=True)
    total = carry[...] + jnp.sum(oh1 + oh2, axis=0, keepdims=True)
    carry[...] = total
    cnt_ref[...] = total

    e1 = i1 - EXPERT_LANE0
    e2 = i2 - EXPERT_LANE0
    out = jnp.zeros(logits.shape, F32)
    for k, val in enumerate((e1, e2, gate1, gate2, rank1, rank2)):
        out = jnp.where(lane_i == k, val, out)
    route_ref[...] = out


def _outproj(x, y_rnn, y_conv, w_out_bf16, mod3, norm_g, wr_hi, wr_lo, br):
    bn, n, d = x.shape
    tm = min(TOK_TILE, n)
    nt = n // tm
    t_all = bn * n
    return pl.pallas_call(
        functools.partial(_outproj_kernel, tm=tm),
        grid=(bn, nt),
        in_specs=[
            pl.BlockSpec((1, tm, d), lambda b, i: (b, i, 0)),
            pl.BlockSpec((1, tm, D_RNN), lambda b, i: (b, i, 0)),
            pl.BlockSpec((1, tm, D_CONV), lambda b, i: (b, i, 0)),
            pl.BlockSpec((D_RNN + D_CONV, d), lambda b, i: (0, 0)),
            pl.BlockSpec((1, 6, d), lambda b, i: (b, 0, 0)),
            pl.BlockSpec((1, d), lambda b, i: (0, 0)),
            pl.BlockSpec((d, LANES), lambda b, i: (0, 0)),
            pl.BlockSpec((d, LANES), lambda b, i: (0, 0)),
            pl.BlockSpec((1, LANES), lambda b, i: (0, 0)),
        ],
        out_specs=[
            pl.BlockSpec((1, tm, d), lambda b, i: (b, i, 0)),
            pl.BlockSpec((tm * ROW_TILES, LANES), lambda b, i: (b * nt + i, 0)),
            pl.BlockSpec((tm, LANES), lambda b, i: (b * nt + i, 0)),
            pl.BlockSpec((1, LANES), lambda b, i: (0, 0)),
        ],
        out_shape=[
            jax.ShapeDtypeStruct((bn, n, d), F32),
            jax.ShapeDtypeStruct((t_all * ROW_TILES, LANES), F32),
            jax.ShapeDtypeStruct((t_all, LANES), F32),
            jax.ShapeDtypeStruct((1, LANES), F32),
        ],
        scratch_shapes=[pltpu.VMEM((1, LANES), F32)],
        compiler_params=pltpu.CompilerParams(
            dimension_semantics=("arbitrary", "arbitrary"), vmem_limit_bytes=VMEM_LIMIT),
        name="outproj",
    )(x, y_rnn, y_conv, w_out_bf16, mod3, norm_g.reshape(1, d), wr_hi, wr_lo, br)


def _row_tile(ref, row):
    return ref.at[pl.ds(pl.multiple_of(row * ROW_TILES, ROW_TILES), ROW_TILES)]


def _slotmap_kernel(dest_ref, asg_ref):
    def init(c, carry):
        for u in range(DMA_UNROLL):
            asg_ref[c * DMA_UNROLL + u] = 0
        return carry
    lax.fori_loop(0, asg_ref.shape[0] // DMA_UNROLL, init, 0)

    def body(c, carry):
        for u in range(DMA_UNROLL):
            a = c * DMA_UNROLL + u
            asg_ref[dest_ref[a]] = a
        return carry
    lax.fori_loop(0, dest_ref.shape[0] // DMA_UNROLL, body, 0)


def _slotmap(dest, n_slots):
    return pl.pallas_call(
        _slotmap_kernel,
        in_specs=[pl.BlockSpec(memory_space=pltpu.SMEM)],
        out_specs=pl.BlockSpec(memory_space=pltpu.SMEM),
        out_shape=jax.ShapeDtypeStruct((n_slots,), jnp.int32),
        name="slotmap",
    )(dest)


def _expert_kernel(be_ref, nu_ref, asg_hbm, mt_hbm, wg_ref, wu_ref, wd_ref, yb_ref,
                   xbuf, sems, idx, isems, wg_s, wu_s, wd_s):
    j = pl.program_id(0)
    n_used = nu_ref[0]
    last = n_used - 1

    def idx_copy(blk, sl):
        return pltpu.make_async_copy(asg_hbm.at[blk], idx.at[sl], isems.at[sl])

    def row_copy(tok, slot, r):
        return pltpu.make_async_copy(_row_tile(mt_hbm, tok), _row_tile(xbuf.at[slot], r), sems.at[slot])

    def issue_rows(slot, r0, n):
        for u in range(n):
            tok = lax.shift_right_logical(idx[slot, 0, r0 + u], 1)
            row_copy(tok, slot, r0 + u).start(priority=u % 2)

    def drain(slot):
        def body(c, carry):
            for u in range(DMA_UNROLL):
                row_copy(0, slot, 0).wait()
            return carry
        lax.fori_loop(0, MOE_BLK // DMA_UNROLL, body, 0)

    @pl.when(j >= n_used)
    def _():
        yb_ref[...] = jnp.zeros_like(yb_ref)

    @pl.when(j < n_used)
    def _():
        slot = j % EXPERT_RING
        ahead = (j + EXPERT_RING - 1) % EXPERT_RING

        @pl.when(j == 0)
        def _():
            for b in range(EXPERT_RING - 1):
                idx_copy(jnp.minimum(b, last), b).start()
                idx_copy(0, b).wait()

                def body(c, carry, b=b):
                    issue_rows(b, c * DMA_UNROLL, DMA_UNROLL)
                    return carry
                lax.fori_loop(0, MOE_BLK // DMA_UNROLL, body, 0)
            idx_copy(jnp.minimum(EXPERT_RING - 1, last), EXPERT_RING - 1).start()

        e = be_ref[j]
        prev = be_ref[jnp.maximum(j - 1, 0)]

        @pl.when((j == 0) | (e != prev))
        def _():
            wg_s[...] = wg_ref[0].astype(BF16)
            wu_s[...] = wu_ref[0].astype(BF16)
            wd_s[...] = wd_ref[0].astype(BF16)

        drain(slot)
        idx_copy(0, ahead).wait()

        n_pieces = (2 * D_EXPERT // MXU_TILE) * (D_MODEL // MXU_TILE) + (
            D_MODEL // MXU_TILE) * (D_EXPERT // MXU_TILE)
        bounds = [(p * MOE_BLK) // n_pieces for p in range(n_pieces + 1)]
        pieces = iter(zip(bounds[:-1], bounds[1:]))

        def dot_pieces(a, w_ref, n0):
            acc = None
            for k0 in range(0, a.shape[1], MXU_TILE):
                part = _dot(a[:, k0:k0 + MXU_TILE], w_ref[k0:k0 + MXU_TILE, n0:n0 + MXU_TILE])
                acc = part if acc is None else acc + part
                r0, r1 = next(pieces)
                issue_rows(ahead, r0, r1 - r0)
            return acc

        x = jnp.concatenate(
            [xbuf[slot, pl.ds(s, MOE_BLK, stride=ROW_TILES), :] for s in range(ROW_TILES)], axis=-1)
        xb16 = x.astype(BF16)
        acts = []
        for n0 in range(0, D_EXPERT, MXU_TILE):
            gate = dot_pieces(xb16, wg_s, n0)
            up = dot_pieces(xb16, wu_s, n0)
            acts.append(((gate * jax.nn.sigmoid(gate)) * up).astype(BF16))
        h = jnp.concatenate(acts, axis=-1)
        for n0 in range(0, D_MODEL, MXU_TILE):
            y = dot_pieces(h, wd_s, n0)
            for s in range(MXU_TILE // LANES):
                yb_ref[pl.ds(n0 // LANES + s, MOE_BLK, stride=ROW_TILES), :] = (
                    y[:, s * LANES:(s + 1) * LANES])

        @pl.when(j < last)
        def _():
            idx_copy(jnp.minimum(j + EXPERT_RING, last), slot).start()

        @pl.when(j == last)
        def _():
            for b in range(1, EXPERT_RING):
                drain((j + b) % EXPERT_RING)


def _experts(block_e, n_used, slot_asg, mt, w_gate, w_up, w_down, n_blocks):
    def wsel(j, be, nu):
        return (be[jnp.minimum(j, nu[0] - 1)], 0, 0)

    return pl.pallas_call(
        _expert_kernel,
        grid_spec=pltpu.PrefetchScalarGridSpec(
            num_scalar_prefetch=2,
            grid=(n_blocks,),
            in_specs=[
                pl.BlockSpec(memory_space=pl.ANY),
                pl.BlockSpec(memory_space=pl.ANY),
                pl.BlockSpec((1, D_MODEL, D_EXPERT), wsel),
                pl.BlockSpec((1, D_MODEL, D_EXPERT), wsel),
                pl.BlockSpec((1, D_EXPERT, D_MODEL), wsel),
            ],
            out_specs=pl.BlockSpec((MOE_BLK * ROW_TILES, LANES), lambda j, be, nu: (j, 0)),
            scratch_shapes=[
                pltpu.VMEM((EXPERT_RING, MOE_BLK * ROW_TILES, LANES), F32),
                pltpu.SemaphoreType.DMA((EXPERT_RING,)),
                pltpu.SMEM((EXPERT_RING, 1, MOE_BLK), jnp.int32),
                pltpu.SemaphoreType.DMA((EXPERT_RING,)),
                pltpu.VMEM((D_MODEL, D_EXPERT), BF16),
                pltpu.VMEM((D_MODEL, D_EXPERT), BF16),
                pltpu.VMEM((D_EXPERT, D_MODEL), BF16),
            ],
        ),
        out_shape=jax.ShapeDtypeStruct((n_blocks * MOE_BLK * ROW_TILES, LANES), F32),
        compiler_params=pltpu.CompilerParams(
            dimension_semantics=("arbitrary",), vmem_limit_bytes=VMEM_LIMIT),
        name="expert",
    )(block_e, n_used, slot_asg.reshape(n_blocks, 1, MOE_BLK), mt, w_gate, w_up, w_down)


def _combine_kernel(dest_ref, yb_hbm, x1_ref, route_ref, mod_ref, g_ref, o_ref, ybuf, sems, *, tc):
    i = pl.program_id(0)
    slot = i % 2

    def row_copy(d, sl, k, r):
        return pltpu.make_async_copy(_row_tile(yb_hbm, d), _row_tile(ybuf.at[sl, k], r), sems.at[sl])

    def gather(step, sl):
        def issue(c, carry):
            for u in range(DMA_UNROLL):
                r = c * DMA_UNROLL + u
                for k in range(TOP_K):
                    row_copy(dest_ref[TOP_K * (step * tc + r) + k], sl, k, r).start(priority=k)
            return carry
        lax.fori_loop(0, tc // DMA_UNROLL, issue, 0)

    @pl.when(i == 0)
    def _():
        gather(0, 0)

    @pl.when(i + 1 < pl.num_programs(0))
    def _():
        gather(i + 1, 1 - slot)

    def drain(c, carry):
        for u in range(DMA_UNROLL * TOP_K):
            row_copy(0, slot, 0, 0).wait()
        return carry
    lax.fori_loop(0, tc // DMA_UNROLL, drain, 0)

    def rows(k):
        return jnp.concatenate(
            [ybuf[slot, k, pl.ds(s, tc, stride=ROW_TILES), :] for s in range(ROW_TILES)], axis=-1)

    route = route_ref[...]
    y = route[:, 2:3] * rows(0) + route[:, 3:4] * rows(1)
    x2 = x1_ref[...] + mod_ref[0, 5:6, :] * y
    ms = jnp.mean(x2 * x2, axis=-1, keepdims=True)
    o_ref[...] = x2 * lax.rsqrt(ms + NORM_EPS) * g_ref[...]


def _combine(dest, yb, x1_2d, route, mod3, final_g, seq):
    t_all, d = x1_2d.shape
    tc = min(COMBINE_TILE, seq)
    per_seq = seq // tc
    return pl.pallas_call(
        functools.partial(_combine_kernel, tc=tc),
        grid_spec=pltpu.PrefetchScalarGridSpec(
            num_scalar_prefetch=1,
            grid=(t_all // tc,),
            in_specs=[
                pl.BlockSpec(memory_space=pl.ANY),
                pl.BlockSpec((tc, d), lambda i, dest: (i, 0)),
                pl.BlockSpec((tc, LANES), lambda i, dest: (i, 0)),
                pl.BlockSpec((1, 6, d), lambda i, dest: (i // per_seq, 0, 0)),
                pl.BlockSpec((1, d), lambda i, dest: (0, 0)),
            ],
            out_specs=pl.BlockSpec((tc, d), lambda i, dest: (i, 0)),
            scratch_shapes=[
                pltpu.VMEM((2, TOP_K, tc * ROW_TILES, LANES), F32),
                pltpu.SemaphoreType.DMA((2,)),
            ],
        ),
        out_shape=jax.ShapeDtypeStruct((t_all, d), F32),
        compiler_params=pltpu.CompilerParams(
            dimension_semantics=("arbitrary",), vmem_limit_bytes=VMEM_LIMIT),
        name="combine",
    )(dest, yb, x1_2d, route, mod3, final_g.reshape(1, d))


def kernel(x, c, ctx, c_ctx, ada_w, ada_b, norm1_g, norm2_g, w_in, rnn_conv_w, rnn_conv_b, rg_wa, rg_ba,
           rg_wx, rg_bx, rg_lambda, sc_conv_w, w_out, router_group_w, router_group_b, router_exp_w,
           router_exp_b, exp_w_gate, exp_w_up, exp_w_down, final_norm_g):
    bn, seq, d = x.shape
    assert d == D_MODEL and bn < MOD_ROWS and ada_w.shape[0] == 1
    t_all = bn * seq

    cc = jnp.concatenate([c, c_ctx[None], jnp.zeros((MOD_ROWS - bn - 1, d), F32)], axis=0)
    mod3 = _modulation(cc, ada_w[0], ada_b[0]).reshape(MOD_ROWS, 6, d)

    w_in_b = w_in[0].astype(BF16)
    xr, gr, u, bg = _inproj(x, mod3, None, norm1_g[0], w_in_b, latent=True)
    (xr_c,) = _inproj(ctx, mod3, bn, norm1_g[0], w_in_b[:, :D_RNN], latent=False)

    wg, bgate = _gate_weights(rg_wa[0], rg_ba[0], rg_wx[0], rg_bx[0])
    y_rnn = _rnn(xr, xr_c, gr, rnn_conv_w[0], rnn_conv_b[0], wg, bgate, rg_lambda[0])
    y_conv = _gconv(u, bg, sc_conv_w[0])

    wr = jnp.zeros((d, LANES), F32)
    wr = wr.at[:, :N_GROUPS].set(router_group_w[0]).at[:, EXPERT_LANE0:EXPERT_LANE0 + N_EXPERTS].set(router_exp_w[0])
    br = jnp.zeros((1, LANES), F32)
    br = br.at[0, :N_GROUPS].set(router_group_b[0]).at[0, EXPERT_LANE0:EXPERT_LANE0 + N_EXPERTS].set(router_exp_b[0])
    wr_hi, wr_lo = _split_bf16(wr)
    x1, mt, route, cnt = _outproj(x, y_rnn, y_conv, w_out[0].astype(BF16), mod3, norm2_g[0], wr_hi, wr_lo, br)

    n_assign = t_all * TOP_K
    n_blocks = (n_assign + N_EXPERTS * (MOE_BLK - 1) + MOE_BLK - 1) // MOE_BLK
    counts = cnt[0, EXPERT_LANE0:EXPERT_LANE0 + N_EXPERTS].astype(jnp.int32)
    pcounts = (counts + MOE_BLK - 1) // MOE_BLK * MOE_BLK
    pends = jnp.cumsum(pcounts)
    pstarts = pends - pcounts
    experts = route[:, 0:TOP_K].astype(jnp.int32)
    ranks = route[:, 4:4 + TOP_K].astype(jnp.int32)
    onehot = experts[:, :, None] == jnp.arange(N_EXPERTS, dtype=jnp.int32)
    dest = (ranks + jnp.sum(jnp.where(onehot, pstarts, 0), axis=-1)).reshape(n_assign)
    n_used = (pends[-1] // MOE_BLK).astype(jnp.int32)
    blk_start = jnp.arange(n_blocks, dtype=jnp.int32) * MOE_BLK
    block_e = jnp.minimum(jnp.sum(blk_start[:, None] >= pends[None, :], axis=1), N_EXPERTS - 1)
    last_e = jnp.max(jnp.where(counts > 0, jnp.arange(N_EXPERTS, dtype=jnp.int32), 0))
    block_e = jnp.where(blk_start < pends[-1], block_e, last_e).astype(jnp.int32)

    n_slots = n_blocks * MOE_BLK
    slot_asg = _slotmap(dest, n_slots)
    yb = _experts(block_e, n_used.reshape(1), slot_asg, mt, exp_w_gate[0], exp_w_up[0], exp_w_down[0],
                  n_blocks)
    out = _combine(dest, yb, x1.reshape(t_all, d), route, mod3, final_norm_g, seq)
    return out.reshape(bn, seq, d)
```

```python
import functools

import jax
import jax.numpy as jnp
from jax import lax
from jax.experimental import pallas as pl
from jax.experimental.pallas import tpu as pltpu

F32 = jnp.float32
BF16 = jnp.bfloat16

D_MODEL = 1024
D_RNN = 512
D_CONV = 512
D_CONV_H = D_CONV // 2
RNN_HEADS = 8
RNN_HEAD_DIM = D_RNN // RNN_HEADS
GRID_W = 64
RG_C = 8.0
N_GROUPS = 4
EXPERTS_PER_GROUP = 8
N_EXPERTS = N_GROUPS * EXPERTS_PER_GROUP
TOP_K = 2
D_EXPERT = 512
NORM_EPS = 1e-6
F32_TINY = 1.1754944e-38

LANES = 128
SUBLANES = 8
ROW_TILES = D_MODEL // LANES
N_LANE_GROUPS = D_RNN // LANES
EXPERT_LANE0 = N_GROUPS

MOD_ROWS = 16
MOD_TN = 768
TOK_TILE = 512
MOE_BLK = 256
MXU_TILE = 256
EXPERT_RING = 3
COMBINE_TILE = 256
DMA_UNROLL = 16
VMEM_LIMIT = 48 * 1024 * 1024


def _dot(a, b):
    return jnp.dot(a, b, preferred_element_type=F32)


def _split_bf16(x):
    hi = x.astype(BF16)
    lo = (x - hi.astype(F32)).astype(BF16)
    return hi, lo


def _mod_kernel(cc_ref, w_ref, b_ref, o_ref):
    s = cc_ref[...]
    s = s * jax.nn.sigmoid(s)
    s_hi, s_lo = _split_bf16(s)
    w_hi, w_lo = _split_bf16(w_ref[...])
    o_ref[...] = _dot(s_hi, w_hi) + _dot(s_lo, w_hi) + _dot(s_hi, w_lo) + b_ref[...]


def _modulation(cc, ada_w, ada_b):
    n = ada_w.shape[1]
    return pl.pallas_call(
        _mod_kernel,
        grid=(n // MOD_TN,),
        in_specs=[
            pl.BlockSpec((MOD_ROWS, D_MODEL), lambda j: (0, 0)),
            pl.BlockSpec((D_MODEL, MOD_TN), lambda j: (0, j)),
            pl.BlockSpec((1, MOD_TN), lambda j: (0, j)),
        ],
        out_specs=pl.BlockSpec((MOD_ROWS, MOD_TN), lambda j: (0, j)),
        out_shape=jax.ShapeDtypeStruct((MOD_ROWS, n), F32),
        compiler_params=pltpu.CompilerParams(vmem_limit_bytes=VMEM_LIMIT),
        name="mod",
    )(cc, ada_w, ada_b.reshape(1, n))


def _norm_mod(x, g, scale, shift):
    ms = jnp.mean(x * x, axis=-1, keepdims=True)
    y = x * lax.rsqrt(ms + NORM_EPS) * g
    return y * (1.0 + scale) + shift


def _inproj_kernel(x_ref, mod_ref, g_ref, w_ref, *out_refs, latent):
    h = _norm_mod(x_ref[0], g_ref[...], mod_ref[0, 1:2, :], mod_ref[0, 0:1, :])
    hb = h.astype(BF16)
    xr = _dot(hb, w_ref[:, 0:D_RNN])
    out_refs[0][0] = xr
    if latent:
        o = D_RNN
        out_refs[1][0] = _dot(hb, w_ref[:, o:o + D_RNN])
        o += D_RNN
        v = _dot(hb, w_ref[:, o:o + D_CONV])
        out_refs[3][0] = _dot(hb, w_ref[:, o + D_CONV:o + 2 * D_CONV])
        cg = _dot(hb, w_ref[:, o + 2 * D_CONV:o + 3 * D_CONV])
        out_refs[2][0] = cg * v


def _inproj(x, mod3, mod_row, norm_g, w_bf16, latent):
    bn, n, d = x.shape
    tm = min(TOK_TILE, n)
    n_out = 4 if latent else 1
    width = w_bf16.shape[1]
    mod_map = (lambda b, i: (b, 0, 0)) if mod_row is None else (lambda b, i: (mod_row, 0, 0))
    return pl.pallas_call(
        functools.partial(_inproj_kernel, latent=latent),
        grid=(bn, n // tm),
        in_specs=[
            pl.BlockSpec((1, tm, d), lambda b, i: (b, i, 0)),
            pl.BlockSpec((1, 6, d), mod_map),
            pl.BlockSpec((1, d), lambda b, i: (0, 0)),
            pl.BlockSpec((d, width), lambda b, i: (0, 0)),
        ],
        out_specs=[pl.BlockSpec((1, tm, D_RNN), lambda b, i: (b, i, 0))] * n_out,
        out_shape=[jax.ShapeDtypeStruct((bn, n, D_RNN), F32)] * n_out,
        compiler_params=pltpu.CompilerParams(vmem_limit_bytes=VMEM_LIMIT),
        name="inproj_lat" if latent else "inproj_ctx",
    )(x, mod3, norm_g.reshape(1, d), w_bf16)


def _shift_rows(x, k):
    n = x.shape[0]
    row = lax.broadcasted_iota(jnp.int32, x.shape, 0)
    rolled = pltpu.roll(x, k % n, axis=0)
    valid = (row >= k) if k > 0 else (row < n + k)
    return jnp.where(valid, rolled, 0.0)


def _scan_pitch(chunk):
    pitch = chunk + SUBLANES
    return pitch if (pitch // SUBLANES) % 2 else pitch + SUBLANES


def _rnn_kernel(xr_ref, xrc_ref, gr_ref, cw_ref, cb_ref, wg_ref, bg_ref, lam_ref, y_ref,
                xc_s, ap_f, bp_f, ap_b, bp_b, hl_f, al_f, hl_b, al_b, hp_f, hp_b, *, n_lat, n_ctx):
    nl = -lam_ref[...]
    sp = jnp.maximum(nl, 0.0) + jnp.log1p(jnp.exp(-jnp.abs(nl)))
    c1 = (-0.5 * RG_C) * sp
    cw = cw_ref[...]
    bias = cb_ref[...]
    wg = wg_ref[0]
    bg = bg_ref[0]
    dirs = ((ap_f, bp_f, hl_f, al_f, hp_f), (ap_b, bp_b, hl_b, al_b, hp_b))

    def conv_into(x, n):
        xc_s[pl.ds(0, n), :] = (cw[0:1] * _shift_rows(x, 2) + cw[1:2] * _shift_rows(x, 1)
                                + cw[2:3] * x + cw[3:4] * _shift_rows(x, -1)) + bias

    def coefficients(n):
        chunk = n // SUBLANES
        pitch = _scan_pitch(chunk)

        def body(c, carry):
            xc = xc_s[pl.ds(pl.multiple_of(c * chunk, SUBLANES), chunk), :]
            dst = pl.multiple_of(c * pitch, SUBLANES)
            gates = _dot(xc.astype(BF16), wg) + bg
            half_xc = 0.5 * xc
            for d in range(2):
                tr = jnp.tanh(0.5 * gates[:, (2 * d) * LANES:(2 * d + 1) * LANES])
                ti = jnp.tanh(0.5 * gates[:, (2 * d + 1) * LANES:(2 * d + 2) * LANES])
                log_a = c1[d:d + 1] + c1[d:d + 1] * tr
                a = jnp.exp(log_a)
                y = -jnp.tanh(log_a) * (a * a + 1.0)
                root = y * lax.rsqrt(jnp.maximum(y, F32_TINY))
                dirs[d][0][pl.ds(dst, chunk), :] = a
                dirs[d][1][pl.ds(dst, chunk), :] = root * (half_xc + half_xc * ti)
            return carry
        lax.fori_loop(0, SUBLANES, body, 0)

    def scan(n, h0_f, h0_b, keep):
        chunk = n // SUBLANES
        pitch = _scan_pitch(chunk)

        def steps(jo, carry):
            h_f, a_f, h_b, a_b = carry
            for u in range(SUBLANES):
                j = jo * SUBLANES + u
                av = ap_f[pl.ds(j, SUBLANES, stride=pitch), :]
                h_f = av * h_f + bp_f[pl.ds(j, SUBLANES, stride=pitch), :]
                a_f = av * a_f
                jb = chunk - 1 - j
                av = ap_b[pl.ds(jb, SUBLANES, stride=pitch), :]
                h_b = av * h_b + bp_b[pl.ds(jb, SUBLANES, stride=pitch), :]
                a_b = av * a_b
                if keep:
                    o = pl.multiple_of(j * SUBLANES, SUBLANES)
                    hl_f[pl.ds(o, SUBLANES), :] = h_f
                    al_f[pl.ds(o, SUBLANES), :] = a_f
                    hl_b[pl.ds(o, SUBLANES), :] = h_b
                    al_b[pl.ds(o, SUBLANES), :] = a_b
            return h_f, a_f, h_b, a_b

        zeros = jnp.zeros((SUBLANES, LANES), F32)
        ones = jnp.ones((SUBLANES, LANES), F32)
        h_f, a_f, h_b, a_b = lax.fori_loop(0, chunk // SUBLANES, steps, (zeros, ones, zeros, ones))

        in_f = [h0_f]
        for c in range(SUBLANES):
            in_f.append(a_f[c:c + 1] * in_f[c] + h_f[c:c + 1])
        in_b = [h0_b]
        for c in range(SUBLANES - 1, -1, -1):
            in_b.append(a_b[c:c + 1] * in_b[-1] + h_b[c:c + 1])
        if keep:
            hin_f = jnp.concatenate(in_f[:SUBLANES], axis=0)
            hin_b = jnp.concatenate(in_b[SUBLANES - 1::-1], axis=0)

            def fix(jo, carry):
                for u in range(SUBLANES):
                    j = jo * SUBLANES + u
                    o = pl.multiple_of(j * SUBLANES, SUBLANES)
                    hp_f[pl.ds(j, SUBLANES, stride=pitch), :] = (
                        hl_f[pl.ds(o, SUBLANES), :] + al_f[pl.ds(o, SUBLANES), :] * hin_f)
                    hp_b[pl.ds(chunk - 1 - j, SUBLANES, stride=pitch), :] = (
                        hl_b[pl.ds(o, SUBLANES), :] + al_b[pl.ds(o, SUBLANES), :] * hin_b)
                return carry
            lax.fori_loop(0, chunk // SUBLANES, fix, 0)
        return in_f[SUBLANES], in_b[SUBLANES]

    zero = jnp.zeros((1, LANES), F32)
    conv_into(xrc_ref[0], n_ctx)
    coefficients(n_ctx)
    h0_f, h0_b = scan(n_ctx, zero, zero, keep=False)

    conv_into(xr_ref[0], n_lat)
    coefficients(n_lat)
    scan(n_lat, h0_f, h0_b, keep=True)

    chunk = n_lat // SUBLANES
    pitch = _scan_pitch(chunk)

    def emit(c, carry):
        src = pl.multiple_of(c * chunk, SUBLANES)
        dst = pl.multiple_of(c * pitch, SUBLANES)
        hsum = hp_f[pl.ds(dst, chunk), :] + hp_b[pl.ds(dst, chunk), :]
        y_ref[0, pl.ds(src, chunk), :] = jax.nn.gelu(gr_ref[0, pl.ds(src, chunk), :], approximate=True) * hsum
        return carry
    lax.fori_loop(0, SUBLANES, emit, 0)


def _rnn(xr, xr_c, gr, conv_w, conv_b, wg, bgate, lam):
    bn, n, _ = xr.shape
    n_ctx = xr_c.shape[1]
    assert n % (SUBLANES * SUBLANES) == 0 and n_ctx % (SUBLANES * SUBLANES) == 0 and n_ctx <= n
    pitched = SUBLANES * _scan_pitch(n // SUBLANES)
    seq_spec = pl.BlockSpec((1, n, LANES), lambda b, p: (b, 0, p))
    return pl.pallas_call(
        functools.partial(_rnn_kernel, n_lat=n, n_ctx=n_ctx),
        grid=(bn, N_LANE_GROUPS),
        in_specs=[
            seq_spec,
            pl.BlockSpec((1, n_ctx, LANES), lambda b, p: (b, 0, p)),
            seq_spec,
            pl.BlockSpec((4, LANES), lambda b, p: (0, p)),
            pl.BlockSpec((1, LANES), lambda b, p: (0, p)),
            pl.BlockSpec((1, LANES, 4 * LANES), lambda b, p: (p, 0, 0)),
            pl.BlockSpec((1, 1, 4 * LANES), lambda b, p: (p, 0, 0)),
            pl.BlockSpec((2, LANES), lambda b, p: (0, p)),
        ],
        out_specs=seq_spec,
        out_shape=jax.ShapeDtypeStruct((bn, n, D_RNN), F32),
        scratch_shapes=[pltpu.VMEM((n, LANES), F32)]
        + [pltpu.VMEM((pitched, LANES), F32)] * 4
        + [pltpu.VMEM((n, LANES), F32)] * 4
        + [pltpu.VMEM((pitched, LANES), F32)] * 2,
        compiler_params=pltpu.CompilerParams(vmem_limit_bytes=VMEM_LIMIT),
        name="rnn",
    )(xr, xr_c, gr, conv_w, conv_b.reshape(1, D_RNN), wg, bgate, lam)


def _gate_weights(rg_wa, rg_ba, rg_wx, rg_bx):
    eye = jnp.eye(2, dtype=F32)
    blocks, biases = [], []
    for d in range(2):
        for w, bvec in ((rg_wa[d], rg_ba[d]), (rg_wx[d], rg_bx[d])):
            w4 = w.reshape(N_LANE_GROUPS, 2, RNN_HEAD_DIM, RNN_HEAD_DIM)
            bd = jnp.einsum("paij,ac->paicj", w4, eye).reshape(N_LANE_GROUPS, LANES, LANES)
            blocks.append(bd)
            biases.append(bvec.reshape(N_LANE_GROUPS, 1, LANES))
    return jnp.concatenate(blocks, axis=-1).astype(BF16), jnp.concatenate(biases, axis=-1)


def _gconv_kernel(u_ref, bg_ref, w_ref, y_ref, *, n):
    p = pl.program_id(1)
    u = u_ref[0]
    w = w_ref[...]

    @pl.when(p < D_CONV_H // LANES)
    def _():
        col = lax.broadcasted_iota(jnp.int32, u.shape, 0) % GRID_W
        left = jnp.where(col > 0, _shift_rows(u, 1), 0.0)
        right = jnp.where(col < GRID_W - 1, _shift_rows(u, -1), 0.0)
        y_ref[0] = bg_ref[0] * (w[0:1] * left + w[1:2] * u + w[2:3] * right)

    @pl.when(p >= D_CONV_H // LANES)
    def _():
        y_ref[0] = bg_ref[0] * (w[0:1] * _shift_rows(u, GRID_W) + w[1:2] * u
                                + w[2:3] * _shift_rows(u, -GRID_W))


def _gconv(u, bg, w):
    bn, n, _ = u.shape
    seq_spec = pl.BlockSpec((1, n, LANES), lambda b, p: (b, 0, p))
    return pl.pallas_call(
        functools.partial(_gconv_kernel, n=n),
        grid=(bn, D_CONV // LANES),
        in_specs=[seq_spec, seq_spec, pl.BlockSpec((3, LANES), lambda b, p: (0, p))],
        out_specs=seq_spec,
        out_shape=jax.ShapeDtypeStruct((bn, n, D_CONV), F32),
        compiler_params=pltpu.CompilerParams(vmem_limit_bytes=VMEM_LIMIT),
        name="gconv",
    )(u, bg, w)


def _lane_max(x, mask):
    return jnp.max(jnp.where(mask, x, -jnp.inf), axis=-1, keepdims=True)


def _first_lane(cond, lane):
    return jnp.min(jnp.where(cond, lane, float(LANES)), axis=-1, keepdims=True)


def _outproj_kernel(x_ref, yr_ref, yc_ref, w_ref, mod_ref, g_ref, wr_hi_ref, wr_lo_ref, br_ref,
                    x1_ref, mt_ref, route_ref, cnt_ref, carry, *, tm):
    first = (pl.program_id(0) == 0) & (pl.program_id(1) == 0)

    @pl.when(first)
    def _():
        carry[...] = jnp.zeros_like(carry)

    mix = _dot(yr_ref[0].astype(BF16), w_ref[0:D_RNN, :]) + _dot(yc_ref[0].astype(BF16), w_ref[D_RNN:, :])
    x1 = x_ref[0] + mod_ref[0, 2:3, :] * mix
    x1_ref[0] = x1
    m = _norm_mod(x1, g_ref[...], mod_ref[0, 4:5, :], mod_ref[0, 3:4, :])
    for s in range(ROW_TILES):
        mt_ref[pl.ds(s, tm, stride=ROW_TILES), :] = m[:, s * LANES:(s + 1) * LANES]

    m_hi, m_lo = _split_bf16(m)
    logits = (_dot(m_hi, wr_hi_ref[...]) + _dot(m_lo, wr_hi_ref[...]) + _dot(m_hi, wr_lo_ref[...])
              + br_ref[...])
    lane_i = lax.broadcasted_iota(jnp.int32, logits.shape, 1)
    lane = lane_i.astype(F32)
    is_grp = lane_i < N_GROUPS
    g_max = _lane_max(logits, is_grp)
    grp = _first_lane(is_grp & (logits == g_max), lane)
    p_g = 1.0 / jnp.sum(jnp.where(is_grp, jnp.exp(logits - g_max), 0.0), axis=-1, keepdims=True)
    lo_lane = EXPERT_LANE0 + grp * EXPERTS_PER_GROUP
    in_grp = (lane >= lo_lane) & (lane < lo_lane + EXPERTS_PER_GROUP)
    l1 = _lane_max(logits, in_grp)
    i1 = _first_lane(in_grp & (logits == l1), lane)
    rest = in_grp & (lane != i1)
    l2 = _lane_max(logits, rest)
    i2 = _first_lane(rest & (logits == l2), lane)
    r21 = jnp.exp(l2 - l1)
    gate1 = p_g / (1.0 + r21)
    gate2 = gate1 * r21

    oh1 = jnp.where(lane == i1, 1.0, 0.0)
    oh2 = jnp.where(lane == i2, 1.0, 0.0)
    both = (oh1 + oh2).astype(BF16)
    ti = lax.broadcasted_iota(jnp.int32, (tm, tm), 0)
    tj = lax.broadcasted_iota(jnp.int32, (tm, tm), 1)
    tri = jnp.where(tj < ti, 1.0, 0.0).astype(BF16)
    before = _dot(tri, both) + carry[...]
    rank1 = jnp.sum(oh1 * before, axis=-1, keepdims=True)
    rank2 = jnp.sum(oh2 * before, axis=-1, keepdims=True)
    total = carry[...] + jnp.sum(oh1 + oh2, axis=0, keepdims=True)
    carry[...] = total
    cnt_ref[...] = total

    e1 = i1 - EXPERT_LANE0
    e2 = i2 - EXPERT_LANE0
    out = jnp.zeros(logits.shape, F32)
    for k, val in enumerate((e1, e2, gate1, gate2, rank1, rank2)):
        out = jnp.where(lane_i == k, val, out)
    route_ref[...] = out


def _outproj(x, y_rnn, y_conv, w_out_bf16, mod3, norm_g, wr_hi, wr_lo, br):
    bn, n, d = x.shape
    tm = min(TOK_TILE, n)
    nt = n // tm
    t_all = bn * n
    return pl.pallas_call(
        functools.partial(_outproj_kernel, tm=tm),
        grid=(bn, nt),
        in_specs=[
            pl.BlockSpec((1, tm, d), lambda b, i: (b, i, 0)),
            pl.BlockSpec((1, tm, D_RNN), lambda b, i: (b, i, 0)),
            pl.BlockSpec((1, tm, D_CONV), lambda b, i: (b, i, 0)),
            pl.BlockSpec((D_RNN + D_CONV, d), lambda b, i: (0, 0)),
            pl.BlockSpec((1, 6, d), lambda b, i: (b, 0, 0)),
            pl.BlockSpec((1, d), lambda b, i: (0, 0)),
            pl.BlockSpec((d, LANES), lambda b, i: (0, 0)),
            pl.BlockSpec((d, LANES), lambda b, i: (0, 0)),
            pl.BlockSpec((1, LANES), lambda b, i: (0, 0)),
        ],
        out_specs=[
            pl.BlockSpec((1, tm, d), lambda b, i: (b, i, 0)),
            pl.BlockSpec((tm * ROW_TILES, LANES), lambda b, i: (b * nt + i, 0)),
            pl.BlockSpec((tm, LANES), lambda b, i: (b * nt + i, 0)),
            pl.BlockSpec((1, LANES), lambda b, i: (0, 0)),
        ],
        out_shape=[
            jax.ShapeDtypeStruct((bn, n, d), F32),
            jax.ShapeDtypeStruct((t_all * ROW_TILES, LANES), F32),
            jax.ShapeDtypeStruct((t_all, LANES), F32),
            jax.ShapeDtypeStruct((1, LANES), F32),
        ],
        scratch_shapes=[pltpu.VMEM((1, LANES), F32)],
        compiler_params=pltpu.CompilerParams(
            dimension_semantics=("arbitrary", "arbitrary"), vmem_limit_bytes=VMEM_LIMIT),
        name="outproj",
    )(x, y_rnn, y_conv, w_out_bf16, mod3, norm_g.reshape(1, d), wr_hi, wr_lo, br)


def _row_tile(ref, row):
    return ref.at[pl.ds(pl.multiple_of(row * ROW_TILES, ROW_TILES), ROW_TILES)]


def _slotmap_kernel(dest_ref, zeros_hbm, asg_ref, sem):
    fill = pltpu.make_async_copy(zeros_hbm, asg_ref, sem)
    fill.start()
    fill.wait()

    def body(c, carry):
        for u in range(DMA_UNROLL):
            a = c * DMA_UNROLL + u
            asg_ref[dest_ref[a]] = a
        return carry
    lax.fori_loop(0, dest_ref.shape[0] // DMA_UNROLL, body, 0)


def _slotmap(dest, n_slots):
    return pl.pallas_call(
        _slotmap_kernel,
        in_specs=[pl.BlockSpec(memory_space=pltpu.SMEM), pl.BlockSpec(memory_space=pl.ANY)],
        out_specs=pl.BlockSpec(memory_space=pltpu.SMEM),
        out_shape=jax.ShapeDtypeStruct((n_slots,), jnp.int32),
        scratch_shapes=[pltpu.SemaphoreType.DMA],
        name="slotmap",
    )(dest, jnp.zeros((n_slots,), jnp.int32))


def _expert_kernel(be_ref, nu_ref, asg_hbm, mt_hbm, wg_ref, wu_ref, wd_ref, yb_ref,
                   xbuf, sems, idx, isems, wg_s, wu_s, wd_s):
    j = pl.program_id(0)
    n_used = nu_ref[0]
    last = n_used - 1

    def idx_copy(blk, sl):
        return pltpu.make_async_copy(asg_hbm.at[blk], idx.at[sl], isems.at[sl])

    def row_copy(tok, slot, r):
        return pltpu.make_async_copy(_row_tile(mt_hbm, tok), _row_tile(xbuf.at[slot], r), sems.at[slot])

    def issue_rows(slot, r0, n):
        for u in range(n):
            tok = lax.shift_right_logical(idx[slot, 0, r0 + u], 1)
            row_copy(tok, slot, r0 + u).start(priority=u % 2)

    def drain(slot):
        def body(c, carry):
            for u in range(DMA_UNROLL):
                row_copy(0, slot, 0).wait()
            return carry
        lax.fori_loop(0, MOE_BLK // DMA_UNROLL, body, 0)

    @pl.when(j >= n_used)
    def _():
        yb_ref[...] = jnp.zeros_like(yb_ref)

    @pl.when(j < n_used)
    def _():
        slot = j % EXPERT_RING
        ahead = (j + EXPERT_RING - 1) % EXPERT_RING

        @pl.when(j == 0)
        def _():
            for b in range(EXPERT_RING - 1):
                idx_copy(jnp.minimum(b, last), b).start()
                idx_copy(0, b).wait()

                def body(c, carry, b=b):
                    issue_rows(b, c * DMA_UNROLL, DMA_UNROLL)
                    return carry
                lax.fori_loop(0, MOE_BLK // DMA_UNROLL, body, 0)
            idx_copy(jnp.minimum(EXPERT_RING - 1, last), EXPERT_RING - 1).start()

        e = be_ref[j]
        prev = be_ref[jnp.maximum(j - 1, 0)]

        @pl.when((j == 0) | (e != prev))
        def _():
            wg_s[...] = wg_ref[0].astype(BF16)
            wu_s[...] = wu_ref[0].astype(BF16)
            wd_s[...] = wd_ref[0].astype(BF16)

        drain(slot)
        idx_copy(0, ahead).wait()

        n_pieces = (2 * D_EXPERT // MXU_TILE) * (D_MODEL // MXU_TILE) + (
            D_MODEL // MXU_TILE) * (D_EXPERT // MXU_TILE)
        bounds = [(p * MOE_BLK) // n_pieces for p in range(n_pieces + 1)]
        pieces = iter(zip(bounds[:-1], bounds[1:]))

        def dot_pieces(a, w_ref, n0):
            acc = None
            for k0 in range(0, a.shape[1], MXU_TILE):
                part = _dot(a[:, k0:k0 + MXU_TILE], w_ref[k0:k0 + MXU_TILE, n0:n0 + MXU_TILE])
                acc = part if acc is None else acc + part
                r0, r1 = next(pieces)
                issue_rows(ahead, r0, r1 - r0)
            return acc

        x = jnp.concatenate(
            [xbuf[slot, pl.ds(s, MOE_BLK, stride=ROW_TILES), :] for s in range(ROW_TILES)], axis=-1)
        xb16 = x.astype(BF16)
        acts = []
        for n0 in range(0, D_EXPERT, MXU_TILE):
            gate = dot_pieces(xb16, wg_s, n0)
            up = dot_pieces(xb16, wu_s, n0)
            acts.append(((gate * jax.nn.sigmoid(gate)) * up).astype(BF16))
        h = jnp.concatenate(acts, axis=-1)
        for n0 in range(0, D_MODEL, MXU_TILE):
            y = dot_pieces(h, wd_s, n0)
            for s in range(MXU_TILE // LANES):
                yb_ref[pl.ds(n0 // LANES + s, MOE_BLK, stride=ROW_TILES), :] = (
                    y[:, s * LANES:(s + 1) * LANES])

        @pl.when(j < last)
        def _():
            idx_copy(jnp.minimum(j + EXPERT_RING, last), slot).start()

        @pl.when(j == last)
        def _():
            for b in range(1, EXPERT_RING):
                drain((j + b) % EXPERT_RING)


def _experts(block_e, n_used, slot_asg, mt, w_gate, w_up, w_down, n_blocks):
    def wsel(j, be, nu):
        return (be[jnp.minimum(j, nu[0] - 1)], 0, 0)

    return pl.pallas_call(
        _expert_kernel,
        grid_spec=pltpu.PrefetchScalarGridSpec(
            num_scalar_prefetch=2,
            grid=(n_blocks,),
            in_specs=[
                pl.BlockSpec(memory_space=pl.ANY),
                pl.BlockSpec(memory_space=pl.ANY),
                pl.BlockSpec((1, D_MODEL, D_EXPERT), wsel),
                pl.BlockSpec((1, D_MODEL, D_EXPERT), wsel),
                pl.BlockSpec((1, D_EXPERT, D_MODEL), wsel),
            ],
            out_specs=pl.BlockSpec((MOE_BLK * ROW_TILES, LANES), lambda j, be, nu: (j, 0)),
            scratch_shapes=[
                pltpu.VMEM((EXPERT_RING, MOE_BLK * ROW_TILES, LANES), F32),
                pltpu.SemaphoreType.DMA((EXPERT_RING,)),
                pltpu.SMEM((EXPERT_RING, 1, MOE_BLK), jnp.int32),
                pltpu.SemaphoreType.DMA((EXPERT_RING,)),
                pltpu.VMEM((D_MODEL, D_EXPERT), BF16),
                pltpu.VMEM((D_MODEL, D_EXPERT), BF16),
                pltpu.VMEM((D_EXPERT, D_MODEL), BF16),
            ],
        ),
        out_shape=jax.ShapeDtypeStruct((n_blocks * MOE_BLK * ROW_TILES, LANES), F32),
        compiler_params=pltpu.CompilerParams(
            dimension_semantics=("arbitrary",), vmem_limit_bytes=VMEM_LIMIT),
        name="expert",
    )(block_e, n_used, slot_asg.reshape(n_blocks, 1, MOE_BLK), mt, w_gate, w_up, w_down)


def _combine_kernel(dest_ref, yb_hbm, x1_ref, route_ref, mod_ref, g_ref, o_ref, ybuf, sems, *, tc):
    i = pl.program_id(0)
    slot = i % 2

    def row_copy(d, sl, k, r):
        return pltpu.make_async_copy(_row_tile(yb_hbm, d), _row_tile(ybuf.at[sl, k], r), sems.at[sl])

    def gather(step, sl):
        def issue(c, carry):
            for u in range(DMA_UNROLL):
                r = c * DMA_UNROLL + u
                for k in range(TOP_K):
                    row_copy(dest_ref[TOP_K * (step * tc + r) + k], sl, k, r).start(priority=k)
            return carry
        lax.fori_loop(0, tc // DMA_UNROLL, issue, 0)

    @pl.when(i == 0)
    def _():
        gather(0, 0)

    @pl.when(i + 1 < pl.num_programs(0))
    def _():
        gather(i + 1, 1 - slot)

    def drain(c, carry):
        for u in range(DMA_UNROLL * TOP_K):
            row_copy(0, slot, 0, 0).wait()
        return carry
    lax.fori_loop(0, tc // DMA_UNROLL, drain, 0)

    def rows(k):
        return jnp.concatenate(
            [ybuf[slot, k, pl.ds(s, tc, stride=ROW_TILES), :] for s in range(ROW_TILES)], axis=-1)

    route = route_ref[...]
    y = route[:, 2:3] * rows(0) + route[:, 3:4] * rows(1)
    x2 = x1_ref[...] + mod_ref[0, 5:6, :] * y
    ms = jnp.mean(x2 * x2, axis=-1, keepdims=True)
    o_ref[...] = x2 * lax.rsqrt(ms + NORM_EPS) * g_ref[...]


def _combine(dest, yb, x1_2d, route, mod3, final_g, seq):
    t_all, d = x1_2d.shape
    tc = min(COMBINE_TILE, seq)
    per_seq = seq // tc
    return pl.pallas_call(
        functools.partial(_combine_kernel, tc=tc),
        grid_spec=pltpu.PrefetchScalarGridSpec(
            num_scalar_prefetch=1,
            grid=(t_all // tc,),
            in_specs=[
                pl.BlockSpec(memory_space=pl.ANY),
                pl.BlockSpec((tc, d), lambda i, dest: (i, 0)),
                pl.BlockSpec((tc, LANES), lambda i, dest: (i, 0)),
                pl.BlockSpec((1, 6, d), lambda i, dest: (i // per_seq, 0, 0)),
                pl.BlockSpec((1, d), lambda i, dest: (0, 0)),
            ],
            out_specs=pl.BlockSpec((tc, d), lambda i, dest: (i, 0)),
            scratch_shapes=[
                pltpu.VMEM((2, TOP_K, tc * ROW_TILES, LANES), F32),
                pltpu.SemaphoreType.DMA((2,)),
            ],
        ),
        out_shape=jax.ShapeDtypeStruct((t_all, d), F32),
        compiler_params=pltpu.CompilerParams(
            dimension_semantics=("arbitrary",), vmem_limit_bytes=VMEM_LIMIT),
        name="combine",
    )(dest, yb, x1_2d, route, mod3, final_g.reshape(1, d))


def kernel(x, c, ctx, c_ctx, ada_w, ada_b, norm1_g, norm2_g, w_in, rnn_conv_w, rnn_conv_b, rg_wa, rg_ba,
           rg_wx, rg_bx, rg_lambda, sc_conv_w, w_out, router_group_w, router_group_b, router_exp_w,
           router_exp_b, exp_w_gate, exp_w_up, exp_w_down, final_norm_g):
    bn, seq, d = x.shape
    assert d == D_MODEL and bn < MOD_ROWS and ada_w.shape[0] == 1
    t_all = bn * seq

    cc = jnp.concatenate([c, c_ctx[None], jnp.zeros((MOD_ROWS - bn - 1, d), F32)], axis=0)
    mod3 = _modulation(cc, ada_w[0], ada_b[0]).reshape(MOD_ROWS, 6, d)

    w_in_b = w_in[0].astype(BF16)
    xr, gr, u, bg = _inproj(x, mod3, None, norm1_g[0], w_in_b, latent=True)
    (xr_c,) = _inproj(ctx, mod3, bn, norm1_g[0], w_in_b[:, :D_RNN], latent=False)

    wg, bgate = _gate_weights(rg_wa[0], rg_ba[0], rg_wx[0], rg_bx[0])
    y_rnn = _rnn(xr, xr_c, gr, rnn_conv_w[0], rnn_conv_b[0], wg, bgate, rg_lambda[0])
    y_conv = _gconv(u, bg, sc_conv_w[0])

    wr = jnp.zeros((d, LANES), F32)
    wr = wr.at[:, :N_GROUPS].set(router_group_w[0]).at[:, EXPERT_LANE0:EXPERT_LANE0 + N_EXPERTS].set(router_exp_w[0])
    br = jnp.zeros((1, LANES), F32)
    br = br.at[0, :N_GROUPS].set(router_group_b[0]).at[0, EXPERT_LANE0:EXPERT_LANE0 + N_EXPERTS].set(router_exp_b[0])
    wr_hi, wr_lo = _split_bf16(wr)
    x1, mt, route, cnt = _outproj(x, y_rnn, y_conv, w_out[0].astype(BF16), mod3, norm2_g[0], wr_hi, wr_lo, br)

    n_assign = t_all * TOP_K
    n_blocks = (n_assign + N_EXPERTS * (MOE_BLK - 1) + MOE_BLK - 1) // MOE_BLK
    counts = cnt[0, EXPERT_LANE0:EXPERT_LANE0 + N_EXPERTS].astype(jnp.int32)
    pcounts = (counts + MOE_BLK - 1) // MOE_BLK * MOE_BLK
    pends = jnp.cumsum(pcounts)
    pstarts = pends - pcounts
    experts = route[:, 0:TOP_K].astype(jnp.int32)
    ranks = route[:, 4:4 + TOP_K].astype(jnp.int32)
    onehot = experts[:, :, None] == jnp.arange(N_EXPERTS, dtype=jnp.int32)
    dest = (ranks + jnp.sum(jnp.where(onehot, pstarts, 0), axis=-1)).reshape(n_assign)
    n_used = (pends[-1] // MOE_BLK).astype(jnp.int32)
    blk_start = jnp.arange(n_blocks, dtype=jnp.int32) * MOE_BLK
    block_e = jnp.minimum(jnp.sum(blk_start[:, None] >= pends[None, :], axis=1), N_EXPERTS - 1)
    last_e = jnp.max(jnp.where(counts > 0, jnp.arange(N_EXPERTS, dtype=jnp.int32), 0))
    block_e = jnp.where(blk_start < pends[-1], block_e, last_e).astype(jnp.int32)

    n_slots = n_blocks * MOE_BLK
    slot_asg = _slotmap(dest, n_slots)
    yb = _experts(block_e, n_used.reshape(1), slot_asg, mt, exp_w_gate[0], exp_w_up[0], exp_w_down[0],
                  n_blocks)
    out = _combine(dest, yb, x1.reshape(t_all, d), route, mod3, final_norm_g, seq)
    return out.reshape(bn, seq, d)
```

```python
import functools

import jax
import jax.numpy as jnp
from jax import lax
from jax.experimental import pallas as pl
from jax.experimental.pallas import tpu as pltpu

F32 = jnp.float32
BF16 = jnp.bfloat16

D_MODEL = 1024
D_RNN = 512
D_CONV = 512
D_CONV_H = D_CONV // 2
RNN_HEADS = 8
RNN_HEAD_DIM = D_RNN // RNN_HEADS
GRID_W = 64
RG_C = 8.0
N_GROUPS = 4
EXPERTS_PER_GROUP = 8
N_EXPERTS = N_GROUPS * EXPERTS_PER_GROUP
TOP_K = 2
D_EXPERT = 512
NORM_EPS = 1e-6
F32_TINY = 1.1754944e-38

LANES = 128
SUBLANES = 8
ROW_TILES = D_MODEL // LANES
N_LANE_GROUPS = D_RNN // LANES
EXPERT_LANE0 = N_GROUPS

MOD_ROWS = 16
MOD_TN = 768
TOK_TILE = 512
COEFF_ROWS = 512
MOE_BLK = 256
MXU_TILE = 256
EXPERT_RING = 3
COMBINE_TILE = 256
DMA_UNROLL = 16
VMEM_LIMIT = 48 * 1024 * 1024


def _dot(a, b):
    return jnp.dot(a, b, preferred_element_type=F32)


def _split_bf16(x):
    hi = x.astype(BF16)
    lo = (x - hi.astype(F32)).astype(BF16)
    return hi, lo


def _mod_kernel(cc_ref, w_ref, b_ref, o_ref):
    s = cc_ref[...]
    s = s * jax.nn.sigmoid(s)
    s_hi, s_lo = _split_bf16(s)
    w_hi, w_lo = _split_bf16(w_ref[...])
    o_ref[...] = _dot(s_hi, w_hi) + _dot(s_lo, w_hi) + _dot(s_hi, w_lo) + b_ref[...]


def _modulation(cc, ada_w, ada_b):
    n = ada_w.shape[1]
    return pl.pallas_call(
        _mod_kernel,
        grid=(n // MOD_TN,),
        in_specs=[
            pl.BlockSpec((MOD_ROWS, D_MODEL), lambda j: (0, 0)),
            pl.BlockSpec((D_MODEL, MOD_TN), lambda j: (0, j)),
            pl.BlockSpec((1, MOD_TN), lambda j: (0, j)),
        ],
        out_specs=pl.BlockSpec((MOD_ROWS, MOD_TN), lambda j: (0, j)),
        out_shape=jax.ShapeDtypeStruct((MOD_ROWS, n), F32),
        compiler_params=pltpu.CompilerParams(vmem_limit_bytes=VMEM_LIMIT),
        name="mod",
    )(cc, ada_w, ada_b.reshape(1, n))


def _norm_mod(x, g, scale, shift):
    ms = jnp.mean(x * x, axis=-1, keepdims=True)
    y = x * lax.rsqrt(ms + NORM_EPS) * g
    return y * (1.0 + scale) + shift


def _inproj_kernel(x_ref, mod_ref, g_ref, w_ref, *out_refs, latent):
    h = _norm_mod(x_ref[0], g_ref[...], mod_ref[0, 1:2, :], mod_ref[0, 0:1, :])
    hb = h.astype(BF16)
    xr = _dot(hb, w_ref[:, 0:D_RNN])
    out_refs[0][0] = xr
    if latent:
        o = D_RNN
        out_refs[1][0] = _dot(hb, w_ref[:, o:o + D_RNN])
        o += D_RNN
        v = _dot(hb, w_ref[:, o:o + D_CONV])
        out_refs[3][0] = _dot(hb, w_ref[:, o + D_CONV:o + 2 * D_CONV])
        cg = _dot(hb, w_ref[:, o + 2 * D_CONV:o + 3 * D_CONV])
        out_refs[2][0] = cg * v


def _inproj(x, mod3, mod_row, norm_g, w_bf16, latent):
    bn, n, d = x.shape
    tm = min(TOK_TILE, n)
    n_out = 4 if latent else 1
    width = w_bf16.shape[1]
    mod_map = (lambda b, i: (b, 0, 0)) if mod_row is None else (lambda b, i: (mod_row, 0, 0))
    return pl.pallas_call(
        functools.partial(_inproj_kernel, latent=latent),
        grid=(bn, n // tm),
        in_specs=[
            pl.BlockSpec((1, tm, d), lambda b, i: (b, i, 0)),
            pl.BlockSpec((1, 6, d), mod_map),
            pl.BlockSpec((1, d), lambda b, i: (0, 0)),
            pl.BlockSpec((d, width), lambda b, i: (0, 0)),
        ],
        out_specs=[pl.BlockSpec((1, tm, D_RNN), lambda b, i: (b, i, 0))] * n_out,
        out_shape=[jax.ShapeDtypeStruct((bn, n, D_RNN), F32)] * n_out,
        compiler_params=pltpu.CompilerParams(vmem_limit_bytes=VMEM_LIMIT),
        name="inproj_lat" if latent else "inproj_ctx",
    )(x, mod3, norm_g.reshape(1, d), w_bf16)


def _shift_rows(x, k):
    n = x.shape[0]
    row = lax.broadcasted_iota(jnp.int32, x.shape, 0)
    rolled = pltpu.roll(x, k % n, axis=0)
    valid = (row >= k) if k > 0 else (row < n + k)
    return jnp.where(valid, rolled, 0.0)


def _scan_pitch(chunk):
    pitch = chunk + SUBLANES
    return pitch if (pitch // SUBLANES) % 2 else pitch + SUBLANES


def _rnn_kernel(xr_ref, xrc_ref, gr_ref, cw_ref, cb_ref, wg_ref, bg_ref, lam_ref, y_ref,
                xc_s, ap_f, bp_f, ap_b, bp_b, hl_f, al_f, hl_b, al_b, hp_f, hp_b, *, n_lat, n_ctx):
    nl = -lam_ref[...]
    sp = jnp.maximum(nl, 0.0) + jnp.log1p(jnp.exp(-jnp.abs(nl)))
    c1 = (-0.5 * RG_C) * sp
    cw = cw_ref[...]
    bias = cb_ref[...]
    wg = wg_ref[0]
    bg = bg_ref[0]
    dirs = ((ap_f, bp_f, hl_f, al_f, hp_f), (ap_b, bp_b, hl_b, al_b, hp_b))

    def conv_into(x, n):
        xc_s[pl.ds(0, n), :] = (cw[0:1] * _shift_rows(x, 2) + cw[1:2] * _shift_rows(x, 1)
                                + cw[2:3] * x + cw[3:4] * _shift_rows(x, -1)) + bias

    def coefficients(n):
        chunk = n // SUBLANES
        pitch = _scan_pitch(chunk)
        rows = max(chunk, min(n, COEFF_ROWS))
        per = rows // chunk

        def body(i, carry):
            xc = xc_s[pl.ds(pl.multiple_of(i * rows, SUBLANES), rows), :]
            gates = _dot(xc.astype(BF16), wg) + bg
            half_xc = 0.5 * xc
            for d in range(2):
                tr = jnp.tanh(0.5 * gates[:, (2 * d) * LANES:(2 * d + 1) * LANES])
                ti = jnp.tanh(0.5 * gates[:, (2 * d + 1) * LANES:(2 * d + 2) * LANES])
                log_a = c1[d:d + 1] + c1[d:d + 1] * tr
                a = jnp.exp(log_a)
                y = -jnp.tanh(log_a) * (a * a + 1.0)
                b = (y * lax.rsqrt(jnp.maximum(y, F32_TINY))) * (half_xc + half_xc * ti)
                for k in range(per):
                    dst = pl.multiple_of((i * per + k) * pitch, SUBLANES)
                    dirs[d][0][pl.ds(dst, chunk), :] = a[k * chunk:(k + 1) * chunk]
                    dirs[d][1][pl.ds(dst, chunk), :] = b[k * chunk:(k + 1) * chunk]
            return carry

        if n == rows:
            body(0, 0)
        else:
            lax.fori_loop(0, n // rows, body, 0)

    def scan(n, h0_f, h0_b, keep):
        chunk = n // SUBLANES
        pitch = _scan_pitch(chunk)

        def steps(jo, carry):
            h_f, a_f, h_b, a_b = carry
            for u in range(SUBLANES):
                j = jo * SUBLANES + u
                av = ap_f[pl.ds(j, SUBLANES, stride=pitch), :]
                h_f = av * h_f + bp_f[pl.ds(j, SUBLANES, stride=pitch), :]
                a_f = av * a_f
                jb = chunk - 1 - j
                av = ap_b[pl.ds(jb, SUBLANES, stride=pitch), :]
                h_b = av * h_b + bp_b[pl.ds(jb, SUBLANES, stride=pitch), :]
                a_b = av * a_b
                if keep:
                    o = pl.multiple_of(j * SUBLANES, SUBLANES)
                    hl_f[pl.ds(o, SUBLANES), :] = h_f
                    al_f[pl.ds(o, SUBLANES), :] = a_f
                    hl_b[pl.ds(o, SUBLANES), :] = h_b
                    al_b[pl.ds(o, SUBLANES), :] = a_b
            return h_f, a_f, h_b, a_b

        zeros = jnp.zeros((SUBLANES, LANES), F32)
        ones = jnp.ones((SUBLANES, LANES), F32)
        h_f, a_f, h_b, a_b = lax.fori_loop(0, chunk // SUBLANES, steps, (zeros, ones, zeros, ones))

        in_f = [h0_f]
        for c in range(SUBLANES):
            in_f.append(a_f[c:c + 1] * in_f[c] + h_f[c:c + 1])
        in_b = [h0_b]
        for c in range(SUBLANES - 1, -1, -1):
            in_b.append(a_b[c:c + 1] * in_b[-1] + h_b[c:c + 1])
        if keep:
            hin_f = jnp.concatenate(in_f[:SUBLANES], axis=0)
            hin_b = jnp.concatenate(in_b[SUBLANES - 1::-1], axis=0)

            def fix(jo, carry):
                for u in range(SUBLANES):
                    j = jo * SUBLANES + u
                    o = pl.multiple_of(j * SUBLANES, SUBLANES)
                    hp_f[pl.ds(j, SUBLANES, stride=pitch), :] = (
                        hl_f[pl.ds(o, SUBLANES), :] + al_f[pl.ds(o, SUBLANES), :] * hin_f)
                    hp_b[pl.ds(chunk - 1 - j, SUBLANES, stride=pitch), :] = (
                        hl_b[pl.ds(o, SUBLANES), :] + al_b[pl.ds(o, SUBLANES), :] * hin_b)
                return carry
            lax.fori_loop(0, chunk // SUBLANES, fix, 0)
        return in_f[SUBLANES], in_b[SUBLANES]

    zero = jnp.zeros((1, LANES), F32)
    conv_into(xrc_ref[0], n_ctx)
    coefficients(n_ctx)
    h0_f, h0_b = scan(n_ctx, zero, zero, keep=False)

    conv_into(xr_ref[0], n_lat)
    coefficients(n_lat)
    scan(n_lat, h0_f, h0_b, keep=True)

    chunk = n_lat // SUBLANES
    pitch = _scan_pitch(chunk)

    def emit(c, carry):
        src = pl.multiple_of(c * chunk, SUBLANES)
        dst = pl.multiple_of(c * pitch, SUBLANES)
        hsum = hp_f[pl.ds(dst, chunk), :] + hp_b[pl.ds(dst, chunk), :]
        y_ref[0, pl.ds(src, chunk), :] = jax.nn.gelu(gr_ref[0, pl.ds(src, chunk), :], approximate=True) * hsum
        return carry
    lax.fori_loop(0, SUBLANES, emit, 0)


def _rnn(xr, xr_c, gr, conv_w, conv_b, wg, bgate, lam):
    bn, n, _ = xr.shape
    n_ctx = xr_c.shape[1]
    assert n % (SUBLANES * SUBLANES) == 0 and n_ctx % (SUBLANES * SUBLANES) == 0 and n_ctx <= n
    pitched = SUBLANES * _scan_pitch(n // SUBLANES)
    seq_spec = pl.BlockSpec((1, n, LANES), lambda b, p: (b, 0, p))
    return pl.pallas_call(
        functools.partial(_rnn_kernel, n_lat=n, n_ctx=n_ctx),
        grid=(bn, N_LANE_GROUPS),
        in_specs=[
            seq_spec,
            pl.BlockSpec((1, n_ctx, LANES), lambda b, p: (b, 0, p)),
            seq_spec,
            pl.BlockSpec((4, LANES), lambda b, p: (0, p)),
            pl.BlockSpec((1, LANES), lambda b, p: (0, p)),
            pl.BlockSpec((1, LANES, 4 * LANES), lambda b, p: (p, 0, 0)),
            pl.BlockSpec((1, 1, 4 * LANES), lambda b, p: (p, 0, 0)),
            pl.BlockSpec((2, LANES), lambda b, p: (0, p)),
        ],
        out_specs=seq_spec,
        out_shape=jax.ShapeDtypeStruct((bn, n, D_RNN), F32),
        scratch_shapes=[pltpu.VMEM((n, LANES), F32)]
        + [pltpu.VMEM((pitched, LANES), F32)] * 4
        + [pltpu.VMEM((n, LANES), F32)] * 4
        + [pltpu.VMEM((pitched, LANES), F32)] * 2,
        compiler_params=pltpu.CompilerParams(vmem_limit_bytes=VMEM_LIMIT),
        name="rnn",
    )(xr, xr_c, gr, conv_w, conv_b.reshape(1, D_RNN), wg, bgate, lam)


def _gate_weights(rg_wa, rg_ba, rg_wx, rg_bx):
    eye = jnp.eye(2, dtype=F32)
    blocks, biases = [], []
    for d in range(2):
        for w, bvec in ((rg_wa[d], rg_ba[d]), (rg_wx[d], rg_bx[d])):
            w4 = w.reshape(N_LANE_GROUPS, 2, RNN_HEAD_DIM, RNN_HEAD_DIM)
            bd = jnp.einsum("paij,ac->paicj", w4, eye).reshape(N_LANE_GROUPS, LANES, LANES)
            blocks.append(bd)
            biases.append(bvec.reshape(N_LANE_GROUPS, 1, LANES))
    return jnp.concatenate(blocks, axis=-1).astype(BF16), jnp.concatenate(biases, axis=-1)


def _gconv_kernel(u_ref, bg_ref, w_ref, y_ref, *, n):
    p = pl.program_id(1)
    u = u_ref[0]
    w = w_ref[...]

    @pl.when(p < D_CONV_H // LANES)
    def _():
        col = lax.broadcasted_iota(jnp.int32, u.shape, 0) % GRID_W
        left = jnp.where(col > 0, _shift_rows(u, 1), 0.0)
        right = jnp.where(col < GRID_W - 1, _shift_rows(u, -1), 0.0)
        y_ref[0] = bg_ref[0] * (w[0:1] * left + w[1:2] * u + w[2:3] * right)

    @pl.when(p >= D_CONV_H // LANES)
    def _():
        y_ref[0] = bg_ref[0] * (w[0:1] * _shift_rows(u, GRID_W) + w[1:2] * u
                                + w[2:3] * _shift_rows(u, -GRID_W))


def _gconv(u, bg, w):
    bn, n, _ = u.shape
    seq_spec = pl.BlockSpec((1, n, LANES), lambda b, p: (b, 0, p))
    return pl.pallas_call(
        functools.partial(_gconv_kernel, n=n),
        grid=(bn, D_CONV // LANES),
        in_specs=[seq_spec, seq_spec, pl.BlockSpec((3, LANES), lambda b, p: (0, p))],
        out_specs=seq_spec,
        out_shape=jax.ShapeDtypeStruct((bn, n, D_CONV), F32),
        compiler_params=pltpu.CompilerParams(vmem_limit_bytes=VMEM_LIMIT),
        name="gconv",
    )(u, bg, w)


def _lane_max(x, mask):
    return jnp.max(jnp.where(mask, x, -jnp.inf), axis=-1, keepdims=True)


def _first_lane(cond, lane):
    return jnp.min(jnp.where(cond, lane, float(LANES)), axis=-1, keepdims=True)


def _outproj_kernel(x_ref, yr_ref, yc_ref, w_ref, mod_ref, g_ref, wr_hi_ref, wr_lo_ref, br_ref,
                    x1_ref, mt_ref, route_ref, cnt_ref, carry, *, tm):
    first = (pl.program_id(0) == 0) & (pl.program_id(1) == 0)

    @pl.when(first)
    def _():
        carry[...] = jnp.zeros_like(carry)

    mix = _dot(yr_ref[0].astype(BF16), w_ref[0:D_RNN, :]) + _dot(yc_ref[0].astype(BF16), w_ref[D_RNN:, :])
    x1 = x_ref[0] + mod_ref[0, 2:3, :] * mix
    x1_ref[0] = x1
    m = _norm_mod(x1, g_ref[...], mod_ref[0, 4:5, :], mod_ref[0, 3:4, :])
    for s in range(ROW_TILES):
        mt_ref[pl.ds(s, tm, stride=ROW_TILES), :] = m[:, s * LANES:(s + 1) * LANES]

    m_hi, m_lo = _split_bf16(m)
    logits = (_dot(m_hi, wr_hi_ref[...]) + _dot(m_lo, wr_hi_ref[...]) + _dot(m_hi, wr_lo_ref[...])
              + br_ref[...])
    lane_i = lax.broadcasted_iota(jnp.int32, logits.shape, 1)
    lane = lane_i.astype(F32)
    is_grp = lane_i < N_GROUPS
    g_max = _lane_max(logits, is_grp)
    grp = _first_lane(is_grp & (logits == g_max), lane)
    p_g = 1.0 / jnp.sum(jnp.where(is_grp, jnp.exp(logits - g_max), 0.0), axis=-1, keepdims=True)
    lo_lane = EXPERT_LANE0 + grp * EXPERTS_PER_GROUP
    in_grp = (lane >= lo_lane) & (lane < lo_lane + EXPERTS_PER_GROUP)
    l1 = _lane_max(logits, in_grp)
    i1 = _first_lane(in_grp & (logits == l1), lane)
    rest = in_grp & (lane != i1)
    l2 = _lane_max(logits, rest)
    i2 = _first_lane(rest & (logits == l2), lane)
    r21 = jnp.exp(l2 - l1)
    gate1 = p_g / (1.0 + r21)
    gate2 = gate1 * r21

    oh1 = jnp.where(lane == i1, 1.0, 0.0)
    oh2 = jnp.where(lane == i2, 1.0, 0.0)
    both = (oh1 + oh2).astype(BF16)
    ti = lax.broadcasted_iota(jnp.int32, (tm, tm), 0)
    tj = lax.broadcasted_iota(jnp.int32, (tm, tm), 1)
    tri = jnp.where(tj < ti, 1.0, 0.0).astype(BF16)
    before = _dot(tri, both) + carry[...]
    rank1 = jnp.sum(oh1 * before, axis=-1, keepdims=True)
    rank2 = jnp.sum(oh2 * before, axis=-1, keepdims=True)
    total = carry[...] + jnp.sum(oh1 + oh2, axis=0, keepdims=True)
    carry[...] = total
    cnt_ref[...] = total

    e1 = i1 - EXPERT_LANE0
    e2 = i2 - EXPERT_LANE0
    out = jnp.zeros(logits.shape, F32)
    for k, val in enumerate((e1, e2, gate1, gate2, rank1, rank2)):
        out = jnp.where(lane_i == k, val, out)
    route_ref[...] = out


def _outproj(x, y_rnn, y_conv, w_out_bf16, mod3, norm_g, wr_hi, wr_lo, br):
    bn, n, d = x.shape
    tm = min(TOK_TILE, n)
    nt = n // tm
    t_all = bn * n
    return pl.pallas_call(
        functools.partial(_outproj_kernel, tm=tm),
        grid=(bn, nt),
        in_specs=[
            pl.BlockSpec((1, tm, d), lambda b, i: (b, i, 0)),
            pl.BlockSpec((1, tm, D_RNN), lambda b, i: (b, i, 0)),
            pl.BlockSpec((1, tm, D_CONV), lambda b, i: (b, i, 0)),
            pl.BlockSpec((D_RNN + D_CONV, d), lambda b, i: (0, 0)),
            pl.BlockSpec((1, 6, d), lambda b, i: (b, 0, 0)),
            pl.BlockSpec((1, d), lambda b, i: (0, 0)),
            pl.BlockSpec((d, LANES), lambda b, i: (0, 0)),
            pl.BlockSpec((d, LANES), lambda b, i: (0, 0)),
            pl.BlockSpec((1, LANES), lambda b, i: (0, 0)),
        ],
        out_specs=[
            pl.BlockSpec((1, tm, d), lambda b, i: (b, i, 0)),
            pl.BlockSpec((tm * ROW_TILES, LANES), lambda b, i: (b * nt + i, 0)),
            pl.BlockSpec((tm, LANES), lambda b, i: (b * nt + i, 0)),
            pl.BlockSpec((1, LANES), lambda b, i: (0, 0)),
        ],
        out_shape=[
            jax.ShapeDtypeStruct((bn, n, d), F32),
            jax.ShapeDtypeStruct((t_all * ROW_TILES, LANES), F32),
            jax.ShapeDtypeStruct((t_all, LANES), F32),
            jax.ShapeDtypeStruct((1, LANES), F32),
        ],
        scratch_shapes=[pltpu.VMEM((1, LANES), F32)],
        compiler_params=pltpu.CompilerParams(
            dimension_semantics=("arbitrary", "arbitrary"), vmem_limit_bytes=VMEM_LIMIT),
        name="outproj",
    )(x, y_rnn, y_conv, w_out_bf16, mod3, norm_g.reshape(1, d), wr_hi, wr_lo, br)


def _row_tile(ref, row):
    return ref.at[pl.ds(pl.multiple_of(row * ROW_TILES, ROW_TILES), ROW_TILES)]


def _slotmap_kernel(dest_ref, zeros_hbm, asg_ref, sem):
    fill = pltpu.make_async_copy(zeros_hbm, asg_ref, sem)
    fill.start()
    fill.wait()

    def body(c, carry):
        for u in range(DMA_UNROLL):
            a = c * DMA_UNROLL + u
            asg_ref[dest_ref[a]] = a
        return carry
    lax.fori_loop(0, dest_ref.shape[0] // DMA_UNROLL, body, 0)


def _slotmap(dest, n_slots):
    return pl.pallas_call(
        _slotmap_kernel,
        in_specs=[pl.BlockSpec(memory_space=pltpu.SMEM), pl.BlockSpec(memory_space=pl.ANY)],
        out_specs=pl.BlockSpec(memory_space=pltpu.SMEM),
        out_shape=jax.ShapeDtypeStruct((n_slots,), jnp.int32),
        scratch_shapes=[pltpu.SemaphoreType.DMA],
        name="slotmap",
    )(dest, jnp.zeros((n_slots,), jnp.int32))


def _expert_kernel(be_ref, nu_ref, asg_hbm, mt_hbm, wg_ref, wu_ref, wd_ref, yb_ref,
                   xbuf, sems, idx, isems, wg_s, wu_s, wd_s):
    j = pl.program_id(0)
    n_used = nu_ref[0]
    last = n_used - 1

    def idx_copy(blk, sl):
        return pltpu.make_async_copy(asg_hbm.at[blk], idx.at[sl], isems.at[sl])

    def row_copy(tok, slot, r):
        return pltpu.make_async_copy(_row_tile(mt_hbm, tok), _row_tile(xbuf.at[slot], r), sems.at[slot])

    def issue_rows(slot, r0, n):
        for u in range(n):
            tok = lax.shift_right_logical(idx[slot, 0, r0 + u], 1)
            row_copy(tok, slot, r0 + u).start(priority=u % 2)

    def drain(slot):
        def body(c, carry):
            for u in range(DMA_UNROLL):
                row_copy(0, slot, 0).wait()
            return carry
        lax.fori_loop(0, MOE_BLK // DMA_UNROLL, body, 0)

    @pl.when(j >= n_used)
    def _():
        yb_ref[...] = jnp.zeros_like(yb_ref)

    @pl.when(j < n_used)
    def _():
        slot = j % EXPERT_RING
        ahead = (j + EXPERT_RING - 1) % EXPERT_RING

        @pl.when(j == 0)
        def _():
            for b in range(EXPERT_RING - 1):
                idx_copy(jnp.minimum(b, last), b).start()
                idx_copy(0, b).wait()

                def body(c, carry, b=b):
                    issue_rows(b, c * DMA_UNROLL, DMA_UNROLL)
                    return carry
                lax.fori_loop(0, MOE_BLK // DMA_UNROLL, body, 0)
            idx_copy(jnp.minimum(EXPERT_RING - 1, last), EXPERT_RING - 1).start()

        e = be_ref[j]
        prev = be_ref[jnp.maximum(j - 1, 0)]

        @pl.when((j == 0) | (e != prev))
        def _():
            wg_s[...] = wg_ref[0].astype(BF16)
            wu_s[...] = wu_ref[0].astype(BF16)
            wd_s[...] = wd_ref[0].astype(BF16)

        drain(slot)
        idx_copy(0, ahead).wait()

        n_pieces = (2 * D_EXPERT // MXU_TILE) * (D_MODEL // MXU_TILE) + (
            D_MODEL // MXU_TILE) * (D_EXPERT // MXU_TILE)
        bounds = [(p * MOE_BLK) // n_pieces for p in range(n_pieces + 1)]
        pieces = iter(zip(bounds[:-1], bounds[1:]))

        def dot_pieces(a, w_ref, n0):
            acc = None
            for k0 in range(0, a.shape[1], MXU_TILE):
                part = _dot(a[:, k0:k0 + MXU_TILE], w_ref[k0:k0 + MXU_TILE, n0:n0 + MXU_TILE])
                acc = part if acc is None else acc + part
                r0, r1 = next(pieces)
                issue_rows(ahead, r0, r1 - r0)
            return acc

        x = jnp.concatenate(
            [xbuf[slot, pl.ds(s, MOE_BLK, stride=ROW_TILES), :] for s in range(ROW_TILES)], axis=-1)
        xb16 = x.astype(BF16)
        acts = []
        for n0 in range(0, D_EXPERT, MXU_TILE):
            gate = dot_pieces(xb16, wg_s, n0)
            up = dot_pieces(xb16, wu_s, n0)
            acts.append(((gate * jax.nn.sigmoid(gate)) * up).astype(BF16))
        h = jnp.concatenate(acts, axis=-1)
        for n0 in range(0, D_MODEL, MXU_TILE):
            y = dot_pieces(h, wd_s, n0)
            for s in range(MXU_TILE // LANES):
                yb_ref[pl.ds(n0 // LANES + s, MOE_BLK, stride=ROW_TILES), :] = (
                    y[:, s * LANES:(s + 1) * LANES])

        @pl.when(j < last)
        def _():
            idx_copy(jnp.minimum(j + EXPERT_RING, last), slot).start()

        @pl.when(j == last)
        def _():
            for b in range(1, EXPERT_RING):
                drain((j + b) % EXPERT_RING)


def _experts(block_e, n_used, slot_asg, mt, w_gate, w_up, w_down, n_blocks):
    def wsel(j, be, nu):
        return (be[jnp.minimum(j, nu[0] - 1)], 0, 0)

    return pl.pallas_call(
        _expert_kernel,
        grid_spec=pltpu.PrefetchScalarGridSpec(
            num_scalar_prefetch=2,
            grid=(n_blocks,),
            in_specs=[
                pl.BlockSpec(memory_space=pl.ANY),
                pl.BlockSpec(memory_space=pl.ANY),
                pl.BlockSpec((1, D_MODEL, D_EXPERT), wsel),
                pl.BlockSpec((1, D_MODEL, D_EXPERT), wsel),
                pl.BlockSpec((1, D_EXPERT, D_MODEL), wsel),
            ],
            out_specs=pl.BlockSpec((MOE_BLK * ROW_TILES, LANES), lambda j, be, nu: (j, 0)),
            scratch_shapes=[
                pltpu.VMEM((EXPERT_RING, MOE_BLK * ROW_TILES, LANES), F32),
                pltpu.SemaphoreType.DMA((EXPERT_RING,)),
                pltpu.SMEM((EXPERT_RING, 1, MOE_BLK), jnp.int32),
                pltpu.SemaphoreType.DMA((EXPERT_RING,)),
                pltpu.VMEM((D_MODEL, D_EXPERT), BF16),
                pltpu.VMEM((D_MODEL, D_EXPERT), BF16),
                pltpu.VMEM((D_EXPERT, D_MODEL), BF16),
            ],
        ),
        out_shape=jax.ShapeDtypeStruct((n_blocks * MOE_BLK * ROW_TILES, LANES), F32),
        compiler_params=pltpu.CompilerParams(
            dimension_semantics=("arbitrary",), vmem_limit_bytes=VMEM_LIMIT),
        name="expert",
    )(block_e, n_used, slot_asg.reshape(n_blocks, 1, MOE_BLK), mt, w_gate, w_up, w_down)


def _combine_kernel(dest_ref, yb_hbm, x1_ref, route_ref, mod_ref, g_ref, o_ref, ybuf, sems, *, tc):
    i = pl.program_id(0)
    slot = i % 2

    def row_copy(d, sl, k, r):
        return pltpu.make_async_copy(_row_tile(yb_hbm, d), _row_tile(ybuf.at[sl, k], r), sems.at[sl])

    def gather(step, sl):
        def issue(c, carry):
            for u in range(DMA_UNROLL):
                r = c * DMA_UNROLL + u
                for k in range(TOP_K):
                    row_copy(dest_ref[TOP_K * (step * tc + r) + k], sl, k, r).start(priority=k)
            return carry
        lax.fori_loop(0, tc // DMA_UNROLL, issue, 0)

    @pl.when(i == 0)
    def _():
        gather(0, 0)

    @pl.when(i + 1 < pl.num_programs(0))
    def _():
        gather(i + 1, 1 - slot)

    def drain(c, carry):
        for u in range(DMA_UNROLL * TOP_K):
            row_copy(0, slot, 0, 0).wait()
        return carry
    lax.fori_loop(0, tc // DMA_UNROLL, drain, 0)

    def rows(k):
        return jnp.concatenate(
            [ybuf[slot, k, pl.ds(s, tc, stride=ROW_TILES), :] for s in range(ROW_TILES)], axis=-1)

    route = route_ref[...]
    y = route[:, 2:3] * rows(0) + route[:, 3:4] * rows(1)
    x2 = x1_ref[...] + mod_ref[0, 5:6, :] * y
    ms = jnp.mean(x2 * x2, axis=-1, keepdims=True)
    o_ref[...] = x2 * lax.rsqrt(ms + NORM_EPS) * g_ref[...]


def _combine(dest, yb, x1_2d, route, mod3, final_g, seq):
    t_all, d = x1_2d.shape
    tc = min(COMBINE_TILE, seq)
    per_seq = seq // tc
    return pl.pallas_call(
        functools.partial(_combine_kernel, tc=tc),
        grid_spec=pltpu.PrefetchScalarGridSpec(
            num_scalar_prefetch=1,
            grid=(t_all // tc,),
            in_specs=[
                pl.BlockSpec(memory_space=pl.ANY),
                pl.BlockSpec((tc, d), lambda i, dest: (i, 0)),
                pl.BlockSpec((tc, LANES), lambda i, dest: (i, 0)),
                pl.BlockSpec((1, 6, d), lambda i, dest: (i // per_seq, 0, 0)),
                pl.BlockSpec((1, d), lambda i, dest: (0, 0)),
            ],
            out_specs=pl.BlockSpec((tc, d), lambda i, dest: (i, 0)),
            scratch_shapes=[
                pltpu.VMEM((2, TOP_K, tc * ROW_TILES, LANES), F32),
                pltpu.SemaphoreType.DMA((2,)),
            ],
        ),
        out_shape=jax.ShapeDtypeStruct((t_all, d), F32),
        compiler_params=pltpu.CompilerParams(
            dimension_semantics=("arbitrary",), vmem_limit_bytes=VMEM_LIMIT),
        name="combine",
    )(dest, yb, x1_2d, route, mod3, final_g.reshape(1, d))


def kernel(x, c, ctx, c_ctx, ada_w, ada_b, norm1_g, norm2_g, w_in, rnn_conv_w, rnn_conv_b, rg_wa, rg_ba,
           rg_wx, rg_bx, rg_lambda, sc_conv_w, w_out, router_group_w, router_group_b, router_exp_w,
           router_exp_b, exp_w_gate, exp_w_up, exp_w_down, final_norm_g):
    bn, seq, d = x.shape
    assert d == D_MODEL and bn < MOD_ROWS and ada_w.shape[0] == 1
    t_all = bn * seq

    cc = jnp.concatenate([c, c_ctx[None], jnp.zeros((MOD_ROWS - bn - 1, d), F32)], axis=0)
    mod3 = _modulation(cc, ada_w[0], ada_b[0]).reshape(MOD_ROWS, 6, d)

    w_in_b = w_in[0].astype(BF16)
    xr, gr, u, bg = _inproj(x, mod3, None, norm1_g[0], w_in_b, latent=True)
    (xr_c,) = _inproj(ctx, mod3, bn, norm1_g[0], w_in_b[:, :D_RNN], latent=False)

    wg, bgate = _gate_weights(rg_wa[0], rg_ba[0], rg_wx[0], rg_bx[0])
    y_rnn = _rnn(xr, xr_c, gr, rnn_conv_w[0], rnn_conv_b[0], wg, bgate, rg_lambda[0])
    y_conv = _gconv(u, bg, sc_conv_w[0])

    wr = jnp.zeros((d, LANES), F32)
    wr = wr.at[:, :N_GROUPS].set(router_group_w[0]).at[:, EXPERT_LANE0:EXPERT_LANE0 + N_EXPERTS].set(router_exp_w[0])
    br = jnp.zeros((1, LANES), F32)
    br = br.at[0, :N_GROUPS].set(router_group_b[0]).at[0, EXPERT_LANE0:EXPERT_LANE0 + N_EXPERTS].set(router_exp_b[0])
    wr_hi, wr_lo = _split_bf16(wr)
    x1, mt, route, cnt = _outproj(x, y_rnn, y_conv, w_out[0].astype(BF16), mod3, norm2_g[0], wr_hi, wr_lo, br)

    n_assign = t_all * TOP_K
    n_blocks = (n_assign + N_EXPERTS * (MOE_BLK - 1) + MOE_BLK - 1) // MOE_BLK
    counts = cnt[0, EXPERT_LANE0:EXPERT_LANE0 + N_EXPERTS].astype(jnp.int32)
    pcounts = (counts + MOE_BLK - 1) // MOE_BLK * MOE_BLK
    pends = jnp.cumsum(pcounts)
    pstarts = pends - pcounts
    experts = route[:, 0:TOP_K].astype(jnp.int32)
    ranks = route[:, 4:4 + TOP_K].astype(jnp.int32)
    onehot = experts[:, :, None] == jnp.arange(N_EXPERTS, dtype=jnp.int32)
    dest = (ranks + jnp.sum(jnp.where(onehot, pstarts, 0), axis=-1)).reshape(n_assign)
    n_used = (pends[-1] // MOE_BLK).astype(jnp.int32)
    blk_start = jnp.arange(n_blocks, dtype=jnp.int32) * MOE_BLK
    block_e = jnp.minimum(jnp.sum(blk_start[:, None] >= pends[None, :], axis=1), N_EXPERTS - 1)
    last_e = jnp.max(jnp.where(counts > 0, jnp.arange(N_EXPERTS, dtype=jnp.int32), 0))
    block_e = jnp.where(blk_start < pends[-1], block_e, last_e).astype(jnp.int32)

    n_slots = n_blocks * MOE_BLK
    slot_asg = _slotmap(dest, n_slots)
    yb = _experts(block_e, n_used.reshape(1), slot_asg, mt, exp_w_gate[0], exp_w_up[0], exp_w_down[0],
                  n_blocks)
    out = _combine(dest, yb, x1.reshape(t_all, d), route, mod3, final_norm_g, seq)
    return out.reshape(bn, seq, d)
```

```python
import functools

import jax
import jax.numpy as jnp
from jax import lax
from jax.experimental import pallas as pl
from jax.experimental.pallas import tpu as pltpu

F32 = jnp.float32
BF16 = jnp.bfloat16

D_MODEL = 1024
D_RNN = 512
D_CONV = 512
D_CONV_H = D_CONV // 2
RNN_HEADS = 8
RNN_HEAD_DIM = D_RNN // RNN_HEADS
GRID_W = 64
RG_C = 8.0
N_GROUPS = 4
EXPERTS_PER_GROUP = 8
N_EXPERTS = N_GROUPS * EXPERTS_PER_GROUP
TOP_K = 2
D_EXPERT = 512
NORM_EPS = 1e-6
F32_TINY = 1.1754944e-38

LANES = 128
SUBLANES = 8
ROW_TILES = D_MODEL // LANES
N_LANE_GROUPS = D_RNN // LANES
EXPERT_LANE0 = N_GROUPS

MOD_ROWS = 16
MOD_TN = 768
TOK_TILE = 512
COEFF_ROWS = 512
MOE_BLK = 512
MXU_TILE = 256
EXPERT_RING = 3
COMBINE_TILE = 256
DMA_UNROLL = 16
VMEM_LIMIT = 48 * 1024 * 1024


def _dot(a, b):
    return jnp.dot(a, b, preferred_element_type=F32)


def _split_bf16(x):
    hi = x.astype(BF16)
    lo = (x - hi.astype(F32)).astype(BF16)
    return hi, lo


def _mod_kernel(cc_ref, w_ref, b_ref, o_ref):
    s = cc_ref[...]
    s = s * jax.nn.sigmoid(s)
    s_hi, s_lo = _split_bf16(s)
    w_hi, w_lo = _split_bf16(w_ref[...])
    o_ref[...] = _dot(s_hi, w_hi) + _dot(s_lo, w_hi) + _dot(s_hi, w_lo) + b_ref[...]


def _modulation(cc, ada_w, ada_b):
    n = ada_w.shape[1]
    return pl.pallas_call(
        _mod_kernel,
        grid=(n // MOD_TN,),
        in_specs=[
            pl.BlockSpec((MOD_ROWS, D_MODEL), lambda j: (0, 0)),
            pl.BlockSpec((D_MODEL, MOD_TN), lambda j: (0, j)),
            pl.BlockSpec((1, MOD_TN), lambda j: (0, j)),
        ],
        out_specs=pl.BlockSpec((MOD_ROWS, MOD_TN), lambda j: (0, j)),
        out_shape=jax.ShapeDtypeStruct((MOD_ROWS, n), F32),
        compiler_params=pltpu.CompilerParams(vmem_limit_bytes=VMEM_LIMIT),
        name="mod",
    )(cc, ada_w, ada_b.reshape(1, n))


def _norm_mod(x, g, scale, shift):
    ms = jnp.mean(x * x, axis=-1, keepdims=True)
    y = x * lax.rsqrt(ms + NORM_EPS) * g
    return y * (1.0 + scale) + shift


def _inproj_kernel(x_ref, mod_ref, g_ref, w32_ref, *refs, latent):
    out_refs, w_ref = refs[:-1], refs[-1]

    @pl.when((pl.program_id(0) == 0) & (pl.program_id(1) == 0))
    def _():
        w_ref[...] = w32_ref[...].astype(BF16)

    h = _norm_mod(x_ref[0], g_ref[...], mod_ref[0, 1:2, :], mod_ref[0, 0:1, :])
    hb = h.astype(BF16)
    xr = _dot(hb, w_ref[:, 0:D_RNN])
    out_refs[0][0] = xr
    if latent:
        o = D_RNN
        out_refs[1][0] = _dot(hb, w_ref[:, o:o + D_RNN])
        o += D_RNN
        v = _dot(hb, w_ref[:, o:o + D_CONV])
        out_refs[3][0] = _dot(hb, w_ref[:, o + D_CONV:o + 2 * D_CONV])
        cg = _dot(hb, w_ref[:, o + 2 * D_CONV:o + 3 * D_CONV])
        out_refs[2][0] = cg * v


def _inproj(x, mod3, mod_row, norm_g, w_in, latent):
    bn, n, d = x.shape
    tm = min(TOK_TILE, n)
    n_out = 4 if latent else 1
    width = w_in.shape[1] if latent else D_RNN
    mod_map = (lambda b, i: (b, 0, 0)) if mod_row is None else (lambda b, i: (mod_row, 0, 0))
    return pl.pallas_call(
        functools.partial(_inproj_kernel, latent=latent),
        grid=(bn, n // tm),
        in_specs=[
            pl.BlockSpec((1, tm, d), lambda b, i: (b, i, 0)),
            pl.BlockSpec((1, 6, d), mod_map),
            pl.BlockSpec((1, d), lambda b, i: (0, 0)),
            pl.BlockSpec((d, width), lambda b, i: (0, 0), pipeline_mode=pl.Buffered(1)),
        ],
        out_specs=[pl.BlockSpec((1, tm, D_RNN), lambda b, i: (b, i, 0))] * n_out,
        out_shape=[jax.ShapeDtypeStruct((bn, n, D_RNN), F32)] * n_out,
        scratch_shapes=[pltpu.VMEM((d, width), BF16)],
        compiler_params=pltpu.CompilerParams(
            dimension_semantics=("arbitrary", "arbitrary"), vmem_limit_bytes=VMEM_LIMIT),
        name="inproj_lat" if latent else "inproj_ctx",
    )(x, mod3, norm_g.reshape(1, d), w_in)


def _shift_rows(x, k):
    n = x.shape[0]
    row = lax.broadcasted_iota(jnp.int32, x.shape, 0)
    rolled = pltpu.roll(x, k % n, axis=0)
    valid = (row >= k) if k > 0 else (row < n + k)
    return jnp.where(valid, rolled, 0.0)


def _scan_pitch(chunk):
    pitch = chunk + SUBLANES
    return pitch if (pitch // SUBLANES) % 2 else pitch + SUBLANES


def _rnn_kernel(xr_ref, xrc_ref, gr_ref, cw_ref, cb_ref, wg_ref, bg_ref, lam_ref, y_ref,
                xc_s, ap_f, bp_f, ap_b, bp_b, hl_f, al_f, hl_b, al_b, hp_f, hp_b, *, n_lat, n_ctx):
    nl = -lam_ref[...]
    sp = jnp.maximum(nl, 0.0) + jnp.log1p(jnp.exp(-jnp.abs(nl)))
    c1 = (-0.5 * RG_C) * sp
    cw = cw_ref[...]
    bias = cb_ref[...]
    wg = wg_ref[0]
    bg = bg_ref[0]
    dirs = ((ap_f, bp_f, hl_f, al_f, hp_f), (ap_b, bp_b, hl_b, al_b, hp_b))

    def conv_into(x, n):
        xc_s[pl.ds(0, n), :] = (cw[0:1] * _shift_rows(x, 2) + cw[1:2] * _shift_rows(x, 1)
                                + cw[2:3] * x + cw[3:4] * _shift_rows(x, -1)) + bias

    def coefficients(n):
        chunk = n // SUBLANES
        pitch = _scan_pitch(chunk)
        rows = max(chunk, min(n, COEFF_ROWS))
        per = rows // chunk

        def body(i, carry):
            xc = xc_s[pl.ds(pl.multiple_of(i * rows, SUBLANES), rows), :]
            gates = _dot(xc.astype(BF16), wg) + bg
            half_xc = 0.5 * xc
            for d in range(2):
                tr = jnp.tanh(0.5 * gates[:, (2 * d) * LANES:(2 * d + 1) * LANES])
                ti = jnp.tanh(0.5 * gates[:, (2 * d + 1) * LANES:(2 * d + 2) * LANES])
                log_a = c1[d:d + 1] + c1[d:d + 1] * tr
                a = jnp.exp(log_a)
                y = -jnp.tanh(log_a) * (a * a + 1.0)
                b = (y * lax.rsqrt(jnp.maximum(y, F32_TINY))) * (half_xc + half_xc * ti)
                for k in range(per):
                    dst = pl.multiple_of((i * per + k) * pitch, SUBLANES)
                    dirs[d][0][pl.ds(dst, chunk), :] = a[k * chunk:(k + 1) * chunk]
                    dirs[d][1][pl.ds(dst, chunk), :] = b[k * chunk:(k + 1) * chunk]
            return carry

        if n == rows:
            body(0, 0)
        else:
            lax.fori_loop(0, n // rows, body, 0)

    def scan(n, h0_f, h0_b, keep):
        chunk = n // SUBLANES
        pitch = _scan_pitch(chunk)

        def steps(jo, carry):
            h_f, a_f, h_b, a_b = carry
            for u in range(SUBLANES):
                j = jo * SUBLANES + u
                av = ap_f[pl.ds(j, SUBLANES, stride=pitch), :]
                h_f = av * h_f + bp_f[pl.ds(j, SUBLANES, stride=pitch), :]
                a_f = av * a_f
                jb = chunk - 1 - j
                av = ap_b[pl.ds(jb, SUBLANES, stride=pitch), :]
                h_b = av * h_b + bp_b[pl.ds(jb, SUBLANES, stride=pitch), :]
                a_b = av * a_b
                if keep:
                    o = pl.multiple_of(j * SUBLANES, SUBLANES)
                    hl_f[pl.ds(o, SUBLANES), :] = h_f
                    al_f[pl.ds(o, SUBLANES), :] = a_f
                    hl_b[pl.ds(o, SUBLANES), :] = h_b
                    al_b[pl.ds(o, SUBLANES), :] = a_b
            return h_f, a_f, h_b, a_b

        zeros = jnp.zeros((SUBLANES, LANES), F32)
        ones = jnp.ones((SUBLANES, LANES), F32)
        h_f, a_f, h_b, a_b = lax.fori_loop(0, chunk // SUBLANES, steps, (zeros, ones, zeros, ones))

        in_f = [h0_f]
        for c in range(SUBLANES):
            in_f.append(a_f[c:c + 1] * in_f[c] + h_f[c:c + 1])
        in_b = [h0_b]
        for c in range(SUBLANES - 1, -1, -1):
            in_b.append(a_b[c:c + 1] * in_b[-1] + h_b[c:c + 1])
        if keep:
            hin_f = jnp.concatenate(in_f[:SUBLANES], axis=0)
            hin_b = jnp.concatenate(in_b[SUBLANES - 1::-1], axis=0)

            def fix(jo, carry):
                for u in range(SUBLANES):
                    j = jo * SUBLANES + u
                    o = pl.multiple_of(j * SUBLANES, SUBLANES)
                    hp_f[pl.ds(j, SUBLANES, stride=pitch), :] = (
                        hl_f[pl.ds(o, SUBLANES), :] + al_f[pl.ds(o, SUBLANES), :] * hin_f)
                    hp_b[pl.ds(chunk - 1 - j, SUBLANES, stride=pitch), :] = (
                        hl_b[pl.ds(o, SUBLANES), :] + al_b[pl.ds(o, SUBLANES), :] * hin_b)
                return carry
            lax.fori_loop(0, chunk // SUBLANES, fix, 0)
        return in_f[SUBLANES], in_b[SUBLANES]

    zero = jnp.zeros((1, LANES), F32)
    conv_into(xrc_ref[0], n_ctx)
    coefficients(n_ctx)
    h0_f, h0_b = scan(n_ctx, zero, zero, keep=False)

    conv_into(xr_ref[0], n_lat)
    coefficients(n_lat)
    scan(n_lat, h0_f, h0_b, keep=True)

    chunk = n_lat // SUBLANES
    pitch = _scan_pitch(chunk)

    def emit(c, carry):
        src = pl.multiple_of(c * chunk, 2 * SUBLANES)
        dst = pl.multiple_of(c * pitch, SUBLANES)
        hsum = hp_f[pl.ds(dst, chunk), :] + hp_b[pl.ds(dst, chunk), :]
        y = jax.nn.gelu(gr_ref[0, pl.ds(src, chunk), :], approximate=True) * hsum
        y_ref[0, pl.ds(src, chunk), :] = y.astype(y_ref.dtype)
        return carry
    lax.fori_loop(0, SUBLANES, emit, 0)


def _rnn(xr, xr_c, gr, conv_w, conv_b, wg, bgate, lam):
    bn, n, _ = xr.shape
    n_ctx = xr_c.shape[1]
    assert n % (SUBLANES * SUBLANES) == 0 and n_ctx % (SUBLANES * SUBLANES) == 0 and n_ctx <= n
    pitched = SUBLANES * _scan_pitch(n // SUBLANES)
    seq_spec = pl.BlockSpec((1, n, LANES), lambda b, p: (b, 0, p))
    return pl.pallas_call(
        functools.partial(_rnn_kernel, n_lat=n, n_ctx=n_ctx),
        grid=(bn, N_LANE_GROUPS),
        in_specs=[
            seq_spec,
            pl.BlockSpec((1, n_ctx, LANES), lambda b, p: (b, 0, p)),
            seq_spec,
            pl.BlockSpec((4, LANES), lambda b, p: (0, p)),
            pl.BlockSpec((1, LANES), lambda b, p: (0, p)),
            pl.BlockSpec((1, LANES, 4 * LANES), lambda b, p: (p, 0, 0)),
            pl.BlockSpec((1, 1, 4 * LANES), lambda b, p: (p, 0, 0)),
            pl.BlockSpec((2, LANES), lambda b, p: (0, p)),
        ],
        out_specs=seq_spec,
        out_shape=jax.ShapeDtypeStruct((bn, n, D_RNN), BF16),
        scratch_shapes=[pltpu.VMEM((n, LANES), F32)]
        + [pltpu.VMEM((pitched, LANES), F32)] * 4
        + [pltpu.VMEM((n, LANES), F32)] * 4
        + [pltpu.VMEM((pitched, LANES), F32)] * 2,
        compiler_params=pltpu.CompilerParams(vmem_limit_bytes=VMEM_LIMIT),
        name="rnn",
    )(xr, xr_c, gr, conv_w, conv_b.reshape(1, D_RNN), wg, bgate, lam)


def _gate_weights(rg_wa, rg_ba, rg_wx, rg_bx):
    eye = jnp.eye(2, dtype=F32)
    blocks, biases = [], []
    for d in range(2):
        for w, bvec in ((rg_wa[d], rg_ba[d]), (rg_wx[d], rg_bx[d])):
            w4 = w.reshape(N_LANE_GROUPS, 2, RNN_HEAD_DIM, RNN_HEAD_DIM)
            bd = jnp.einsum("paij,ac->paicj", w4, eye).reshape(N_LANE_GROUPS, LANES, LANES)
            blocks.append(bd)
            biases.append(bvec.reshape(N_LANE_GROUPS, 1, LANES))
    return jnp.concatenate(blocks, axis=-1).astype(BF16), jnp.concatenate(biases, axis=-1)


def _gconv_kernel(u_ref, bg_ref, w_ref, y_ref, *, n):
    p = pl.program_id(1)
    u = u_ref[0]
    w = w_ref[...]

    @pl.when(p < D_CONV_H // LANES)
    def _():
        col = lax.broadcasted_iota(jnp.int32, u.shape, 0) % GRID_W
        left = jnp.where(col > 0, _shift_rows(u, 1), 0.0)
        right = jnp.where(col < GRID_W - 1, _shift_rows(u, -1), 0.0)
        y_ref[0] = (bg_ref[0] * (w[0:1] * left + w[1:2] * u + w[2:3] * right)).astype(y_ref.dtype)

    @pl.when(p >= D_CONV_H // LANES)
    def _():
        y_ref[0] = (bg_ref[0] * (w[0:1] * _shift_rows(u, GRID_W) + w[1:2] * u
                                 + w[2:3] * _shift_rows(u, -GRID_W))).astype(y_ref.dtype)


def _gconv(u, bg, w):
    bn, n, _ = u.shape
    seq_spec = pl.BlockSpec((1, n, LANES), lambda b, p: (b, 0, p))
    return pl.pallas_call(
        functools.partial(_gconv_kernel, n=n),
        grid=(bn, D_CONV // LANES),
        in_specs=[seq_spec, seq_spec, pl.BlockSpec((3, LANES), lambda b, p: (0, p))],
        out_specs=seq_spec,
        out_shape=jax.ShapeDtypeStruct((bn, n, D_CONV), BF16),
        compiler_params=pltpu.CompilerParams(vmem_limit_bytes=VMEM_LIMIT),
        name="gconv",
    )(u, bg, w)


def _lane_max(x, mask):
    return jnp.max(jnp.where(mask, x, -jnp.inf), axis=-1, keepdims=True)


def _first_lane(cond, lane):
    return jnp.min(jnp.where(cond, lane, float(LANES)), axis=-1, keepdims=True)


def _outproj_kernel(x_ref, yr_ref, yc_ref, w32_ref, mod_ref, g_ref, wr_hi_ref, wr_lo_ref, br_ref,
                    x1_ref, mt_ref, route_ref, cnt_ref, carry, w_ref, *, tm):
    first = (pl.program_id(0) == 0) & (pl.program_id(1) == 0)

    @pl.when(first)
    def _():
        carry[...] = jnp.zeros_like(carry)
        w_ref[...] = w32_ref[...].astype(BF16)

    mix = _dot(yr_ref[0], w_ref[0:D_RNN, :]) + _dot(yc_ref[0], w_ref[D_RNN:, :])
    x1 = x_ref[0] + mod_ref[0, 2:3, :] * mix
    x1_ref[0] = x1
    m = _norm_mod(x1, g_ref[...], mod_ref[0, 4:5, :], mod_ref[0, 3:4, :])
    for s in range(ROW_TILES):
        mt_ref[pl.ds(s, tm, stride=ROW_TILES), :] = m[:, s * LANES:(s + 1) * LANES]

    m_hi, m_lo = _split_bf16(m)
    logits = (_dot(m_hi, wr_hi_ref[...]) + _dot(m_lo, wr_hi_ref[...]) + _dot(m_hi, wr_lo_ref[...])
              + br_ref[...])
    lane_i = lax.broadcasted_iota(jnp.int32, logits.shape, 1)
    lane = lane_i.astype(F32)
    is_grp = lane_i < N_GROUPS
    g_max = _lane_max(logits, is_grp)
    grp = _first_lane(is_grp & (logits == g_max), lane)
    p_g = 1.0 / jnp.sum(jnp.where(is_grp, jnp.exp(logits - g_max), 0.0), axis=-1, keepdims=True)
    lo_lane = EXPERT_LANE0 + grp * EXPERTS_PER_GROUP
    in_grp = (lane >= lo_lane) & (lane < lo_lane + EXPERTS_PER_GROUP)
    l1 = _lane_max(logits, in_grp)
    i1 = _first_lane(in_grp & (logits == l1), lane)
    rest = in_grp & (lane != i1)
    l2 = _lane_max(logits, rest)
    i2 = _first_lane(rest & (logits == l2), lane)
    r21 = jnp.exp(l2 - l1)
    gate1 = p_g / (1.0 + r21)
    gate2 = gate1 * r21

    oh1 = jnp.where(lane == i1, 1.0, 0.0)
    oh2 = jnp.where(lane == i2, 1.0, 0.0)
    both = (oh1 + oh2).astype(BF16)
    ti = lax.broadcasted_iota(jnp.int32, (tm, tm), 0)
    tj = lax.broadcasted_iota(jnp.int32, (tm, tm), 1)
    tri = jnp.where(tj < ti, 1.0, 0.0).astype(BF16)
    before = _dot(tri, both) + carry[...]
    rank1 = jnp.sum(oh1 * before, axis=-1, keepdims=True)
    rank2 = jnp.sum(oh2 * before, axis=-1, keepdims=True)
    total = carry[...] + jnp.sum(oh1 + oh2, axis=0, keepdims=True)
    carry[...] = total
    cnt_ref[...] = total

    e1 = i1 - EXPERT_LANE0
    e2 = i2 - EXPERT_LANE0
    out = jnp.zeros(logits.shape, F32)
    for k, val in enumerate((e1, e2, gate1, gate2, rank1, rank2)):
        out = jnp.where(lane_i == k, val, out)
    route_ref[...] = out


def _outproj(x, y_rnn, y_conv, w_out, mod3, norm_g, wr_hi, wr_lo, br):
    bn, n, d = x.shape
    tm = min(TOK_TILE, n)
    nt = n // tm
    t_all = bn * n
    return pl.pallas_call(
        functools.partial(_outproj_kernel, tm=tm),
        grid=(bn, nt),
        in_specs=[
            pl.BlockSpec((1, tm, d), lambda b, i: (b, i, 0)),
            pl.BlockSpec((1, tm, D_RNN), lambda b, i: (b, i, 0)),
            pl.BlockSpec((1, tm, D_CONV), lambda b, i: (b, i, 0)),
            pl.BlockSpec((D_RNN + D_CONV, d), lambda b, i: (0, 0), pipeline_mode=pl.Buffered(1)),
            pl.BlockSpec((1, 6, d), lambda b, i: (b, 0, 0)),
            pl.BlockSpec((1, d), lambda b, i: (0, 0)),
            pl.BlockSpec((d, LANES), lambda b, i: (0, 0)),
            pl.BlockSpec((d, LANES), lambda b, i: (0, 0)),
            pl.BlockSpec((1, LANES), lambda b, i: (0, 0)),
        ],
        out_specs=[
            pl.BlockSpec((1, tm, d), lambda b, i: (b, i, 0)),
            pl.BlockSpec((tm * ROW_TILES, LANES), lambda b, i: (b * nt + i, 0)),
            pl.BlockSpec((tm, LANES), lambda b, i: (b * nt + i, 0)),
            pl.BlockSpec((1, LANES), lambda b, i: (0, 0)),
        ],
        out_shape=[
            jax.ShapeDtypeStruct((bn, n, d), F32),
            jax.ShapeDtypeStruct((t_all * ROW_TILES, LANES), F32),
            jax.ShapeDtypeStruct((t_all, LANES), F32),
            jax.ShapeDtypeStruct((1, LANES), F32),
        ],
        scratch_shapes=[pltpu.VMEM((1, LANES), F32), pltpu.VMEM((D_RNN + D_CONV, d), BF16)],
        compiler_params=pltpu.CompilerParams(
            dimension_semantics=("arbitrary", "arbitrary"), vmem_limit_bytes=VMEM_LIMIT),
        name="outproj",
    )(x, y_rnn, y_conv, w_out, mod3, norm_g.reshape(1, d), wr_hi, wr_lo, br)


def _row_tile(ref, row):
    return ref.at[pl.ds(pl.multiple_of(row * ROW_TILES, ROW_TILES), ROW_TILES)]


def _slotmap_kernel(dest_ref, zeros_hbm, asg_ref, sem):
    fill = pltpu.make_async_copy(zeros_hbm, asg_ref, sem)
    fill.start()
    fill.wait()

    def body(c, carry):
        for u in range(DMA_UNROLL):
            a = c * DMA_UNROLL + u
            asg_ref[dest_ref[a]] = a
        return carry
    lax.fori_loop(0, dest_ref.shape[0] // DMA_UNROLL, body, 0)


def _slotmap(dest, n_slots):
    return pl.pallas_call(
        _slotmap_kernel,
        in_specs=[pl.BlockSpec(memory_space=pltpu.SMEM), pl.BlockSpec(memory_space=pl.ANY)],
        out_specs=pl.BlockSpec(memory_space=pltpu.SMEM),
        out_shape=jax.ShapeDtypeStruct((n_slots,), jnp.int32),
        scratch_shapes=[pltpu.SemaphoreType.DMA],
        name="slotmap",
    )(dest, jnp.zeros((n_slots,), jnp.int32))


def _expert_kernel(be_ref, nu_ref, asg_hbm, mt_hbm, wg_ref, wu_ref, wd_ref, yb_ref,
                   xbuf, sems, idx, isems, wg_s, wu_s, wd_s):
    j = pl.program_id(0)
    n_used = nu_ref[0]
    last = n_used - 1

    def idx_copy(blk, sl):
        return pltpu.make_async_copy(asg_hbm.at[blk], idx.at[sl], isems.at[sl])

    def row_copy(tok, slot, r):
        return pltpu.make_async_copy(_row_tile(mt_hbm, tok), _row_tile(xbuf.at[slot], r), sems.at[slot])

    def issue_rows(slot, r0, n):
        for u in range(n):
            tok = lax.shift_right_logical(idx[slot, 0, r0 + u], 1)
            row_copy(tok, slot, r0 + u).start(priority=u % 2)

    def drain(slot):
        def body(c, carry):
            for u in range(DMA_UNROLL):
                row_copy(0, slot, 0).wait()
            return carry
        lax.fori_loop(0, MOE_BLK // DMA_UNROLL, body, 0)

    @pl.when(j >= n_used)
    def _():
        yb_ref[...] = jnp.zeros_like(yb_ref)

    @pl.when(j < n_used)
    def _():
        slot = j % EXPERT_RING
        ahead = (j + EXPERT_RING - 1) % EXPERT_RING

        @pl.when(j == 0)
        def _():
            for b in range(EXPERT_RING - 1):
                idx_copy(jnp.minimum(b, last), b).start()
                idx_copy(0, b).wait()

                def body(c, carry, b=b):
                    issue_rows(b, c * DMA_UNROLL, DMA_UNROLL)
                    return carry
                lax.fori_loop(0, MOE_BLK // DMA_UNROLL, body, 0)
            idx_copy(jnp.minimum(EXPERT_RING - 1, last), EXPERT_RING - 1).start()

        e = be_ref[j]
        prev = be_ref[jnp.maximum(j - 1, 0)]

        @pl.when((j == 0) | (e != prev))
        def _():
            wg_s[...] = wg_ref[0].astype(BF16)
            wu_s[...] = wu_ref[0].astype(BF16)
            wd_s[...] = wd_ref[0].astype(BF16)

        drain(slot)
        idx_copy(0, ahead).wait()

        n_pieces = (2 * D_EXPERT // MXU_TILE) * (D_MODEL // MXU_TILE) + (
            D_MODEL // MXU_TILE) * (D_EXPERT // MXU_TILE)
        bounds = [(p * MOE_BLK) // n_pieces for p in range(n_pieces + 1)]
        pieces = iter(zip(bounds[:-1], bounds[1:]))

        def dot_pieces(a, w_ref, n0):
            acc = None
            for k0 in range(0, a.shape[1], MXU_TILE):
                part = _dot(a[:, k0:k0 + MXU_TILE], w_ref[k0:k0 + MXU_TILE, n0:n0 + MXU_TILE])
                acc = part if acc is None else acc + part
                r0, r1 = next(pieces)
                issue_rows(ahead, r0, r1 - r0)
            return acc

        x = jnp.concatenate(
            [xbuf[slot, pl.ds(s, MOE_BLK, stride=ROW_TILES), :] for s in range(ROW_TILES)], axis=-1)
        xb16 = x.astype(BF16)
        acts = []
        for n0 in range(0, D_EXPERT, MXU_TILE):
            gate = dot_pieces(xb16, wg_s, n0)
            up = dot_pieces(xb16, wu_s, n0)
            acts.append(((gate * jax.nn.sigmoid(gate)) * up).astype(BF16))
        h = jnp.concatenate(acts, axis=-1)
        for n0 in range(0, D_MODEL, MXU_TILE):
            y = dot_pieces(h, wd_s, n0)
            for s in range(MXU_TILE // LANES):
                yb_ref[pl.ds(n0 // LANES + s, MOE_BLK, stride=ROW_TILES), :] = (
                    y[:, s * LANES:(s + 1) * LANES])

        @pl.when(j < last)
        def _():
            idx_copy(jnp.minimum(j + EXPERT_RING, last), slot).start()

        @pl.when(j == last)
        def _():
            for b in range(1, EXPERT_RING):
                drain((j + b) % EXPERT_RING)


def _experts(block_e, n_used, slot_asg, mt, w_gate, w_up, w_down, n_blocks):
    def wsel(j, be, nu):
        return (be[jnp.minimum(j, nu[0] - 1)], 0, 0)

    return pl.pallas_call(
        _expert_kernel,
        grid_spec=pltpu.PrefetchScalarGridSpec(
            num_scalar_prefetch=2,
            grid=(n_blocks,),
            in_specs=[
                pl.BlockSpec(memory_space=pl.ANY),
                pl.BlockSpec(memory_space=pl.ANY),
                pl.BlockSpec((1, D_MODEL, D_EXPERT), wsel),
                pl.BlockSpec((1, D_MODEL, D_EXPERT), wsel),
                pl.BlockSpec((1, D_EXPERT, D_MODEL), wsel),
            ],
            out_specs=pl.BlockSpec((MOE_BLK * ROW_TILES, LANES), lambda j, be, nu: (j, 0)),
            scratch_shapes=[
                pltpu.VMEM((EXPERT_RING, MOE_BLK * ROW_TILES, LANES), F32),
                pltpu.SemaphoreType.DMA((EXPERT_RING,)),
                pltpu.SMEM((EXPERT_RING, 1, MOE_BLK), jnp.int32),
                pltpu.SemaphoreType.DMA((EXPERT_RING,)),
                pltpu.VMEM((D_MODEL, D_EXPERT), BF16),
                pltpu.VMEM((D_MODEL, D_EXPERT), BF16),
                pltpu.VMEM((D_EXPERT, D_MODEL), BF16),
            ],
        ),
        out_shape=jax.ShapeDtypeStruct((n_blocks * MOE_BLK * ROW_TILES, LANES), F32),
        compiler_params=pltpu.CompilerParams(
            dimension_semantics=("arbitrary",), vmem_limit_bytes=VMEM_LIMIT),
        name="expert",
    )(block_e, n_used, slot_asg.reshape(n_blocks, 1, MOE_BLK), mt, w_gate, w_up, w_down)


def _combine_kernel(dest_ref, yb_hbm, x1_ref, route_ref, mod_ref, g_ref, o_ref, ybuf, sems, *, tc):
    i = pl.program_id(0)
    slot = i % 2

    def row_copy(d, sl, k, r):
        return pltpu.make_async_copy(_row_tile(yb_hbm, d), _row_tile(ybuf.at[sl, k], r), sems.at[sl])

    def gather(step, sl):
        def issue(c, carry):
            for u in range(DMA_UNROLL):
                r = c * DMA_UNROLL + u
                for k in range(TOP_K):
                    row_copy(dest_ref[TOP_K * (step * tc + r) + k], sl, k, r).start(priority=k)
            return carry
        lax.fori_loop(0, tc // DMA_UNROLL, issue, 0)

    @pl.when(i == 0)
    def _():
        gather(0, 0)

    @pl.when(i + 1 < pl.num_programs(0))
    def _():
        gather(i + 1, 1 - slot)

    def drain(c, carry):
        for u in range(DMA_UNROLL * TOP_K):
            row_copy(0, slot, 0, 0).wait()
        return carry
    lax.fori_loop(0, tc // DMA_UNROLL, drain, 0)

    def rows(k):
        return jnp.concatenate(
            [ybuf[slot, k, pl.ds(s, tc, stride=ROW_TILES), :] for s in range(ROW_TILES)], axis=-1)

    route = route_ref[...]
    y = route[:, 2:3] * rows(0) + route[:, 3:4] * rows(1)
    x2 = x1_ref[...] + mod_ref[0, 5:6, :] * y
    ms = jnp.mean(x2 * x2, axis=-1, keepdims=True)
    o_ref[...] = x2 * lax.rsqrt(ms + NORM_EPS) * g_ref[...]


def _combine(dest, yb, x1_2d, route, mod3, final_g, seq):
    t_all, d = x1_2d.shape
    tc = min(COMBINE_TILE, seq)
    per_seq = seq // tc
    return pl.pallas_call(
        functools.partial(_combine_kernel, tc=tc),
        grid_spec=pltpu.PrefetchScalarGridSpec(
            num_scalar_prefetch=1,
            grid=(t_all // tc,),
            in_specs=[
                pl.BlockSpec(memory_space=pl.ANY),
                pl.BlockSpec((tc, d), lambda i, dest: (i, 0)),
                pl.BlockSpec((tc, LANES), lambda i, dest: (i, 0)),
                pl.BlockSpec((1, 6, d), lambda i, dest: (i // per_seq, 0, 0)),
                pl.BlockSpec((1, d), lambda i, dest: (0, 0)),
            ],
            out_specs=pl.BlockSpec((tc, d), lambda i, dest: (i, 0)),
            scratch_shapes=[
                pltpu.VMEM((2, TOP_K, tc * ROW_TILES, LANES), F32),
                pltpu.SemaphoreType.DMA((2,)),
            ],
        ),
        out_shape=jax.ShapeDtypeStruct((t_all, d), F32),
        compiler_params=pltpu.CompilerParams(
            dimension_semantics=("arbitrary",), vmem_limit_bytes=VMEM_LIMIT),
        name="combine",
    )(dest, yb, x1_2d, route, mod3, final_g.reshape(1, d))


def kernel(x, c, ctx, c_ctx, ada_w, ada_b, norm1_g, norm2_g, w_in, rnn_conv_w, rnn_conv_b, rg_wa, rg_ba,
           rg_wx, rg_bx, rg_lambda, sc_conv_w, w_out, router_group_w, router_group_b, router_exp_w,
           router_exp_b, exp_w_gate, exp_w_up, exp_w_down, final_norm_g):
    bn, seq, d = x.shape
    assert d == D_MODEL and bn < MOD_ROWS and ada_w.shape[0] == 1
    t_all = bn * seq

    cc = jnp.concatenate([c, c_ctx[None], jnp.zeros((MOD_ROWS - bn - 1, d), F32)], axis=0)
    mod3 = _modulation(cc, ada_w[0], ada_b[0]).reshape(MOD_ROWS, 6, d)

    xr, gr, u, bg = _inproj(x, mod3, None, norm1_g[0], w_in[0], latent=True)
    (xr_c,) = _inproj(ctx, mod3, bn, norm1_g[0], w_in[0], latent=False)

    wg, bgate = _gate_weights(rg_wa[0], rg_ba[0], rg_wx[0], rg_bx[0])
    y_rnn = _rnn(xr, xr_c, gr, rnn_conv_w[0], rnn_conv_b[0], wg, bgate, rg_lambda[0])
    y_conv = _gconv(u, bg, sc_conv_w[0])

    wr = jnp.zeros((d, LANES), F32)
    wr = wr.at[:, :N_GROUPS].set(router_group_w[0]).at[:, EXPERT_LANE0:EXPERT_LANE0 + N_EXPERTS].set(router_exp_w[0])
    br = jnp.zeros((1, LANES), F32)
    br = br.at[0, :N_GROUPS].set(router_group_b[0]).at[0, EXPERT_LANE0:EXPERT_LANE0 + N_EXPERTS].set(router_exp_b[0])
    wr_hi, wr_lo = _split_bf16(wr)
    x1, mt, route, cnt = _outproj(x, y_rnn, y_conv, w_out[0], mod3, norm2_g[0], wr_hi, wr_lo, br)

    n_assign = t_all * TOP_K
    n_blocks = (n_assign + N_EXPERTS * (MOE_BLK - 1) + MOE_BLK - 1) // MOE_BLK
    counts = cnt[0, EXPERT_LANE0:EXPERT_LANE0 + N_EXPERTS].astype(jnp.int32)
    pcounts = (counts + MOE_BLK - 1) // MOE_BLK * MOE_BLK
    pends = jnp.cumsum(pcounts)
    pstarts = pends - pcounts
    experts = route[:, 0:TOP_K].astype(jnp.int32)
    ranks = route[:, 4:4 + TOP_K].astype(jnp.int32)
    onehot = experts[:, :, None] == jnp.arange(N_EXPERTS, dtype=jnp.int32)
    dest = (ranks + jnp.sum(jnp.where(onehot, pstarts, 0), axis=-1)).reshape(n_assign)
    n_used = (pends[-1] // MOE_BLK).astype(jnp.int32)
    blk_start = jnp.arange(n_blocks, dtype=jnp.int32) * MOE_BLK
    block_e = jnp.minimum(jnp.sum(blk_start[:, None] >= pends[None, :], axis=1), N_EXPERTS - 1)
    last_e = jnp.max(jnp.where(counts > 0, jnp.arange(N_EXPERTS, dtype=jnp.int32), 0))
    block_e = jnp.where(blk_start < pends[-1], block_e, last_e).astype(jnp.int32)

    n_slots = n_blocks * MOE_BLK
    slot_asg = _slotmap(dest, n_slots)
    yb = _experts(block_e, n_used.reshape(1), slot_asg, mt, exp_w_gate[0], exp_w_up[0], exp_w_down[0],
                  n_blocks)
    out = _combine(dest, yb, x1.reshape(t_all, d), route, mod3, final_norm_g, seq)
    return out.reshape(bn, seq, d)
```

```python
import functools

import jax
import jax.numpy as jnp
from jax import lax
from jax.experimental import pallas as pl
from jax.experimental.pallas import tpu as pltpu

F32 = jnp.float32
BF16 = jnp.bfloat16

D_MODEL = 1024
D_RNN = 512
D_CONV = 512
D_CONV_H = D_CONV // 2
RNN_HEADS = 8
RNN_HEAD_DIM = D_RNN // RNN_HEADS
GRID_W = 64
RG_C = 8.0
N_GROUPS = 4
EXPERTS_PER_GROUP = 8
N_EXPERTS = N_GROUPS * EXPERTS_PER_GROUP
TOP_K = 2
D_EXPERT = 512
NORM_EPS = 1e-6
F32_TINY = 1.1754944e-38

LANES = 128
SUBLANES = 8
ROW_TILES = D_MODEL // LANES
PACK_TILES = ROW_TILES // 2
N_LANE_GROUPS = D_RNN // LANES
EXPERT_LANE0 = N_GROUPS

MOD_ROWS = 16
MOD_TN = 768
TOK_TILE = 512
COEFF_ROWS = 512
MOE_BLK = 256
MXU_TILE = 256
COMBINE_TILE = 256
DMA_UNROLL = 16
VMEM_LIMIT = 48 * 1024 * 1024
EXPERT_VMEM_LIMIT = 58 * 1024 * 1024


def _dot(a, b):
    return jnp.dot(a, b, preferred_element_type=F32)


def _split_bf16(x):
    hi = x.astype(BF16)
    lo = (x - hi.astype(F32)).astype(BF16)
    return hi, lo


def _mod_kernel(cc_ref, w_ref, b_ref, o_ref):
    s = cc_ref[...]
    s = s * jax.nn.sigmoid(s)
    s_hi, s_lo = _split_bf16(s)
    w_hi, w_lo = _split_bf16(w_ref[...])
    o_ref[...] = _dot(s_hi, w_hi) + _dot(s_lo, w_hi) + _dot(s_hi, w_lo) + b_ref[...]


def _modulation(cc, ada_w, ada_b):
    n = ada_w.shape[1]
    return pl.pallas_call(
        _mod_kernel,
        grid=(n // MOD_TN,),
        in_specs=[
            pl.BlockSpec((MOD_ROWS, D_MODEL), lambda j: (0, 0)),
            pl.BlockSpec((D_MODEL, MOD_TN), lambda j: (0, j)),
            pl.BlockSpec((1, MOD_TN), lambda j: (0, j)),
        ],
        out_specs=pl.BlockSpec((MOD_ROWS, MOD_TN), lambda j: (0, j)),
        out_shape=jax.ShapeDtypeStruct((MOD_ROWS, n), F32),
        compiler_params=pltpu.CompilerParams(vmem_limit_bytes=VMEM_LIMIT),
        name="mod",
    )(cc, ada_w, ada_b.reshape(1, n))


def _norm_mod(x, g, scale, shift):
    ms = jnp.mean(x * x, axis=-1, keepdims=True)
    y = x * lax.rsqrt(ms + NORM_EPS) * g
    return y * (1.0 + scale) + shift


def _inproj_kernel(x_ref, mod_ref, g_ref, w32_ref, *refs, latent):
    out_refs, w_ref = refs[:-1], refs[-1]

    @pl.when((pl.program_id(0) == 0) & (pl.program_id(1) == 0))
    def _():
        w_ref[...] = w32_ref[...].astype(BF16)

    h = _norm_mod(x_ref[0], g_ref[...], mod_ref[0, 1:2, :], mod_ref[0, 0:1, :])
    hb = h.astype(BF16)
    xr = _dot(hb, w_ref[:, 0:D_RNN])
    out_refs[0][0] = xr
    if latent:
        o = D_RNN
        out_refs[1][0] = _dot(hb, w_ref[:, o:o + D_RNN])
        o += D_RNN
        v = _dot(hb, w_ref[:, o:o + D_CONV])
        out_refs[3][0] = _dot(hb, w_ref[:, o + D_CONV:o + 2 * D_CONV])
        cg = _dot(hb, w_ref[:, o + 2 * D_CONV:o + 3 * D_CONV])
        out_refs[2][0] = cg * v


def _inproj(x, mod3, mod_row, norm_g, w_in, latent):
    bn, n, d = x.shape
    tm = min(TOK_TILE, n)
    n_out = 4 if latent else 1
    width = w_in.shape[1] if latent else D_RNN
    mod_map = (lambda b, i: (b, 0, 0)) if mod_row is None else (lambda b, i: (mod_row, 0, 0))
    return pl.pallas_call(
        functools.partial(_inproj_kernel, latent=latent),
        grid=(bn, n // tm),
        in_specs=[
            pl.BlockSpec((1, tm, d), lambda b, i: (b, i, 0)),
            pl.BlockSpec((1, 6, d), mod_map),
            pl.BlockSpec((1, d), lambda b, i: (0, 0)),
            pl.BlockSpec((d, width), lambda b, i: (0, 0), pipeline_mode=pl.Buffered(1)),
        ],
        out_specs=[pl.BlockSpec((1, tm, D_RNN), lambda b, i: (b, i, 0))] * n_out,
        out_shape=[jax.ShapeDtypeStruct((bn, n, D_RNN), F32)] * n_out,
        scratch_shapes=[pltpu.VMEM((d, width), BF16)],
        compiler_params=pltpu.CompilerParams(
            dimension_semantics=("arbitrary", "arbitrary"), vmem_limit_bytes=VMEM_LIMIT),
        name="inproj_lat" if latent else "inproj_ctx",
    )(x, mod3, norm_g.reshape(1, d), w_in)


def _shift_rows(x, k):
    n = x.shape[0]
    row = lax.broadcasted_iota(jnp.int32, x.shape, 0)
    rolled = pltpu.roll(x, k % n, axis=0)
    valid = (row >= k) if k > 0 else (row < n + k)
    return jnp.where(valid, rolled, 0.0)


def _scan_pitch(chunk):
    pitch = chunk + SUBLANES
    return pitch if (pitch // SUBLANES) % 2 else pitch + SUBLANES


def _rnn_kernel(xr_ref, xrc_ref, gr_ref, cw_ref, cb_ref, wg_ref, bg_ref, lam_ref, y_ref,
                xc_s, ap_f, bp_f, ap_b, bp_b, hl_f, al_f, hl_b, al_b, hp_f, hp_b, *, n_lat, n_ctx):
    nl = -lam_ref[...]
    sp = jnp.maximum(nl, 0.0) + jnp.log1p(jnp.exp(-jnp.abs(nl)))
    c1 = (-0.5 * RG_C) * sp
    cw = cw_ref[...]
    bias = cb_ref[...]
    wg = wg_ref[0]
    bg = bg_ref[0]
    dirs = ((ap_f, bp_f, hl_f, al_f, hp_f), (ap_b, bp_b, hl_b, al_b, hp_b))

    def conv_into(x, n):
        xc_s[pl.ds(0, n), :] = (cw[0:1] * _shift_rows(x, 2) + cw[1:2] * _shift_rows(x, 1)
                                + cw[2:3] * x + cw[3:4] * _shift_rows(x, -1)) + bias

    def coefficients(n):
        chunk = n // SUBLANES
        pitch = _scan_pitch(chunk)
        rows = max(chunk, min(n, COEFF_ROWS))
        per = rows // chunk

        def body(i, carry):
            xc = xc_s[pl.ds(pl.multiple_of(i * rows, SUBLANES), rows), :]
            gates = _dot(xc.astype(BF16), wg) + bg
            half_xc = 0.5 * xc
            for d in range(2):
                tr = jnp.tanh(0.5 * gates[:, (2 * d) * LANES:(2 * d + 1) * LANES])
                ti = jnp.tanh(0.5 * gates[:, (2 * d + 1) * LANES:(2 * d + 2) * LANES])
                log_a = c1[d:d + 1] + c1[d:d + 1] * tr
                a = jnp.exp(log_a)
                y = -jnp.tanh(log_a) * (a * a + 1.0)
                b = (y * lax.rsqrt(jnp.maximum(y, F32_TINY))) * (half_xc + half_xc * ti)
                for k in range(per):
                    dst = pl.multiple_of((i * per + k) * pitch, SUBLANES)
                    dirs[d][0][pl.ds(dst, chunk), :] = a[k * chunk:(k + 1) * chunk]
                    dirs[d][1][pl.ds(dst, chunk), :] = b[k * chunk:(k + 1) * chunk]
            return carry

        if n == rows:
            body(0, 0)
        else:
            lax.fori_loop(0, n // rows, body, 0)

    def scan(n, h0_f, h0_b, keep):
        chunk = n // SUBLANES
        pitch = _scan_pitch(chunk)

        def steps(jo, carry):
            h_f, a_f, h_b, a_b = carry
            for u in range(SUBLANES):
                j = jo * SUBLANES + u
                av = ap_f[pl.ds(j, SUBLANES, stride=pitch), :]
                h_f = av * h_f + bp_f[pl.ds(j, SUBLANES, stride=pitch), :]
                a_f = av * a_f
                jb = chunk - 1 - j
                av = ap_b[pl.ds(jb, SUBLANES, stride=pitch), :]
                h_b = av * h_b + bp_b[pl.ds(jb, SUBLANES, stride=pitch), :]
                a_b = av * a_b
                if keep:
                    o = pl.multiple_of(j * SUBLANES, SUBLANES)
                    hl_f[pl.ds(o, SUBLANES), :] = h_f
                    al_f[pl.ds(o, SUBLANES), :] = a_f
                    hl_b[pl.ds(o, SUBLANES), :] = h_b
                    al_b[pl.ds(o, SUBLANES), :] = a_b
            return h_f, a_f, h_b, a_b

        zeros = jnp.zeros((SUBLANES, LANES), F32)
        ones = jnp.ones((SUBLANES, LANES), F32)
        h_f, a_f, h_b, a_b = lax.fori_loop(0, chunk // SUBLANES, steps, (zeros, ones, zeros, ones))

        in_f = [h0_f]
        for c in range(SUBLANES):
            in_f.append(a_f[c:c + 1] * in_f[c] + h_f[c:c + 1])
        in_b = [h0_b]
        for c in range(SUBLANES - 1, -1, -1):
            in_b.append(a_b[c:c + 1] * in_b[-1] + h_b[c:c + 1])
        if keep:
            hin_f = jnp.concatenate(in_f[:SUBLANES], axis=0)
            hin_b = jnp.concatenate(in_b[SUBLANES - 1::-1], axis=0)

            def fix(jo, carry):
                for u in range(SUBLANES):
                    j = jo * SUBLANES + u
                    o = pl.multiple_of(j * SUBLANES, SUBLANES)
                    hp_f[pl.ds(j, SUBLANES, stride=pitch), :] = (
                        hl_f[pl.ds(o, SUBLANES), :] + al_f[pl.ds(o, SUBLANES), :] * hin_f)
                    hp_b[pl.ds(chunk - 1 - j, SUBLANES, stride=pitch), :] = (
                        hl_b[pl.ds(o, SUBLANES), :] + al_b[pl.ds(o, SUBLANES), :] * hin_b)
                return carry
            lax.fori_loop(0, chunk // SUBLANES, fix, 0)
        return in_f[SUBLANES], in_b[SUBLANES]

    zero = jnp.zeros((1, LANES), F32)
    conv_into(xrc_ref[0], n_ctx)
    coefficients(n_ctx)
    h0_f, h0_b = scan(n_ctx, zero, zero, keep=False)

    conv_into(xr_ref[0], n_lat)
    coefficients(n_lat)
    scan(n_lat, h0_f, h0_b, keep=True)

    chunk = n_lat // SUBLANES
    pitch = _scan_pitch(chunk)

    def emit(c, carry):
        src = pl.multiple_of(c * chunk, 2 * SUBLANES)
        dst = pl.multiple_of(c * pitch, SUBLANES)
        hsum = hp_f[pl.ds(dst, chunk), :] + hp_b[pl.ds(dst, chunk), :]
        y = jax.nn.gelu(gr_ref[0, pl.ds(src, chunk), :], approximate=True) * hsum
        y_ref[0, pl.ds(src, chunk), :] = y.astype(y_ref.dtype)
        return carry
    lax.fori_loop(0, SUBLANES, emit, 0)


def _rnn(xr, xr_c, gr, conv_w, conv_b, wg, bgate, lam):
    bn, n, _ = xr.shape
    n_ctx = xr_c.shape[1]
    assert n % (SUBLANES * SUBLANES) == 0 and n_ctx % (SUBLANES * SUBLANES) == 0 and n_ctx <= n
    pitched = SUBLANES * _scan_pitch(n // SUBLANES)
    seq_spec = pl.BlockSpec((1, n, LANES), lambda b, p: (b, 0, p))
    return pl.pallas_call(
        functools.partial(_rnn_kernel, n_lat=n, n_ctx=n_ctx),
        grid=(bn, N_LANE_GROUPS),
        in_specs=[
            seq_spec,
            pl.BlockSpec((1, n_ctx, LANES), lambda b, p: (b, 0, p)),
            seq_spec,
            pl.BlockSpec((4, LANES), lambda b, p: (0, p)),
            pl.BlockSpec((1, LANES), lambda b, p: (0, p)),
            pl.BlockSpec((1, LANES, 4 * LANES), lambda b, p: (p, 0, 0)),
            pl.BlockSpec((1, 1, 4 * LANES), lambda b, p: (p, 0, 0)),
            pl.BlockSpec((2, LANES), lambda b, p: (0, p)),
        ],
        out_specs=seq_spec,
        out_shape=jax.ShapeDtypeStruct((bn, n, D_RNN), BF16),
        scratch_shapes=[pltpu.VMEM((n, LANES), F32)]
        + [pltpu.VMEM((pitched, LANES), F32)] * 4
        + [pltpu.VMEM((n, LANES), F32)] * 4
        + [pltpu.VMEM((pitched, LANES), F32)] * 2,
        compiler_params=pltpu.CompilerParams(vmem_limit_bytes=VMEM_LIMIT),
        name="rnn",
    )(xr, xr_c, gr, conv_w, conv_b.reshape(1, D_RNN), wg, bgate, lam)


def _gate_weights(rg_wa, rg_ba, rg_wx, rg_bx):
    eye = jnp.eye(2, dtype=F32)
    blocks, biases = [], []
    for d in range(2):
        for w, bvec in ((rg_wa[d], rg_ba[d]), (rg_wx[d], rg_bx[d])):
            w4 = w.reshape(N_LANE_GROUPS, 2, RNN_HEAD_DIM, RNN_HEAD_DIM)
            bd = jnp.einsum("paij,ac->paicj", w4, eye).reshape(N_LANE_GROUPS, LANES, LANES)
            blocks.append(bd)
            biases.append(bvec.reshape(N_LANE_GROUPS, 1, LANES))
    return jnp.concatenate(blocks, axis=-1).astype(BF16), jnp.concatenate(biases, axis=-1)


def _gconv_kernel(u_ref, bg_ref, w_ref, y_ref, *, n):
    p = pl.program_id(1)
    u = u_ref[0]
    w = w_ref[...]

    @pl.when(p < D_CONV_H // LANES)
    def _():
        col = lax.broadcasted_iota(jnp.int32, u.shape, 0) % GRID_W
        left = jnp.where(col > 0, _shift_rows(u, 1), 0.0)
        right = jnp.where(col < GRID_W - 1, _shift_rows(u, -1), 0.0)
        y_ref[0] = (bg_ref[0] * (w[0:1] * left + w[1:2] * u + w[2:3] * right)).astype(y_ref.dtype)

    @pl.when(p >= D_CONV_H // LANES)
    def _():
        y_ref[0] = (bg_ref[0] * (w[0:1] * _shift_rows(u, GRID_W) + w[1:2] * u
                                 + w[2:3] * _shift_rows(u, -GRID_W))).astype(y_ref.dtype)


def _gconv(u, bg, w):
    bn, n, _ = u.shape
    seq_spec = pl.BlockSpec((1, n, LANES), lambda b, p: (b, 0, p))
    return pl.pallas_call(
        functools.partial(_gconv_kernel, n=n),
        grid=(bn, D_CONV // LANES),
        in_specs=[seq_spec, seq_spec, pl.BlockSpec((3, LANES), lambda b, p: (0, p))],
        out_specs=seq_spec,
        out_shape=jax.ShapeDtypeStruct((bn, n, D_CONV), BF16),
        compiler_params=pltpu.CompilerParams(vmem_limit_bytes=VMEM_LIMIT),
        name="gconv",
    )(u, bg, w)


def _lane_max(x, mask):
    return jnp.max(jnp.where(mask, x, -jnp.inf), axis=-1, keepdims=True)


def _first_lane(cond, lane):
    return jnp.min(jnp.where(cond, lane, float(LANES)), axis=-1, keepdims=True)


def _outproj_kernel(x_ref, yr_ref, yc_ref, w32_ref, mod_ref, g_ref, wr_hi_ref, wr_lo_ref, br_ref,
                    x1_ref, mt_ref, route_ref, cnt_ref, carry, w_ref, *, tm):
    first = (pl.program_id(0) == 0) & (pl.program_id(1) == 0)

    @pl.when(first)
    def _():
        carry[...] = jnp.zeros_like(carry)
        w_ref[...] = w32_ref[...].astype(BF16)

    mix = _dot(yr_ref[0], w_ref[0:D_RNN, :]) + _dot(yc_ref[0], w_ref[D_RNN:, :])
    x1 = x_ref[0] + mod_ref[0, 2:3, :] * mix
    x1_ref[0] = x1
    m = _norm_mod(x1, g_ref[...], mod_ref[0, 4:5, :], mod_ref[0, 3:4, :])
    half = D_MODEL // 2
    packed = pltpu.pack_elementwise([m[:, :half], m[:, half:]], packed_dtype=BF16)
    for s in range(PACK_TILES):
        mt_ref[pl.ds(s, tm, stride=PACK_TILES), :] = packed[:, s * LANES:(s + 1) * LANES]

    m_hi, m_lo = _split_bf16(m)
    logits = (_dot(m_hi, wr_hi_ref[...]) + _dot(m_lo, wr_hi_ref[...]) + _dot(m_hi, wr_lo_ref[...])
              + br_ref[...])
    lane_i = lax.broadcasted_iota(jnp.int32, logits.shape, 1)
    lane = lane_i.astype(F32)
    is_grp = lane_i < N_GROUPS
    g_max = _lane_max(logits, is_grp)
    grp = _first_lane(is_grp & (logits == g_max), lane)
    p_g = 1.0 / jnp.sum(jnp.where(is_grp, jnp.exp(logits - g_max), 0.0), axis=-1, keepdims=True)
    lo_lane = EXPERT_LANE0 + grp * EXPERTS_PER_GROUP
    in_grp = (lane >= lo_lane) & (lane < lo_lane + EXPERTS_PER_GROUP)
    l1 = _lane_max(logits, in_grp)
    i1 = _first_lane(in_grp & (logits == l1), lane)
    rest = in_grp & (lane != i1)
    l2 = _lane_max(logits, rest)
    i2 = _first_lane(rest & (logits == l2), lane)
    r21 = jnp.exp(l2 - l1)
    gate1 = p_g / (1.0 + r21)
    gate2 = gate1 * r21

    oh1 = jnp.where(lane == i1, 1.0, 0.0)
    oh2 = jnp.where(lane == i2, 1.0, 0.0)
    both = (oh1 + oh2).astype(BF16)
    ti = lax.broadcasted_iota(jnp.int32, (tm, tm), 0)
    tj = lax.broadcasted_iota(jnp.int32, (tm, tm), 1)
    tri = jnp.where(tj < ti, 1.0, 0.0).astype(BF16)
    before = _dot(tri, both) + carry[...]
    rank1 = jnp.sum(oh1 * before, axis=-1, keepdims=True)
    rank2 = jnp.sum(oh2 * before, axis=-1, keepdims=True)
    total = carry[...] + jnp.sum(oh1 + oh2, axis=0, keepdims=True)
    carry[...] = total
    cnt_ref[...] = total

    e1 = i1 - EXPERT_LANE0
    e2 = i2 - EXPERT_LANE0
    out = jnp.zeros(logits.shape, F32)
    for k, val in enumerate((e1, e2, gate1, gate2, rank1, rank2)):
        out = jnp.where(lane_i == k, val, out)
    route_ref[...] = out


def _outproj(x, y_rnn, y_conv, w_out, mod3, norm_g, wr_hi, wr_lo, br):
    bn, n, d = x.shape
    tm = min(TOK_TILE, n)
    nt = n // tm
    t_all = bn * n
    return pl.pallas_call(
        functools.partial(_outproj_kernel, tm=tm),
        grid=(bn, nt),
        in_specs=[
            pl.BlockSpec((1, tm, d), lambda b, i: (b, i, 0)),
            pl.BlockSpec((1, tm, D_RNN), lambda b, i: (b, i, 0)),
            pl.BlockSpec((1, tm, D_CONV), lambda b, i: (b, i, 0)),
            pl.BlockSpec((D_RNN + D_CONV, d), lambda b, i: (0, 0), pipeline_mode=pl.Buffered(1)),
            pl.BlockSpec((1, 6, d), lambda b, i: (b, 0, 0)),
            pl.BlockSpec((1, d), lambda b, i: (0, 0)),
            pl.BlockSpec((d, LANES), lambda b, i: (0, 0)),
            pl.BlockSpec((d, LANES), lambda b, i: (0, 0)),
            pl.BlockSpec((1, LANES), lambda b, i: (0, 0)),
        ],
        out_specs=[
            pl.BlockSpec((1, tm, d), lambda b, i: (b, i, 0)),
            pl.BlockSpec((tm * PACK_TILES, LANES), lambda b, i: (b * nt + i, 0)),
            pl.BlockSpec((tm, LANES), lambda b, i: (b * nt + i, 0)),
            pl.BlockSpec((1, LANES), lambda b, i: (0, 0)),
        ],
        out_shape=[
            jax.ShapeDtypeStruct((bn, n, d), F32),
            jax.ShapeDtypeStruct((t_all * PACK_TILES, LANES), jnp.uint32),
            jax.ShapeDtypeStruct((t_all, LANES), F32),
            jax.ShapeDtypeStruct((1, LANES), F32),
        ],
        scratch_shapes=[pltpu.VMEM((1, LANES), F32), pltpu.VMEM((D_RNN + D_CONV, d), BF16)],
        compiler_params=pltpu.CompilerParams(
            dimension_semantics=("arbitrary", "arbitrary"), vmem_limit_bytes=VMEM_LIMIT),
        name="outproj",
    )(x, y_rnn, y_conv, w_out, mod3, norm_g.reshape(1, d), wr_hi, wr_lo, br)


def _row_tile(ref, row):
    return ref.at[pl.ds(pl.multiple_of(row * ROW_TILES, ROW_TILES), ROW_TILES)]


def _slotmap_kernel(dest_ref, zeros_hbm, asg_ref, sem):
    fill = pltpu.make_async_copy(zeros_hbm, asg_ref, sem)
    fill.start()
    fill.wait()

    def body(c, carry):
        for u in range(DMA_UNROLL):
            a = c * DMA_UNROLL + u
            asg_ref[dest_ref[a]] = a
        return carry
    lax.fori_loop(0, dest_ref.shape[0] // DMA_UNROLL, body, 0)


def _slotmap(dest, n_slots):
    return pl.pallas_call(
        _slotmap_kernel,
        in_specs=[pl.BlockSpec(memory_space=pltpu.SMEM), pl.BlockSpec(memory_space=pl.ANY)],
        out_specs=pl.BlockSpec(memory_space=pltpu.SMEM),
        out_shape=jax.ShapeDtypeStruct((n_slots,), jnp.int32),
        scratch_shapes=[pltpu.SemaphoreType.DMA],
        name="slotmap",
    )(dest, jnp.zeros((n_slots,), jnp.int32))


def _expert_kernel(be_ref, nu_ref, asg_hbm, m_ref, wg_ref, wu_ref, wd_ref, yb_ref,
                   xbuf_a, xbuf_b, idx, isems, wg_s, wu_s, wd_s):
    j = pl.program_id(0)
    n_used = nu_ref[0]
    last = n_used - 1

    def idx_copy(blk, sl):
        return pltpu.make_async_copy(asg_hbm.at[blk], idx.at[sl], isems.at[sl])

    def copy_rows(buf, sl, r0, n):
        for u in range(n):
            tok = lax.shift_right_logical(idx[sl, 0, r0 + u], 1)
            src = pl.multiple_of(tok * PACK_TILES, PACK_TILES)
            buf[pl.ds((r0 + u) * PACK_TILES, PACK_TILES), :] = m_ref[pl.ds(src, PACK_TILES), :]

    def unpack(buf):
        halves = ([], [])
        for s in range(PACK_TILES):
            word = buf[pl.ds(s, MOE_BLK, stride=PACK_TILES), :]
            for k in range(2):
                part = pltpu.unpack_elementwise(word, index=k, packed_dtype=BF16, unpacked_dtype=F32)
                halves[k].append(part.astype(BF16))
        return jnp.concatenate(halves[0] + halves[1], axis=-1)

    n_pieces = (2 * D_EXPERT // MXU_TILE) * (D_MODEL // MXU_TILE) + (
        D_MODEL // MXU_TILE) * (D_EXPERT // MXU_TILE)
    bounds = [(p * MOE_BLK) // n_pieces for p in range(n_pieces + 1)]

    def compute(cur, nxt, nxt_sl):
        pieces = iter(zip(bounds[:-1], bounds[1:]))

        def dot_pieces(a, w_ref, n0):
            acc = None
            for k0 in range(0, a.shape[1], MXU_TILE):
                part = _dot(a[:, k0:k0 + MXU_TILE], w_ref[k0:k0 + MXU_TILE, n0:n0 + MXU_TILE])
                acc = part if acc is None else acc + part
                r0, r1 = next(pieces)
                copy_rows(nxt, nxt_sl, r0, r1 - r0)
            return acc

        xb16 = unpack(cur)
        acts = []
        for n0 in range(0, D_EXPERT, MXU_TILE):
            gate = dot_pieces(xb16, wg_s, n0)
            up = dot_pieces(xb16, wu_s, n0)
            acts.append(((gate * jax.nn.sigmoid(gate)) * up).astype(BF16))
        h = jnp.concatenate(acts, axis=-1)
        for n0 in range(0, D_MODEL, MXU_TILE):
            y = dot_pieces(h, wd_s, n0)
            for s in range(MXU_TILE // LANES):
                yb_ref[pl.ds(n0 // LANES + s, MOE_BLK, stride=ROW_TILES), :] = (
                    y[:, s * LANES:(s + 1) * LANES])

    @pl.when(j >= n_used)
    def _():
        yb_ref[...] = jnp.zeros_like(yb_ref)

    @pl.when(j < n_used)
    def _():
        slot = j % 2
        other = 1 - slot

        @pl.when(j == 0)
        def _():
            idx_copy(0, 0).start()
            idx_copy(0, 0).wait()

            def body(c, carry):
                copy_rows(xbuf_a, 0, c * DMA_UNROLL, DMA_UNROLL)
                return carry
            lax.fori_loop(0, MOE_BLK // DMA_UNROLL, body, 0)
            idx_copy(jnp.minimum(1, last), 1).start()

        e = be_ref[j]
        prev = be_ref[jnp.maximum(j - 1, 0)]

        @pl.when((j == 0) | (e != prev))
        def _():
            wg_s[...] = wg_ref[0].astype(BF16)
            wu_s[...] = wu_ref[0].astype(BF16)
            wd_s[...] = wd_ref[0].astype(BF16)

        idx_copy(0, other).wait()

        @pl.when(slot == 0)
        def _():
            compute(xbuf_a, xbuf_b, 1)

        @pl.when(slot == 1)
        def _():
            compute(xbuf_b, xbuf_a, 0)

        @pl.when(j < last)
        def _():
            idx_copy(jnp.minimum(j + 2, last), slot).start()


def _experts(block_e, n_used, slot_asg, mt, w_gate, w_up, w_down, n_blocks):
    def wsel(j, be, nu):
        return (be[jnp.minimum(j, nu[0] - 1)], 0, 0)

    return pl.pallas_call(
        _expert_kernel,
        grid_spec=pltpu.PrefetchScalarGridSpec(
            num_scalar_prefetch=2,
            grid=(n_blocks,),
            in_specs=[
                pl.BlockSpec(memory_space=pl.ANY),
                pl.BlockSpec(memory_space=pltpu.VMEM),
                pl.BlockSpec((1, D_MODEL, D_EXPERT), wsel),
                pl.BlockSpec((1, D_MODEL, D_EXPERT), wsel),
                pl.BlockSpec((1, D_EXPERT, D_MODEL), wsel),
            ],
            out_specs=pl.BlockSpec((MOE_BLK * ROW_TILES, LANES), lambda j, be, nu: (j, 0)),
            scratch_shapes=[
                pltpu.VMEM((MOE_BLK * PACK_TILES, LANES), jnp.uint32),
                pltpu.VMEM((MOE_BLK * PACK_TILES, LANES), jnp.uint32),
                pltpu.SMEM((2, 1, MOE_BLK), jnp.int32),
                pltpu.SemaphoreType.DMA((2,)),
                pltpu.VMEM((D_MODEL, D_EXPERT), BF16),
                pltpu.VMEM((D_MODEL, D_EXPERT), BF16),
                pltpu.VMEM((D_EXPERT, D_MODEL), BF16),
            ],
        ),
        out_shape=jax.ShapeDtypeStruct((n_blocks * MOE_BLK * ROW_TILES, LANES), F32),
        compiler_params=pltpu.CompilerParams(
            dimension_semantics=("arbitrary",), vmem_limit_bytes=EXPERT_VMEM_LIMIT),
        name="expert",
    )(block_e, n_used, slot_asg.reshape(n_blocks, 1, MOE_BLK), mt, w_gate, w_up, w_down)


def _combine_kernel(dest_ref, yb_hbm, x1_ref, route_ref, mod_ref, g_ref, o_ref, ybuf, sems, *, tc):
    i = pl.program_id(0)
    slot = i % 2

    def row_copy(d, sl, k, r):
        return pltpu.make_async_copy(_row_tile(yb_hbm, d), _row_tile(ybuf.at[sl, k], r), sems.at[sl])

    def gather(step, sl):
        def issue(c, carry):
            for u in range(DMA_UNROLL):
                r = c * DMA_UNROLL + u
                for k in range(TOP_K):
                    row_copy(dest_ref[TOP_K * (step * tc + r) + k], sl, k, r).start(priority=k)
            return carry
        lax.fori_loop(0, tc // DMA_UNROLL, issue, 0)

    @pl.when(i == 0)
    def _():
        gather(0, 0)

    @pl.when(i + 1 < pl.num_programs(0))
    def _():
        gather(i + 1, 1 - slot)

    def drain(c, carry):
        for u in range(DMA_UNROLL * TOP_K):
            row_copy(0, slot, 0, 0).wait()
        return carry
    lax.fori_loop(0, tc // DMA_UNROLL, drain, 0)

    def rows(k):
        return jnp.concatenate(
            [ybuf[slot, k, pl.ds(s, tc, stride=ROW_TILES), :] for s in range(ROW_TILES)], axis=-1)

    route = route_ref[...]
    y = route[:, 2:3] * rows(0) + route[:, 3:4] * rows(1)
    x2 = x1_ref[...] + mod_ref[0, 5:6, :] * y
    ms = jnp.mean(x2 * x2, axis=-1, keepdims=True)
    o_ref[...] = x2 * lax.rsqrt(ms + NORM_EPS) * g_ref[...]


def _combine(dest, yb, x1_2d, route, mod3, final_g, seq):
    t_all, d = x1_2d.shape
    tc = min(COMBINE_TILE, seq)
    per_seq = seq // tc
    return pl.pallas_call(
        functools.partial(_combine_kernel, tc=tc),
        grid_spec=pltpu.PrefetchScalarGridSpec(
            num_scalar_prefetch=1,
            grid=(t_all // tc,),
            in_specs=[
                pl.BlockSpec(memory_space=pl.ANY),
                pl.BlockSpec((tc, d), lambda i, dest: (i, 0)),
                pl.BlockSpec((tc, LANES), lambda i, dest: (i, 0)),
                pl.BlockSpec((1, 6, d), lambda i, dest: (i // per_seq, 0, 0)),
                pl.BlockSpec((1, d), lambda i, dest: (0, 0)),
            ],
            out_specs=pl.BlockSpec((tc, d), lambda i, dest: (i, 0)),
            scratch_shapes=[
                pltpu.VMEM((2, TOP_K, tc * ROW_TILES, LANES), F32),
                pltpu.SemaphoreType.DMA((2,)),
            ],
        ),
        out_shape=jax.ShapeDtypeStruct((t_all, d), F32),
        compiler_params=pltpu.CompilerParams(
            dimension_semantics=("arbitrary",), vmem_limit_bytes=VMEM_LIMIT),
        name="combine",
    )(dest, yb, x1_2d, route, mod3, final_g.reshape(1, d))


def kernel(x, c, ctx, c_ctx, ada_w, ada_b, norm1_g, norm2_g, w_in, rnn_conv_w, rnn_conv_b, rg_wa, rg_ba,
           rg_wx, rg_bx, rg_lambda, sc_conv_w, w_out, router_group_w, router_group_b, router_exp_w,
           router_exp_b, exp_w_gate, exp_w_up, exp_w_down, final_norm_g):
    bn, seq, d = x.shape
    assert d == D_MODEL and bn < MOD_ROWS and ada_w.shape[0] == 1
    t_all = bn * seq

    cc = jnp.concatenate([c, c_ctx[None], jnp.zeros((MOD_ROWS - bn - 1, d), F32)], axis=0)
    mod3 = _modulation(cc, ada_w[0], ada_b[0]).reshape(MOD_ROWS, 6, d)

    xr, gr, u, bg = _inproj(x, mod3, None, norm1_g[0], w_in[0], latent=True)
    (xr_c,) = _inproj(ctx, mod3, bn, norm1_g[0], w_in[0], latent=False)

    wg, bgate = _gate_weights(rg_wa[0], rg_ba[0], rg_wx[0], rg_bx[0])
    y_rnn = _rnn(xr, xr_c, gr, rnn_conv_w[0], rnn_conv_b[0], wg, bgate, rg_lambda[0])
    y_conv = _gconv(u, bg, sc_conv_w[0])

    wr = jnp.zeros((d, LANES), F32)
    wr = wr.at[:, :N_GROUPS].set(router_group_w[0]).at[:, EXPERT_LANE0:EXPERT_LANE0 + N_EXPERTS].set(router_exp_w[0])
    br = jnp.zeros((1, LANES), F32)
    br = br.at[0, :N_GROUPS].set(router_group_b[0]).at[0, EXPERT_LANE0:EXPERT_LANE0 + N_EXPERTS].set(router_exp_b[0])
    wr_hi, wr_lo = _split_bf16(wr)
    x1, mt, route, cnt = _outproj(x, y_rnn, y_conv, w_out[0], mod3, norm2_g[0], wr_hi, wr_lo, br)

    n_assign = t_all * TOP_K
    n_blocks = (n_assign + N_EXPERTS * (MOE_BLK - 1) + MOE_BLK - 1) // MOE_BLK
    counts = cnt[0, EXPERT_LANE0:EXPERT_LANE0 + N_EXPERTS].astype(jnp.int32)
    pcounts = (counts + MOE_BLK - 1) // MOE_BLK * MOE_BLK
    pends = jnp.cumsum(pcounts)
    pstarts = pends - pcounts
    experts = route[:, 0:TOP_K].astype(jnp.int32)
    ranks = route[:, 4:4 + TOP_K].astype(jnp.int32)
    onehot = experts[:, :, None] == jnp.arange(N_EXPERTS, dtype=jnp.int32)
    dest = (ranks + jnp.sum(jnp.where(onehot, pstarts, 0), axis=-1)).reshape(n_assign)
    n_used = (pends[-1] // MOE_BLK).astype(jnp.int32)
    blk_start = jnp.arange(n_blocks, dtype=jnp.int32) * MOE_BLK
    block_e = jnp.minimum(jnp.sum(blk_start[:, None] >= pends[None, :], axis=1), N_EXPERTS - 1)
    last_e = jnp.max(jnp.where(counts > 0, jnp.arange(N_EXPERTS, dtype=jnp.int32), 0))
    block_e = jnp.where(blk_start < pends[-1], block_e, last_e).astype(jnp.int32)

    n_slots = n_blocks * MOE_BLK
    slot_asg = _slotmap(dest, n_slots)
    yb = _experts(block_e, n_used.reshape(1), slot_asg, mt, exp_w_gate[0], exp_w_up[0], exp_w_down[0],
                  n_blocks)
    out = _combine(dest, yb, x1.reshape(t_all, d), route, mod3, final_norm_g, seq)
    return out.reshape(bn, seq, d)
```

```python
import functools

import jax
import jax.numpy as jnp
from jax import lax
from jax.experimental import pallas as pl
from jax.experimental.pallas import tpu as pltpu

F32 = jnp.float32
BF16 = jnp.bfloat16

D_MODEL = 1024
D_RNN = 512
D_CONV = 512
D_CONV_H = D_CONV // 2
RNN_HEADS = 8
RNN_HEAD_DIM = D_RNN // RNN_HEADS
GRID_W = 64
RG_C = 8.0
N_GROUPS = 4
EXPERTS_PER_GROUP = 8
N_EXPERTS = N_GROUPS * EXPERTS_PER_GROUP
TOP_K = 2
D_EXPERT = 512
NORM_EPS = 1e-6
F32_TINY = 1.1754944e-38

LANES = 128
SUBLANES = 8
ROW_TILES = D_MODEL // LANES
PACK_TILES = ROW_TILES // 2
N_LANE_GROUPS = D_RNN // LANES
EXPERT_LANE0 = N_GROUPS

MOD_ROWS = 16
MOD_TN = 768
TOK_TILE = 1024
OUTPROJ_CHAINS = 1
COEFF_ROWS = 512
MOE_BLK = 256
MXU_TILE = 256
COMBINE_TILE = 256
DMA_UNROLL = 16
VMEM_LIMIT = 48 * 1024 * 1024
EXPERT_VMEM_LIMIT = 58 * 1024 * 1024


def _dot(a, b):
    return jnp.dot(a, b, preferred_element_type=F32)


def _split_bf16(x):
    hi = x.astype(BF16)
    lo = (x - hi.astype(F32)).astype(BF16)
    return hi, lo


def _mod_kernel(cc_ref, w_ref, b_ref, o_ref):
    s = cc_ref[...]
    s = s * jax.nn.sigmoid(s)
    s_hi, s_lo = _split_bf16(s)
    w_hi, w_lo = _split_bf16(w_ref[...])
    o_ref[...] = _dot(s_hi, w_hi) + _dot(s_lo, w_hi) + _dot(s_hi, w_lo) + b_ref[...]


def _modulation(cc, ada_w, ada_b):
    n = ada_w.shape[1]
    return pl.pallas_call(
        _mod_kernel,
        grid=(n // MOD_TN,),
        in_specs=[
            pl.BlockSpec((MOD_ROWS, D_MODEL), lambda j: (0, 0)),
            pl.BlockSpec((D_MODEL, MOD_TN), lambda j: (0, j)),
            pl.BlockSpec((1, MOD_TN), lambda j: (0, j)),
        ],
        out_specs=pl.BlockSpec((MOD_ROWS, MOD_TN), lambda j: (0, j)),
        out_shape=jax.ShapeDtypeStruct((MOD_ROWS, n), F32),
        compiler_params=pltpu.CompilerParams(vmem_limit_bytes=VMEM_LIMIT),
        name="mod",
    )(cc, ada_w, ada_b.reshape(1, n))


def _norm_mod(x, g, scale, shift):
    ms = jnp.mean(x * x, axis=-1, keepdims=True)
    y = x * lax.rsqrt(ms + NORM_EPS) * g
    return y * (1.0 + scale) + shift


def _inproj_kernel(x_ref, mod_ref, g_ref, w32_ref, *refs, latent):
    out_refs, w_ref = refs[:-1], refs[-1]

    @pl.when((pl.program_id(0) == 0) & (pl.program_id(1) == 0))
    def _():
        w_ref[...] = w32_ref[...].astype(BF16)

    h = _norm_mod(x_ref[0], g_ref[...], mod_ref[0, 1:2, :], mod_ref[0, 0:1, :])
    hb = h.astype(BF16)
    xr = _dot(hb, w_ref[:, 0:D_RNN])
    out_refs[0][0] = xr
    if latent:
        o = D_RNN
        out_refs[1][0] = _dot(hb, w_ref[:, o:o + D_RNN])
        o += D_RNN
        v = _dot(hb, w_ref[:, o:o + D_CONV])
        out_refs[3][0] = _dot(hb, w_ref[:, o + D_CONV:o + 2 * D_CONV])
        cg = _dot(hb, w_ref[:, o + 2 * D_CONV:o + 3 * D_CONV])
        out_refs[2][0] = cg * v


def _inproj(x, mod3, mod_row, norm_g, w_in, latent):
    bn, n, d = x.shape
    tm = min(TOK_TILE, n)
    n_out = 4 if latent else 1
    width = w_in.shape[1] if latent else D_RNN
    mod_map = (lambda b, i: (b, 0, 0)) if mod_row is None else (lambda b, i: (mod_row, 0, 0))
    return pl.pallas_call(
        functools.partial(_inproj_kernel, latent=latent),
        grid=(bn, n // tm),
        in_specs=[
            pl.BlockSpec((1, tm, d), lambda b, i: (b, i, 0)),
            pl.BlockSpec((1, 6, d), mod_map),
            pl.BlockSpec((1, d), lambda b, i: (0, 0)),
            pl.BlockSpec((d, width), lambda b, i: (0, 0), pipeline_mode=pl.Buffered(1)),
        ],
        out_specs=[pl.BlockSpec((1, tm, D_RNN), lambda b, i: (b, i, 0))] * n_out,
        out_shape=[jax.ShapeDtypeStruct((bn, n, D_RNN), F32)] * n_out,
        scratch_shapes=[pltpu.VMEM((d, width), BF16)],
        compiler_params=pltpu.CompilerParams(
            dimension_semantics=("arbitrary", "arbitrary"), vmem_limit_bytes=VMEM_LIMIT),
        name="inproj_lat" if latent else "inproj_ctx",
    )(x, mod3, norm_g.reshape(1, d), w_in)


def _shift_rows(x, k):
    n = x.shape[0]
    row = lax.broadcasted_iota(jnp.int32, x.shape, 0)
    rolled = pltpu.roll(x, k % n, axis=0)
    valid = (row >= k) if k > 0 else (row < n + k)
    return jnp.where(valid, rolled, 0.0)


def _scan_pitch(chunk):
    pitch = chunk + SUBLANES
    return pitch if (pitch // SUBLANES) % 2 else pitch + SUBLANES


def _rnn_kernel(xr_ref, xrc_ref, gr_ref, cw_ref, cb_ref, wg_ref, bg_ref, lam_ref, y_ref,
                xc_s, ap_f, bp_f, ap_b, bp_b, hl_f, al_f, hl_b, al_b, hp_f, hp_b, *, n_lat, n_ctx):
    nl = -lam_ref[...]
    sp = jnp.maximum(nl, 0.0) + jnp.log1p(jnp.exp(-jnp.abs(nl)))
    c1 = (-0.5 * RG_C) * sp
    cw = cw_ref[...]
    bias = cb_ref[...]
    wg = wg_ref[0]
    bg = bg_ref[0]
    dirs = ((ap_f, bp_f, hl_f, al_f, hp_f), (ap_b, bp_b, hl_b, al_b, hp_b))

    def conv_into(x, n):
        xc_s[pl.ds(0, n), :] = (cw[0:1] * _shift_rows(x, 2) + cw[1:2] * _shift_rows(x, 1)
                                + cw[2:3] * x + cw[3:4] * _shift_rows(x, -1)) + bias

    def coefficients(n):
        chunk = n // SUBLANES
        pitch = _scan_pitch(chunk)
        rows = max(chunk, min(n, COEFF_ROWS))
        per = rows // chunk

        def body(i, carry):
            xc = xc_s[pl.ds(pl.multiple_of(i * rows, SUBLANES), rows), :]
            gates = _dot(xc.astype(BF16), wg) + bg
            half_xc = 0.5 * xc
            for d in range(2):
                tr = jnp.tanh(0.5 * gates[:, (2 * d) * LANES:(2 * d + 1) * LANES])
                ti = jnp.tanh(0.5 * gates[:, (2 * d + 1) * LANES:(2 * d + 2) * LANES])
                log_a = c1[d:d + 1] + c1[d:d + 1] * tr
                a = jnp.exp(log_a)
                y = -jnp.tanh(log_a) * (a * a + 1.0)
                b = (y * lax.rsqrt(jnp.maximum(y, F32_TINY))) * (half_xc + half_xc * ti)
                for k in range(per):
                    dst = pl.multiple_of((i * per + k) * pitch, SUBLANES)
                    dirs[d][0][pl.ds(dst, chunk), :] = a[k * chunk:(k + 1) * chunk]
                    dirs[d][1][pl.ds(dst, chunk), :] = b[k * chunk:(k + 1) * chunk]
            return carry

        if n == rows:
            body(0, 0)
        else:
            lax.fori_loop(0, n // rows, body, 0)

    def scan(n, h0_f, h0_b, keep):
        chunk = n // SUBLANES
        pitch = _scan_pitch(chunk)

        def steps(jo, carry):
            h_f, a_f, h_b, a_b = carry
            for u in range(SUBLANES):
                j = jo * SUBLANES + u
                av = ap_f[pl.ds(j, SUBLANES, stride=pitch), :]
                h_f = av * h_f + bp_f[pl.ds(j, SUBLANES, stride=pitch), :]
                a_f = av * a_f
                jb = chunk - 1 - j
                av = ap_b[pl.ds(jb, SUBLANES, stride=pitch), :]
                h_b = av * h_b + bp_b[pl.ds(jb, SUBLANES, stride=pitch), :]
                a_b = av * a_b
                if keep:
                    o = pl.multiple_of(j * SUBLANES, SUBLANES)
                    hl_f[pl.ds(o, SUBLANES), :] = h_f
                    al_f[pl.ds(o, SUBLANES), :] = a_f
                    hl_b[pl.ds(o, SUBLANES), :] = h_b
                    al_b[pl.ds(o, SUBLANES), :] = a_b
            return h_f, a_f, h_b, a_b

        zeros = jnp.zeros((SUBLANES, LANES), F32)
        ones = jnp.ones((SUBLANES, LANES), F32)
        h_f, a_f, h_b, a_b = lax.fori_loop(0, chunk // SUBLANES, steps, (zeros, ones, zeros, ones))

        in_f = [h0_f]
        for c in range(SUBLANES):
            in_f.append(a_f[c:c + 1] * in_f[c] + h_f[c:c + 1])
        in_b = [h0_b]
        for c in range(SUBLANES - 1, -1, -1):
            in_b.append(a_b[c:c + 1] * in_b[-1] + h_b[c:c + 1])
        if keep:
            hin_f = jnp.concatenate(in_f[:SUBLANES], axis=0)
            hin_b = jnp.concatenate(in_b[SUBLANES - 1::-1], axis=0)

            def fix(jo, carry):
                for u in range(SUBLANES):
                    j = jo * SUBLANES + u
                    o = pl.multiple_of(j * SUBLANES, SUBLANES)
                    hp_f[pl.ds(j, SUBLANES, stride=pitch), :] = (
                        hl_f[pl.ds(o, SUBLANES), :] + al_f[pl.ds(o, SUBLANES), :] * hin_f)
                    hp_b[pl.ds(chunk - 1 - j, SUBLANES, stride=pitch), :] = (
                        hl_b[pl.ds(o, SUBLANES), :] + al_b[pl.ds(o, SUBLANES), :] * hin_b)
                return carry
            lax.fori_loop(0, chunk // SUBLANES, fix, 0)
        return in_f[SUBLANES], in_b[SUBLANES]

    zero = jnp.zeros((1, LANES), F32)
    conv_into(xrc_ref[0], n_ctx)
    coefficients(n_ctx)
    h0_f, h0_b = scan(n_ctx, zero, zero, keep=False)

    conv_into(xr_ref[0], n_lat)
    coefficients(n_lat)
    scan(n_lat, h0_f, h0_b, keep=True)

    chunk = n_lat // SUBLANES
    pitch = _scan_pitch(chunk)

    def emit(c, carry):
        src = pl.multiple_of(c * chunk, 2 * SUBLANES)
        dst = pl.multiple_of(c * pitch, SUBLANES)
        hsum = hp_f[pl.ds(dst, chunk), :] + hp_b[pl.ds(dst, chunk), :]
        y = jax.nn.gelu(gr_ref[0, pl.ds(src, chunk), :], approximate=True) * hsum
        y_ref[0, pl.ds(src, chunk), :] = y.astype(y_ref.dtype)
        return carry
    lax.fori_loop(0, SUBLANES, emit, 0)


def _rnn(xr, xr_c, gr, conv_w, conv_b, wg, bgate, lam):
    bn, n, _ = xr.shape
    n_ctx = xr_c.shape[1]
    assert n % (SUBLANES * SUBLANES) == 0 and n_ctx % (SUBLANES * SUBLANES) == 0 and n_ctx <= n
    pitched = SUBLANES * _scan_pitch(n // SUBLANES)
    seq_spec = pl.BlockSpec((1, n, LANES), lambda b, p: (b, 0, p))
    return pl.pallas_call(
        functools.partial(_rnn_kernel, n_lat=n, n_ctx=n_ctx),
        grid=(bn, N_LANE_GROUPS),
        in_specs=[
            seq_spec,
            pl.BlockSpec((1, n_ctx, LANES), lambda b, p: (b, 0, p)),
            seq_spec,
            pl.BlockSpec((4, LANES), lambda b, p: (0, p)),
            pl.BlockSpec((1, LANES), lambda b, p: (0, p)),
            pl.BlockSpec((1, LANES, 4 * LANES), lambda b, p: (p, 0, 0)),
            pl.BlockSpec((1, 1, 4 * LANES), lambda b, p: (p, 0, 0)),
            pl.BlockSpec((2, LANES), lambda b, p: (0, p)),
        ],
        out_specs=seq_spec,
        out_shape=jax.ShapeDtypeStruct((bn, n, D_RNN), BF16),
        scratch_shapes=[pltpu.VMEM((n, LANES), F32)]
        + [pltpu.VMEM((pitched, LANES), F32)] * 4
        + [pltpu.VMEM((n, LANES), F32)] * 4
        + [pltpu.VMEM((pitched, LANES), F32)] * 2,
        compiler_params=pltpu.CompilerParams(vmem_limit_bytes=VMEM_LIMIT),
        name="rnn",
    )(xr, xr_c, gr, conv_w, conv_b.reshape(1, D_RNN), wg, bgate, lam)


def _gate_weights(rg_wa, rg_ba, rg_wx, rg_bx):
    eye = jnp.eye(2, dtype=F32)
    blocks, biases = [], []
    for d in range(2):
        for w, bvec in ((rg_wa[d], rg_ba[d]), (rg_wx[d], rg_bx[d])):
            w4 = w.reshape(N_LANE_GROUPS, 2, RNN_HEAD_DIM, RNN_HEAD_DIM)
            bd = jnp.einsum("paij,ac->paicj", w4, eye).reshape(N_LANE_GROUPS, LANES, LANES)
            blocks.append(bd)
            biases.append(bvec.reshape(N_LANE_GROUPS, 1, LANES))
    return jnp.concatenate(blocks, axis=-1).astype(BF16), jnp.concatenate(biases, axis=-1)


def _gconv_kernel(u_ref, bg_ref, w_ref, y_ref, *, n):
    p = pl.program_id(1)
    u = u_ref[0]
    w = w_ref[...]

    @pl.when(p < D_CONV_H // LANES)
    def _():
        col = lax.broadcasted_iota(jnp.int32, u.shape, 0) % GRID_W
        left = jnp.where(col > 0, _shift_rows(u, 1), 0.0)
        right = jnp.where(col < GRID_W - 1, _shift_rows(u, -1), 0.0)
        y_ref[0] = (bg_ref[0] * (w[0:1] * left + w[1:2] * u + w[2:3] * right)).astype(y_ref.dtype)

    @pl.when(p >= D_CONV_H // LANES)
    def _():
        y_ref[0] = (bg_ref[0] * (w[0:1] * _shift_rows(u, GRID_W) + w[1:2] * u
                                 + w[2:3] * _shift_rows(u, -GRID_W))).astype(y_ref.dtype)


def _gconv(u, bg, w):
    bn, n, _ = u.shape
    seq_spec = pl.BlockSpec((1, n, LANES), lambda b, p: (b, 0, p))
    return pl.pallas_call(
        functools.partial(_gconv_kernel, n=n),
        grid=(bn, D_CONV // LANES),
        in_specs=[seq_spec, seq_spec, pl.BlockSpec((3, LANES), lambda b, p: (0, p))],
        out_specs=seq_spec,
        out_shape=jax.ShapeDtypeStruct((bn, n, D_CONV), BF16),
        compiler_params=pltpu.CompilerParams(vmem_limit_bytes=VMEM_LIMIT),
        name="gconv",
    )(u, bg, w)


def _lane_max(x, mask):
    return jnp.max(jnp.where(mask, x, -jnp.inf), axis=-1, keepdims=True)


def _first_lane(cond, lane):
    return jnp.min(jnp.where(cond, lane, float(LANES)), axis=-1, keepdims=True)


def _outproj_kernel(x_ref, yr_ref, yc_ref, w32_ref, mod_ref, g_ref, wr_hi_ref, wr_lo_ref, br_ref,
                    x1_ref, mt_ref, route_ref, cnt_ref, carry, w_ref, *, tm):
    first = (pl.program_id(0) == 0) & (pl.program_id(1) == 0)

    @pl.when(first)
    def _():
        carry[...] = jnp.zeros_like(carry)
        w_ref[...] = w32_ref[...].astype(BF16)

    rows = tm // OUTPROJ_CHAINS
    total = carry[...]
    for c in range(OUTPROJ_CHAINS):
        total = _outproj_rows(c * rows, rows, total, x_ref, yr_ref, yc_ref, w_ref, mod_ref, g_ref,
                              wr_hi_ref, wr_lo_ref, br_ref, x1_ref, mt_ref, route_ref)
    carry[...] = total
    cnt_ref[...] = total


def _outproj_rows(r0, rows, counts, x_ref, yr_ref, yc_ref, w_ref, mod_ref, g_ref, wr_hi_ref, wr_lo_ref,
                  br_ref, x1_ref, mt_ref, route_ref):
    sl = pl.ds(r0, rows)
    mix = _dot(yr_ref[0, sl, :], w_ref[0:D_RNN, :]) + _dot(yc_ref[0, sl, :], w_ref[D_RNN:, :])
    x1 = x_ref[0, sl, :] + mod_ref[0, 2:3, :] * mix
    x1_ref[0, sl, :] = x1
    m = _norm_mod(x1, g_ref[...], mod_ref[0, 4:5, :], mod_ref[0, 3:4, :])
    half = D_MODEL // 2
    packed = pltpu.pack_elementwise([m[:, :half], m[:, half:]], packed_dtype=BF16)
    for s in range(PACK_TILES):
        mt_ref[pl.ds(r0 * PACK_TILES + s, rows, stride=PACK_TILES), :] = packed[:, s * LANES:(s + 1) * LANES]

    m_hi, m_lo = _split_bf16(m)
    logits = (_dot(m_hi, wr_hi_ref[...]) + _dot(m_lo, wr_hi_ref[...]) + _dot(m_hi, wr_lo_ref[...])
              + br_ref[...])
    lane_i = lax.broadcasted_iota(jnp.int32, logits.shape, 1)
    lane = lane_i.astype(F32)
    is_grp = lane_i < N_GROUPS
    g_max = _lane_max(logits, is_grp)
    grp = _first_lane(is_grp & (logits == g_max), lane)
    p_g = 1.0 / jnp.sum(jnp.where(is_grp, jnp.exp(logits - g_max), 0.0), axis=-1, keepdims=True)
    lo_lane = EXPERT_LANE0 + grp * EXPERTS_PER_GROUP
    in_grp = (lane >= lo_lane) & (lane < lo_lane + EXPERTS_PER_GROUP)
    l1 = _lane_max(logits, in_grp)
    i1 = _first_lane(in_grp & (logits == l1), lane)
    rest = in_grp & (lane != i1)
    l2 = _lane_max(logits, rest)
    i2 = _first_lane(rest & (logits == l2), lane)
    r21 = jnp.exp(l2 - l1)
    gate1 = p_g / (1.0 + r21)
    gate2 = gate1 * r21

    oh1 = jnp.where(lane == i1, 1.0, 0.0)
    oh2 = jnp.where(lane == i2, 1.0, 0.0)
    both = (oh1 + oh2).astype(BF16)
    ti = lax.broadcasted_iota(jnp.int32, (rows, rows), 0)
    tj = lax.broadcasted_iota(jnp.int32, (rows, rows), 1)
    tri = jnp.where(tj < ti, 1.0, 0.0).astype(BF16)
    before = _dot(tri, both) + counts
    rank1 = jnp.sum(oh1 * before, axis=-1, keepdims=True)
    rank2 = jnp.sum(oh2 * before, axis=-1, keepdims=True)

    e1 = i1 - EXPERT_LANE0
    e2 = i2 - EXPERT_LANE0
    out = jnp.zeros(logits.shape, F32)
    for k, val in enumerate((e1, e2, gate1, gate2, rank1, rank2)):
        out = jnp.where(lane_i == k, val, out)
    route_ref[sl, :] = out
    return counts + jnp.sum(oh1 + oh2, axis=0, keepdims=True)


def _outproj(x, y_rnn, y_conv, w_out, mod3, norm_g, wr_hi, wr_lo, br):
    bn, n, d = x.shape
    tm = min(TOK_TILE, n)
    nt = n // tm
    t_all = bn * n
    return pl.pallas_call(
        functools.partial(_outproj_kernel, tm=tm),
        grid=(bn, nt),
        in_specs=[
            pl.BlockSpec((1, tm, d), lambda b, i: (b, i, 0)),
            pl.BlockSpec((1, tm, D_RNN), lambda b, i: (b, i, 0)),
            pl.BlockSpec((1, tm, D_CONV), lambda b, i: (b, i, 0)),
            pl.BlockSpec((D_RNN + D_CONV, d), lambda b, i: (0, 0), pipeline_mode=pl.Buffered(1)),
            pl.BlockSpec((1, 6, d), lambda b, i: (b, 0, 0)),
            pl.BlockSpec((1, d), lambda b, i: (0, 0)),
            pl.BlockSpec((d, LANES), lambda b, i: (0, 0)),
            pl.BlockSpec((d, LANES), lambda b, i: (0, 0)),
            pl.BlockSpec((1, LANES), lambda b, i: (0, 0)),
        ],
        out_specs=[
            pl.BlockSpec((1, tm, d), lambda b, i: (b, i, 0)),
            pl.BlockSpec((tm * PACK_TILES, LANES), lambda b, i: (b * nt + i, 0)),
            pl.BlockSpec((tm, LANES), lambda b, i: (b * nt + i, 0)),
            pl.BlockSpec((1, LANES), lambda b, i: (0, 0)),
        ],
        out_shape=[
            jax.ShapeDtypeStruct((bn, n, d), F32),
            jax.ShapeDtypeStruct((t_all * PACK_TILES, LANES), jnp.uint32),
            jax.ShapeDtypeStruct((t_all, LANES), F32),
            jax.ShapeDtypeStruct((1, LANES), F32),
        ],
        scratch_shapes=[pltpu.VMEM((1, LANES), F32), pltpu.VMEM((D_RNN + D_CONV, d), BF16)],
        compiler_params=pltpu.CompilerParams(
            dimension_semantics=("arbitrary", "arbitrary"), vmem_limit_bytes=VMEM_LIMIT),
        name="outproj",
    )(x, y_rnn, y_conv, w_out, mod3, norm_g.reshape(1, d), wr_hi, wr_lo, br)


def _row_tile(ref, row):
    return ref.at[pl.ds(pl.multiple_of(row * ROW_TILES, ROW_TILES), ROW_TILES)]


def _slotmap_kernel(dest_ref, zeros_hbm, asg_ref, sem):
    fill = pltpu.make_async_copy(zeros_hbm, asg_ref, sem)
    fill.start()
    fill.wait()

    def body(c, carry):
        for u in range(DMA_UNROLL):
            a = c * DMA_UNROLL + u
            asg_ref[dest_ref[a]] = a
        return carry
    lax.fori_loop(0, dest_ref.shape[0] // DMA_UNROLL, body, 0)


def _slotmap(dest, n_slots):
    return pl.pallas_call(
        _slotmap_kernel,
        in_specs=[pl.BlockSpec(memory_space=pltpu.SMEM), pl.BlockSpec(memory_space=pl.ANY)],
        out_specs=pl.BlockSpec(memory_space=pltpu.SMEM),
        out_shape=jax.ShapeDtypeStruct((n_slots,), jnp.int32),
        scratch_shapes=[pltpu.SemaphoreType.DMA],
        name="slotmap",
    )(dest, jnp.zeros((n_slots,), jnp.int32))


def _expert_kernel(be_ref, ws_ref, ne_ref, nu_ref, asg_hbm, m_ref, wg_hbm, wu_hbm, wd_hbm, yb_ref,
                   xbuf_a, xbuf_b, idx, isems, wbuf_g, wbuf_u, wbuf_d, wsems, wg_s, wu_s, wd_s):
    j = pl.program_id(0)
    n_used = nu_ref[0]
    last = n_used - 1

    def idx_copy(blk, sl):
        return pltpu.make_async_copy(asg_hbm.at[blk], idx.at[sl], isems.at[sl])

    def copy_rows(buf, sl, r0, n):
        for u in range(n):
            tok = lax.shift_right_logical(idx[sl, 0, r0 + u], 1)
            src = pl.multiple_of(tok * PACK_TILES, PACK_TILES)
            buf[pl.ds((r0 + u) * PACK_TILES, PACK_TILES), :] = m_ref[pl.ds(src, PACK_TILES), :]

    def unpack(buf):
        halves = ([], [])
        for s in range(PACK_TILES):
            word = buf[pl.ds(s, MOE_BLK, stride=PACK_TILES), :]
            for k in range(2):
                part = pltpu.unpack_elementwise(word, index=k, packed_dtype=BF16, unpacked_dtype=F32)
                halves[k].append(part.astype(BF16))
        return jnp.concatenate(halves[0] + halves[1], axis=-1)

    n_pieces = (2 * D_EXPERT // MXU_TILE) * (D_MODEL // MXU_TILE) + (
        D_MODEL // MXU_TILE) * (D_EXPERT // MXU_TILE)
    bounds = [(p * MOE_BLK) // n_pieces for p in range(n_pieces + 1)]

    def compute(cur, nxt, nxt_sl):
        pieces = iter(zip(bounds[:-1], bounds[1:]))

        def dot_pieces(a, w_ref, n0):
            acc = None
            for k0 in range(0, a.shape[1], MXU_TILE):
                part = _dot(a[:, k0:k0 + MXU_TILE], w_ref[k0:k0 + MXU_TILE, n0:n0 + MXU_TILE])
                acc = part if acc is None else acc + part
                r0, r1 = next(pieces)
                copy_rows(nxt, nxt_sl, r0, r1 - r0)
            return acc

        xb16 = unpack(cur)
        acts = []
        for n0 in range(0, D_EXPERT, MXU_TILE):
            gate = dot_pieces(xb16, wg_s, n0)
            up = dot_pieces(xb16, wu_s, n0)
            acts.append(((gate * jax.nn.sigmoid(gate)) * up).astype(BF16))
        h = jnp.concatenate(acts, axis=-1)
        for n0 in range(0, D_MODEL, MXU_TILE):
            y = dot_pieces(h, wd_s, n0)
            for s in range(MXU_TILE // LANES):
                yb_ref[pl.ds(n0 // LANES + s, MOE_BLK, stride=ROW_TILES), :] = (
                    y[:, s * LANES:(s + 1) * LANES])

    @pl.when(j >= n_used)
    def _():
        yb_ref[...] = jnp.zeros_like(yb_ref)

    @pl.when(j < n_used)
    def _():
        slot = j % 2
        other = 1 - slot

        @pl.when(j == 0)
        def _():
            idx_copy(0, 0).start()
            idx_copy(0, 0).wait()

            def body(c, carry):
                copy_rows(xbuf_a, 0, c * DMA_UNROLL, DMA_UNROLL)
                return carry
            lax.fori_loop(0, MOE_BLK // DMA_UNROLL, body, 0)
            idx_copy(jnp.minimum(1, last), 1).start()

        e = be_ref[j]
        wslot = ws_ref[j]

        def weight_copies(expert, sl):
            return [pltpu.make_async_copy(src.at[expert], dst.at[sl], wsems.at[sl])
                    for src, dst in ((wg_hbm, wbuf_g), (wu_hbm, wbuf_u), (wd_hbm, wbuf_d))]

        @pl.when(j == 0)
        def _():
            for cp in weight_copies(e, wslot):
                cp.start()

        @pl.when((j == 0) | (e != be_ref[jnp.maximum(j - 1, 0)]))
        def _():
            for cp in weight_copies(e, wslot):
                cp.wait()
            wg_s[...] = wbuf_g[wslot].astype(BF16)
            wu_s[...] = wbuf_u[wslot].astype(BF16)
            wd_s[...] = wbuf_d[wslot].astype(BF16)

            @pl.when(ne_ref[j] >= 0)
            def _():
                for cp in weight_copies(ne_ref[j], 1 - wslot):
                    cp.start()

        idx_copy(0, other).wait()

        @pl.when(slot == 0)
        def _():
            compute(xbuf_a, xbuf_b, 1)

        @pl.when(slot == 1)
        def _():
            compute(xbuf_b, xbuf_a, 0)

        @pl.when(j < last)
        def _():
            idx_copy(jnp.minimum(j + 2, last), slot).start()


def _experts(block_e, weight_slot, next_e, n_used, slot_asg, mt, w_gate, w_up, w_down, n_blocks):
    return pl.pallas_call(
        _expert_kernel,
        grid_spec=pltpu.PrefetchScalarGridSpec(
            num_scalar_prefetch=4,
            grid=(n_blocks,),
            in_specs=[
                pl.BlockSpec(memory_space=pl.ANY),
                pl.BlockSpec(memory_space=pltpu.VMEM),
                pl.BlockSpec(memory_space=pl.ANY),
                pl.BlockSpec(memory_space=pl.ANY),
                pl.BlockSpec(memory_space=pl.ANY),
            ],
            out_specs=pl.BlockSpec((MOE_BLK * ROW_TILES, LANES), lambda j, *_: (j, 0)),
            scratch_shapes=[
                pltpu.VMEM((MOE_BLK * PACK_TILES, LANES), jnp.uint32),
                pltpu.VMEM((MOE_BLK * PACK_TILES, LANES), jnp.uint32),
                pltpu.SMEM((2, 1, MOE_BLK), jnp.int32),
                pltpu.SemaphoreType.DMA((2,)),
                pltpu.VMEM((2, D_MODEL, D_EXPERT), F32),
                pltpu.VMEM((2, D_MODEL, D_EXPERT), F32),
                pltpu.VMEM((2, D_EXPERT, D_MODEL), F32),
                pltpu.SemaphoreType.DMA((2,)),
                pltpu.VMEM((D_MODEL, D_EXPERT), BF16),
                pltpu.VMEM((D_MODEL, D_EXPERT), BF16),
                pltpu.VMEM((D_EXPERT, D_MODEL), BF16),
            ],
        ),
        out_shape=jax.ShapeDtypeStruct((n_blocks * MOE_BLK * ROW_TILES, LANES), F32),
        compiler_params=pltpu.CompilerParams(
            dimension_semantics=("arbitrary",), vmem_limit_bytes=EXPERT_VMEM_LIMIT),
        name="expert",
    )(block_e, weight_slot, next_e, n_used, slot_asg.reshape(n_blocks, 1, MOE_BLK), mt, w_gate, w_up, w_down)


def _combine_kernel(dest_ref, yb_hbm, x1_ref, route_ref, mod_ref, g_ref, o_ref, ybuf, sems, *, tc):
    i = pl.program_id(0)
    slot = i % 2

    def row_copy(d, sl, k, r):
        return pltpu.make_async_copy(_row_tile(yb_hbm, d), _row_tile(ybuf.at[sl, k], r), sems.at[sl])

    def gather(step, sl):
        def issue(c, carry):
            for u in range(DMA_UNROLL):
                r = c * DMA_UNROLL + u
                for k in range(TOP_K):
                    row_copy(dest_ref[TOP_K * (step * tc + r) + k], sl, k, r).start(priority=k)
            return carry
        lax.fori_loop(0, tc // DMA_UNROLL, issue, 0)

    @pl.when(i == 0)
    def _():
        gather(0, 0)

    @pl.when(i + 1 < pl.num_programs(0))
    def _():
        gather(i + 1, 1 - slot)

    def drain(c, carry):
        for u in range(DMA_UNROLL * TOP_K):
            row_copy(0, slot, 0, 0).wait()
        return carry
    lax.fori_loop(0, tc // DMA_UNROLL, drain, 0)

    def rows(k):
        return jnp.concatenate(
            [ybuf[slot, k, pl.ds(s, tc, stride=ROW_TILES), :] for s in range(ROW_TILES)], axis=-1)

    route = route_ref[...]
    y = route[:, 2:3] * rows(0) + route[:, 3:4] * rows(1)
    x2 = x1_ref[...] + mod_ref[0, 5:6, :] * y
    ms = jnp.mean(x2 * x2, axis=-1, keepdims=True)
    o_ref[...] = x2 * lax.rsqrt(ms + NORM_EPS) * g_ref[...]


def _combine(dest, yb, x1_2d, route, mod3, final_g, seq):
    t_all, d = x1_2d.shape
    tc = min(COMBINE_TILE, seq)
    per_seq = seq // tc
    return pl.pallas_call(
        functools.partial(_combine_kernel, tc=tc),
        grid_spec=pltpu.PrefetchScalarGridSpec(
            num_scalar_prefetch=1,
            grid=(t_all // tc,),
            in_specs=[
                pl.BlockSpec(memory_space=pl.ANY),
                pl.BlockSpec((tc, d), lambda i, dest: (i, 0)),
                pl.BlockSpec((tc, LANES), lambda i, dest: (i, 0)),
                pl.BlockSpec((1, 6, d), lambda i, dest: (i // per_seq, 0, 0)),
                pl.BlockSpec((1, d), lambda i, dest: (0, 0)),
            ],
            out_specs=pl.BlockSpec((tc, d), lambda i, dest: (i, 0)),
            scratch_shapes=[
                pltpu.VMEM((2, TOP_K, tc * ROW_TILES, LANES), F32),
                pltpu.SemaphoreType.DMA((2,)),
            ],
        ),
        out_shape=jax.ShapeDtypeStruct((t_all, d), F32),
        compiler_params=pltpu.CompilerParams(
            dimension_semantics=("arbitrary",), vmem_limit_bytes=VMEM_LIMIT),
        name="combine",
    )(dest, yb, x1_2d, route, mod3, final_g.reshape(1, d))


def kernel(x, c, ctx, c_ctx, ada_w, ada_b, norm1_g, norm2_g, w_in, rnn_conv_w, rnn_conv_b, rg_wa, rg_ba,
           rg_wx, rg_bx, rg_lambda, sc_conv_w, w_out, router_group_w, router_group_b, router_exp_w,
           router_exp_b, exp_w_gate, exp_w_up, exp_w_down, final_norm_g):
    bn, seq, d = x.shape
    assert d == D_MODEL and bn < MOD_ROWS and ada_w.shape[0] == 1
    t_all = bn * seq

    cc = jnp.concatenate([c, c_ctx[None], jnp.zeros((MOD_ROWS - bn - 1, d), F32)], axis=0)
    mod3 = _modulation(cc, ada_w[0], ada_b[0]).reshape(MOD_ROWS, 6, d)

    xr, gr, u, bg = _inproj(x, mod3, None, norm1_g[0], w_in[0], latent=True)
    (xr_c,) = _inproj(ctx, mod3, bn, norm1_g[0], w_in[0], latent=False)

    wg, bgate = _gate_weights(rg_wa[0], rg_ba[0], rg_wx[0], rg_bx[0])
    y_rnn = _rnn(xr, xr_c, gr, rnn_conv_w[0], rnn_conv_b[0], wg, bgate, rg_lambda[0])
    y_conv = _gconv(u, bg, sc_conv_w[0])

    wr = jnp.zeros((d, LANES), F32)
    wr = wr.at[:, :N_GROUPS].set(router_group_w[0]).at[:, EXPERT_LANE0:EXPERT_LANE0 + N_EXPERTS].set(router_exp_w[0])
    br = jnp.zeros((1, LANES), F32)
    br = br.at[0, :N_GROUPS].set(router_group_b[0]).at[0, EXPERT_LANE0:EXPERT_LANE0 + N_EXPERTS].set(router_exp_b[0])
    wr_hi, wr_lo = _split_bf16(wr)
    x1, mt, route, cnt = _outproj(x, y_rnn, y_conv, w_out[0], mod3, norm2_g[0], wr_hi, wr_lo, br)

    n_assign = t_all * TOP_K
    n_blocks = (n_assign + N_EXPERTS * (MOE_BLK - 1) + MOE_BLK - 1) // MOE_BLK
    counts = cnt[0, EXPERT_LANE0:EXPERT_LANE0 + N_EXPERTS].astype(jnp.int32)
    pcounts = (counts + MOE_BLK - 1) // MOE_BLK * MOE_BLK
    pends = jnp.cumsum(pcounts)
    pstarts = pends - pcounts
    experts = route[:, 0:TOP_K].astype(jnp.int32)
    ranks = route[:, 4:4 + TOP_K].astype(jnp.int32)
    onehot = experts[:, :, None] == jnp.arange(N_EXPERTS, dtype=jnp.int32)
    dest = (ranks + jnp.sum(jnp.where(onehot, pstarts, 0), axis=-1)).reshape(n_assign)
    n_used = (pends[-1] // MOE_BLK).astype(jnp.int32)
    blk_start = jnp.arange(n_blocks, dtype=jnp.int32) * MOE_BLK
    block_e = jnp.minimum(jnp.sum(blk_start[:, None] >= pends[None, :], axis=1), N_EXPERTS - 1)
    last_e = jnp.max(jnp.where(counts > 0, jnp.arange(N_EXPERTS, dtype=jnp.int32), 0))
    block_e = jnp.where(blk_start < pends[-1], block_e, last_e).astype(jnp.int32)
    eids = jnp.arange(N_EXPERTS, dtype=jnp.int32)
    used = counts > 0
    slot_of_e = (jnp.cumsum(used.astype(jnp.int32)) - 1) % 2
    later = jnp.where(used[None, :] & (eids[None, :] > eids[:, None]), eids[None, :], N_EXPERTS)
    next_of_e = jnp.min(later, axis=1)
    next_of_e = jnp.where(next_of_e == N_EXPERTS, -1, next_of_e)
    weight_slot = slot_of_e[block_e].astype(jnp.int32)
    next_e = next_of_e[block_e].astype(jnp.int32)

    n_slots = n_blocks * MOE_BLK
    slot_asg = _slotmap(dest, n_slots)
    yb = _experts(block_e, weight_slot, next_e, n_used.reshape(1), slot_asg, mt, exp_w_gate[0], exp_w_up[0],
                  exp_w_down[0], n_blocks)
    out = _combine(dest, yb, x1.reshape(t_all, d), route, mod3, final_norm_g, seq)
    return out.reshape(bn, seq, d)
```

```python
import functools

import jax
import jax.numpy as jnp
from jax import lax
from jax.experimental import pallas as pl
from jax.experimental.pallas import tpu as pltpu

F32 = jnp.float32
BF16 = jnp.bfloat16

D_MODEL = 1024
D_RNN = 512
D_CONV = 512
D_CONV_H = D_CONV // 2
RNN_HEADS = 8
RNN_HEAD_DIM = D_RNN // RNN_HEADS
GRID_W = 64
RG_C = 8.0
N_GROUPS = 4
EXPERTS_PER_GROUP = 8
N_EXPERTS = N_GROUPS * EXPERTS_PER_GROUP
TOP_K = 2
D_EXPERT = 512
NORM_EPS = 1e-6
F32_TINY = 1.1754944e-38

LANES = 128
SUBLANES = 8
ROW_TILES = D_MODEL // LANES
PACK_TILES = ROW_TILES // 2
N_LANE_GROUPS = D_RNN // LANES
EXPERT_LANE0 = N_GROUPS

MOD_ROWS = 16
MOD_TN = 768
INPROJ_TILE = 1024
OUTPROJ_TILE = 512
OUTPROJ_CHAINS = 1
COEFF_ROWS = 512
MOE_BLK = 256
MXU_TILE = 256
COMBINE_TILE = 256
DMA_UNROLL = 16
VMEM_LIMIT = 48 * 1024 * 1024
EXPERT_VMEM_LIMIT = 58 * 1024 * 1024


def _dot(a, b):
    return jnp.dot(a, b, preferred_element_type=F32)


def _split_bf16(x):
    hi = x.astype(BF16)
    lo = (x - hi.astype(F32)).astype(BF16)
    return hi, lo


def _mod_kernel(cc_ref, w_ref, b_ref, o_ref):
    s = cc_ref[...]
    s = s * jax.nn.sigmoid(s)
    s_hi, s_lo = _split_bf16(s)
    w_hi, w_lo = _split_bf16(w_ref[...])
    o_ref[...] = _dot(s_hi, w_hi) + _dot(s_lo, w_hi) + _dot(s_hi, w_lo) + b_ref[...]


def _modulation(cc, ada_w, ada_b):
    n = ada_w.shape[1]
    return pl.pallas_call(
        _mod_kernel,
        grid=(n // MOD_TN,),
        in_specs=[
            pl.BlockSpec((MOD_ROWS, D_MODEL), lambda j: (0, 0)),
            pl.BlockSpec((D_MODEL, MOD_TN), lambda j: (0, j)),
            pl.BlockSpec((1, MOD_TN), lambda j: (0, j)),
        ],
        out_specs=pl.BlockSpec((MOD_ROWS, MOD_TN), lambda j: (0, j)),
        out_shape=jax.ShapeDtypeStruct((MOD_ROWS, n), F32),
        compiler_params=pltpu.CompilerParams(vmem_limit_bytes=VMEM_LIMIT),
        name="mod",
    )(cc, ada_w, ada_b.reshape(1, n))


def _norm_mod(x, g, scale, shift):
    ms = jnp.mean(x * x, axis=-1, keepdims=True)
    y = x * lax.rsqrt(ms + NORM_EPS) * g
    return y * (1.0 + scale) + shift


def _inproj_kernel(x_ref, mod_ref, g_ref, w32_ref, *refs, latent):
    out_refs, w_ref = refs[:-1], refs[-1]

    @pl.when((pl.program_id(0) == 0) & (pl.program_id(1) == 0))
    def _():
        w_ref[...] = w32_ref[...].astype(BF16)

    h = _norm_mod(x_ref[0], g_ref[...], mod_ref[0, 1:2, :], mod_ref[0, 0:1, :])
    hb = h.astype(BF16)
    xr = _dot(hb, w_ref[:, 0:D_RNN])
    out_refs[0][0] = xr
    if latent:
        o = D_RNN
        out_refs[1][0] = jax.nn.gelu(_dot(hb, w_ref[:, o:o + D_RNN]), approximate=True)
        o += D_RNN
        v = _dot(hb, w_ref[:, o:o + D_CONV])
        out_refs[3][0] = _dot(hb, w_ref[:, o + D_CONV:o + 2 * D_CONV])
        cg = _dot(hb, w_ref[:, o + 2 * D_CONV:o + 3 * D_CONV])
        out_refs[2][0] = cg * v


def _inproj(x, mod3, mod_row, norm_g, w_in, latent):
    bn, n, d = x.shape
    tm = min(INPROJ_TILE, n)
    n_out = 4 if latent else 1
    width = w_in.shape[1] if latent else D_RNN
    mod_map = (lambda b, i: (b, 0, 0)) if mod_row is None else (lambda b, i: (mod_row, 0, 0))
    return pl.pallas_call(
        functools.partial(_inproj_kernel, latent=latent),
        grid=(bn, n // tm),
        in_specs=[
            pl.BlockSpec((1, tm, d), lambda b, i: (b, i, 0)),
            pl.BlockSpec((1, 6, d), mod_map),
            pl.BlockSpec((1, d), lambda b, i: (0, 0)),
            pl.BlockSpec((d, width), lambda b, i: (0, 0), pipeline_mode=pl.Buffered(1)),
        ],
        out_specs=[pl.BlockSpec((1, tm, D_RNN), lambda b, i: (b, i, 0))] * n_out,
        out_shape=[jax.ShapeDtypeStruct((bn, n, D_RNN), F32)] * n_out,
        scratch_shapes=[pltpu.VMEM((d, width), BF16)],
        compiler_params=pltpu.CompilerParams(
            dimension_semantics=("arbitrary", "arbitrary"), vmem_limit_bytes=VMEM_LIMIT),
        name="inproj_lat" if latent else "inproj_ctx",
    )(x, mod3, norm_g.reshape(1, d), w_in)


def _shift_rows(x, k):
    n = x.shape[0]
    row = lax.broadcasted_iota(jnp.int32, x.shape, 0)
    rolled = pltpu.roll(x, k % n, axis=0)
    valid = (row >= k) if k > 0 else (row < n + k)
    return jnp.where(valid, rolled, 0.0)


def _scan_pitch(chunk):
    pitch = chunk + SUBLANES
    return pitch if (pitch // SUBLANES) % 2 else pitch + SUBLANES


def _rnn_kernel(xr_ref, xrc_ref, gr_ref, cw_ref, cb_ref, wg_ref, bg_ref, lam_ref, u_ref, bgc_ref, scw_ref,
                y_ref, yc_ref,
                xc_s, ap_f, bp_f, ap_b, bp_b, hl_f, al_f, hl_b, al_b, hp_f, hp_b, *, n_lat, n_ctx):
    _gconv_block(u_ref, bgc_ref, scw_ref, yc_ref)

    nl = -lam_ref[...]
    sp = jnp.maximum(nl, 0.0) + jnp.log1p(jnp.exp(-jnp.abs(nl)))
    c1 = (-0.5 * RG_C) * sp
    cw = cw_ref[...]
    bias = cb_ref[...]
    wg = wg_ref[0]
    bg = bg_ref[0]
    dirs = ((ap_f, bp_f, hl_f, al_f, hp_f), (ap_b, bp_b, hl_b, al_b, hp_b))

    def conv_into(x, n):
        xc_s[pl.ds(0, n), :] = (cw[0:1] * _shift_rows(x, 2) + cw[1:2] * _shift_rows(x, 1)
                                + cw[2:3] * x + cw[3:4] * _shift_rows(x, -1)) + bias

    def coefficients(n):
        chunk = n // SUBLANES
        pitch = _scan_pitch(chunk)
        rows = max(chunk, min(n, COEFF_ROWS))
        per = rows // chunk

        def body(i, carry):
            xc = xc_s[pl.ds(pl.multiple_of(i * rows, SUBLANES), rows), :]
            gates = _dot(xc.astype(BF16), wg) + bg
            half_xc = 0.5 * xc
            for d in range(2):
                tr = jnp.tanh(0.5 * gates[:, (2 * d) * LANES:(2 * d + 1) * LANES])
                ti = jnp.tanh(0.5 * gates[:, (2 * d + 1) * LANES:(2 * d + 2) * LANES])
                log_a = c1[d:d + 1] + c1[d:d + 1] * tr
                a = jnp.exp(log_a)
                y = -jnp.tanh(log_a) * (a * a + 1.0)
                b = (y * lax.rsqrt(jnp.maximum(y, F32_TINY))) * (half_xc + half_xc * ti)
                for k in range(per):
                    dst = pl.multiple_of((i * per + k) * pitch, SUBLANES)
                    dirs[d][0][pl.ds(dst, chunk), :] = a[k * chunk:(k + 1) * chunk]
                    dirs[d][1][pl.ds(dst, chunk), :] = b[k * chunk:(k + 1) * chunk]
            return carry

        if n == rows:
            body(0, 0)
        else:
            lax.fori_loop(0, n // rows, body, 0)

    def scan(n, h0_f, h0_b, keep):
        chunk = n // SUBLANES
        pitch = _scan_pitch(chunk)

        def steps(jo, carry):
            h_f, a_f, h_b, a_b = carry
            for u in range(SUBLANES):
                j = jo * SUBLANES + u
                av = ap_f[pl.ds(j, SUBLANES, stride=pitch), :]
                h_f = av * h_f + bp_f[pl.ds(j, SUBLANES, stride=pitch), :]
                a_f = av * a_f
                jb = chunk - 1 - j
                av = ap_b[pl.ds(jb, SUBLANES, stride=pitch), :]
                h_b = av * h_b + bp_b[pl.ds(jb, SUBLANES, stride=pitch), :]
                a_b = av * a_b
                if keep:
                    o = pl.multiple_of(j * SUBLANES, SUBLANES)
                    hl_f[pl.ds(o, SUBLANES), :] = h_f
                    al_f[pl.ds(o, SUBLANES), :] = a_f
                    hl_b[pl.ds(o, SUBLANES), :] = h_b
                    al_b[pl.ds(o, SUBLANES), :] = a_b
            return h_f, a_f, h_b, a_b

        zeros = jnp.zeros((SUBLANES, LANES), F32)
        ones = jnp.ones((SUBLANES, LANES), F32)
        h_f, a_f, h_b, a_b = lax.fori_loop(0, chunk // SUBLANES, steps, (zeros, ones, zeros, ones))

        in_f = [h0_f]
        for c in range(SUBLANES):
            in_f.append(a_f[c:c + 1] * in_f[c] + h_f[c:c + 1])
        in_b = [h0_b]
        for c in range(SUBLANES - 1, -1, -1):
            in_b.append(a_b[c:c + 1] * in_b[-1] + h_b[c:c + 1])
        if keep:
            hin_f = jnp.concatenate(in_f[:SUBLANES], axis=0)
            hin_b = jnp.concatenate(in_b[SUBLANES - 1::-1], axis=0)

            def fix(jo, carry):
                for u in range(SUBLANES):
                    j = jo * SUBLANES + u
                    o = pl.multiple_of(j * SUBLANES, SUBLANES)
                    hp_f[pl.ds(j, SUBLANES, stride=pitch), :] = (
                        hl_f[pl.ds(o, SUBLANES), :] + al_f[pl.ds(o, SUBLANES), :] * hin_f)
                    hp_b[pl.ds(chunk - 1 - j, SUBLANES, stride=pitch), :] = (
                        hl_b[pl.ds(o, SUBLANES), :] + al_b[pl.ds(o, SUBLANES), :] * hin_b)
                return carry
            lax.fori_loop(0, chunk // SUBLANES, fix, 0)
        return in_f[SUBLANES], in_b[SUBLANES]

    zero = jnp.zeros((1, LANES), F32)
    conv_into(xrc_ref[0], n_ctx)
    coefficients(n_ctx)
    h0_f, h0_b = scan(n_ctx, zero, zero, keep=False)

    conv_into(xr_ref[0], n_lat)
    coefficients(n_lat)
    scan(n_lat, h0_f, h0_b, keep=True)

    chunk = n_lat // SUBLANES
    pitch = _scan_pitch(chunk)

    def emit(c, carry):
        src = pl.multiple_of(c * chunk, 2 * SUBLANES)
        dst = pl.multiple_of(c * pitch, SUBLANES)
        hsum = hp_f[pl.ds(dst, chunk), :] + hp_b[pl.ds(dst, chunk), :]
        y = gr_ref[0, pl.ds(src, chunk), :] * hsum
        y_ref[0, pl.ds(src, chunk), :] = y.astype(y_ref.dtype)
        return carry
    lax.fori_loop(0, SUBLANES, emit, 0)


def _mixers(xr, xr_c, gr, conv_w, conv_b, wg, bgate, lam, u, bg, sc_w):
    bn, n, _ = xr.shape
    n_ctx = xr_c.shape[1]
    assert n % (SUBLANES * SUBLANES) == 0 and n_ctx % (SUBLANES * SUBLANES) == 0 and n_ctx <= n
    pitched = SUBLANES * _scan_pitch(n // SUBLANES)
    seq_spec = pl.BlockSpec((1, n, LANES), lambda b, p: (b, 0, p))
    return pl.pallas_call(
        functools.partial(_rnn_kernel, n_lat=n, n_ctx=n_ctx),
        grid=(bn, N_LANE_GROUPS),
        in_specs=[
            seq_spec,
            pl.BlockSpec((1, n_ctx, LANES), lambda b, p: (b, 0, p)),
            seq_spec,
            pl.BlockSpec((4, LANES), lambda b, p: (0, p)),
            pl.BlockSpec((1, LANES), lambda b, p: (0, p)),
            pl.BlockSpec((1, LANES, 4 * LANES), lambda b, p: (p, 0, 0)),
            pl.BlockSpec((1, 1, 4 * LANES), lambda b, p: (p, 0, 0)),
            pl.BlockSpec((2, LANES), lambda b, p: (0, p)),
            seq_spec,
            seq_spec,
            pl.BlockSpec((3, LANES), lambda b, p: (0, p)),
        ],
        out_specs=[seq_spec, seq_spec],
        out_shape=[jax.ShapeDtypeStruct((bn, n, D_RNN), BF16)] * 2,
        scratch_shapes=[pltpu.VMEM((n, LANES), F32)]
        + [pltpu.VMEM((pitched, LANES), F32)] * 4
        + [pltpu.VMEM((n, LANES), F32)] * 4
        + [pltpu.VMEM((pitched, LANES), F32)] * 2,
        compiler_params=pltpu.CompilerParams(vmem_limit_bytes=VMEM_LIMIT),
        name="mixers",
    )(xr, xr_c, gr, conv_w, conv_b.reshape(1, D_RNN), wg, bgate, lam, u, bg, sc_w)


def _gate_weights(rg_wa, rg_ba, rg_wx, rg_bx):
    eye = jnp.eye(2, dtype=F32)
    blocks, biases = [], []
    for d in range(2):
        for w, bvec in ((rg_wa[d], rg_ba[d]), (rg_wx[d], rg_bx[d])):
            w4 = w.reshape(N_LANE_GROUPS, 2, RNN_HEAD_DIM, RNN_HEAD_DIM)
            bd = jnp.einsum("paij,ac->paicj", w4, eye).reshape(N_LANE_GROUPS, LANES, LANES)
            blocks.append(bd)
            biases.append(bvec.reshape(N_LANE_GROUPS, 1, LANES))
    return jnp.concatenate(blocks, axis=-1).astype(BF16), jnp.concatenate(biases, axis=-1)


def _gconv_block(u_ref, bg_ref, w_ref, y_ref):
    p = pl.program_id(1)
    u = u_ref[0]
    w = w_ref[...]

    @pl.when(p < D_CONV_H // LANES)
    def _():
        col = lax.broadcasted_iota(jnp.int32, u.shape, 0) % GRID_W
        left = jnp.where(col > 0, _shift_rows(u, 1), 0.0)
        right = jnp.where(col < GRID_W - 1, _shift_rows(u, -1), 0.0)
        y_ref[0] = (bg_ref[0] * (w[0:1] * left + w[1:2] * u + w[2:3] * right)).astype(y_ref.dtype)

    @pl.when(p >= D_CONV_H // LANES)
    def _():
        y_ref[0] = (bg_ref[0] * (w[0:1] * _shift_rows(u, GRID_W) + w[1:2] * u
                                 + w[2:3] * _shift_rows(u, -GRID_W))).astype(y_ref.dtype)


def _lane_max(x, mask):
    return jnp.max(jnp.where(mask, x, -jnp.inf), axis=-1, keepdims=True)


def _first_lane(cond, lane):
    return jnp.min(jnp.where(cond, lane, float(LANES)), axis=-1, keepdims=True)


def _outproj_kernel(x_ref, yr_ref, yc_ref, w32_ref, mod_ref, g_ref, wr_hi_ref, wr_lo_ref, br_ref,
                    x1_ref, mt_ref, route_ref, cnt_ref, carry, w_ref, *, tm):
    first = (pl.program_id(0) == 0) & (pl.program_id(1) == 0)

    @pl.when(first)
    def _():
        carry[...] = jnp.zeros_like(carry)
        w_ref[...] = w32_ref[...].astype(BF16)

    rows = tm // OUTPROJ_CHAINS
    total = carry[...]
    for c in range(OUTPROJ_CHAINS):
        total = _outproj_rows(c * rows, rows, total, x_ref, yr_ref, yc_ref, w_ref, mod_ref, g_ref,
                              wr_hi_ref, wr_lo_ref, br_ref, x1_ref, mt_ref, route_ref)
    carry[...] = total
    cnt_ref[...] = total


def _outproj_rows(r0, rows, counts, x_ref, yr_ref, yc_ref, w_ref, mod_ref, g_ref, wr_hi_ref, wr_lo_ref,
                  br_ref, x1_ref, mt_ref, route_ref):
    sl = pl.ds(r0, rows)
    mix = _dot(yr_ref[0, sl, :], w_ref[0:D_RNN, :]) + _dot(yc_ref[0, sl, :], w_ref[D_RNN:, :])
    x1 = x_ref[0, sl, :] + mod_ref[0, 2:3, :] * mix
    x1_ref[0, sl, :] = x1
    m = _norm_mod(x1, g_ref[...], mod_ref[0, 4:5, :], mod_ref[0, 3:4, :])
    half = D_MODEL // 2
    packed = pltpu.pack_elementwise([m[:, :half], m[:, half:]], packed_dtype=BF16)
    for s in range(PACK_TILES):
        mt_ref[pl.ds(r0 * PACK_TILES + s, rows, stride=PACK_TILES), :] = packed[:, s * LANES:(s + 1) * LANES]

    m_hi, m_lo = _split_bf16(m)
    logits = (_dot(m_hi, wr_hi_ref[...]) + _dot(m_lo, wr_hi_ref[...]) + _dot(m_hi, wr_lo_ref[...])
              + br_ref[...])
    lane_i = lax.broadcasted_iota(jnp.int32, logits.shape, 1)
    lane = lane_i.astype(F32)
    is_grp = lane_i < N_GROUPS
    g_max = _lane_max(logits, is_grp)
    grp = _first_lane(is_grp & (logits == g_max), lane)
    p_g = 1.0 / jnp.sum(jnp.where(is_grp, jnp.exp(logits - g_max), 0.0), axis=-1, keepdims=True)
    lo_lane = EXPERT_LANE0 + grp * EXPERTS_PER_GROUP
    in_grp = (lane >= lo_lane) & (lane < lo_lane + EXPERTS_PER_GROUP)
    l1 = _lane_max(logits, in_grp)
    i1 = _first_lane(in_grp & (logits == l1), lane)
    rest = in_grp & (lane != i1)
    l2 = _lane_max(logits, rest)
    i2 = _first_lane(rest & (logits == l2), lane)
    r21 = jnp.exp(l2 - l1)
    gate1 = p_g / (1.0 + r21)
    gate2 = gate1 * r21

    oh1 = jnp.where(lane == i1, 1.0, 0.0)
    oh2 = jnp.where(lane == i2, 1.0, 0.0)
    both = (oh1 + oh2).astype(BF16)
    ti = lax.broadcasted_iota(jnp.int32, (rows, rows), 0)
    tj = lax.broadcasted_iota(jnp.int32, (rows, rows), 1)
    tri = jnp.where(tj < ti, 1.0, 0.0).astype(BF16)
    before = _dot(tri, both) + counts
    rank1 = jnp.sum(oh1 * before, axis=-1, keepdims=True)
    rank2 = jnp.sum(oh2 * before, axis=-1, keepdims=True)

    e1 = i1 - EXPERT_LANE0
    e2 = i2 - EXPERT_LANE0
    out = jnp.zeros(logits.shape, F32)
    for k, val in enumerate((e1, e2, gate1, gate2, rank1, rank2)):
        out = jnp.where(lane_i == k, val, out)
    route_ref[sl, :] = out
    return counts + jnp.sum(oh1 + oh2, axis=0, keepdims=True)


def _outproj(x, y_rnn, y_conv, w_out, mod3, norm_g, wr_hi, wr_lo, br):
    bn, n, d = x.shape
    tm = min(OUTPROJ_TILE, n)
    nt = n // tm
    t_all = bn * n
    return pl.pallas_call(
        functools.partial(_outproj_kernel, tm=tm),
        grid=(bn, nt),
        in_specs=[
            pl.BlockSpec((1, tm, d), lambda b, i: (b, i, 0)),
            pl.BlockSpec((1, tm, D_RNN), lambda b, i: (b, i, 0)),
            pl.BlockSpec((1, tm, D_CONV), lambda b, i: (b, i, 0)),
            pl.BlockSpec((D_RNN + D_CONV, d), lambda b, i: (0, 0), pipeline_mode=pl.Buffered(1)),
            pl.BlockSpec((1, 6, d), lambda b, i: (b, 0, 0)),
            pl.BlockSpec((1, d), lambda b, i: (0, 0)),
            pl.BlockSpec((d, LANES), lambda b, i: (0, 0)),
            pl.BlockSpec((d, LANES), lambda b, i: (0, 0)),
            pl.BlockSpec((1, LANES), lambda b, i: (0, 0)),
        ],
        out_specs=[
            pl.BlockSpec((1, tm, d), lambda b, i: (b, i, 0)),
            pl.BlockSpec((tm * PACK_TILES, LANES), lambda b, i: (b * nt + i, 0)),
            pl.BlockSpec((tm, LANES), lambda b, i: (b * nt + i, 0)),
            pl.BlockSpec((1, LANES), lambda b, i: (0, 0)),
        ],
        out_shape=[
            jax.ShapeDtypeStruct((bn, n, d), F32),
            jax.ShapeDtypeStruct((t_all * PACK_TILES, LANES), jnp.uint32),
            jax.ShapeDtypeStruct((t_all, LANES), F32),
            jax.ShapeDtypeStruct((1, LANES), F32),
        ],
        scratch_shapes=[pltpu.VMEM((1, LANES), F32), pltpu.VMEM((D_RNN + D_CONV, d), BF16)],
        compiler_params=pltpu.CompilerParams(
            dimension_semantics=("arbitrary", "arbitrary"), vmem_limit_bytes=VMEM_LIMIT),
        name="outproj",
    )(x, y_rnn, y_conv, w_out, mod3, norm_g.reshape(1, d), wr_hi, wr_lo, br)


def _row_tile(ref, row):
    return ref.at[pl.ds(pl.multiple_of(row * ROW_TILES, ROW_TILES), ROW_TILES)]


def _slotmap_kernel(dest_ref, zeros_hbm, asg_ref, sem):
    fill = pltpu.make_async_copy(zeros_hbm, asg_ref, sem)
    fill.start()
    fill.wait()

    def body(c, carry):
        for u in range(DMA_UNROLL):
            a = c * DMA_UNROLL + u
            asg_ref[dest_ref[a]] = a
        return carry
    lax.fori_loop(0, dest_ref.shape[0] // DMA_UNROLL, body, 0)


def _slotmap(dest, n_slots):
    return pl.pallas_call(
        _slotmap_kernel,
        in_specs=[pl.BlockSpec(memory_space=pltpu.SMEM), pl.BlockSpec(memory_space=pl.ANY)],
        out_specs=pl.BlockSpec(memory_space=pltpu.SMEM),
        out_shape=jax.ShapeDtypeStruct((n_slots,), jnp.int32),
        scratch_shapes=[pltpu.SemaphoreType.DMA],
        name="slotmap",
    )(dest, jnp.zeros((n_slots,), jnp.int32))


def _expert_kernel(be_ref, ws_ref, ne_ref, nu_ref, asg_hbm, m_ref, wg_hbm, wu_hbm, wd_hbm, yb_ref,
                   xbuf_a, xbuf_b, idx, isems, wbuf_g, wbuf_u, wbuf_d, wsems, wg_s, wu_s, wd_s):
    j = pl.program_id(0)
    n_used = nu_ref[0]
    last = n_used - 1

    def idx_copy(blk, sl):
        return pltpu.make_async_copy(asg_hbm.at[blk], idx.at[sl], isems.at[sl])

    def copy_rows(buf, sl, r0, n):
        for u in range(n):
            tok = lax.shift_right_logical(idx[sl, 0, r0 + u], 1)
            src = pl.multiple_of(tok * PACK_TILES, PACK_TILES)
            buf[pl.ds((r0 + u) * PACK_TILES, PACK_TILES), :] = m_ref[pl.ds(src, PACK_TILES), :]

    def unpack(buf):
        halves = ([], [])
        for s in range(PACK_TILES):
            word = buf[pl.ds(s, MOE_BLK, stride=PACK_TILES), :]
            for k in range(2):
                part = pltpu.unpack_elementwise(word, index=k, packed_dtype=BF16, unpacked_dtype=F32)
                halves[k].append(part.astype(BF16))
        return jnp.concatenate(halves[0] + halves[1], axis=-1)

    n_pieces = (2 * D_EXPERT // MXU_TILE) * (D_MODEL // MXU_TILE) + (
        D_MODEL // MXU_TILE) * (D_EXPERT // MXU_TILE)
    bounds = [(p * MOE_BLK) // n_pieces for p in range(n_pieces + 1)]

    def compute(cur, nxt, nxt_sl):
        pieces = iter(zip(bounds[:-1], bounds[1:]))

        def dot_pieces(a, w_ref, n0):
            acc = None
            for k0 in range(0, a.shape[1], MXU_TILE):
                part = _dot(a[:, k0:k0 + MXU_TILE], w_ref[k0:k0 + MXU_TILE, n0:n0 + MXU_TILE])
                acc = part if acc is None else acc + part
                r0, r1 = next(pieces)
                copy_rows(nxt, nxt_sl, r0, r1 - r0)
            return acc

        xb16 = unpack(cur)
        acts = []
        for n0 in range(0, D_EXPERT, MXU_TILE):
            gate = dot_pieces(xb16, wg_s, n0)
            up = dot_pieces(xb16, wu_s, n0)
            acts.append(((gate * jax.nn.sigmoid(gate)) * up).astype(BF16))
        h = jnp.concatenate(acts, axis=-1)
        for n0 in range(0, D_MODEL, MXU_TILE):
            y = dot_pieces(h, wd_s, n0)
            for s in range(MXU_TILE // LANES):
                yb_ref[pl.ds(n0 // LANES + s, MOE_BLK, stride=ROW_TILES), :] = (
                    y[:, s * LANES:(s + 1) * LANES])

    @pl.when(j >= n_used)
    def _():
        yb_ref[...] = jnp.zeros_like(yb_ref)

    @pl.when(j < n_used)
    def _():
        slot = j % 2
        other = 1 - slot

        @pl.when(j == 0)
        def _():
            idx_copy(0, 0).start()
            idx_copy(0, 0).wait()

            def body(c, carry):
                copy_rows(xbuf_a, 0, c * DMA_UNROLL, DMA_UNROLL)
                return carry
            lax.fori_loop(0, MOE_BLK // DMA_UNROLL, body, 0)
            idx_copy(jnp.minimum(1, last), 1).start()

        e = be_ref[j]
        wslot = ws_ref[j]

        def weight_copies(expert, sl):
            return [pltpu.make_async_copy(src.at[expert], dst.at[sl], wsems.at[sl])
                    for src, dst in ((wg_hbm, wbuf_g), (wu_hbm, wbuf_u), (wd_hbm, wbuf_d))]

        @pl.when(j == 0)
        def _():
            for cp in weight_copies(e, wslot):
                cp.start()

        @pl.when((j == 0) | (e != be_ref[jnp.maximum(j - 1, 0)]))
        def _():
            for cp in weight_copies(e, wslot):
                cp.wait()
            wg_s[...] = wbuf_g[wslot].astype(BF16)
            wu_s[...] = wbuf_u[wslot].astype(BF16)
            wd_s[...] = wbuf_d[wslot].astype(BF16)

            @pl.when(ne_ref[j] >= 0)
            def _():
                for cp in weight_copies(ne_ref[j], 1 - wslot):
                    cp.start()

        idx_copy(0, other).wait()

        @pl.when(slot == 0)
        def _():
            compute(xbuf_a, xbuf_b, 1)

        @pl.when(slot == 1)
        def _():
            compute(xbuf_b, xbuf_a, 0)

        @pl.when(j < last)
        def _():
            idx_copy(jnp.minimum(j + 2, last), slot).start()


def _experts(block_e, weight_slot, next_e, n_used, slot_asg, mt, w_gate, w_up, w_down, n_blocks):
    return pl.pallas_call(
        _expert_kernel,
        grid_spec=pltpu.PrefetchScalarGridSpec(
            num_scalar_prefetch=4,
            grid=(n_blocks,),
            in_specs=[
                pl.BlockSpec(memory_space=pl.ANY),
                pl.BlockSpec(memory_space=pltpu.VMEM),
                pl.BlockSpec(memory_space=pl.ANY),
                pl.BlockSpec(memory_space=pl.ANY),
                pl.BlockSpec(memory_space=pl.ANY),
            ],
            out_specs=pl.BlockSpec((MOE_BLK * ROW_TILES, LANES), lambda j, *_: (j, 0)),
            scratch_shapes=[
                pltpu.VMEM((MOE_BLK * PACK_TILES, LANES), jnp.uint32),
                pltpu.VMEM((MOE_BLK * PACK_TILES, LANES), jnp.uint32),
                pltpu.SMEM((2, 1, MOE_BLK), jnp.int32),
                pltpu.SemaphoreType.DMA((2,)),
                pltpu.VMEM((2, D_MODEL, D_EXPERT), F32),
                pltpu.VMEM((2, D_MODEL, D_EXPERT), F32),
                pltpu.VMEM((2, D_EXPERT, D_MODEL), F32),
                pltpu.SemaphoreType.DMA((2,)),
                pltpu.VMEM((D_MODEL, D_EXPERT), BF16),
                pltpu.VMEM((D_MODEL, D_EXPERT), BF16),
                pltpu.VMEM((D_EXPERT, D_MODEL), BF16),
            ],
        ),
        out_shape=jax.ShapeDtypeStruct((n_blocks * MOE_BLK * ROW_TILES, LANES), F32),
        compiler_params=pltpu.CompilerParams(
            dimension_semantics=("arbitrary",), vmem_limit_bytes=EXPERT_VMEM_LIMIT),
        name="expert",
    )(block_e, weight_slot, next_e, n_used, slot_asg.reshape(n_blocks, 1, MOE_BLK), mt, w_gate, w_up, w_down)


def _combine_kernel(dest_ref, yb_hbm, x1_ref, route_ref, mod_ref, g_ref, o_ref, ybuf, sems, *, tc):
    i = pl.program_id(0)
    slot = i % 2

    def row_copy(d, sl, k, r):
        return pltpu.make_async_copy(_row_tile(yb_hbm, d), _row_tile(ybuf.at[sl, k], r), sems.at[sl])

    def gather(step, sl):
        def issue(c, carry):
            for u in range(DMA_UNROLL):
                r = c * DMA_UNROLL + u
                for k in range(TOP_K):
                    row_copy(dest_ref[TOP_K * (step * tc + r) + k], sl, k, r).start(priority=k)
            return carry
        lax.fori_loop(0, tc // DMA_UNROLL, issue, 0)

    @pl.when(i == 0)
    def _():
        gather(0, 0)

    @pl.when(i + 1 < pl.num_programs(0))
    def _():
        gather(i + 1, 1 - slot)

    def drain(c, carry):
        for u in range(DMA_UNROLL * TOP_K):
            row_copy(0, slot, 0, 0).wait()
        return carry
    lax.fori_loop(0, tc // DMA_UNROLL, drain, 0)

    def rows(k):
        return jnp.concatenate(
            [ybuf[slot, k, pl.ds(s, tc, stride=ROW_TILES), :] for s in range(ROW_TILES)], axis=-1)

    route = route_ref[...]
    y = route[:, 2:3] * rows(0) + route[:, 3:4] * rows(1)
    x2 = x1_ref[...] + mod_ref[0, 5:6, :] * y
    ms = jnp.mean(x2 * x2, axis=-1, keepdims=True)
    o_ref[...] = x2 * lax.rsqrt(ms + NORM_EPS) * g_ref[...]


def _combine(dest, yb, x1_2d, route, mod3, final_g, seq):
    t_all, d = x1_2d.shape
    tc = min(COMBINE_TILE, seq)
    per_seq = seq // tc
    return pl.pallas_call(
        functools.partial(_combine_kernel, tc=tc),
        grid_spec=pltpu.PrefetchScalarGridSpec(
            num_scalar_prefetch=1,
            grid=(t_all // tc,),
            in_specs=[
                pl.BlockSpec(memory_space=pl.ANY),
                pl.BlockSpec((tc, d), lambda i, dest: (i, 0)),
                pl.BlockSpec((tc, LANES), lambda i, dest: (i, 0)),
                pl.BlockSpec((1, 6, d), lambda i, dest: (i // per_seq, 0, 0)),
                pl.BlockSpec((1, d), lambda i, dest: (0, 0)),
            ],
            out_specs=pl.BlockSpec((tc, d), lambda i, dest: (i, 0)),
            scratch_shapes=[
                pltpu.VMEM((2, TOP_K, tc * ROW_TILES, LANES), F32),
                pltpu.SemaphoreType.DMA((2,)),
            ],
        ),
        out_shape=jax.ShapeDtypeStruct((t_all, d), F32),
        compiler_params=pltpu.CompilerParams(
            dimension_semantics=("arbitrary",), vmem_limit_bytes=VMEM_LIMIT),
        name="combine",
    )(dest, yb, x1_2d, route, mod3, final_g.reshape(1, d))


def kernel(x, c, ctx, c_ctx, ada_w, ada_b, norm1_g, norm2_g, w_in, rnn_conv_w, rnn_conv_b, rg_wa, rg_ba,
           rg_wx, rg_bx, rg_lambda, sc_conv_w, w_out, router_group_w, router_group_b, router_exp_w,
           router_exp_b, exp_w_gate, exp_w_up, exp_w_down, final_norm_g):
    bn, seq, d = x.shape
    assert d == D_MODEL and bn < MOD_ROWS and ada_w.shape[0] == 1
    t_all = bn * seq

    cc = jnp.concatenate([c, c_ctx[None], jnp.zeros((MOD_ROWS - bn - 1, d), F32)], axis=0)
    mod3 = _modulation(cc, ada_w[0], ada_b[0]).reshape(MOD_ROWS, 6, d)

    xr, gr, u, bg = _inproj(x, mod3, None, norm1_g[0], w_in[0], latent=True)
    (xr_c,) = _inproj(ctx, mod3, bn, norm1_g[0], w_in[0], latent=False)

    wg, bgate = _gate_weights(rg_wa[0], rg_ba[0], rg_wx[0], rg_bx[0])
    assert D_RNN == D_CONV
    y_rnn, y_conv = _mixers(xr, xr_c, gr, rnn_conv_w[0], rnn_conv_b[0], wg, bgate, rg_lambda[0], u, bg,
                            sc_conv_w[0])

    wr = jnp.zeros((d, LANES), F32)
    wr = wr.at[:, :N_GROUPS].set(router_group_w[0]).at[:, EXPERT_LANE0:EXPERT_LANE0 + N_EXPERTS].set(router_exp_w[0])
    br = jnp.zeros((1, LANES), F32)
    br = br.at[0, :N_GROUPS].set(router_group_b[0]).at[0, EXPERT_LANE0:EXPERT_LANE0 + N_EXPERTS].set(router_exp_b[0])
    wr_hi, wr_lo = _split_bf16(wr)
    x1, mt, route, cnt = _outproj(x, y_rnn, y_conv, w_out[0], mod3, norm2_g[0], wr_hi, wr_lo, br)

    n_assign = t_all * TOP_K
    n_blocks = (n_assign + N_EXPERTS * (MOE_BLK - 1) + MOE_BLK - 1) // MOE_BLK
    counts = cnt[0, EXPERT_LANE0:EXPERT_LANE0 + N_EXPERTS].astype(jnp.int32)
    pcounts = (counts + MOE_BLK - 1) // MOE_BLK * MOE_BLK
    pends = jnp.cumsum(pcounts)
    pstarts = pends - pcounts
    experts = route[:, 0:TOP_K].astype(jnp.int32)
    ranks = route[:, 4:4 + TOP_K].astype(jnp.int32)
    onehot = experts[:, :, None] == jnp.arange(N_EXPERTS, dtype=jnp.int32)
    dest = (ranks + jnp.sum(jnp.where(onehot, pstarts, 0), axis=-1)).reshape(n_assign)
    n_used = (pends[-1] // MOE_BLK).astype(jnp.int32)
    blk_start = jnp.arange(n_blocks, dtype=jnp.int32) * MOE_BLK
    block_e = jnp.minimum(jnp.sum(blk_start[:, None] >= pends[None, :], axis=1), N_EXPERTS - 1)
    last_e = jnp.max(jnp.where(counts > 0, jnp.arange(N_EXPERTS, dtype=jnp.int32), 0))
    block_e = jnp.where(blk_start < pends[-1], block_e, last_e).astype(jnp.int32)
    eids = jnp.arange(N_EXPERTS, dtype=jnp.int32)
    used = counts > 0
    slot_of_e = (jnp.cumsum(used.astype(jnp.int32)) - 1) % 2
    later = jnp.where(used[None, :] & (eids[None, :] > eids[:, None]), eids[None, :], N_EXPERTS)
    next_of_e = jnp.min(later, axis=1)
    next_of_e = jnp.where(next_of_e == N_EXPERTS, -1, next_of_e)
    is_e = block_e[:, None] == eids[None, :]
    weight_slot = jnp.sum(jnp.where(is_e, slot_of_e[None, :], 0), axis=1).astype(jnp.int32)
    next_e = jnp.sum(jnp.where(is_e, next_of_e[None, :], 0), axis=1).astype(jnp.int32)

    n_slots = n_blocks * MOE_BLK
    slot_asg = _slotmap(dest, n_slots)
    yb = _experts(block_e, weight_slot, next_e, n_used.reshape(1), slot_asg, mt, exp_w_gate[0], exp_w_up[0],
                  exp_w_down[0], n_blocks)
    out = _combine(dest, yb, x1.reshape(t_all, d), route, mod3, final_norm_g, seq)
    return out.reshape(bn, seq, d)
```

```python
import functools

import jax
import jax.numpy as jnp
from jax import lax
from jax.experimental import pallas as pl
from jax.experimental.pallas import tpu as pltpu

F32 = jnp.float32
BF16 = jnp.bfloat16

D_MODEL = 1024
D_RNN = 512
D_CONV = 512
D_CONV_H = D_CONV // 2
RNN_HEADS = 8
RNN_HEAD_DIM = D_RNN // RNN_HEADS
GRID_W = 64
RG_C = 8.0
N_GROUPS = 4
EXPERTS_PER_GROUP = 8
N_EXPERTS = N_GROUPS * EXPERTS_PER_GROUP
TOP_K = 2
D_EXPERT = 512
NORM_EPS = 1e-6
F32_TINY = 1.1754944e-38

LANES = 128
SUBLANES = 8
ROW_TILES = D_MODEL // LANES
PACK_TILES = ROW_TILES // 2
N_LANE_GROUPS = D_RNN // LANES
EXPERT_LANE0 = N_GROUPS

MOD_ROWS = 16
MOD_TN = 768
INPROJ_TILE = 1024
OUTPROJ_TILE = 512
COEFF_ROWS = 512
MOE_BLK = 256
MXU_TILE = 256
COMBINE_TILE = 256
DMA_UNROLL = 16
VMEM_LIMIT = 48 * 1024 * 1024
EXPERT_VMEM_LIMIT = 58 * 1024 * 1024


def _dot(a, b):
    return jnp.dot(a, b, preferred_element_type=F32)


def _split_bf16(x):
    hi = x.astype(BF16)
    lo = (x - hi.astype(F32)).astype(BF16)
    return hi, lo


def _mod_kernel(cc_ref, w_ref, b_ref, o_ref):
    s = cc_ref[...]
    s = s * jax.nn.sigmoid(s)
    s_hi, s_lo = _split_bf16(s)
    w_hi, w_lo = _split_bf16(w_ref[...])
    o_ref[...] = _dot(s_hi, w_hi) + _dot(s_lo, w_hi) + _dot(s_hi, w_lo) + b_ref[...]


def _modulation(cc, ada_w, ada_b):
    n = ada_w.shape[1]
    return pl.pallas_call(
        _mod_kernel,
        grid=(n // MOD_TN,),
        in_specs=[
            pl.BlockSpec((MOD_ROWS, D_MODEL), lambda j: (0, 0)),
            pl.BlockSpec((D_MODEL, MOD_TN), lambda j: (0, j)),
            pl.BlockSpec((1, MOD_TN), lambda j: (0, j)),
        ],
        out_specs=pl.BlockSpec((MOD_ROWS, MOD_TN), lambda j: (0, j)),
        out_shape=jax.ShapeDtypeStruct((MOD_ROWS, n), F32),
        compiler_params=pltpu.CompilerParams(vmem_limit_bytes=VMEM_LIMIT),
        name="mod",
    )(cc, ada_w, ada_b.reshape(1, n))


def _norm_mod(x, g, scale, shift):
    ms = jnp.mean(x * x, axis=-1, keepdims=True)
    y = x * lax.rsqrt(ms + NORM_EPS) * g
    return y * (1.0 + scale) + shift


def _inproj_kernel(x_ref, mod_ref, g_ref, w32_ref, *refs, latent):
    out_refs, w_ref = refs[:-1], refs[-1]

    @pl.when((pl.program_id(0) == 0) & (pl.program_id(1) == 0))
    def _():
        w_ref[...] = w32_ref[...].astype(BF16)

    h = _norm_mod(x_ref[0], g_ref[...], mod_ref[0, 1:2, :], mod_ref[0, 0:1, :])
    hb = h.astype(BF16)
    xr = _dot(hb, w_ref[:, 0:D_RNN])
    out_refs[0][0] = xr
    if latent:
        o = D_RNN
        out_refs[1][0] = jax.nn.gelu(_dot(hb, w_ref[:, o:o + D_RNN]), approximate=True)
        o += D_RNN
        v = _dot(hb, w_ref[:, o:o + D_CONV])
        out_refs[3][0] = _dot(hb, w_ref[:, o + D_CONV:o + 2 * D_CONV])
        cg = _dot(hb, w_ref[:, o + 2 * D_CONV:o + 3 * D_CONV])
        out_refs[2][0] = cg * v


def _inproj(x, mod3, mod_row, norm_g, w_in, latent):
    bn, n, d = x.shape
    tm = min(INPROJ_TILE, n)
    n_out = 4 if latent else 1
    width = w_in.shape[1] if latent else D_RNN
    mod_map = (lambda b, i: (b, 0, 0)) if mod_row is None else (lambda b, i: (mod_row, 0, 0))
    return pl.pallas_call(
        functools.partial(_inproj_kernel, latent=latent),
        grid=(bn, n // tm),
        in_specs=[
            pl.BlockSpec((1, tm, d), lambda b, i: (b, i, 0)),
            pl.BlockSpec((1, 6, d), mod_map),
            pl.BlockSpec((1, d), lambda b, i: (0, 0)),
            pl.BlockSpec((d, width), lambda b, i: (0, 0), pipeline_mode=pl.Buffered(1)),
        ],
        out_specs=[pl.BlockSpec((1, tm, D_RNN), lambda b, i: (b, i, 0))] * n_out,
        out_shape=[jax.ShapeDtypeStruct((bn, n, D_RNN), F32)] * n_out,
        scratch_shapes=[pltpu.VMEM((d, width), BF16)],
        compiler_params=pltpu.CompilerParams(
            dimension_semantics=("arbitrary", "arbitrary"), vmem_limit_bytes=VMEM_LIMIT),
        name="inproj_lat" if latent else "inproj_ctx",
    )(x, mod3, norm_g.reshape(1, d), w_in)


def _shift_rows(x, k):
    n = x.shape[0]
    row = lax.broadcasted_iota(jnp.int32, x.shape, 0)
    rolled = pltpu.roll(x, k % n, axis=0)
    valid = (row >= k) if k > 0 else (row < n + k)
    return jnp.where(valid, rolled, 0.0)


def _scan_pitch(chunk):
    pitch = chunk + SUBLANES
    return pitch if (pitch // SUBLANES) % 2 else pitch + SUBLANES


def _rnn_kernel(xr_ref, xrc_ref, gr_ref, cw_ref, cb_ref, wg_ref, bg_ref, lam_ref, u_ref, bgc_ref, scw_ref,
                y_ref, yc_ref,
                xc_s, ap_f, bp_f, ap_b, bp_b, hl_f, al_f, hl_b, al_b, hp_f, hp_b, *, n_lat, n_ctx):
    _gconv_block(u_ref, bgc_ref, scw_ref, yc_ref)

    nl = -lam_ref[...]
    sp = jnp.maximum(nl, 0.0) + jnp.log1p(jnp.exp(-jnp.abs(nl)))
    c1 = (-0.5 * RG_C) * sp
    cw = cw_ref[...]
    bias = cb_ref[...]
    wg = wg_ref[0]
    bg = bg_ref[0]
    dirs = ((ap_f, bp_f, hl_f, al_f, hp_f), (ap_b, bp_b, hl_b, al_b, hp_b))

    def conv_into(x, n):
        xc_s[pl.ds(0, n), :] = (cw[0:1] * _shift_rows(x, 2) + cw[1:2] * _shift_rows(x, 1)
                                + cw[2:3] * x + cw[3:4] * _shift_rows(x, -1)) + bias

    def coefficients(n):
        chunk = n // SUBLANES
        pitch = _scan_pitch(chunk)
        rows = max(chunk, min(n, COEFF_ROWS))
        per = rows // chunk

        def body(i, carry):
            xc = xc_s[pl.ds(pl.multiple_of(i * rows, SUBLANES), rows), :]
            gates = _dot(xc.astype(BF16), wg) + bg
            half_xc = 0.5 * xc
            for d in range(2):
                tr = jnp.tanh(0.5 * gates[:, (2 * d) * LANES:(2 * d + 1) * LANES])
                ti = jnp.tanh(0.5 * gates[:, (2 * d + 1) * LANES:(2 * d + 2) * LANES])
                log_a = c1[d:d + 1] + c1[d:d + 1] * tr
                a = jnp.exp(log_a)
                y = -jnp.tanh(log_a) * (a * a + 1.0)
                b = (y * lax.rsqrt(jnp.maximum(y, F32_TINY))) * (half_xc + half_xc * ti)
                for k in range(per):
                    dst = pl.multiple_of((i * per + k) * pitch, SUBLANES)
                    dirs[d][0][pl.ds(dst, chunk), :] = a[k * chunk:(k + 1) * chunk]
                    dirs[d][1][pl.ds(dst, chunk), :] = b[k * chunk:(k + 1) * chunk]
            return carry

        if n == rows:
            body(0, 0)
        else:
            lax.fori_loop(0, n // rows, body, 0)

    def scan(n, h0_f, h0_b, keep):
        chunk = n // SUBLANES
        pitch = _scan_pitch(chunk)

        def steps(jo, carry):
            h_f, a_f, h_b, a_b = carry
            for u in range(SUBLANES):
                j = jo * SUBLANES + u
                av = ap_f[pl.ds(j, SUBLANES, stride=pitch), :]
                h_f = av * h_f + bp_f[pl.ds(j, SUBLANES, stride=pitch), :]
                a_f = av * a_f
                jb = chunk - 1 - j
                av = ap_b[pl.ds(jb, SUBLANES, stride=pitch), :]
                h_b = av * h_b + bp_b[pl.ds(jb, SUBLANES, stride=pitch), :]
                a_b = av * a_b
                if keep:
                    o = pl.multiple_of(j * SUBLANES, SUBLANES)
                    hl_f[pl.ds(o, SUBLANES), :] = h_f
                    al_f[pl.ds(o, SUBLANES), :] = a_f
                    hl_b[pl.ds(o, SUBLANES), :] = h_b
                    al_b[pl.ds(o, SUBLANES), :] = a_b
            return h_f, a_f, h_b, a_b

        zeros = jnp.zeros((SUBLANES, LANES), F32)
        ones = jnp.ones((SUBLANES, LANES), F32)
        h_f, a_f, h_b, a_b = lax.fori_loop(0, chunk // SUBLANES, steps, (zeros, ones, zeros, ones))

        in_f = [h0_f]
        for c in range(SUBLANES):
            in_f.append(a_f[c:c + 1] * in_f[c] + h_f[c:c + 1])
        in_b = [h0_b]
        for c in range(SUBLANES - 1, -1, -1):
            in_b.append(a_b[c:c + 1] * in_b[-1] + h_b[c:c + 1])
        if keep:
            hin_f = jnp.concatenate(in_f[:SUBLANES], axis=0)
            hin_b = jnp.concatenate(in_b[SUBLANES - 1::-1], axis=0)

            def fix(jo, carry):
                for u in range(SUBLANES):
                    j = jo * SUBLANES + u
                    o = pl.multiple_of(j * SUBLANES, SUBLANES)
                    hp_f[pl.ds(j, SUBLANES, stride=pitch), :] = (
                        hl_f[pl.ds(o, SUBLANES), :] + al_f[pl.ds(o, SUBLANES), :] * hin_f)
                    hp_b[pl.ds(chunk - 1 - j, SUBLANES, stride=pitch), :] = (
                        hl_b[pl.ds(o, SUBLANES), :] + al_b[pl.ds(o, SUBLANES), :] * hin_b)
                return carry
            lax.fori_loop(0, chunk // SUBLANES, fix, 0)
        return in_f[SUBLANES], in_b[SUBLANES]

    zero = jnp.zeros((1, LANES), F32)
    conv_into(xrc_ref[0], n_ctx)
    coefficients(n_ctx)
    h0_f, h0_b = scan(n_ctx, zero, zero, keep=False)

    conv_into(xr_ref[0], n_lat)
    coefficients(n_lat)
    scan(n_lat, h0_f, h0_b, keep=True)

    chunk = n_lat // SUBLANES
    pitch = _scan_pitch(chunk)

    def emit(c, carry):
        src = pl.multiple_of(c * chunk, 2 * SUBLANES)
        dst = pl.multiple_of(c * pitch, SUBLANES)
        hsum = hp_f[pl.ds(dst, chunk), :] + hp_b[pl.ds(dst, chunk), :]
        y = gr_ref[0, pl.ds(src, chunk), :] * hsum
        y_ref[0, pl.ds(src, chunk), :] = y.astype(y_ref.dtype)
        return carry
    lax.fori_loop(0, SUBLANES, emit, 0)


def _mixers(xr, xr_c, gr, conv_w, conv_b, wg, bgate, lam, u, bg, sc_w):
    bn, n, _ = xr.shape
    n_ctx = xr_c.shape[1]
    assert n % (SUBLANES * SUBLANES) == 0 and n_ctx % (SUBLANES * SUBLANES) == 0 and n_ctx <= n
    pitched = SUBLANES * _scan_pitch(n // SUBLANES)
    seq_spec = pl.BlockSpec((1, n, LANES), lambda b, p: (b, 0, p))
    return pl.pallas_call(
        functools.partial(_rnn_kernel, n_lat=n, n_ctx=n_ctx),
        grid=(bn, N_LANE_GROUPS),
        in_specs=[
            seq_spec,
            pl.BlockSpec((1, n_ctx, LANES), lambda b, p: (b, 0, p)),
            seq_spec,
            pl.BlockSpec((4, LANES), lambda b, p: (0, p)),
            pl.BlockSpec((1, LANES), lambda b, p: (0, p)),
            pl.BlockSpec((1, LANES, 4 * LANES), lambda b, p: (p, 0, 0)),
            pl.BlockSpec((1, 1, 4 * LANES), lambda b, p: (p, 0, 0)),
            pl.BlockSpec((2, LANES), lambda b, p: (0, p)),
            seq_spec,
            seq_spec,
            pl.BlockSpec((3, LANES), lambda b, p: (0, p)),
        ],
        out_specs=[seq_spec, seq_spec],
        out_shape=[jax.ShapeDtypeStruct((bn, n, D_RNN), BF16)] * 2,
        scratch_shapes=[pltpu.VMEM((n, LANES), F32)]
        + [pltpu.VMEM((pitched, LANES), F32)] * 4
        + [pltpu.VMEM((n, LANES), F32)] * 4
        + [pltpu.VMEM((pitched, LANES), F32)] * 2,
        compiler_params=pltpu.CompilerParams(vmem_limit_bytes=VMEM_LIMIT),
        name="mixers",
    )(xr, xr_c, gr, conv_w, conv_b.reshape(1, D_RNN), wg, bgate, lam, u, bg, sc_w)


def _gate_weights(rg_wa, rg_ba, rg_wx, rg_bx):
    eye = jnp.eye(2, dtype=F32)
    blocks, biases = [], []
    for d in range(2):
        for w, bvec in ((rg_wa[d], rg_ba[d]), (rg_wx[d], rg_bx[d])):
            w4 = w.reshape(N_LANE_GROUPS, 2, RNN_HEAD_DIM, RNN_HEAD_DIM)
            bd = jnp.einsum("paij,ac->paicj", w4, eye).reshape(N_LANE_GROUPS, LANES, LANES)
            blocks.append(bd)
            biases.append(bvec.reshape(N_LANE_GROUPS, 1, LANES))
    return jnp.concatenate(blocks, axis=-1).astype(BF16), jnp.concatenate(biases, axis=-1)


def _gconv_block(u_ref, bg_ref, w_ref, y_ref):
    p = pl.program_id(1)
    u = u_ref[0]
    w = w_ref[...]

    @pl.when(p < D_CONV_H // LANES)
    def _():
        col = lax.broadcasted_iota(jnp.int32, u.shape, 0) % GRID_W
        left = jnp.where(col > 0, _shift_rows(u, 1), 0.0)
        right = jnp.where(col < GRID_W - 1, _shift_rows(u, -1), 0.0)
        y_ref[0] = (bg_ref[0] * (w[0:1] * left + w[1:2] * u + w[2:3] * right)).astype(y_ref.dtype)

    @pl.when(p >= D_CONV_H // LANES)
    def _():
        y_ref[0] = (bg_ref[0] * (w[0:1] * _shift_rows(u, GRID_W) + w[1:2] * u
                                 + w[2:3] * _shift_rows(u, -GRID_W))).astype(y_ref.dtype)


def _lane_max(x, mask):
    return jnp.max(jnp.where(mask, x, -jnp.inf), axis=-1, keepdims=True)


def _first_lane(cond, lane):
    return jnp.min(jnp.where(cond, lane, float(LANES)), axis=-1, keepdims=True)


def _outproj_kernel(x_ref, yr_ref, yc_ref, w32_ref, mod_ref, g_ref, wr_ref, br_ref,
                    x1_ref, mt_ref, route_ref, cnt_ref, carry, w_ref, m_s, *, tm):
    s = pl.program_id(0)

    @pl.when(s == 0)
    def _():
        carry[...] = jnp.zeros_like(carry)
        w_ref[...] = w32_ref[...].astype(BF16)
        m_s[...] = jnp.zeros_like(m_s)

    logits = _dot(m_s[...], wr_ref[...]) + br_ref[...]
    lane_i = lax.broadcasted_iota(jnp.int32, logits.shape, 1)
    lane = lane_i.astype(F32)
    is_grp = lane_i < N_GROUPS
    g_max = _lane_max(logits, is_grp)
    grp = _first_lane(is_grp & (logits == g_max), lane)
    p_g = 1.0 / jnp.sum(jnp.where(is_grp, jnp.exp(logits - g_max), 0.0), axis=-1, keepdims=True)
    lo_lane = EXPERT_LANE0 + grp * EXPERTS_PER_GROUP
    in_grp = (lane >= lo_lane) & (lane < lo_lane + EXPERTS_PER_GROUP)
    l1 = _lane_max(logits, in_grp)
    i1 = _first_lane(in_grp & (logits == l1), lane)
    rest = in_grp & (lane != i1)
    l2 = _lane_max(logits, rest)
    i2 = _first_lane(rest & (logits == l2), lane)
    r21 = jnp.exp(l2 - l1)
    gate1 = p_g / (1.0 + r21)
    gate2 = gate1 * r21

    oh1 = jnp.where(lane == i1, 1.0, 0.0)
    oh2 = jnp.where(lane == i2, 1.0, 0.0)
    both = (oh1 + oh2).astype(BF16)
    ti = lax.broadcasted_iota(jnp.int32, (tm, tm), 0)
    tj = lax.broadcasted_iota(jnp.int32, (tm, tm), 1)
    tri = jnp.where(tj < ti, 1.0, 0.0).astype(BF16)
    counts = carry[...]
    before = _dot(tri, both) + counts
    rank1 = jnp.sum(oh1 * before, axis=-1, keepdims=True)
    rank2 = jnp.sum(oh2 * before, axis=-1, keepdims=True)
    out = jnp.zeros(logits.shape, F32)
    for k, val in enumerate((i1 - EXPERT_LANE0, i2 - EXPERT_LANE0, gate1, gate2, rank1, rank2)):
        out = jnp.where(lane_i == k, val, out)
    route_ref[...] = out
    real = jnp.where(s > 0, 1.0, 0.0)
    total = counts + real * jnp.sum(oh1 + oh2, axis=0, keepdims=True)
    carry[...] = total
    cnt_ref[...] = total

    mix = _dot(yr_ref[0], w_ref[0:D_RNN, :]) + _dot(yc_ref[0], w_ref[D_RNN:, :])
    x1 = x_ref[0] + mod_ref[0, 2:3, :] * mix
    x1_ref[0] = x1
    m_new = _norm_mod(x1, g_ref[...], mod_ref[0, 4:5, :], mod_ref[0, 3:4, :])
    m_s[...] = m_new.astype(BF16)
    half = D_MODEL // 2
    packed = pltpu.pack_elementwise([m_new[:, :half], m_new[:, half:]], packed_dtype=BF16)
    for q in range(PACK_TILES):
        mt_ref[pl.ds(q, tm, stride=PACK_TILES), :] = packed[:, q * LANES:(q + 1) * LANES]


def _outproj(x, y_rnn, y_conv, w_out, mod3, norm_g, wr, br):
    bn, n, d = x.shape
    tm = min(OUTPROJ_TILE, n)
    nt = n // tm
    n_tiles = bn * nt
    t_all = bn * n

    def cur(s):
        return jnp.minimum(s, n_tiles - 1)

    def prev(s):
        return jnp.maximum(s - 1, 0)

    def seq_map(s):
        return (cur(s) // nt, cur(s) % nt, 0)

    const = lambda s: (0, 0)
    return pl.pallas_call(
        functools.partial(_outproj_kernel, tm=tm),
        grid=(n_tiles + 1,),
        in_specs=[
            pl.BlockSpec((1, tm, d), seq_map),
            pl.BlockSpec((1, tm, D_RNN), seq_map),
            pl.BlockSpec((1, tm, D_CONV), seq_map),
            pl.BlockSpec((D_RNN + D_CONV, d), const, pipeline_mode=pl.Buffered(1)),
            pl.BlockSpec((1, 6, d), lambda s: (cur(s) // nt, 0, 0)),
            pl.BlockSpec((1, d), const),
            pl.BlockSpec((d, LANES), const),
            pl.BlockSpec((1, LANES), const),
        ],
        out_specs=[
            pl.BlockSpec((1, tm, d), seq_map),
            pl.BlockSpec((tm * PACK_TILES, LANES), lambda s: (cur(s), 0)),
            pl.BlockSpec((tm, LANES), lambda s: (prev(s), 0)),
            pl.BlockSpec((1, LANES), const),
        ],
        out_shape=[
            jax.ShapeDtypeStruct((bn, n, d), F32),
            jax.ShapeDtypeStruct((t_all * PACK_TILES, LANES), jnp.uint32),
            jax.ShapeDtypeStruct((t_all, LANES), F32),
            jax.ShapeDtypeStruct((1, LANES), F32),
        ],
        scratch_shapes=[pltpu.VMEM((1, LANES), F32), pltpu.VMEM((D_RNN + D_CONV, d), BF16),
                        pltpu.VMEM((tm, d), BF16)],
        compiler_params=pltpu.CompilerParams(
            dimension_semantics=("arbitrary",), vmem_limit_bytes=VMEM_LIMIT),
        name="outproj",
    )(x, y_rnn, y_conv, w_out, mod3, norm_g.reshape(1, d), wr, br)


def _row_tile(ref, row):
    return ref.at[pl.ds(pl.multiple_of(row * ROW_TILES, ROW_TILES), ROW_TILES)]


def _slotmap_kernel(dest_ref, zeros_hbm, asg_ref, sem):
    fill = pltpu.make_async_copy(zeros_hbm, asg_ref, sem)
    fill.start()
    fill.wait()

    def body(c, carry):
        for u in range(DMA_UNROLL):
            a = c * DMA_UNROLL + u
            asg_ref[dest_ref[a]] = a
        return carry
    lax.fori_loop(0, dest_ref.shape[0] // DMA_UNROLL, body, 0)


def _slotmap(dest, n_slots):
    return pl.pallas_call(
        _slotmap_kernel,
        in_specs=[pl.BlockSpec(memory_space=pltpu.SMEM), pl.BlockSpec(memory_space=pl.ANY)],
        out_specs=pl.BlockSpec(memory_space=pltpu.SMEM),
        out_shape=jax.ShapeDtypeStruct((n_slots,), jnp.int32),
        scratch_shapes=[pltpu.SemaphoreType.DMA],
        name="slotmap",
    )(dest, jnp.zeros((n_slots,), jnp.int32))


def _expert_kernel(be_ref, ws_ref, ne_ref, nu_ref, asg_hbm, m_ref, wg_hbm, wu_hbm, wd_hbm, yb_ref,
                   xbuf_a, xbuf_b, idx, isems, wbuf_g, wbuf_u, wbuf_d, wsems, wg_s, wu_s, wd_s):
    j = pl.program_id(0)
    n_used = nu_ref[0]
    last = n_used - 1

    def idx_copy(blk, sl):
        return pltpu.make_async_copy(asg_hbm.at[blk], idx.at[sl], isems.at[sl])

    def copy_rows(buf, sl, r0, n):
        for u in range(n):
            tok = lax.shift_right_logical(idx[sl, 0, r0 + u], 1)
            src = pl.multiple_of(tok * PACK_TILES, PACK_TILES)
            buf[pl.ds((r0 + u) * PACK_TILES, PACK_TILES), :] = m_ref[pl.ds(src, PACK_TILES), :]

    def unpack(buf):
        halves = ([], [])
        for s in range(PACK_TILES):
            word = buf[pl.ds(s, MOE_BLK, stride=PACK_TILES), :]
            for k in range(2):
                part = pltpu.unpack_elementwise(word, index=k, packed_dtype=BF16, unpacked_dtype=F32)
                halves[k].append(part.astype(BF16))
        return jnp.concatenate(halves[0] + halves[1], axis=-1)

    n_pieces = (2 * D_EXPERT // MXU_TILE) * (D_MODEL // MXU_TILE) + (
        D_MODEL // MXU_TILE) * (D_EXPERT // MXU_TILE)
    bounds = [(p * MOE_BLK) // n_pieces for p in range(n_pieces + 1)]

    def compute(cur, nxt, nxt_sl):
        pieces = iter(zip(bounds[:-1], bounds[1:]))

        def dot_pieces(a, w_ref, n0):
            acc = None
            for k0 in range(0, a.shape[1], MXU_TILE):
                part = _dot(a[:, k0:k0 + MXU_TILE], w_ref[k0:k0 + MXU_TILE, n0:n0 + MXU_TILE])
                acc = part if acc is None else acc + part
                r0, r1 = next(pieces)
                copy_rows(nxt, nxt_sl, r0, r1 - r0)
            return acc

        xb16 = unpack(cur)
        acts = []
        for n0 in range(0, D_EXPERT, MXU_TILE):
            gate = dot_pieces(xb16, wg_s, n0)
            up = dot_pieces(xb16, wu_s, n0)
            acts.append(((gate * jax.nn.sigmoid(gate)) * up).astype(BF16))
        h = jnp.concatenate(acts, axis=-1)
        for n0 in range(0, D_MODEL, MXU_TILE):
            y = dot_pieces(h, wd_s, n0)
            for s in range(MXU_TILE // LANES):
                yb_ref[pl.ds(n0 // LANES + s, MOE_BLK, stride=ROW_TILES), :] = (
                    y[:, s * LANES:(s + 1) * LANES])

    @pl.when(j >= n_used)
    def _():
        yb_ref[...] = jnp.zeros_like(yb_ref)

    @pl.when(j < n_used)
    def _():
        slot = j % 2
        other = 1 - slot

        @pl.when(j == 0)
        def _():
            idx_copy(0, 0).start()
            idx_copy(0, 0).wait()

            def body(c, carry):
                copy_rows(xbuf_a, 0, c * DMA_UNROLL, DMA_UNROLL)
                return carry
            lax.fori_loop(0, MOE_BLK // DMA_UNROLL, body, 0)
            idx_copy(jnp.minimum(1, last), 1).start()

        e = be_ref[j]
        wslot = ws_ref[j]

        def weight_copies(expert, sl):
            return [pltpu.make_async_copy(src.at[expert], dst.at[sl], wsems.at[sl])
                    for src, dst in ((wg_hbm, wbuf_g), (wu_hbm, wbuf_u), (wd_hbm, wbuf_d))]

        @pl.when(j == 0)
        def _():
            for cp in weight_copies(e, wslot):
                cp.start()

        @pl.when((j == 0) | (e != be_ref[jnp.maximum(j - 1, 0)]))
        def _():
            for cp in weight_copies(e, wslot):
                cp.wait()
            wg_s[...] = wbuf_g[wslot].astype(BF16)
            wu_s[...] = wbuf_u[wslot].astype(BF16)
            wd_s[...] = wbuf_d[wslot].astype(BF16)

            @pl.when(ne_ref[j] >= 0)
            def _():
                for cp in weight_copies(ne_ref[j], 1 - wslot):
                    cp.start()

        idx_copy(0, other).wait()

        @pl.when(slot == 0)
        def _():
            compute(xbuf_a, xbuf_b, 1)

        @pl.when(slot == 1)
        def _():
            compute(xbuf_b, xbuf_a, 0)

        @pl.when(j < last)
        def _():
            idx_copy(jnp.minimum(j + 2, last), slot).start()


def _experts(block_e, weight_slot, next_e, n_used, slot_asg, mt, w_gate, w_up, w_down, n_blocks):
    return pl.pallas_call(
        _expert_kernel,
        grid_spec=pltpu.PrefetchScalarGridSpec(
            num_scalar_prefetch=4,
            grid=(n_blocks,),
            in_specs=[
                pl.BlockSpec(memory_space=pl.ANY),
                pl.BlockSpec(memory_space=pltpu.VMEM),
                pl.BlockSpec(memory_space=pl.ANY),
                pl.BlockSpec(memory_space=pl.ANY),
                pl.BlockSpec(memory_space=pl.ANY),
            ],
            out_specs=pl.BlockSpec((MOE_BLK * ROW_TILES, LANES), lambda j, *_: (j, 0)),
            scratch_shapes=[
                pltpu.VMEM((MOE_BLK * PACK_TILES, LANES), jnp.uint32),
                pltpu.VMEM((MOE_BLK * PACK_TILES, LANES), jnp.uint32),
                pltpu.SMEM((2, 1, MOE_BLK), jnp.int32),
                pltpu.SemaphoreType.DMA((2,)),
                pltpu.VMEM((2, D_MODEL, D_EXPERT), F32),
                pltpu.VMEM((2, D_MODEL, D_EXPERT), F32),
                pltpu.VMEM((2, D_EXPERT, D_MODEL), F32),
                pltpu.SemaphoreType.DMA((2,)),
                pltpu.VMEM((D_MODEL, D_EXPERT), BF16),
                pltpu.VMEM((D_MODEL, D_EXPERT), BF16),
                pltpu.VMEM((D_EXPERT, D_MODEL), BF16),
            ],
        ),
        out_shape=jax.ShapeDtypeStruct((n_blocks * MOE_BLK * ROW_TILES, LANES), F32),
        compiler_params=pltpu.CompilerParams(
            dimension_semantics=("arbitrary",), vmem_limit_bytes=EXPERT_VMEM_LIMIT),
        name="expert",
    )(block_e, weight_slot, next_e, n_used, slot_asg.reshape(n_blocks, 1, MOE_BLK), mt, w_gate, w_up, w_down)


def _combine_kernel(dest_ref, yb_hbm, x1_ref, route_ref, mod_ref, g_ref, o_ref, ybuf, sems, *, tc):
    i = pl.program_id(0)
    slot = i % 2

    def row_copy(d, sl, k, r):
        return pltpu.make_async_copy(_row_tile(yb_hbm, d), _row_tile(ybuf.at[sl, k], r), sems.at[sl])

    def gather(step, sl):
        def issue(c, carry):
            for u in range(DMA_UNROLL):
                r = c * DMA_UNROLL + u
                for k in range(TOP_K):
                    row_copy(dest_ref[TOP_K * (step * tc + r) + k], sl, k, r).start(priority=k)
            return carry
        lax.fori_loop(0, tc // DMA_UNROLL, issue, 0)

    @pl.when(i == 0)
    def _():
        gather(0, 0)

    @pl.when(i + 1 < pl.num_programs(0))
    def _():
        gather(i + 1, 1 - slot)

    def drain(c, carry):
        for u in range(DMA_UNROLL * TOP_K):
            row_copy(0, slot, 0, 0).wait()
        return carry
    lax.fori_loop(0, tc // DMA_UNROLL, drain, 0)

    def rows(k):
        return jnp.concatenate(
            [ybuf[slot, k, pl.ds(s, tc, stride=ROW_TILES), :] for s in range(ROW_TILES)], axis=-1)

    route = route_ref[...]
    y = route[:, 2:3] * rows(0) + route[:, 3:4] * rows(1)
    x2 = x1_ref[...] + mod_ref[0, 5:6, :] * y
    ms = jnp.mean(x2 * x2, axis=-1, keepdims=True)
    o_ref[...] = x2 * lax.rsqrt(ms + NORM_EPS) * g_ref[...]


def _combine(dest, yb, x1_2d, route, mod3, final_g, seq):
    t_all, d = x1_2d.shape
    tc = min(COMBINE_TILE, seq)
    per_seq = seq // tc
    return pl.pallas_call(
        functools.partial(_combine_kernel, tc=tc),
        grid_spec=pltpu.PrefetchScalarGridSpec(
            num_scalar_prefetch=1,
            grid=(t_all // tc,),
            in_specs=[
                pl.BlockSpec(memory_space=pl.ANY),
                pl.BlockSpec((tc, d), lambda i, dest: (i, 0)),
                pl.BlockSpec((tc, LANES), lambda i, dest: (i, 0)),
                pl.BlockSpec((1, 6, d), lambda i, dest: (i // per_seq, 0, 0)),
                pl.BlockSpec((1, d), lambda i, dest: (0, 0)),
            ],
            out_specs=pl.BlockSpec((tc, d), lambda i, dest: (i, 0)),
            scratch_shapes=[
                pltpu.VMEM((2, TOP_K, tc * ROW_TILES, LANES), F32),
                pltpu.SemaphoreType.DMA((2,)),
            ],
        ),
        out_shape=jax.ShapeDtypeStruct((t_all, d), F32),
        compiler_params=pltpu.CompilerParams(
            dimension_semantics=("arbitrary",), vmem_limit_bytes=VMEM_LIMIT),
        name="combine",
    )(dest, yb, x1_2d, route, mod3, final_g.reshape(1, d))


def kernel(x, c, ctx, c_ctx, ada_w, ada_b, norm1_g, norm2_g, w_in, rnn_conv_w, rnn_conv_b, rg_wa, rg_ba,
           rg_wx, rg_bx, rg_lambda, sc_conv_w, w_out, router_group_w, router_group_b, router_exp_w,
           router_exp_b, exp_w_gate, exp_w_up, exp_w_down, final_norm_g):
    bn, seq, d = x.shape
    assert d == D_MODEL and bn < MOD_ROWS and ada_w.shape[0] == 1
    t_all = bn * seq

    cc = jnp.concatenate([c, c_ctx[None], jnp.zeros((MOD_ROWS - bn - 1, d), F32)], axis=0)
    mod3 = _modulation(cc, ada_w[0], ada_b[0]).reshape(MOD_ROWS, 6, d)

    xr, gr, u, bg = _inproj(x, mod3, None, norm1_g[0], w_in[0], latent=True)
    (xr_c,) = _inproj(ctx, mod3, bn, norm1_g[0], w_in[0], latent=False)

    wg, bgate = _gate_weights(rg_wa[0], rg_ba[0], rg_wx[0], rg_bx[0])
    assert D_RNN == D_CONV
    y_rnn, y_conv = _mixers(xr, xr_c, gr, rnn_conv_w[0], rnn_conv_b[0], wg, bgate, rg_lambda[0], u, bg,
                            sc_conv_w[0])

    wr = jnp.zeros((d, LANES), F32)
    wr = wr.at[:, :N_GROUPS].set(router_group_w[0]).at[:, EXPERT_LANE0:EXPERT_LANE0 + N_EXPERTS].set(router_exp_w[0])
    br = jnp.zeros((1, LANES), F32)
    br = br.at[0, :N_GROUPS].set(router_group_b[0]).at[0, EXPERT_LANE0:EXPERT_LANE0 + N_EXPERTS].set(router_exp_b[0])
    x1, mt, route, cnt = _outproj(x, y_rnn, y_conv, w_out[0], mod3, norm2_g[0], wr.astype(BF16), br)

    n_assign = t_all * TOP_K
    n_blocks = (n_assign + N_EXPERTS * (MOE_BLK - 1) + MOE_BLK - 1) // MOE_BLK
    counts = cnt[0, EXPERT_LANE0:EXPERT_LANE0 + N_EXPERTS].astype(jnp.int32)
    pcounts = (counts + MOE_BLK - 1) // MOE_BLK * MOE_BLK
    pends = jnp.cumsum(pcounts)
    pstarts = pends - pcounts
    experts = route[:, 0:TOP_K].astype(jnp.int32)
    ranks = route[:, 4:4 + TOP_K].astype(jnp.int32)
    onehot = experts[:, :, None] == jnp.arange(N_EXPERTS, dtype=jnp.int32)
    dest = (ranks + jnp.sum(jnp.where(onehot, pstarts, 0), axis=-1)).reshape(n_assign)
    n_used = (pends[-1] // MOE_BLK).astype(jnp.int32)
    blk_start = jnp.arange(n_blocks, dtype=jnp.int32) * MOE_BLK
    block_e = jnp.minimum(jnp.sum(blk_start[:, None] >= pends[None, :], axis=1), N_EXPERTS - 1)
    last_e = jnp.max(jnp.where(counts > 0, jnp.arange(N_EXPERTS, dtype=jnp.int32), 0))
    block_e = jnp.where(blk_start < pends[-1], block_e, last_e).astype(jnp.int32)
    eids = jnp.arange(N_EXPERTS, dtype=jnp.int32)
    used = counts > 0
    slot_of_e = (jnp.cumsum(used.astype(jnp.int32)) - 1) % 2
    later = jnp.where(used[None, :] & (eids[None, :] > eids[:, None]), eids[None, :], N_EXPERTS)
    next_of_e = jnp.min(later, axis=1)
    next_of_e = jnp.where(next_of_e == N_EXPERTS, -1, next_of_e)
    is_e = block_e[:, None] == eids[None, :]
    weight_slot = jnp.sum(jnp.where(is_e, slot_of_e[None, :], 0), axis=1).astype(jnp.int32)
    next_e = jnp.sum(jnp.where(is_e, next_of_e[None, :], 0), axis=1).astype(jnp.int32)

    n_slots = n_blocks * MOE_BLK
    slot_asg = _slotmap(dest, n_slots)
    yb = _experts(block_e, weight_slot, next_e, n_used.reshape(1), slot_asg, mt, exp_w_gate[0], exp_w_up[0],
                  exp_w_down[0], n_blocks)
    out = _combine(dest, yb, x1.reshape(t_all, d), route, mod3, final_norm_g, seq)
    return out.reshape(bn, seq, d)
```

```python
import functools

import jax
import jax.numpy as jnp
from jax import lax
from jax.experimental import pallas as pl
from jax.experimental.pallas import tpu as pltpu

F32 = jnp.float32
BF16 = jnp.bfloat16

D_MODEL = 1024
D_RNN = 512
D_CONV = 512
D_CONV_H = D_CONV // 2
RNN_HEADS = 8
RNN_HEAD_DIM = D_RNN // RNN_HEADS
GRID_W = 64
RG_C = 8.0
N_GROUPS = 4
EXPERTS_PER_GROUP = 8
N_EXPERTS = N_GROUPS * EXPERTS_PER_GROUP
TOP_K = 2
D_EXPERT = 512
NORM_EPS = 1e-6
F32_TINY = 1.1754944e-38

LANES = 128
SUBLANES = 8
ROW_TILES = D_MODEL // LANES
PACK_TILES = ROW_TILES // 2
N_LANE_GROUPS = D_RNN // LANES
EXPERT_LANE0 = N_GROUPS

MOD_ROWS = 16
MOD_TN = 768
INPROJ_TILE = 1024
OUTPROJ_TILE = 512
COEFF_ROWS = 512
MOE_BLK = 256
MXU_TILE = 256
SEG_SHIFT = 4
SEG_CHUNK = 1 << SEG_SHIFT
SLOT_BITS = 24
SLOT_MASK = (1 << SLOT_BITS) - 1
DMA_UNROLL = 16
VMEM_LIMIT = 48 * 1024 * 1024
EXPERT_VMEM_LIMIT = 58 * 1024 * 1024


def _dot(a, b):
    return jnp.dot(a, b, preferred_element_type=F32)


def _split_bf16(x):
    hi = x.astype(BF16)
    lo = (x - hi.astype(F32)).astype(BF16)
    return hi, lo


def _mod_kernel(cc_ref, w_ref, b_ref, o_ref):
    s = cc_ref[...]
    s = s * jax.nn.sigmoid(s)
    s_hi, s_lo = _split_bf16(s)
    w_hi, w_lo = _split_bf16(w_ref[...])
    o_ref[...] = _dot(s_hi, w_hi) + _dot(s_lo, w_hi) + _dot(s_hi, w_lo) + b_ref[...]


def _modulation(cc, ada_w, ada_b):
    n = ada_w.shape[1]
    return pl.pallas_call(
        _mod_kernel,
        grid=(n // MOD_TN,),
        in_specs=[
            pl.BlockSpec((MOD_ROWS, D_MODEL), lambda j: (0, 0)),
            pl.BlockSpec((D_MODEL, MOD_TN), lambda j: (0, j)),
            pl.BlockSpec((1, MOD_TN), lambda j: (0, j)),
        ],
        out_specs=pl.BlockSpec((MOD_ROWS, MOD_TN), lambda j: (0, j)),
        out_shape=jax.ShapeDtypeStruct((MOD_ROWS, n), F32),
        compiler_params=pltpu.CompilerParams(vmem_limit_bytes=VMEM_LIMIT),
        name="mod",
    )(cc, ada_w, ada_b.reshape(1, n))


def _norm_mod(x, g, scale, shift):
    ms = jnp.mean(x * x, axis=-1, keepdims=True)
    y = x * lax.rsqrt(ms + NORM_EPS) * g
    return y * (1.0 + scale) + shift


def _inproj_kernel(x_ref, mod_ref, g_ref, w32_ref, *refs, latent):
    out_refs, w_ref = refs[:-1], refs[-1]

    @pl.when((pl.program_id(0) == 0) & (pl.program_id(1) == 0))
    def _():
        w_ref[...] = w32_ref[...].astype(BF16)

    h = _norm_mod(x_ref[0], g_ref[...], mod_ref[0, 1:2, :], mod_ref[0, 0:1, :])
    hb = h.astype(BF16)
    xr = _dot(hb, w_ref[:, 0:D_RNN])
    out_refs[0][0] = xr
    if latent:
        o = D_RNN
        out_refs[1][0] = jax.nn.gelu(_dot(hb, w_ref[:, o:o + D_RNN]), approximate=True)
        o += D_RNN
        v = _dot(hb, w_ref[:, o:o + D_CONV])
        out_refs[3][0] = _dot(hb, w_ref[:, o + D_CONV:o + 2 * D_CONV])
        cg = _dot(hb, w_ref[:, o + 2 * D_CONV:o + 3 * D_CONV])
        out_refs[2][0] = cg * v


def _inproj(x, mod3, mod_row, norm_g, w_in, latent):
    bn, n, d = x.shape
    tm = min(INPROJ_TILE, n)
    n_out = 4 if latent else 1
    width = w_in.shape[1] if latent else D_RNN
    mod_map = (lambda b, i: (b, 0, 0)) if mod_row is None else (lambda b, i: (mod_row, 0, 0))
    return pl.pallas_call(
        functools.partial(_inproj_kernel, latent=latent),
        grid=(bn, n // tm),
        in_specs=[
            pl.BlockSpec((1, tm, d), lambda b, i: (b, i, 0)),
            pl.BlockSpec((1, 6, d), mod_map),
            pl.BlockSpec((1, d), lambda b, i: (0, 0)),
            pl.BlockSpec((d, width), lambda b, i: (0, 0), pipeline_mode=pl.Buffered(1)),
        ],
        out_specs=[pl.BlockSpec((1, tm, D_RNN), lambda b, i: (b, i, 0))] * n_out,
        out_shape=[jax.ShapeDtypeStruct((bn, n, D_RNN), F32)] * n_out,
        scratch_shapes=[pltpu.VMEM((d, width), BF16)],
        compiler_params=pltpu.CompilerParams(
            dimension_semantics=("arbitrary", "arbitrary"), vmem_limit_bytes=VMEM_LIMIT),
        name="inproj_lat" if latent else "inproj_ctx",
    )(x, mod3, norm_g.reshape(1, d), w_in)


def _shift_rows(x, k):
    n = x.shape[0]
    row = lax.broadcasted_iota(jnp.int32, x.shape, 0)
    rolled = pltpu.roll(x, k % n, axis=0)
    valid = (row >= k) if k > 0 else (row < n + k)
    return jnp.where(valid, rolled, 0.0)


def _scan_pitch(chunk):
    pitch = chunk + SUBLANES
    return pitch if (pitch // SUBLANES) % 2 else pitch + SUBLANES


def _rnn_kernel(xr_ref, xrc_ref, gr_ref, cw_ref, cb_ref, wg_ref, bg_ref, lam_ref, u_ref, bgc_ref, scw_ref,
                y_ref, yc_ref,
                xc_s, ap_f, bp_f, ap_b, bp_b, hl_f, al_f, hl_b, al_b, hp_f, hp_b, *, n_lat, n_ctx):
    _gconv_block(u_ref, bgc_ref, scw_ref, yc_ref)

    nl = -lam_ref[...]
    sp = jnp.maximum(nl, 0.0) + jnp.log1p(jnp.exp(-jnp.abs(nl)))
    c1 = (-0.5 * RG_C) * sp
    cw = cw_ref[...]
    bias = cb_ref[...]
    wg = wg_ref[0]
    bg = bg_ref[0]
    dirs = ((ap_f, bp_f, hl_f, al_f, hp_f), (ap_b, bp_b, hl_b, al_b, hp_b))

    def conv_into(x, n):
        xc_s[pl.ds(0, n), :] = (cw[0:1] * _shift_rows(x, 2) + cw[1:2] * _shift_rows(x, 1)
                                + cw[2:3] * x + cw[3:4] * _shift_rows(x, -1)) + bias

    def coefficients(n):
        chunk = n // SUBLANES
        pitch = _scan_pitch(chunk)
        rows = max(chunk, min(n, COEFF_ROWS))
        per = rows // chunk

        def body(i, carry):
            xc = xc_s[pl.ds(pl.multiple_of(i * rows, SUBLANES), rows), :]
            gates = _dot(xc.astype(BF16), wg) + bg
            half_xc = 0.5 * xc
            for d in range(2):
                tr = jnp.tanh(0.5 * gates[:, (2 * d) * LANES:(2 * d + 1) * LANES])
                ti = jnp.tanh(0.5 * gates[:, (2 * d + 1) * LANES:(2 * d + 2) * LANES])
                log_a = c1[d:d + 1] + c1[d:d + 1] * tr
                a = jnp.exp(log_a)
                y = -jnp.tanh(log_a) * (a * a + 1.0)
                b = (y * lax.rsqrt(jnp.maximum(y, F32_TINY))) * (half_xc + half_xc * ti)
                for k in range(per):
                    dst = pl.multiple_of((i * per + k) * pitch, SUBLANES)
                    dirs[d][0][pl.ds(dst, chunk), :] = a[k * chunk:(k + 1) * chunk]
                    dirs[d][1][pl.ds(dst, chunk), :] = b[k * chunk:(k + 1) * chunk]
            return carry

        if n == rows:
            body(0, 0)
        else:
            lax.fori_loop(0, n // rows, body, 0)

    def scan(n, h0_f, h0_b, keep):
        chunk = n // SUBLANES
        pitch = _scan_pitch(chunk)

        def steps(jo, carry):
            h_f, a_f, h_b, a_b = carry
            for u in range(SUBLANES):
                j = jo * SUBLANES + u
                av = ap_f[pl.ds(j, SUBLANES, stride=pitch), :]
                h_f = av * h_f + bp_f[pl.ds(j, SUBLANES, stride=pitch), :]
                a_f = av * a_f
                jb = chunk - 1 - j
                av = ap_b[pl.ds(jb, SUBLANES, stride=pitch), :]
                h_b = av * h_b + bp_b[pl.ds(jb, SUBLANES, stride=pitch), :]
                a_b = av * a_b
                if keep:
                    o = pl.multiple_of(j * SUBLANES, SUBLANES)
                    hl_f[pl.ds(o, SUBLANES), :] = h_f
                    al_f[pl.ds(o, SUBLANES), :] = a_f
                    hl_b[pl.ds(o, SUBLANES), :] = h_b
                    al_b[pl.ds(o, SUBLANES), :] = a_b
            return h_f, a_f, h_b, a_b

        zeros = jnp.zeros((SUBLANES, LANES), F32)
        ones = jnp.ones((SUBLANES, LANES), F32)
        h_f, a_f, h_b, a_b = lax.fori_loop(0, chunk // SUBLANES, steps, (zeros, ones, zeros, ones))

        in_f = [h0_f]
        for c in range(SUBLANES):
            in_f.append(a_f[c:c + 1] * in_f[c] + h_f[c:c + 1])
        in_b = [h0_b]
        for c in range(SUBLANES - 1, -1, -1):
            in_b.append(a_b[c:c + 1] * in_b[-1] + h_b[c:c + 1])
        if keep:
            hin_f = jnp.concatenate(in_f[:SUBLANES], axis=0)
            hin_b = jnp.concatenate(in_b[SUBLANES - 1::-1], axis=0)

            def fix(jo, carry):
                for u in range(SUBLANES):
                    j = jo * SUBLANES + u
                    o = pl.multiple_of(j * SUBLANES, SUBLANES)
                    hp_f[pl.ds(j, SUBLANES, stride=pitch), :] = (
                        hl_f[pl.ds(o, SUBLANES), :] + al_f[pl.ds(o, SUBLANES), :] * hin_f)
                    hp_b[pl.ds(chunk - 1 - j, SUBLANES, stride=pitch), :] = (
                        hl_b[pl.ds(o, SUBLANES), :] + al_b[pl.ds(o, SUBLANES), :] * hin_b)
                return carry
            lax.fori_loop(0, chunk // SUBLANES, fix, 0)
        return in_f[SUBLANES], in_b[SUBLANES]

    zero = jnp.zeros((1, LANES), F32)
    conv_into(xrc_ref[0], n_ctx)
    coefficients(n_ctx)
    h0_f, h0_b = scan(n_ctx, zero, zero, keep=False)

    conv_into(xr_ref[0], n_lat)
    coefficients(n_lat)
    scan(n_lat, h0_f, h0_b, keep=True)

    chunk = n_lat // SUBLANES
    pitch = _scan_pitch(chunk)

    def emit(c, carry):
        src = pl.multiple_of(c * chunk, 2 * SUBLANES)
        dst = pl.multiple_of(c * pitch, SUBLANES)
        hsum = hp_f[pl.ds(dst, chunk), :] + hp_b[pl.ds(dst, chunk), :]
        y = gr_ref[0, pl.ds(src, chunk), :] * hsum
        y_ref[0, pl.ds(src, chunk), :] = y.astype(y_ref.dtype)
        return carry
    lax.fori_loop(0, SUBLANES, emit, 0)


def _mixers(xr, xr_c, gr, conv_w, conv_b, wg, bgate, lam, u, bg, sc_w):
    bn, n, _ = xr.shape
    n_ctx = xr_c.shape[1]
    assert n % (SUBLANES * SUBLANES) == 0 and n_ctx % (SUBLANES * SUBLANES) == 0 and n_ctx <= n
    pitched = SUBLANES * _scan_pitch(n // SUBLANES)
    seq_spec = pl.BlockSpec((1, n, LANES), lambda b, p: (b, 0, p))
    return pl.pallas_call(
        functools.partial(_rnn_kernel, n_lat=n, n_ctx=n_ctx),
        grid=(bn, N_LANE_GROUPS),
        in_specs=[
            seq_spec,
            pl.BlockSpec((1, n_ctx, LANES), lambda b, p: (b, 0, p)),
            seq_spec,
            pl.BlockSpec((4, LANES), lambda b, p: (0, p)),
            pl.BlockSpec((1, LANES), lambda b, p: (0, p)),
            pl.BlockSpec((1, LANES, 4 * LANES), lambda b, p: (p, 0, 0)),
            pl.BlockSpec((1, 1, 4 * LANES), lambda b, p: (p, 0, 0)),
            pl.BlockSpec((2, LANES), lambda b, p: (0, p)),
            seq_spec,
            seq_spec,
            pl.BlockSpec((3, LANES), lambda b, p: (0, p)),
        ],
        out_specs=[seq_spec, seq_spec],
        out_shape=[jax.ShapeDtypeStruct((bn, n, D_RNN), BF16)] * 2,
        scratch_shapes=[pltpu.VMEM((n, LANES), F32)]
        + [pltpu.VMEM((pitched, LANES), F32)] * 4
        + [pltpu.VMEM((n, LANES), F32)] * 4
        + [pltpu.VMEM((pitched, LANES), F32)] * 2,
        compiler_params=pltpu.CompilerParams(vmem_limit_bytes=VMEM_LIMIT),
        name="mixers",
    )(xr, xr_c, gr, conv_w, conv_b.reshape(1, D_RNN), wg, bgate, lam, u, bg, sc_w)


def _gate_weights(rg_wa, rg_ba, rg_wx, rg_bx):
    eye = jnp.eye(2, dtype=F32)
    blocks, biases = [], []
    for d in range(2):
        for w, bvec in ((rg_wa[d], rg_ba[d]), (rg_wx[d], rg_bx[d])):
            w4 = w.reshape(N_LANE_GROUPS, 2, RNN_HEAD_DIM, RNN_HEAD_DIM)
            bd = jnp.einsum("paij,ac->paicj", w4, eye).reshape(N_LANE_GROUPS, LANES, LANES)
            blocks.append(bd)
            biases.append(bvec.reshape(N_LANE_GROUPS, 1, LANES))
    return jnp.concatenate(blocks, axis=-1).astype(BF16), jnp.concatenate(biases, axis=-1)


def _gconv_block(u_ref, bg_ref, w_ref, y_ref):
    p = pl.program_id(1)
    u = u_ref[0]
    w = w_ref[...]

    @pl.when(p < D_CONV_H // LANES)
    def _():
        col = lax.broadcasted_iota(jnp.int32, u.shape, 0) % GRID_W
        left = jnp.where(col > 0, _shift_rows(u, 1), 0.0)
        right = jnp.where(col < GRID_W - 1, _shift_rows(u, -1), 0.0)
        y_ref[0] = (bg_ref[0] * (w[0:1] * left + w[1:2] * u + w[2:3] * right)).astype(y_ref.dtype)

    @pl.when(p >= D_CONV_H // LANES)
    def _():
        y_ref[0] = (bg_ref[0] * (w[0:1] * _shift_rows(u, GRID_W) + w[1:2] * u
                                 + w[2:3] * _shift_rows(u, -GRID_W))).astype(y_ref.dtype)


def _lane_max(x, mask):
    return jnp.max(jnp.where(mask, x, -jnp.inf), axis=-1, keepdims=True)


def _first_lane(cond, lane):
    return jnp.min(jnp.where(cond, lane, float(LANES)), axis=-1, keepdims=True)


def _outproj_kernel(x_ref, yr_ref, yc_ref, w32_ref, mod_ref, g_ref, wr_ref, br_ref,
                    x1_ref, mt_ref, route_ref, cnt_ref, bef_ref, carry, w_ref, m_s, *, tm):
    s = pl.program_id(0)

    @pl.when(s == 0)
    def _():
        carry[...] = jnp.zeros_like(carry)
        w_ref[...] = w32_ref[...].astype(BF16)
        m_s[...] = jnp.zeros_like(m_s)

    logits = _dot(m_s[...], wr_ref[...]) + br_ref[...]
    lane_i = lax.broadcasted_iota(jnp.int32, logits.shape, 1)
    lane = lane_i.astype(F32)
    is_grp = lane_i < N_GROUPS
    g_max = _lane_max(logits, is_grp)
    grp = _first_lane(is_grp & (logits == g_max), lane)
    p_g = 1.0 / jnp.sum(jnp.where(is_grp, jnp.exp(logits - g_max), 0.0), axis=-1, keepdims=True)
    lo_lane = EXPERT_LANE0 + grp * EXPERTS_PER_GROUP
    in_grp = (lane >= lo_lane) & (lane < lo_lane + EXPERTS_PER_GROUP)
    l1 = _lane_max(logits, in_grp)
    i1 = _first_lane(in_grp & (logits == l1), lane)
    rest = in_grp & (lane != i1)
    l2 = _lane_max(logits, rest)
    i2 = _first_lane(rest & (logits == l2), lane)
    r21 = jnp.exp(l2 - l1)
    gate1 = p_g / (1.0 + r21)
    gate2 = gate1 * r21

    oh1 = jnp.where(lane == i1, 1.0, 0.0)
    oh2 = jnp.where(lane == i2, 1.0, 0.0)
    both = (oh1 + oh2).astype(BF16)
    ti = lax.broadcasted_iota(jnp.int32, (tm, tm), 0)
    tj = lax.broadcasted_iota(jnp.int32, (tm, tm), 1)
    tri = jnp.where(tj < ti, 1.0, 0.0).astype(BF16)
    counts = carry[...]
    bef_ref[0] = counts
    before = _dot(tri, both) + counts
    rank1 = jnp.sum(oh1 * before, axis=-1, keepdims=True)
    rank2 = jnp.sum(oh2 * before, axis=-1, keepdims=True)
    out = jnp.zeros(logits.shape, F32)
    for k, val in enumerate((i1 - EXPERT_LANE0, i2 - EXPERT_LANE0, gate1, gate2, rank1, rank2)):
        out = jnp.where(lane_i == k, val, out)
    route_ref[...] = out
    real = jnp.where(s > 0, 1.0, 0.0)
    total = counts + real * jnp.sum(oh1 + oh2, axis=0, keepdims=True)
    carry[...] = total
    cnt_ref[...] = total

    mix = _dot(yr_ref[0], w_ref[0:D_RNN, :]) + _dot(yc_ref[0], w_ref[D_RNN:, :])
    x1 = x_ref[0] + mod_ref[0, 2:3, :] * mix
    x1_ref[0] = x1
    m_new = _norm_mod(x1, g_ref[...], mod_ref[0, 4:5, :], mod_ref[0, 3:4, :])
    m_s[...] = m_new.astype(BF16)
    half = D_MODEL // 2
    packed = pltpu.pack_elementwise([m_new[:, :half], m_new[:, half:]], packed_dtype=BF16)
    for q in range(PACK_TILES):
        mt_ref[pl.ds(q, tm, stride=PACK_TILES), :] = packed[:, q * LANES:(q + 1) * LANES]


def _outproj(x, y_rnn, y_conv, w_out, mod3, norm_g, wr, br):
    bn, n, d = x.shape
    tm = min(OUTPROJ_TILE, n)
    nt = n // tm
    n_tiles = bn * nt
    t_all = bn * n

    def cur(s):
        return jnp.minimum(s, n_tiles - 1)

    def prev(s):
        return jnp.maximum(s - 1, 0)

    def seq_map(s):
        return (cur(s) // nt, cur(s) % nt, 0)

    const = lambda s: (0, 0)
    return pl.pallas_call(
        functools.partial(_outproj_kernel, tm=tm),
        grid=(n_tiles + 1,),
        in_specs=[
            pl.BlockSpec((1, tm, d), seq_map),
            pl.BlockSpec((1, tm, D_RNN), seq_map),
            pl.BlockSpec((1, tm, D_CONV), seq_map),
            pl.BlockSpec((D_RNN + D_CONV, d), const, pipeline_mode=pl.Buffered(1)),
            pl.BlockSpec((1, 6, d), lambda s: (cur(s) // nt, 0, 0)),
            pl.BlockSpec((1, d), const),
            pl.BlockSpec((d, LANES), const),
            pl.BlockSpec((1, LANES), const),
        ],
        out_specs=[
            pl.BlockSpec((1, tm, d), seq_map),
            pl.BlockSpec((tm * PACK_TILES, LANES), lambda s: (cur(s), 0)),
            pl.BlockSpec((tm, LANES), lambda s: (prev(s), 0)),
            pl.BlockSpec((1, LANES), const),
            pl.BlockSpec((1, 1, LANES), lambda s: (prev(s), 0, 0)),
        ],
        out_shape=[
            jax.ShapeDtypeStruct((bn, n, d), F32),
            jax.ShapeDtypeStruct((t_all * PACK_TILES, LANES), jnp.uint32),
            jax.ShapeDtypeStruct((t_all, LANES), F32),
            jax.ShapeDtypeStruct((1, LANES), F32),
            jax.ShapeDtypeStruct((n_tiles, 1, LANES), F32),
        ],
        scratch_shapes=[pltpu.VMEM((1, LANES), F32), pltpu.VMEM((D_RNN + D_CONV, d), BF16),
                        pltpu.VMEM((tm, d), BF16)],
        compiler_params=pltpu.CompilerParams(
            dimension_semantics=("arbitrary",), vmem_limit_bytes=VMEM_LIMIT),
        name="outproj",
    )(x, y_rnn, y_conv, w_out, mod3, norm_g.reshape(1, d), wr, br)


def _row_tile(ref, row):
    return ref.at[pl.ds(pl.multiple_of(row * ROW_TILES, ROW_TILES), ROW_TILES)]


def _slotmap_kernel(dest_ref, zeros_hbm, asg_ref, sem):
    fill = pltpu.make_async_copy(zeros_hbm, asg_ref, sem)
    fill.start()
    fill.wait()

    def body(c, carry):
        for u in range(DMA_UNROLL):
            a = c * DMA_UNROLL + u
            asg_ref[dest_ref[a]] = a
        return carry
    lax.fori_loop(0, dest_ref.shape[0] // DMA_UNROLL, body, 0)


def _slotmap(dest, n_slots):
    return pl.pallas_call(
        _slotmap_kernel,
        in_specs=[pl.BlockSpec(memory_space=pltpu.SMEM), pl.BlockSpec(memory_space=pl.ANY)],
        out_specs=pl.BlockSpec(memory_space=pltpu.SMEM),
        out_shape=jax.ShapeDtypeStruct((n_slots,), jnp.int32),
        scratch_shapes=[pltpu.SemaphoreType.DMA],
        name="slotmap",
    )(dest, jnp.zeros((n_slots,), jnp.int32))


def _expert_kernel(be_ref, ws_ref, ne_ref, nu_ref, asg_hbm, m_ref, wg_hbm, wu_hbm, wd_hbm, yb_ref,
                   xbuf_a, xbuf_b, idx, isems, wbuf_g, wbuf_u, wbuf_d, wsems, wg_s, wu_s, wd_s):
    j = pl.program_id(0)
    n_used = nu_ref[0]
    last = n_used - 1

    def idx_copy(blk, sl):
        return pltpu.make_async_copy(asg_hbm.at[blk], idx.at[sl], isems.at[sl])

    def copy_rows(buf, sl, r0, n):
        for u in range(n):
            tok = lax.shift_right_logical(idx[sl, 0, r0 + u], 1)
            src = pl.multiple_of(tok * PACK_TILES, PACK_TILES)
            buf[pl.ds((r0 + u) * PACK_TILES, PACK_TILES), :] = m_ref[pl.ds(src, PACK_TILES), :]

    def unpack(buf):
        halves = ([], [])
        for s in range(PACK_TILES):
            word = buf[pl.ds(s, MOE_BLK, stride=PACK_TILES), :]
            for k in range(2):
                part = pltpu.unpack_elementwise(word, index=k, packed_dtype=BF16, unpacked_dtype=F32)
                halves[k].append(part.astype(BF16))
        return jnp.concatenate(halves[0] + halves[1], axis=-1)

    n_pieces = (2 * D_EXPERT // MXU_TILE) * (D_MODEL // MXU_TILE) + (
        D_MODEL // MXU_TILE) * (D_EXPERT // MXU_TILE)
    bounds = [(p * MOE_BLK) // n_pieces for p in range(n_pieces + 1)]

    def compute(cur, nxt, nxt_sl):
        pieces = iter(zip(bounds[:-1], bounds[1:]))

        def dot_pieces(a, w_ref, n0):
            acc = None
            for k0 in range(0, a.shape[1], MXU_TILE):
                part = _dot(a[:, k0:k0 + MXU_TILE], w_ref[k0:k0 + MXU_TILE, n0:n0 + MXU_TILE])
                acc = part if acc is None else acc + part
                r0, r1 = next(pieces)
                copy_rows(nxt, nxt_sl, r0, r1 - r0)
            return acc

        xb16 = unpack(cur)
        acts = []
        for n0 in range(0, D_EXPERT, MXU_TILE):
            gate = dot_pieces(xb16, wg_s, n0)
            up = dot_pieces(xb16, wu_s, n0)
            acts.append(((gate * jax.nn.sigmoid(gate)) * up).astype(BF16))
        h = jnp.concatenate(acts, axis=-1)
        for n0 in range(0, D_MODEL, MXU_TILE):
            y = dot_pieces(h, wd_s, n0)
            for s in range(MXU_TILE // LANES):
                yb_ref[pl.ds(n0 // LANES + s, MOE_BLK, stride=ROW_TILES), :] = (
                    y[:, s * LANES:(s + 1) * LANES])

    @pl.when(j >= n_used)
    def _():
        yb_ref[...] = jnp.zeros_like(yb_ref)

    @pl.when(j < n_used)
    def _():
        slot = j % 2
        other = 1 - slot

        @pl.when(j == 0)
        def _():
            idx_copy(0, 0).start()
            idx_copy(0, 0).wait()

            def body(c, carry):
                copy_rows(xbuf_a, 0, c * DMA_UNROLL, DMA_UNROLL)
                return carry
            lax.fori_loop(0, MOE_BLK // DMA_UNROLL, body, 0)
            idx_copy(jnp.minimum(1, last), 1).start()

        e = be_ref[j]
        wslot = ws_ref[j]

        def weight_copies(expert, sl):
            return [pltpu.make_async_copy(src.at[expert], dst.at[sl], wsems.at[sl])
                    for src, dst in ((wg_hbm, wbuf_g), (wu_hbm, wbuf_u), (wd_hbm, wbuf_d))]

        @pl.when(j == 0)
        def _():
            for cp in weight_copies(e, wslot):
                cp.start()

        @pl.when((j == 0) | (e != be_ref[jnp.maximum(j - 1, 0)]))
        def _():
            for cp in weight_copies(e, wslot):
                cp.wait()
            wg_s[...] = wbuf_g[wslot].astype(BF16)
            wu_s[...] = wbuf_u[wslot].astype(BF16)
            wd_s[...] = wbuf_d[wslot].astype(BF16)

            @pl.when(ne_ref[j] >= 0)
            def _():
                for cp in weight_copies(ne_ref[j], 1 - wslot):
                    cp.start()

        idx_copy(0, other).wait()

        @pl.when(slot == 0)
        def _():
            compute(xbuf_a, xbuf_b, 1)

        @pl.when(slot == 1)
        def _():
            compute(xbuf_b, xbuf_a, 0)

        @pl.when(j < last)
        def _():
            idx_copy(jnp.minimum(j + 2, last), slot).start()


def _experts(block_e, weight_slot, next_e, n_used, slot_asg, mt, w_gate, w_up, w_down, n_blocks):
    return pl.pallas_call(
        _expert_kernel,
        grid_spec=pltpu.PrefetchScalarGridSpec(
            num_scalar_prefetch=4,
            grid=(n_blocks,),
            in_specs=[
                pl.BlockSpec(memory_space=pl.ANY),
                pl.BlockSpec(memory_space=pltpu.VMEM),
                pl.BlockSpec(memory_space=pl.ANY),
                pl.BlockSpec(memory_space=pl.ANY),
                pl.BlockSpec(memory_space=pl.ANY),
            ],
            out_specs=pl.BlockSpec((MOE_BLK * ROW_TILES, LANES), lambda j, *_: (j, 0)),
            scratch_shapes=[
                pltpu.VMEM((MOE_BLK * PACK_TILES, LANES), jnp.uint32),
                pltpu.VMEM((MOE_BLK * PACK_TILES, LANES), jnp.uint32),
                pltpu.SMEM((2, 1, MOE_BLK), jnp.int32),
                pltpu.SemaphoreType.DMA((2,)),
                pltpu.VMEM((2, D_MODEL, D_EXPERT), F32),
                pltpu.VMEM((2, D_MODEL, D_EXPERT), F32),
                pltpu.VMEM((2, D_EXPERT, D_MODEL), F32),
                pltpu.SemaphoreType.DMA((2,)),
                pltpu.VMEM((D_MODEL, D_EXPERT), BF16),
                pltpu.VMEM((D_MODEL, D_EXPERT), BF16),
                pltpu.VMEM((D_EXPERT, D_MODEL), BF16),
            ],
        ),
        out_shape=jax.ShapeDtypeStruct((n_blocks * MOE_BLK * ROW_TILES, LANES), F32),
        compiler_params=pltpu.CompilerParams(
            dimension_semantics=("arbitrary",), vmem_limit_bytes=EXPERT_VMEM_LIMIT),
        name="expert",
    )(block_e, weight_slot, next_e, n_used, slot_asg.reshape(n_blocks, 1, MOE_BLK), mt, w_gate, w_up, w_down)


def _combine_kernel(denc_ref, segs_ref, segc_ref, yb_hbm, x1_ref, route_ref, mod_ref, g_ref, o_ref,
                    stage, ybuf, off, nchunk, sems, *, tc):
    i = pl.program_id(0)
    slot = i % 2
    chunk_rows = SEG_CHUNK * ROW_TILES

    def chunk_copy(src_row, dst_row, sl):
        src = yb_hbm.at[pl.ds(pl.multiple_of(src_row * ROW_TILES, ROW_TILES), chunk_rows)]
        dst = stage.at[sl, pl.ds(pl.multiple_of(dst_row * ROW_TILES, chunk_rows), chunk_rows)]
        return pltpu.make_async_copy(src, dst, sems.at[sl])

    def fetch(tile, sl):
        base = jnp.int32(0)
        for e in range(N_EXPERTS):
            first = segs_ref[tile * N_EXPERTS + e]
            n_ch = lax.shift_right_logical(segc_ref[tile * N_EXPERTS + e] + (SEG_CHUNK - 1), SEG_SHIFT)
            off[sl, e] = base - first

            def body(k, carry, first=first, base=base):
                chunk_copy(first + k * SEG_CHUNK, base + k * SEG_CHUNK, sl).start()
                return carry
            lax.fori_loop(0, n_ch, body, 0)
            base = base + n_ch * SEG_CHUNK
        nchunk[sl] = lax.shift_right_logical(base, SEG_SHIFT)

    @pl.when(i == 0)
    def _():
        fetch(0, 0)

    @pl.when(i + 1 < pl.num_programs(0))
    def _():
        fetch(i + 1, 1 - slot)

    lax.fori_loop(0, nchunk[slot], lambda k, c: (chunk_copy(0, 0, slot).wait(), c)[1], 0)

    def to_token_order(c, carry):
        for u in range(SUBLANES):
            r = c * SUBLANES + u
            for k in range(TOP_K):
                v = denc_ref[TOP_K * (i * tc + r) + k]
                row = (v & SLOT_MASK) + off[slot, lax.shift_right_logical(v, SLOT_BITS)]
                ybuf[k, pl.ds(r * ROW_TILES, ROW_TILES), :] = stage[
                    slot, pl.ds(pl.multiple_of(row * ROW_TILES, ROW_TILES), ROW_TILES), :]
        return carry
    lax.fori_loop(0, tc // SUBLANES, to_token_order, 0)

    def rows(k):
        return jnp.concatenate(
            [ybuf[k, pl.ds(s, tc, stride=ROW_TILES), :] for s in range(ROW_TILES)], axis=-1)

    route = route_ref[...]
    y = route[:, 2:3] * rows(0) + route[:, 3:4] * rows(1)
    x2 = x1_ref[...] + mod_ref[0, 5:6, :] * y
    ms = jnp.mean(x2 * x2, axis=-1, keepdims=True)
    o_ref[...] = x2 * lax.rsqrt(ms + NORM_EPS) * g_ref[...]


def _combine(denc, seg_start, seg_cnt, yb, x1_2d, route, mod3, final_g, seq, tc):
    t_all, d = x1_2d.shape
    per_seq = seq // tc
    stage_rows = (TOP_K * tc + N_EXPERTS * SEG_CHUNK) * ROW_TILES
    return pl.pallas_call(
        functools.partial(_combine_kernel, tc=tc),
        grid_spec=pltpu.PrefetchScalarGridSpec(
            num_scalar_prefetch=3,
            grid=(t_all // tc,),
            in_specs=[
                pl.BlockSpec(memory_space=pl.ANY),
                pl.BlockSpec((tc, d), lambda i, *_: (i, 0)),
                pl.BlockSpec((tc, LANES), lambda i, *_: (i, 0)),
                pl.BlockSpec((1, 6, d), lambda i, *_: (i // per_seq, 0, 0)),
                pl.BlockSpec((1, d), lambda i, *_: (0, 0)),
            ],
            out_specs=pl.BlockSpec((tc, d), lambda i, *_: (i, 0)),
            scratch_shapes=[
                pltpu.VMEM((2, stage_rows, LANES), F32),
                pltpu.VMEM((TOP_K, tc * ROW_TILES, LANES), F32),
                pltpu.SMEM((2, N_EXPERTS), jnp.int32),
                pltpu.SMEM((2,), jnp.int32),
                pltpu.SemaphoreType.DMA((2,)),
            ],
        ),
        out_shape=jax.ShapeDtypeStruct((t_all, d), F32),
        compiler_params=pltpu.CompilerParams(
            dimension_semantics=("arbitrary",), vmem_limit_bytes=VMEM_LIMIT),
        name="combine",
    )(denc, seg_start, seg_cnt, yb, x1_2d, route, mod3, final_g.reshape(1, d))


def kernel(x, c, ctx, c_ctx, ada_w, ada_b, norm1_g, norm2_g, w_in, rnn_conv_w, rnn_conv_b, rg_wa, rg_ba,
           rg_wx, rg_bx, rg_lambda, sc_conv_w, w_out, router_group_w, router_group_b, router_exp_w,
           router_exp_b, exp_w_gate, exp_w_up, exp_w_down, final_norm_g):
    bn, seq, d = x.shape
    assert d == D_MODEL and bn < MOD_ROWS and ada_w.shape[0] == 1
    t_all = bn * seq

    cc = jnp.concatenate([c, c_ctx[None], jnp.zeros((MOD_ROWS - bn - 1, d), F32)], axis=0)
    mod3 = _modulation(cc, ada_w[0], ada_b[0]).reshape(MOD_ROWS, 6, d)

    xr, gr, u, bg = _inproj(x, mod3, None, norm1_g[0], w_in[0], latent=True)
    (xr_c,) = _inproj(ctx, mod3, bn, norm1_g[0], w_in[0], latent=False)

    wg, bgate = _gate_weights(rg_wa[0], rg_ba[0], rg_wx[0], rg_bx[0])
    assert D_RNN == D_CONV
    y_rnn, y_conv = _mixers(xr, xr_c, gr, rnn_conv_w[0], rnn_conv_b[0], wg, bgate, rg_lambda[0], u, bg,
                            sc_conv_w[0])

    wr = jnp.zeros((d, LANES), F32)
    wr = wr.at[:, :N_GROUPS].set(router_group_w[0]).at[:, EXPERT_LANE0:EXPERT_LANE0 + N_EXPERTS].set(router_exp_w[0])
    br = jnp.zeros((1, LANES), F32)
    br = br.at[0, :N_GROUPS].set(router_group_b[0]).at[0, EXPERT_LANE0:EXPERT_LANE0 + N_EXPERTS].set(router_exp_b[0])
    x1, mt, route, cnt, bef = _outproj(x, y_rnn, y_conv, w_out[0], mod3, norm2_g[0], wr.astype(BF16), br)

    n_assign = t_all * TOP_K
    n_blocks = (n_assign + N_EXPERTS * (MOE_BLK - 1) + MOE_BLK - 1) // MOE_BLK + 1
    counts = cnt[0, EXPERT_LANE0:EXPERT_LANE0 + N_EXPERTS].astype(jnp.int32)
    pcounts = (counts + MOE_BLK - 1) // MOE_BLK * MOE_BLK
    pends = jnp.cumsum(pcounts)
    pstarts = pends - pcounts
    experts = route[:, 0:TOP_K].astype(jnp.int32)
    ranks = route[:, 4:4 + TOP_K].astype(jnp.int32)
    onehot = experts[:, :, None] == jnp.arange(N_EXPERTS, dtype=jnp.int32)
    dest = (ranks + jnp.sum(jnp.where(onehot, pstarts, 0), axis=-1)).reshape(n_assign)
    n_used = (pends[-1] // MOE_BLK).astype(jnp.int32)
    blk_start = jnp.arange(n_blocks, dtype=jnp.int32) * MOE_BLK
    block_e = jnp.minimum(jnp.sum(blk_start[:, None] >= pends[None, :], axis=1), N_EXPERTS - 1)
    last_e = jnp.max(jnp.where(counts > 0, jnp.arange(N_EXPERTS, dtype=jnp.int32), 0))
    block_e = jnp.where(blk_start < pends[-1], block_e, last_e).astype(jnp.int32)
    eids = jnp.arange(N_EXPERTS, dtype=jnp.int32)
    used = counts > 0
    slot_of_e = (jnp.cumsum(used.astype(jnp.int32)) - 1) % 2
    later = jnp.where(used[None, :] & (eids[None, :] > eids[:, None]), eids[None, :], N_EXPERTS)
    next_of_e = jnp.min(later, axis=1)
    next_of_e = jnp.where(next_of_e == N_EXPERTS, -1, next_of_e)
    is_e = block_e[:, None] == eids[None, :]
    weight_slot = jnp.sum(jnp.where(is_e, slot_of_e[None, :], 0), axis=1).astype(jnp.int32)
    next_e = jnp.sum(jnp.where(is_e, next_of_e[None, :], 0), axis=1).astype(jnp.int32)

    n_slots = n_blocks * MOE_BLK
    slot_asg = _slotmap(dest, n_slots)
    yb = _experts(block_e, weight_slot, next_e, n_used.reshape(1), slot_asg, mt, exp_w_gate[0], exp_w_up[0],
                  exp_w_down[0], n_blocks)
    tc = min(OUTPROJ_TILE, seq)
    assert SEG_CHUNK <= MOE_BLK and n_slots < SLOT_MASK
    before = bef[:, 0, EXPERT_LANE0:EXPERT_LANE0 + N_EXPERTS].astype(jnp.int32)
    after = jnp.concatenate([before[1:], counts[None, :]], axis=0)
    seg_start = (pstarts[None, :] + before).reshape(-1).astype(jnp.int32)
    seg_cnt = (after - before).reshape(-1).astype(jnp.int32)
    denc = dest | (experts.reshape(n_assign) << SLOT_BITS)
    out = _combine(denc, seg_start, seg_cnt, yb, x1.reshape(t_all, d), route, mod3, final_norm_g, seq, tc)
    return out.reshape(bn, seq, d)
```

```python
import functools

import jax
import jax.numpy as jnp
from jax import lax
from jax.experimental import pallas as pl
from jax.experimental.pallas import tpu as pltpu

F32 = jnp.float32
BF16 = jnp.bfloat16

D_MODEL = 1024
D_RNN = 512
D_CONV = 512
D_CONV_H = D_CONV // 2
RNN_HEADS = 8
RNN_HEAD_DIM = D_RNN // RNN_HEADS
GRID_W = 64
RG_C = 8.0
N_GROUPS = 4
EXPERTS_PER_GROUP = 8
N_EXPERTS = N_GROUPS * EXPERTS_PER_GROUP
TOP_K = 2
D_EXPERT = 512
NORM_EPS = 1e-6
F32_TINY = 1.1754944e-38

LANES = 128
SUBLANES = 8
ROW_TILES = D_MODEL // LANES
PACK_TILES = ROW_TILES // 2
N_LANE_GROUPS = D_RNN // LANES
EXPERT_LANE0 = N_GROUPS

MOD_ROWS = 16
MOD_TN = 768
INPROJ_TILE = 1024
OUTPROJ_TILE = 512
COEFF_ROWS = 512
MOE_BLK = 256
MXU_TILE = 256
SEG_SHIFT = 4
SEG_CHUNK = 1 << SEG_SHIFT
DMA_UNROLL = 16
VMEM_LIMIT = 48 * 1024 * 1024
EXPERT_VMEM_LIMIT = 58 * 1024 * 1024


def _dot(a, b):
    return jnp.dot(a, b, preferred_element_type=F32)


def _split_bf16(x):
    hi = x.astype(BF16)
    lo = (x - hi.astype(F32)).astype(BF16)
    return hi, lo


def _mod_kernel(cc_ref, w_ref, b_ref, o_ref):
    s = cc_ref[...]
    s = s * jax.nn.sigmoid(s)
    s_hi, s_lo = _split_bf16(s)
    w_hi, w_lo = _split_bf16(w_ref[...])
    o_ref[...] = _dot(s_hi, w_hi) + _dot(s_lo, w_hi) + _dot(s_hi, w_lo) + b_ref[...]


def _modulation(cc, ada_w, ada_b):
    n = ada_w.shape[1]
    return pl.pallas_call(
        _mod_kernel,
        grid=(n // MOD_TN,),
        in_specs=[
            pl.BlockSpec((MOD_ROWS, D_MODEL), lambda j: (0, 0)),
            pl.BlockSpec((D_MODEL, MOD_TN), lambda j: (0, j)),
            pl.BlockSpec((1, MOD_TN), lambda j: (0, j)),
        ],
        out_specs=pl.BlockSpec((MOD_ROWS, MOD_TN), lambda j: (0, j)),
        out_shape=jax.ShapeDtypeStruct((MOD_ROWS, n), F32),
        compiler_params=pltpu.CompilerParams(vmem_limit_bytes=VMEM_LIMIT),
        name="mod",
    )(cc, ada_w, ada_b.reshape(1, n))


def _norm_mod(x, g, scale, shift):
    ms = jnp.mean(x * x, axis=-1, keepdims=True)
    y = x * lax.rsqrt(ms + NORM_EPS) * g
    return y * (1.0 + scale) + shift


def _inproj_kernel(x_ref, mod_ref, g_ref, w32_ref, *refs, latent):
    out_refs, w_ref = refs[:-1], refs[-1]

    @pl.when((pl.program_id(0) == 0) & (pl.program_id(1) == 0))
    def _():
        w_ref[...] = w32_ref[...].astype(BF16)

    h = _norm_mod(x_ref[0], g_ref[...], mod_ref[0, 1:2, :], mod_ref[0, 0:1, :])
    hb = h.astype(BF16)
    xr = _dot(hb, w_ref[:, 0:D_RNN])
    out_refs[0][0] = xr
    if latent:
        o = D_RNN
        out_refs[1][0] = jax.nn.gelu(_dot(hb, w_ref[:, o:o + D_RNN]), approximate=True)
        o += D_RNN
        v = _dot(hb, w_ref[:, o:o + D_CONV])
        out_refs[3][0] = _dot(hb, w_ref[:, o + D_CONV:o + 2 * D_CONV])
        cg = _dot(hb, w_ref[:, o + 2 * D_CONV:o + 3 * D_CONV])
        out_refs[2][0] = cg * v


def _inproj(x, mod3, mod_row, norm_g, w_in, latent):
    bn, n, d = x.shape
    tm = min(INPROJ_TILE, n)
    n_out = 4 if latent else 1
    width = w_in.shape[1] if latent else D_RNN
    mod_map = (lambda b, i: (b, 0, 0)) if mod_row is None else (lambda b, i: (mod_row, 0, 0))
    return pl.pallas_call(
        functools.partial(_inproj_kernel, latent=latent),
        grid=(bn, n // tm),
        in_specs=[
            pl.BlockSpec((1, tm, d), lambda b, i: (b, i, 0)),
            pl.BlockSpec((1, 6, d), mod_map),
            pl.BlockSpec((1, d), lambda b, i: (0, 0)),
            pl.BlockSpec((d, width), lambda b, i: (0, 0), pipeline_mode=pl.Buffered(1)),
        ],
        out_specs=[pl.BlockSpec((1, tm, D_RNN), lambda b, i: (b, i, 0))] * n_out,
        out_shape=[jax.ShapeDtypeStruct((bn, n, D_RNN), F32)] * n_out,
        scratch_shapes=[pltpu.VMEM((d, width), BF16)],
        compiler_params=pltpu.CompilerParams(
            dimension_semantics=("arbitrary", "arbitrary"), vmem_limit_bytes=VMEM_LIMIT),
        name="inproj_lat" if latent else "inproj_ctx",
    )(x, mod3, norm_g.reshape(1, d), w_in)


def _shift_rows(x, k):
    n = x.shape[0]
    row = lax.broadcasted_iota(jnp.int32, x.shape, 0)
    rolled = pltpu.roll(x, k % n, axis=0)
    valid = (row >= k) if k > 0 else (row < n + k)
    return jnp.where(valid, rolled, 0.0)


def _scan_pitch(chunk):
    pitch = chunk + SUBLANES
    return pitch if (pitch // SUBLANES) % 2 else pitch + SUBLANES


def _rnn_kernel(xr_ref, xrc_ref, gr_ref, cw_ref, cb_ref, wg_ref, bg_ref, lam_ref, u_ref, bgc_ref, scw_ref,
                y_ref, yc_ref,
                xc_s, ap_f, bp_f, ap_b, bp_b, hl_f, al_f, hl_b, al_b, hp_f, hp_b, *, n_lat, n_ctx):
    _gconv_block(u_ref, bgc_ref, scw_ref, yc_ref)

    nl = -lam_ref[...]
    sp = jnp.maximum(nl, 0.0) + jnp.log1p(jnp.exp(-jnp.abs(nl)))
    c1 = (-0.5 * RG_C) * sp
    cw = cw_ref[...]
    bias = cb_ref[...]
    wg = wg_ref[0]
    bg = bg_ref[0]
    dirs = ((ap_f, bp_f, hl_f, al_f, hp_f), (ap_b, bp_b, hl_b, al_b, hp_b))

    def conv_into(x, n):
        xc_s[pl.ds(0, n), :] = (cw[0:1] * _shift_rows(x, 2) + cw[1:2] * _shift_rows(x, 1)
                                + cw[2:3] * x + cw[3:4] * _shift_rows(x, -1)) + bias

    def coefficients(n):
        chunk = n // SUBLANES
        pitch = _scan_pitch(chunk)
        rows = max(chunk, min(n, COEFF_ROWS))
        per = rows // chunk

        def body(i, carry):
            xc = xc_s[pl.ds(pl.multiple_of(i * rows, SUBLANES), rows), :]
            gates = _dot(xc.astype(BF16), wg) + bg
            half_xc = 0.5 * xc
            for d in range(2):
                tr = jnp.tanh(0.5 * gates[:, (2 * d) * LANES:(2 * d + 1) * LANES])
                ti = jnp.tanh(0.5 * gates[:, (2 * d + 1) * LANES:(2 * d + 2) * LANES])
                log_a = c1[d:d + 1] + c1[d:d + 1] * tr
                a = jnp.exp(log_a)
                y = -jnp.tanh(log_a) * (a * a + 1.0)
                b = (y * lax.rsqrt(jnp.maximum(y, F32_TINY))) * (half_xc + half_xc * ti)
                for k in range(per):
                    dst = pl.multiple_of((i * per + k) * pitch, SUBLANES)
                    dirs[d][0][pl.ds(dst, chunk), :] = a[k * chunk:(k + 1) * chunk]
                    dirs[d][1][pl.ds(dst, chunk), :] = b[k * chunk:(k + 1) * chunk]
            return carry

        if n == rows:
            body(0, 0)
        else:
            lax.fori_loop(0, n // rows, body, 0)

    def scan(n, h0_f, h0_b, keep):
        chunk = n // SUBLANES
        pitch = _scan_pitch(chunk)

        def steps(jo, carry):
            h_f, a_f, h_b, a_b = carry
            for u in range(SUBLANES):
                j = jo * SUBLANES + u
                av = ap_f[pl.ds(j, SUBLANES, stride=pitch), :]
                h_f = av * h_f + bp_f[pl.ds(j, SUBLANES, stride=pitch), :]
                a_f = av * a_f
                jb = chunk - 1 - j
                av = ap_b[pl.ds(jb, SUBLANES, stride=pitch), :]
                h_b = av * h_b + bp_b[pl.ds(jb, SUBLANES, stride=pitch), :]
                a_b = av * a_b
                if keep:
                    o = pl.multiple_of(j * SUBLANES, SUBLANES)
                    hl_f[pl.ds(o, SUBLANES), :] = h_f
                    al_f[pl.ds(o, SUBLANES), :] = a_f
                    hl_b[pl.ds(o, SUBLANES), :] = h_b
                    al_b[pl.ds(o, SUBLANES), :] = a_b
            return h_f, a_f, h_b, a_b

        zeros = jnp.zeros((SUBLANES, LANES), F32)
        ones = jnp.ones((SUBLANES, LANES), F32)
        h_f, a_f, h_b, a_b = lax.fori_loop(0, chunk // SUBLANES, steps, (zeros, ones, zeros, ones))

        in_f = [h0_f]
        for c in range(SUBLANES):
            in_f.append(a_f[c:c + 1] * in_f[c] + h_f[c:c + 1])
        in_b = [h0_b]
        for c in range(SUBLANES - 1, -1, -1):
            in_b.append(a_b[c:c + 1] * in_b[-1] + h_b[c:c + 1])
        if keep:
            hin_f = jnp.concatenate(in_f[:SUBLANES], axis=0)
            hin_b = jnp.concatenate(in_b[SUBLANES - 1::-1], axis=0)

            def fix(jo, carry):
                for u in range(SUBLANES):
                    j = jo * SUBLANES + u
                    o = pl.multiple_of(j * SUBLANES, SUBLANES)
                    hp_f[pl.ds(j, SUBLANES, stride=pitch), :] = (
                        hl_f[pl.ds(o, SUBLANES), :] + al_f[pl.ds(o, SUBLANES), :] * hin_f)
                    hp_b[pl.ds(chunk - 1 - j, SUBLANES, stride=pitch), :] = (
                        hl_b[pl.ds(o, SUBLANES), :] + al_b[pl.ds(o, SUBLANES), :] * hin_b)
                return carry
            lax.fori_loop(0, chunk // SUBLANES, fix, 0)
        return in_f[SUBLANES], in_b[SUBLANES]

    zero = jnp.zeros((1, LANES), F32)
    conv_into(xrc_ref[0], n_ctx)
    coefficients(n_ctx)
    h0_f, h0_b = scan(n_ctx, zero, zero, keep=False)

    conv_into(xr_ref[0], n_lat)
    coefficients(n_lat)
    scan(n_lat, h0_f, h0_b, keep=True)

    chunk = n_lat // SUBLANES
    pitch = _scan_pitch(chunk)

    def emit(c, carry):
        src = pl.multiple_of(c * chunk, 2 * SUBLANES)
        dst = pl.multiple_of(c * pitch, SUBLANES)
        hsum = hp_f[pl.ds(dst, chunk), :] + hp_b[pl.ds(dst, chunk), :]
        y = gr_ref[0, pl.ds(src, chunk), :] * hsum
        y_ref[0, pl.ds(src, chunk), :] = y.astype(y_ref.dtype)
        return carry
    lax.fori_loop(0, SUBLANES, emit, 0)


def _mixers(xr, xr_c, gr, conv_w, conv_b, wg, bgate, lam, u, bg, sc_w):
    bn, n, _ = xr.shape
    n_ctx = xr_c.shape[1]
    assert n % (SUBLANES * SUBLANES) == 0 and n_ctx % (SUBLANES * SUBLANES) == 0 and n_ctx <= n
    pitched = SUBLANES * _scan_pitch(n // SUBLANES)
    seq_spec = pl.BlockSpec((1, n, LANES), lambda b, p: (b, 0, p))
    return pl.pallas_call(
        functools.partial(_rnn_kernel, n_lat=n, n_ctx=n_ctx),
        grid=(bn, N_LANE_GROUPS),
        in_specs=[
            seq_spec,
            pl.BlockSpec((1, n_ctx, LANES), lambda b, p: (b, 0, p)),
            seq_spec,
            pl.BlockSpec((4, LANES), lambda b, p: (0, p)),
            pl.BlockSpec((1, LANES), lambda b, p: (0, p)),
            pl.BlockSpec((1, LANES, 4 * LANES), lambda b, p: (p, 0, 0)),
            pl.BlockSpec((1, 1, 4 * LANES), lambda b, p: (p, 0, 0)),
            pl.BlockSpec((2, LANES), lambda b, p: (0, p)),
            seq_spec,
            seq_spec,
            pl.BlockSpec((3, LANES), lambda b, p: (0, p)),
        ],
        out_specs=[seq_spec, seq_spec],
        out_shape=[jax.ShapeDtypeStruct((bn, n, D_RNN), BF16)] * 2,
        scratch_shapes=[pltpu.VMEM((n, LANES), F32)]
        + [pltpu.VMEM((pitched, LANES), F32)] * 4
        + [pltpu.VMEM((n, LANES), F32)] * 4
        + [pltpu.VMEM((pitched, LANES), F32)] * 2,
        compiler_params=pltpu.CompilerParams(vmem_limit_bytes=VMEM_LIMIT),
        name="mixers",
    )(xr, xr_c, gr, conv_w, conv_b.reshape(1, D_RNN), wg, bgate, lam, u, bg, sc_w)


def _gate_weights(rg_wa, rg_ba, rg_wx, rg_bx):
    eye = jnp.eye(2, dtype=F32)
    blocks, biases = [], []
    for d in range(2):
        for w, bvec in ((rg_wa[d], rg_ba[d]), (rg_wx[d], rg_bx[d])):
            w4 = w.reshape(N_LANE_GROUPS, 2, RNN_HEAD_DIM, RNN_HEAD_DIM)
            bd = jnp.einsum("paij,ac->paicj", w4, eye).reshape(N_LANE_GROUPS, LANES, LANES)
            blocks.append(bd)
            biases.append(bvec.reshape(N_LANE_GROUPS, 1, LANES))
    return jnp.concatenate(blocks, axis=-1).astype(BF16), jnp.concatenate(biases, axis=-1)


def _gconv_block(u_ref, bg_ref, w_ref, y_ref):
    p = pl.program_id(1)
    u = u_ref[0]
    w = w_ref[...]

    @pl.when(p < D_CONV_H // LANES)
    def _():
        col = lax.broadcasted_iota(jnp.int32, u.shape, 0) % GRID_W
        left = jnp.where(col > 0, _shift_rows(u, 1), 0.0)
        right = jnp.where(col < GRID_W - 1, _shift_rows(u, -1), 0.0)
        y_ref[0] = (bg_ref[0] * (w[0:1] * left + w[1:2] * u + w[2:3] * right)).astype(y_ref.dtype)

    @pl.when(p >= D_CONV_H // LANES)
    def _():
        y_ref[0] = (bg_ref[0] * (w[0:1] * _shift_rows(u, GRID_W) + w[1:2] * u
                                 + w[2:3] * _shift_rows(u, -GRID_W))).astype(y_ref.dtype)


def _lane_max(x, mask):
    return jnp.max(jnp.where(mask, x, -jnp.inf), axis=-1, keepdims=True)


def _first_lane(cond, lane):
    return jnp.min(jnp.where(cond, lane, float(LANES)), axis=-1, keepdims=True)


def _outproj_kernel(x_ref, yr_ref, yc_ref, w32_ref, mod_ref, g_ref, wr_ref, br_ref,
                    x1_ref, mt_ref, route_ref, cnt_ref, bef_ref, carry, w_ref, m_s, *, tm):
    s = pl.program_id(0)

    @pl.when(s == 0)
    def _():
        carry[...] = jnp.zeros_like(carry)
        w_ref[...] = w32_ref[...].astype(BF16)
        m_s[...] = jnp.zeros_like(m_s)

    logits = _dot(m_s[...], wr_ref[...]) + br_ref[...]
    lane_i = lax.broadcasted_iota(jnp.int32, logits.shape, 1)
    lane = lane_i.astype(F32)
    is_grp = lane_i < N_GROUPS
    g_max = _lane_max(logits, is_grp)
    grp = _first_lane(is_grp & (logits == g_max), lane)
    p_g = 1.0 / jnp.sum(jnp.where(is_grp, jnp.exp(logits - g_max), 0.0), axis=-1, keepdims=True)
    lo_lane = EXPERT_LANE0 + grp * EXPERTS_PER_GROUP
    in_grp = (lane >= lo_lane) & (lane < lo_lane + EXPERTS_PER_GROUP)
    l1 = _lane_max(logits, in_grp)
    i1 = _first_lane(in_grp & (logits == l1), lane)
    rest = in_grp & (lane != i1)
    l2 = _lane_max(logits, rest)
    i2 = _first_lane(rest & (logits == l2), lane)
    r21 = jnp.exp(l2 - l1)
    gate1 = p_g / (1.0 + r21)
    gate2 = gate1 * r21

    oh1 = jnp.where(lane == i1, 1.0, 0.0)
    oh2 = jnp.where(lane == i2, 1.0, 0.0)
    both = (oh1 + oh2).astype(BF16)
    ti = lax.broadcasted_iota(jnp.int32, (tm, tm), 0)
    tj = lax.broadcasted_iota(jnp.int32, (tm, tm), 1)
    tri = jnp.where(tj < ti, 1.0, 0.0).astype(BF16)
    counts = carry[...]
    bef_ref[0] = counts
    before = _dot(tri, both) + counts
    rank1 = jnp.sum(oh1 * before, axis=-1, keepdims=True)
    rank2 = jnp.sum(oh2 * before, axis=-1, keepdims=True)
    out = jnp.zeros(logits.shape, F32)
    for k, val in enumerate((i1 - EXPERT_LANE0, i2 - EXPERT_LANE0, gate1, gate2, rank1, rank2)):
        out = jnp.where(lane_i == k, val, out)
    route_ref[...] = out
    real = jnp.where(s > 0, 1.0, 0.0)
    total = counts + real * jnp.sum(oh1 + oh2, axis=0, keepdims=True)
    carry[...] = total
    cnt_ref[...] = total

    mix = _dot(yr_ref[0], w_ref[0:D_RNN, :]) + _dot(yc_ref[0], w_ref[D_RNN:, :])
    x1 = x_ref[0] + mod_ref[0, 2:3, :] * mix
    x1_ref[0] = x1
    m_new = _norm_mod(x1, g_ref[...], mod_ref[0, 4:5, :], mod_ref[0, 3:4, :])
    m_s[...] = m_new.astype(BF16)
    half = D_MODEL // 2
    packed = pltpu.pack_elementwise([m_new[:, :half], m_new[:, half:]], packed_dtype=BF16)
    for q in range(PACK_TILES):
        mt_ref[pl.ds(q, tm, stride=PACK_TILES), :] = packed[:, q * LANES:(q + 1) * LANES]


def _outproj(x, y_rnn, y_conv, w_out, mod3, norm_g, wr, br):
    bn, n, d = x.shape
    tm = min(OUTPROJ_TILE, n)
    nt = n // tm
    n_tiles = bn * nt
    t_all = bn * n

    def cur(s):
        return jnp.minimum(s, n_tiles - 1)

    def prev(s):
        return jnp.maximum(s - 1, 0)

    def seq_map(s):
        return (cur(s) // nt, cur(s) % nt, 0)

    const = lambda s: (0, 0)
    return pl.pallas_call(
        functools.partial(_outproj_kernel, tm=tm),
        grid=(n_tiles + 1,),
        in_specs=[
            pl.BlockSpec((1, tm, d), seq_map),
            pl.BlockSpec((1, tm, D_RNN), seq_map),
            pl.BlockSpec((1, tm, D_CONV), seq_map),
            pl.BlockSpec((D_RNN + D_CONV, d), const, pipeline_mode=pl.Buffered(1)),
            pl.BlockSpec((1, 6, d), lambda s: (cur(s) // nt, 0, 0)),
            pl.BlockSpec((1, d), const),
            pl.BlockSpec((d, LANES), const),
            pl.BlockSpec((1, LANES), const),
        ],
        out_specs=[
            pl.BlockSpec((1, tm, d), seq_map),
            pl.BlockSpec((tm * PACK_TILES, LANES), lambda s: (cur(s), 0)),
            pl.BlockSpec((tm, LANES), lambda s: (prev(s), 0)),
            pl.BlockSpec((1, LANES), const),
            pl.BlockSpec((1, 1, LANES), lambda s: (prev(s), 0, 0)),
        ],
        out_shape=[
            jax.ShapeDtypeStruct((bn, n, d), F32),
            jax.ShapeDtypeStruct((t_all * PACK_TILES, LANES), jnp.uint32),
            jax.ShapeDtypeStruct((t_all, LANES), F32),
            jax.ShapeDtypeStruct((1, LANES), F32),
            jax.ShapeDtypeStruct((n_tiles, 1, LANES), F32),
        ],
        scratch_shapes=[pltpu.VMEM((1, LANES), F32), pltpu.VMEM((D_RNN + D_CONV, d), BF16),
                        pltpu.VMEM((tm, d), BF16)],
        compiler_params=pltpu.CompilerParams(
            dimension_semantics=("arbitrary",), vmem_limit_bytes=VMEM_LIMIT),
        name="outproj",
    )(x, y_rnn, y_conv, w_out, mod3, norm_g.reshape(1, d), wr, br)


def _row_tile(ref, row):
    return ref.at[pl.ds(pl.multiple_of(row * ROW_TILES, ROW_TILES), ROW_TILES)]


def _slotmap_kernel(dest_ref, zeros_hbm, asg_ref, sem):
    fill = pltpu.make_async_copy(zeros_hbm, asg_ref, sem)
    fill.start()
    fill.wait()

    def body(c, carry):
        for u in range(DMA_UNROLL):
            a = c * DMA_UNROLL + u
            asg_ref[dest_ref[a]] = a
        return carry
    lax.fori_loop(0, dest_ref.shape[0] // DMA_UNROLL, body, 0)


def _slotmap(dest, n_slots):
    return pl.pallas_call(
        _slotmap_kernel,
        in_specs=[pl.BlockSpec(memory_space=pltpu.SMEM), pl.BlockSpec(memory_space=pl.ANY)],
        out_specs=pl.BlockSpec(memory_space=pltpu.SMEM),
        out_shape=jax.ShapeDtypeStruct((n_slots,), jnp.int32),
        scratch_shapes=[pltpu.SemaphoreType.DMA],
        name="slotmap",
    )(dest, jnp.zeros((n_slots,), jnp.int32))


def _expert_kernel(be_ref, ws_ref, ne_ref, nu_ref, asg_hbm, m_ref, wg_hbm, wu_hbm, wd_hbm, yb_ref,
                   xbuf_a, xbuf_b, idx, isems, wbuf_g, wbuf_u, wbuf_d, wsems, wg_s, wu_s, wd_s):
    j = pl.program_id(0)
    n_used = nu_ref[0]
    last = n_used - 1

    def idx_copy(blk, sl):
        return pltpu.make_async_copy(asg_hbm.at[blk], idx.at[sl], isems.at[sl])

    def copy_rows(buf, sl, r0, n):
        for u in range(n):
            tok = lax.shift_right_logical(idx[sl, 0, r0 + u], 1)
            src = pl.multiple_of(tok * PACK_TILES, PACK_TILES)
            buf[pl.ds((r0 + u) * PACK_TILES, PACK_TILES), :] = m_ref[pl.ds(src, PACK_TILES), :]

    def unpack(buf):
        halves = ([], [])
        for s in range(PACK_TILES):
            word = buf[pl.ds(s, MOE_BLK, stride=PACK_TILES), :]
            for k in range(2):
                part = pltpu.unpack_elementwise(word, index=k, packed_dtype=BF16, unpacked_dtype=F32)
                halves[k].append(part.astype(BF16))
        return jnp.concatenate(halves[0] + halves[1], axis=-1)

    n_pieces = (2 * D_EXPERT // MXU_TILE) * (D_MODEL // MXU_TILE) + (
        D_MODEL // MXU_TILE) * (D_EXPERT // MXU_TILE)
    bounds = [(p * MOE_BLK) // n_pieces for p in range(n_pieces + 1)]

    def compute(cur, nxt, nxt_sl):
        pieces = iter(zip(bounds[:-1], bounds[1:]))

        def dot_pieces(a, w_ref, n0):
            acc = None
            for k0 in range(0, a.shape[1], MXU_TILE):
                part = _dot(a[:, k0:k0 + MXU_TILE], w_ref[k0:k0 + MXU_TILE, n0:n0 + MXU_TILE])
                acc = part if acc is None else acc + part
                r0, r1 = next(pieces)
                copy_rows(nxt, nxt_sl, r0, r1 - r0)
            return acc

        xb16 = unpack(cur)
        acts = []
        for n0 in range(0, D_EXPERT, MXU_TILE):
            gate = dot_pieces(xb16, wg_s, n0)
            up = dot_pieces(xb16, wu_s, n0)
            acts.append(((gate * jax.nn.sigmoid(gate)) * up).astype(BF16))
        h = jnp.concatenate(acts, axis=-1)
        for n0 in range(0, D_MODEL, MXU_TILE):
            y = dot_pieces(h, wd_s, n0)
            for s in range(MXU_TILE // LANES):
                yb_ref[pl.ds(n0 // LANES + s, MOE_BLK, stride=ROW_TILES), :] = (
                    y[:, s * LANES:(s + 1) * LANES])

    @pl.when(j >= n_used)
    def _():
        yb_ref[...] = jnp.zeros_like(yb_ref)

    @pl.when(j < n_used)
    def _():
        slot = j % 2
        other = 1 - slot

        @pl.when(j == 0)
        def _():
            idx_copy(0, 0).start()
            idx_copy(0, 0).wait()

            def body(c, carry):
                copy_rows(xbuf_a, 0, c * DMA_UNROLL, DMA_UNROLL)
                return carry
            lax.fori_loop(0, MOE_BLK // DMA_UNROLL, body, 0)
            idx_copy(jnp.minimum(1, last), 1).start()

        e = be_ref[j]
        wslot = ws_ref[j]

        def weight_copies(expert, sl):
            return [pltpu.make_async_copy(src.at[expert], dst.at[sl], wsems.at[sl])
                    for src, dst in ((wg_hbm, wbuf_g), (wu_hbm, wbuf_u), (wd_hbm, wbuf_d))]

        @pl.when(j == 0)
        def _():
            for cp in weight_copies(e, wslot):
                cp.start()

        @pl.when((j == 0) | (e != be_ref[jnp.maximum(j - 1, 0)]))
        def _():
            for cp in weight_copies(e, wslot):
                cp.wait()
            wg_s[...] = wbuf_g[wslot].astype(BF16)
            wu_s[...] = wbuf_u[wslot].astype(BF16)
            wd_s[...] = wbuf_d[wslot].astype(BF16)

            @pl.when(ne_ref[j] >= 0)
            def _():
                for cp in weight_copies(ne_ref[j], 1 - wslot):
                    cp.start()

        idx_copy(0, other).wait()

        @pl.when(slot == 0)
        def _():
            compute(xbuf_a, xbuf_b, 1)

        @pl.when(slot == 1)
        def _():
            compute(xbuf_b, xbuf_a, 0)

        @pl.when(j < last)
        def _():
            idx_copy(jnp.minimum(j + 2, last), slot).start()


def _experts(block_e, weight_slot, next_e, n_used, slot_asg, mt, w_gate, w_up, w_down, n_blocks):
    return pl.pallas_call(
        _expert_kernel,
        grid_spec=pltpu.PrefetchScalarGridSpec(
            num_scalar_prefetch=4,
            grid=(n_blocks,),
            in_specs=[
                pl.BlockSpec(memory_space=pl.ANY),
                pl.BlockSpec(memory_space=pltpu.VMEM),
                pl.BlockSpec(memory_space=pl.ANY),
                pl.BlockSpec(memory_space=pl.ANY),
                pl.BlockSpec(memory_space=pl.ANY),
            ],
            out_specs=pl.BlockSpec((MOE_BLK * ROW_TILES, LANES), lambda j, *_: (j, 0)),
            scratch_shapes=[
                pltpu.VMEM((MOE_BLK * PACK_TILES, LANES), jnp.uint32),
                pltpu.VMEM((MOE_BLK * PACK_TILES, LANES), jnp.uint32),
                pltpu.SMEM((2, 1, MOE_BLK), jnp.int32),
                pltpu.SemaphoreType.DMA((2,)),
                pltpu.VMEM((2, D_MODEL, D_EXPERT), F32),
                pltpu.VMEM((2, D_MODEL, D_EXPERT), F32),
                pltpu.VMEM((2, D_EXPERT, D_MODEL), F32),
                pltpu.SemaphoreType.DMA((2,)),
                pltpu.VMEM((D_MODEL, D_EXPERT), BF16),
                pltpu.VMEM((D_MODEL, D_EXPERT), BF16),
                pltpu.VMEM((D_EXPERT, D_MODEL), BF16),
            ],
        ),
        out_shape=jax.ShapeDtypeStruct((n_blocks * MOE_BLK * ROW_TILES, LANES), F32),
        compiler_params=pltpu.CompilerParams(
            dimension_semantics=("arbitrary",), vmem_limit_bytes=EXPERT_VMEM_LIMIT),
        name="expert",
    )(block_e, weight_slot, next_e, n_used, slot_asg.reshape(n_blocks, 1, MOE_BLK), mt, w_gate, w_up, w_down)


def _combine_kernel(srow_ref, segs_ref, segn_ref, segb_ref, ntot_ref, yb_hbm, x1_ref, route_ref, mod_ref,
                    g_ref, o_ref, stage, ybuf, sems, *, tc):
    i = pl.program_id(0)
    slot = i % 2
    chunk_rows = SEG_CHUNK * ROW_TILES

    def chunk_copy(src_row, dst_row, sl):
        src = yb_hbm.at[pl.ds(pl.multiple_of(src_row * ROW_TILES, ROW_TILES), chunk_rows)]
        dst = stage.at[sl, pl.ds(pl.multiple_of(dst_row * ROW_TILES, chunk_rows), chunk_rows)]
        return pltpu.make_async_copy(src, dst, sems.at[sl])

    def fetch(tile, sl):
        for e in range(N_EXPERTS):
            first = segs_ref[tile * N_EXPERTS + e]
            base = segb_ref[tile * N_EXPERTS + e]

            def body(k, carry, first=first, base=base):
                chunk_copy(first + k * SEG_CHUNK, base + k * SEG_CHUNK, sl).start()
                return carry
            lax.fori_loop(0, segn_ref[tile * N_EXPERTS + e], body, 0)

    @pl.when(i == 0)
    def _():
        fetch(0, 0)

    @pl.when(i + 1 < pl.num_programs(0))
    def _():
        fetch(i + 1, 1 - slot)

    lax.fori_loop(0, ntot_ref[i], lambda k, c: (chunk_copy(0, 0, slot).wait(), c)[1], 0)

    def to_token_order(c, carry):
        for u in range(SUBLANES):
            r = c * SUBLANES + u
            for k in range(TOP_K):
                row = srow_ref[TOP_K * (i * tc + r) + k]
                ybuf[k, pl.ds(r * ROW_TILES, ROW_TILES), :] = stage[
                    slot, pl.ds(pl.multiple_of(row * ROW_TILES, ROW_TILES), ROW_TILES), :]
        return carry
    lax.fori_loop(0, tc // SUBLANES, to_token_order, 0)

    def rows(k):
        return jnp.concatenate(
            [ybuf[k, pl.ds(s, tc, stride=ROW_TILES), :] for s in range(ROW_TILES)], axis=-1)

    route = route_ref[...]
    y = route[:, 2:3] * rows(0) + route[:, 3:4] * rows(1)
    x2 = x1_ref[...] + mod_ref[0, 5:6, :] * y
    ms = jnp.mean(x2 * x2, axis=-1, keepdims=True)
    o_ref[...] = x2 * lax.rsqrt(ms + NORM_EPS) * g_ref[...]


def _combine(stage_row, seg_first, seg_dmas, seg_base, tile_dmas, yb, x1_2d, route, mod3, final_g, seq, tc):
    t_all, d = x1_2d.shape
    per_seq = seq // tc
    stage_rows = (TOP_K * tc + N_EXPERTS * SEG_CHUNK) * ROW_TILES
    return pl.pallas_call(
        functools.partial(_combine_kernel, tc=tc),
        grid_spec=pltpu.PrefetchScalarGridSpec(
            num_scalar_prefetch=5,
            grid=(t_all // tc,),
            in_specs=[
                pl.BlockSpec(memory_space=pl.ANY),
                pl.BlockSpec((tc, d), lambda i, *_: (i, 0)),
                pl.BlockSpec((tc, LANES), lambda i, *_: (i, 0)),
                pl.BlockSpec((1, 6, d), lambda i, *_: (i // per_seq, 0, 0)),
                pl.BlockSpec((1, d), lambda i, *_: (0, 0)),
            ],
            out_specs=pl.BlockSpec((tc, d), lambda i, *_: (i, 0)),
            scratch_shapes=[
                pltpu.VMEM((2, stage_rows, LANES), F32),
                pltpu.VMEM((TOP_K, tc * ROW_TILES, LANES), F32),
                pltpu.SemaphoreType.DMA((2,)),
            ],
        ),
        out_shape=jax.ShapeDtypeStruct((t_all, d), F32),
        compiler_params=pltpu.CompilerParams(
            dimension_semantics=("arbitrary",), vmem_limit_bytes=VMEM_LIMIT),
        name="combine",
    )(stage_row, seg_first, seg_dmas, seg_base, tile_dmas, yb, x1_2d, route, mod3, final_g.reshape(1, d))


def kernel(x, c, ctx, c_ctx, ada_w, ada_b, norm1_g, norm2_g, w_in, rnn_conv_w, rnn_conv_b, rg_wa, rg_ba,
           rg_wx, rg_bx, rg_lambda, sc_conv_w, w_out, router_group_w, router_group_b, router_exp_w,
           router_exp_b, exp_w_gate, exp_w_up, exp_w_down, final_norm_g):
    bn, seq, d = x.shape
    assert d == D_MODEL and bn < MOD_ROWS and ada_w.shape[0] == 1
    t_all = bn * seq

    cc = jnp.concatenate([c, c_ctx[None], jnp.zeros((MOD_ROWS - bn - 1, d), F32)], axis=0)
    mod3 = _modulation(cc, ada_w[0], ada_b[0]).reshape(MOD_ROWS, 6, d)

    xr, gr, u, bg = _inproj(x, mod3, None, norm1_g[0], w_in[0], latent=True)
    (xr_c,) = _inproj(ctx, mod3, bn, norm1_g[0], w_in[0], latent=False)

    wg, bgate = _gate_weights(rg_wa[0], rg_ba[0], rg_wx[0], rg_bx[0])
    assert D_RNN == D_CONV
    y_rnn, y_conv = _mixers(xr, xr_c, gr, rnn_conv_w[0], rnn_conv_b[0], wg, bgate, rg_lambda[0], u, bg,
                            sc_conv_w[0])

    wr = jnp.zeros((d, LANES), F32)
    wr = wr.at[:, :N_GROUPS].set(router_group_w[0]).at[:, EXPERT_LANE0:EXPERT_LANE0 + N_EXPERTS].set(router_exp_w[0])
    br = jnp.zeros((1, LANES), F32)
    br = br.at[0, :N_GROUPS].set(router_group_b[0]).at[0, EXPERT_LANE0:EXPERT_LANE0 + N_EXPERTS].set(router_exp_b[0])
    x1, mt, route, cnt, bef = _outproj(x, y_rnn, y_conv, w_out[0], mod3, norm2_g[0], wr.astype(BF16), br)

    n_assign = t_all * TOP_K
    n_blocks = (n_assign + N_EXPERTS * (MOE_BLK - 1) + MOE_BLK - 1) // MOE_BLK + 1
    counts = cnt[0, EXPERT_LANE0:EXPERT_LANE0 + N_EXPERTS].astype(jnp.int32)
    pcounts = (counts + MOE_BLK - 1) // MOE_BLK * MOE_BLK
    pends = jnp.cumsum(pcounts)
    pstarts = pends - pcounts
    experts = route[:, 0:TOP_K].astype(jnp.int32)
    ranks = route[:, 4:4 + TOP_K].astype(jnp.int32)
    onehot = experts[:, :, None] == jnp.arange(N_EXPERTS, dtype=jnp.int32)
    dest = (ranks + jnp.sum(jnp.where(onehot, pstarts, 0), axis=-1)).reshape(n_assign)
    n_used = (pends[-1] // MOE_BLK).astype(jnp.int32)
    blk_start = jnp.arange(n_blocks, dtype=jnp.int32) * MOE_BLK
    block_e = jnp.minimum(jnp.sum(blk_start[:, None] >= pends[None, :], axis=1), N_EXPERTS - 1)
    last_e = jnp.max(jnp.where(counts > 0, jnp.arange(N_EXPERTS, dtype=jnp.int32), 0))
    block_e = jnp.where(blk_start < pends[-1], block_e, last_e).astype(jnp.int32)
    eids = jnp.arange(N_EXPERTS, dtype=jnp.int32)
    used = counts > 0
    slot_of_e = (jnp.cumsum(used.astype(jnp.int32)) - 1) % 2
    later = jnp.where(used[None, :] & (eids[None, :] > eids[:, None]), eids[None, :], N_EXPERTS)
    next_of_e = jnp.min(later, axis=1)
    next_of_e = jnp.where(next_of_e == N_EXPERTS, -1, next_of_e)
    is_e = block_e[:, None] == eids[None, :]
    weight_slot = jnp.sum(jnp.where(is_e, slot_of_e[None, :], 0), axis=1).astype(jnp.int32)
    next_e = jnp.sum(jnp.where(is_e, next_of_e[None, :], 0), axis=1).astype(jnp.int32)

    n_slots = n_blocks * MOE_BLK
    slot_asg = _slotmap(dest, n_slots)
    yb = _experts(block_e, weight_slot, next_e, n_used.reshape(1), slot_asg, mt, exp_w_gate[0], exp_w_up[0],
                  exp_w_down[0], n_blocks)
    tc = min(OUTPROJ_TILE, seq)
    assert SEG_CHUNK <= MOE_BLK
    n_tiles = t_all // tc
    before = bef[:, 0, EXPERT_LANE0:EXPERT_LANE0 + N_EXPERTS].astype(jnp.int32)
    after = jnp.concatenate([before[1:], counts[None, :]], axis=0)
    seg_first = pstarts[None, :] + before
    seg_dmas = (after - before + (SEG_CHUNK - 1)) // SEG_CHUNK
    seg_rows = seg_dmas * SEG_CHUNK
    seg_base = jnp.cumsum(seg_rows, axis=1) - seg_rows
    tile_dmas = jnp.sum(seg_dmas, axis=1)
    shift = (seg_base - seg_first)[:, None, :]
    stage_row = dest.reshape(n_tiles, tc * TOP_K) + jnp.sum(
        jnp.where(onehot.reshape(n_tiles, tc * TOP_K, N_EXPERTS), shift, 0), axis=-1)
    flat = lambda a: a.reshape(-1).astype(jnp.int32)
    out = _combine(flat(stage_row), flat(seg_first), flat(seg_dmas), flat(seg_base), flat(tile_dmas), yb,
                   x1.reshape(t_all, d), route, mod3, final_norm_g, seq, tc)
    return out.reshape(bn, seq, d)
```

```python
import functools

import jax
import jax.numpy as jnp
from jax import lax
from jax.experimental import pallas as pl
from jax.experimental.pallas import tpu as pltpu

F32 = jnp.float32
BF16 = jnp.bfloat16

D_MODEL = 1024
D_RNN = 512
D_CONV = 512
D_CONV_H = D_CONV // 2
RNN_HEADS = 8
RNN_HEAD_DIM = D_RNN // RNN_HEADS
GRID_W = 64
RG_C = 8.0
N_GROUPS = 4
EXPERTS_PER_GROUP = 8
N_EXPERTS = N_GROUPS * EXPERTS_PER_GROUP
TOP_K = 2
D_EXPERT = 512
NORM_EPS = 1e-6
F32_TINY = 1.1754944e-38

LANES = 128
SUBLANES = 8
ROW_TILES = D_MODEL // LANES
PACK_TILES = ROW_TILES // 2
N_LANE_GROUPS = D_RNN // LANES
EXPERT_LANE0 = N_GROUPS

MOD_ROWS = 16
MOD_TN = 768
INPROJ_TILE = 1024
OUTPROJ_TILE = 512
COEFF_ROWS = 512
MOE_BLK = 512
MXU_TILE = 256
COMBINE_TILE = 256
DMA_UNROLL = 16
VMEM_LIMIT = 48 * 1024 * 1024
EXPERT_VMEM_LIMIT = 58 * 1024 * 1024


def _dot(a, b):
    return jnp.dot(a, b, preferred_element_type=F32)


def _split_bf16(x):
    hi = x.astype(BF16)
    lo = (x - hi.astype(F32)).astype(BF16)
    return hi, lo


def _mod_kernel(cc_ref, w_ref, b_ref, o_ref):
    s = cc_ref[...]
    s = s * jax.nn.sigmoid(s)
    s_hi, s_lo = _split_bf16(s)
    w_hi, w_lo = _split_bf16(w_ref[...])
    o_ref[...] = _dot(s_hi, w_hi) + _dot(s_lo, w_hi) + _dot(s_hi, w_lo) + b_ref[...]


def _modulation(cc, ada_w, ada_b):
    n = ada_w.shape[1]
    return pl.pallas_call(
        _mod_kernel,
        grid=(n // MOD_TN,),
        in_specs=[
            pl.BlockSpec((MOD_ROWS, D_MODEL), lambda j: (0, 0)),
            pl.BlockSpec((D_MODEL, MOD_TN), lambda j: (0, j)),
            pl.BlockSpec((1, MOD_TN), lambda j: (0, j)),
        ],
        out_specs=pl.BlockSpec((MOD_ROWS, MOD_TN), lambda j: (0, j)),
        out_shape=jax.ShapeDtypeStruct((MOD_ROWS, n), F32),
        compiler_params=pltpu.CompilerParams(vmem_limit_bytes=VMEM_LIMIT),
        name="mod",
    )(cc, ada_w, ada_b.reshape(1, n))


def _norm_mod(x, g, scale, shift):
    ms = jnp.mean(x * x, axis=-1, keepdims=True)
    y = x * lax.rsqrt(ms + NORM_EPS) * g
    return y * (1.0 + scale) + shift


def _inproj_kernel(x_ref, mod_ref, g_ref, w32_ref, *refs, latent):
    out_refs, w_ref = refs[:-1], refs[-1]

    @pl.when((pl.program_id(0) == 0) & (pl.program_id(1) == 0))
    def _():
        w_ref[...] = w32_ref[...].astype(BF16)

    h = _norm_mod(x_ref[0], g_ref[...], mod_ref[0, 1:2, :], mod_ref[0, 0:1, :])
    hb = h.astype(BF16)
    xr = _dot(hb, w_ref[:, 0:D_RNN])
    out_refs[0][0] = xr
    if latent:
        o = D_RNN
        out_refs[1][0] = jax.nn.gelu(_dot(hb, w_ref[:, o:o + D_RNN]), approximate=True)
        o += D_RNN
        v = _dot(hb, w_ref[:, o:o + D_CONV])
        out_refs[3][0] = _dot(hb, w_ref[:, o + D_CONV:o + 2 * D_CONV])
        cg = _dot(hb, w_ref[:, o + 2 * D_CONV:o + 3 * D_CONV])
        out_refs[2][0] = cg * v


def _inproj(x, mod3, mod_row, norm_g, w_in, latent):
    bn, n, d = x.shape
    tm = min(INPROJ_TILE, n)
    n_out = 4 if latent else 1
    width = w_in.shape[1] if latent else D_RNN
    mod_map = (lambda b, i: (b, 0, 0)) if mod_row is None else (lambda b, i: (mod_row, 0, 0))
    return pl.pallas_call(
        functools.partial(_inproj_kernel, latent=latent),
        grid=(bn, n // tm),
        in_specs=[
            pl.BlockSpec((1, tm, d), lambda b, i: (b, i, 0)),
            pl.BlockSpec((1, 6, d), mod_map),
            pl.BlockSpec((1, d), lambda b, i: (0, 0)),
            pl.BlockSpec((d, width), lambda b, i: (0, 0), pipeline_mode=pl.Buffered(1)),
        ],
        out_specs=[pl.BlockSpec((1, tm, D_RNN), lambda b, i: (b, i, 0))] * n_out,
        out_shape=[jax.ShapeDtypeStruct((bn, n, D_RNN), F32)] * n_out,
        scratch_shapes=[pltpu.VMEM((d, width), BF16)],
        compiler_params=pltpu.CompilerParams(
            dimension_semantics=("arbitrary", "arbitrary"), vmem_limit_bytes=VMEM_LIMIT),
        name="inproj_lat" if latent else "inproj_ctx",
    )(x, mod3, norm_g.reshape(1, d), w_in)


def _shift_rows(x, k):
    n = x.shape[0]
    row = lax.broadcasted_iota(jnp.int32, x.shape, 0)
    rolled = pltpu.roll(x, k % n, axis=0)
    valid = (row >= k) if k > 0 else (row < n + k)
    return jnp.where(valid, rolled, 0.0)


def _scan_pitch(chunk):
    pitch = chunk + SUBLANES
    return pitch if (pitch // SUBLANES) % 2 else pitch + SUBLANES


def _rnn_kernel(xr_ref, xrc_ref, gr_ref, cw_ref, cb_ref, wg_ref, bg_ref, lam_ref, u_ref, bgc_ref, scw_ref,
                y_ref, yc_ref,
                xc_s, ap_f, bp_f, ap_b, bp_b, hl_f, al_f, hl_b, al_b, hp_f, hp_b, *, n_lat, n_ctx):
    _gconv_block(u_ref, bgc_ref, scw_ref, yc_ref)

    nl = -lam_ref[...]
    sp = jnp.maximum(nl, 0.0) + jnp.log1p(jnp.exp(-jnp.abs(nl)))
    c1 = (-0.5 * RG_C) * sp
    cw = cw_ref[...]
    bias = cb_ref[...]
    wg = wg_ref[0]
    bg = bg_ref[0]
    dirs = ((ap_f, bp_f, hl_f, al_f, hp_f), (ap_b, bp_b, hl_b, al_b, hp_b))

    def conv_into(x, n):
        xc_s[pl.ds(0, n), :] = (cw[0:1] * _shift_rows(x, 2) + cw[1:2] * _shift_rows(x, 1)
                                + cw[2:3] * x + cw[3:4] * _shift_rows(x, -1)) + bias

    def coefficients(n):
        chunk = n // SUBLANES
        pitch = _scan_pitch(chunk)
        rows = max(chunk, min(n, COEFF_ROWS))
        per = rows // chunk

        def body(i, carry):
            xc = xc_s[pl.ds(pl.multiple_of(i * rows, SUBLANES), rows), :]
            gates = _dot(xc.astype(BF16), wg) + bg
            half_xc = 0.5 * xc
            for d in range(2):
                tr = jnp.tanh(0.5 * gates[:, (2 * d) * LANES:(2 * d + 1) * LANES])
                ti = jnp.tanh(0.5 * gates[:, (2 * d + 1) * LANES:(2 * d + 2) * LANES])
                log_a = c1[d:d + 1] + c1[d:d + 1] * tr
                a = jnp.exp(log_a)
                y = -jnp.tanh(log_a) * (a * a + 1.0)
                b = (y * lax.rsqrt(jnp.maximum(y, F32_TINY))) * (half_xc + half_xc * ti)
                for k in range(per):
                    dst = pl.multiple_of((i * per + k) * pitch, SUBLANES)
                    dirs[d][0][pl.ds(dst, chunk), :] = a[k * chunk:(k + 1) * chunk]
                    dirs[d][1][pl.ds(dst, chunk), :] = b[k * chunk:(k + 1) * chunk]
            return carry

        if n == rows:
            body(0, 0)
        else:
            lax.fori_loop(0, n // rows, body, 0)

    def scan(n, h0_f, h0_b, keep):
        chunk = n // SUBLANES
        pitch = _scan_pitch(chunk)

        def steps(jo, carry):
            h_f, a_f, h_b, a_b = carry
            for u in range(SUBLANES):
                j = jo * SUBLANES + u
                av = ap_f[pl.ds(j, SUBLANES, stride=pitch), :]
                h_f = av * h_f + bp_f[pl.ds(j, SUBLANES, stride=pitch), :]
                a_f = av * a_f
                jb = chunk - 1 - j
                av = ap_b[pl.ds(jb, SUBLANES, stride=pitch), :]
                h_b = av * h_b + bp_b[pl.ds(jb, SUBLANES, stride=pitch), :]
                a_b = av * a_b
                if keep:
                    o = pl.multiple_of(j * SUBLANES, SUBLANES)
                    hl_f[pl.ds(o, SUBLANES), :] = h_f
                    al_f[pl.ds(o, SUBLANES), :] = a_f
                    hl_b[pl.ds(o, SUBLANES), :] = h_b
                    al_b[pl.ds(o, SUBLANES), :] = a_b
            return h_f, a_f, h_b, a_b

        zeros = jnp.zeros((SUBLANES, LANES), F32)
        ones = jnp.ones((SUBLANES, LANES), F32)
        h_f, a_f, h_b, a_b = lax.fori_loop(0, chunk // SUBLANES, steps, (zeros, ones, zeros, ones))

        in_f = [h0_f]
        for c in range(SUBLANES):
            in_f.append(a_f[c:c + 1] * in_f[c] + h_f[c:c + 1])
        in_b = [h0_b]
        for c in range(SUBLANES - 1, -1, -1):
            in_b.append(a_b[c:c + 1] * in_b[-1] + h_b[c:c + 1])
        if keep:
            hin_f = jnp.concatenate(in_f[:SUBLANES], axis=0)
            hin_b = jnp.concatenate(in_b[SUBLANES - 1::-1], axis=0)

            def fix(jo, carry):
                for u in range(SUBLANES):
                    j = jo * SUBLANES + u
                    o = pl.multiple_of(j * SUBLANES, SUBLANES)
                    hp_f[pl.ds(j, SUBLANES, stride=pitch), :] = (
                        hl_f[pl.ds(o, SUBLANES), :] + al_f[pl.ds(o, SUBLANES), :] * hin_f)
                    hp_b[pl.ds(chunk - 1 - j, SUBLANES, stride=pitch), :] = (
                        hl_b[pl.ds(o, SUBLANES), :] + al_b[pl.ds(o, SUBLANES), :] * hin_b)
                return carry
            lax.fori_loop(0, chunk // SUBLANES, fix, 0)
        return in_f[SUBLANES], in_b[SUBLANES]

    zero = jnp.zeros((1, LANES), F32)
    conv_into(xrc_ref[0], n_ctx)
    coefficients(n_ctx)
    h0_f, h0_b = scan(n_ctx, zero, zero, keep=False)

    conv_into(xr_ref[0], n_lat)
    coefficients(n_lat)
    scan(n_lat, h0_f, h0_b, keep=True)

    chunk = n_lat // SUBLANES
    pitch = _scan_pitch(chunk)

    def emit(c, carry):
        src = pl.multiple_of(c * chunk, 2 * SUBLANES)
        dst = pl.multiple_of(c * pitch, SUBLANES)
        hsum = hp_f[pl.ds(dst, chunk), :] + hp_b[pl.ds(dst, chunk), :]
        y = gr_ref[0, pl.ds(src, chunk), :] * hsum
        y_ref[0, pl.ds(src, chunk), :] = y.astype(y_ref.dtype)
        return carry
    lax.fori_loop(0, SUBLANES, emit, 0)


def _mixers(xr, xr_c, gr, conv_w, conv_b, wg, bgate, lam, u, bg, sc_w):
    bn, n, _ = xr.shape
    n_ctx = xr_c.shape[1]
    assert n % (SUBLANES * SUBLANES) == 0 and n_ctx % (SUBLANES * SUBLANES) == 0 and n_ctx <= n
    pitched = SUBLANES * _scan_pitch(n // SUBLANES)
    seq_spec = pl.BlockSpec((1, n, LANES), lambda b, p: (b, 0, p))
    return pl.pallas_call(
        functools.partial(_rnn_kernel, n_lat=n, n_ctx=n_ctx),
        grid=(bn, N_LANE_GROUPS),
        in_specs=[
            seq_spec,
            pl.BlockSpec((1, n_ctx, LANES), lambda b, p: (b, 0, p)),
            seq_spec,
            pl.BlockSpec((4, LANES), lambda b, p: (0, p)),
            pl.BlockSpec((1, LANES), lambda b, p: (0, p)),
            pl.BlockSpec((1, LANES, 4 * LANES), lambda b, p: (p, 0, 0)),
            pl.BlockSpec((1, 1, 4 * LANES), lambda b, p: (p, 0, 0)),
            pl.BlockSpec((2, LANES), lambda b, p: (0, p)),
            seq_spec,
            seq_spec,
            pl.BlockSpec((3, LANES), lambda b, p: (0, p)),
        ],
        out_specs=[seq_spec, seq_spec],
        out_shape=[jax.ShapeDtypeStruct((bn, n, D_RNN), BF16)] * 2,
        scratch_shapes=[pltpu.VMEM((n, LANES), F32)]
        + [pltpu.VMEM((pitched, LANES), F32)] * 4
        + [pltpu.VMEM((n, LANES), F32)] * 4
        + [pltpu.VMEM((pitched, LANES), F32)] * 2,
        compiler_params=pltpu.CompilerParams(vmem_limit_bytes=VMEM_LIMIT),
        name="mixers",
    )(xr, xr_c, gr, conv_w, conv_b.reshape(1, D_RNN), wg, bgate, lam, u, bg, sc_w)


def _gate_weights(rg_wa, rg_ba, rg_wx, rg_bx):
    eye = jnp.eye(2, dtype=F32)
    blocks, biases = [], []
    for d in range(2):
        for w, bvec in ((rg_wa[d], rg_ba[d]), (rg_wx[d], rg_bx[d])):
            w4 = w.reshape(N_LANE_GROUPS, 2, RNN_HEAD_DIM, RNN_HEAD_DIM)
            bd = jnp.einsum("paij,ac->paicj", w4, eye).reshape(N_LANE_GROUPS, LANES, LANES)
            blocks.append(bd)
            biases.append(bvec.reshape(N_LANE_GROUPS, 1, LANES))
    return jnp.concatenate(blocks, axis=-1).astype(BF16), jnp.concatenate(biases, axis=-1)


def _gconv_block(u_ref, bg_ref, w_ref, y_ref):
    p = pl.program_id(1)
    u = u_ref[0]
    w = w_ref[...]

    @pl.when(p < D_CONV_H // LANES)
    def _():
        col = lax.broadcasted_iota(jnp.int32, u.shape, 0) % GRID_W
        left = jnp.where(col > 0, _shift_rows(u, 1), 0.0)
        right = jnp.where(col < GRID_W - 1, _shift_rows(u, -1), 0.0)
        y_ref[0] = (bg_ref[0] * (w[0:1] * left + w[1:2] * u + w[2:3] * right)).astype(y_ref.dtype)

    @pl.when(p >= D_CONV_H // LANES)
    def _():
        y_ref[0] = (bg_ref[0] * (w[0:1] * _shift_rows(u, GRID_W) + w[1:2] * u
                                 + w[2:3] * _shift_rows(u, -GRID_W))).astype(y_ref.dtype)


def _lane_max(x, mask):
    return jnp.max(jnp.where(mask, x, -jnp.inf), axis=-1, keepdims=True)


def _first_lane(cond, lane):
    return jnp.min(jnp.where(cond, lane, float(LANES)), axis=-1, keepdims=True)


def _outproj_kernel(x_ref, yr_ref, yc_ref, w32_ref, mod_ref, g_ref, wr_ref, br_ref,
                    x1_ref, mt_ref, route_ref, cnt_ref, carry, w_ref, m_s, *, tm):
    s = pl.program_id(0)

    @pl.when(s == 0)
    def _():
        carry[...] = jnp.zeros_like(carry)
        w_ref[...] = w32_ref[...].astype(BF16)
        m_s[...] = jnp.zeros_like(m_s)

    logits = _dot(m_s[...], wr_ref[...]) + br_ref[...]
    lane_i = lax.broadcasted_iota(jnp.int32, logits.shape, 1)
    lane = lane_i.astype(F32)
    is_grp = lane_i < N_GROUPS
    g_max = _lane_max(logits, is_grp)
    grp = _first_lane(is_grp & (logits == g_max), lane)
    p_g = 1.0 / jnp.sum(jnp.where(is_grp, jnp.exp(logits - g_max), 0.0), axis=-1, keepdims=True)
    lo_lane = EXPERT_LANE0 + grp * EXPERTS_PER_GROUP
    in_grp = (lane >= lo_lane) & (lane < lo_lane + EXPERTS_PER_GROUP)
    l1 = _lane_max(logits, in_grp)
    i1 = _first_lane(in_grp & (logits == l1), lane)
    rest = in_grp & (lane != i1)
    l2 = _lane_max(logits, rest)
    i2 = _first_lane(rest & (logits == l2), lane)
    r21 = jnp.exp(l2 - l1)
    gate1 = p_g / (1.0 + r21)
    gate2 = gate1 * r21

    oh1 = jnp.where(lane == i1, 1.0, 0.0)
    oh2 = jnp.where(lane == i2, 1.0, 0.0)
    both = (oh1 + oh2).astype(BF16)
    ti = lax.broadcasted_iota(jnp.int32, (tm, tm), 0)
    tj = lax.broadcasted_iota(jnp.int32, (tm, tm), 1)
    tri = jnp.where(tj < ti, 1.0, 0.0).astype(BF16)
    counts = carry[...]
    before = _dot(tri, both) + counts
    rank1 = jnp.sum(oh1 * before, axis=-1, keepdims=True)
    rank2 = jnp.sum(oh2 * before, axis=-1, keepdims=True)
    out = jnp.zeros(logits.shape, F32)
    for k, val in enumerate((i1 - EXPERT_LANE0, i2 - EXPERT_LANE0, gate1, gate2, rank1, rank2)):
        out = jnp.where(lane_i == k, val, out)
    route_ref[...] = out
    real = jnp.where(s > 0, 1.0, 0.0)
    total = counts + real * jnp.sum(oh1 + oh2, axis=0, keepdims=True)
    carry[...] = total
    cnt_ref[...] = total

    mix = _dot(yr_ref[0], w_ref[0:D_RNN, :]) + _dot(yc_ref[0], w_ref[D_RNN:, :])
    x1 = x_ref[0] + mod_ref[0, 2:3, :] * mix
    x1_ref[0] = x1
    m_new = _norm_mod(x1, g_ref[...], mod_ref[0, 4:5, :], mod_ref[0, 3:4, :])
    m_s[...] = m_new.astype(BF16)
    half = D_MODEL // 2
    packed = pltpu.pack_elementwise([m_new[:, :half], m_new[:, half:]], packed_dtype=BF16)
    for q in range(PACK_TILES):
        mt_ref[pl.ds(q, tm, stride=PACK_TILES), :] = packed[:, q * LANES:(q + 1) * LANES]


def _outproj(x, y_rnn, y_conv, w_out, mod3, norm_g, wr, br):
    bn, n, d = x.shape
    tm = min(OUTPROJ_TILE, n)
    nt = n // tm
    n_tiles = bn * nt
    t_all = bn * n

    def cur(s):
        return jnp.minimum(s, n_tiles - 1)

    def prev(s):
        return jnp.maximum(s - 1, 0)

    def seq_map(s):
        return (cur(s) // nt, cur(s) % nt, 0)

    const = lambda s: (0, 0)
    return pl.pallas_call(
        functools.partial(_outproj_kernel, tm=tm),
        grid=(n_tiles + 1,),
        in_specs=[
            pl.BlockSpec((1, tm, d), seq_map),
            pl.BlockSpec((1, tm, D_RNN), seq_map),
            pl.BlockSpec((1, tm, D_CONV), seq_map),
            pl.BlockSpec((D_RNN + D_CONV, d), const, pipeline_mode=pl.Buffered(1)),
            pl.BlockSpec((1, 6, d), lambda s: (cur(s) // nt, 0, 0)),
            pl.BlockSpec((1, d), const),
            pl.BlockSpec((d, LANES), const),
            pl.BlockSpec((1, LANES), const),
        ],
        out_specs=[
            pl.BlockSpec((1, tm, d), seq_map),
            pl.BlockSpec((tm * PACK_TILES, LANES), lambda s: (cur(s), 0)),
            pl.BlockSpec((tm, LANES), lambda s: (prev(s), 0)),
            pl.BlockSpec((1, LANES), const),
        ],
        out_shape=[
            jax.ShapeDtypeStruct((bn, n, d), F32),
            jax.ShapeDtypeStruct((t_all * PACK_TILES, LANES), jnp.uint32),
            jax.ShapeDtypeStruct((t_all, LANES), F32),
            jax.ShapeDtypeStruct((1, LANES), F32),
        ],
        scratch_shapes=[pltpu.VMEM((1, LANES), F32), pltpu.VMEM((D_RNN + D_CONV, d), BF16),
                        pltpu.VMEM((tm, d), BF16)],
        compiler_params=pltpu.CompilerParams(
            dimension_semantics=("arbitrary",), vmem_limit_bytes=VMEM_LIMIT),
        name="outproj",
    )(x, y_rnn, y_conv, w_out, mod3, norm_g.reshape(1, d), wr, br)


def _row_tile(ref, row):
    return ref.at[pl.ds(pl.multiple_of(row * ROW_TILES, ROW_TILES), ROW_TILES)]


def _slotmap_kernel(dest_ref, zeros_hbm, asg_ref, sem):
    fill = pltpu.make_async_copy(zeros_hbm, asg_ref, sem)
    fill.start()
    fill.wait()

    def body(c, carry):
        for u in range(DMA_UNROLL):
            a = c * DMA_UNROLL + u
            asg_ref[dest_ref[a]] = a
        return carry
    lax.fori_loop(0, dest_ref.shape[0] // DMA_UNROLL, body, 0)


def _slotmap(dest, n_slots):
    return pl.pallas_call(
        _slotmap_kernel,
        in_specs=[pl.BlockSpec(memory_space=pltpu.SMEM), pl.BlockSpec(memory_space=pl.ANY)],
        out_specs=pl.BlockSpec(memory_space=pltpu.SMEM),
        out_shape=jax.ShapeDtypeStruct((n_slots,), jnp.int32),
        scratch_shapes=[pltpu.SemaphoreType.DMA],
        name="slotmap",
    )(dest, jnp.zeros((n_slots,), jnp.int32))


def _expert_kernel(be_ref, ws_ref, ne_ref, nu_ref, asg_hbm, m_ref, wg_hbm, wu_hbm, wd_hbm, yb_ref,
                   xbuf_a, xbuf_b, idx, isems, wbuf_g, wbuf_u, wbuf_d, wsems, wg_s, wu_s, wd_s):
    j = pl.program_id(0)
    n_used = nu_ref[0]
    last = n_used - 1

    def idx_copy(blk, sl):
        return pltpu.make_async_copy(asg_hbm.at[blk], idx.at[sl], isems.at[sl])

    def copy_rows(buf, sl, r0, n):
        for u in range(n):
            tok = lax.shift_right_logical(idx[sl, 0, r0 + u], 1)
            src = pl.multiple_of(tok * PACK_TILES, PACK_TILES)
            buf[pl.ds((r0 + u) * PACK_TILES, PACK_TILES), :] = m_ref[pl.ds(src, PACK_TILES), :]

    def unpack(buf):
        halves = ([], [])
        for s in range(PACK_TILES):
            word = buf[pl.ds(s, MOE_BLK, stride=PACK_TILES), :]
            for k in range(2):
                part = pltpu.unpack_elementwise(word, index=k, packed_dtype=BF16, unpacked_dtype=F32)
                halves[k].append(part.astype(BF16))
        return jnp.concatenate(halves[0] + halves[1], axis=-1)

    n_pieces = (2 * D_EXPERT // MXU_TILE) * (D_MODEL // MXU_TILE) + (
        D_MODEL // MXU_TILE) * (D_EXPERT // MXU_TILE)
    bounds = [(p * MOE_BLK) // n_pieces for p in range(n_pieces + 1)]

    def compute(cur, nxt, nxt_sl):
        pieces = iter(zip(bounds[:-1], bounds[1:]))

        def dot_pieces(a, w_ref, n0):
            acc = None
            for k0 in range(0, a.shape[1], MXU_TILE):
                part = _dot(a[:, k0:k0 + MXU_TILE], w_ref[k0:k0 + MXU_TILE, n0:n0 + MXU_TILE])
                acc = part if acc is None else acc + part
                r0, r1 = next(pieces)
                copy_rows(nxt, nxt_sl, r0, r1 - r0)
            return acc

        xb16 = unpack(cur)
        acts = []
        for n0 in range(0, D_EXPERT, MXU_TILE):
            gate = dot_pieces(xb16, wg_s, n0)
            up = dot_pieces(xb16, wu_s, n0)
            acts.append(((gate * jax.nn.sigmoid(gate)) * up).astype(BF16))
        h = jnp.concatenate(acts, axis=-1)
        for n0 in range(0, D_MODEL, MXU_TILE):
            y = dot_pieces(h, wd_s, n0)
            for s in range(MXU_TILE // LANES):
                yb_ref[pl.ds(n0 // LANES + s, MOE_BLK, stride=ROW_TILES), :] = (
                    y[:, s * LANES:(s + 1) * LANES])

    @pl.when(j >= n_used)
    def _():
        yb_ref[...] = jnp.zeros_like(yb_ref)

    @pl.when(j < n_used)
    def _():
        slot = j % 2
        other = 1 - slot

        @pl.when(j == 0)
        def _():
            idx_copy(0, 0).start()
            idx_copy(0, 0).wait()

            def body(c, carry):
                copy_rows(xbuf_a, 0, c * DMA_UNROLL, DMA_UNROLL)
                return carry
            lax.fori_loop(0, MOE_BLK // DMA_UNROLL, body, 0)
            idx_copy(jnp.minimum(1, last), 1).start()

        e = be_ref[j]
        wslot = ws_ref[j]

        def weight_copies(expert, sl):
            return [pltpu.make_async_copy(src.at[expert], dst.at[sl], wsems.at[sl])
                    for src, dst in ((wg_hbm, wbuf_g), (wu_hbm, wbuf_u), (wd_hbm, wbuf_d))]

        @pl.when(j == 0)
        def _():
            for cp in weight_copies(e, wslot):
                cp.start()

        @pl.when((j == 0) | (e != be_ref[jnp.maximum(j - 1, 0)]))
        def _():
            for cp in weight_copies(e, wslot):
                cp.wait()
            wg_s[...] = wbuf_g[wslot].astype(BF16)
            wu_s[...] = wbuf_u[wslot].astype(BF16)
            wd_s[...] = wbuf_d[wslot].astype(BF16)

            @pl.when(ne_ref[j] >= 0)
            def _():
                for cp in weight_copies(ne_ref[j], 1 - wslot):
                    cp.start()

        idx_copy(0, other).wait()

        @pl.when(slot == 0)
        def _():
            compute(xbuf_a, xbuf_b, 1)

        @pl.when(slot == 1)
        def _():
            compute(xbuf_b, xbuf_a, 0)

        @pl.when(j < last)
        def _():
            idx_copy(jnp.minimum(j + 2, last), slot).start()


def _experts(block_e, weight_slot, next_e, n_used, slot_asg, mt, w_gate, w_up, w_down, n_blocks):
    return pl.pallas_call(
        _expert_kernel,
        grid_spec=pltpu.PrefetchScalarGridSpec(
            num_scalar_prefetch=4,
            grid=(n_blocks,),
            in_specs=[
                pl.BlockSpec(memory_space=pl.ANY),
                pl.BlockSpec(memory_space=pltpu.VMEM),
                pl.BlockSpec(memory_space=pl.ANY),
                pl.BlockSpec(memory_space=pl.ANY),
                pl.BlockSpec(memory_space=pl.ANY),
            ],
            out_specs=pl.BlockSpec((MOE_BLK * ROW_TILES, LANES), lambda j, *_: (j, 0)),
            scratch_shapes=[
                pltpu.VMEM((MOE_BLK * PACK_TILES, LANES), jnp.uint32),
                pltpu.VMEM((MOE_BLK * PACK_TILES, LANES), jnp.uint32),
                pltpu.SMEM((2, 1, MOE_BLK), jnp.int32),
                pltpu.SemaphoreType.DMA((2,)),
                pltpu.VMEM((2, D_MODEL, D_EXPERT), F32),
                pltpu.VMEM((2, D_MODEL, D_EXPERT), F32),
                pltpu.VMEM((2, D_EXPERT, D_MODEL), F32),
                pltpu.SemaphoreType.DMA((2,)),
                pltpu.VMEM((D_MODEL, D_EXPERT), BF16),
                pltpu.VMEM((D_MODEL, D_EXPERT), BF16),
                pltpu.VMEM((D_EXPERT, D_MODEL), BF16),
            ],
        ),
        out_shape=jax.ShapeDtypeStruct((n_blocks * MOE_BLK * ROW_TILES, LANES), F32),
        compiler_params=pltpu.CompilerParams(
            dimension_semantics=("arbitrary",), vmem_limit_bytes=EXPERT_VMEM_LIMIT),
        name="expert",
    )(block_e, weight_slot, next_e, n_used, slot_asg.reshape(n_blocks, 1, MOE_BLK), mt, w_gate, w_up, w_down)


def _combine_kernel(dest_ref, yb_hbm, x1_ref, route_ref, mod_ref, g_ref, o_ref, ybuf, sems, *, tc):
    i = pl.program_id(0)
    slot = i % 2

    def row_copy(d, sl, k, r):
        return pltpu.make_async_copy(_row_tile(yb_hbm, d), _row_tile(ybuf.at[sl, k], r), sems.at[sl])

    def gather(step, sl):
        def issue(c, carry):
            for u in range(DMA_UNROLL):
                r = c * DMA_UNROLL + u
                for k in range(TOP_K):
                    row_copy(dest_ref[TOP_K * (step * tc + r) + k], sl, k, r).start(priority=k)
            return carry
        lax.fori_loop(0, tc // DMA_UNROLL, issue, 0)

    @pl.when(i == 0)
    def _():
        gather(0, 0)

    @pl.when(i + 1 < pl.num_programs(0))
    def _():
        gather(i + 1, 1 - slot)

    def drain(c, carry):
        for u in range(DMA_UNROLL * TOP_K):
            row_copy(0, slot, 0, 0).wait()
        return carry
    lax.fori_loop(0, tc // DMA_UNROLL, drain, 0)

    def rows(k):
        return jnp.concatenate(
            [ybuf[slot, k, pl.ds(s, tc, stride=ROW_TILES), :] for s in range(ROW_TILES)], axis=-1)

    route = route_ref[...]
    y = route[:, 2:3] * rows(0) + route[:, 3:4] * rows(1)
    x2 = x1_ref[...] + mod_ref[0, 5:6, :] * y
    ms = jnp.mean(x2 * x2, axis=-1, keepdims=True)
    o_ref[...] = x2 * lax.rsqrt(ms + NORM_EPS) * g_ref[...]


def _combine(dest, yb, x1_2d, route, mod3, final_g, seq):
    t_all, d = x1_2d.shape
    tc = min(COMBINE_TILE, seq)
    per_seq = seq // tc
    return pl.pallas_call(
        functools.partial(_combine_kernel, tc=tc),
        grid_spec=pltpu.PrefetchScalarGridSpec(
            num_scalar_prefetch=1,
            grid=(t_all // tc,),
            in_specs=[
                pl.BlockSpec(memory_space=pl.ANY),
                pl.BlockSpec((tc, d), lambda i, dest: (i, 0)),
                pl.BlockSpec((tc, LANES), lambda i, dest: (i, 0)),
                pl.BlockSpec((1, 6, d), lambda i, dest: (i // per_seq, 0, 0)),
                pl.BlockSpec((1, d), lambda i, dest: (0, 0)),
            ],
            out_specs=pl.BlockSpec((tc, d), lambda i, dest: (i, 0)),
            scratch_shapes=[
                pltpu.VMEM((2, TOP_K, tc * ROW_TILES, LANES), F32),
                pltpu.SemaphoreType.DMA((2,)),
            ],
        ),
        out_shape=jax.ShapeDtypeStruct((t_all, d), F32),
        compiler_params=pltpu.CompilerParams(
            dimension_semantics=("arbitrary",), vmem_limit_bytes=VMEM_LIMIT),
        name="combine",
    )(dest, yb, x1_2d, route, mod3, final_g.reshape(1, d))


def kernel(x, c, ctx, c_ctx, ada_w, ada_b, norm1_g, norm2_g, w_in, rnn_conv_w, rnn_conv_b, rg_wa, rg_ba,
           rg_wx, rg_bx, rg_lambda, sc_conv_w, w_out, router_group_w, router_group_b, router_exp_w,
           router_exp_b, exp_w_gate, exp_w_up, exp_w_down, final_norm_g):
    bn, seq, d = x.shape
    assert d == D_MODEL and bn < MOD_ROWS and ada_w.shape[0] == 1
    t_all = bn * seq

    cc = jnp.concatenate([c, c_ctx[None], jnp.zeros((MOD_ROWS - bn - 1, d), F32)], axis=0)
    mod3 = _modulation(cc, ada_w[0], ada_b[0]).reshape(MOD_ROWS, 6, d)

    xr, gr, u, bg = _inproj(x, mod3, None, norm1_g[0], w_in[0], latent=True)
    (xr_c,) = _inproj(ctx, mod3, bn, norm1_g[0], w_in[0], latent=False)

    wg, bgate = _gate_weights(rg_wa[0], rg_ba[0], rg_wx[0], rg_bx[0])
    assert D_RNN == D_CONV
    y_rnn, y_conv = _mixers(xr, xr_c, gr, rnn_conv_w[0], rnn_conv_b[0], wg, bgate, rg_lambda[0], u, bg,
                            sc_conv_w[0])

    wr = jnp.zeros((d, LANES), F32)
    wr = wr.at[:, :N_GROUPS].set(router_group_w[0]).at[:, EXPERT_LANE0:EXPERT_LANE0 + N_EXPERTS].set(router_exp_w[0])
    br = jnp.zeros((1, LANES), F32)
    br = br.at[0, :N_GROUPS].set(router_group_b[0]).at[0, EXPERT_LANE0:EXPERT_LANE0 + N_EXPERTS].set(router_exp_b[0])
    x1, mt, route, cnt = _outproj(x, y_rnn, y_conv, w_out[0], mod3, norm2_g[0], wr.astype(BF16), br)

    n_assign = t_all * TOP_K
    n_blocks = (n_assign + N_EXPERTS * (MOE_BLK - 1) + MOE_BLK - 1) // MOE_BLK
    counts = cnt[0, EXPERT_LANE0:EXPERT_LANE0 + N_EXPERTS].astype(jnp.int32)
    pcounts = (counts + MOE_BLK - 1) // MOE_BLK * MOE_BLK
    pends = jnp.cumsum(pcounts)
    pstarts = pends - pcounts
    experts = route[:, 0:TOP_K].astype(jnp.int32)
    ranks = route[:, 4:4 + TOP_K].astype(jnp.int32)
    onehot = experts[:, :, None] == jnp.arange(N_EXPERTS, dtype=jnp.int32)
    dest = (ranks + jnp.sum(jnp.where(onehot, pstarts, 0), axis=-1)).reshape(n_assign)
    n_used = (pends[-1] // MOE_BLK).astype(jnp.int32)
    blk_start = jnp.arange(n_blocks, dtype=jnp.int32) * MOE_BLK
    block_e = jnp.minimum(jnp.sum(blk_start[:, None] >= pends[None, :], axis=1), N_EXPERTS - 1)
    last_e = jnp.max(jnp.where(counts > 0, jnp.arange(N_EXPERTS, dtype=jnp.int32), 0))
    block_e = jnp.where(blk_start < pends[-1], block_e, last_e).astype(jnp.int32)
    eids = jnp.arange(N_EXPERTS, dtype=jnp.int32)
    used = counts > 0
    slot_of_e = (jnp.cumsum(used.astype(jnp.int32)) - 1) % 2
    later = jnp.where(used[None, :] & (eids[None, :] > eids[:, None]), eids[None, :], N_EXPERTS)
    next_of_e = jnp.min(later, axis=1)
    next_of_e = jnp.where(next_of_e == N_EXPERTS, -1, next_of_e)
    is_e = block_e[:, None] == eids[None, :]
    weight_slot = jnp.sum(jnp.where(is_e, slot_of_e[None, :], 0), axis=1).astype(jnp.int32)
    next_e = jnp.sum(jnp.where(is_e, next_of_e[None, :], 0), axis=1).astype(jnp.int32)

    n_slots = n_blocks * MOE_BLK
    slot_asg = _slotmap(dest, n_slots)
    yb = _experts(block_e, weight_slot, next_e, n_used.reshape(1), slot_asg, mt, exp_w_gate[0], exp_w_up[0],
                  exp_w_down[0], n_blocks)
    out = _combine(dest, yb, x1.reshape(t_all, d), route, mod3, final_norm_g, seq)
    return out.reshape(bn, seq, d)
```

```python
import functools

import jax
import jax.numpy as jnp
from jax import lax
from jax.experimental import pallas as pl
from jax.experimental.pallas import tpu as pltpu

F32 = jnp.float32
BF16 = jnp.bfloat16

D_MODEL = 1024
D_RNN = 512
D_CONV = 512
D_CONV_H = D_CONV // 2
RNN_HEADS = 8
RNN_HEAD_DIM = D_RNN // RNN_HEADS
GRID_W = 64
RG_C = 8.0
N_GROUPS = 4
EXPERTS_PER_GROUP = 8
N_EXPERTS = N_GROUPS * EXPERTS_PER_GROUP
TOP_K = 2
D_EXPERT = 512
NORM_EPS = 1e-6
F32_TINY = 1.1754944e-38

LANES = 128
SUBLANES = 8
ROW_TILES = D_MODEL // LANES
PACK_TILES = ROW_TILES // 2
N_LANE_GROUPS = D_RNN // LANES
EXPERT_LANE0 = N_GROUPS

MOD_ROWS = 16
MOD_TN = 768
INPROJ_TILE = 1024
OUTPROJ_TILE = 512
COEFF_ROWS = 512
MOE_BLK = 512
MXU_TILE = 256
COMBINE_TILE = 256
DMA_UNROLL = 16
VMEM_LIMIT = 48 * 1024 * 1024
EXPERT_VMEM_LIMIT = 58 * 1024 * 1024


def _dot(a, b):
    return jnp.dot(a, b, preferred_element_type=F32)


def _split_bf16(x):
    hi = x.astype(BF16)
    lo = (x - hi.astype(F32)).astype(BF16)
    return hi, lo


def _mod_kernel(cc_ref, w_ref, b_ref, o_ref):
    s = cc_ref[...]
    s = s * jax.nn.sigmoid(s)
    s_hi, s_lo = _split_bf16(s)
    w_hi, w_lo = _split_bf16(w_ref[...])
    o_ref[...] = _dot(s_hi, w_hi) + _dot(s_lo, w_hi) + _dot(s_hi, w_lo) + b_ref[...]


def _modulation(cc, ada_w, ada_b):
    n = ada_w.shape[1]
    return pl.pallas_call(
        _mod_kernel,
        grid=(n // MOD_TN,),
        in_specs=[
            pl.BlockSpec((MOD_ROWS, D_MODEL), lambda j: (0, 0)),
            pl.BlockSpec((D_MODEL, MOD_TN), lambda j: (0, j)),
            pl.BlockSpec((1, MOD_TN), lambda j: (0, j)),
        ],
        out_specs=pl.BlockSpec((MOD_ROWS, MOD_TN), lambda j: (0, j)),
        out_shape=jax.ShapeDtypeStruct((MOD_ROWS, n), F32),
        compiler_params=pltpu.CompilerParams(vmem_limit_bytes=VMEM_LIMIT),
        name="mod",
    )(cc, ada_w, ada_b.reshape(1, n))


def _norm_mod(x, g, scale, shift):
    ms = jnp.mean(x * x, axis=-1, keepdims=True)
    y = x * lax.rsqrt(ms + NORM_EPS) * g
    return y * (1.0 + scale) + shift


def _inproj_kernel(x_ref, mod_ref, g_ref, w32_ref, *refs, latent):
    out_refs, w_ref = refs[:-1], refs[-1]

    @pl.when((pl.program_id(0) == 0) & (pl.program_id(1) == 0))
    def _():
        w_ref[...] = w32_ref[...].astype(BF16)

    h = _norm_mod(x_ref[0], g_ref[...], mod_ref[0, 1:2, :], mod_ref[0, 0:1, :])
    hb = h.astype(BF16)
    xr = _dot(hb, w_ref[:, 0:D_RNN])
    out_refs[0][0] = xr
    if latent:
        o = D_RNN
        out_refs[1][0] = jax.nn.gelu(_dot(hb, w_ref[:, o:o + D_RNN]), approximate=True)
        o += D_RNN
        v = _dot(hb, w_ref[:, o:o + D_CONV])
        out_refs[3][0] = _dot(hb, w_ref[:, o + D_CONV:o + 2 * D_CONV])
        cg = _dot(hb, w_ref[:, o + 2 * D_CONV:o + 3 * D_CONV])
        out_refs[2][0] = cg * v


def _inproj(x, mod3, mod_row, norm_g, w_in, latent):
    bn, n, d = x.shape
    tm = min(INPROJ_TILE, n)
    n_out = 4 if latent else 1
    width = w_in.shape[1] if latent else D_RNN
    mod_map = (lambda b, i: (b, 0, 0)) if mod_row is None else (lambda b, i: (mod_row, 0, 0))
    return pl.pallas_call(
        functools.partial(_inproj_kernel, latent=latent),
        grid=(bn, n // tm),
        in_specs=[
            pl.BlockSpec((1, tm, d), lambda b, i: (b, i, 0)),
            pl.BlockSpec((1, 6, d), mod_map),
            pl.BlockSpec((1, d), lambda b, i: (0, 0)),
            pl.BlockSpec((d, width), lambda b, i: (0, 0), pipeline_mode=pl.Buffered(1)),
        ],
        out_specs=[pl.BlockSpec((1, tm, D_RNN), lambda b, i: (b, i, 0))] * n_out,
        out_shape=[jax.ShapeDtypeStruct((bn, n, D_RNN), F32)] * n_out,
        scratch_shapes=[pltpu.VMEM((d, width), BF16)],
        compiler_params=pltpu.CompilerParams(
            dimension_semantics=("arbitrary", "arbitrary"), vmem_limit_bytes=VMEM_LIMIT),
        name="inproj_lat" if latent else "inproj_ctx",
    )(x, mod3, norm_g.reshape(1, d), w_in)


def _shift_rows(x, k):
    n = x.shape[0]
    row = lax.broadcasted_iota(jnp.int32, x.shape, 0)
    rolled = pltpu.roll(x, k % n, axis=0)
    valid = (row >= k) if k > 0 else (row < n + k)
    return jnp.where(valid, rolled, 0.0)


def _scan_pitch(chunk):
    pitch = chunk + SUBLANES
    return pitch if (pitch // SUBLANES) % 2 else pitch + SUBLANES


def _rnn_kernel(xr_ref, xrc_ref, gr_ref, cw_ref, cb_ref, wg_ref, bg_ref, lam_ref, u_ref, bgc_ref, scw_ref,
                y_ref, yc_ref,
                xc_s, ap_f, bp_f, ap_b, bp_b, hl_f, al_f, hl_b, al_b, hp_f, hp_b, *, n_lat, n_ctx):
    _gconv_block(u_ref, bgc_ref, scw_ref, yc_ref)

    nl = -lam_ref[...]
    sp = jnp.maximum(nl, 0.0) + jnp.log1p(jnp.exp(-jnp.abs(nl)))
    c1 = (0.5 * RG_C) * sp
    cw = cw_ref[...]
    bias = cb_ref[...]
    wg = wg_ref[0]
    bg = bg_ref[0]
    dirs = ((ap_f, bp_f, hl_f, al_f, hp_f), (ap_b, bp_b, hl_b, al_b, hp_b))

    def conv_into(x, n):
        xc_s[pl.ds(0, n), :] = (cw[0:1] * _shift_rows(x, 2) + cw[1:2] * _shift_rows(x, 1)
                                + cw[2:3] * x + cw[3:4] * _shift_rows(x, -1)) + bias

    def coefficients(n):
        chunk = n // SUBLANES
        pitch = _scan_pitch(chunk)
        rows = max(chunk, min(n, COEFF_ROWS))
        per = rows // chunk

        def body(i, carry):
            xc = xc_s[pl.ds(pl.multiple_of(i * rows, SUBLANES), rows), :]
            gates = _dot(xc.astype(BF16), wg) + bg
            half_xc = 0.5 * xc
            for d in range(2):
                tr = jnp.tanh(gates[:, (2 * d) * LANES:(2 * d + 1) * LANES])
                ti = jnp.tanh(gates[:, (2 * d + 1) * LANES:(2 * d + 2) * LANES])
                neg_log_a = c1[d:d + 1] + c1[d:d + 1] * tr
                a = jnp.exp(-neg_log_a)
                y = jnp.tanh(neg_log_a) * (a * a + 1.0)
                b = (y * lax.rsqrt(jnp.maximum(y, F32_TINY))) * (half_xc + half_xc * ti)
                for k in range(per):
                    dst = pl.multiple_of((i * per + k) * pitch, SUBLANES)
                    dirs[d][0][pl.ds(dst, chunk), :] = a[k * chunk:(k + 1) * chunk]
                    dirs[d][1][pl.ds(dst, chunk), :] = b[k * chunk:(k + 1) * chunk]
            return carry

        if n == rows:
            body(0, 0)
        else:
            lax.fori_loop(0, n // rows, body, 0)

    def scan(n, h0_f, h0_b, keep):
        chunk = n // SUBLANES
        pitch = _scan_pitch(chunk)

        def steps(jo, carry):
            h_f, a_f, h_b, a_b = carry
            for u in range(SUBLANES):
                j = jo * SUBLANES + u
                av = ap_f[pl.ds(j, SUBLANES, stride=pitch), :]
                h_f = av * h_f + bp_f[pl.ds(j, SUBLANES, stride=pitch), :]
                a_f = av * a_f
                jb = chunk - 1 - j
                av = ap_b[pl.ds(jb, SUBLANES, stride=pitch), :]
                h_b = av * h_b + bp_b[pl.ds(jb, SUBLANES, stride=pitch), :]
                a_b = av * a_b
                if keep:
                    o = pl.multiple_of(j * SUBLANES, SUBLANES)
                    hl_f[pl.ds(o, SUBLANES), :] = h_f
                    al_f[pl.ds(o, SUBLANES), :] = a_f
                    hl_b[pl.ds(o, SUBLANES), :] = h_b
                    al_b[pl.ds(o, SUBLANES), :] = a_b
            return h_f, a_f, h_b, a_b

        zeros = jnp.zeros((SUBLANES, LANES), F32)
        ones = jnp.ones((SUBLANES, LANES), F32)
        h_f, a_f, h_b, a_b = lax.fori_loop(0, chunk // SUBLANES, steps, (zeros, ones, zeros, ones))

        in_f = [h0_f]
        for c in range(SUBLANES):
            in_f.append(a_f[c:c + 1] * in_f[c] + h_f[c:c + 1])
        in_b = [h0_b]
        for c in range(SUBLANES - 1, -1, -1):
            in_b.append(a_b[c:c + 1] * in_b[-1] + h_b[c:c + 1])
        if keep:
            hin_f = jnp.concatenate(in_f[:SUBLANES], axis=0)
            hin_b = jnp.concatenate(in_b[SUBLANES - 1::-1], axis=0)

            def fix(jo, carry):
                for u in range(SUBLANES):
                    j = jo * SUBLANES + u
                    o = pl.multiple_of(j * SUBLANES, SUBLANES)
                    hp_f[pl.ds(j, SUBLANES, stride=pitch), :] = (
                        hl_f[pl.ds(o, SUBLANES), :] + al_f[pl.ds(o, SUBLANES), :] * hin_f)
                    hp_b[pl.ds(chunk - 1 - j, SUBLANES, stride=pitch), :] = (
                        hl_b[pl.ds(o, SUBLANES), :] + al_b[pl.ds(o, SUBLANES), :] * hin_b)
                return carry
            lax.fori_loop(0, chunk // SUBLANES, fix, 0)
        return in_f[SUBLANES], in_b[SUBLANES]

    zero = jnp.zeros((1, LANES), F32)
    conv_into(xrc_ref[0], n_ctx)
    coefficients(n_ctx)
    h0_f, h0_b = scan(n_ctx, zero, zero, keep=False)

    conv_into(xr_ref[0], n_lat)
    coefficients(n_lat)
    scan(n_lat, h0_f, h0_b, keep=True)

    chunk = n_lat // SUBLANES
    pitch = _scan_pitch(chunk)

    def emit(c, carry):
        src = pl.multiple_of(c * chunk, 2 * SUBLANES)
        dst = pl.multiple_of(c * pitch, SUBLANES)
        hsum = hp_f[pl.ds(dst, chunk), :] + hp_b[pl.ds(dst, chunk), :]
        y = gr_ref[0, pl.ds(src, chunk), :] * hsum
        y_ref[0, pl.ds(src, chunk), :] = y.astype(y_ref.dtype)
        return carry
    lax.fori_loop(0, SUBLANES, emit, 0)


def _mixers(xr, xr_c, gr, conv_w, conv_b, wg, bgate, lam, u, bg, sc_w):
    bn, n, _ = xr.shape
    n_ctx = xr_c.shape[1]
    assert n % (SUBLANES * SUBLANES) == 0 and n_ctx % (SUBLANES * SUBLANES) == 0 and n_ctx <= n
    pitched = SUBLANES * _scan_pitch(n // SUBLANES)
    seq_spec = pl.BlockSpec((1, n, LANES), lambda b, p: (b, 0, p))
    return pl.pallas_call(
        functools.partial(_rnn_kernel, n_lat=n, n_ctx=n_ctx),
        grid=(bn, N_LANE_GROUPS),
        in_specs=[
            seq_spec,
            pl.BlockSpec((1, n_ctx, LANES), lambda b, p: (b, 0, p)),
            seq_spec,
            pl.BlockSpec((4, LANES), lambda b, p: (0, p)),
            pl.BlockSpec((1, LANES), lambda b, p: (0, p)),
            pl.BlockSpec((1, LANES, 4 * LANES), lambda b, p: (p, 0, 0)),
            pl.BlockSpec((1, 1, 4 * LANES), lambda b, p: (p, 0, 0)),
            pl.BlockSpec((2, LANES), lambda b, p: (0, p)),
            seq_spec,
            seq_spec,
            pl.BlockSpec((3, LANES), lambda b, p: (0, p)),
        ],
        out_specs=[seq_spec, seq_spec],
        out_shape=[jax.ShapeDtypeStruct((bn, n, D_RNN), BF16)] * 2,
        scratch_shapes=[pltpu.VMEM((n, LANES), F32)]
        + [pltpu.VMEM((pitched, LANES), F32)] * 4
        + [pltpu.VMEM((n, LANES), F32)] * 4
        + [pltpu.VMEM((pitched, LANES), F32)] * 2,
        compiler_params=pltpu.CompilerParams(vmem_limit_bytes=VMEM_LIMIT),
        name="mixers",
    )(xr, xr_c, gr, conv_w, conv_b.reshape(1, D_RNN), wg, bgate, lam, u, bg, sc_w)


def _gate_weights(rg_wa, rg_ba, rg_wx, rg_bx):
    eye = jnp.eye(2, dtype=F32)
    blocks, biases = [], []
    for d in range(2):
        for w, bvec in ((rg_wa[d], rg_ba[d]), (rg_wx[d], rg_bx[d])):
            w4 = w.reshape(N_LANE_GROUPS, 2, RNN_HEAD_DIM, RNN_HEAD_DIM)
            bd = jnp.einsum("paij,ac->paicj", w4, eye).reshape(N_LANE_GROUPS, LANES, LANES)
            blocks.append(0.5 * bd)
            biases.append(0.5 * bvec.reshape(N_LANE_GROUPS, 1, LANES))
    return jnp.concatenate(blocks, axis=-1).astype(BF16), jnp.concatenate(biases, axis=-1)


def _gconv_block(u_ref, bg_ref, w_ref, y_ref):
    p = pl.program_id(1)
    u = u_ref[0]
    w = w_ref[...]

    @pl.when(p < D_CONV_H // LANES)
    def _():
        col = lax.broadcasted_iota(jnp.int32, u.shape, 0) % GRID_W
        left = jnp.where(col > 0, _shift_rows(u, 1), 0.0)
        right = jnp.where(col < GRID_W - 1, _shift_rows(u, -1), 0.0)
        y_ref[0] = (bg_ref[0] * (w[0:1] * left + w[1:2] * u + w[2:3] * right)).astype(y_ref.dtype)

    @pl.when(p >= D_CONV_H // LANES)
    def _():
        y_ref[0] = (bg_ref[0] * (w[0:1] * _shift_rows(u, GRID_W) + w[1:2] * u
                                 + w[2:3] * _shift_rows(u, -GRID_W))).astype(y_ref.dtype)


def _lane_max(x, mask):
    return jnp.max(jnp.where(mask, x, -jnp.inf), axis=-1, keepdims=True)


def _first_lane(cond, lane):
    return jnp.min(jnp.where(cond, lane, float(LANES)), axis=-1, keepdims=True)


def _outproj_kernel(x_ref, yr_ref, yc_ref, w32_ref, mod_ref, g_ref, wr_ref, br_ref,
                    x1_ref, mt_ref, route_ref, cnt_ref, carry, w_ref, m_s, *, tm):
    s = pl.program_id(0)

    @pl.when(s == 0)
    def _():
        carry[...] = jnp.zeros_like(carry)
        w_ref[...] = w32_ref[...].astype(BF16)
        m_s[...] = jnp.zeros_like(m_s)

    logits = _dot(m_s[...], wr_ref[...]) + br_ref[...]
    lane_i = lax.broadcasted_iota(jnp.int32, logits.shape, 1)
    lane = lane_i.astype(F32)
    is_grp = lane_i < N_GROUPS
    g_max = _lane_max(logits, is_grp)
    grp = _first_lane(is_grp & (logits == g_max), lane)
    p_g = 1.0 / jnp.sum(jnp.where(is_grp, jnp.exp(logits - g_max), 0.0), axis=-1, keepdims=True)
    lo_lane = EXPERT_LANE0 + grp * EXPERTS_PER_GROUP
    in_grp = (lane >= lo_lane) & (lane < lo_lane + EXPERTS_PER_GROUP)
    l1 = _lane_max(logits, in_grp)
    i1 = _first_lane(in_grp & (logits == l1), lane)
    rest = in_grp & (lane != i1)
    l2 = _lane_max(logits, rest)
    i2 = _first_lane(rest & (logits == l2), lane)
    r21 = jnp.exp(l2 - l1)
    gate1 = p_g / (1.0 + r21)
    gate2 = gate1 * r21

    oh1 = jnp.where(lane == i1, 1.0, 0.0)
    oh2 = jnp.where(lane == i2, 1.0, 0.0)
    both = (oh1 + oh2).astype(BF16)
    ti = lax.broadcasted_iota(jnp.int32, (tm, tm), 0)
    tj = lax.broadcasted_iota(jnp.int32, (tm, tm), 1)
    tri = jnp.where(tj < ti, 1.0, 0.0).astype(BF16)
    counts = carry[...]
    before = _dot(tri, both) + counts
    rank1 = jnp.sum(oh1 * before, axis=-1, keepdims=True)
    rank2 = jnp.sum(oh2 * before, axis=-1, keepdims=True)
    out = jnp.zeros(logits.shape, F32)
    for k, val in enumerate((i1 - EXPERT_LANE0, i2 - EXPERT_LANE0, gate1, gate2, rank1, rank2)):
        out = jnp.where(lane_i == k, val, out)
    route_ref[...] = out
    real = jnp.where(s > 0, 1.0, 0.0)
    total = counts + real * jnp.sum(oh1 + oh2, axis=0, keepdims=True)
    carry[...] = total
    cnt_ref[...] = total

    mix = _dot(yr_ref[0], w_ref[0:D_RNN, :]) + _dot(yc_ref[0], w_ref[D_RNN:, :])
    x1 = x_ref[0] + mod_ref[0, 2:3, :] * mix
    x1_ref[0] = x1
    m_new = _norm_mod(x1, g_ref[...], mod_ref[0, 4:5, :], mod_ref[0, 3:4, :])
    m_s[...] = m_new.astype(BF16)
    half = D_MODEL // 2
    packed = pltpu.pack_elementwise([m_new[:, :half], m_new[:, half:]], packed_dtype=BF16)
    for q in range(PACK_TILES):
        mt_ref[pl.ds(q, tm, stride=PACK_TILES), :] = packed[:, q * LANES:(q + 1) * LANES]


def _outproj(x, y_rnn, y_conv, w_out, mod3, norm_g, wr, br):
    bn, n, d = x.shape
    tm = min(OUTPROJ_TILE, n)
    nt = n // tm
    n_tiles = bn * nt
    t_all = bn * n

    def cur(s):
        return jnp.minimum(s, n_tiles - 1)

    def prev(s):
        return jnp.maximum(s - 1, 0)

    def seq_map(s):
        return (cur(s) // nt, cur(s) % nt, 0)

    const = lambda s: (0, 0)
    return pl.pallas_call(
        functools.partial(_outproj_kernel, tm=tm),
        grid=(n_tiles + 1,),
        in_specs=[
            pl.BlockSpec((1, tm, d), seq_map),
            pl.BlockSpec((1, tm, D_RNN), seq_map),
            pl.BlockSpec((1, tm, D_CONV), seq_map),
            pl.BlockSpec((D_RNN + D_CONV, d), const, pipeline_mode=pl.Buffered(1)),
            pl.BlockSpec((1, 6, d), lambda s: (cur(s) // nt, 0, 0)),
            pl.BlockSpec((1, d), const),
            pl.BlockSpec((d, LANES), const),
            pl.BlockSpec((1, LANES), const),
        ],
        out_specs=[
            pl.BlockSpec((1, tm, d), seq_map),
            pl.BlockSpec((tm * PACK_TILES, LANES), lambda s: (cur(s), 0)),
            pl.BlockSpec((tm, LANES), lambda s: (prev(s), 0)),
            pl.BlockSpec((1, LANES), const),
        ],
        out_shape=[
            jax.ShapeDtypeStruct((bn, n, d), F32),
            jax.ShapeDtypeStruct((t_all * PACK_TILES, LANES), jnp.uint32),
            jax.ShapeDtypeStruct((t_all, LANES), F32),
            jax.ShapeDtypeStruct((1, LANES), F32),
        ],
        scratch_shapes=[pltpu.VMEM((1, LANES), F32), pltpu.VMEM((D_RNN + D_CONV, d), BF16),
                        pltpu.VMEM((tm, d), BF16)],
        compiler_params=pltpu.CompilerParams(
            dimension_semantics=("arbitrary",), vmem_limit_bytes=VMEM_LIMIT),
        name="outproj",
    )(x, y_rnn, y_conv, w_out, mod3, norm_g.reshape(1, d), wr, br)


def _row_tile(ref, row):
    return ref.at[pl.ds(pl.multiple_of(row * ROW_TILES, ROW_TILES), ROW_TILES)]


def _slotmap_kernel(dest_ref, zeros_hbm, asg_ref, sem):
    fill = pltpu.make_async_copy(zeros_hbm, asg_ref, sem)
    fill.start()
    fill.wait()

    def body(c, carry):
        for u in range(DMA_UNROLL):
            a = c * DMA_UNROLL + u
            asg_ref[dest_ref[a]] = a
        return carry
    lax.fori_loop(0, dest_ref.shape[0] // DMA_UNROLL, body, 0)


def _slotmap(dest, n_slots):
    return pl.pallas_call(
        _slotmap_kernel,
        in_specs=[pl.BlockSpec(memory_space=pltpu.SMEM), pl.BlockSpec(memory_space=pl.ANY)],
        out_specs=pl.BlockSpec(memory_space=pltpu.SMEM),
        out_shape=jax.ShapeDtypeStruct((n_slots,), jnp.int32),
        scratch_shapes=[pltpu.SemaphoreType.DMA],
        name="slotmap",
    )(dest, jnp.zeros((n_slots,), jnp.int32))


def _expert_kernel(be_ref, ws_ref, ne_ref, bv_ref, nu_ref, asg_hbm, m_ref, wg_hbm, wu_hbm, wd_hbm, yb_ref,
                   xbuf_a, xbuf_b, idx, isems, wbuf_g, wbuf_u, wbuf_d, wsems, wg_s, wu_s, wd_s):
    j = pl.program_id(0)
    n_used = nu_ref[0]
    last = n_used - 1

    def idx_copy(blk, sl):
        return pltpu.make_async_copy(asg_hbm.at[blk], idx.at[sl], isems.at[sl])

    def copy_rows(buf, sl, r0, n):
        for u in range(n):
            tok = lax.shift_right_logical(idx[sl, 0, r0 + u], 1)
            src = pl.multiple_of(tok * PACK_TILES, PACK_TILES)
            buf[pl.ds((r0 + u) * PACK_TILES, PACK_TILES), :] = m_ref[pl.ds(src, PACK_TILES), :]

    def unpack(buf, rows):
        halves = ([], [])
        for s in range(PACK_TILES):
            word = buf[pl.ds(s, rows, stride=PACK_TILES), :]
            for k in range(2):
                part = pltpu.unpack_elementwise(word, index=k, packed_dtype=BF16, unpacked_dtype=F32)
                halves[k].append(part.astype(BF16))
        return jnp.concatenate(halves[0] + halves[1], axis=-1)

    n_pieces = 2 * D_EXPERT // MXU_TILE + D_MODEL // MXU_TILE
    bounds = [(p * MOE_BLK) // n_pieces for p in range(n_pieces + 1)]

    def compute(cur, nxt, nxt_sl, rows):
        pieces = iter(zip(bounds[:-1], bounds[1:]))

        def dot_pieces(a, w_ref, n0):
            acc = _dot(a, w_ref[:, n0:n0 + MXU_TILE])
            r0, r1 = next(pieces)
            copy_rows(nxt, nxt_sl, r0, r1 - r0)
            return acc

        xb16 = unpack(cur, rows)
        acts = []
        for n0 in range(0, D_EXPERT, MXU_TILE):
            gate = dot_pieces(xb16, wg_s, n0)
            up = dot_pieces(xb16, wu_s, n0)
            acts.append(((gate * jax.nn.sigmoid(gate)) * up).astype(BF16))
        h = jnp.concatenate(acts, axis=-1)
        for n0 in range(0, D_MODEL, MXU_TILE):
            y = dot_pieces(h, wd_s, n0)
            for s in range(MXU_TILE // LANES):
                yb_ref[pl.ds(n0 // LANES + s, rows, stride=ROW_TILES), :] = (
                    y[:, s * LANES:(s + 1) * LANES])
        if rows < MOE_BLK:
            yb_ref[pl.ds(rows * ROW_TILES, (MOE_BLK - rows) * ROW_TILES), :] = jnp.zeros(
                ((MOE_BLK - rows) * ROW_TILES, LANES), F32)

    @pl.when(j >= n_used)
    def _():
        yb_ref[...] = jnp.zeros_like(yb_ref)

    @pl.when(j < n_used)
    def _():
        slot = j % 2
        other = 1 - slot

        @pl.when(j == 0)
        def _():
            idx_copy(0, 0).start()
            idx_copy(0, 0).wait()

            def body(c, carry):
                copy_rows(xbuf_a, 0, c * DMA_UNROLL, DMA_UNROLL)
                return carry
            lax.fori_loop(0, MOE_BLK // DMA_UNROLL, body, 0)
            idx_copy(jnp.minimum(1, last), 1).start()

        e = be_ref[j]
        wslot = ws_ref[j]

        def weight_copies(expert, sl):
            return [pltpu.make_async_copy(src.at[expert], dst.at[sl], wsems.at[sl])
                    for src, dst in ((wg_hbm, wbuf_g), (wu_hbm, wbuf_u), (wd_hbm, wbuf_d))]

        @pl.when(j == 0)
        def _():
            for cp in weight_copies(e, wslot):
                cp.start()

        @pl.when((j == 0) | (e != be_ref[jnp.maximum(j - 1, 0)]))
        def _():
            for cp in weight_copies(e, wslot):
                cp.wait()
            wg_s[...] = wbuf_g[wslot].astype(BF16)
            wu_s[...] = wbuf_u[wslot].astype(BF16)
            wd_s[...] = wbuf_d[wslot].astype(BF16)

            @pl.when(ne_ref[j] >= 0)
            def _():
                for cp in weight_copies(ne_ref[j], 1 - wslot):
                    cp.start()

        idx_copy(0, other).wait()

        short = bv_ref[j] <= MOE_BLK // 2
        for par, (cur, nxt) in enumerate(((xbuf_a, xbuf_b), (xbuf_b, xbuf_a))):
            @pl.when((slot == par) & jnp.logical_not(short))
            def _(cur=cur, nxt=nxt, par=par):
                compute(cur, nxt, 1 - par, MOE_BLK)

            @pl.when((slot == par) & short)
            def _(cur=cur, nxt=nxt, par=par):
                compute(cur, nxt, 1 - par, MOE_BLK // 2)

        @pl.when(j < last)
        def _():
            idx_copy(jnp.minimum(j + 2, last), slot).start()


def _experts(block_e, weight_slot, next_e, block_valid, n_used, slot_asg, mt, w_gate, w_up, w_down, n_blocks):
    return pl.pallas_call(
        _expert_kernel,
        grid_spec=pltpu.PrefetchScalarGridSpec(
            num_scalar_prefetch=5,
            grid=(n_blocks,),
            in_specs=[
                pl.BlockSpec(memory_space=pl.ANY),
                pl.BlockSpec(memory_space=pltpu.VMEM),
                pl.BlockSpec(memory_space=pl.ANY),
                pl.BlockSpec(memory_space=pl.ANY),
                pl.BlockSpec(memory_space=pl.ANY),
            ],
            out_specs=pl.BlockSpec((MOE_BLK * ROW_TILES, LANES), lambda j, *_: (j, 0)),
            scratch_shapes=[
                pltpu.VMEM((MOE_BLK * PACK_TILES, LANES), jnp.uint32),
                pltpu.VMEM((MOE_BLK * PACK_TILES, LANES), jnp.uint32),
                pltpu.SMEM((2, 1, MOE_BLK), jnp.int32),
                pltpu.SemaphoreType.DMA((2,)),
                pltpu.VMEM((2, D_MODEL, D_EXPERT), F32),
                pltpu.VMEM((2, D_MODEL, D_EXPERT), F32),
                pltpu.VMEM((2, D_EXPERT, D_MODEL), F32),
                pltpu.SemaphoreType.DMA((2,)),
                pltpu.VMEM((D_MODEL, D_EXPERT), BF16),
                pltpu.VMEM((D_MODEL, D_EXPERT), BF16),
                pltpu.VMEM((D_EXPERT, D_MODEL), BF16),
            ],
        ),
        out_shape=jax.ShapeDtypeStruct((n_blocks * MOE_BLK * ROW_TILES, LANES), F32),
        compiler_params=pltpu.CompilerParams(
            dimension_semantics=("arbitrary",), vmem_limit_bytes=EXPERT_VMEM_LIMIT),
        name="expert",
    )(block_e, weight_slot, next_e, block_valid, n_used, slot_asg.reshape(n_blocks, 1, MOE_BLK), mt, w_gate,
      w_up, w_down)


def _combine_kernel(dest_ref, yb_hbm, x1_ref, route_ref, mod_ref, g_ref, o_ref, ybuf, sems, *, tc):
    i = pl.program_id(0)
    slot = i % 2

    def row_copy(d, sl, k, r):
        return pltpu.make_async_copy(_row_tile(yb_hbm, d), _row_tile(ybuf.at[sl, k], r), sems.at[sl])

    def gather(step, sl):
        def issue(c, carry):
            for u in range(DMA_UNROLL):
                r = c * DMA_UNROLL + u
                for k in range(TOP_K):
                    row_copy(dest_ref[TOP_K * (step * tc + r) + k], sl, k, r).start(priority=k)
            return carry
        lax.fori_loop(0, tc // DMA_UNROLL, issue, 0)

    @pl.when(i == 0)
    def _():
        gather(0, 0)

    @pl.when(i + 1 < pl.num_programs(0))
    def _():
        gather(i + 1, 1 - slot)

    def drain(c, carry):
        for u in range(DMA_UNROLL * TOP_K):
            row_copy(0, slot, 0, 0).wait()
        return carry
    lax.fori_loop(0, tc // DMA_UNROLL, drain, 0)

    def rows(k):
        return jnp.concatenate(
            [ybuf[slot, k, pl.ds(s, tc, stride=ROW_TILES), :] for s in range(ROW_TILES)], axis=-1)

    route = route_ref[...]
    y = route[:, 2:3] * rows(0) + route[:, 3:4] * rows(1)
    x2 = x1_ref[...] + mod_ref[0, 5:6, :] * y
    ms = jnp.mean(x2 * x2, axis=-1, keepdims=True)
    o_ref[...] = x2 * lax.rsqrt(ms + NORM_EPS) * g_ref[...]


def _combine(dest, yb, x1_2d, route, mod3, final_g, seq):
    t_all, d = x1_2d.shape
    tc = min(COMBINE_TILE, seq)
    per_seq = seq // tc
    return pl.pallas_call(
        functools.partial(_combine_kernel, tc=tc),
        grid_spec=pltpu.PrefetchScalarGridSpec(
            num_scalar_prefetch=1,
            grid=(t_all // tc,),
            in_specs=[
                pl.BlockSpec(memory_space=pl.ANY),
                pl.BlockSpec((tc, d), lambda i, dest: (i, 0)),
                pl.BlockSpec((tc, LANES), lambda i, dest: (i, 0)),
                pl.BlockSpec((1, 6, d), lambda i, dest: (i // per_seq, 0, 0)),
                pl.BlockSpec((1, d), lambda i, dest: (0, 0)),
            ],
            out_specs=pl.BlockSpec((tc, d), lambda i, dest: (i, 0)),
            scratch_shapes=[
                pltpu.VMEM((2, TOP_K, tc * ROW_TILES, LANES), F32),
                pltpu.SemaphoreType.DMA((2,)),
            ],
        ),
        out_shape=jax.ShapeDtypeStruct((t_all, d), F32),
        compiler_params=pltpu.CompilerParams(
            dimension_semantics=("arbitrary",), vmem_limit_bytes=VMEM_LIMIT),
        name="combine",
    )(dest, yb, x1_2d, route, mod3, final_g.reshape(1, d))


def kernel(x, c, ctx, c_ctx, ada_w, ada_b, norm1_g, norm2_g, w_in, rnn_conv_w, rnn_conv_b, rg_wa, rg_ba,
           rg_wx, rg_bx, rg_lambda, sc_conv_w, w_out, router_group_w, router_group_b, router_exp_w,
           router_exp_b, exp_w_gate, exp_w_up, exp_w_down, final_norm_g):
    bn, seq, d = x.shape
    assert d == D_MODEL and bn < MOD_ROWS and ada_w.shape[0] == 1
    t_all = bn * seq

    cc = jnp.concatenate([c, c_ctx[None], jnp.zeros((MOD_ROWS - bn - 1, d), F32)], axis=0)
    mod3 = _modulation(cc, ada_w[0], ada_b[0]).reshape(MOD_ROWS, 6, d)

    xr, gr, u, bg = _inproj(x, mod3, None, norm1_g[0], w_in[0], latent=True)
    (xr_c,) = _inproj(ctx, mod3, bn, norm1_g[0], w_in[0], latent=False)

    wg, bgate = _gate_weights(rg_wa[0], rg_ba[0], rg_wx[0], rg_bx[0])
    assert D_RNN == D_CONV
    y_rnn, y_conv = _mixers(xr, xr_c, gr, rnn_conv_w[0], rnn_conv_b[0], wg, bgate, rg_lambda[0], u, bg,
                            sc_conv_w[0])

    wr = jnp.zeros((d, LANES), F32)
    wr = wr.at[:, :N_GROUPS].set(router_group_w[0]).at[:, EXPERT_LANE0:EXPERT_LANE0 + N_EXPERTS].set(router_exp_w[0])
    br = jnp.zeros((1, LANES), F32)
    br = br.at[0, :N_GROUPS].set(router_group_b[0]).at[0, EXPERT_LANE0:EXPERT_LANE0 + N_EXPERTS].set(router_exp_b[0])
    x1, mt, route, cnt = _outproj(x, y_rnn, y_conv, w_out[0], mod3, norm2_g[0], wr.astype(BF16), br)

    n_assign = t_all * TOP_K
    n_blocks = (n_assign + N_EXPERTS * (MOE_BLK - 1) + MOE_BLK - 1) // MOE_BLK
    counts = cnt[0, EXPERT_LANE0:EXPERT_LANE0 + N_EXPERTS].astype(jnp.int32)
    pcounts = (counts + MOE_BLK - 1) // MOE_BLK * MOE_BLK
    pends = jnp.cumsum(pcounts)
    pstarts = pends - pcounts
    experts = route[:, 0:TOP_K].astype(jnp.int32)
    ranks = route[:, 4:4 + TOP_K].astype(jnp.int32)
    onehot = experts[:, :, None] == jnp.arange(N_EXPERTS, dtype=jnp.int32)
    dest = (ranks + jnp.sum(jnp.where(onehot, pstarts, 0), axis=-1)).reshape(n_assign)
    n_used = (pends[-1] // MOE_BLK).astype(jnp.int32)
    blk_start = jnp.arange(n_blocks, dtype=jnp.int32) * MOE_BLK
    block_e = jnp.minimum(jnp.sum(blk_start[:, None] >= pends[None, :], axis=1), N_EXPERTS - 1)
    last_e = jnp.max(jnp.where(counts > 0, jnp.arange(N_EXPERTS, dtype=jnp.int32), 0))
    block_e = jnp.where(blk_start < pends[-1], block_e, last_e).astype(jnp.int32)
    eids = jnp.arange(N_EXPERTS, dtype=jnp.int32)
    used = counts > 0
    slot_of_e = (jnp.cumsum(used.astype(jnp.int32)) - 1) % 2
    later = jnp.where(used[None, :] & (eids[None, :] > eids[:, None]), eids[None, :], N_EXPERTS)
    next_of_e = jnp.min(later, axis=1)
    next_of_e = jnp.where(next_of_e == N_EXPERTS, -1, next_of_e)
    is_e = block_e[:, None] == eids[None, :]
    weight_slot = jnp.sum(jnp.where(is_e, slot_of_e[None, :], 0), axis=1).astype(jnp.int32)
    next_e = jnp.sum(jnp.where(is_e, next_of_e[None, :], 0), axis=1).astype(jnp.int32)
    rows_end = jnp.sum(jnp.where(is_e, (pstarts + counts)[None, :], 0), axis=1)
    block_valid = jnp.clip(rows_end - blk_start, 0, MOE_BLK).astype(jnp.int32)

    n_slots = n_blocks * MOE_BLK
    slot_asg = _slotmap(dest, n_slots)
    yb = _experts(block_e, weight_slot, next_e, block_valid, n_used.reshape(1), slot_asg, mt, exp_w_gate[0],
                  exp_w_up[0], exp_w_down[0], n_blocks)
    out = _combine(dest, yb, x1.reshape(t_all, d), route, mod3, final_norm_g, seq)
    return out.reshape(bn, seq, d)
```

```python
import functools

import jax
import jax.numpy as jnp
from jax import lax
from jax.experimental import pallas as pl
from jax.experimental.pallas import tpu as pltpu

F32 = jnp.float32
BF16 = jnp.bfloat16

D_MODEL = 1024
D_RNN = 512
D_CONV = 512
D_CONV_H = D_CONV // 2
RNN_HEADS = 8
RNN_HEAD_DIM = D_RNN // RNN_HEADS
GRID_W = 64
RG_C = 8.0
N_GROUPS = 4
EXPERTS_PER_GROUP = 8
N_EXPERTS = N_GROUPS * EXPERTS_PER_GROUP
TOP_K = 2
D_EXPERT = 512
NORM_EPS = 1e-6
F32_TINY = 1.1754944e-38

LANES = 128
SUBLANES = 8
ROW_TILES = D_MODEL // LANES
PACK_TILES = ROW_TILES // 2
N_LANE_GROUPS = D_RNN // LANES
EXPERT_LANE0 = N_GROUPS

MOD_ROWS = 16
MOD_TN = 768
INPROJ_TILE = 1024
OUTPROJ_TILE = 512
COEFF_ROWS = 2048
MOE_BLK = 512
MXU_TILE = 256
COMBINE_TILE = 256
DMA_UNROLL = 16
VMEM_LIMIT = 48 * 1024 * 1024
EXPERT_VMEM_LIMIT = 58 * 1024 * 1024


def _dot(a, b):
    return jnp.dot(a, b, preferred_element_type=F32)


def _split_bf16(x):
    hi = x.astype(BF16)
    lo = (x - hi.astype(F32)).astype(BF16)
    return hi, lo


def _mod_kernel(cc_ref, w_ref, b_ref, o_ref):
    s = cc_ref[...]
    s = s * jax.nn.sigmoid(s)
    s_hi, s_lo = _split_bf16(s)
    w_hi, w_lo = _split_bf16(w_ref[...])
    o_ref[...] = _dot(s_hi, w_hi) + _dot(s_lo, w_hi) + _dot(s_hi, w_lo) + b_ref[...]


def _modulation(cc, ada_w, ada_b):
    n = ada_w.shape[1]
    return pl.pallas_call(
        _mod_kernel,
        grid=(n // MOD_TN,),
        in_specs=[
            pl.BlockSpec((MOD_ROWS, D_MODEL), lambda j: (0, 0)),
            pl.BlockSpec((D_MODEL, MOD_TN), lambda j: (0, j)),
            pl.BlockSpec((1, MOD_TN), lambda j: (0, j)),
        ],
        out_specs=pl.BlockSpec((MOD_ROWS, MOD_TN), lambda j: (0, j)),
        out_shape=jax.ShapeDtypeStruct((MOD_ROWS, n), F32),
        compiler_params=pltpu.CompilerParams(vmem_limit_bytes=VMEM_LIMIT),
        name="mod",
    )(cc, ada_w, ada_b.reshape(1, n))


def _norm_mod(x, g, scale, shift):
    ms = jnp.mean(x * x, axis=-1, keepdims=True)
    y = x * lax.rsqrt(ms + NORM_EPS) * g
    return y * (1.0 + scale) + shift


def _inproj_kernel(x_ref, mod_ref, g_ref, w32_ref, *refs, latent):
    out_refs, w_ref = refs[:-1], refs[-1]

    @pl.when((pl.program_id(0) == 0) & (pl.program_id(1) == 0))
    def _():
        w_ref[...] = w32_ref[...].astype(BF16)

    h = _norm_mod(x_ref[0], g_ref[...], mod_ref[0, 1:2, :], mod_ref[0, 0:1, :])
    hb = h.astype(BF16)
    xr = _dot(hb, w_ref[:, 0:D_RNN])
    out_refs[0][0] = xr
    if latent:
        o = D_RNN
        out_refs[1][0] = jax.nn.gelu(_dot(hb, w_ref[:, o:o + D_RNN]), approximate=True)
        o += D_RNN
        v = _dot(hb, w_ref[:, o:o + D_CONV])
        out_refs[3][0] = _dot(hb, w_ref[:, o + D_CONV:o + 2 * D_CONV])
        cg = _dot(hb, w_ref[:, o + 2 * D_CONV:o + 3 * D_CONV])
        out_refs[2][0] = cg * v


def _inproj(x, mod3, mod_row, norm_g, w_in, latent):
    bn, n, d = x.shape
    tm = min(INPROJ_TILE, n)
    n_out = 4 if latent else 1
    width = w_in.shape[1] if latent else D_RNN
    mod_map = (lambda b, i: (b, 0, 0)) if mod_row is None else (lambda b, i: (mod_row, 0, 0))
    return pl.pallas_call(
        functools.partial(_inproj_kernel, latent=latent),
        grid=(bn, n // tm),
        in_specs=[
            pl.BlockSpec((1, tm, d), lambda b, i: (b, i, 0)),
            pl.BlockSpec((1, 6, d), mod_map),
            pl.BlockSpec((1, d), lambda b, i: (0, 0)),
            pl.BlockSpec((d, width), lambda b, i: (0, 0), pipeline_mode=pl.Buffered(1)),
        ],
        out_specs=[pl.BlockSpec((1, tm, D_RNN), lambda b, i: (b, i, 0))] * n_out,
        out_shape=[jax.ShapeDtypeStruct((bn, n, D_RNN), F32)] * n_out,
        scratch_shapes=[pltpu.VMEM((d, width), BF16)],
        compiler_params=pltpu.CompilerParams(
            dimension_semantics=("arbitrary", "arbitrary"), vmem_limit_bytes=VMEM_LIMIT),
        name="inproj_lat" if latent else "inproj_ctx",
    )(x, mod3, norm_g.reshape(1, d), w_in)


def _shift_rows(x, k):
    n = x.shape[0]
    row = lax.broadcasted_iota(jnp.int32, x.shape, 0)
    rolled = pltpu.roll(x, k % n, axis=0)
    valid = (row >= k) if k > 0 else (row < n + k)
    return jnp.where(valid, rolled, 0.0)


def _scan_pitch(chunk):
    pitch = chunk + SUBLANES
    return pitch if (pitch // SUBLANES) % 2 else pitch + SUBLANES


def _rnn_kernel(xr_ref, xrc_ref, gr_ref, cw_ref, cb_ref, wg_ref, bg_ref, lam_ref, u_ref, bgc_ref, scw_ref,
                y_ref, yc_ref,
                xc_s, ap_f, bp_f, ap_b, bp_b, hl_f, al_f, hl_b, al_b, hp_f, hp_b, *, n_lat, n_ctx):
    _gconv_block(u_ref, bgc_ref, scw_ref, yc_ref)

    nl = -lam_ref[...]
    sp = jnp.maximum(nl, 0.0) + jnp.log1p(jnp.exp(-jnp.abs(nl)))
    c1 = (0.5 * RG_C) * sp
    cw = cw_ref[...]
    bias = cb_ref[...]
    wg = wg_ref[0]
    bg = bg_ref[0]
    dirs = ((ap_f, bp_f, hl_f, al_f, hp_f), (ap_b, bp_b, hl_b, al_b, hp_b))

    def conv_into(x, n):
        xc_s[pl.ds(0, n), :] = (cw[0:1] * _shift_rows(x, 2) + cw[1:2] * _shift_rows(x, 1)
                                + cw[2:3] * x + cw[3:4] * _shift_rows(x, -1)) + bias

    def coefficients(n):
        chunk = n // SUBLANES
        pitch = _scan_pitch(chunk)
        rows = max(chunk, min(n, COEFF_ROWS))
        per = rows // chunk

        def body(i, carry):
            xc = xc_s[pl.ds(pl.multiple_of(i * rows, SUBLANES), rows), :]
            gates = _dot(xc.astype(BF16), wg) + bg
            half_xc = 0.5 * xc
            for d in range(2):
                tr = jnp.tanh(gates[:, (2 * d) * LANES:(2 * d + 1) * LANES])
                ti = jnp.tanh(gates[:, (2 * d + 1) * LANES:(2 * d + 2) * LANES])
                neg_log_a = c1[d:d + 1] + c1[d:d + 1] * tr
                a = jnp.exp(-neg_log_a)
                y = jnp.tanh(neg_log_a) * (a * a + 1.0)
                b = (y * lax.rsqrt(jnp.maximum(y, F32_TINY))) * (half_xc + half_xc * ti)
                for k in range(per):
                    dst = pl.multiple_of((i * per + k) * pitch, SUBLANES)
                    dirs[d][0][pl.ds(dst, chunk), :] = a[k * chunk:(k + 1) * chunk]
                    dirs[d][1][pl.ds(dst, chunk), :] = b[k * chunk:(k + 1) * chunk]
            return carry

        if n == rows:
            body(0, 0)
        else:
            lax.fori_loop(0, n // rows, body, 0)

    def scan(n, h0_f, h0_b, keep):
        chunk = n // SUBLANES
        pitch = _scan_pitch(chunk)

        def steps(jo, carry):
            h_f, a_f, h_b, a_b = carry
            for u in range(SUBLANES):
                j = jo * SUBLANES + u
                av = ap_f[pl.ds(j, SUBLANES, stride=pitch), :]
                h_f = av * h_f + bp_f[pl.ds(j, SUBLANES, stride=pitch), :]
                a_f = av * a_f
                jb = chunk - 1 - j
                av = ap_b[pl.ds(jb, SUBLANES, stride=pitch), :]
                h_b = av * h_b + bp_b[pl.ds(jb, SUBLANES, stride=pitch), :]
                a_b = av * a_b
                if keep:
                    o = pl.multiple_of(j * SUBLANES, SUBLANES)
                    hl_f[pl.ds(o, SUBLANES), :] = h_f
                    al_f[pl.ds(o, SUBLANES), :] = a_f
                    hl_b[pl.ds(o, SUBLANES), :] = h_b
                    al_b[pl.ds(o, SUBLANES), :] = a_b
            return h_f, a_f, h_b, a_b

        zeros = jnp.zeros((SUBLANES, LANES), F32)
        ones = jnp.ones((SUBLANES, LANES), F32)
        h_f, a_f, h_b, a_b = lax.fori_loop(0, chunk // SUBLANES, steps, (zeros, ones, zeros, ones))

        in_f = [h0_f]
        for c in range(SUBLANES):
            in_f.append(a_f[c:c + 1] * in_f[c] + h_f[c:c + 1])
        in_b = [h0_b]
        for c in range(SUBLANES - 1, -1, -1):
            in_b.append(a_b[c:c + 1] * in_b[-1] + h_b[c:c + 1])
        if keep:
            hin_f = jnp.concatenate(in_f[:SUBLANES], axis=0)
            hin_b = jnp.concatenate(in_b[SUBLANES - 1::-1], axis=0)

            def fix(jo, carry):
                for u in range(SUBLANES):
                    j = jo * SUBLANES + u
                    o = pl.multiple_of(j * SUBLANES, SUBLANES)
                    hp_f[pl.ds(j, SUBLANES, stride=pitch), :] = (
                        hl_f[pl.ds(o, SUBLANES), :] + al_f[pl.ds(o, SUBLANES), :] * hin_f)
                    hp_b[pl.ds(chunk - 1 - j, SUBLANES, stride=pitch), :] = (
                        hl_b[pl.ds(o, SUBLANES), :] + al_b[pl.ds(o, SUBLANES), :] * hin_b)
                return carry
            lax.fori_loop(0, chunk // SUBLANES, fix, 0)
        return in_f[SUBLANES], in_b[SUBLANES]

    zero = jnp.zeros((1, LANES), F32)
    conv_into(xrc_ref[0], n_ctx)
    coefficients(n_ctx)
    h0_f, h0_b = scan(n_ctx, zero, zero, keep=False)

    conv_into(xr_ref[0], n_lat)
    coefficients(n_lat)
    scan(n_lat, h0_f, h0_b, keep=True)

    chunk = n_lat // SUBLANES
    pitch = _scan_pitch(chunk)

    def emit(c, carry):
        src = pl.multiple_of(c * chunk, 2 * SUBLANES)
        dst = pl.multiple_of(c * pitch, SUBLANES)
        hsum = hp_f[pl.ds(dst, chunk), :] + hp_b[pl.ds(dst, chunk), :]
        y = gr_ref[0, pl.ds(src, chunk), :] * hsum
        y_ref[0, pl.ds(src, chunk), :] = y.astype(y_ref.dtype)
        return carry
    lax.fori_loop(0, SUBLANES, emit, 0)


def _mixers(xr, xr_c, gr, conv_w, conv_b, wg, bgate, lam, u, bg, sc_w):
    bn, n, _ = xr.shape
    n_ctx = xr_c.shape[1]
    assert n % (SUBLANES * SUBLANES) == 0 and n_ctx % (SUBLANES * SUBLANES) == 0 and n_ctx <= n
    pitched = SUBLANES * _scan_pitch(n // SUBLANES)
    seq_spec = pl.BlockSpec((1, n, LANES), lambda b, p: (b, 0, p))
    return pl.pallas_call(
        functools.partial(_rnn_kernel, n_lat=n, n_ctx=n_ctx),
        grid=(bn, N_LANE_GROUPS),
        in_specs=[
            seq_spec,
            pl.BlockSpec((1, n_ctx, LANES), lambda b, p: (b, 0, p)),
            seq_spec,
            pl.BlockSpec((4, LANES), lambda b, p: (0, p)),
            pl.BlockSpec((1, LANES), lambda b, p: (0, p)),
            pl.BlockSpec((1, LANES, 4 * LANES), lambda b, p: (p, 0, 0)),
            pl.BlockSpec((1, 1, 4 * LANES), lambda b, p: (p, 0, 0)),
            pl.BlockSpec((2, LANES), lambda b, p: (0, p)),
            seq_spec,
            seq_spec,
            pl.BlockSpec((3, LANES), lambda b, p: (0, p)),
        ],
        out_specs=[seq_spec, seq_spec],
        out_shape=[jax.ShapeDtypeStruct((bn, n, D_RNN), BF16)] * 2,
        scratch_shapes=[pltpu.VMEM((n, LANES), F32)]
        + [pltpu.VMEM((pitched, LANES), F32)] * 4
        + [pltpu.VMEM((n, LANES), F32)] * 4
        + [pltpu.VMEM((pitched, LANES), F32)] * 2,
        compiler_params=pltpu.CompilerParams(vmem_limit_bytes=VMEM_LIMIT),
        name="mixers",
    )(xr, xr_c, gr, conv_w, conv_b.reshape(1, D_RNN), wg, bgate, lam, u, bg, sc_w)


def _gate_weights(rg_wa, rg_ba, rg_wx, rg_bx):
    eye = jnp.eye(2, dtype=F32)
    blocks, biases = [], []
    for d in range(2):
        for w, bvec in ((rg_wa[d], rg_ba[d]), (rg_wx[d], rg_bx[d])):
            w4 = w.reshape(N_LANE_GROUPS, 2, RNN_HEAD_DIM, RNN_HEAD_DIM)
            bd = jnp.einsum("paij,ac->paicj", w4, eye).reshape(N_LANE_GROUPS, LANES, LANES)
            blocks.append(0.5 * bd)
            biases.append(0.5 * bvec.reshape(N_LANE_GROUPS, 1, LANES))
    return jnp.concatenate(blocks, axis=-1).astype(BF16), jnp.concatenate(biases, axis=-1)


def _gconv_block(u_ref, bg_ref, w_ref, y_ref):
    p = pl.program_id(1)
    u = u_ref[0]
    w = w_ref[...]

    @pl.when(p < D_CONV_H // LANES)
    def _():
        col = lax.broadcasted_iota(jnp.int32, u.shape, 0) % GRID_W
        left = jnp.where(col > 0, _shift_rows(u, 1), 0.0)
        right = jnp.where(col < GRID_W - 1, _shift_rows(u, -1), 0.0)
        y_ref[0] = (bg_ref[0] * (w[0:1] * left + w[1:2] * u + w[2:3] * right)).astype(y_ref.dtype)

    @pl.when(p >= D_CONV_H // LANES)
    def _():
        y_ref[0] = (bg_ref[0] * (w[0:1] * _shift_rows(u, GRID_W) + w[1:2] * u
                                 + w[2:3] * _shift_rows(u, -GRID_W))).astype(y_ref.dtype)


def _lane_max(x, mask):
    return jnp.max(jnp.where(mask, x, -jnp.inf), axis=-1, keepdims=True)


def _first_lane(cond, lane):
    return jnp.min(jnp.where(cond, lane, float(LANES)), axis=-1, keepdims=True)


def _outproj_kernel(x_ref, yr_ref, yc_ref, w32_ref, mod_ref, g_ref, wr_ref, br_ref,
                    x1_ref, mt_ref, route_ref, cnt_ref, carry, w_ref, m_s, *, tm):
    s = pl.program_id(0)

    @pl.when(s == 0)
    def _():
        carry[...] = jnp.zeros_like(carry)
        w_ref[...] = w32_ref[...].astype(BF16)
        m_s[...] = jnp.zeros_like(m_s)

    logits = _dot(m_s[...], wr_ref[...]) + br_ref[...]
    lane_i = lax.broadcasted_iota(jnp.int32, logits.shape, 1)
    lane = lane_i.astype(F32)
    is_grp = lane_i < N_GROUPS
    g_max = _lane_max(logits, is_grp)
    grp = _first_lane(is_grp & (logits == g_max), lane)
    p_g = 1.0 / jnp.sum(jnp.where(is_grp, jnp.exp(logits - g_max), 0.0), axis=-1, keepdims=True)
    lo_lane = EXPERT_LANE0 + grp * EXPERTS_PER_GROUP
    in_grp = (lane >= lo_lane) & (lane < lo_lane + EXPERTS_PER_GROUP)
    l1 = _lane_max(logits, in_grp)
    i1 = _first_lane(in_grp & (logits == l1), lane)
    rest = in_grp & (lane != i1)
    l2 = _lane_max(logits, rest)
    i2 = _first_lane(rest & (logits == l2), lane)
    r21 = jnp.exp(l2 - l1)
    gate1 = p_g / (1.0 + r21)
    gate2 = gate1 * r21

    oh1 = jnp.where(lane == i1, 1.0, 0.0)
    oh2 = jnp.where(lane == i2, 1.0, 0.0)
    both = (oh1 + oh2).astype(BF16)
    ti = lax.broadcasted_iota(jnp.int32, (tm, tm), 0)
    tj = lax.broadcasted_iota(jnp.int32, (tm, tm), 1)
    tri = jnp.where(tj < ti, 1.0, 0.0).astype(BF16)
    counts = carry[...]
    before = _dot(tri, both) + counts
    rank1 = jnp.sum(oh1 * before, axis=-1, keepdims=True)
    rank2 = jnp.sum(oh2 * before, axis=-1, keepdims=True)
    out = jnp.zeros(logits.shape, F32)
    for k, val in enumerate((i1 - EXPERT_LANE0, i2 - EXPERT_LANE0, gate1, gate2, rank1, rank2)):
        out = jnp.where(lane_i == k, val, out)
    route_ref[...] = out
    real = jnp.where(s > 0, 1.0, 0.0)
    total = counts + real * jnp.sum(oh1 + oh2, axis=0, keepdims=True)
    carry[...] = total
    cnt_ref[...] = total

    mix = _dot(yr_ref[0], w_ref[0:D_RNN, :]) + _dot(yc_ref[0], w_ref[D_RNN:, :])
    x1 = x_ref[0] + mod_ref[0, 2:3, :] * mix
    x1_ref[0] = x1
    m_new = _norm_mod(x1, g_ref[...], mod_ref[0, 4:5, :], mod_ref[0, 3:4, :])
    m_s[...] = m_new.astype(BF16)
    half = D_MODEL // 2
    packed = pltpu.pack_elementwise([m_new[:, :half], m_new[:, half:]], packed_dtype=BF16)
    for q in range(PACK_TILES):
        mt_ref[pl.ds(q, tm, stride=PACK_TILES), :] = packed[:, q * LANES:(q + 1) * LANES]


def _outproj(x, y_rnn, y_conv, w_out, mod3, norm_g, wr, br):
    bn, n, d = x.shape
    tm = min(OUTPROJ_TILE, n)
    nt = n // tm
    n_tiles = bn * nt
    t_all = bn * n

    def cur(s):
        return jnp.minimum(s, n_tiles - 1)

    def prev(s):
        return jnp.maximum(s - 1, 0)

    def seq_map(s):
        return (cur(s) // nt, cur(s) % nt, 0)

    const = lambda s: (0, 0)
    return pl.pallas_call(
        functools.partial(_outproj_kernel, tm=tm),
        grid=(n_tiles + 1,),
        in_specs=[
            pl.BlockSpec((1, tm, d), seq_map),
            pl.BlockSpec((1, tm, D_RNN), seq_map),
            pl.BlockSpec((1, tm, D_CONV), seq_map),
            pl.BlockSpec((D_RNN + D_CONV, d), const, pipeline_mode=pl.Buffered(1)),
            pl.BlockSpec((1, 6, d), lambda s: (cur(s) // nt, 0, 0)),
            pl.BlockSpec((1, d), const),
            pl.BlockSpec((d, LANES), const),
            pl.BlockSpec((1, LANES), const),
        ],
        out_specs=[
            pl.BlockSpec((1, tm, d), seq_map),
            pl.BlockSpec((tm * PACK_TILES, LANES), lambda s: (cur(s), 0)),
            pl.BlockSpec((tm, LANES), lambda s: (prev(s), 0)),
            pl.BlockSpec((1, LANES), const),
        ],
        out_shape=[
            jax.ShapeDtypeStruct((bn, n, d), F32),
            jax.ShapeDtypeStruct((t_all * PACK_TILES, LANES), jnp.uint32),
            jax.ShapeDtypeStruct((t_all, LANES), F32),
            jax.ShapeDtypeStruct((1, LANES), F32),
        ],
        scratch_shapes=[pltpu.VMEM((1, LANES), F32), pltpu.VMEM((D_RNN + D_CONV, d), BF16),
                        pltpu.VMEM((tm, d), BF16)],
        compiler_params=pltpu.CompilerParams(
            dimension_semantics=("arbitrary",), vmem_limit_bytes=VMEM_LIMIT),
        name="outproj",
    )(x, y_rnn, y_conv, w_out, mod3, norm_g.reshape(1, d), wr, br)


def _row_tile(ref, row):
    return ref.at[pl.ds(pl.multiple_of(row * ROW_TILES, ROW_TILES), ROW_TILES)]


def _slotmap_kernel(dest_ref, zeros_hbm, asg_ref, sem):
    fill = pltpu.make_async_copy(zeros_hbm, asg_ref, sem)
    fill.start()
    fill.wait()

    def body(c, carry):
        for u in range(DMA_UNROLL):
            a = c * DMA_UNROLL + u
            asg_ref[dest_ref[a]] = a
        return carry
    lax.fori_loop(0, dest_ref.shape[0] // DMA_UNROLL, body, 0)


def _slotmap(dest, n_slots):
    return pl.pallas_call(
        _slotmap_kernel,
        in_specs=[pl.BlockSpec(memory_space=pltpu.SMEM), pl.BlockSpec(memory_space=pl.ANY)],
        out_specs=pl.BlockSpec(memory_space=pltpu.SMEM),
        out_shape=jax.ShapeDtypeStruct((n_slots,), jnp.int32),
        scratch_shapes=[pltpu.SemaphoreType.DMA],
        name="slotmap",
    )(dest, jnp.zeros((n_slots,), jnp.int32))


def _expert_kernel(be_ref, ws_ref, ne_ref, bv_ref, nu_ref, asg_hbm, m_ref, wg_hbm, wu_hbm, wd_hbm, yb_ref,
                   xbuf_a, xbuf_b, idx, isems, wbuf_g, wbuf_u, wbuf_d, wsems, wg_s, wu_s, wd_s):
    j = pl.program_id(0)
    n_used = nu_ref[0]
    last = n_used - 1

    def idx_copy(blk, sl):
        return pltpu.make_async_copy(asg_hbm.at[blk], idx.at[sl], isems.at[sl])

    def copy_rows(buf, sl, r0, n):
        for u in range(n):
            tok = lax.shift_right_logical(idx[sl, 0, r0 + u], 1)
            src = pl.multiple_of(tok * PACK_TILES, PACK_TILES)
            buf[pl.ds((r0 + u) * PACK_TILES, PACK_TILES), :] = m_ref[pl.ds(src, PACK_TILES), :]

    def unpack(buf, rows):
        halves = ([], [])
        for s in range(PACK_TILES):
            word = buf[pl.ds(s, rows, stride=PACK_TILES), :]
            for k in range(2):
                part = pltpu.unpack_elementwise(word, index=k, packed_dtype=BF16, unpacked_dtype=F32)
                halves[k].append(part.astype(BF16))
        return jnp.concatenate(halves[0] + halves[1], axis=-1)

    n_pieces = 2 * D_EXPERT // MXU_TILE + D_MODEL // MXU_TILE
    bounds = [(p * MOE_BLK) // n_pieces for p in range(n_pieces + 1)]

    def compute(cur, nxt, nxt_sl, rows):
        pieces = iter(zip(bounds[:-1], bounds[1:]))

        def dot_pieces(a, w_ref, n0):
            acc = _dot(a, w_ref[:, n0:n0 + MXU_TILE])
            r0, r1 = next(pieces)
            copy_rows(nxt, nxt_sl, r0, r1 - r0)
            return acc

        xb16 = unpack(cur, rows)
        acts = []
        for n0 in range(0, D_EXPERT, MXU_TILE):
            gate = dot_pieces(xb16, wg_s, n0)
            up = dot_pieces(xb16, wu_s, n0)
            acts.append(((gate * jax.nn.sigmoid(gate)) * up).astype(BF16))
        h = jnp.concatenate(acts, axis=-1)
        for n0 in range(0, D_MODEL, MXU_TILE):
            y = dot_pieces(h, wd_s, n0)
            for s in range(MXU_TILE // LANES):
                yb_ref[pl.ds(n0 // LANES + s, rows, stride=ROW_TILES), :] = (
                    y[:, s * LANES:(s + 1) * LANES])
        if rows < MOE_BLK:
            yb_ref[pl.ds(rows * ROW_TILES, (MOE_BLK - rows) * ROW_TILES), :] = jnp.zeros(
                ((MOE_BLK - rows) * ROW_TILES, LANES), F32)

    @pl.when(j >= n_used)
    def _():
        yb_ref[...] = jnp.zeros_like(yb_ref)

    @pl.when(j < n_used)
    def _():
        slot = j % 2
        other = 1 - slot

        @pl.when(j == 0)
        def _():
            idx_copy(0, 0).start()
            idx_copy(0, 0).wait()

            def body(c, carry):
                copy_rows(xbuf_a, 0, c * DMA_UNROLL, DMA_UNROLL)
                return carry
            lax.fori_loop(0, MOE_BLK // DMA_UNROLL, body, 0)
            idx_copy(jnp.minimum(1, last), 1).start()

        e = be_ref[j]
        wslot = ws_ref[j]

        def weight_copies(expert, sl):
            return [pltpu.make_async_copy(src.at[expert], dst.at[sl], wsems.at[sl])
                    for src, dst in ((wg_hbm, wbuf_g), (wu_hbm, wbuf_u), (wd_hbm, wbuf_d))]

        @pl.when(j == 0)
        def _():
            for cp in weight_copies(e, wslot):
                cp.start()

        @pl.when((j == 0) | (e != be_ref[jnp.maximum(j - 1, 0)]))
        def _():
            for cp in weight_copies(e, wslot):
                cp.wait()
            wg_s[...] = wbuf_g[wslot].astype(BF16)
            wu_s[...] = wbuf_u[wslot].astype(BF16)
            wd_s[...] = wbuf_d[wslot].astype(BF16)

            @pl.when(ne_ref[j] >= 0)
            def _():
                for cp in weight_copies(ne_ref[j], 1 - wslot):
                    cp.start()

        idx_copy(0, other).wait()

        short = bv_ref[j] <= MOE_BLK // 2
        for par, (cur, nxt) in enumerate(((xbuf_a, xbuf_b), (xbuf_b, xbuf_a))):
            @pl.when((slot == par) & jnp.logical_not(short))
            def _(cur=cur, nxt=nxt, par=par):
                compute(cur, nxt, 1 - par, MOE_BLK)

            @pl.when((slot == par) & short)
            def _(cur=cur, nxt=nxt, par=par):
                compute(cur, nxt, 1 - par, MOE_BLK // 2)

        @pl.when(j < last)
        def _():
            idx_copy(jnp.minimum(j + 2, last), slot).start()


def _experts(block_e, weight_slot, next_e, block_valid, n_used, slot_asg, mt, w_gate, w_up, w_down, n_blocks):
    return pl.pallas_call(
        _expert_kernel,
        grid_spec=pltpu.PrefetchScalarGridSpec(
            num_scalar_prefetch=5,
            grid=(n_blocks,),
            in_specs=[
                pl.BlockSpec(memory_space=pl.ANY),
                pl.BlockSpec(memory_space=pltpu.VMEM),
                pl.BlockSpec(memory_space=pl.ANY),
                pl.BlockSpec(memory_space=pl.ANY),
                pl.BlockSpec(memory_space=pl.ANY),
            ],
            out_specs=pl.BlockSpec((MOE_BLK * ROW_TILES, LANES), lambda j, *_: (j, 0)),
            scratch_shapes=[
                pltpu.VMEM((MOE_BLK * PACK_TILES, LANES), jnp.uint32),
                pltpu.VMEM((MOE_BLK * PACK_TILES, LANES), jnp.uint32),
                pltpu.SMEM((2, 1, MOE_BLK), jnp.int32),
                pltpu.SemaphoreType.DMA((2,)),
                pltpu.VMEM((2, D_MODEL, D_EXPERT), F32),
                pltpu.VMEM((2, D_MODEL, D_EXPERT), F32),
                pltpu.VMEM((2, D_EXPERT, D_MODEL), F32),
                pltpu.SemaphoreType.DMA((2,)),
                pltpu.VMEM((D_MODEL, D_EXPERT), BF16),
                pltpu.VMEM((D_MODEL, D_EXPERT), BF16),
                pltpu.VMEM((D_EXPERT, D_MODEL), BF16),
            ],
        ),
        out_shape=jax.ShapeDtypeStruct((n_blocks * MOE_BLK * ROW_TILES, LANES), F32),
        compiler_params=pltpu.CompilerParams(
            dimension_semantics=("arbitrary",), vmem_limit_bytes=EXPERT_VMEM_LIMIT),
        name="expert",
    )(block_e, weight_slot, next_e, block_valid, n_used, slot_asg.reshape(n_blocks, 1, MOE_BLK), mt, w_gate,
      w_up, w_down)


def _combine_kernel(dest_ref, yb_hbm, x1_ref, route_ref, mod_ref, g_ref, o_ref, ybuf, sems, *, tc):
    i = pl.program_id(0)
    slot = i % 2

    def row_copy(d, sl, k, r):
        return pltpu.make_async_copy(_row_tile(yb_hbm, d), _row_tile(ybuf.at[sl, k], r), sems.at[sl])

    def gather(step, sl):
        def issue(c, carry):
            for u in range(DMA_UNROLL):
                r = c * DMA_UNROLL + u
                for k in range(TOP_K):
                    row_copy(dest_ref[TOP_K * (step * tc + r) + k], sl, k, r).start(priority=k)
            return carry
        lax.fori_loop(0, tc // DMA_UNROLL, issue, 0)

    @pl.when(i == 0)
    def _():
        gather(0, 0)

    @pl.when(i + 1 < pl.num_programs(0))
    def _():
        gather(i + 1, 1 - slot)

    def drain(c, carry):
        for u in range(DMA_UNROLL * TOP_K):
            row_copy(0, slot, 0, 0).wait()
        return carry
    lax.fori_loop(0, tc // DMA_UNROLL, drain, 0)

    def rows(k):
        return jnp.concatenate(
            [ybuf[slot, k, pl.ds(s, tc, stride=ROW_TILES), :] for s in range(ROW_TILES)], axis=-1)

    route = route_ref[...]
    y = route[:, 2:3] * rows(0) + route[:, 3:4] * rows(1)
    x2 = x1_ref[...] + mod_ref[0, 5:6, :] * y
    ms = jnp.mean(x2 * x2, axis=-1, keepdims=True)
    o_ref[...] = x2 * lax.rsqrt(ms + NORM_EPS) * g_ref[...]


def _combine(dest, yb, x1_2d, route, mod3, final_g, seq):
    t_all, d = x1_2d.shape
    tc = min(COMBINE_TILE, seq)
    per_seq = seq // tc
    return pl.pallas_call(
        functools.partial(_combine_kernel, tc=tc),
        grid_spec=pltpu.PrefetchScalarGridSpec(
            num_scalar_prefetch=1,
            grid=(t_all // tc,),
            in_specs=[
                pl.BlockSpec(memory_space=pl.ANY),
                pl.BlockSpec((tc, d), lambda i, dest: (i, 0)),
                pl.BlockSpec((tc, LANES), lambda i, dest: (i, 0)),
                pl.BlockSpec((1, 6, d), lambda i, dest: (i // per_seq, 0, 0)),
                pl.BlockSpec((1, d), lambda i, dest: (0, 0)),
            ],
            out_specs=pl.BlockSpec((tc, d), lambda i, dest: (i, 0)),
            scratch_shapes=[
                pltpu.VMEM((2, TOP_K, tc * ROW_TILES, LANES), F32),
                pltpu.SemaphoreType.DMA((2,)),
            ],
        ),
        out_shape=jax.ShapeDtypeStruct((t_all, d), F32),
        compiler_params=pltpu.CompilerParams(
            dimension_semantics=("arbitrary",), vmem_limit_bytes=VMEM_LIMIT),
        name="combine",
    )(dest, yb, x1_2d, route, mod3, final_g.reshape(1, d))


def kernel(x, c, ctx, c_ctx, ada_w, ada_b, norm1_g, norm2_g, w_in, rnn_conv_w, rnn_conv_b, rg_wa, rg_ba,
           rg_wx, rg_bx, rg_lambda, sc_conv_w, w_out, router_group_w, router_group_b, router_exp_w,
           router_exp_b, exp_w_gate, exp_w_up, exp_w_down, final_norm_g):
    bn, seq, d = x.shape
    assert d == D_MODEL and bn < MOD_ROWS and ada_w.shape[0] == 1
    t_all = bn * seq

    cc = jnp.concatenate([c, c_ctx[None], jnp.zeros((MOD_ROWS - bn - 1, d), F32)], axis=0)
    mod3 = _modulation(cc, ada_w[0], ada_b[0]).reshape(MOD_ROWS, 6, d)

    xr, gr, u, bg = _inproj(x, mod3, None, norm1_g[0], w_in[0], latent=True)
    (xr_c,) = _inproj(ctx, mod3, bn, norm1_g[0], w_in[0], latent=False)

    wg, bgate = _gate_weights(rg_wa[0], rg_ba[0], rg_wx[0], rg_bx[0])
    assert D_RNN == D_CONV
    y_rnn, y_conv = _mixers(xr, xr_c, gr, rnn_conv_w[0], rnn_conv_b[0], wg, bgate, rg_lambda[0], u, bg,
                            sc_conv_w[0])

    wr = jnp.zeros((d, LANES), F32)
    wr = wr.at[:, :N_GROUPS].set(router_group_w[0]).at[:, EXPERT_LANE0:EXPERT_LANE0 + N_EXPERTS].set(router_exp_w[0])
    br = jnp.zeros((1, LANES), F32)
    br = br.at[0, :N_GROUPS].set(router_group_b[0]).at[0, EXPERT_LANE0:EXPERT_LANE0 + N_EXPERTS].set(router_exp_b[0])
    x1, mt, route, cnt = _outproj(x, y_rnn, y_conv, w_out[0], mod3, norm2_g[0], wr.astype(BF16), br)

    n_assign = t_all * TOP_K
    n_blocks = (n_assign + N_EXPERTS * (MOE_BLK - 1) + MOE_BLK - 1) // MOE_BLK
    counts = cnt[0, EXPERT_LANE0:EXPERT_LANE0 + N_EXPERTS].astype(jnp.int32)
    pcounts = (counts + MOE_BLK - 1) // MOE_BLK * MOE_BLK
    pends = jnp.cumsum(pcounts)
    pstarts = pends - pcounts
    experts = route[:, 0:TOP_K].astype(jnp.int32)
    ranks = route[:, 4:4 + TOP_K].astype(jnp.int32)
    onehot = experts[:, :, None] == jnp.arange(N_EXPERTS, dtype=jnp.int32)
    dest = (ranks + jnp.sum(jnp.where(onehot, pstarts, 0), axis=-1)).reshape(n_assign)
    n_used = (pends[-1] // MOE_BLK).astype(jnp.int32)
    blk_start = jnp.arange(n_blocks, dtype=jnp.int32) * MOE_BLK
    block_e = jnp.minimum(jnp.sum(blk_start[:, None] >= pends[None, :], axis=1), N_EXPERTS - 1)
    last_e = jnp.max(jnp.where(counts > 0, jnp.arange(N_EXPERTS, dtype=jnp.int32), 0))
    block_e = jnp.where(blk_start < pends[-1], block_e, last_e).astype(jnp.int32)
    eids = jnp.arange(N_EXPERTS, dtype=jnp.int32)
    used = counts > 0
    slot_of_e = (jnp.cumsum(used.astype(jnp.int32)) - 1) % 2
    later = jnp.where(used[None, :] & (eids[None, :] > eids[:, None]), eids[None, :], N_EXPERTS)
    next_of_e = jnp.min(later, axis=1)
    next_of_e = jnp.where(next_of_e == N_EXPERTS, -1, next_of_e)
    is_e = block_e[:, None] == eids[None, :]
    weight_slot = jnp.sum(jnp.where(is_e, slot_of_e[None, :], 0), axis=1).astype(jnp.int32)
    next_e = jnp.sum(jnp.where(is_e, next_of_e[None, :], 0), axis=1).astype(jnp.int32)
    rows_end = jnp.sum(jnp.where(is_e, (pstarts + counts)[None, :], 0), axis=1)
    block_valid = jnp.clip(rows_end - blk_start, 0, MOE_BLK).astype(jnp.int32)

    n_slots = n_blocks * MOE_BLK
    slot_asg = _slotmap(dest, n_slots)
    yb = _experts(block_e, weight_slot, next_e, block_valid, n_used.reshape(1), slot_asg, mt, exp_w_gate[0],
                  exp_w_up[0], exp_w_down[0], n_blocks)
    out = _combine(dest, yb, x1.reshape(t_all, d), route, mod3, final_norm_g, seq)
    return out.reshape(bn, seq, d)
```

```python
import functools

import jax
import jax.numpy as jnp
from jax import lax
from jax.experimental import pallas as pl
from jax.experimental.pallas import tpu as pltpu

F32 = jnp.float32
BF16 = jnp.bfloat16

D_MODEL = 1024
D_RNN = 512
D_CONV = 512
D_CONV_H = D_CONV // 2
RNN_HEADS = 8
RNN_HEAD_DIM = D_RNN // RNN_HEADS
GRID_W = 64
RG_C = 8.0
N_GROUPS = 4
EXPERTS_PER_GROUP = 8
N_EXPERTS = N_GROUPS * EXPERTS_PER_GROUP
TOP_K = 2
D_EXPERT = 512
NORM_EPS = 1e-6
F32_TINY = 1.1754944e-38

LANES = 128
SUBLANES = 8
ROW_TILES = D_MODEL // LANES
PACK_TILES = ROW_TILES // 2
N_LANE_GROUPS = D_RNN // LANES
EXPERT_LANE0 = N_GROUPS

MOD_ROWS = 16
MOD_TN = 768
INPROJ_TILE = 1024
OUTPROJ_TILE = 512
COEFF_ROWS = 2048
SCAN_UNROLL = 16
MOE_BLK = 512
MXU_TILE = 256
COMBINE_TILE = 256
DMA_UNROLL = 16
VMEM_LIMIT = 48 * 1024 * 1024
EXPERT_VMEM_LIMIT = 58 * 1024 * 1024


def _dot(a, b):
    return jnp.dot(a, b, preferred_element_type=F32)


def _split_bf16(x):
    hi = x.astype(BF16)
    lo = (x - hi.astype(F32)).astype(BF16)
    return hi, lo


def _mod_kernel(cc_ref, w_ref, b_ref, o_ref):
    s = cc_ref[...]
    s = s * jax.nn.sigmoid(s)
    s_hi, s_lo = _split_bf16(s)
    w_hi, w_lo = _split_bf16(w_ref[...])
    o_ref[...] = _dot(s_hi, w_hi) + _dot(s_lo, w_hi) + _dot(s_hi, w_lo) + b_ref[...]


def _modulation(cc, ada_w, ada_b):
    n = ada_w.shape[1]
    return pl.pallas_call(
        _mod_kernel,
        grid=(n // MOD_TN,),
        in_specs=[
            pl.BlockSpec((MOD_ROWS, D_MODEL), lambda j: (0, 0)),
            pl.BlockSpec((D_MODEL, MOD_TN), lambda j: (0, j)),
            pl.BlockSpec((1, MOD_TN), lambda j: (0, j)),
        ],
        out_specs=pl.BlockSpec((MOD_ROWS, MOD_TN), lambda j: (0, j)),
        out_shape=jax.ShapeDtypeStruct((MOD_ROWS, n), F32),
        compiler_params=pltpu.CompilerParams(vmem_limit_bytes=VMEM_LIMIT),
        name="mod",
    )(cc, ada_w, ada_b.reshape(1, n))


def _norm_mod(x, g, scale, shift):
    ms = jnp.mean(x * x, axis=-1, keepdims=True)
    y = x * lax.rsqrt(ms + NORM_EPS) * g
    return y * (1.0 + scale) + shift


def _inproj_kernel(x_ref, mod_ref, g_ref, w32_ref, *refs, latent):
    out_refs, w_ref = refs[:-1], refs[-1]

    @pl.when((pl.program_id(0) == 0) & (pl.program_id(1) == 0))
    def _():
        w_ref[...] = w32_ref[...].astype(BF16)

    h = _norm_mod(x_ref[0], g_ref[...], mod_ref[0, 1:2, :], mod_ref[0, 0:1, :])
    hb = h.astype(BF16)
    xr = _dot(hb, w_ref[:, 0:D_RNN])
    out_refs[0][0] = xr
    if latent:
        o = D_RNN
        out_refs[1][0] = jax.nn.gelu(_dot(hb, w_ref[:, o:o + D_RNN]), approximate=True)
        o += D_RNN
        v = _dot(hb, w_ref[:, o:o + D_CONV])
        out_refs[3][0] = _dot(hb, w_ref[:, o + D_CONV:o + 2 * D_CONV])
        cg = _dot(hb, w_ref[:, o + 2 * D_CONV:o + 3 * D_CONV])
        out_refs[2][0] = cg * v


def _inproj(x, mod3, mod_row, norm_g, w_in, latent):
    bn, n, d = x.shape
    tm = min(INPROJ_TILE, n)
    n_out = 4 if latent else 1
    width = w_in.shape[1] if latent else D_RNN
    mod_map = (lambda b, i: (b, 0, 0)) if mod_row is None else (lambda b, i: (mod_row, 0, 0))
    return pl.pallas_call(
        functools.partial(_inproj_kernel, latent=latent),
        grid=(bn, n // tm),
        in_specs=[
            pl.BlockSpec((1, tm, d), lambda b, i: (b, i, 0)),
            pl.BlockSpec((1, 6, d), mod_map),
            pl.BlockSpec((1, d), lambda b, i: (0, 0)),
            pl.BlockSpec((d, width), lambda b, i: (0, 0), pipeline_mode=pl.Buffered(1)),
        ],
        out_specs=[pl.BlockSpec((1, tm, D_RNN), lambda b, i: (b, i, 0))] * n_out,
        out_shape=[jax.ShapeDtypeStruct((bn, n, D_RNN), F32)] * n_out,
        scratch_shapes=[pltpu.VMEM((d, width), BF16)],
        compiler_params=pltpu.CompilerParams(
            dimension_semantics=("arbitrary", "arbitrary"), vmem_limit_bytes=VMEM_LIMIT),
        name="inproj_lat" if latent else "inproj_ctx",
    )(x, mod3, norm_g.reshape(1, d), w_in)


def _shift_rows(x, k):
    n = x.shape[0]
    row = lax.broadcasted_iota(jnp.int32, x.shape, 0)
    rolled = pltpu.roll(x, k % n, axis=0)
    valid = (row >= k) if k > 0 else (row < n + k)
    return jnp.where(valid, rolled, 0.0)


def _scan_pitch(chunk):
    pitch = chunk + SUBLANES
    return pitch if (pitch // SUBLANES) % 2 else pitch + SUBLANES


def _rnn_kernel(xr_ref, xrc_ref, gr_ref, cw_ref, cb_ref, wg_ref, bg_ref, lam_ref, u_ref, bgc_ref, scw_ref,
                y_ref, yc_ref,
                xc_s, ap_f, bp_f, ap_b, bp_b, hl_f, al_f, hl_b, al_b, hp_f, hp_b, xpad, *, n_lat, n_ctx):
    _gconv_block(u_ref, bgc_ref, scw_ref, yc_ref)

    nl = -lam_ref[...]
    sp = jnp.maximum(nl, 0.0) + jnp.log1p(jnp.exp(-jnp.abs(nl)))
    c1 = (0.5 * RG_C) * sp
    cw = cw_ref[...]
    bias = cb_ref[...]
    wg = wg_ref[0]
    bg = bg_ref[0]
    dirs = ((ap_f, bp_f, hl_f, al_f, hp_f), (ap_b, bp_b, hl_b, al_b, hp_b))

    def conv_into(x, n):
        halo = jnp.zeros((SUBLANES, LANES), F32)
        xpad[pl.ds(0, SUBLANES), :] = halo
        xpad[pl.ds(SUBLANES, n), :] = x
        xpad[pl.ds(SUBLANES + n, SUBLANES), :] = halo
        xc_s[pl.ds(0, n), :] = (cw[0:1] * xpad[pl.ds(SUBLANES - 2, n), :]
                                + cw[1:2] * xpad[pl.ds(SUBLANES - 1, n), :] + cw[2:3] * x
                                + cw[3:4] * xpad[pl.ds(SUBLANES + 1, n), :]) + bias

    def coefficients(n):
        chunk = n // SUBLANES
        pitch = _scan_pitch(chunk)
        rows = max(chunk, min(n, COEFF_ROWS))
        per = rows // chunk

        def body(i, carry):
            xc = xc_s[pl.ds(pl.multiple_of(i * rows, SUBLANES), rows), :]
            gates = _dot(xc.astype(BF16), wg) + bg
            half_xc = 0.5 * xc
            for d in range(2):
                tr = jnp.tanh(gates[:, (2 * d) * LANES:(2 * d + 1) * LANES])
                ti = jnp.tanh(gates[:, (2 * d + 1) * LANES:(2 * d + 2) * LANES])
                neg_log_a = c1[d:d + 1] + c1[d:d + 1] * tr
                a = jnp.exp(-neg_log_a)
                y = jnp.tanh(neg_log_a) * (a * a + 1.0)
                b = (y * lax.rsqrt(jnp.maximum(y, F32_TINY))) * (half_xc + half_xc * ti)
                for k in range(per):
                    dst = pl.multiple_of((i * per + k) * pitch, SUBLANES)
                    dirs[d][0][pl.ds(dst, chunk), :] = a[k * chunk:(k + 1) * chunk]
                    dirs[d][1][pl.ds(dst, chunk), :] = b[k * chunk:(k + 1) * chunk]
            return carry

        if n == rows:
            body(0, 0)
        else:
            lax.fori_loop(0, n // rows, body, 0)

    def scan(n, h0_f, h0_b, keep):
        chunk = n // SUBLANES
        pitch = _scan_pitch(chunk)

        def steps(jo, carry):
            h_f, a_f, h_b, a_b = carry
            for u in range(SCAN_UNROLL):
                j = jo * SCAN_UNROLL + u
                av = ap_f[pl.ds(j, SUBLANES, stride=pitch), :]
                h_f = av * h_f + bp_f[pl.ds(j, SUBLANES, stride=pitch), :]
                a_f = av * a_f
                jb = chunk - 1 - j
                av = ap_b[pl.ds(jb, SUBLANES, stride=pitch), :]
                h_b = av * h_b + bp_b[pl.ds(jb, SUBLANES, stride=pitch), :]
                a_b = av * a_b
                if keep:
                    o = pl.multiple_of(j * SUBLANES, SUBLANES)
                    hl_f[pl.ds(o, SUBLANES), :] = h_f
                    al_f[pl.ds(o, SUBLANES), :] = a_f
                    hl_b[pl.ds(o, SUBLANES), :] = h_b
                    al_b[pl.ds(o, SUBLANES), :] = a_b
            return h_f, a_f, h_b, a_b

        zeros = jnp.zeros((SUBLANES, LANES), F32)
        ones = jnp.ones((SUBLANES, LANES), F32)
        h_f, a_f, h_b, a_b = lax.fori_loop(0, chunk // SCAN_UNROLL, steps, (zeros, ones, zeros, ones))

        in_f = [h0_f]
        for c in range(SUBLANES):
            in_f.append(a_f[c:c + 1] * in_f[c] + h_f[c:c + 1])
        in_b = [h0_b]
        for c in range(SUBLANES - 1, -1, -1):
            in_b.append(a_b[c:c + 1] * in_b[-1] + h_b[c:c + 1])
        if keep:
            hin_f = jnp.concatenate(in_f[:SUBLANES], axis=0)
            hin_b = jnp.concatenate(in_b[SUBLANES - 1::-1], axis=0)

            def fix(jo, carry):
                for u in range(SCAN_UNROLL):
                    j = jo * SCAN_UNROLL + u
                    o = pl.multiple_of(j * SUBLANES, SUBLANES)
                    hp_f[pl.ds(j, SUBLANES, stride=pitch), :] = (
                        hl_f[pl.ds(o, SUBLANES), :] + al_f[pl.ds(o, SUBLANES), :] * hin_f)
                    hp_b[pl.ds(chunk - 1 - j, SUBLANES, stride=pitch), :] = (
                        hl_b[pl.ds(o, SUBLANES), :] + al_b[pl.ds(o, SUBLANES), :] * hin_b)
                return carry
            lax.fori_loop(0, chunk // SCAN_UNROLL, fix, 0)
        return in_f[SUBLANES], in_b[SUBLANES]

    zero = jnp.zeros((1, LANES), F32)
    conv_into(xrc_ref[0], n_ctx)
    coefficients(n_ctx)
    h0_f, h0_b = scan(n_ctx, zero, zero, keep=False)

    conv_into(xr_ref[0], n_lat)
    coefficients(n_lat)
    scan(n_lat, h0_f, h0_b, keep=True)

    chunk = n_lat // SUBLANES
    pitch = _scan_pitch(chunk)

    def emit(c, carry):
        src = pl.multiple_of(c * chunk, 2 * SUBLANES)
        dst = pl.multiple_of(c * pitch, SUBLANES)
        hsum = hp_f[pl.ds(dst, chunk), :] + hp_b[pl.ds(dst, chunk), :]
        y = gr_ref[0, pl.ds(src, chunk), :] * hsum
        y_ref[0, pl.ds(src, chunk), :] = y.astype(y_ref.dtype)
        return carry
    lax.fori_loop(0, SUBLANES, emit, 0)


def _mixers(xr, xr_c, gr, conv_w, conv_b, wg, bgate, lam, u, bg, sc_w):
    bn, n, _ = xr.shape
    n_ctx = xr_c.shape[1]
    assert n % (SUBLANES * SCAN_UNROLL) == 0 and n_ctx % (SUBLANES * SCAN_UNROLL) == 0 and n_ctx <= n
    pitched = SUBLANES * _scan_pitch(n // SUBLANES)
    seq_spec = pl.BlockSpec((1, n, LANES), lambda b, p: (b, 0, p))
    return pl.pallas_call(
        functools.partial(_rnn_kernel, n_lat=n, n_ctx=n_ctx),
        grid=(bn, N_LANE_GROUPS),
        in_specs=[
            seq_spec,
            pl.BlockSpec((1, n_ctx, LANES), lambda b, p: (b, 0, p)),
            seq_spec,
            pl.BlockSpec((4, LANES), lambda b, p: (0, p)),
            pl.BlockSpec((1, LANES), lambda b, p: (0, p)),
            pl.BlockSpec((1, LANES, 4 * LANES), lambda b, p: (p, 0, 0)),
            pl.BlockSpec((1, 1, 4 * LANES), lambda b, p: (p, 0, 0)),
            pl.BlockSpec((2, LANES), lambda b, p: (0, p)),
            seq_spec,
            seq_spec,
            pl.BlockSpec((3, LANES), lambda b, p: (0, p)),
        ],
        out_specs=[seq_spec, seq_spec],
        out_shape=[jax.ShapeDtypeStruct((bn, n, D_RNN), BF16)] * 2,
        scratch_shapes=[pltpu.VMEM((n, LANES), F32)]
        + [pltpu.VMEM((pitched, LANES), F32)] * 4
        + [pltpu.VMEM((n, LANES), F32)] * 4
        + [pltpu.VMEM((pitched, LANES), F32)] * 2
        + [pltpu.VMEM((n + 2 * SUBLANES, LANES), F32)],
        compiler_params=pltpu.CompilerParams(vmem_limit_bytes=VMEM_LIMIT),
        name="mixers",
    )(xr, xr_c, gr, conv_w, conv_b.reshape(1, D_RNN), wg, bgate, lam, u, bg, sc_w)


def _gate_weights(rg_wa, rg_ba, rg_wx, rg_bx):
    eye = jnp.eye(2, dtype=F32)
    blocks, biases = [], []
    for d in range(2):
        for w, bvec in ((rg_wa[d], rg_ba[d]), (rg_wx[d], rg_bx[d])):
            w4 = w.reshape(N_LANE_GROUPS, 2, RNN_HEAD_DIM, RNN_HEAD_DIM)
            bd = jnp.einsum("paij,ac->paicj", w4, eye).reshape(N_LANE_GROUPS, LANES, LANES)
            blocks.append(0.5 * bd)
            biases.append(0.5 * bvec.reshape(N_LANE_GROUPS, 1, LANES))
    return jnp.concatenate(blocks, axis=-1).astype(BF16), jnp.concatenate(biases, axis=-1)


def _gconv_block(u_ref, bg_ref, w_ref, y_ref):
    p = pl.program_id(1)
    u = u_ref[0]
    w = w_ref[...]

    @pl.when(p < D_CONV_H // LANES)
    def _():
        col = lax.broadcasted_iota(jnp.int32, u.shape, 0) % GRID_W
        left = jnp.where(col > 0, _shift_rows(u, 1), 0.0)
        right = jnp.where(col < GRID_W - 1, _shift_rows(u, -1), 0.0)
        y_ref[0] = (bg_ref[0] * (w[0:1] * left + w[1:2] * u + w[2:3] * right)).astype(y_ref.dtype)

    @pl.when(p >= D_CONV_H // LANES)
    def _():
        y_ref[0] = (bg_ref[0] * (w[0:1] * _shift_rows(u, GRID_W) + w[1:2] * u
                                 + w[2:3] * _shift_rows(u, -GRID_W))).astype(y_ref.dtype)


def _lane_max(x, mask):
    return jnp.max(jnp.where(mask, x, -jnp.inf), axis=-1, keepdims=True)


def _first_lane(cond, lane):
    return jnp.min(jnp.where(cond, lane, float(LANES)), axis=-1, keepdims=True)


def _outproj_kernel(x_ref, yr_ref, yc_ref, w32_ref, mod_ref, g_ref, wr_ref, br_ref,
                    x1_ref, mt_ref, route_ref, cnt_ref, carry, w_ref, m_s, *, tm):
    s = pl.program_id(0)

    @pl.when(s == 0)
    def _():
        carry[...] = jnp.zeros_like(carry)
        w_ref[...] = w32_ref[...].astype(BF16)
        m_s[...] = jnp.zeros_like(m_s)

    logits = _dot(m_s[...], wr_ref[...]) + br_ref[...]
    lane_i = lax.broadcasted_iota(jnp.int32, logits.shape, 1)
    lane = lane_i.astype(F32)
    is_grp = lane_i < N_GROUPS
    g_max = _lane_max(logits, is_grp)
    grp = _first_lane(is_grp & (logits == g_max), lane)
    p_g = 1.0 / jnp.sum(jnp.where(is_grp, jnp.exp(logits - g_max), 0.0), axis=-1, keepdims=True)
    lo_lane = EXPERT_LANE0 + grp * EXPERTS_PER_GROUP
    in_grp = (lane >= lo_lane) & (lane < lo_lane + EXPERTS_PER_GROUP)
    l1 = _lane_max(logits, in_grp)
    i1 = _first_lane(in_grp & (logits == l1), lane)
    rest = in_grp & (lane != i1)
    l2 = _lane_max(logits, rest)
    i2 = _first_lane(rest & (logits == l2), lane)
    r21 = jnp.exp(l2 - l1)
    gate1 = p_g / (1.0 + r21)
    gate2 = gate1 * r21

    oh1 = jnp.where(lane == i1, 1.0, 0.0)
    oh2 = jnp.where(lane == i2, 1.0, 0.0)
    both = (oh1 + oh2).astype(BF16)
    ti = lax.broadcasted_iota(jnp.int32, (tm, tm), 0)
    tj = lax.broadcasted_iota(jnp.int32, (tm, tm), 1)
    tri = jnp.where(tj < ti, 1.0, 0.0).astype(BF16)
    counts = carry[...]
    before = _dot(tri, both) + counts
    rank1 = jnp.sum(oh1 * before, axis=-1, keepdims=True)
    rank2 = jnp.sum(oh2 * before, axis=-1, keepdims=True)
    out = jnp.zeros(logits.shape, F32)
    for k, val in enumerate((i1 - EXPERT_LANE0, i2 - EXPERT_LANE0, gate1, gate2, rank1, rank2)):
        out = jnp.where(lane_i == k, val, out)
    route_ref[...] = out
    real = jnp.where(s > 0, 1.0, 0.0)
    total = counts + real * jnp.sum(oh1 + oh2, axis=0, keepdims=True)
    carry[...] = total
    cnt_ref[...] = total

    mix = _dot(yr_ref[0], w_ref[0:D_RNN, :]) + _dot(yc_ref[0], w_ref[D_RNN:, :])
    x1 = x_ref[0] + mod_ref[0, 2:3, :] * mix
    x1_ref[0] = x1
    m_new = _norm_mod(x1, g_ref[...], mod_ref[0, 4:5, :], mod_ref[0, 3:4, :])
    m_s[...] = m_new.astype(BF16)
    half = D_MODEL // 2
    packed = pltpu.pack_elementwise([m_new[:, :half], m_new[:, half:]], packed_dtype=BF16)
    for q in range(PACK_TILES):
        mt_ref[pl.ds(q, tm, stride=PACK_TILES), :] = packed[:, q * LANES:(q + 1) * LANES]


def _outproj(x, y_rnn, y_conv, w_out, mod3, norm_g, wr, br):
    bn, n, d = x.shape
    tm = min(OUTPROJ_TILE, n)
    nt = n // tm
    n_tiles = bn * nt
    t_all = bn * n

    def cur(s):
        return jnp.minimum(s, n_tiles - 1)

    def prev(s):
        return jnp.maximum(s - 1, 0)

    def seq_map(s):
        return (cur(s) // nt, cur(s) % nt, 0)

    const = lambda s: (0, 0)
    return pl.pallas_call(
        functools.partial(_outproj_kernel, tm=tm),
        grid=(n_tiles + 1,),
        in_specs=[
            pl.BlockSpec((1, tm, d), seq_map),
            pl.BlockSpec((1, tm, D_RNN), seq_map),
            pl.BlockSpec((1, tm, D_CONV), seq_map),
            pl.BlockSpec((D_RNN + D_CONV, d), const, pipeline_mode=pl.Buffered(1)),
            pl.BlockSpec((1, 6, d), lambda s: (cur(s) // nt, 0, 0)),
            pl.BlockSpec((1, d), const),
            pl.BlockSpec((d, LANES), const),
            pl.BlockSpec((1, LANES), const),
        ],
        out_specs=[
            pl.BlockSpec((1, tm, d), seq_map),
            pl.BlockSpec((tm * PACK_TILES, LANES), lambda s: (cur(s), 0)),
            pl.BlockSpec((tm, LANES), lambda s: (prev(s), 0)),
            pl.BlockSpec((1, LANES), const),
        ],
        out_shape=[
            jax.ShapeDtypeStruct((bn, n, d), F32),
            jax.ShapeDtypeStruct((t_all * PACK_TILES, LANES), jnp.uint32),
            jax.ShapeDtypeStruct((t_all, LANES), F32),
            jax.ShapeDtypeStruct((1, LANES), F32),
        ],
        scratch_shapes=[pltpu.VMEM((1, LANES), F32), pltpu.VMEM((D_RNN + D_CONV, d), BF16),
                        pltpu.VMEM((tm, d), BF16)],
        compiler_params=pltpu.CompilerParams(
            dimension_semantics=("arbitrary",), vmem_limit_bytes=VMEM_LIMIT),
        name="outproj",
    )(x, y_rnn, y_conv, w_out, mod3, norm_g.reshape(1, d), wr, br)


def _row_tile(ref, row):
    return ref.at[pl.ds(pl.multiple_of(row * ROW_TILES, ROW_TILES), ROW_TILES)]


def _slotmap_kernel(dest_ref, zeros_hbm, asg_ref, sem):
    fill = pltpu.make_async_copy(zeros_hbm, asg_ref, sem)
    fill.start()
    fill.wait()

    def body(c, carry):
        for u in range(DMA_UNROLL):
            a = c * DMA_UNROLL + u
            asg_ref[dest_ref[a]] = a
        return carry
    lax.fori_loop(0, dest_ref.shape[0] // DMA_UNROLL, body, 0)


def _slotmap(dest, n_slots):
    return pl.pallas_call(
        _slotmap_kernel,
        in_specs=[pl.BlockSpec(memory_space=pltpu.SMEM), pl.BlockSpec(memory_space=pl.ANY)],
        out_specs=pl.BlockSpec(memory_space=pltpu.SMEM),
        out_shape=jax.ShapeDtypeStruct((n_slots,), jnp.int32),
        scratch_shapes=[pltpu.SemaphoreType.DMA],
        name="slotmap",
    )(dest, jnp.zeros((n_slots,), jnp.int32))


def _expert_kernel(be_ref, ws_ref, ne_ref, bv_ref, nu_ref, asg_hbm, m_ref, wg_hbm, wu_hbm, wd_hbm, yb_ref,
                   xbuf_a, xbuf_b, idx, isems, wbuf_g, wbuf_u, wbuf_d, wsems, wg_s, wu_s, wd_s):
    j = pl.program_id(0)
    n_used = nu_ref[0]
    last = n_used - 1

    def idx_copy(blk, sl):
        return pltpu.make_async_copy(asg_hbm.at[blk], idx.at[sl], isems.at[sl])

    def copy_rows(buf, sl, r0, n):
        for u in range(n):
            tok = lax.shift_right_logical(idx[sl, 0, r0 + u], 1)
            src = pl.multiple_of(tok * PACK_TILES, PACK_TILES)
            buf[pl.ds((r0 + u) * PACK_TILES, PACK_TILES), :] = m_ref[pl.ds(src, PACK_TILES), :]

    def unpack(buf, rows):
        halves = ([], [])
        for s in range(PACK_TILES):
            word = buf[pl.ds(s, rows, stride=PACK_TILES), :]
            for k in range(2):
                part = pltpu.unpack_elementwise(word, index=k, packed_dtype=BF16, unpacked_dtype=F32)
                halves[k].append(part.astype(BF16))
        return jnp.concatenate(halves[0] + halves[1], axis=-1)

    n_pieces = 2 * D_EXPERT // MXU_TILE + D_MODEL // MXU_TILE
    bounds = [(p * MOE_BLK) // n_pieces for p in range(n_pieces + 1)]

    def compute(cur, nxt, nxt_sl, rows):
        pieces = iter(zip(bounds[:-1], bounds[1:]))

        def dot_pieces(a, w_ref, n0):
            acc = _dot(a, w_ref[:, n0:n0 + MXU_TILE])
            r0, r1 = next(pieces)
            copy_rows(nxt, nxt_sl, r0, r1 - r0)
            return acc

        xb16 = unpack(cur, rows)
        acts = []
        for n0 in range(0, D_EXPERT, MXU_TILE):
            gate = dot_pieces(xb16, wg_s, n0)
            up = dot_pieces(xb16, wu_s, n0)
            acts.append(((gate * jax.nn.sigmoid(gate)) * up).astype(BF16))
        h = jnp.concatenate(acts, axis=-1)
        for n0 in range(0, D_MODEL, MXU_TILE):
            y = dot_pieces(h, wd_s, n0)
            for s in range(MXU_TILE // LANES):
                yb_ref[pl.ds(n0 // LANES + s, rows, stride=ROW_TILES), :] = (
                    y[:, s * LANES:(s + 1) * LANES])
        if rows < MOE_BLK:
            yb_ref[pl.ds(rows * ROW_TILES, (MOE_BLK - rows) * ROW_TILES), :] = jnp.zeros(
                ((MOE_BLK - rows) * ROW_TILES, LANES), F32)

    @pl.when(j >= n_used)
    def _():
        yb_ref[...] = jnp.zeros_like(yb_ref)

    @pl.when(j < n_used)
    def _():
        slot = j % 2
        other = 1 - slot

        @pl.when(j == 0)
        def _():
            idx_copy(0, 0).start()
            idx_copy(0, 0).wait()

            def body(c, carry):
                copy_rows(xbuf_a, 0, c * DMA_UNROLL, DMA_UNROLL)
                return carry
            lax.fori_loop(0, MOE_BLK // DMA_UNROLL, body, 0)
            idx_copy(jnp.minimum(1, last), 1).start()

        e = be_ref[j]
        wslot = ws_ref[j]

        def weight_copies(expert, sl):
            return [pltpu.make_async_copy(src.at[expert], dst.at[sl], wsems.at[sl])
                    for src, dst in ((wg_hbm, wbuf_g), (wu_hbm, wbuf_u), (wd_hbm, wbuf_d))]

        @pl.when(j == 0)
        def _():
            for cp in weight_copies(e, wslot):
                cp.start()

        @pl.when((j == 0) | (e != be_ref[jnp.maximum(j - 1, 0)]))
        def _():
            for cp in weight_copies(e, wslot):
                cp.wait()
            wg_s[...] = wbuf_g[wslot].astype(BF16)
            wu_s[...] = wbuf_u[wslot].astype(BF16)
            wd_s[...] = wbuf_d[wslot].astype(BF16)

            @pl.when(ne_ref[j] >= 0)
            def _():
                for cp in weight_copies(ne_ref[j], 1 - wslot):
                    cp.start()

        idx_copy(0, other).wait()

        short = bv_ref[j] <= MOE_BLK // 2
        for par, (cur, nxt) in enumerate(((xbuf_a, xbuf_b), (xbuf_b, xbuf_a))):
            @pl.when((slot == par) & jnp.logical_not(short))
            def _(cur=cur, nxt=nxt, par=par):
                compute(cur, nxt, 1 - par, MOE_BLK)

            @pl.when((slot == par) & short)
            def _(cur=cur, nxt=nxt, par=par):
                compute(cur, nxt, 1 - par, MOE_BLK // 2)

        @pl.when(j < last)
        def _():
            idx_copy(jnp.minimum(j + 2, last), slot).start()


def _experts(block_e, weight_slot, next_e, block_valid, n_used, slot_asg, mt, w_gate, w_up, w_down, n_blocks):
    return pl.pallas_call(
        _expert_kernel,
        grid_spec=pltpu.PrefetchScalarGridSpec(
            num_scalar_prefetch=5,
            grid=(n_blocks,),
            in_specs=[
                pl.BlockSpec(memory_space=pl.ANY),
                pl.BlockSpec(memory_space=pltpu.VMEM),
                pl.BlockSpec(memory_space=pl.ANY),
                pl.BlockSpec(memory_space=pl.ANY),
                pl.BlockSpec(memory_space=pl.ANY),
            ],
            out_specs=pl.BlockSpec((MOE_BLK * ROW_TILES, LANES), lambda j, *_: (j, 0)),
            scratch_shapes=[
                pltpu.VMEM((MOE_BLK * PACK_TILES, LANES), jnp.uint32),
                pltpu.VMEM((MOE_BLK * PACK_TILES, LANES), jnp.uint32),
                pltpu.SMEM((2, 1, MOE_BLK), jnp.int32),
                pltpu.SemaphoreType.DMA((2,)),
                pltpu.VMEM((2, D_MODEL, D_EXPERT), F32),
                pltpu.VMEM((2, D_MODEL, D_EXPERT), F32),
                pltpu.VMEM((2, D_EXPERT, D_MODEL), F32),
                pltpu.SemaphoreType.DMA((2,)),
                pltpu.VMEM((D_MODEL, D_EXPERT), BF16),
                pltpu.VMEM((D_MODEL, D_EXPERT), BF16),
                pltpu.VMEM((D_EXPERT, D_MODEL), BF16),
            ],
        ),
        out_shape=jax.ShapeDtypeStruct((n_blocks * MOE_BLK * ROW_TILES, LANES), F32),
        compiler_params=pltpu.CompilerParams(
            dimension_semantics=("arbitrary",), vmem_limit_bytes=EXPERT_VMEM_LIMIT),
        name="expert",
    )(block_e, weight_slot, next_e, block_valid, n_used, slot_asg.reshape(n_blocks, 1, MOE_BLK), mt, w_gate,
      w_up, w_down)


def _combine_kernel(dest_ref, yb_hbm, x1_ref, route_ref, mod_ref, g_ref, o_ref, ybuf, sems, *, tc):
    i = pl.program_id(0)
    slot = i % 2

    def row_copy(d, sl, k, r):
        return pltpu.make_async_copy(_row_tile(yb_hbm, d), _row_tile(ybuf.at[sl, k], r), sems.at[sl])

    def gather(step, sl):
        def issue(c, carry):
            for u in range(DMA_UNROLL):
                r = c * DMA_UNROLL + u
                for k in range(TOP_K):
                    row_copy(dest_ref[TOP_K * (step * tc + r) + k], sl, k, r).start(priority=k)
            return carry
        lax.fori_loop(0, tc // DMA_UNROLL, issue, 0)

    @pl.when(i == 0)
    def _():
        gather(0, 0)

    @pl.when(i + 1 < pl.num_programs(0))
    def _():
        gather(i + 1, 1 - slot)

    def drain(c, carry):
        for u in range(DMA_UNROLL * TOP_K):
            row_copy(0, slot, 0, 0).wait()
        return carry
    lax.fori_loop(0, tc // DMA_UNROLL, drain, 0)

    def rows(k):
        return jnp.concatenate(
            [ybuf[slot, k, pl.ds(s, tc, stride=ROW_TILES), :] for s in range(ROW_TILES)], axis=-1)

    route = route_ref[...]
    y = route[:, 2:3] * rows(0) + route[:, 3:4] * rows(1)
    x2 = x1_ref[...] + mod_ref[0, 5:6, :] * y
    ms = jnp.mean(x2 * x2, axis=-1, keepdims=True)
    o_ref[...] = x2 * lax.rsqrt(ms + NORM_EPS) * g_ref[...]


def _combine(dest, yb, x1_2d, route, mod3, final_g, seq):
    t_all, d = x1_2d.shape
    tc = min(COMBINE_TILE, seq)
    per_seq = seq // tc
    return pl.pallas_call(
        functools.partial(_combine_kernel, tc=tc),
        grid_spec=pltpu.PrefetchScalarGridSpec(
            num_scalar_prefetch=1,
            grid=(t_all // tc,),
            in_specs=[
                pl.BlockSpec(memory_space=pl.ANY),
                pl.BlockSpec((tc, d), lambda i, dest: (i, 0)),
                pl.BlockSpec((tc, LANES), lambda i, dest: (i, 0)),
                pl.BlockSpec((1, 6, d), lambda i, dest: (i // per_seq, 0, 0)),
                pl.BlockSpec((1, d), lambda i, dest: (0, 0)),
            ],
            out_specs=pl.BlockSpec((tc, d), lambda i, dest: (i, 0)),
            scratch_shapes=[
                pltpu.VMEM((2, TOP_K, tc * ROW_TILES, LANES), F32),
                pltpu.SemaphoreType.DMA((2,)),
            ],
        ),
        out_shape=jax.ShapeDtypeStruct((t_all, d), F32),
        compiler_params=pltpu.CompilerParams(
            dimension_semantics=("arbitrary",), vmem_limit_bytes=VMEM_LIMIT),
        name="combine",
    )(dest, yb, x1_2d, route, mod3, final_g.reshape(1, d))


def kernel(x, c, ctx, c_ctx, ada_w, ada_b, norm1_g, norm2_g, w_in, rnn_conv_w, rnn_conv_b, rg_wa, rg_ba,
           rg_wx, rg_bx, rg_lambda, sc_conv_w, w_out, router_group_w, router_group_b, router_exp_w,
           router_exp_b, exp_w_gate, exp_w_up, exp_w_down, final_norm_g):
    bn, seq, d = x.shape
    assert d == D_MODEL and bn < MOD_ROWS and ada_w.shape[0] == 1
    t_all = bn * seq

    cc = jnp.concatenate([c, c_ctx[None], jnp.zeros((MOD_ROWS - bn - 1, d), F32)], axis=0)
    mod3 = _modulation(cc, ada_w[0], ada_b[0]).reshape(MOD_ROWS, 6, d)

    xr, gr, u, bg = _inproj(x, mod3, None, norm1_g[0], w_in[0], latent=True)
    (xr_c,) = _inproj(ctx, mod3, bn, norm1_g[0], w_in[0], latent=False)

    wg, bgate = _gate_weights(rg_wa[0], rg_ba[0], rg_wx[0], rg_bx[0])
    assert D_RNN == D_CONV
    y_rnn, y_conv = _mixers(xr, xr_c, gr, rnn_conv_w[0], rnn_conv_b[0], wg, bgate, rg_lambda[0], u, bg,
                            sc_conv_w[0])

    wr = jnp.zeros((d, LANES), F32)
    wr = wr.at[:, :N_GROUPS].set(router_group_w[0]).at[:, EXPERT_LANE0:EXPERT_LANE0 + N_EXPERTS].set(router_exp_w[0])
    br = jnp.zeros((1, LANES), F32)
    br = br.at[0, :N_GROUPS].set(router_group_b[0]).at[0, EXPERT_LANE0:EXPERT_LANE0 + N_EXPERTS].set(router_exp_b[0])
    x1, mt, route, cnt = _outproj(x, y_rnn, y_conv, w_out[0], mod3, norm2_g[0], wr.astype(BF16), br)

    n_assign = t_all * TOP_K
    n_blocks = (n_assign + N_EXPERTS * (MOE_BLK - 1) + MOE_BLK - 1) // MOE_BLK
    counts = cnt[0, EXPERT_LANE0:EXPERT_LANE0 + N_EXPERTS].astype(jnp.int32)
    pcounts = (counts + MOE_BLK - 1) // MOE_BLK * MOE_BLK
    pends = jnp.cumsum(pcounts)
    pstarts = pends - pcounts
    experts = route[:, 0:TOP_K].astype(jnp.int32)
    ranks = route[:, 4:4 + TOP_K].astype(jnp.int32)
    onehot = experts[:, :, None] == jnp.arange(N_EXPERTS, dtype=jnp.int32)
    dest = (ranks + jnp.sum(jnp.where(onehot, pstarts, 0), axis=-1)).reshape(n_assign)
    n_used = (pends[-1] // MOE_BLK).astype(jnp.int32)
    blk_start = jnp.arange(n_blocks, dtype=jnp.int32) * MOE_BLK
    block_e = jnp.minimum(jnp.sum(blk_start[:, None] >= pends[None, :], axis=1), N_EXPERTS - 1)
    last_e = jnp.max(jnp.where(counts > 0, jnp.arange(N_EXPERTS, dtype=jnp.int32), 0))
    block_e = jnp.where(blk_start < pends[-1], block_e, last_e).astype(jnp.int32)
    eids = jnp.arange(N_EXPERTS, dtype=jnp.int32)
    used = counts > 0
    slot_of_e = (jnp.cumsum(used.astype(jnp.int32)) - 1) % 2
    later = jnp.where(used[None, :] & (eids[None, :] > eids[:, None]), eids[None, :], N_EXPERTS)
    next_of_e = jnp.min(later, axis=1)
    next_of_e = jnp.where(next_of_e == N_EXPERTS, -1, next_of_e)
    is_e = block_e[:, None] == eids[None, :]
    weight_slot = jnp.sum(jnp.where(is_e, slot_of_e[None, :], 0), axis=1).astype(jnp.int32)
    next_e = jnp.sum(jnp.where(is_e, next_of_e[None, :], 0), axis=1).astype(jnp.int32)
    rows_end = jnp.sum(jnp.where(is_e, (pstarts + counts)[None, :], 0), axis=1)
    block_valid = jnp.clip(rows_end - blk_start, 0, MOE_BLK).astype(jnp.int32)

    n_slots = n_blocks * MOE_BLK
    slot_asg = _slotmap(dest, n_slots)
    yb = _experts(block_e, weight_slot, next_e, block_valid, n_used.reshape(1), slot_asg, mt, exp_w_gate[0],
                  exp_w_up[0], exp_w_down[0], n_blocks)
    out = _combine(dest, yb, x1.reshape(t_all, d), route, mod3, final_norm_g, seq)
    return out.reshape(bn, seq, d)
```

```python
import functools

import jax
import jax.numpy as jnp
from jax import lax
from jax.experimental import pallas as pl
from jax.experimental.pallas import tpu as pltpu

F32 = jnp.float32
BF16 = jnp.bfloat16

D_MODEL = 1024
D_RNN = 512
D_CONV = 512
D_CONV_H = D_CONV // 2
RNN_HEADS = 8
RNN_HEAD_DIM = D_RNN // RNN_HEADS
GRID_W = 64
HALO = GRID_W
RG_C = 8.0
N_GROUPS = 4
EXPERTS_PER_GROUP = 8
N_EXPERTS = N_GROUPS * EXPERTS_PER_GROUP
TOP_K = 2
D_EXPERT = 512
NORM_EPS = 1e-6
F32_TINY = 1.1754944e-38

LANES = 128
SUBLANES = 8
ROW_TILES = D_MODEL // LANES
PACK_TILES = ROW_TILES // 2
N_LANE_GROUPS = D_RNN // LANES
EXPERT_LANE0 = N_GROUPS

MOD_ROWS = 16
MOD_TN = 768
INPROJ_TILE = 1024
OUTPROJ_TILE = 1024
COEFF_ROWS = 2048
SCAN_UNROLL = 16
MOE_BLK = 512
MXU_TILE = 256
COMBINE_TILE = 256
DMA_UNROLL = 16
VMEM_LIMIT = 48 * 1024 * 1024
EXPERT_VMEM_LIMIT = 58 * 1024 * 1024


def _dot(a, b):
    return jnp.dot(a, b, preferred_element_type=F32)


def _split_bf16(x):
    hi = x.astype(BF16)
    lo = (x - hi.astype(F32)).astype(BF16)
    return hi, lo


def _mod_kernel(cc_ref, w_ref, b_ref, o_ref):
    s = cc_ref[...]
    s = s * jax.nn.sigmoid(s)
    s_hi, s_lo = _split_bf16(s)
    w_hi, w_lo = _split_bf16(w_ref[...])
    o_ref[...] = _dot(s_hi, w_hi) + _dot(s_lo, w_hi) + _dot(s_hi, w_lo) + b_ref[...]


def _modulation(cc, ada_w, ada_b):
    n = ada_w.shape[1]
    return pl.pallas_call(
        _mod_kernel,
        grid=(n // MOD_TN,),
        in_specs=[
            pl.BlockSpec((MOD_ROWS, D_MODEL), lambda j: (0, 0)),
            pl.BlockSpec((D_MODEL, MOD_TN), lambda j: (0, j)),
            pl.BlockSpec((1, MOD_TN), lambda j: (0, j)),
        ],
        out_specs=pl.BlockSpec((MOD_ROWS, MOD_TN), lambda j: (0, j)),
        out_shape=jax.ShapeDtypeStruct((MOD_ROWS, n), F32),
        compiler_params=pltpu.CompilerParams(vmem_limit_bytes=VMEM_LIMIT),
        name="mod",
    )(cc, ada_w, ada_b.reshape(1, n))


def _norm_mod(x, g, scale, shift):
    ms = jnp.mean(x * x, axis=-1, keepdims=True)
    y = x * lax.rsqrt(ms + NORM_EPS) * g
    return y * (1.0 + scale) + shift


def _inproj_kernel(x_ref, mod_ref, g_ref, w32_ref, *refs, latent):
    out_refs, w_ref = refs[:-1], refs[-1]

    @pl.when((pl.program_id(0) == 0) & (pl.program_id(1) == 0))
    def _():
        w_ref[...] = w32_ref[...].astype(BF16)

    h = _norm_mod(x_ref[0], g_ref[...], mod_ref[0, 1:2, :], mod_ref[0, 0:1, :])
    hb = h.astype(BF16)
    xr = _dot(hb, w_ref[:, 0:D_RNN])
    out_refs[0][0] = xr
    if latent:
        o = D_RNN
        out_refs[1][0] = jax.nn.gelu(_dot(hb, w_ref[:, o:o + D_RNN]), approximate=True)
        o += D_RNN
        v = _dot(hb, w_ref[:, o:o + D_CONV])
        out_refs[3][0] = _dot(hb, w_ref[:, o + D_CONV:o + 2 * D_CONV])
        cg = _dot(hb, w_ref[:, o + 2 * D_CONV:o + 3 * D_CONV])
        out_refs[2][0] = cg * v


def _inproj(x, mod3, mod_row, norm_g, w_in, latent):
    bn, n, d = x.shape
    tm = min(INPROJ_TILE, n)
    n_out = 4 if latent else 1
    width = w_in.shape[1] if latent else D_RNN
    mod_map = (lambda b, i: (b, 0, 0)) if mod_row is None else (lambda b, i: (mod_row, 0, 0))
    return pl.pallas_call(
        functools.partial(_inproj_kernel, latent=latent),
        grid=(bn, n // tm),
        in_specs=[
            pl.BlockSpec((1, tm, d), lambda b, i: (b, i, 0)),
            pl.BlockSpec((1, 6, d), mod_map),
            pl.BlockSpec((1, d), lambda b, i: (0, 0)),
            pl.BlockSpec((d, width), lambda b, i: (0, 0), pipeline_mode=pl.Buffered(1)),
        ],
        out_specs=[pl.BlockSpec((1, tm, D_RNN), lambda b, i: (b, i, 0))] * n_out,
        out_shape=[jax.ShapeDtypeStruct((bn, n, D_RNN), F32)] * n_out,
        scratch_shapes=[pltpu.VMEM((d, width), BF16)],
        compiler_params=pltpu.CompilerParams(
            dimension_semantics=("arbitrary", "arbitrary"), vmem_limit_bytes=VMEM_LIMIT),
        name="inproj_lat" if latent else "inproj_ctx",
    )(x, mod3, norm_g.reshape(1, d), w_in)


def _scan_pitch(chunk):
    pitch = chunk + SUBLANES
    return pitch if (pitch // SUBLANES) % 2 else pitch + SUBLANES


def _rnn_kernel(xr_ref, xrc_ref, gr_ref, cw_ref, cb_ref, wg_ref, bg_ref, lam_ref, u_ref, bgc_ref, scw_ref,
                y_ref, yc_ref,
                xc_s, ap_f, bp_f, ap_b, bp_b, hl_f, al_f, hl_b, al_b, hp_f, hp_b, xpad, *, n_lat, n_ctx):
    _gconv_block(u_ref, bgc_ref, scw_ref, yc_ref, xpad)

    nl = -lam_ref[...]
    sp = jnp.maximum(nl, 0.0) + jnp.log1p(jnp.exp(-jnp.abs(nl)))
    c1 = (0.5 * RG_C) * sp
    cw = cw_ref[...]
    bias = cb_ref[...]
    wg = wg_ref[0]
    bg = bg_ref[0]
    dirs = ((ap_f, bp_f, hl_f, al_f, hp_f), (ap_b, bp_b, hl_b, al_b, hp_b))

    def conv_into(x, n):
        halo = jnp.zeros((SUBLANES, LANES), F32)
        xpad[pl.ds(HALO - SUBLANES, SUBLANES), :] = halo
        xpad[pl.ds(HALO, n), :] = x
        xpad[pl.ds(HALO + n, SUBLANES), :] = halo
        xc_s[pl.ds(0, n), :] = (cw[0:1] * xpad[pl.ds(HALO - 2, n), :]
                                + cw[1:2] * xpad[pl.ds(HALO - 1, n), :] + cw[2:3] * x
                                + cw[3:4] * xpad[pl.ds(HALO + 1, n), :]) + bias

    def coefficients(n):
        chunk = n // SUBLANES
        pitch = _scan_pitch(chunk)
        rows = max(chunk, min(n, COEFF_ROWS))
        per = rows // chunk

        def body(i, carry):
            xc = xc_s[pl.ds(pl.multiple_of(i * rows, SUBLANES), rows), :]
            gates = _dot(xc.astype(BF16), wg) + bg
            half_xc = 0.5 * xc
            for d in range(2):
                tr = jnp.tanh(gates[:, (2 * d) * LANES:(2 * d + 1) * LANES])
                ti = jnp.tanh(gates[:, (2 * d + 1) * LANES:(2 * d + 2) * LANES])
                neg_log_a = c1[d:d + 1] + c1[d:d + 1] * tr
                a = jnp.exp(-neg_log_a)
                y = jnp.tanh(neg_log_a) * (a * a + 1.0)
                b = (y * lax.rsqrt(jnp.maximum(y, F32_TINY))) * (half_xc + half_xc * ti)
                for k in range(per):
                    dst = pl.multiple_of((i * per + k) * pitch, SUBLANES)
                    dirs[d][0][pl.ds(dst, chunk), :] = a[k * chunk:(k + 1) * chunk]
                    dirs[d][1][pl.ds(dst, chunk), :] = b[k * chunk:(k + 1) * chunk]
            return carry

        if n == rows:
            body(0, 0)
        else:
            lax.fori_loop(0, n // rows, body, 0)

    def scan(n, h0_f, h0_b, keep):
        chunk = n // SUBLANES
        pitch = _scan_pitch(chunk)

        def steps(jo, carry):
            h_f, a_f, h_b, a_b = carry
            for u in range(SCAN_UNROLL):
                j = jo * SCAN_UNROLL + u
                av = ap_f[pl.ds(j, SUBLANES, stride=pitch), :]
                h_f = av * h_f + bp_f[pl.ds(j, SUBLANES, stride=pitch), :]
                a_f = av * a_f
                jb = chunk - 1 - j
                av = ap_b[pl.ds(jb, SUBLANES, stride=pitch), :]
                h_b = av * h_b + bp_b[pl.ds(jb, SUBLANES, stride=pitch), :]
                a_b = av * a_b
                if keep:
                    o = pl.multiple_of(j * SUBLANES, SUBLANES)
                    hl_f[pl.ds(o, SUBLANES), :] = h_f
                    al_f[pl.ds(o, SUBLANES), :] = a_f
                    hl_b[pl.ds(o, SUBLANES), :] = h_b
                    al_b[pl.ds(o, SUBLANES), :] = a_b
            return h_f, a_f, h_b, a_b

        zeros = jnp.zeros((SUBLANES, LANES), F32)
        ones = jnp.ones((SUBLANES, LANES), F32)
        h_f, a_f, h_b, a_b = lax.fori_loop(0, chunk // SCAN_UNROLL, steps, (zeros, ones, zeros, ones))

        in_f = [h0_f]
        for c in range(SUBLANES):
            in_f.append(a_f[c:c + 1] * in_f[c] + h_f[c:c + 1])
        in_b = [h0_b]
        for c in range(SUBLANES - 1, -1, -1):
            in_b.append(a_b[c:c + 1] * in_b[-1] + h_b[c:c + 1])
        if keep:
            hin_f = jnp.concatenate(in_f[:SUBLANES], axis=0)
            hin_b = jnp.concatenate(in_b[SUBLANES - 1::-1], axis=0)

            def fix(jo, carry):
                for u in range(SCAN_UNROLL):
                    j = jo * SCAN_UNROLL + u
                    o = pl.multiple_of(j * SUBLANES, SUBLANES)
                    hp_f[pl.ds(j, SUBLANES, stride=pitch), :] = (
                        hl_f[pl.ds(o, SUBLANES), :] + al_f[pl.ds(o, SUBLANES), :] * hin_f)
                    hp_b[pl.ds(chunk - 1 - j, SUBLANES, stride=pitch), :] = (
                        hl_b[pl.ds(o, SUBLANES), :] + al_b[pl.ds(o, SUBLANES), :] * hin_b)
                return carry
            lax.fori_loop(0, chunk // SCAN_UNROLL, fix, 0)
        return in_f[SUBLANES], in_b[SUBLANES]

    zero = jnp.zeros((1, LANES), F32)
    conv_into(xrc_ref[0], n_ctx)
    coefficients(n_ctx)
    h0_f, h0_b = scan(n_ctx, zero, zero, keep=False)

    conv_into(xr_ref[0], n_lat)
    coefficients(n_lat)
    scan(n_lat, h0_f, h0_b, keep=True)

    chunk = n_lat // SUBLANES
    pitch = _scan_pitch(chunk)

    def emit(c, carry):
        src = pl.multiple_of(c * chunk, 2 * SUBLANES)
        dst = pl.multiple_of(c * pitch, SUBLANES)
        hsum = hp_f[pl.ds(dst, chunk), :] + hp_b[pl.ds(dst, chunk), :]
        y = gr_ref[0, pl.ds(src, chunk), :] * hsum
        y_ref[0, pl.ds(src, chunk), :] = y.astype(y_ref.dtype)
        return carry
    lax.fori_loop(0, SUBLANES, emit, 0)


def _mixers(xr, xr_c, gr, conv_w, conv_b, wg, bgate, lam, u, bg, sc_w):
    bn, n, _ = xr.shape
    n_ctx = xr_c.shape[1]
    assert n % (SUBLANES * SCAN_UNROLL) == 0 and n_ctx % (SUBLANES * SCAN_UNROLL) == 0 and n_ctx <= n
    pitched = SUBLANES * _scan_pitch(n // SUBLANES)
    seq_spec = pl.BlockSpec((1, n, LANES), lambda b, p: (b, 0, p))
    return pl.pallas_call(
        functools.partial(_rnn_kernel, n_lat=n, n_ctx=n_ctx),
        grid=(bn, N_LANE_GROUPS),
        in_specs=[
            seq_spec,
            pl.BlockSpec((1, n_ctx, LANES), lambda b, p: (b, 0, p)),
            seq_spec,
            pl.BlockSpec((4, LANES), lambda b, p: (0, p)),
            pl.BlockSpec((1, LANES), lambda b, p: (0, p)),
            pl.BlockSpec((1, LANES, 4 * LANES), lambda b, p: (p, 0, 0)),
            pl.BlockSpec((1, 1, 4 * LANES), lambda b, p: (p, 0, 0)),
            pl.BlockSpec((2, LANES), lambda b, p: (0, p)),
            seq_spec,
            seq_spec,
            pl.BlockSpec((3, LANES), lambda b, p: (0, p)),
        ],
        out_specs=[seq_spec, seq_spec],
        out_shape=[jax.ShapeDtypeStruct((bn, n, D_RNN), BF16)] * 2,
        scratch_shapes=[pltpu.VMEM((n, LANES), F32)]
        + [pltpu.VMEM((pitched, LANES), F32)] * 4
        + [pltpu.VMEM((n, LANES), F32)] * 4
        + [pltpu.VMEM((pitched, LANES), F32)] * 2
        + [pltpu.VMEM((n + 2 * HALO, LANES), F32)],
        compiler_params=pltpu.CompilerParams(vmem_limit_bytes=VMEM_LIMIT),
        name="mixers",
    )(xr, xr_c, gr, conv_w, conv_b.reshape(1, D_RNN), wg, bgate, lam, u, bg, sc_w)


def _gate_weights(rg_wa, rg_ba, rg_wx, rg_bx):
    eye = jnp.eye(2, dtype=F32)
    blocks, biases = [], []
    for d in range(2):
        for w, bvec in ((rg_wa[d], rg_ba[d]), (rg_wx[d], rg_bx[d])):
            w4 = w.reshape(N_LANE_GROUPS, 2, RNN_HEAD_DIM, RNN_HEAD_DIM)
            bd = jnp.einsum("paij,ac->paicj", w4, eye).reshape(N_LANE_GROUPS, LANES, LANES)
            blocks.append(0.5 * bd)
            biases.append(0.5 * bvec.reshape(N_LANE_GROUPS, 1, LANES))
    return jnp.concatenate(blocks, axis=-1).astype(BF16), jnp.concatenate(biases, axis=-1)


def _gconv_block(u_ref, bg_ref, w_ref, y_ref, pad):
    p = pl.program_id(1)
    u = u_ref[0]
    n = u.shape[0]
    w = w_ref[...]
    halo = jnp.zeros((HALO, LANES), F32)
    pad[pl.ds(0, HALO), :] = halo
    pad[pl.ds(HALO, n), :] = u
    pad[pl.ds(HALO + n, HALO), :] = halo

    @pl.when(p < D_CONV_H // LANES)
    def _():
        col = lax.broadcasted_iota(jnp.int32, u.shape, 0) % GRID_W
        left = jnp.where(col > 0, pad[pl.ds(HALO - 1, n), :], 0.0)
        right = jnp.where(col < GRID_W - 1, pad[pl.ds(HALO + 1, n), :], 0.0)
        y_ref[0] = (bg_ref[0] * (w[0:1] * left + w[1:2] * u + w[2:3] * right)).astype(y_ref.dtype)

    @pl.when(p >= D_CONV_H // LANES)
    def _():
        y_ref[0] = (bg_ref[0] * (w[0:1] * pad[pl.ds(HALO - GRID_W, n), :] + w[1:2] * u
                                 + w[2:3] * pad[pl.ds(HALO + GRID_W, n), :])).astype(y_ref.dtype)


def _lane_max(x, mask):
    return jnp.max(jnp.where(mask, x, -jnp.inf), axis=-1, keepdims=True)


def _first_lane(cond, lane):
    return jnp.min(jnp.where(cond, lane, float(LANES)), axis=-1, keepdims=True)


def _outproj_kernel(x_ref, yr_ref, yc_ref, w32_ref, mod_ref, g_ref, wr_ref, br_ref,
                    x1_ref, mt_ref, route_ref, cnt_ref, carry, w_ref, m_s, *, tm):
    s = pl.program_id(0)

    @pl.when(s == 0)
    def _():
        carry[...] = jnp.zeros_like(carry)
        w_ref[...] = w32_ref[...].astype(BF16)
        m_s[...] = jnp.zeros_like(m_s)

    logits = _dot(m_s[...], wr_ref[...]) + br_ref[...]
    lane_i = lax.broadcasted_iota(jnp.int32, logits.shape, 1)
    lane = lane_i.astype(F32)
    is_grp = lane_i < N_GROUPS
    g_max = _lane_max(logits, is_grp)
    grp = _first_lane(is_grp & (logits == g_max), lane)
    p_g = 1.0 / jnp.sum(jnp.where(is_grp, jnp.exp(logits - g_max), 0.0), axis=-1, keepdims=True)
    lo_lane = EXPERT_LANE0 + grp * EXPERTS_PER_GROUP
    in_grp = (lane >= lo_lane) & (lane < lo_lane + EXPERTS_PER_GROUP)
    l1 = _lane_max(logits, in_grp)
    i1 = _first_lane(in_grp & (logits == l1), lane)
    rest = in_grp & (lane != i1)
    l2 = _lane_max(logits, rest)
    i2 = _first_lane(rest & (logits == l2), lane)
    r21 = jnp.exp(l2 - l1)
    gate1 = p_g / (1.0 + r21)
    gate2 = gate1 * r21

    oh1 = jnp.where(lane == i1, 1.0, 0.0)
    oh2 = jnp.where(lane == i2, 1.0, 0.0)
    both = (oh1 + oh2).astype(BF16)
    ti = lax.broadcasted_iota(jnp.int32, (tm, tm), 0)
    tj = lax.broadcasted_iota(jnp.int32, (tm, tm), 1)
    tri = jnp.where(tj < ti, 1.0, 0.0).astype(BF16)
    counts = carry[...]
    before = _dot(tri, both) + counts
    rank1 = jnp.sum(oh1 * before, axis=-1, keepdims=True)
    rank2 = jnp.sum(oh2 * before, axis=-1, keepdims=True)
    out = jnp.zeros(logits.shape, F32)
    for k, val in enumerate((i1 - EXPERT_LANE0, i2 - EXPERT_LANE0, gate1, gate2, rank1, rank2)):
        out = jnp.where(lane_i == k, val, out)
    route_ref[...] = out
    real = jnp.where(s > 0, 1.0, 0.0)
    total = counts + real * jnp.sum(oh1 + oh2, axis=0, keepdims=True)
    carry[...] = total
    cnt_ref[...] = total

    mix = _dot(yr_ref[0], w_ref[0:D_RNN, :]) + _dot(yc_ref[0], w_ref[D_RNN:, :])
    x1 = x_ref[0] + mod_ref[0, 2:3, :] * mix
    x1_ref[0] = x1
    m_new = _norm_mod(x1, g_ref[...], mod_ref[0, 4:5, :], mod_ref[0, 3:4, :])
    m_s[...] = m_new.astype(BF16)
    half = D_MODEL // 2
    packed = pltpu.pack_elementwise([m_new[:, :half], m_new[:, half:]], packed_dtype=BF16)
    for q in range(PACK_TILES):
        mt_ref[pl.ds(q, tm, stride=PACK_TILES), :] = packed[:, q * LANES:(q + 1) * LANES]


def _outproj(x, y_rnn, y_conv, w_out, mod3, norm_g, wr, br):
    bn, n, d = x.shape
    tm = min(OUTPROJ_TILE, n)
    nt = n // tm
    n_tiles = bn * nt
    t_all = bn * n

    def cur(s):
        return jnp.minimum(s, n_tiles - 1)

    def prev(s):
        return jnp.maximum(s - 1, 0)

    def seq_map(s):
        return (cur(s) // nt, cur(s) % nt, 0)

    const = lambda s: (0, 0)
    return pl.pallas_call(
        functools.partial(_outproj_kernel, tm=tm),
        grid=(n_tiles + 1,),
        in_specs=[
            pl.BlockSpec((1, tm, d), seq_map),
            pl.BlockSpec((1, tm, D_RNN), seq_map),
            pl.BlockSpec((1, tm, D_CONV), seq_map),
            pl.BlockSpec((D_RNN + D_CONV, d), const, pipeline_mode=pl.Buffered(1)),
            pl.BlockSpec((1, 6, d), lambda s: (cur(s) // nt, 0, 0)),
            pl.BlockSpec((1, d), const),
            pl.BlockSpec((d, LANES), const),
            pl.BlockSpec((1, LANES), const),
        ],
        out_specs=[
            pl.BlockSpec((1, tm, d), seq_map),
            pl.BlockSpec((tm * PACK_TILES, LANES), lambda s: (cur(s), 0)),
            pl.BlockSpec((tm, LANES), lambda s: (prev(s), 0)),
            pl.BlockSpec((1, LANES), const),
        ],
        out_shape=[
            jax.ShapeDtypeStruct((bn, n, d), F32),
            jax.ShapeDtypeStruct((t_all * PACK_TILES, LANES), jnp.uint32),
            jax.ShapeDtypeStruct((t_all, LANES), F32),
            jax.ShapeDtypeStruct((1, LANES), F32),
        ],
        scratch_shapes=[pltpu.VMEM((1, LANES), F32), pltpu.VMEM((D_RNN + D_CONV, d), BF16),
                        pltpu.VMEM((tm, d), BF16)],
        compiler_params=pltpu.CompilerParams(
            dimension_semantics=("arbitrary",), vmem_limit_bytes=VMEM_LIMIT),
        name="outproj",
    )(x, y_rnn, y_conv, w_out, mod3, norm_g.reshape(1, d), wr, br)


def _row_tile(ref, row):
    return ref.at[pl.ds(pl.multiple_of(row * ROW_TILES, ROW_TILES), ROW_TILES)]


def _slotmap_kernel(dest_ref, zeros_hbm, asg_ref, sem):
    fill = pltpu.make_async_copy(zeros_hbm, asg_ref, sem)
    fill.start()
    fill.wait()

    def body(c, carry):
        for u in range(DMA_UNROLL):
            a = c * DMA_UNROLL + u
            asg_ref[dest_ref[a]] = a
        return carry
    lax.fori_loop(0, dest_ref.shape[0] // DMA_UNROLL, body, 0)


def _slotmap(dest, n_slots):
    return pl.pallas_call(
        _slotmap_kernel,
        in_specs=[pl.BlockSpec(memory_space=pltpu.SMEM), pl.BlockSpec(memory_space=pl.ANY)],
        out_specs=pl.BlockSpec(memory_space=pltpu.SMEM),
        out_shape=jax.ShapeDtypeStruct((n_slots,), jnp.int32),
        scratch_shapes=[pltpu.SemaphoreType.DMA],
        name="slotmap",
    )(dest, jnp.zeros((n_slots,), jnp.int32))


def _expert_kernel(be_ref, ws_ref, ne_ref, bv_ref, nu_ref, asg_hbm, m_ref, wg_hbm, wu_hbm, wd_hbm, yb_ref,
                   xbuf_a, xbuf_b, idx, isems, wbuf_g, wbuf_u, wbuf_d, wsems, wg_s, wu_s, wd_s):
    j = pl.program_id(0)
    n_used = nu_ref[0]
    last = n_used - 1

    def idx_copy(blk, sl):
        return pltpu.make_async_copy(asg_hbm.at[blk], idx.at[sl], isems.at[sl])

    def copy_rows(buf, sl, r0, n):
        for u in range(n):
            tok = lax.shift_right_logical(idx[sl, 0, r0 + u], 1)
            src = pl.multiple_of(tok * PACK_TILES, PACK_TILES)
            buf[pl.ds((r0 + u) * PACK_TILES, PACK_TILES), :] = m_ref[pl.ds(src, PACK_TILES), :]

    def unpack(buf, rows):
        halves = ([], [])
        for s in range(PACK_TILES):
            word = buf[pl.ds(s, rows, stride=PACK_TILES), :]
            for k in range(2):
                part = pltpu.unpack_elementwise(word, index=k, packed_dtype=BF16, unpacked_dtype=F32)
                halves[k].append(part.astype(BF16))
        return jnp.concatenate(halves[0] + halves[1], axis=-1)

    n_pieces = 2 * D_EXPERT // MXU_TILE + D_MODEL // MXU_TILE
    bounds = [(p * MOE_BLK) // n_pieces for p in range(n_pieces + 1)]

    def compute(cur, nxt, nxt_sl, rows):
        pieces = iter(zip(bounds[:-1], bounds[1:]))

        def dot_pieces(a, w_ref, n0):
            acc = _dot(a, w_ref[:, n0:n0 + MXU_TILE])
            r0, r1 = next(pieces)
            copy_rows(nxt, nxt_sl, r0, r1 - r0)
            return acc

        xb16 = unpack(cur, rows)
        acts = []
        for n0 in range(0, D_EXPERT, MXU_TILE):
            gate = dot_pieces(xb16, wg_s, n0)
            up = dot_pieces(xb16, wu_s, n0)
            acts.append(((gate * jax.nn.sigmoid(gate)) * up).astype(BF16))
        h = jnp.concatenate(acts, axis=-1)
        for n0 in range(0, D_MODEL, MXU_TILE):
            y = dot_pieces(h, wd_s, n0)
            for s in range(MXU_TILE // LANES):
                yb_ref[pl.ds(n0 // LANES + s, rows, stride=ROW_TILES), :] = (
                    y[:, s * LANES:(s + 1) * LANES])
        if rows < MOE_BLK:
            yb_ref[pl.ds(rows * ROW_TILES, (MOE_BLK - rows) * ROW_TILES), :] = jnp.zeros(
                ((MOE_BLK - rows) * ROW_TILES, LANES), F32)

    @pl.when(j >= n_used)
    def _():
        yb_ref[...] = jnp.zeros_like(yb_ref)

    @pl.when(j < n_used)
    def _():
        slot = j % 2
        other = 1 - slot

        @pl.when(j == 0)
        def _():
            idx_copy(0, 0).start()
            idx_copy(0, 0).wait()

            def body(c, carry):
                copy_rows(xbuf_a, 0, c * DMA_UNROLL, DMA_UNROLL)
                return carry
            lax.fori_loop(0, MOE_BLK // DMA_UNROLL, body, 0)
            idx_copy(jnp.minimum(1, last), 1).start()

        e = be_ref[j]
        wslot = ws_ref[j]

        def weight_copies(expert, sl):
            return [pltpu.make_async_copy(src.at[expert], dst.at[sl], wsems.at[sl])
                    for src, dst in ((wg_hbm, wbuf_g), (wu_hbm, wbuf_u), (wd_hbm, wbuf_d))]

        @pl.when(j == 0)
        def _():
            for cp in weight_copies(e, wslot):
                cp.start()

        @pl.when((j == 0) | (e != be_ref[jnp.maximum(j - 1, 0)]))
        def _():
            for cp in weight_copies(e, wslot):
                cp.wait()
            wg_s[...] = wbuf_g[wslot].astype(BF16)
            wu_s[...] = wbuf_u[wslot].astype(BF16)
            wd_s[...] = wbuf_d[wslot].astype(BF16)

            @pl.when(ne_ref[j] >= 0)
            def _():
                for cp in weight_copies(ne_ref[j], 1 - wslot):
                    cp.start()

        idx_copy(0, other).wait()

        short = bv_ref[j] <= MOE_BLK // 2
        for par, (cur, nxt) in enumerate(((xbuf_a, xbuf_b), (xbuf_b, xbuf_a))):
            @pl.when((slot == par) & jnp.logical_not(short))
            def _(cur=cur, nxt=nxt, par=par):
                compute(cur, nxt, 1 - par, MOE_BLK)

            @pl.when((slot == par) & short)
            def _(cur=cur, nxt=nxt, par=par):
                compute(cur, nxt, 1 - par, MOE_BLK // 2)

        @pl.when(j < last)
        def _():
            idx_copy(jnp.minimum(j + 2, last), slot).start()


def _experts(block_e, weight_slot, next_e, block_valid, n_used, slot_asg, mt, w_gate, w_up, w_down, n_blocks):
    return pl.pallas_call(
        _expert_kernel,
        grid_spec=pltpu.PrefetchScalarGridSpec(
            num_scalar_prefetch=5,
            grid=(n_blocks,),
            in_specs=[
                pl.BlockSpec(memory_space=pl.ANY),
                pl.BlockSpec(memory_space=pltpu.VMEM),
                pl.BlockSpec(memory_space=pl.ANY),
                pl.BlockSpec(memory_space=pl.ANY),
                pl.BlockSpec(memory_space=pl.ANY),
            ],
            out_specs=pl.BlockSpec((MOE_BLK * ROW_TILES, LANES), lambda j, *_: (j, 0)),
            scratch_shapes=[
                pltpu.VMEM((MOE_BLK * PACK_TILES, LANES), jnp.uint32),
                pltpu.VMEM((MOE_BLK * PACK_TILES, LANES), jnp.uint32),
                pltpu.SMEM((2, 1, MOE_BLK), jnp.int32),
                pltpu.SemaphoreType.DMA((2,)),
                pltpu.VMEM((2, D_MODEL, D_EXPERT), F32),
                pltpu.VMEM((2, D_MODEL, D_EXPERT), F32),
                pltpu.VMEM((2, D_EXPERT, D_MODEL), F32),
                pltpu.SemaphoreType.DMA((2,)),
                pltpu.VMEM((D_MODEL, D_EXPERT), BF16),
                pltpu.VMEM((D_MODEL, D_EXPERT), BF16),
                pltpu.VMEM((D_EXPERT, D_MODEL), BF16),
            ],
        ),
        out_shape=jax.ShapeDtypeStruct((n_blocks * MOE_BLK * ROW_TILES, LANES), F32),
        compiler_params=pltpu.CompilerParams(
            dimension_semantics=("arbitrary",), vmem_limit_bytes=EXPERT_VMEM_LIMIT),
        name="expert",
    )(block_e, weight_slot, next_e, block_valid, n_used, slot_asg.reshape(n_blocks, 1, MOE_BLK), mt, w_gate,
      w_up, w_down)


def _combine_kernel(dest_ref, yb_hbm, x1_ref, route_ref, mod_ref, g_ref, o_ref, ybuf, sems, *, tc):
    i = pl.program_id(0)
    slot = i % 2

    def row_copy(d, sl, k, r):
        return pltpu.make_async_copy(_row_tile(yb_hbm, d), _row_tile(ybuf.at[sl, k], r), sems.at[sl])

    def gather(step, sl):
        def issue(c, carry):
            for u in range(DMA_UNROLL):
                r = c * DMA_UNROLL + u
                for k in range(TOP_K):
                    row_copy(dest_ref[TOP_K * (step * tc + r) + k], sl, k, r).start(priority=k)
            return carry
        lax.fori_loop(0, tc // DMA_UNROLL, issue, 0)

    @pl.when(i == 0)
    def _():
        gather(0, 0)

    @pl.when(i + 1 < pl.num_programs(0))
    def _():
        gather(i + 1, 1 - slot)

    def drain(c, carry):
        for u in range(DMA_UNROLL * TOP_K):
            row_copy(0, slot, 0, 0).wait()
        return carry
    lax.fori_loop(0, tc // DMA_UNROLL, drain, 0)

    def rows(k):
        return jnp.concatenate(
            [ybuf[slot, k, pl.ds(s, tc, stride=ROW_TILES), :] for s in range(ROW_TILES)], axis=-1)

    route = route_ref[...]
    y = route[:, 2:3] * rows(0) + route[:, 3:4] * rows(1)
    x2 = x1_ref[...] + mod_ref[0, 5:6, :] * y
    ms = jnp.mean(x2 * x2, axis=-1, keepdims=True)
    o_ref[...] = x2 * lax.rsqrt(ms + NORM_EPS) * g_ref[...]


def _combine(dest, yb, x1_2d, route, mod3, final_g, seq):
    t_all, d = x1_2d.shape
    tc = min(COMBINE_TILE, seq)
    per_seq = seq // tc
    return pl.pallas_call(
        functools.partial(_combine_kernel, tc=tc),
        grid_spec=pltpu.PrefetchScalarGridSpec(
            num_scalar_prefetch=1,
            grid=(t_all // tc,),
            in_specs=[
                pl.BlockSpec(memory_space=pl.ANY),
                pl.BlockSpec((tc, d), lambda i, dest: (i, 0)),
                pl.BlockSpec((tc, LANES), lambda i, dest: (i, 0)),
                pl.BlockSpec((1, 6, d), lambda i, dest: (i // per_seq, 0, 0)),
                pl.BlockSpec((1, d), lambda i, dest: (0, 0)),
            ],
            out_specs=pl.BlockSpec((tc, d), lambda i, dest: (i, 0)),
            scratch_shapes=[
                pltpu.VMEM((2, TOP_K, tc * ROW_TILES, LANES), F32),
                pltpu.SemaphoreType.DMA((2,)),
            ],
        ),
        out_shape=jax.ShapeDtypeStruct((t_all, d), F32),
        compiler_params=pltpu.CompilerParams(
            dimension_semantics=("arbitrary",), vmem_limit_bytes=VMEM_LIMIT),
        name="combine",
    )(dest, yb, x1_2d, route, mod3, final_g.reshape(1, d))


def kernel(x, c, ctx, c_ctx, ada_w, ada_b, norm1_g, norm2_g, w_in, rnn_conv_w, rnn_conv_b, rg_wa, rg_ba,
           rg_wx, rg_bx, rg_lambda, sc_conv_w, w_out, router_group_w, router_group_b, router_exp_w,
           router_exp_b, exp_w_gate, exp_w_up, exp_w_down, final_norm_g):
    bn, seq, d = x.shape
    assert d == D_MODEL and bn < MOD_ROWS and ada_w.shape[0] == 1
    t_all = bn * seq

    cc = jnp.concatenate([c, c_ctx[None], jnp.zeros((MOD_ROWS - bn - 1, d), F32)], axis=0)
    mod3 = _modulation(cc, ada_w[0], ada_b[0]).reshape(MOD_ROWS, 6, d)

    xr, gr, u, bg = _inproj(x, mod3, None, norm1_g[0], w_in[0], latent=True)
    (xr_c,) = _inproj(ctx, mod3, bn, norm1_g[0], w_in[0], latent=False)

    wg, bgate = _gate_weights(rg_wa[0], rg_ba[0], rg_wx[0], rg_bx[0])
    assert D_RNN == D_CONV
    y_rnn, y_conv = _mixers(xr, xr_c, gr, rnn_conv_w[0], rnn_conv_b[0], wg, bgate, rg_lambda[0], u, bg,
                            sc_conv_w[0])

    wr = jnp.zeros((d, LANES), F32)
    wr = wr.at[:, :N_GROUPS].set(router_group_w[0]).at[:, EXPERT_LANE0:EXPERT_LANE0 + N_EXPERTS].set(router_exp_w[0])
    br = jnp.zeros((1, LANES), F32)
    br = br.at[0, :N_GROUPS].set(router_group_b[0]).at[0, EXPERT_LANE0:EXPERT_LANE0 + N_EXPERTS].set(router_exp_b[0])
    x1, mt, route, cnt = _outproj(x, y_rnn, y_conv, w_out[0], mod3, norm2_g[0], wr.astype(BF16), br)

    n_assign = t_all * TOP_K
    n_blocks = (n_assign + N_EXPERTS * (MOE_BLK - 1) + MOE_BLK - 1) // MOE_BLK
    counts = cnt[0, EXPERT_LANE0:EXPERT_LANE0 + N_EXPERTS].astype(jnp.int32)
    pcounts = (counts + MOE_BLK - 1) // MOE_BLK * MOE_BLK
    pends = jnp.cumsum(pcounts)
    pstarts = pends - pcounts
    experts = route[:, 0:TOP_K].astype(jnp.int32)
    ranks = route[:, 4:4 + TOP_K].astype(jnp.int32)
    onehot = experts[:, :, None] == jnp.arange(N_EXPERTS, dtype=jnp.int32)
    dest = (ranks + jnp.sum(jnp.where(onehot, pstarts, 0), axis=-1)).reshape(n_assign)
    n_used = (pends[-1] // MOE_BLK).astype(jnp.int32)
    blk_start = jnp.arange(n_blocks, dtype=jnp.int32) * MOE_BLK
    block_e = jnp.minimum(jnp.sum(blk_start[:, None] >= pends[None, :], axis=1), N_EXPERTS - 1)
    last_e = jnp.max(jnp.where(counts > 0, jnp.arange(N_EXPERTS, dtype=jnp.int32), 0))
    block_e = jnp.where(blk_start < pends[-1], block_e, last_e).astype(jnp.int32)
    eids = jnp.arange(N_EXPERTS, dtype=jnp.int32)
    used = counts > 0
    slot_of_e = (jnp.cumsum(used.astype(jnp.int32)) - 1) % 2
    later = jnp.where(used[None, :] & (eids[None, :] > eids[:, None]), eids[None, :], N_EXPERTS)
    next_of_e = jnp.min(later, axis=1)
    next_of_e = jnp.where(next_of_e == N_EXPERTS, -1, next_of_e)
    is_e = block_e[:, None] == eids[None, :]
    weight_slot = jnp.sum(jnp.where(is_e, slot_of_e[None, :], 0), axis=1).astype(jnp.int32)
    next_e = jnp.sum(jnp.where(is_e, next_of_e[None, :], 0), axis=1).astype(jnp.int32)
    rows_end = jnp.sum(jnp.where(is_e, (pstarts + counts)[None, :], 0), axis=1)
    block_valid = jnp.clip(rows_end - blk_start, 0, MOE_BLK).astype(jnp.int32)

    n_slots = n_blocks * MOE_BLK
    slot_asg = _slotmap(dest, n_slots)
    yb = _experts(block_e, weight_slot, next_e, block_valid, n_used.reshape(1), slot_asg, mt, exp_w_gate[0],
                  exp_w_up[0], exp_w_down[0], n_blocks)
    out = _combine(dest, yb, x1.reshape(t_all, d), route, mod3, final_norm_g, seq)
    return out.reshape(bn, seq, d)
```

```python
import functools

import jax
import jax.numpy as jnp
from jax import lax
from jax.experimental import pallas as pl
from jax.experimental.pallas import tpu as pltpu

F32 = jnp.float32
BF16 = jnp.bfloat16

D_MODEL = 1024
D_RNN = 512
D_CONV = 512
D_CONV_H = D_CONV // 2
RNN_HEADS = 8
RNN_HEAD_DIM = D_RNN // RNN_HEADS
GRID_W = 64
RG_C = 8.0
N_GROUPS = 4
EXPERTS_PER_GROUP = 8
N_EXPERTS = N_GROUPS * EXPERTS_PER_GROUP
TOP_K = 2
D_EXPERT = 512
NORM_EPS = 1e-6
F32_TINY = 1.1754944e-38

LANES = 128
SUBLANES = 8
ROW_TILES = D_MODEL // LANES
PACK_TILES = ROW_TILES // 2
N_LANE_GROUPS = D_RNN // LANES
EXPERT_LANE0 = N_GROUPS

MOD_ROWS = 16
MOD_TN = 768
INPROJ_TILE = 1024
OUTPROJ_TILE = 512
COEFF_ROWS = 2048
SCAN_UNROLL = 16
MOE_BLK = 512
MXU_TILE = 256
COMBINE_TILE = 256
DMA_UNROLL = 16
VMEM_LIMIT = 48 * 1024 * 1024
EXPERT_VMEM_LIMIT = 58 * 1024 * 1024


def _dot(a, b):
    return jnp.dot(a, b, preferred_element_type=F32)


def _split_bf16(x):
    hi = x.astype(BF16)
    lo = (x - hi.astype(F32)).astype(BF16)
    return hi, lo


def _mod_kernel(cc_ref, w_ref, b_ref, o_ref):
    s = cc_ref[...]
    s = s * jax.nn.sigmoid(s)
    s_hi, s_lo = _split_bf16(s)
    w_hi, w_lo = _split_bf16(w_ref[...])
    o_ref[...] = _dot(s_hi, w_hi) + _dot(s_lo, w_hi) + _dot(s_hi, w_lo) + b_ref[...]


def _modulation(cc, ada_w, ada_b):
    n = ada_w.shape[1]
    return pl.pallas_call(
        _mod_kernel,
        grid=(n // MOD_TN,),
        in_specs=[
            pl.BlockSpec((MOD_ROWS, D_MODEL), lambda j: (0, 0)),
            pl.BlockSpec((D_MODEL, MOD_TN), lambda j: (0, j)),
            pl.BlockSpec((1, MOD_TN), lambda j: (0, j)),
        ],
        out_specs=pl.BlockSpec((MOD_ROWS, MOD_TN), lambda j: (0, j)),
        out_shape=jax.ShapeDtypeStruct((MOD_ROWS, n), F32),
        compiler_params=pltpu.CompilerParams(vmem_limit_bytes=VMEM_LIMIT),
        name="mod",
    )(cc, ada_w, ada_b.reshape(1, n))


def _norm_mod(x, g, scale, shift):
    ms = jnp.mean(x * x, axis=-1, keepdims=True)
    y = x * lax.rsqrt(ms + NORM_EPS) * g
    return y * (1.0 + scale) + shift


def _inproj_kernel(x_ref, mod_ref, g_ref, w32_ref, *refs, latent):
    out_refs, w_ref = refs[:-1], refs[-1]

    @pl.when((pl.program_id(0) == 0) & (pl.program_id(1) == 0))
    def _():
        w_ref[...] = w32_ref[...].astype(BF16)

    h = _norm_mod(x_ref[0], g_ref[...], mod_ref[0, 1:2, :], mod_ref[0, 0:1, :])
    hb = h.astype(BF16)
    xr = _dot(hb, w_ref[:, 0:D_RNN])
    out_refs[0][0] = xr
    if latent:
        o = D_RNN
        out_refs[1][0] = jax.nn.gelu(_dot(hb, w_ref[:, o:o + D_RNN]), approximate=True)
        o += D_RNN
        v = _dot(hb, w_ref[:, o:o + D_CONV])
        out_refs[3][0] = _dot(hb, w_ref[:, o + D_CONV:o + 2 * D_CONV])
        cg = _dot(hb, w_ref[:, o + 2 * D_CONV:o + 3 * D_CONV])
        out_refs[2][0] = cg * v


def _inproj(x, mod3, mod_row, norm_g, w_in, latent):
    bn, n, d = x.shape
    tm = min(INPROJ_TILE, n)
    n_out = 4 if latent else 1
    width = w_in.shape[1] if latent else D_RNN
    mod_map = (lambda b, i: (b, 0, 0)) if mod_row is None else (lambda b, i: (mod_row, 0, 0))
    return pl.pallas_call(
        functools.partial(_inproj_kernel, latent=latent),
        grid=(bn, n // tm),
        in_specs=[
            pl.BlockSpec((1, tm, d), lambda b, i: (b, i, 0)),
            pl.BlockSpec((1, 6, d), mod_map),
            pl.BlockSpec((1, d), lambda b, i: (0, 0)),
            pl.BlockSpec((d, width), lambda b, i: (0, 0), pipeline_mode=pl.Buffered(1)),
        ],
        out_specs=[pl.BlockSpec((1, tm, D_RNN), lambda b, i: (b, i, 0))] * n_out,
        out_shape=[jax.ShapeDtypeStruct((bn, n, D_RNN), F32)] * n_out,
        scratch_shapes=[pltpu.VMEM((d, width), BF16)],
        compiler_params=pltpu.CompilerParams(
            dimension_semantics=("arbitrary", "arbitrary"), vmem_limit_bytes=VMEM_LIMIT),
        name="inproj_lat" if latent else "inproj_ctx",
    )(x, mod3, norm_g.reshape(1, d), w_in)


def _shift_rows(x, k):
    n = x.shape[0]
    row = lax.broadcasted_iota(jnp.int32, x.shape, 0)
    rolled = pltpu.roll(x, k % n, axis=0)
    valid = (row >= k) if k > 0 else (row < n + k)
    return jnp.where(valid, rolled, 0.0)


def _scan_pitch(chunk):
    pitch = chunk + SUBLANES
    return pitch if (pitch // SUBLANES) % 2 else pitch + SUBLANES


def _rnn_kernel(xr_ref, xrc_ref, gr_ref, cw_ref, cb_ref, wg_ref, bg_ref, lam_ref, u_ref, bgc_ref, scw_ref,
                y_ref, yc_ref,
                xc_s, ap_f, bp_f, ap_b, bp_b, hl_f, al_f, hl_b, al_b, hp_f, hp_b, xpad, *, n_lat, n_ctx):
    _gconv_block(u_ref, bgc_ref, scw_ref, yc_ref)

    nl = -lam_ref[...]
    sp = jnp.maximum(nl, 0.0) + jnp.log1p(jnp.exp(-jnp.abs(nl)))
    c1 = (0.5 * RG_C) * sp
    cw = cw_ref[...]
    bias = cb_ref[...]
    wg = wg_ref[0]
    bg = bg_ref[0]
    dirs = ((ap_f, bp_f, hl_f, al_f, hp_f), (ap_b, bp_b, hl_b, al_b, hp_b))

    def conv_into(x, n):
        halo = jnp.zeros((SUBLANES, LANES), F32)
        xpad[pl.ds(0, SUBLANES), :] = halo
        xpad[pl.ds(SUBLANES, n), :] = x
        xpad[pl.ds(SUBLANES + n, SUBLANES), :] = halo
        xc_s[pl.ds(0, n), :] = (cw[0:1] * xpad[pl.ds(SUBLANES - 2, n), :]
                                + cw[1:2] * xpad[pl.ds(SUBLANES - 1, n), :] + cw[2:3] * x
                                + cw[3:4] * xpad[pl.ds(SUBLANES + 1, n), :]) + bias

    def coefficients(n):
        chunk = n // SUBLANES
        pitch = _scan_pitch(chunk)
        rows = max(chunk, min(n, COEFF_ROWS))
        per = rows // chunk

        def body(i, carry):
            xc = xc_s[pl.ds(pl.multiple_of(i * rows, SUBLANES), rows), :]
            gates = _dot(xc.astype(BF16), wg) + bg
            half_xc = 0.5 * xc
            for d in range(2):
                tr = jnp.tanh(gates[:, (2 * d) * LANES:(2 * d + 1) * LANES])
                ti = jnp.tanh(gates[:, (2 * d + 1) * LANES:(2 * d + 2) * LANES])
                neg_log_a = c1[d:d + 1] + c1[d:d + 1] * tr
                a = jnp.exp(-neg_log_a)
                y = jnp.tanh(neg_log_a) * (a * a + 1.0)
                b = (y * lax.rsqrt(jnp.maximum(y, F32_TINY))) * (half_xc + half_xc * ti)
                for k in range(per):
                    dst = pl.multiple_of((i * per + k) * pitch, SUBLANES)
                    dirs[d][0][pl.ds(dst, chunk), :] = a[k * chunk:(k + 1) * chunk]
                    dirs[d][1][pl.ds(dst, chunk), :] = b[k * chunk:(k + 1) * chunk]
            return carry

        if n == rows:
            body(0, 0)
        else:
            lax.fori_loop(0, n // rows, body, 0)

    def scan(n, h0_f, h0_b, keep):
        chunk = n // SUBLANES
        pitch = _scan_pitch(chunk)

        def steps(jo, carry):
            h_f, a_f, h_b, a_b = carry
            for u in range(SCAN_UNROLL):
                j = jo * SCAN_UNROLL + u
                av = ap_f[pl.ds(j, SUBLANES, stride=pitch), :]
                h_f = av * h_f + bp_f[pl.ds(j, SUBLANES, stride=pitch), :]
                a_f = av * a_f
                jb = chunk - 1 - j
                av = ap_b[pl.ds(jb, SUBLANES, stride=pitch), :]
                h_b = av * h_b + bp_b[pl.ds(jb, SUBLANES, stride=pitch), :]
                a_b = av * a_b
                if keep:
                    o = pl.multiple_of(j * SUBLANES, SUBLANES)
                    hl_f[pl.ds(o, SUBLANES), :] = h_f
                    al_f[pl.ds(o, SUBLANES), :] = a_f
                    hl_b[pl.ds(o, SUBLANES), :] = h_b
                    al_b[pl.ds(o, SUBLANES), :] = a_b
            return h_f, a_f, h_b, a_b

        zeros = jnp.zeros((SUBLANES, LANES), F32)
        ones = jnp.ones((SUBLANES, LANES), F32)
        h_f, a_f, h_b, a_b = lax.fori_loop(0, chunk // SCAN_UNROLL, steps, (zeros, ones, zeros, ones))

        in_f = [h0_f]
        for c in range(SUBLANES):
            in_f.append(a_f[c:c + 1] * in_f[c] + h_f[c:c + 1])
        in_b = [h0_b]
        for c in range(SUBLANES - 1, -1, -1):
            in_b.append(a_b[c:c + 1] * in_b[-1] + h_b[c:c + 1])
        if keep:
            hin_f = jnp.concatenate(in_f[:SUBLANES], axis=0)
            hin_b = jnp.concatenate(in_b[SUBLANES - 1::-1], axis=0)

            def fix(jo, carry):
                for u in range(SCAN_UNROLL):
                    j = jo * SCAN_UNROLL + u
                    o = pl.multiple_of(j * SUBLANES, SUBLANES)
                    hp_f[pl.ds(j, SUBLANES, stride=pitch), :] = (
                        hl_f[pl.ds(o, SUBLANES), :] + al_f[pl.ds(o, SUBLANES), :] * hin_f)
                    hp_b[pl.ds(chunk - 1 - j, SUBLANES, stride=pitch), :] = (
                        hl_b[pl.ds(o, SUBLANES), :] + al_b[pl.ds(o, SUBLANES), :] * hin_b)
                return carry
            lax.fori_loop(0, chunk // SCAN_UNROLL, fix, 0)
        return in_f[SUBLANES], in_b[SUBLANES]

    zero = jnp.zeros((1, LANES), F32)
    conv_into(xrc_ref[0], n_ctx)
    coefficients(n_ctx)
    h0_f, h0_b = scan(n_ctx, zero, zero, keep=False)

    conv_into(xr_ref[0], n_lat)
    coefficients(n_lat)
    scan(n_lat, h0_f, h0_b, keep=True)

    chunk = n_lat // SUBLANES
    pitch = _scan_pitch(chunk)

    def emit(c, carry):
        src = pl.multiple_of(c * chunk, 2 * SUBLANES)
        dst = pl.multiple_of(c * pitch, SUBLANES)
        hsum = hp_f[pl.ds(dst, chunk), :] + hp_b[pl.ds(dst, chunk), :]
        y = gr_ref[0, pl.ds(src, chunk), :] * hsum
        y_ref[0, pl.ds(src, chunk), :] = y.astype(y_ref.dtype)
        return carry
    lax.fori_loop(0, SUBLANES, emit, 0)


def _mixers(xr, xr_c, gr, conv_w, conv_b, wg, bgate, lam, u, bg, sc_w):
    bn, n, _ = xr.shape
    n_ctx = xr_c.shape[1]
    assert n % (SUBLANES * SCAN_UNROLL) == 0 and n_ctx % (SUBLANES * SCAN_UNROLL) == 0 and n_ctx <= n
    pitched = SUBLANES * _scan_pitch(n // SUBLANES)
    seq_spec = pl.BlockSpec((1, n, LANES), lambda b, p: (b, 0, p))
    return pl.pallas_call(
        functools.partial(_rnn_kernel, n_lat=n, n_ctx=n_ctx),
        grid=(bn, N_LANE_GROUPS),
        in_specs=[
            seq_spec,
            pl.BlockSpec((1, n_ctx, LANES), lambda b, p: (b, 0, p)),
            seq_spec,
            pl.BlockSpec((4, LANES), lambda b, p: (0, p)),
            pl.BlockSpec((1, LANES), lambda b, p: (0, p)),
            pl.BlockSpec((1, LANES, 4 * LANES), lambda b, p: (p, 0, 0)),
            pl.BlockSpec((1, 1, 4 * LANES), lambda b, p: (p, 0, 0)),
            pl.BlockSpec((2, LANES), lambda b, p: (0, p)),
            seq_spec,
            seq_spec,
            pl.BlockSpec((3, LANES), lambda b, p: (0, p)),
        ],
        out_specs=[seq_spec, seq_spec],
        out_shape=[jax.ShapeDtypeStruct((bn, n, D_RNN), BF16)] * 2,
        scratch_shapes=[pltpu.VMEM((n, LANES), F32)]
        + [pltpu.VMEM((pitched, LANES), F32)] * 4
        + [pltpu.VMEM((n, LANES), F32)] * 4
        + [pltpu.VMEM((pitched, LANES), F32)] * 2
        + [pltpu.VMEM((n + 2 * SUBLANES, LANES), F32)],
        compiler_params=pltpu.CompilerParams(vmem_limit_bytes=VMEM_LIMIT),
        name="mixers",
    )(xr, xr_c, gr, conv_w, conv_b.reshape(1, D_RNN), wg, bgate, lam, u, bg, sc_w)


def _gate_weights(rg_wa, rg_ba, rg_wx, rg_bx):
    eye = jnp.eye(2, dtype=F32)
    blocks, biases = [], []
    for d in range(2):
        for w, bvec in ((rg_wa[d], rg_ba[d]), (rg_wx[d], rg_bx[d])):
            w4 = w.reshape(N_LANE_GROUPS, 2, RNN_HEAD_DIM, RNN_HEAD_DIM)
            bd = jnp.einsum("paij,ac->paicj", w4, eye).reshape(N_LANE_GROUPS, LANES, LANES)
            blocks.append(0.5 * bd)
            biases.append(0.5 * bvec.reshape(N_LANE_GROUPS, 1, LANES))
    return jnp.concatenate(blocks, axis=-1).astype(BF16), jnp.concatenate(biases, axis=-1)


def _gconv_block(u_ref, bg_ref, w_ref, y_ref):
    p = pl.program_id(1)
    u = u_ref[0]
    w = w_ref[...]

    @pl.when(p < D_CONV_H // LANES)
    def _():
        col = lax.broadcasted_iota(jnp.int32, u.shape, 0) % GRID_W
        left = jnp.where(col > 0, _shift_rows(u, 1), 0.0)
        right = jnp.where(col < GRID_W - 1, _shift_rows(u, -1), 0.0)
        y_ref[0] = (bg_ref[0] * (w[0:1] * left + w[1:2] * u + w[2:3] * right)).astype(y_ref.dtype)

    @pl.when(p >= D_CONV_H // LANES)
    def _():
        y_ref[0] = (bg_ref[0] * (w[0:1] * _shift_rows(u, GRID_W) + w[1:2] * u
                                 + w[2:3] * _shift_rows(u, -GRID_W))).astype(y_ref.dtype)


def _lane_max(x, mask):
    return jnp.max(jnp.where(mask, x, -jnp.inf), axis=-1, keepdims=True)


def _first_lane(cond, lane):
    return jnp.min(jnp.where(cond, lane, float(LANES)), axis=-1, keepdims=True)


def _outproj_kernel(x_ref, yr_ref, yc_ref, w32_ref, mod_ref, g_ref, wr_ref, br_ref,
                    x1_ref, mt_ref, route_ref, cnt_ref, rt_ref, carry, w_ref, m_s, *, tm):
    s = pl.program_id(0)

    @pl.when(s == 0)
    def _():
        carry[...] = jnp.zeros_like(carry)
        w_ref[...] = w32_ref[...].astype(BF16)
        m_s[...] = jnp.zeros_like(m_s)

    logits = _dot(m_s[...], wr_ref[...]) + br_ref[...]
    lane_i = lax.broadcasted_iota(jnp.int32, logits.shape, 1)
    lane = lane_i.astype(F32)
    is_grp = lane_i < N_GROUPS
    g_max = _lane_max(logits, is_grp)
    grp = _first_lane(is_grp & (logits == g_max), lane)
    p_g = 1.0 / jnp.sum(jnp.where(is_grp, jnp.exp(logits - g_max), 0.0), axis=-1, keepdims=True)
    lo_lane = EXPERT_LANE0 + grp * EXPERTS_PER_GROUP
    in_grp = (lane >= lo_lane) & (lane < lo_lane + EXPERTS_PER_GROUP)
    l1 = _lane_max(logits, in_grp)
    i1 = _first_lane(in_grp & (logits == l1), lane)
    rest = in_grp & (lane != i1)
    l2 = _lane_max(logits, rest)
    i2 = _first_lane(rest & (logits == l2), lane)
    r21 = jnp.exp(l2 - l1)
    gate1 = p_g / (1.0 + r21)
    gate2 = gate1 * r21

    oh1 = jnp.where(lane == i1, 1.0, 0.0)
    oh2 = jnp.where(lane == i2, 1.0, 0.0)
    both = (oh1 + oh2).astype(BF16)
    ti = lax.broadcasted_iota(jnp.int32, (tm, tm), 0)
    tj = lax.broadcasted_iota(jnp.int32, (tm, tm), 1)
    tri = jnp.where(tj < ti, 1.0, 0.0).astype(BF16)
    counts = carry[...]
    before = _dot(tri, both) + counts
    rank1 = jnp.sum(oh1 * before, axis=-1, keepdims=True)
    rank2 = jnp.sum(oh2 * before, axis=-1, keepdims=True)
    out = jnp.zeros(logits.shape, F32)
    for k, val in enumerate((i1 - EXPERT_LANE0, i2 - EXPERT_LANE0, gate1, gate2, rank1, rank2)):
        out = jnp.where(lane_i == k, val, out)
    route_ref[...] = out
    rt_ref[...] = out.T[0:SUBLANES, :]
    real = jnp.where(s > 0, 1.0, 0.0)
    total = counts + real * jnp.sum(oh1 + oh2, axis=0, keepdims=True)
    carry[...] = total
    cnt_ref[...] = total

    mix = _dot(yr_ref[0], w_ref[0:D_RNN, :]) + _dot(yc_ref[0], w_ref[D_RNN:, :])
    x1 = x_ref[0] + mod_ref[0, 2:3, :] * mix
    x1_ref[0] = x1
    m_new = _norm_mod(x1, g_ref[...], mod_ref[0, 4:5, :], mod_ref[0, 3:4, :])
    m_s[...] = m_new.astype(BF16)
    half = D_MODEL // 2
    packed = pltpu.pack_elementwise([m_new[:, :half], m_new[:, half:]], packed_dtype=BF16)
    for q in range(PACK_TILES):
        mt_ref[pl.ds(q, tm, stride=PACK_TILES), :] = packed[:, q * LANES:(q + 1) * LANES]


def _outproj(x, y_rnn, y_conv, w_out, mod3, norm_g, wr, br):
    bn, n, d = x.shape
    tm = min(OUTPROJ_TILE, n)
    nt = n // tm
    n_tiles = bn * nt
    t_all = bn * n

    def cur(s):
        return jnp.minimum(s, n_tiles - 1)

    def prev(s):
        return jnp.maximum(s - 1, 0)

    def seq_map(s):
        return (cur(s) // nt, cur(s) % nt, 0)

    const = lambda s: (0, 0)
    return pl.pallas_call(
        functools.partial(_outproj_kernel, tm=tm),
        grid=(n_tiles + 1,),
        in_specs=[
            pl.BlockSpec((1, tm, d), seq_map),
            pl.BlockSpec((1, tm, D_RNN), seq_map),
            pl.BlockSpec((1, tm, D_CONV), seq_map),
            pl.BlockSpec((D_RNN + D_CONV, d), const, pipeline_mode=pl.Buffered(1)),
            pl.BlockSpec((1, 6, d), lambda s: (cur(s) // nt, 0, 0)),
            pl.BlockSpec((1, d), const),
            pl.BlockSpec((d, LANES), const),
            pl.BlockSpec((1, LANES), const),
        ],
        out_specs=[
            pl.BlockSpec((1, tm, d), seq_map),
            pl.BlockSpec((tm * PACK_TILES, LANES), lambda s: (cur(s), 0)),
            pl.BlockSpec((tm, LANES), lambda s: (prev(s), 0)),
            pl.BlockSpec((1, LANES), const),
            pl.BlockSpec((SUBLANES, tm), lambda s: (0, prev(s))),
        ],
        out_shape=[
            jax.ShapeDtypeStruct((bn, n, d), F32),
            jax.ShapeDtypeStruct((t_all * PACK_TILES, LANES), jnp.uint32),
            jax.ShapeDtypeStruct((t_all, LANES), F32),
            jax.ShapeDtypeStruct((1, LANES), F32),
            jax.ShapeDtypeStruct((SUBLANES, t_all), F32),
        ],
        scratch_shapes=[pltpu.VMEM((1, LANES), F32), pltpu.VMEM((D_RNN + D_CONV, d), BF16),
                        pltpu.VMEM((tm, d), BF16)],
        compiler_params=pltpu.CompilerParams(
            dimension_semantics=("arbitrary",), vmem_limit_bytes=VMEM_LIMIT),
        name="outproj",
    )(x, y_rnn, y_conv, w_out, mod3, norm_g.reshape(1, d), wr, br)


def _row_tile(ref, row):
    return ref.at[pl.ds(pl.multiple_of(row * ROW_TILES, ROW_TILES), ROW_TILES)]


def _slotmap_kernel(dest_ref, zeros_hbm, asg_ref, sem):
    fill = pltpu.make_async_copy(zeros_hbm, asg_ref, sem)
    fill.start()
    fill.wait()

    def body(c, carry):
        for u in range(DMA_UNROLL):
            a = c * DMA_UNROLL + u
            asg_ref[dest_ref[a]] = a
        return carry
    lax.fori_loop(0, dest_ref.shape[0] // DMA_UNROLL, body, 0)


def _slotmap(dest, n_slots):
    return pl.pallas_call(
        _slotmap_kernel,
        in_specs=[pl.BlockSpec(memory_space=pltpu.SMEM), pl.BlockSpec(memory_space=pl.ANY)],
        out_specs=pl.BlockSpec(memory_space=pltpu.SMEM),
        out_shape=jax.ShapeDtypeStruct((n_slots,), jnp.int32),
        scratch_shapes=[pltpu.SemaphoreType.DMA],
        name="slotmap",
    )(dest, jnp.zeros((n_slots,), jnp.int32))


def _expert_kernel(be_ref, ws_ref, ne_ref, bv_ref, nu_ref, asg_hbm, m_ref, wg_hbm, wu_hbm, wd_hbm, yb_ref,
                   xbuf_a, xbuf_b, idx, isems, wbuf_g, wbuf_u, wbuf_d, wsems, wg_s, wu_s, wd_s):
    j = pl.program_id(0)
    n_used = nu_ref[0]
    last = n_used - 1
    n_tok = m_ref.shape[0] // PACK_TILES

    def idx_copy(blk, sl):
        return pltpu.make_async_copy(asg_hbm.at[blk], idx.at[sl], isems.at[sl])

    def copy_rows(buf, sl, r0, n):
        for u in range(n):
            asg = idx[sl, 0, r0 + u]
            tok = jnp.where(asg >= n_tok, asg - n_tok, asg)
            src = pl.multiple_of(tok * PACK_TILES, PACK_TILES)
            buf[pl.ds((r0 + u) * PACK_TILES, PACK_TILES), :] = m_ref[pl.ds(src, PACK_TILES), :]

    def unpack(buf, rows):
        halves = ([], [])
        for s in range(PACK_TILES):
            word = buf[pl.ds(s, rows, stride=PACK_TILES), :]
            for k in range(2):
                part = pltpu.unpack_elementwise(word, index=k, packed_dtype=BF16, unpacked_dtype=F32)
                halves[k].append(part.astype(BF16))
        return jnp.concatenate(halves[0] + halves[1], axis=-1)

    n_pieces = 2 * D_EXPERT // MXU_TILE + D_MODEL // MXU_TILE
    bounds = [(p * MOE_BLK) // n_pieces for p in range(n_pieces + 1)]

    def compute(cur, nxt, nxt_sl, rows):
        pieces = iter(zip(bounds[:-1], bounds[1:]))

        def dot_pieces(a, w_ref, n0):
            acc = _dot(a, w_ref[:, n0:n0 + MXU_TILE])
            r0, r1 = next(pieces)
            copy_rows(nxt, nxt_sl, r0, r1 - r0)
            return acc

        xb16 = unpack(cur, rows)
        acts = []
        for n0 in range(0, D_EXPERT, MXU_TILE):
            gate = dot_pieces(xb16, wg_s, n0)
            up = dot_pieces(xb16, wu_s, n0)
            acts.append(((gate * jax.nn.sigmoid(gate)) * up).astype(BF16))
        h = jnp.concatenate(acts, axis=-1)
        for n0 in range(0, D_MODEL, MXU_TILE):
            y = dot_pieces(h, wd_s, n0)
            for s in range(MXU_TILE // LANES):
                yb_ref[pl.ds(n0 // LANES + s, rows, stride=ROW_TILES), :] = (
                    y[:, s * LANES:(s + 1) * LANES])
        if rows < MOE_BLK:
            yb_ref[pl.ds(rows * ROW_TILES, (MOE_BLK - rows) * ROW_TILES), :] = jnp.zeros(
                ((MOE_BLK - rows) * ROW_TILES, LANES), F32)

    @pl.when(j >= n_used)
    def _():
        yb_ref[...] = jnp.zeros_like(yb_ref)

    @pl.when(j < n_used)
    def _():
        slot = j % 2
        other = 1 - slot

        @pl.when(j == 0)
        def _():
            idx_copy(0, 0).start()
            idx_copy(0, 0).wait()

            def body(c, carry):
                copy_rows(xbuf_a, 0, c * DMA_UNROLL, DMA_UNROLL)
                return carry
            lax.fori_loop(0, MOE_BLK // DMA_UNROLL, body, 0)
            idx_copy(jnp.minimum(1, last), 1).start()

        e = be_ref[j]
        wslot = ws_ref[j]

        def weight_copies(expert, sl):
            return [pltpu.make_async_copy(src.at[expert], dst.at[sl], wsems.at[sl])
                    for src, dst in ((wg_hbm, wbuf_g), (wu_hbm, wbuf_u), (wd_hbm, wbuf_d))]

        @pl.when(j == 0)
        def _():
            for cp in weight_copies(e, wslot):
                cp.start()

        @pl.when((j == 0) | (e != be_ref[jnp.maximum(j - 1, 0)]))
        def _():
            for cp in weight_copies(e, wslot):
                cp.wait()
            wg_s[...] = wbuf_g[wslot].astype(BF16)
            wu_s[...] = wbuf_u[wslot].astype(BF16)
            wd_s[...] = wbuf_d[wslot].astype(BF16)

            @pl.when(ne_ref[j] >= 0)
            def _():
                for cp in weight_copies(ne_ref[j], 1 - wslot):
                    cp.start()

        idx_copy(0, other).wait()

        short = bv_ref[j] <= MOE_BLK // 2
        for par, (cur, nxt) in enumerate(((xbuf_a, xbuf_b), (xbuf_b, xbuf_a))):
            @pl.when((slot == par) & jnp.logical_not(short))
            def _(cur=cur, nxt=nxt, par=par):
                compute(cur, nxt, 1 - par, MOE_BLK)

            @pl.when((slot == par) & short)
            def _(cur=cur, nxt=nxt, par=par):
                compute(cur, nxt, 1 - par, MOE_BLK // 2)

        @pl.when(j < last)
        def _():
            idx_copy(jnp.minimum(j + 2, last), slot).start()


def _experts(block_e, weight_slot, next_e, block_valid, n_used, slot_asg, mt, w_gate, w_up, w_down, n_blocks):
    return pl.pallas_call(
        _expert_kernel,
        grid_spec=pltpu.PrefetchScalarGridSpec(
            num_scalar_prefetch=5,
            grid=(n_blocks,),
            in_specs=[
                pl.BlockSpec(memory_space=pl.ANY),
                pl.BlockSpec(memory_space=pltpu.VMEM),
                pl.BlockSpec(memory_space=pl.ANY),
                pl.BlockSpec(memory_space=pl.ANY),
                pl.BlockSpec(memory_space=pl.ANY),
            ],
            out_specs=pl.BlockSpec((MOE_BLK * ROW_TILES, LANES), lambda j, *_: (j, 0)),
            scratch_shapes=[
                pltpu.VMEM((MOE_BLK * PACK_TILES, LANES), jnp.uint32),
                pltpu.VMEM((MOE_BLK * PACK_TILES, LANES), jnp.uint32),
                pltpu.SMEM((2, 1, MOE_BLK), jnp.int32),
                pltpu.SemaphoreType.DMA((2,)),
                pltpu.VMEM((2, D_MODEL, D_EXPERT), F32),
                pltpu.VMEM((2, D_MODEL, D_EXPERT), F32),
                pltpu.VMEM((2, D_EXPERT, D_MODEL), F32),
                pltpu.SemaphoreType.DMA((2,)),
                pltpu.VMEM((D_MODEL, D_EXPERT), BF16),
                pltpu.VMEM((D_MODEL, D_EXPERT), BF16),
                pltpu.VMEM((D_EXPERT, D_MODEL), BF16),
            ],
        ),
        out_shape=jax.ShapeDtypeStruct((n_blocks * MOE_BLK * ROW_TILES, LANES), F32),
        compiler_params=pltpu.CompilerParams(
            dimension_semantics=("arbitrary",), vmem_limit_bytes=EXPERT_VMEM_LIMIT),
        name="expert",
    )(block_e, weight_slot, next_e, block_valid, n_used, slot_asg.reshape(n_blocks, 1, MOE_BLK), mt, w_gate,
      w_up, w_down)


def _combine_kernel(dest_ref, yb_hbm, x1_ref, route_ref, mod_ref, g_ref, o_ref, ybuf, sems, *, tc, n_tok):
    i = pl.program_id(0)
    slot = i % 2

    def row_copy(d, sl, k, r):
        return pltpu.make_async_copy(_row_tile(yb_hbm, d), _row_tile(ybuf.at[sl, k], r), sems.at[sl])

    def gather(step, sl):
        def issue(c, carry):
            for u in range(DMA_UNROLL):
                r = c * DMA_UNROLL + u
                for k in range(TOP_K):
                    row_copy(dest_ref[k * n_tok + step * tc + r], sl, k, r).start(priority=k)
            return carry
        lax.fori_loop(0, tc // DMA_UNROLL, issue, 0)

    @pl.when(i == 0)
    def _():
        gather(0, 0)

    @pl.when(i + 1 < pl.num_programs(0))
    def _():
        gather(i + 1, 1 - slot)

    def drain(c, carry):
        for u in range(DMA_UNROLL * TOP_K):
            row_copy(0, slot, 0, 0).wait()
        return carry
    lax.fori_loop(0, tc // DMA_UNROLL, drain, 0)

    def rows(k):
        return jnp.concatenate(
            [ybuf[slot, k, pl.ds(s, tc, stride=ROW_TILES), :] for s in range(ROW_TILES)], axis=-1)

    route = route_ref[...]
    y = route[:, 2:3] * rows(0) + route[:, 3:4] * rows(1)
    x2 = x1_ref[...] + mod_ref[0, 5:6, :] * y
    ms = jnp.mean(x2 * x2, axis=-1, keepdims=True)
    o_ref[...] = x2 * lax.rsqrt(ms + NORM_EPS) * g_ref[...]


def _combine(dest, yb, x1_2d, route, mod3, final_g, seq):
    t_all, d = x1_2d.shape
    tc = min(COMBINE_TILE, seq)
    per_seq = seq // tc
    return pl.pallas_call(
        functools.partial(_combine_kernel, tc=tc, n_tok=t_all),
        grid_spec=pltpu.PrefetchScalarGridSpec(
            num_scalar_prefetch=1,
            grid=(t_all // tc,),
            in_specs=[
                pl.BlockSpec(memory_space=pl.ANY),
                pl.BlockSpec((tc, d), lambda i, dest: (i, 0)),
                pl.BlockSpec((tc, LANES), lambda i, dest: (i, 0)),
                pl.BlockSpec((1, 6, d), lambda i, dest: (i // per_seq, 0, 0)),
                pl.BlockSpec((1, d), lambda i, dest: (0, 0)),
            ],
            out_specs=pl.BlockSpec((tc, d), lambda i, dest: (i, 0)),
            scratch_shapes=[
                pltpu.VMEM((2, TOP_K, tc * ROW_TILES, LANES), F32),
                pltpu.SemaphoreType.DMA((2,)),
            ],
        ),
        out_shape=jax.ShapeDtypeStruct((t_all, d), F32),
        compiler_params=pltpu.CompilerParams(
            dimension_semantics=("arbitrary",), vmem_limit_bytes=VMEM_LIMIT),
        name="combine",
    )(dest, yb, x1_2d, route, mod3, final_g.reshape(1, d))


def kernel(x, c, ctx, c_ctx, ada_w, ada_b, norm1_g, norm2_g, w_in, rnn_conv_w, rnn_conv_b, rg_wa, rg_ba,
           rg_wx, rg_bx, rg_lambda, sc_conv_w, w_out, router_group_w, router_group_b, router_exp_w,
           router_exp_b, exp_w_gate, exp_w_up, exp_w_down, final_norm_g):
    bn, seq, d = x.shape
    assert d == D_MODEL and bn < MOD_ROWS and ada_w.shape[0] == 1
    t_all = bn * seq

    cc = jnp.concatenate([c, c_ctx[None], jnp.zeros((MOD_ROWS - bn - 1, d), F32)], axis=0)
    mod3 = _modulation(cc, ada_w[0], ada_b[0]).reshape(MOD_ROWS, 6, d)

    xr, gr, u, bg = _inproj(x, mod3, None, norm1_g[0], w_in[0], latent=True)
    (xr_c,) = _inproj(ctx, mod3, bn, norm1_g[0], w_in[0], latent=False)

    wg, bgate = _gate_weights(rg_wa[0], rg_ba[0], rg_wx[0], rg_bx[0])
    assert D_RNN == D_CONV
    y_rnn, y_conv = _mixers(xr, xr_c, gr, rnn_conv_w[0], rnn_conv_b[0], wg, bgate, rg_lambda[0], u, bg,
                            sc_conv_w[0])

    wr = jnp.zeros((d, LANES), F32)
    wr = wr.at[:, :N_GROUPS].set(router_group_w[0]).at[:, EXPERT_LANE0:EXPERT_LANE0 + N_EXPERTS].set(router_exp_w[0])
    br = jnp.zeros((1, LANES), F32)
    br = br.at[0, :N_GROUPS].set(router_group_b[0]).at[0, EXPERT_LANE0:EXPERT_LANE0 + N_EXPERTS].set(router_exp_b[0])
    x1, mt, route, cnt, route_t = _outproj(x, y_rnn, y_conv, w_out[0], mod3, norm2_g[0], wr.astype(BF16), br)

    n_assign = t_all * TOP_K
    n_blocks = (n_assign + N_EXPERTS * (MOE_BLK - 1) + MOE_BLK - 1) // MOE_BLK
    counts = cnt[0, EXPERT_LANE0:EXPERT_LANE0 + N_EXPERTS].astype(jnp.int32)
    pcounts = (counts + MOE_BLK - 1) // MOE_BLK * MOE_BLK
    pends = jnp.cumsum(pcounts)
    pstarts = pends - pcounts
    experts = route_t[0:TOP_K].astype(jnp.int32)
    ranks = route_t[4:4 + TOP_K].astype(jnp.int32)
    onehot = experts[None] == jnp.arange(N_EXPERTS, dtype=jnp.int32)[:, None, None]
    dest = (ranks + jnp.sum(jnp.where(onehot, pstarts[:, None, None], 0), axis=0)).reshape(n_assign)
    n_used = (pends[-1] // MOE_BLK).astype(jnp.int32)
    blk_start = jnp.arange(n_blocks, dtype=jnp.int32) * MOE_BLK
    block_e = jnp.minimum(jnp.sum(blk_start[:, None] >= pends[None, :], axis=1), N_EXPERTS - 1)
    last_e = jnp.max(jnp.where(counts > 0, jnp.arange(N_EXPERTS, dtype=jnp.int32), 0))
    block_e = jnp.where(blk_start < pends[-1], block_e, last_e).astype(jnp.int32)
    eids = jnp.arange(N_EXPERTS, dtype=jnp.int32)
    used = counts > 0
    slot_of_e = (jnp.cumsum(used.astype(jnp.int32)) - 1) % 2
    later = jnp.where(used[None, :] & (eids[None, :] > eids[:, None]), eids[None, :], N_EXPERTS)
    next_of_e = jnp.min(later, axis=1)
    next_of_e = jnp.where(next_of_e == N_EXPERTS, -1, next_of_e)
    is_e = block_e[:, None] == eids[None, :]
    weight_slot = jnp.sum(jnp.where(is_e, slot_of_e[None, :], 0), axis=1).astype(jnp.int32)
    next_e = jnp.sum(jnp.where(is_e, next_of_e[None, :], 0), axis=1).astype(jnp.int32)
    rows_end = jnp.sum(jnp.where(is_e, (pstarts + counts)[None, :], 0), axis=1)
    block_valid = jnp.clip(rows_end - blk_start, 0, MOE_BLK).astype(jnp.int32)

    n_slots = n_blocks * MOE_BLK
    slot_asg = _slotmap(dest, n_slots)
    yb = _experts(block_e, weight_slot, next_e, block_valid, n_used.reshape(1), slot_asg, mt, exp_w_gate[0],
                  exp_w_up[0], exp_w_down[0], n_blocks)
    out = _combine(dest, yb, x1.reshape(t_all, d), route, mod3, final_norm_g, seq)
    return out.reshape(bn, seq, d)
```

```python
import functools

import jax
import jax.numpy as jnp
from jax import lax
from jax.experimental import pallas as pl
from jax.experimental.pallas import tpu as pltpu

F32 = jnp.float32
BF16 = jnp.bfloat16

D_MODEL = 1024
D_RNN = 512
D_CONV = 512
D_CONV_H = D_CONV // 2
RNN_HEADS = 8
RNN_HEAD_DIM = D_RNN // RNN_HEADS
GRID_W = 64
RG_C = 8.0
N_GROUPS = 4
EXPERTS_PER_GROUP = 8
N_EXPERTS = N_GROUPS * EXPERTS_PER_GROUP
TOP_K = 2
D_EXPERT = 512
NORM_EPS = 1e-6
F32_TINY = 1.1754944e-38

LANES = 128
SUBLANES = 8
ROW_TILES = D_MODEL // LANES
PACK_TILES = ROW_TILES // 2
N_LANE_GROUPS = D_RNN // LANES
EXPERT_LANE0 = N_GROUPS

MOD_ROWS = 16
MOD_TN = 1536
INPROJ_TILE = 1024
OUTPROJ_TILE = 512
COEFF_ROWS = 2048
SCAN_UNROLL = 16
MOE_BLK = 512
MXU_TILE = 256
COMBINE_TILE = 256
DMA_UNROLL = 16
VMEM_LIMIT = 48 * 1024 * 1024
EXPERT_VMEM_LIMIT = 58 * 1024 * 1024


def _dot(a, b):
    return jnp.dot(a, b, preferred_element_type=F32)


def _split_bf16(x):
    hi = x.astype(BF16)
    lo = (x - hi.astype(F32)).astype(BF16)
    return hi, lo


def _mod_kernel(cc_ref, w_ref, b_ref, o_ref):
    s = cc_ref[...]
    s = s * jax.nn.sigmoid(s)
    s_hi, s_lo = _split_bf16(s)
    w_hi, w_lo = _split_bf16(w_ref[...])
    o_ref[...] = _dot(s_hi, w_hi) + _dot(s_lo, w_hi) + _dot(s_hi, w_lo) + b_ref[...]


def _modulation(cc, ada_w, ada_b):
    n = ada_w.shape[1]
    return pl.pallas_call(
        _mod_kernel,
        grid=(n // MOD_TN,),
        in_specs=[
            pl.BlockSpec((MOD_ROWS, D_MODEL), lambda j: (0, 0)),
            pl.BlockSpec((D_MODEL, MOD_TN), lambda j: (0, j)),
            pl.BlockSpec((1, MOD_TN), lambda j: (0, j)),
        ],
        out_specs=pl.BlockSpec((MOD_ROWS, MOD_TN), lambda j: (0, j)),
        out_shape=jax.ShapeDtypeStruct((MOD_ROWS, n), F32),
        compiler_params=pltpu.CompilerParams(vmem_limit_bytes=VMEM_LIMIT),
        name="mod",
    )(cc, ada_w, ada_b.reshape(1, n))


def _norm_mod(x, g, scale, shift):
    ms = jnp.mean(x * x, axis=-1, keepdims=True)
    y = x * lax.rsqrt(ms + NORM_EPS) * g
    return y * (1.0 + scale) + shift


def _inproj_kernel(x_ref, mod_ref, g_ref, w32_ref, *refs, latent):
    out_refs, w_ref = refs[:-1], refs[-1]

    @pl.when((pl.program_id(0) == 0) & (pl.program_id(1) == 0))
    def _():
        w_ref[...] = w32_ref[...].astype(BF16)

    h = _norm_mod(x_ref[0], g_ref[...], mod_ref[0, 1:2, :], mod_ref[0, 0:1, :])
    hb = h.astype(BF16)
    xr = _dot(hb, w_ref[:, 0:D_RNN])
    out_refs[0][0] = xr
    if latent:
        o = D_RNN
        out_refs[1][0] = jax.nn.gelu(_dot(hb, w_ref[:, o:o + D_RNN]), approximate=True)
        o += D_RNN
        v = _dot(hb, w_ref[:, o:o + D_CONV])
        out_refs[3][0] = _dot(hb, w_ref[:, o + D_CONV:o + 2 * D_CONV])
        cg = _dot(hb, w_ref[:, o + 2 * D_CONV:o + 3 * D_CONV])
        out_refs[2][0] = cg * v


def _inproj(x, mod3, mod_row, norm_g, w_in, latent):
    bn, n, d = x.shape
    tm = min(INPROJ_TILE, n)
    assert n % tm == 0
    n_out = 4 if latent else 1
    width = w_in.shape[1] if latent else D_RNN
    mod_map = (lambda b, i: (b, 0, 0)) if mod_row is None else (lambda b, i: (mod_row, 0, 0))
    return pl.pallas_call(
        functools.partial(_inproj_kernel, latent=latent),
        grid=(bn, n // tm),
        in_specs=[
            pl.BlockSpec((1, tm, d), lambda b, i: (b, i, 0)),
            pl.BlockSpec((1, 6, d), mod_map),
            pl.BlockSpec((1, d), lambda b, i: (0, 0)),
            pl.BlockSpec((d, width), lambda b, i: (0, 0), pipeline_mode=pl.Buffered(1)),
        ],
        out_specs=[pl.BlockSpec((1, tm, D_RNN), lambda b, i: (b, i, 0))] * n_out,
        out_shape=[jax.ShapeDtypeStruct((bn, n, D_RNN), F32)] * n_out,
        scratch_shapes=[pltpu.VMEM((d, width), BF16)],
        compiler_params=pltpu.CompilerParams(
            dimension_semantics=("arbitrary", "arbitrary"), vmem_limit_bytes=VMEM_LIMIT),
        name="inproj_lat" if latent else "inproj_ctx",
    )(x, mod3, norm_g.reshape(1, d), w_in)


def _shift_rows(x, k):
    n = x.shape[0]
    row = lax.broadcasted_iota(jnp.int32, x.shape, 0)
    rolled = pltpu.roll(x, k % n, axis=0)
    valid = (row >= k) if k > 0 else (row < n + k)
    return jnp.where(valid, rolled, 0.0)


def _scan_pitch(chunk):
    pitch = chunk + SUBLANES
    return pitch if (pitch // SUBLANES) % 2 else pitch + SUBLANES


def _rnn_kernel(xr_ref, xrc_ref, gr_ref, cw_ref, cb_ref, wg_ref, bg_ref, lam_ref, u_ref, bgc_ref, scw_ref,
                y_ref, yc_ref,
                xc_s, ap_f, bp_f, ap_b, bp_b, hl_f, al_f, hl_b, al_b, hp_f, hp_b, xpad, *, n_lat, n_ctx):
    _gconv_block(u_ref, bgc_ref, scw_ref, yc_ref)

    nl = -lam_ref[...]
    sp = jnp.maximum(nl, 0.0) + jnp.log1p(jnp.exp(-jnp.abs(nl)))
    c1 = (0.5 * RG_C) * sp
    cw = cw_ref[...]
    bias = cb_ref[...]
    wg = wg_ref[0]
    bg = bg_ref[0]
    dirs = ((ap_f, bp_f, hl_f, al_f, hp_f), (ap_b, bp_b, hl_b, al_b, hp_b))

    def conv_into(x, n):
        halo = jnp.zeros((SUBLANES, LANES), F32)
        xpad[pl.ds(0, SUBLANES), :] = halo
        xpad[pl.ds(SUBLANES, n), :] = x
        xpad[pl.ds(SUBLANES + n, SUBLANES), :] = halo
        xc_s[pl.ds(0, n), :] = (cw[0:1] * xpad[pl.ds(SUBLANES - 2, n), :]
                                + cw[1:2] * xpad[pl.ds(SUBLANES - 1, n), :] + cw[2:3] * x
                                + cw[3:4] * xpad[pl.ds(SUBLANES + 1, n), :]) + bias

    def coefficients(n):
        chunk = n // SUBLANES
        pitch = _scan_pitch(chunk)
        rows = max(chunk, min(n, COEFF_ROWS))
        per = rows // chunk

        def body(i, carry):
            xc = xc_s[pl.ds(pl.multiple_of(i * rows, SUBLANES), rows), :]
            gates = _dot(xc.astype(BF16), wg) + bg
            half_xc = 0.5 * xc
            for d in range(2):
                tr = jnp.tanh(gates[:, (2 * d) * LANES:(2 * d + 1) * LANES])
                ti = jnp.tanh(gates[:, (2 * d + 1) * LANES:(2 * d + 2) * LANES])
                neg_log_a = c1[d:d + 1] + c1[d:d + 1] * tr
                a = jnp.exp(-neg_log_a)
                y = jnp.tanh(neg_log_a) * (a * a + 1.0)
                b = (y * lax.rsqrt(jnp.maximum(y, F32_TINY))) * (half_xc + half_xc * ti)
                for k in range(per):
                    dst = pl.multiple_of((i * per + k) * pitch, SUBLANES)
                    dirs[d][0][pl.ds(dst, chunk), :] = a[k * chunk:(k + 1) * chunk]
                    dirs[d][1][pl.ds(dst, chunk), :] = b[k * chunk:(k + 1) * chunk]
            return carry

        if n == rows:
            body(0, 0)
        else:
            lax.fori_loop(0, n // rows, body, 0)

    def scan(n, h0_f, h0_b, keep):
        chunk = n // SUBLANES
        pitch = _scan_pitch(chunk)

        def steps(jo, carry):
            h_f, a_f, h_b, a_b = carry
            for u in range(SCAN_UNROLL):
                j = jo * SCAN_UNROLL + u
                av = ap_f[pl.ds(j, SUBLANES, stride=pitch), :]
                h_f = av * h_f + bp_f[pl.ds(j, SUBLANES, stride=pitch), :]
                a_f = av * a_f
                jb = chunk - 1 - j
                av = ap_b[pl.ds(jb, SUBLANES, stride=pitch), :]
                h_b = av * h_b + bp_b[pl.ds(jb, SUBLANES, stride=pitch), :]
                a_b = av * a_b
                if keep:
                    o = pl.multiple_of(j * SUBLANES, SUBLANES)
                    hl_f[pl.ds(o, SUBLANES), :] = h_f
                    al_f[pl.ds(o, SUBLANES), :] = a_f
                    hl_b[pl.ds(o, SUBLANES), :] = h_b
                    al_b[pl.ds(o, SUBLANES), :] = a_b
            return h_f, a_f, h_b, a_b

        zeros = jnp.zeros((SUBLANES, LANES), F32)
        ones = jnp.ones((SUBLANES, LANES), F32)
        h_f, a_f, h_b, a_b = lax.fori_loop(0, chunk // SCAN_UNROLL, steps, (zeros, ones, zeros, ones))

        in_f = [h0_f]
        for c in range(SUBLANES):
            in_f.append(a_f[c:c + 1] * in_f[c] + h_f[c:c + 1])
        in_b = [h0_b]
        for c in range(SUBLANES - 1, -1, -1):
            in_b.append(a_b[c:c + 1] * in_b[-1] + h_b[c:c + 1])
        if keep:
            hin_f = jnp.concatenate(in_f[:SUBLANES], axis=0)
            hin_b = jnp.concatenate(in_b[SUBLANES - 1::-1], axis=0)

            def fix(jo, carry):
                for u in range(SCAN_UNROLL):
                    j = jo * SCAN_UNROLL + u
                    o = pl.multiple_of(j * SUBLANES, SUBLANES)
                    hp_f[pl.ds(j, SUBLANES, stride=pitch), :] = (
                        hl_f[pl.ds(o, SUBLANES), :] + al_f[pl.ds(o, SUBLANES), :] * hin_f)
                    hp_b[pl.ds(chunk - 1 - j, SUBLANES, stride=pitch), :] = (
                        hl_b[pl.ds(o, SUBLANES), :] + al_b[pl.ds(o, SUBLANES), :] * hin_b)
                return carry
            lax.fori_loop(0, chunk // SCAN_UNROLL, fix, 0)
        return in_f[SUBLANES], in_b[SUBLANES]

    zero = jnp.zeros((1, LANES), F32)
    conv_into(xrc_ref[0], n_ctx)
    coefficients(n_ctx)
    h0_f, h0_b = scan(n_ctx, zero, zero, keep=False)

    conv_into(xr_ref[0], n_lat)
    coefficients(n_lat)
    scan(n_lat, h0_f, h0_b, keep=True)

    chunk = n_lat // SUBLANES
    pitch = _scan_pitch(chunk)

    def emit(c, carry):
        src = pl.multiple_of(c * chunk, 2 * SUBLANES)
        dst = pl.multiple_of(c * pitch, SUBLANES)
        hsum = hp_f[pl.ds(dst, chunk), :] + hp_b[pl.ds(dst, chunk), :]
        y = gr_ref[0, pl.ds(src, chunk), :] * hsum
        y_ref[0, pl.ds(src, chunk), :] = y.astype(y_ref.dtype)
        return carry
    lax.fori_loop(0, SUBLANES, emit, 0)


def _mixers(xr, xr_c, gr, conv_w, conv_b, wg, bgate, lam, u, bg, sc_w):
    bn, n, _ = xr.shape
    n_ctx = xr_c.shape[1]
    assert n % (SUBLANES * SCAN_UNROLL) == 0 and n_ctx % (SUBLANES * SCAN_UNROLL) == 0 and n_ctx <= n
    pitched = SUBLANES * _scan_pitch(n // SUBLANES)
    seq_spec = pl.BlockSpec((1, n, LANES), lambda b, p: (b, 0, p))
    return pl.pallas_call(
        functools.partial(_rnn_kernel, n_lat=n, n_ctx=n_ctx),
        grid=(bn, N_LANE_GROUPS),
        in_specs=[
            seq_spec,
            pl.BlockSpec((1, n_ctx, LANES), lambda b, p: (b, 0, p)),
            seq_spec,
            pl.BlockSpec((4, LANES), lambda b, p: (0, p)),
            pl.BlockSpec((1, LANES), lambda b, p: (0, p)),
            pl.BlockSpec((1, LANES, 4 * LANES), lambda b, p: (p, 0, 0)),
            pl.BlockSpec((1, 1, 4 * LANES), lambda b, p: (p, 0, 0)),
            pl.BlockSpec((2, LANES), lambda b, p: (0, p)),
            seq_spec,
            seq_spec,
            pl.BlockSpec((3, LANES), lambda b, p: (0, p)),
        ],
        out_specs=[seq_spec, seq_spec],
        out_shape=[jax.ShapeDtypeStruct((bn, n, D_RNN), BF16)] * 2,
        scratch_shapes=[pltpu.VMEM((n, LANES), F32)]
        + [pltpu.VMEM((pitched, LANES), F32)] * 4
        + [pltpu.VMEM((n, LANES), F32)] * 4
        + [pltpu.VMEM((pitched, LANES), F32)] * 2
        + [pltpu.VMEM((n + 2 * SUBLANES, LANES), F32)],
        compiler_params=pltpu.CompilerParams(vmem_limit_bytes=VMEM_LIMIT),
        name="mixers",
    )(xr, xr_c, gr, conv_w, conv_b.reshape(1, D_RNN), wg, bgate, lam, u, bg, sc_w)


def _gate_weights(rg_wa, rg_ba, rg_wx, rg_bx):
    eye = jnp.eye(2, dtype=F32)
    blocks, biases = [], []
    for d in range(2):
        for w, bvec in ((rg_wa[d], rg_ba[d]), (rg_wx[d], rg_bx[d])):
            w4 = w.reshape(N_LANE_GROUPS, 2, RNN_HEAD_DIM, RNN_HEAD_DIM)
            bd = jnp.einsum("paij,ac->paicj", w4, eye).reshape(N_LANE_GROUPS, LANES, LANES)
            blocks.append(0.5 * bd)
            biases.append(0.5 * bvec.reshape(N_LANE_GROUPS, 1, LANES))
    return jnp.concatenate(blocks, axis=-1).astype(BF16), jnp.concatenate(biases, axis=-1)


def _gconv_block(u_ref, bg_ref, w_ref, y_ref):
    p = pl.program_id(1)
    u = u_ref[0]
    w = w_ref[...]

    @pl.when(p < D_CONV_H // LANES)
    def _():
        col = lax.broadcasted_iota(jnp.int32, u.shape, 0) % GRID_W
        left = jnp.where(col > 0, _shift_rows(u, 1), 0.0)
        right = jnp.where(col < GRID_W - 1, _shift_rows(u, -1), 0.0)
        y_ref[0] = (bg_ref[0] * (w[0:1] * left + w[1:2] * u + w[2:3] * right)).astype(y_ref.dtype)

    @pl.when(p >= D_CONV_H // LANES)
    def _():
        y_ref[0] = (bg_ref[0] * (w[0:1] * _shift_rows(u, GRID_W) + w[1:2] * u
                                 + w[2:3] * _shift_rows(u, -GRID_W))).astype(y_ref.dtype)


def _lane_max(x, mask):
    return jnp.max(jnp.where(mask, x, -jnp.inf), axis=-1, keepdims=True)


def _first_lane(cond, lane):
    return jnp.min(jnp.where(cond, lane, float(LANES)), axis=-1, keepdims=True)


def _outproj_kernel(x_ref, yr_ref, yc_ref, w32_ref, mod_ref, g_ref, wr_ref, br_ref,
                    x1_ref, mt_ref, route_ref, cnt_ref, rt_ref, carry, w_ref, m_s, *, tm):
    s = pl.program_id(0)

    @pl.when(s == 0)
    def _():
        carry[...] = jnp.zeros_like(carry)
        w_ref[...] = w32_ref[...].astype(BF16)
        m_s[...] = jnp.zeros_like(m_s)

    logits = _dot(m_s[...], wr_ref[...]) + br_ref[...]
    lane_i = lax.broadcasted_iota(jnp.int32, logits.shape, 1)
    lane = lane_i.astype(F32)
    is_grp = lane_i < N_GROUPS
    g_max = _lane_max(logits, is_grp)
    grp = _first_lane(is_grp & (logits == g_max), lane)
    p_g = 1.0 / jnp.sum(jnp.where(is_grp, jnp.exp(logits - g_max), 0.0), axis=-1, keepdims=True)
    lo_lane = EXPERT_LANE0 + grp * EXPERTS_PER_GROUP
    in_grp = (lane >= lo_lane) & (lane < lo_lane + EXPERTS_PER_GROUP)
    l1 = _lane_max(logits, in_grp)
    i1 = _first_lane(in_grp & (logits == l1), lane)
    rest = in_grp & (lane != i1)
    l2 = _lane_max(logits, rest)
    i2 = _first_lane(rest & (logits == l2), lane)
    r21 = jnp.exp(l2 - l1)
    gate1 = p_g / (1.0 + r21)
    gate2 = gate1 * r21

    oh1 = jnp.where(lane == i1, 1.0, 0.0)
    oh2 = jnp.where(lane == i2, 1.0, 0.0)
    both = (oh1 + oh2).astype(BF16)
    ti = lax.broadcasted_iota(jnp.int32, (tm, tm), 0)
    tj = lax.broadcasted_iota(jnp.int32, (tm, tm), 1)
    tri = jnp.where(tj < ti, 1.0, 0.0).astype(BF16)
    counts = carry[...]
    before = _dot(tri, both) + counts
    rank1 = jnp.sum(oh1 * before, axis=-1, keepdims=True)
    rank2 = jnp.sum(oh2 * before, axis=-1, keepdims=True)
    out = jnp.zeros(logits.shape, F32)
    for k, val in enumerate((i1 - EXPERT_LANE0, i2 - EXPERT_LANE0, gate1, gate2, rank1, rank2)):
        out = jnp.where(lane_i == k, val, out)
    route_ref[...] = out
    rt_ref[...] = out.T[0:SUBLANES, :]
    real = jnp.where(s > 0, 1.0, 0.0)
    total = counts + real * jnp.sum(oh1 + oh2, axis=0, keepdims=True)
    carry[...] = total
    cnt_ref[...] = total

    mix = _dot(yr_ref[0], w_ref[0:D_RNN, :]) + _dot(yc_ref[0], w_ref[D_RNN:, :])
    x1 = x_ref[0] + mod_ref[0, 2:3, :] * mix
    x1_ref[0] = x1
    m_new = _norm_mod(x1, g_ref[...], mod_ref[0, 4:5, :], mod_ref[0, 3:4, :])
    m_s[...] = m_new.astype(BF16)
    half = D_MODEL // 2
    packed = pltpu.pack_elementwise([m_new[:, :half], m_new[:, half:]], packed_dtype=BF16)
    for q in range(PACK_TILES):
        mt_ref[pl.ds(q, tm, stride=PACK_TILES), :] = packed[:, q * LANES:(q + 1) * LANES]


def _outproj(x, y_rnn, y_conv, w_out, mod3, norm_g, wr, br):
    bn, n, d = x.shape
    tm = min(OUTPROJ_TILE, n)
    nt = n // tm
    n_tiles = bn * nt
    t_all = bn * n

    def cur(s):
        return jnp.minimum(s, n_tiles - 1)

    def prev(s):
        return jnp.maximum(s - 1, 0)

    def seq_map(s):
        return (cur(s) // nt, cur(s) % nt, 0)

    const = lambda s: (0, 0)
    return pl.pallas_call(
        functools.partial(_outproj_kernel, tm=tm),
        grid=(n_tiles + 1,),
        in_specs=[
            pl.BlockSpec((1, tm, d), seq_map),
            pl.BlockSpec((1, tm, D_RNN), seq_map),
            pl.BlockSpec((1, tm, D_CONV), seq_map),
            pl.BlockSpec((D_RNN + D_CONV, d), const, pipeline_mode=pl.Buffered(1)),
            pl.BlockSpec((1, 6, d), lambda s: (cur(s) // nt, 0, 0)),
            pl.BlockSpec((1, d), const),
            pl.BlockSpec((d, LANES), const),
            pl.BlockSpec((1, LANES), const),
        ],
        out_specs=[
            pl.BlockSpec((1, tm, d), seq_map),
            pl.BlockSpec((tm * PACK_TILES, LANES), lambda s: (cur(s), 0)),
            pl.BlockSpec((tm, LANES), lambda s: (prev(s), 0)),
            pl.BlockSpec((1, LANES), const),
            pl.BlockSpec((SUBLANES, tm), lambda s: (0, prev(s))),
        ],
        out_shape=[
            jax.ShapeDtypeStruct((bn, n, d), F32),
            jax.ShapeDtypeStruct((t_all * PACK_TILES, LANES), jnp.uint32),
            jax.ShapeDtypeStruct((t_all, LANES), F32),
            jax.ShapeDtypeStruct((1, LANES), F32),
            jax.ShapeDtypeStruct((SUBLANES, t_all), F32),
        ],
        scratch_shapes=[pltpu.VMEM((1, LANES), F32), pltpu.VMEM((D_RNN + D_CONV, d), BF16),
                        pltpu.VMEM((tm, d), BF16)],
        compiler_params=pltpu.CompilerParams(
            dimension_semantics=("arbitrary",), vmem_limit_bytes=VMEM_LIMIT),
        name="outproj",
    )(x, y_rnn, y_conv, w_out, mod3, norm_g.reshape(1, d), wr, br)


def _row_tile(ref, row):
    return ref.at[pl.ds(pl.multiple_of(row * ROW_TILES, ROW_TILES), ROW_TILES)]


def _slotmap_kernel(dest_ref, zeros_hbm, asg_ref, sem):
    fill = pltpu.make_async_copy(zeros_hbm, asg_ref, sem)
    fill.start()
    fill.wait()

    def body(c, carry):
        for u in range(DMA_UNROLL):
            a = c * DMA_UNROLL + u
            asg_ref[dest_ref[a]] = a
        return carry
    lax.fori_loop(0, dest_ref.shape[0] // DMA_UNROLL, body, 0)


def _slotmap(dest, n_slots):
    return pl.pallas_call(
        _slotmap_kernel,
        in_specs=[pl.BlockSpec(memory_space=pltpu.SMEM), pl.BlockSpec(memory_space=pl.ANY)],
        out_specs=pl.BlockSpec(memory_space=pltpu.SMEM),
        out_shape=jax.ShapeDtypeStruct((n_slots,), jnp.int32),
        scratch_shapes=[pltpu.SemaphoreType.DMA],
        name="slotmap",
    )(dest, jnp.zeros((n_slots,), jnp.int32))


def _expert_kernel(be_ref, ws_ref, ne_ref, bv_ref, nu_ref, asg_hbm, m_ref, wg_hbm, wu_hbm, wd_hbm, yb_ref,
                   xbuf_a, xbuf_b, idx, isems, wbuf_g, wbuf_u, wbuf_d, wsems, wg_s, wu_s, wd_s):
    j = pl.program_id(0)
    n_used = nu_ref[0]
    last = n_used - 1
    n_tok = m_ref.shape[0] // PACK_TILES

    def idx_copy(blk, sl):
        return pltpu.make_async_copy(asg_hbm.at[blk], idx.at[sl], isems.at[sl])

    def copy_rows(buf, sl, r0, n):
        for u in range(n):
            asg = idx[sl, 0, r0 + u]
            tok = jnp.where(asg >= n_tok, asg - n_tok, asg)
            src = pl.multiple_of(tok * PACK_TILES, PACK_TILES)
            buf[pl.ds((r0 + u) * PACK_TILES, PACK_TILES), :] = m_ref[pl.ds(src, PACK_TILES), :]

    def unpack(buf, rows):
        halves = ([], [])
        for s in range(PACK_TILES):
            word = buf[pl.ds(s, rows, stride=PACK_TILES), :]
            for k in range(2):
                part = pltpu.unpack_elementwise(word, index=k, packed_dtype=BF16, unpacked_dtype=F32)
                halves[k].append(part.astype(BF16))
        return jnp.concatenate(halves[0] + halves[1], axis=-1)

    n_pieces = 2 * D_EXPERT // MXU_TILE + D_MODEL // MXU_TILE
    bounds = [(p * MOE_BLK) // n_pieces for p in range(n_pieces + 1)]

    def compute(cur, nxt, nxt_sl, rows):
        pieces = iter(zip(bounds[:-1], bounds[1:]))

        def dot_pieces(a, w_ref, n0):
            acc = _dot(a, w_ref[:, n0:n0 + MXU_TILE])
            r0, r1 = next(pieces)
            copy_rows(nxt, nxt_sl, r0, r1 - r0)
            return acc

        xb16 = unpack(cur, rows)
        acts = []
        for n0 in range(0, D_EXPERT, MXU_TILE):
            gate = dot_pieces(xb16, wg_s, n0)
            up = dot_pieces(xb16, wu_s, n0)
            acts.append(((gate * jax.nn.sigmoid(gate)) * up).astype(BF16))
        h = jnp.concatenate(acts, axis=-1)
        for n0 in range(0, D_MODEL, MXU_TILE):
            y = dot_pieces(h, wd_s, n0)
            for s in range(MXU_TILE // LANES):
                yb_ref[pl.ds(n0 // LANES + s, rows, stride=ROW_TILES), :] = (
                    y[:, s * LANES:(s + 1) * LANES])
        if rows < MOE_BLK:
            yb_ref[pl.ds(rows * ROW_TILES, (MOE_BLK - rows) * ROW_TILES), :] = jnp.zeros(
                ((MOE_BLK - rows) * ROW_TILES, LANES), F32)

    @pl.when(j >= n_used)
    def _():
        yb_ref[...] = jnp.zeros_like(yb_ref)

    @pl.when(j < n_used)
    def _():
        slot = j % 2
        other = 1 - slot

        @pl.when(j == 0)
        def _():
            idx_copy(0, 0).start()
            idx_copy(0, 0).wait()

            def body(c, carry):
                copy_rows(xbuf_a, 0, c * DMA_UNROLL, DMA_UNROLL)
                return carry
            lax.fori_loop(0, MOE_BLK // DMA_UNROLL, body, 0)
            idx_copy(jnp.minimum(1, last), 1).start()

        e = be_ref[j]
        wslot = ws_ref[j]

        def weight_copies(expert, sl):
            return [pltpu.make_async_copy(src.at[expert], dst.at[sl], wsems.at[sl])
                    for src, dst in ((wg_hbm, wbuf_g), (wu_hbm, wbuf_u), (wd_hbm, wbuf_d))]

        @pl.when(j == 0)
        def _():
            for cp in weight_copies(e, wslot):
                cp.start()

        @pl.when((j == 0) | (e != be_ref[jnp.maximum(j - 1, 0)]))
        def _():
            for cp in weight_copies(e, wslot):
                cp.wait()
            wg_s[...] = wbuf_g[wslot].astype(BF16)
            wu_s[...] = wbuf_u[wslot].astype(BF16)
            wd_s[...] = wbuf_d[wslot].astype(BF16)

            @pl.when(ne_ref[j] >= 0)
            def _():
                for cp in weight_copies(ne_ref[j], 1 - wslot):
                    cp.start()

        idx_copy(0, other).wait()

        short = bv_ref[j] <= MOE_BLK // 2
        for par, (cur, nxt) in enumerate(((xbuf_a, xbuf_b), (xbuf_b, xbuf_a))):
            @pl.when((slot == par) & jnp.logical_not(short))
            def _(cur=cur, nxt=nxt, par=par):
                compute(cur, nxt, 1 - par, MOE_BLK)

            @pl.when((slot == par) & short)
            def _(cur=cur, nxt=nxt, par=par):
                compute(cur, nxt, 1 - par, MOE_BLK // 2)

        @pl.when(j < last)
        def _():
            idx_copy(jnp.minimum(j + 2, last), slot).start()


def _experts(block_e, weight_slot, next_e, block_valid, n_used, slot_asg, mt, w_gate, w_up, w_down, n_blocks):
    return pl.pallas_call(
        _expert_kernel,
        grid_spec=pltpu.PrefetchScalarGridSpec(
            num_scalar_prefetch=5,
            grid=(n_blocks,),
            in_specs=[
                pl.BlockSpec(memory_space=pl.ANY),
                pl.BlockSpec(memory_space=pltpu.VMEM),
                pl.BlockSpec(memory_space=pl.ANY),
                pl.BlockSpec(memory_space=pl.ANY),
                pl.BlockSpec(memory_space=pl.ANY),
            ],
            out_specs=pl.BlockSpec((MOE_BLK * ROW_TILES, LANES), lambda j, *_: (j, 0)),
            scratch_shapes=[
                pltpu.VMEM((MOE_BLK * PACK_TILES, LANES), jnp.uint32),
                pltpu.VMEM((MOE_BLK * PACK_TILES, LANES), jnp.uint32),
                pltpu.SMEM((2, 1, MOE_BLK), jnp.int32),
                pltpu.SemaphoreType.DMA((2,)),
                pltpu.VMEM((2, D_MODEL, D_EXPERT), F32),
                pltpu.VMEM((2, D_MODEL, D_EXPERT), F32),
                pltpu.VMEM((2, D_EXPERT, D_MODEL), F32),
                pltpu.SemaphoreType.DMA((2,)),
                pltpu.VMEM((D_MODEL, D_EXPERT), BF16),
                pltpu.VMEM((D_MODEL, D_EXPERT), BF16),
                pltpu.VMEM((D_EXPERT, D_MODEL), BF16),
            ],
        ),
        out_shape=jax.ShapeDtypeStruct((n_blocks * MOE_BLK * ROW_TILES, LANES), F32),
        compiler_params=pltpu.CompilerParams(
            dimension_semantics=("arbitrary",), vmem_limit_bytes=EXPERT_VMEM_LIMIT),
        name="expert",
    )(block_e, weight_slot, next_e, block_valid, n_used, slot_asg.reshape(n_blocks, 1, MOE_BLK), mt, w_gate,
      w_up, w_down)


def _combine_kernel(dest_ref, yb_hbm, x1_ref, route_ref, mod_ref, g_ref, o_ref, ybuf, sems, *, tc, n_tok):
    i = pl.program_id(0)
    slot = i % 2

    def row_copy(d, sl, k, r):
        return pltpu.make_async_copy(_row_tile(yb_hbm, d), _row_tile(ybuf.at[sl, k], r), sems.at[sl])

    def gather(step, sl):
        def issue(c, carry):
            for u in range(DMA_UNROLL):
                r = c * DMA_UNROLL + u
                for k in range(TOP_K):
                    row_copy(dest_ref[k * n_tok + step * tc + r], sl, k, r).start(priority=k)
            return carry
        lax.fori_loop(0, tc // DMA_UNROLL, issue, 0)

    @pl.when(i == 0)
    def _():
        gather(0, 0)

    @pl.when(i + 1 < pl.num_programs(0))
    def _():
        gather(i + 1, 1 - slot)

    def drain(c, carry):
        for u in range(DMA_UNROLL * TOP_K):
            row_copy(0, slot, 0, 0).wait()
        return carry
    lax.fori_loop(0, tc // DMA_UNROLL, drain, 0)

    def rows(k):
        return jnp.concatenate(
            [ybuf[slot, k, pl.ds(s, tc, stride=ROW_TILES), :] for s in range(ROW_TILES)], axis=-1)

    route = route_ref[...]
    y = route[:, 2:3] * rows(0) + route[:, 3:4] * rows(1)
    x2 = x1_ref[...] + mod_ref[0, 5:6, :] * y
    ms = jnp.mean(x2 * x2, axis=-1, keepdims=True)
    o_ref[...] = x2 * lax.rsqrt(ms + NORM_EPS) * g_ref[...]


def _combine(dest, yb, x1_2d, route, mod3, final_g, seq):
    t_all, d = x1_2d.shape
    tc = min(COMBINE_TILE, seq)
    per_seq = seq // tc
    return pl.pallas_call(
        functools.partial(_combine_kernel, tc=tc, n_tok=t_all),
        grid_spec=pltpu.PrefetchScalarGridSpec(
            num_scalar_prefetch=1,
            grid=(t_all // tc,),
            in_specs=[
                pl.BlockSpec(memory_space=pl.ANY),
                pl.BlockSpec((tc, d), lambda i, dest: (i, 0)),
                pl.BlockSpec((tc, LANES), lambda i, dest: (i, 0)),
                pl.BlockSpec((1, 6, d), lambda i, dest: (i // per_seq, 0, 0)),
                pl.BlockSpec((1, d), lambda i, dest: (0, 0)),
            ],
            out_specs=pl.BlockSpec((tc, d), lambda i, dest: (i, 0)),
            scratch_shapes=[
                pltpu.VMEM((2, TOP_K, tc * ROW_TILES, LANES), F32),
                pltpu.SemaphoreType.DMA((2,)),
            ],
        ),
        out_shape=jax.ShapeDtypeStruct((t_all, d), F32),
        compiler_params=pltpu.CompilerParams(
            dimension_semantics=("arbitrary",), vmem_limit_bytes=VMEM_LIMIT),
        name="combine",
    )(dest, yb, x1_2d, route, mod3, final_g.reshape(1, d))


def kernel(x, c, ctx, c_ctx, ada_w, ada_b, norm1_g, norm2_g, w_in, rnn_conv_w, rnn_conv_b, rg_wa, rg_ba,
           rg_wx, rg_bx, rg_lambda, sc_conv_w, w_out, router_group_w, router_group_b, router_exp_w,
           router_exp_b, exp_w_gate, exp_w_up, exp_w_down, final_norm_g):
    bn, seq, d = x.shape
    assert d == D_MODEL and bn < MOD_ROWS and ada_w.shape[0] == 1
    t_all = bn * seq

    cc = jnp.concatenate([c, c_ctx[None], jnp.zeros((MOD_ROWS - bn - 1, d), F32)], axis=0)
    mod3 = _modulation(cc, ada_w[0], ada_b[0]).reshape(MOD_ROWS, 6, d)

    xr, gr, u, bg = _inproj(x, mod3, None, norm1_g[0], w_in[0], latent=True)
    ctx_len = ctx.shape[1]
    (xr_c,) = _inproj(ctx.reshape(1, bn * ctx_len, d), mod3, bn, norm1_g[0], w_in[0], latent=False)
    xr_c = xr_c.reshape(bn, ctx_len, D_RNN)

    wg, bgate = _gate_weights(rg_wa[0], rg_ba[0], rg_wx[0], rg_bx[0])
    assert D_RNN == D_CONV
    y_rnn, y_conv = _mixers(xr, xr_c, gr, rnn_conv_w[0], rnn_conv_b[0], wg, bgate, rg_lambda[0], u, bg,
                            sc_conv_w[0])

    wr = jnp.zeros((d, LANES), F32)
    wr = wr.at[:, :N_GROUPS].set(router_group_w[0]).at[:, EXPERT_LANE0:EXPERT_LANE0 + N_EXPERTS].set(router_exp_w[0])
    br = jnp.zeros((1, LANES), F32)
    br = br.at[0, :N_GROUPS].set(router_group_b[0]).at[0, EXPERT_LANE0:EXPERT_LANE0 + N_EXPERTS].set(router_exp_b[0])
    x1, mt, route, cnt, route_t = _outproj(x, y_rnn, y_conv, w_out[0], mod3, norm2_g[0], wr.astype(BF16), br)

    n_assign = t_all * TOP_K
    n_blocks = (n_assign + N_EXPERTS * (MOE_BLK - 1) + MOE_BLK - 1) // MOE_BLK
    counts = cnt[0, EXPERT_LANE0:EXPERT_LANE0 + N_EXPERTS].astype(jnp.int32)
    pcounts = (counts + MOE_BLK - 1) // MOE_BLK * MOE_BLK
    pends = jnp.cumsum(pcounts)
    pstarts = pends - pcounts
    experts = route_t[0:TOP_K].astype(jnp.int32)
    ranks = route_t[4:4 + TOP_K].astype(jnp.int32)
    onehot = experts[None] == jnp.arange(N_EXPERTS, dtype=jnp.int32)[:, None, None]
    dest = (ranks + jnp.sum(jnp.where(onehot, pstarts[:, None, None], 0), axis=0)).reshape(n_assign)
    n_used = (pends[-1] // MOE_BLK).astype(jnp.int32)
    blk_start = jnp.arange(n_blocks, dtype=jnp.int32) * MOE_BLK
    block_e = jnp.minimum(jnp.sum(blk_start[:, None] >= pends[None, :], axis=1), N_EXPERTS - 1)
    last_e = jnp.max(jnp.where(counts > 0, jnp.arange(N_EXPERTS, dtype=jnp.int32), 0))
    block_e = jnp.where(blk_start < pends[-1], block_e, last_e).astype(jnp.int32)
    eids = jnp.arange(N_EXPERTS, dtype=jnp.int32)
    used = counts > 0
    slot_of_e = (jnp.cumsum(used.astype(jnp.int32)) - 1) % 2
    later = jnp.where(used[None, :] & (eids[None, :] > eids[:, None]), eids[None, :], N_EXPERTS)
    next_of_e = jnp.min(later, axis=1)
    next_of_e = jnp.where(next_of_e == N_EXPERTS, -1, next_of_e)
    is_e = block_e[:, None] == eids[None, :]
    weight_slot = jnp.sum(jnp.where(is_e, slot_of_e[None, :], 0), axis=1).astype(jnp.int32)
    next_e = jnp.sum(jnp.where(is_e, next_of_e[None, :], 0), axis=1).astype(jnp.int32)
    rows_end = jnp.sum(jnp.where(is_e, (pstarts + counts)[None, :], 0), axis=1)
    block_valid = jnp.clip(rows_end - blk_start, 0, MOE_BLK).astype(jnp.int32)

    n_slots = n_blocks * MOE_BLK
    slot_asg = _slotmap(dest, n_slots)
    yb = _experts(block_e, weight_slot, next_e, block_valid, n_used.reshape(1), slot_asg, mt, exp_w_gate[0],
                  exp_w_up[0], exp_w_down[0], n_blocks)
    out = _combine(dest, yb, x1.reshape(t_all, d), route, mod3, final_norm_g, seq)
    return out.reshape(bn, seq, d)
```

```python
import functools

import jax
import jax.numpy as jnp
from jax import lax
from jax.experimental import pallas as pl
from jax.experimental.pallas import tpu as pltpu

F32 = jnp.float32
BF16 = jnp.bfloat16

D_MODEL = 1024
D_RNN = 512
D_CONV = 512
D_CONV_H = D_CONV // 2
RNN_HEADS = 8
RNN_HEAD_DIM = D_RNN // RNN_HEADS
GRID_W = 64
RG_C = 8.0
N_GROUPS = 4
EXPERTS_PER_GROUP = 8
N_EXPERTS = N_GROUPS * EXPERTS_PER_GROUP
TOP_K = 2
D_EXPERT = 512
NORM_EPS = 1e-6
F32_TINY = 1.1754944e-38

LANES = 128
SUBLANES = 8
ROW_TILES = D_MODEL // LANES
PACK_TILES = ROW_TILES // 2
N_LANE_GROUPS = D_RNN // LANES
EXPERT_LANE0 = N_GROUPS

MOD_ROWS = 16
MOD_TN = 1536
INPROJ_TILE = 1024
OUTPROJ_TILE = 512
COEFF_ROWS = 2048
SCAN_UNROLL = 16
MOE_BLK = 512
MXU_TILE = 256
COMBINE_TILE = 256
DMA_UNROLL = 128
VMEM_LIMIT = 48 * 1024 * 1024
EXPERT_VMEM_LIMIT = 58 * 1024 * 1024


def _dot(a, b):
    return jnp.dot(a, b, preferred_element_type=F32)


def _split_bf16(x):
    hi = x.astype(BF16)
    lo = (x - hi.astype(F32)).astype(BF16)
    return hi, lo


def _mod_kernel(cc_ref, w_ref, b_ref, o_ref):
    s = cc_ref[...]
    s = s * jax.nn.sigmoid(s)
    s_hi, s_lo = _split_bf16(s)
    w_hi, w_lo = _split_bf16(w_ref[...])
    o_ref[...] = _dot(s_hi, w_hi) + _dot(s_lo, w_hi) + _dot(s_hi, w_lo) + b_ref[...]


def _modulation(cc, ada_w, ada_b):
    n = ada_w.shape[1]
    return pl.pallas_call(
        _mod_kernel,
        grid=(n // MOD_TN,),
        in_specs=[
            pl.BlockSpec((MOD_ROWS, D_MODEL), lambda j: (0, 0)),
            pl.BlockSpec((D_MODEL, MOD_TN), lambda j: (0, j)),
            pl.BlockSpec((1, MOD_TN), lambda j: (0, j)),
        ],
        out_specs=pl.BlockSpec((MOD_ROWS, MOD_TN), lambda j: (0, j)),
        out_shape=jax.ShapeDtypeStruct((MOD_ROWS, n), F32),
        compiler_params=pltpu.CompilerParams(vmem_limit_bytes=VMEM_LIMIT),
        name="mod",
    )(cc, ada_w, ada_b.reshape(1, n))


def _norm_mod(x, g, scale, shift):
    ms = jnp.mean(x * x, axis=-1, keepdims=True)
    y = x * lax.rsqrt(ms + NORM_EPS) * g
    return y * (1.0 + scale) + shift


def _inproj_kernel(x_ref, mod_ref, g_ref, w32_ref, *refs, latent):
    out_refs, w_ref = refs[:-1], refs[-1]

    @pl.when((pl.program_id(0) == 0) & (pl.program_id(1) == 0))
    def _():
        w_ref[...] = w32_ref[...].astype(BF16)

    h = _norm_mod(x_ref[0], g_ref[...], mod_ref[0, 1:2, :], mod_ref[0, 0:1, :])
    hb = h.astype(BF16)
    xr = _dot(hb, w_ref[:, 0:D_RNN])
    out_refs[0][0] = xr
    if latent:
        o = D_RNN
        out_refs[1][0] = jax.nn.gelu(_dot(hb, w_ref[:, o:o + D_RNN]), approximate=True)
        o += D_RNN
        v = _dot(hb, w_ref[:, o:o + D_CONV])
        out_refs[3][0] = _dot(hb, w_ref[:, o + D_CONV:o + 2 * D_CONV])
        cg = _dot(hb, w_ref[:, o + 2 * D_CONV:o + 3 * D_CONV])
        out_refs[2][0] = cg * v


def _inproj(x, mod3, mod_row, norm_g, w_in, latent):
    bn, n, d = x.shape
    tm = min(INPROJ_TILE, n)
    assert n % tm == 0
    n_out = 4 if latent else 1
    width = w_in.shape[1] if latent else D_RNN
    mod_map = (lambda b, i: (b, 0, 0)) if mod_row is None else (lambda b, i: (mod_row, 0, 0))
    return pl.pallas_call(
        functools.partial(_inproj_kernel, latent=latent),
        grid=(bn, n // tm),
        in_specs=[
            pl.BlockSpec((1, tm, d), lambda b, i: (b, i, 0)),
            pl.BlockSpec((1, 6, d), mod_map),
            pl.BlockSpec((1, d), lambda b, i: (0, 0)),
            pl.BlockSpec((d, width), lambda b, i: (0, 0), pipeline_mode=pl.Buffered(1)),
        ],
        out_specs=[pl.BlockSpec((1, tm, D_RNN), lambda b, i: (b, i, 0))] * n_out,
        out_shape=[jax.ShapeDtypeStruct((bn, n, D_RNN), F32)] * n_out,
        scratch_shapes=[pltpu.VMEM((d, width), BF16)],
        compiler_params=pltpu.CompilerParams(
            dimension_semantics=("arbitrary", "arbitrary"), vmem_limit_bytes=VMEM_LIMIT),
        name="inproj_lat" if latent else "inproj_ctx",
    )(x, mod3, norm_g.reshape(1, d), w_in)


def _shift_rows(x, k):
    n = x.shape[0]
    row = lax.broadcasted_iota(jnp.int32, x.shape, 0)
    rolled = pltpu.roll(x, k % n, axis=0)
    valid = (row >= k) if k > 0 else (row < n + k)
    return jnp.where(valid, rolled, 0.0)


def _scan_pitch(chunk):
    pitch = chunk + SUBLANES
    return pitch if (pitch // SUBLANES) % 2 else pitch + SUBLANES


def _rnn_kernel(xr_ref, xrc_ref, gr_ref, cw_ref, cb_ref, wg_ref, bg_ref, lam_ref, u_ref, bgc_ref, scw_ref,
                y_ref, yc_ref,
                xc_s, ap_f, bp_f, ap_b, bp_b, hl_f, al_f, hl_b, al_b, hp_f, hp_b, xpad, *, n_lat, n_ctx):
    _gconv_block(u_ref, bgc_ref, scw_ref, yc_ref)

    nl = -lam_ref[...]
    sp = jnp.maximum(nl, 0.0) + jnp.log1p(jnp.exp(-jnp.abs(nl)))
    c1 = (0.5 * RG_C) * sp
    cw = cw_ref[...]
    bias = cb_ref[...]
    wg = wg_ref[0]
    bg = bg_ref[0]
    dirs = ((ap_f, bp_f, hl_f, al_f, hp_f), (ap_b, bp_b, hl_b, al_b, hp_b))

    def conv_into(x, n):
        halo = jnp.zeros((SUBLANES, LANES), F32)
        xpad[pl.ds(0, SUBLANES), :] = halo
        xpad[pl.ds(SUBLANES, n), :] = x
        xpad[pl.ds(SUBLANES + n, SUBLANES), :] = halo
        xc_s[pl.ds(0, n), :] = (cw[0:1] * xpad[pl.ds(SUBLANES - 2, n), :]
                                + cw[1:2] * xpad[pl.ds(SUBLANES - 1, n), :] + cw[2:3] * x
                                + cw[3:4] * xpad[pl.ds(SUBLANES + 1, n), :]) + bias

    def coefficients(n):
        chunk = n // SUBLANES
        pitch = _scan_pitch(chunk)
        rows = max(chunk, min(n, COEFF_ROWS))
        per = rows // chunk

        def body(i, carry):
            xc = xc_s[pl.ds(pl.multiple_of(i * rows, SUBLANES), rows), :]
            gates = _dot(xc.astype(BF16), wg) + bg
            half_xc = 0.5 * xc
            for d in range(2):
                tr = jnp.tanh(gates[:, (2 * d) * LANES:(2 * d + 1) * LANES])
                ti = jnp.tanh(gates[:, (2 * d + 1) * LANES:(2 * d + 2) * LANES])
                neg_log_a = c1[d:d + 1] + c1[d:d + 1] * tr
                a = jnp.exp(-neg_log_a)
                y = jnp.tanh(neg_log_a) * (a * a + 1.0)
                b = (y * lax.rsqrt(jnp.maximum(y, F32_TINY))) * (half_xc + half_xc * ti)
                for k in range(per):
                    dst = pl.multiple_of((i * per + k) * pitch, SUBLANES)
                    dirs[d][0][pl.ds(dst, chunk), :] = a[k * chunk:(k + 1) * chunk]
                    dirs[d][1][pl.ds(dst, chunk), :] = b[k * chunk:(k + 1) * chunk]
            return carry

        if n == rows:
            body(0, 0)
        else:
            lax.fori_loop(0, n // rows, body, 0)

    def scan(n, h0_f, h0_b, keep):
        chunk = n // SUBLANES
        pitch = _scan_pitch(chunk)

        def steps(jo, carry):
            h_f, a_f, h_b, a_b = carry
            for u in range(SCAN_UNROLL):
                j = jo * SCAN_UNROLL + u
                av = ap_f[pl.ds(j, SUBLANES, stride=pitch), :]
                h_f = av * h_f + bp_f[pl.ds(j, SUBLANES, stride=pitch), :]
                a_f = av * a_f
                jb = chunk - 1 - j
                av = ap_b[pl.ds(jb, SUBLANES, stride=pitch), :]
                h_b = av * h_b + bp_b[pl.ds(jb, SUBLANES, stride=pitch), :]
                a_b = av * a_b
                if keep:
                    o = pl.multiple_of(j * SUBLANES, SUBLANES)
                    hl_f[pl.ds(o, SUBLANES), :] = h_f
                    al_f[pl.ds(o, SUBLANES), :] = a_f
                    hl_b[pl.ds(o, SUBLANES), :] = h_b
                    al_b[pl.ds(o, SUBLANES), :] = a_b
            return h_f, a_f, h_b, a_b

        zeros = jnp.zeros((SUBLANES, LANES), F32)
        ones = jnp.ones((SUBLANES, LANES), F32)
        h_f, a_f, h_b, a_b = lax.fori_loop(0, chunk // SCAN_UNROLL, steps, (zeros, ones, zeros, ones))

        in_f = [h0_f]
        for c in range(SUBLANES):
            in_f.append(a_f[c:c + 1] * in_f[c] + h_f[c:c + 1])
        in_b = [h0_b]
        for c in range(SUBLANES - 1, -1, -1):
            in_b.append(a_b[c:c + 1] * in_b[-1] + h_b[c:c + 1])
        if keep:
            hin_f = jnp.concatenate(in_f[:SUBLANES], axis=0)
            hin_b = jnp.concatenate(in_b[SUBLANES - 1::-1], axis=0)

            def fix(jo, carry):
                for u in range(SCAN_UNROLL):
                    j = jo * SCAN_UNROLL + u
                    o = pl.multiple_of(j * SUBLANES, SUBLANES)
                    hp_f[pl.ds(j, SUBLANES, stride=pitch), :] = (
                        hl_f[pl.ds(o, SUBLANES), :] + al_f[pl.ds(o, SUBLANES), :] * hin_f)
                    hp_b[pl.ds(chunk - 1 - j, SUBLANES, stride=pitch), :] = (
                        hl_b[pl.ds(o, SUBLANES), :] + al_b[pl.ds(o, SUBLANES), :] * hin_b)
                return carry
            lax.fori_loop(0, chunk // SCAN_UNROLL, fix, 0)
        return in_f[SUBLANES], in_b[SUBLANES]

    zero = jnp.zeros((1, LANES), F32)
    conv_into(xrc_ref[0], n_ctx)
    coefficients(n_ctx)
    h0_f, h0_b = scan(n_ctx, zero, zero, keep=False)

    conv_into(xr_ref[0], n_lat)
    coefficients(n_lat)
    scan(n_lat, h0_f, h0_b, keep=True)

    chunk = n_lat // SUBLANES
    pitch = _scan_pitch(chunk)

    def emit(c, carry):
        src = pl.multiple_of(c * chunk, 2 * SUBLANES)
        dst = pl.multiple_of(c * pitch, SUBLANES)
        hsum = hp_f[pl.ds(dst, chunk), :] + hp_b[pl.ds(dst, chunk), :]
        y = gr_ref[0, pl.ds(src, chunk), :] * hsum
        y_ref[0, pl.ds(src, chunk), :] = y.astype(y_ref.dtype)
        return carry
    lax.fori_loop(0, SUBLANES, emit, 0)


def _mixers(xr, xr_c, gr, conv_w, conv_b, wg, bgate, lam, u, bg, sc_w):
    bn, n, _ = xr.shape
    n_ctx = xr_c.shape[1]
    assert n % (SUBLANES * SCAN_UNROLL) == 0 and n_ctx % (SUBLANES * SCAN_UNROLL) == 0 and n_ctx <= n
    pitched = SUBLANES * _scan_pitch(n // SUBLANES)
    seq_spec = pl.BlockSpec((1, n, LANES), lambda b, p: (b, 0, p))
    return pl.pallas_call(
        functools.partial(_rnn_kernel, n_lat=n, n_ctx=n_ctx),
        grid=(bn, N_LANE_GROUPS),
        in_specs=[
            seq_spec,
            pl.BlockSpec((1, n_ctx, LANES), lambda b, p: (b, 0, p)),
            seq_spec,
            pl.BlockSpec((4, LANES), lambda b, p: (0, p)),
            pl.BlockSpec((1, LANES), lambda b, p: (0, p)),
            pl.BlockSpec((1, LANES, 4 * LANES), lambda b, p: (p, 0, 0)),
            pl.BlockSpec((1, 1, 4 * LANES), lambda b, p: (p, 0, 0)),
            pl.BlockSpec((2, LANES), lambda b, p: (0, p)),
            seq_spec,
            seq_spec,
            pl.BlockSpec((3, LANES), lambda b, p: (0, p)),
        ],
        out_specs=[seq_spec, seq_spec],
        out_shape=[jax.ShapeDtypeStruct((bn, n, D_RNN), BF16)] * 2,
        scratch_shapes=[pltpu.VMEM((n, LANES), F32)]
        + [pltpu.VMEM((pitched, LANES), F32)] * 4
        + [pltpu.VMEM((n, LANES), F32)] * 4
        + [pltpu.VMEM((pitched, LANES), F32)] * 2
        + [pltpu.VMEM((n + 2 * SUBLANES, LANES), F32)],
        compiler_params=pltpu.CompilerParams(vmem_limit_bytes=VMEM_LIMIT),
        name="mixers",
    )(xr, xr_c, gr, conv_w, conv_b.reshape(1, D_RNN), wg, bgate, lam, u, bg, sc_w)


def _gate_weights(rg_wa, rg_ba, rg_wx, rg_bx):
    eye = jnp.eye(2, dtype=F32)
    blocks, biases = [], []
    for d in range(2):
        for w, bvec in ((rg_wa[d], rg_ba[d]), (rg_wx[d], rg_bx[d])):
            w4 = w.reshape(N_LANE_GROUPS, 2, RNN_HEAD_DIM, RNN_HEAD_DIM)
            bd = jnp.einsum("paij,ac->paicj", w4, eye).reshape(N_LANE_GROUPS, LANES, LANES)
            blocks.append(0.5 * bd)
            biases.append(0.5 * bvec.reshape(N_LANE_GROUPS, 1, LANES))
    return jnp.concatenate(blocks, axis=-1).astype(BF16), jnp.concatenate(biases, axis=-1)


def _gconv_block(u_ref, bg_ref, w_ref, y_ref):
    p = pl.program_id(1)
    u = u_ref[0]
    w = w_ref[...]

    @pl.when(p < D_CONV_H // LANES)
    def _():
        col = lax.broadcasted_iota(jnp.int32, u.shape, 0) % GRID_W
        left = jnp.where(col > 0, _shift_rows(u, 1), 0.0)
        right = jnp.where(col < GRID_W - 1, _shift_rows(u, -1), 0.0)
        y_ref[0] = (bg_ref[0] * (w[0:1] * left + w[1:2] * u + w[2:3] * right)).astype(y_ref.dtype)

    @pl.when(p >= D_CONV_H // LANES)
    def _():
        y_ref[0] = (bg_ref[0] * (w[0:1] * _shift_rows(u, GRID_W) + w[1:2] * u
                                 + w[2:3] * _shift_rows(u, -GRID_W))).astype(y_ref.dtype)


def _lane_max(x, mask):
    return jnp.max(jnp.where(mask, x, -jnp.inf), axis=-1, keepdims=True)


def _first_lane(cond, lane):
    return jnp.min(jnp.where(cond, lane, float(LANES)), axis=-1, keepdims=True)


def _outproj_kernel(x_ref, yr_ref, yc_ref, w32_ref, mod_ref, g_ref, wr_ref, br_ref,
                    x1_ref, mt_ref, route_ref, cnt_ref, rt_ref, carry, w_ref, m_s, *, tm):
    s = pl.program_id(0)

    @pl.when(s == 0)
    def _():
        carry[...] = jnp.zeros_like(carry)
        w_ref[...] = w32_ref[...].astype(BF16)
        m_s[...] = jnp.zeros_like(m_s)

    logits = _dot(m_s[...], wr_ref[...]) + br_ref[...]
    lane_i = lax.broadcasted_iota(jnp.int32, logits.shape, 1)
    lane = lane_i.astype(F32)
    is_grp = lane_i < N_GROUPS
    g_max = _lane_max(logits, is_grp)
    grp = _first_lane(is_grp & (logits == g_max), lane)
    p_g = 1.0 / jnp.sum(jnp.where(is_grp, jnp.exp(logits - g_max), 0.0), axis=-1, keepdims=True)
    lo_lane = EXPERT_LANE0 + grp * EXPERTS_PER_GROUP
    in_grp = (lane >= lo_lane) & (lane < lo_lane + EXPERTS_PER_GROUP)
    l1 = _lane_max(logits, in_grp)
    i1 = _first_lane(in_grp & (logits == l1), lane)
    rest = in_grp & (lane != i1)
    l2 = _lane_max(logits, rest)
    i2 = _first_lane(rest & (logits == l2), lane)
    r21 = jnp.exp(l2 - l1)
    gate1 = p_g / (1.0 + r21)
    gate2 = gate1 * r21

    oh1 = jnp.where(lane == i1, 1.0, 0.0)
    oh2 = jnp.where(lane == i2, 1.0, 0.0)
    both = (oh1 + oh2).astype(BF16)
    ti = lax.broadcasted_iota(jnp.int32, (tm, tm), 0)
    tj = lax.broadcasted_iota(jnp.int32, (tm, tm), 1)
    tri = jnp.where(tj < ti, 1.0, 0.0).astype(BF16)
    counts = carry[...]
    before = _dot(tri, both) + counts
    rank1 = jnp.sum(oh1 * before, axis=-1, keepdims=True)
    rank2 = jnp.sum(oh2 * before, axis=-1, keepdims=True)
    out = jnp.zeros(logits.shape, F32)
    for k, val in enumerate((i1 - EXPERT_LANE0, i2 - EXPERT_LANE0, gate1, gate2, rank1, rank2)):
        out = jnp.where(lane_i == k, val, out)
    route_ref[...] = out
    rt_ref[...] = out.T[0:SUBLANES, :]
    real = jnp.where(s > 0, 1.0, 0.0)
    total = counts + real * jnp.sum(oh1 + oh2, axis=0, keepdims=True)
    carry[...] = total
    cnt_ref[...] = total

    mix = _dot(yr_ref[0], w_ref[0:D_RNN, :]) + _dot(yc_ref[0], w_ref[D_RNN:, :])
    x1 = x_ref[0] + mod_ref[0, 2:3, :] * mix
    x1_ref[0] = x1
    m_new = _norm_mod(x1, g_ref[...], mod_ref[0, 4:5, :], mod_ref[0, 3:4, :])
    m_s[...] = m_new.astype(BF16)
    half = D_MODEL // 2
    packed = pltpu.pack_elementwise([m_new[:, :half], m_new[:, half:]], packed_dtype=BF16)
    for q in range(PACK_TILES):
        mt_ref[pl.ds(q, tm, stride=PACK_TILES), :] = packed[:, q * LANES:(q + 1) * LANES]


def _outproj(x, y_rnn, y_conv, w_out, mod3, norm_g, wr, br):
    bn, n, d = x.shape
    tm = min(OUTPROJ_TILE, n)
    nt = n // tm
    n_tiles = bn * nt
    t_all = bn * n

    def cur(s):
        return jnp.minimum(s, n_tiles - 1)

    def prev(s):
        return jnp.maximum(s - 1, 0)

    def seq_map(s):
        return (cur(s) // nt, cur(s) % nt, 0)

    const = lambda s: (0, 0)
    return pl.pallas_call(
        functools.partial(_outproj_kernel, tm=tm),
        grid=(n_tiles + 1,),
        in_specs=[
            pl.BlockSpec((1, tm, d), seq_map),
            pl.BlockSpec((1, tm, D_RNN), seq_map),
            pl.BlockSpec((1, tm, D_CONV), seq_map),
            pl.BlockSpec((D_RNN + D_CONV, d), const, pipeline_mode=pl.Buffered(1)),
            pl.BlockSpec((1, 6, d), lambda s: (cur(s) // nt, 0, 0)),
            pl.BlockSpec((1, d), const),
            pl.BlockSpec((d, LANES), const),
            pl.BlockSpec((1, LANES), const),
        ],
        out_specs=[
            pl.BlockSpec((1, tm, d), seq_map),
            pl.BlockSpec((tm * PACK_TILES, LANES), lambda s: (cur(s), 0)),
            pl.BlockSpec((tm, LANES), lambda s: (prev(s), 0)),
            pl.BlockSpec((1, LANES), const),
            pl.BlockSpec((SUBLANES, tm), lambda s: (0, prev(s))),
        ],
        out_shape=[
            jax.ShapeDtypeStruct((bn, n, d), F32),
            jax.ShapeDtypeStruct((t_all * PACK_TILES, LANES), jnp.uint32),
            jax.ShapeDtypeStruct((t_all, LANES), F32),
            jax.ShapeDtypeStruct((1, LANES), F32),
            jax.ShapeDtypeStruct((SUBLANES, t_all), F32),
        ],
        scratch_shapes=[pltpu.VMEM((1, LANES), F32), pltpu.VMEM((D_RNN + D_CONV, d), BF16),
                        pltpu.VMEM((tm, d), BF16)],
        compiler_params=pltpu.CompilerParams(
            dimension_semantics=("arbitrary",), vmem_limit_bytes=VMEM_LIMIT),
        name="outproj",
    )(x, y_rnn, y_conv, w_out, mod3, norm_g.reshape(1, d), wr, br)


def _row_tile(ref, row):
    return ref.at[pl.ds(pl.multiple_of(row * ROW_TILES, ROW_TILES), ROW_TILES)]


def _slotmap_kernel(dest_ref, zeros_hbm, asg_ref, sem):
    fill = pltpu.make_async_copy(zeros_hbm, asg_ref, sem)
    fill.start()
    fill.wait()

    def body(c, carry):
        for u in range(DMA_UNROLL):
            a = c * DMA_UNROLL + u
            asg_ref[dest_ref[a]] = a
        return carry
    lax.fori_loop(0, dest_ref.shape[0] // DMA_UNROLL, body, 0)


def _slotmap(dest, n_slots):
    return pl.pallas_call(
        _slotmap_kernel,
        in_specs=[pl.BlockSpec(memory_space=pltpu.SMEM), pl.BlockSpec(memory_space=pl.ANY)],
        out_specs=pl.BlockSpec(memory_space=pltpu.SMEM),
        out_shape=jax.ShapeDtypeStruct((n_slots,), jnp.int32),
        scratch_shapes=[pltpu.SemaphoreType.DMA],
        name="slotmap",
    )(dest, jnp.zeros((n_slots,), jnp.int32))


def _expert_kernel(be_ref, ws_ref, ne_ref, bv_ref, nu_ref, asg_hbm, m_ref, wg_hbm, wu_hbm, wd_hbm, yb_ref,
                   xbuf_a, xbuf_b, idx, isems, wbuf_g, wbuf_u, wbuf_d, wsems, wg_s, wu_s, wd_s):
    j = pl.program_id(0)
    n_used = nu_ref[0]
    last = n_used - 1
    n_tok = m_ref.shape[0] // PACK_TILES

    def idx_copy(blk, sl):
        return pltpu.make_async_copy(asg_hbm.at[blk], idx.at[sl], isems.at[sl])

    def copy_rows(buf, sl, r0, n):
        for u in range(n):
            asg = idx[sl, 0, r0 + u]
            tok = jnp.where(asg >= n_tok, asg - n_tok, asg)
            src = pl.multiple_of(tok * PACK_TILES, PACK_TILES)
            buf[pl.ds((r0 + u) * PACK_TILES, PACK_TILES), :] = m_ref[pl.ds(src, PACK_TILES), :]

    def unpack(buf, rows):
        halves = ([], [])
        for s in range(PACK_TILES):
            word = buf[pl.ds(s, rows, stride=PACK_TILES), :]
            for k in range(2):
                part = pltpu.unpack_elementwise(word, index=k, packed_dtype=BF16, unpacked_dtype=F32)
                halves[k].append(part.astype(BF16))
        return jnp.concatenate(halves[0] + halves[1], axis=-1)

    n_pieces = 2 * D_EXPERT // MXU_TILE + D_MODEL // MXU_TILE
    bounds = [(p * MOE_BLK) // n_pieces for p in range(n_pieces + 1)]

    def compute(cur, nxt, nxt_sl, rows):
        pieces = iter(zip(bounds[:-1], bounds[1:]))

        def dot_pieces(a, w_ref, n0):
            acc = _dot(a, w_ref[:, n0:n0 + MXU_TILE])
            r0, r1 = next(pieces)
            copy_rows(nxt, nxt_sl, r0, r1 - r0)
            return acc

        xb16 = unpack(cur, rows)
        acts = []
        for n0 in range(0, D_EXPERT, MXU_TILE):
            gate = dot_pieces(xb16, wg_s, n0)
            up = dot_pieces(xb16, wu_s, n0)
            acts.append(((gate * jax.nn.sigmoid(gate)) * up).astype(BF16))
        h = jnp.concatenate(acts, axis=-1)
        for n0 in range(0, D_MODEL, MXU_TILE):
            y = dot_pieces(h, wd_s, n0)
            for s in range(MXU_TILE // LANES):
                yb_ref[pl.ds(n0 // LANES + s, rows, stride=ROW_TILES), :] = (
                    y[:, s * LANES:(s + 1) * LANES])
        if rows < MOE_BLK:
            yb_ref[pl.ds(rows * ROW_TILES, (MOE_BLK - rows) * ROW_TILES), :] = jnp.zeros(
                ((MOE_BLK - rows) * ROW_TILES, LANES), F32)

    @pl.when(j >= n_used)
    def _():
        yb_ref[...] = jnp.zeros_like(yb_ref)

    @pl.when(j < n_used)
    def _():
        slot = j % 2
        other = 1 - slot

        @pl.when(j == 0)
        def _():
            idx_copy(0, 0).start()
            idx_copy(0, 0).wait()

            def body(c, carry):
                copy_rows(xbuf_a, 0, c * DMA_UNROLL, DMA_UNROLL)
                return carry
            lax.fori_loop(0, MOE_BLK // DMA_UNROLL, body, 0)
            idx_copy(jnp.minimum(1, last), 1).start()

        e = be_ref[j]
        wslot = ws_ref[j]

        def weight_copies(expert, sl):
            return [pltpu.make_async_copy(src.at[expert], dst.at[sl], wsems.at[sl])
                    for src, dst in ((wg_hbm, wbuf_g), (wu_hbm, wbuf_u), (wd_hbm, wbuf_d))]

        @pl.when(j == 0)
        def _():
            for cp in weight_copies(e, wslot):
                cp.start()

        @pl.when((j == 0) | (e != be_ref[jnp.maximum(j - 1, 0)]))
        def _():
            for cp in weight_copies(e, wslot):
                cp.wait()
            wg_s[...] = wbuf_g[wslot].astype(BF16)
            wu_s[...] = wbuf_u[wslot].astype(BF16)
            wd_s[...] = wbuf_d[wslot].astype(BF16)

            @pl.when(ne_ref[j] >= 0)
            def _():
                for cp in weight_copies(ne_ref[j], 1 - wslot):
                    cp.start()

        idx_copy(0, other).wait()

        short = bv_ref[j] <= MOE_BLK // 2
        for par, (cur, nxt) in enumerate(((xbuf_a, xbuf_b), (xbuf_b, xbuf_a))):
            @pl.when((slot == par) & jnp.logical_not(short))
            def _(cur=cur, nxt=nxt, par=par):
                compute(cur, nxt, 1 - par, MOE_BLK)

            @pl.when((slot == par) & short)
            def _(cur=cur, nxt=nxt, par=par):
                compute(cur, nxt, 1 - par, MOE_BLK // 2)

        @pl.when(j < last)
        def _():
            idx_copy(jnp.minimum(j + 2, last), slot).start()


def _experts(block_e, weight_slot, next_e, block_valid, n_used, slot_asg, mt, w_gate, w_up, w_down, n_blocks):
    return pl.pallas_call(
        _expert_kernel,
        grid_spec=pltpu.PrefetchScalarGridSpec(
            num_scalar_prefetch=5,
            grid=(n_blocks,),
            in_specs=[
                pl.BlockSpec(memory_space=pl.ANY),
                pl.BlockSpec(memory_space=pltpu.VMEM),
                pl.BlockSpec(memory_space=pl.ANY),
                pl.BlockSpec(memory_space=pl.ANY),
                pl.BlockSpec(memory_space=pl.ANY),
            ],
            out_specs=pl.BlockSpec((MOE_BLK * ROW_TILES, LANES), lambda j, *_: (j, 0)),
            scratch_shapes=[
                pltpu.VMEM((MOE_BLK * PACK_TILES, LANES), jnp.uint32),
                pltpu.VMEM((MOE_BLK * PACK_TILES, LANES), jnp.uint32),
                pltpu.SMEM((2, 1, MOE_BLK), jnp.int32),
                pltpu.SemaphoreType.DMA((2,)),
                pltpu.VMEM((2, D_MODEL, D_EXPERT), F32),
                pltpu.VMEM((2, D_MODEL, D_EXPERT), F32),
                pltpu.VMEM((2, D_EXPERT, D_MODEL), F32),
                pltpu.SemaphoreType.DMA((2,)),
                pltpu.VMEM((D_MODEL, D_EXPERT), BF16),
                pltpu.VMEM((D_MODEL, D_EXPERT), BF16),
                pltpu.VMEM((D_EXPERT, D_MODEL), BF16),
            ],
        ),
        out_shape=jax.ShapeDtypeStruct((n_blocks * MOE_BLK * ROW_TILES, LANES), F32),
        compiler_params=pltpu.CompilerParams(
            dimension_semantics=("arbitrary",), vmem_limit_bytes=EXPERT_VMEM_LIMIT),
        name="expert",
    )(block_e, weight_slot, next_e, block_valid, n_used, slot_asg.reshape(n_blocks, 1, MOE_BLK), mt, w_gate,
      w_up, w_down)


def _combine_kernel(dest_ref, yb_hbm, x1_ref, route_ref, mod_ref, g_ref, o_ref, ybuf, sems, *, tc, n_tok):
    i = pl.program_id(0)
    slot = i % 2

    def row_copy(d, sl, k, r):
        return pltpu.make_async_copy(_row_tile(yb_hbm, d), _row_tile(ybuf.at[sl, k], r), sems.at[sl])

    def gather(step, sl):
        def issue(c, carry):
            for u in range(DMA_UNROLL):
                r = c * DMA_UNROLL + u
                for k in range(TOP_K):
                    row_copy(dest_ref[k * n_tok + step * tc + r], sl, k, r).start(priority=k)
            return carry
        lax.fori_loop(0, tc // DMA_UNROLL, issue, 0)

    @pl.when(i == 0)
    def _():
        gather(0, 0)

    @pl.when(i + 1 < pl.num_programs(0))
    def _():
        gather(i + 1, 1 - slot)

    def drain(c, carry):
        for u in range(DMA_UNROLL * TOP_K):
            row_copy(0, slot, 0, 0).wait()
        return carry
    lax.fori_loop(0, tc // DMA_UNROLL, drain, 0)

    def rows(k):
        return jnp.concatenate(
            [ybuf[slot, k, pl.ds(s, tc, stride=ROW_TILES), :] for s in range(ROW_TILES)], axis=-1)

    route = route_ref[...]
    y = route[:, 2:3] * rows(0) + route[:, 3:4] * rows(1)
    x2 = x1_ref[...] + mod_ref[0, 5:6, :] * y
    ms = jnp.mean(x2 * x2, axis=-1, keepdims=True)
    o_ref[...] = x2 * lax.rsqrt(ms + NORM_EPS) * g_ref[...]


def _combine(dest, yb, x1_2d, route, mod3, final_g, seq):
    t_all, d = x1_2d.shape
    tc = min(COMBINE_TILE, seq)
    per_seq = seq // tc
    return pl.pallas_call(
        functools.partial(_combine_kernel, tc=tc, n_tok=t_all),
        grid_spec=pltpu.PrefetchScalarGridSpec(
            num_scalar_prefetch=1,
            grid=(t_all // tc,),
            in_specs=[
                pl.BlockSpec(memory_space=pl.ANY),
                pl.BlockSpec((tc, d), lambda i, dest: (i, 0)),
                pl.BlockSpec((tc, LANES), lambda i, dest: (i, 0)),
                pl.BlockSpec((1, 6, d), lambda i, dest: (i // per_seq, 0, 0)),
                pl.BlockSpec((1, d), lambda i, dest: (0, 0)),
            ],
            out_specs=pl.BlockSpec((tc, d), lambda i, dest: (i, 0)),
            scratch_shapes=[
                pltpu.VMEM((2, TOP_K, tc * ROW_TILES, LANES), F32),
                pltpu.SemaphoreType.DMA((2,)),
            ],
        ),
        out_shape=jax.ShapeDtypeStruct((t_all, d), F32),
        compiler_params=pltpu.CompilerParams(
            dimension_semantics=("arbitrary",), vmem_limit_bytes=VMEM_LIMIT),
        name="combine",
    )(dest, yb, x1_2d, route, mod3, final_g.reshape(1, d))


def kernel(x, c, ctx, c_ctx, ada_w, ada_b, norm1_g, norm2_g, w_in, rnn_conv_w, rnn_conv_b, rg_wa, rg_ba,
           rg_wx, rg_bx, rg_lambda, sc_conv_w, w_out, router_group_w, router_group_b, router_exp_w,
           router_exp_b, exp_w_gate, exp_w_up, exp_w_down, final_norm_g):
    bn, seq, d = x.shape
    assert d == D_MODEL and bn < MOD_ROWS and ada_w.shape[0] == 1
    t_all = bn * seq

    cc = jnp.concatenate([c, c_ctx[None], jnp.zeros((MOD_ROWS - bn - 1, d), F32)], axis=0)
    mod3 = _modulation(cc, ada_w[0], ada_b[0]).reshape(MOD_ROWS, 6, d)

    xr, gr, u, bg = _inproj(x, mod3, None, norm1_g[0], w_in[0], latent=True)
    ctx_len = ctx.shape[1]
    (xr_c,) = _inproj(ctx.reshape(1, bn * ctx_len, d), mod3, bn, norm1_g[0], w_in[0], latent=False)
    xr_c = xr_c.reshape(bn, ctx_len, D_RNN)

    wg, bgate = _gate_weights(rg_wa[0], rg_ba[0], rg_wx[0], rg_bx[0])
    assert D_RNN == D_CONV
    y_rnn, y_conv = _mixers(xr, xr_c, gr, rnn_conv_w[0], rnn_conv_b[0], wg, bgate, rg_lambda[0], u, bg,
                            sc_conv_w[0])

    wr = jnp.zeros((d, LANES), F32)
    wr = wr.at[:, :N_GROUPS].set(router_group_w[0]).at[:, EXPERT_LANE0:EXPERT_LANE0 + N_EXPERTS].set(router_exp_w[0])
    br = jnp.zeros((1, LANES), F32)
    br = br.at[0, :N_GROUPS].set(router_group_b[0]).at[0, EXPERT_LANE0:EXPERT_LANE0 + N_EXPERTS].set(router_exp_b[0])
    x1, mt, route, cnt, route_t = _outproj(x, y_rnn, y_conv, w_out[0], mod3, norm2_g[0], wr.astype(BF16), br)

    n_assign = t_all * TOP_K
    n_blocks = (n_assign + N_EXPERTS * (MOE_BLK - 1) + MOE_BLK - 1) // MOE_BLK
    counts = cnt[0, EXPERT_LANE0:EXPERT_LANE0 + N_EXPERTS].astype(jnp.int32)
    pcounts = (counts + MOE_BLK - 1) // MOE_BLK * MOE_BLK
    pends = jnp.cumsum(pcounts)
    pstarts = pends - pcounts
    experts = route_t[0:TOP_K].astype(jnp.int32)
    ranks = route_t[4:4 + TOP_K].astype(jnp.int32)
    onehot = experts[None] == jnp.arange(N_EXPERTS, dtype=jnp.int32)[:, None, None]
    dest = (ranks + jnp.sum(jnp.where(onehot, pstarts[:, None, None], 0), axis=0)).reshape(n_assign)
    n_used = (pends[-1] // MOE_BLK).astype(jnp.int32)
    blk_start = jnp.arange(n_blocks, dtype=jnp.int32) * MOE_BLK
    block_e = jnp.minimum(jnp.sum(blk_start[:, None] >= pends[None, :], axis=1), N_EXPERTS - 1)
    last_e = jnp.max(jnp.where(counts > 0, jnp.arange(N_EXPERTS, dtype=jnp.int32), 0))
    block_e = jnp.where(blk_start < pends[-1], block_e, last_e).astype(jnp.int32)
    eids = jnp.arange(N_EXPERTS, dtype=jnp.int32)
    used = counts > 0
    slot_of_e = (jnp.cumsum(used.astype(jnp.int32)) - 1) % 2
    later = jnp.where(used[None, :] & (eids[None, :] > eids[:, None]), eids[None, :], N_EXPERTS)
    next_of_e = jnp.min(later, axis=1)
    next_of_e = jnp.where(next_of_e == N_EXPERTS, -1, next_of_e)
    is_e = block_e[:, None] == eids[None, :]
    weight_slot = jnp.sum(jnp.where(is_e, slot_of_e[None, :], 0), axis=1).astype(jnp.int32)
    next_e = jnp.sum(jnp.where(is_e, next_of_e[None, :], 0), axis=1).astype(jnp.int32)
    rows_end = jnp.sum(jnp.where(is_e, (pstarts + counts)[None, :], 0), axis=1)
    block_valid = jnp.clip(rows_end - blk_start, 0, MOE_BLK).astype(jnp.int32)

    n_slots = n_blocks * MOE_BLK
    slot_asg = _slotmap(dest, n_slots)
    yb = _experts(block_e, weight_slot, next_e, block_valid, n_used.reshape(1), slot_asg, mt, exp_w_gate[0],
                  exp_w_up[0], exp_w_down[0], n_blocks)
    out = _combine(dest, yb, x1.reshape(t_all, d), route, mod3, final_norm_g, seq)
    return out.reshape(bn, seq, d)
```

```python
import functools

import jax
import jax.numpy as jnp
from jax import lax
from jax.experimental import pallas as pl
from jax.experimental.pallas import tpu as pltpu

F32 = jnp.float32
BF16 = jnp.bfloat16

D_MODEL = 1024
D_RNN = 512
D_CONV = 512
D_CONV_H = D_CONV // 2
RNN_HEADS = 8
RNN_HEAD_DIM = D_RNN // RNN_HEADS
GRID_W = 64
RG_C = 8.0
N_GROUPS = 4
EXPERTS_PER_GROUP = 8
N_EXPERTS = N_GROUPS * EXPERTS_PER_GROUP
TOP_K = 2
D_EXPERT = 512
NORM_EPS = 1e-6
F32_TINY = 1.1754944e-38

LANES = 128
SUBLANES = 8
ROW_TILES = D_MODEL // LANES
PACK_TILES = ROW_TILES // 2
N_LANE_GROUPS = D_RNN // LANES
EXPERT_LANE0 = N_GROUPS

MOD_ROWS = 16
MOD_TN = 1536
INPROJ_TILE = 1024
OUTPROJ_TILE = 512
COEFF_ROWS = 2048
SCAN_UNROLL = 16
MOE_BLK = 512
MXU_TILE = 256
COMBINE_TILE = 256
DMA_UNROLL = 256
VMEM_LIMIT = 48 * 1024 * 1024
EXPERT_VMEM_LIMIT = 58 * 1024 * 1024


def _dot(a, b):
    return jnp.dot(a, b, preferred_element_type=F32)


def _split_bf16(x):
    hi = x.astype(BF16)
    lo = (x - hi.astype(F32)).astype(BF16)
    return hi, lo


def _mod_kernel(cc_ref, w_ref, b_ref, o_ref):
    s = cc_ref[...]
    s = s * jax.nn.sigmoid(s)
    s_hi, s_lo = _split_bf16(s)
    w_hi, w_lo = _split_bf16(w_ref[...])
    o_ref[...] = _dot(s_hi, w_hi) + _dot(s_lo, w_hi) + _dot(s_hi, w_lo) + b_ref[...]


def _modulation(cc, ada_w, ada_b):
    n = ada_w.shape[1]
    return pl.pallas_call(
        _mod_kernel,
        grid=(n // MOD_TN,),
        in_specs=[
            pl.BlockSpec((MOD_ROWS, D_MODEL), lambda j: (0, 0)),
            pl.BlockSpec((D_MODEL, MOD_TN), lambda j: (0, j)),
            pl.BlockSpec((1, MOD_TN), lambda j: (0, j)),
        ],
        out_specs=pl.BlockSpec((MOD_ROWS, MOD_TN), lambda j: (0, j)),
        out_shape=jax.ShapeDtypeStruct((MOD_ROWS, n), F32),
        compiler_params=pltpu.CompilerParams(vmem_limit_bytes=VMEM_LIMIT),
        name="mod",
    )(cc, ada_w, ada_b.reshape(1, n))


def _norm_mod(x, g, scale, shift):
    ms = jnp.mean(x * x, axis=-1, keepdims=True)
    y = x * lax.rsqrt(ms + NORM_EPS) * g
    return y * (1.0 + scale) + shift


def _inproj_kernel(x_ref, mod_ref, g_ref, w32_ref, *refs, latent):
    out_refs, w_ref = refs[:-1], refs[-1]

    @pl.when((pl.program_id(0) == 0) & (pl.program_id(1) == 0))
    def _():
        w_ref[...] = w32_ref[...].astype(BF16)

    h = _norm_mod(x_ref[0], g_ref[...], mod_ref[0, 1:2, :], mod_ref[0, 0:1, :])
    hb = h.astype(BF16)
    xr = _dot(hb, w_ref[:, 0:D_RNN])
    out_refs[0][0] = xr
    if latent:
        o = D_RNN
        out_refs[1][0] = jax.nn.gelu(_dot(hb, w_ref[:, o:o + D_RNN]), approximate=True)
        o += D_RNN
        v = _dot(hb, w_ref[:, o:o + D_CONV])
        out_refs[3][0] = _dot(hb, w_ref[:, o + D_CONV:o + 2 * D_CONV])
        cg = _dot(hb, w_ref[:, o + 2 * D_CONV:o + 3 * D_CONV])
        out_refs[2][0] = cg * v


def _inproj(x, mod3, mod_row, norm_g, w_in, latent):
    bn, n, d = x.shape
    tm = min(INPROJ_TILE, n)
    assert n % tm == 0
    n_out = 4 if latent else 1
    width = w_in.shape[1] if latent else D_RNN
    mod_map = (lambda b, i: (b, 0, 0)) if mod_row is None else (lambda b, i: (mod_row, 0, 0))
    return pl.pallas_call(
        functools.partial(_inproj_kernel, latent=latent),
        grid=(bn, n // tm),
        in_specs=[
            pl.BlockSpec((1, tm, d), lambda b, i: (b, i, 0)),
            pl.BlockSpec((1, 6, d), mod_map),
            pl.BlockSpec((1, d), lambda b, i: (0, 0)),
            pl.BlockSpec((d, width), lambda b, i: (0, 0), pipeline_mode=pl.Buffered(1)),
        ],
        out_specs=[pl.BlockSpec((1, tm, D_RNN), lambda b, i: (b, i, 0))] * n_out,
        out_shape=[jax.ShapeDtypeStruct((bn, n, D_RNN), F32)] * n_out,
        scratch_shapes=[pltpu.VMEM((d, width), BF16)],
        compiler_params=pltpu.CompilerParams(
            dimension_semantics=("arbitrary", "arbitrary"), vmem_limit_bytes=VMEM_LIMIT),
        name="inproj_lat" if latent else "inproj_ctx",
    )(x, mod3, norm_g.reshape(1, d), w_in)


def _shift_rows(x, k):
    n = x.shape[0]
    row = lax.broadcasted_iota(jnp.int32, x.shape, 0)
    rolled = pltpu.roll(x, k % n, axis=0)
    valid = (row >= k) if k > 0 else (row < n + k)
    return jnp.where(valid, rolled, 0.0)


def _scan_pitch(chunk):
    pitch = chunk + SUBLANES
    return pitch if (pitch // SUBLANES) % 2 else pitch + SUBLANES


def _rnn_kernel(xr_ref, xrc_ref, gr_ref, cw_ref, cb_ref, wg_ref, bg_ref, lam_ref, u_ref, bgc_ref, scw_ref,
                y_ref, yc_ref,
                xc_s, ap_f, bp_f, ap_b, bp_b, hl_f, al_f, hl_b, al_b, hp_f, hp_b, xpad, *, n_lat, n_ctx):
    _gconv_block(u_ref, bgc_ref, scw_ref, yc_ref)

    nl = -lam_ref[...]
    sp = jnp.maximum(nl, 0.0) + jnp.log1p(jnp.exp(-jnp.abs(nl)))
    c1 = (0.5 * RG_C) * sp
    cw = cw_ref[...]
    bias = cb_ref[...]
    wg = wg_ref[0]
    bg = bg_ref[0]
    dirs = ((ap_f, bp_f, hl_f, al_f, hp_f), (ap_b, bp_b, hl_b, al_b, hp_b))

    def conv_into(x, n):
        halo = jnp.zeros((SUBLANES, LANES), F32)
        xpad[pl.ds(0, SUBLANES), :] = halo
        xpad[pl.ds(SUBLANES, n), :] = x
        xpad[pl.ds(SUBLANES + n, SUBLANES), :] = halo
        xc_s[pl.ds(0, n), :] = (cw[0:1] * xpad[pl.ds(SUBLANES - 2, n), :]
                                + cw[1:2] * xpad[pl.ds(SUBLANES - 1, n), :] + cw[2:3] * x
                                + cw[3:4] * xpad[pl.ds(SUBLANES + 1, n), :]) + bias

    def coefficients(n):
        chunk = n // SUBLANES
        pitch = _scan_pitch(chunk)
        rows = max(chunk, min(n, COEFF_ROWS))
        per = rows // chunk

        def body(i, carry):
            xc = xc_s[pl.ds(pl.multiple_of(i * rows, SUBLANES), rows), :]
            gates = _dot(xc.astype(BF16), wg) + bg
            half_xc = 0.5 * xc
            for d in range(2):
                tr = jnp.tanh(gates[:, (2 * d) * LANES:(2 * d + 1) * LANES])
                ti = jnp.tanh(gates[:, (2 * d + 1) * LANES:(2 * d + 2) * LANES])
                neg_log_a = c1[d:d + 1] + c1[d:d + 1] * tr
                a = jnp.exp(-neg_log_a)
                y = jnp.tanh(neg_log_a) * (a * a + 1.0)
                b = (y * lax.rsqrt(jnp.maximum(y, F32_TINY))) * (half_xc + half_xc * ti)
                for k in range(per):
                    dst = pl.multiple_of((i * per + k) * pitch, SUBLANES)
                    dirs[d][0][pl.ds(dst, chunk), :] = a[k * chunk:(k + 1) * chunk]
                    dirs[d][1][pl.ds(dst, chunk), :] = b[k * chunk:(k + 1) * chunk]
            return carry

        if n == rows:
            body(0, 0)
        else:
            lax.fori_loop(0, n // rows, body, 0)

    def scan(n, h0_f, h0_b, keep):
        chunk = n // SUBLANES
        pitch = _scan_pitch(chunk)

        def steps(jo, carry):
            h_f, a_f, h_b, a_b = carry
            for u in range(SCAN_UNROLL):
                j = jo * SCAN_UNROLL + u
                av = ap_f[pl.ds(j, SUBLANES, stride=pitch), :]
                h_f = av * h_f + bp_f[pl.ds(j, SUBLANES, stride=pitch), :]
                a_f = av * a_f
                jb = chunk - 1 - j
                av = ap_b[pl.ds(jb, SUBLANES, stride=pitch), :]
                h_b = av * h_b + bp_b[pl.ds(jb, SUBLANES, stride=pitch), :]
                a_b = av * a_b
                if keep:
                    o = pl.multiple_of(j * SUBLANES, SUBLANES)
                    hl_f[pl.ds(o, SUBLANES), :] = h_f
                    al_f[pl.ds(o, SUBLANES), :] = a_f
                    hl_b[pl.ds(o, SUBLANES), :] = h_b
                    al_b[pl.ds(o, SUBLANES), :] = a_b
            return h_f, a_f, h_b, a_b

        zeros = jnp.zeros((SUBLANES, LANES), F32)
        ones = jnp.ones((SUBLANES, LANES), F32)
        h_f, a_f, h_b, a_b = lax.fori_loop(0, chunk // SCAN_UNROLL, steps, (zeros, ones, zeros, ones))

        in_f = [h0_f]
        for c in range(SUBLANES):
            in_f.append(a_f[c:c + 1] * in_f[c] + h_f[c:c + 1])
        in_b = [h0_b]
        for c in range(SUBLANES - 1, -1, -1):
            in_b.append(a_b[c:c + 1] * in_b[-1] + h_b[c:c + 1])
        if keep:
            hin_f = jnp.concatenate(in_f[:SUBLANES], axis=0)
            hin_b = jnp.concatenate(in_b[SUBLANES - 1::-1], axis=0)

            def fix(jo, carry):
                for u in range(SCAN_UNROLL):
                    j = jo * SCAN_UNROLL + u
                    o = pl.multiple_of(j * SUBLANES, SUBLANES)
                    hp_f[pl.ds(j, SUBLANES, stride=pitch), :] = (
                        hl_f[pl.ds(o, SUBLANES), :] + al_f[pl.ds(o, SUBLANES), :] * hin_f)
                    hp_b[pl.ds(chunk - 1 - j, SUBLANES, stride=pitch), :] = (
                        hl_b[pl.ds(o, SUBLANES), :] + al_b[pl.ds(o, SUBLANES), :] * hin_b)
                return carry
            lax.fori_loop(0, chunk // SCAN_UNROLL, fix, 0)
        return in_f[SUBLANES], in_b[SUBLANES]

    zero = jnp.zeros((1, LANES), F32)
    conv_into(xrc_ref[0], n_ctx)
    coefficients(n_ctx)
    h0_f, h0_b = scan(n_ctx, zero, zero, keep=False)

    conv_into(xr_ref[0], n_lat)
    coefficients(n_lat)
    scan(n_lat, h0_f, h0_b, keep=True)

    chunk = n_lat // SUBLANES
    pitch = _scan_pitch(chunk)

    def emit(c, carry):
        src = pl.multiple_of(c * chunk, 2 * SUBLANES)
        dst = pl.multiple_of(c * pitch, SUBLANES)
        hsum = hp_f[pl.ds(dst, chunk), :] + hp_b[pl.ds(dst, chunk), :]
        y = gr_ref[0, pl.ds(src, chunk), :] * hsum
        y_ref[0, pl.ds(src, chunk), :] = y.astype(y_ref.dtype)
        return carry
    lax.fori_loop(0, SUBLANES, emit, 0)


def _mixers(xr, xr_c, gr, conv_w, conv_b, wg, bgate, lam, u, bg, sc_w):
    bn, n, _ = xr.shape
    n_ctx = xr_c.shape[1]
    assert n % (SUBLANES * SCAN_UNROLL) == 0 and n_ctx % (SUBLANES * SCAN_UNROLL) == 0 and n_ctx <= n
    pitched = SUBLANES * _scan_pitch(n // SUBLANES)
    seq_spec = pl.BlockSpec((1, n, LANES), lambda b, p: (b, 0, p))
    return pl.pallas_call(
        functools.partial(_rnn_kernel, n_lat=n, n_ctx=n_ctx),
        grid=(bn, N_LANE_GROUPS),
        in_specs=[
            seq_spec,
            pl.BlockSpec((1, n_ctx, LANES), lambda b, p: (b, 0, p)),
            seq_spec,
            pl.BlockSpec((4, LANES), lambda b, p: (0, p)),
            pl.BlockSpec((1, LANES), lambda b, p: (0, p)),
            pl.BlockSpec((1, LANES, 4 * LANES), lambda b, p: (p, 0, 0)),
            pl.BlockSpec((1, 1, 4 * LANES), lambda b, p: (p, 0, 0)),
            pl.BlockSpec((2, LANES), lambda b, p: (0, p)),
            seq_spec,
            seq_spec,
            pl.BlockSpec((3, LANES), lambda b, p: (0, p)),
        ],
        out_specs=[seq_spec, seq_spec],
        out_shape=[jax.ShapeDtypeStruct((bn, n, D_RNN), BF16)] * 2,
        scratch_shapes=[pltpu.VMEM((n, LANES), F32)]
        + [pltpu.VMEM((pitched, LANES), F32)] * 4
        + [pltpu.VMEM((n, LANES), F32)] * 4
        + [pltpu.VMEM((pitched, LANES), F32)] * 2
        + [pltpu.VMEM((n + 2 * SUBLANES, LANES), F32)],
        compiler_params=pltpu.CompilerParams(vmem_limit_bytes=VMEM_LIMIT),
        name="mixers",
    )(xr, xr_c, gr, conv_w, conv_b.reshape(1, D_RNN), wg, bgate, lam, u, bg, sc_w)


def _gate_weights(rg_wa, rg_ba, rg_wx, rg_bx):
    eye = jnp.eye(2, dtype=F32)
    blocks, biases = [], []
    for d in range(2):
        for w, bvec in ((rg_wa[d], rg_ba[d]), (rg_wx[d], rg_bx[d])):
            w4 = w.reshape(N_LANE_GROUPS, 2, RNN_HEAD_DIM, RNN_HEAD_DIM)
            bd = jnp.einsum("paij,ac->paicj", w4, eye).reshape(N_LANE_GROUPS, LANES, LANES)
            blocks.append(0.5 * bd)
            biases.append(0.5 * bvec.reshape(N_LANE_GROUPS, 1, LANES))
    return jnp.concatenate(blocks, axis=-1).astype(BF16), jnp.concatenate(biases, axis=-1)


def _gconv_block(u_ref, bg_ref, w_ref, y_ref):
    p = pl.program_id(1)
    u = u_ref[0]
    w = w_ref[...]

    @pl.when(p < D_CONV_H // LANES)
    def _():
        col = lax.broadcasted_iota(jnp.int32, u.shape, 0) % GRID_W
        left = jnp.where(col > 0, _shift_rows(u, 1), 0.0)
        right = jnp.where(col < GRID_W - 1, _shift_rows(u, -1), 0.0)
        y_ref[0] = (bg_ref[0] * (w[0:1] * left + w[1:2] * u + w[2:3] * right)).astype(y_ref.dtype)

    @pl.when(p >= D_CONV_H // LANES)
    def _():
        y_ref[0] = (bg_ref[0] * (w[0:1] * _shift_rows(u, GRID_W) + w[1:2] * u
                                 + w[2:3] * _shift_rows(u, -GRID_W))).astype(y_ref.dtype)


def _lane_max(x, mask):
    return jnp.max(jnp.where(mask, x, -jnp.inf), axis=-1, keepdims=True)


def _first_lane(cond, lane):
    return jnp.min(jnp.where(cond, lane, float(LANES)), axis=-1, keepdims=True)


def _outproj_kernel(x_ref, yr_ref, yc_ref, w32_ref, mod_ref, g_ref, wr_ref, br_ref,
                    x1_ref, mt_ref, route_ref, cnt_ref, rt_ref, carry, w_ref, m_s, *, tm):
    s = pl.program_id(0)

    @pl.when(s == 0)
    def _():
        carry[...] = jnp.zeros_like(carry)
        w_ref[...] = w32_ref[...].astype(BF16)
        m_s[...] = jnp.zeros_like(m_s)

    logits = _dot(m_s[...], wr_ref[...]) + br_ref[...]
    lane_i = lax.broadcasted_iota(jnp.int32, logits.shape, 1)
    lane = lane_i.astype(F32)
    is_grp = lane_i < N_GROUPS
    g_max = _lane_max(logits, is_grp)
    grp = _first_lane(is_grp & (logits == g_max), lane)
    p_g = 1.0 / jnp.sum(jnp.where(is_grp, jnp.exp(logits - g_max), 0.0), axis=-1, keepdims=True)
    lo_lane = EXPERT_LANE0 + grp * EXPERTS_PER_GROUP
    in_grp = (lane >= lo_lane) & (lane < lo_lane + EXPERTS_PER_GROUP)
    l1 = _lane_max(logits, in_grp)
    i1 = _first_lane(in_grp & (logits == l1), lane)
    rest = in_grp & (lane != i1)
    l2 = _lane_max(logits, rest)
    i2 = _first_lane(rest & (logits == l2), lane)
    r21 = jnp.exp(l2 - l1)
    gate1 = p_g / (1.0 + r21)
    gate2 = gate1 * r21

    oh1 = jnp.where(lane == i1, 1.0, 0.0)
    oh2 = jnp.where(lane == i2, 1.0, 0.0)
    both = (oh1 + oh2).astype(BF16)
    ti = lax.broadcasted_iota(jnp.int32, (tm, tm), 0)
    tj = lax.broadcasted_iota(jnp.int32, (tm, tm), 1)
    tri = jnp.where(tj < ti, 1.0, 0.0).astype(BF16)
    counts = carry[...]
    before = _dot(tri, both) + counts
    rank1 = jnp.sum(oh1 * before, axis=-1, keepdims=True)
    rank2 = jnp.sum(oh2 * before, axis=-1, keepdims=True)
    out = jnp.zeros(logits.shape, F32)
    for k, val in enumerate((i1 - EXPERT_LANE0, i2 - EXPERT_LANE0, gate1, gate2, rank1, rank2)):
        out = jnp.where(lane_i == k, val, out)
    route_ref[...] = out
    rt_ref[...] = out.T[0:SUBLANES, :]
    real = jnp.where(s > 0, 1.0, 0.0)
    total = counts + real * jnp.sum(oh1 + oh2, axis=0, keepdims=True)
    carry[...] = total
    cnt_ref[...] = total

    mix = _dot(yr_ref[0], w_ref[0:D_RNN, :]) + _dot(yc_ref[0], w_ref[D_RNN:, :])
    x1 = x_ref[0] + mod_ref[0, 2:3, :] * mix
    x1_ref[0] = x1
    m_new = _norm_mod(x1, g_ref[...], mod_ref[0, 4:5, :], mod_ref[0, 3:4, :])
    m_s[...] = m_new.astype(BF16)
    half = D_MODEL // 2
    packed = pltpu.pack_elementwise([m_new[:, :half], m_new[:, half:]], packed_dtype=BF16)
    for q in range(PACK_TILES):
        mt_ref[pl.ds(q, tm, stride=PACK_TILES), :] = packed[:, q * LANES:(q + 1) * LANES]


def _outproj(x, y_rnn, y_conv, w_out, mod3, norm_g, wr, br):
    bn, n, d = x.shape
    tm = min(OUTPROJ_TILE, n)
    nt = n // tm
    n_tiles = bn * nt
    t_all = bn * n

    def cur(s):
        return jnp.minimum(s, n_tiles - 1)

    def prev(s):
        return jnp.maximum(s - 1, 0)

    def seq_map(s):
        return (cur(s) // nt, cur(s) % nt, 0)

    const = lambda s: (0, 0)
    return pl.pallas_call(
        functools.partial(_outproj_kernel, tm=tm),
        grid=(n_tiles + 1,),
        in_specs=[
            pl.BlockSpec((1, tm, d), seq_map),
            pl.BlockSpec((1, tm, D_RNN), seq_map),
            pl.BlockSpec((1, tm, D_CONV), seq_map),
            pl.BlockSpec((D_RNN + D_CONV, d), const, pipeline_mode=pl.Buffered(1)),
            pl.BlockSpec((1, 6, d), lambda s: (cur(s) // nt, 0, 0)),
            pl.BlockSpec((1, d), const),
            pl.BlockSpec((d, LANES), const),
            pl.BlockSpec((1, LANES), const),
        ],
        out_specs=[
            pl.BlockSpec((1, tm, d), seq_map),
            pl.BlockSpec((tm * PACK_TILES, LANES), lambda s: (cur(s), 0)),
            pl.BlockSpec((tm, LANES), lambda s: (prev(s), 0)),
            pl.BlockSpec((1, LANES), const),
            pl.BlockSpec((SUBLANES, tm), lambda s: (0, prev(s))),
        ],
        out_shape=[
            jax.ShapeDtypeStruct((bn, n, d), F32),
            jax.ShapeDtypeStruct((t_all * PACK_TILES, LANES), jnp.uint32),
            jax.ShapeDtypeStruct((t_all, LANES), F32),
            jax.ShapeDtypeStruct((1, LANES), F32),
            jax.ShapeDtypeStruct((SUBLANES, t_all), F32),
        ],
        scratch_shapes=[pltpu.VMEM((1, LANES), F32), pltpu.VMEM((D_RNN + D_CONV, d), BF16),
                        pltpu.VMEM((tm, d), BF16)],
        compiler_params=pltpu.CompilerParams(
            dimension_semantics=("arbitrary",), vmem_limit_bytes=VMEM_LIMIT),
        name="outproj",
    )(x, y_rnn, y_conv, w_out, mod3, norm_g.reshape(1, d), wr, br)


def _row_tile(ref, row):
    return ref.at[pl.ds(pl.multiple_of(row * ROW_TILES, ROW_TILES), ROW_TILES)]


def _slotmap_kernel(dest_ref, zeros_hbm, asg_ref, sem):
    fill = pltpu.make_async_copy(zeros_hbm, asg_ref, sem)
    fill.start()
    fill.wait()

    def body(c, carry):
        for u in range(DMA_UNROLL):
            a = c * DMA_UNROLL + u
            asg_ref[dest_ref[a]] = a
        return carry
    lax.fori_loop(0, dest_ref.shape[0] // DMA_UNROLL, body, 0)


def _slotmap(dest, n_slots):
    return pl.pallas_call(
        _slotmap_kernel,
        in_specs=[pl.BlockSpec(memory_space=pltpu.SMEM), pl.BlockSpec(memory_space=pl.ANY)],
        out_specs=pl.BlockSpec(memory_space=pltpu.SMEM),
        out_shape=jax.ShapeDtypeStruct((n_slots,), jnp.int32),
        scratch_shapes=[pltpu.SemaphoreType.DMA],
        name="slotmap",
    )(dest, jnp.zeros((n_slots,), jnp.int32))


def _expert_kernel(be_ref, ws_ref, ne_ref, bv_ref, nu_ref, asg_hbm, m_ref, wg_hbm, wu_hbm, wd_hbm, yb_ref,
                   xbuf_a, xbuf_b, idx, isems, wbuf_g, wbuf_u, wbuf_d, wsems, wg_s, wu_s, wd_s):
    j = pl.program_id(0)
    n_used = nu_ref[0]
    last = n_used - 1
    n_tok = m_ref.shape[0] // PACK_TILES

    def idx_copy(blk, sl):
        return pltpu.make_async_copy(asg_hbm.at[blk], idx.at[sl], isems.at[sl])

    def copy_rows(buf, sl, r0, n):
        for u in range(n):
            asg = idx[sl, 0, r0 + u]
            tok = jnp.where(asg >= n_tok, asg - n_tok, asg)
            src = pl.multiple_of(tok * PACK_TILES, PACK_TILES)
            buf[pl.ds((r0 + u) * PACK_TILES, PACK_TILES), :] = m_ref[pl.ds(src, PACK_TILES), :]

    def unpack(buf, rows):
        halves = ([], [])
        for s in range(PACK_TILES):
            word = buf[pl.ds(s, rows, stride=PACK_TILES), :]
            for k in range(2):
                part = pltpu.unpack_elementwise(word, index=k, packed_dtype=BF16, unpacked_dtype=F32)
                halves[k].append(part.astype(BF16))
        return jnp.concatenate(halves[0] + halves[1], axis=-1)

    n_pieces = 2 * D_EXPERT // MXU_TILE + D_MODEL // MXU_TILE
    bounds = [(p * MOE_BLK) // n_pieces for p in range(n_pieces + 1)]

    def compute(cur, nxt, nxt_sl, rows):
        pieces = iter(zip(bounds[:-1], bounds[1:]))

        def dot_pieces(a, w_ref, n0):
            acc = _dot(a, w_ref[:, n0:n0 + MXU_TILE])
            r0, r1 = next(pieces)
            copy_rows(nxt, nxt_sl, r0, r1 - r0)
            return acc

        xb16 = unpack(cur, rows)
        acts = []
        for n0 in range(0, D_EXPERT, MXU_TILE):
            gate = dot_pieces(xb16, wg_s, n0)
            up = dot_pieces(xb16, wu_s, n0)
            acts.append(((gate * jax.nn.sigmoid(gate)) * up).astype(BF16))
        h = jnp.concatenate(acts, axis=-1)
        for n0 in range(0, D_MODEL, MXU_TILE):
            y = dot_pieces(h, wd_s, n0)
            for s in range(MXU_TILE // LANES):
                yb_ref[pl.ds(n0 // LANES + s, rows, stride=ROW_TILES), :] = (
                    y[:, s * LANES:(s + 1) * LANES])
        if rows < MOE_BLK:
            yb_ref[pl.ds(rows * ROW_TILES, (MOE_BLK - rows) * ROW_TILES), :] = jnp.zeros(
                ((MOE_BLK - rows) * ROW_TILES, LANES), F32)

    @pl.when(j >= n_used)
    def _():
        yb_ref[...] = jnp.zeros_like(yb_ref)

    @pl.when(j < n_used)
    def _():
        slot = j % 2
        other = 1 - slot

        @pl.when(j == 0)
        def _():
            idx_copy(0, 0).start()
            idx_copy(0, 0).wait()

            def body(c, carry):
                copy_rows(xbuf_a, 0, c * DMA_UNROLL, DMA_UNROLL)
                return carry
            lax.fori_loop(0, MOE_BLK // DMA_UNROLL, body, 0)
            idx_copy(jnp.minimum(1, last), 1).start()

        e = be_ref[j]
        wslot = ws_ref[j]

        def weight_copies(expert, sl):
            return [pltpu.make_async_copy(src.at[expert], dst.at[sl], wsems.at[sl])
                    for src, dst in ((wg_hbm, wbuf_g), (wu_hbm, wbuf_u), (wd_hbm, wbuf_d))]

        @pl.when(j == 0)
        def _():
            for cp in weight_copies(e, wslot):
                cp.start()

        @pl.when((j == 0) | (e != be_ref[jnp.maximum(j - 1, 0)]))
        def _():
            for cp in weight_copies(e, wslot):
                cp.wait()
            wg_s[...] = wbuf_g[wslot].astype(BF16)
            wu_s[...] = wbuf_u[wslot].astype(BF16)
            wd_s[...] = wbuf_d[wslot].astype(BF16)

            @pl.when(ne_ref[j] >= 0)
            def _():
                for cp in weight_copies(ne_ref[j], 1 - wslot):
                    cp.start()

        idx_copy(0, other).wait()

        short = bv_ref[j] <= MOE_BLK // 2
        for par, (cur, nxt) in enumerate(((xbuf_a, xbuf_b), (xbuf_b, xbuf_a))):
            @pl.when((slot == par) & jnp.logical_not(short))
            def _(cur=cur, nxt=nxt, par=par):
                compute(cur, nxt, 1 - par, MOE_BLK)

            @pl.when((slot == par) & short)
            def _(cur=cur, nxt=nxt, par=par):
                compute(cur, nxt, 1 - par, MOE_BLK // 2)

        @pl.when(j < last)
        def _():
            idx_copy(jnp.minimum(j + 2, last), slot).start()


def _experts(block_e, weight_slot, next_e, block_valid, n_used, slot_asg, mt, w_gate, w_up, w_down, n_blocks):
    return pl.pallas_call(
        _expert_kernel,
        grid_spec=pltpu.PrefetchScalarGridSpec(
            num_scalar_prefetch=5,
            grid=(n_blocks,),
            in_specs=[
                pl.BlockSpec(memory_space=pl.ANY),
                pl.BlockSpec(memory_space=pltpu.VMEM),
                pl.BlockSpec(memory_space=pl.ANY),
                pl.BlockSpec(memory_space=pl.ANY),
                pl.BlockSpec(memory_space=pl.ANY),
            ],
            out_specs=pl.BlockSpec((MOE_BLK * ROW_TILES, LANES), lambda j, *_: (j, 0)),
            scratch_shapes=[
                pltpu.VMEM((MOE_BLK * PACK_TILES, LANES), jnp.uint32),
                pltpu.VMEM((MOE_BLK * PACK_TILES, LANES), jnp.uint32),
                pltpu.SMEM((2, 1, MOE_BLK), jnp.int32),
                pltpu.SemaphoreType.DMA((2,)),
                pltpu.VMEM((2, D_MODEL, D_EXPERT), F32),
                pltpu.VMEM((2, D_MODEL, D_EXPERT), F32),
                pltpu.VMEM((2, D_EXPERT, D_MODEL), F32),
                pltpu.SemaphoreType.DMA((2,)),
                pltpu.VMEM((D_MODEL, D_EXPERT), BF16),
                pltpu.VMEM((D_MODEL, D_EXPERT), BF16),
                pltpu.VMEM((D_EXPERT, D_MODEL), BF16),
            ],
        ),
        out_shape=jax.ShapeDtypeStruct((n_blocks * MOE_BLK * ROW_TILES, LANES), F32),
        compiler_params=pltpu.CompilerParams(
            dimension_semantics=("arbitrary",), vmem_limit_bytes=EXPERT_VMEM_LIMIT),
        name="expert",
    )(block_e, weight_slot, next_e, block_valid, n_used, slot_asg.reshape(n_blocks, 1, MOE_BLK), mt, w_gate,
      w_up, w_down)


def _combine_kernel(dest_ref, yb_hbm, x1_ref, route_ref, mod_ref, g_ref, o_ref, ybuf, sems, *, tc, n_tok):
    i = pl.program_id(0)
    slot = i % 2

    def row_copy(d, sl, k, r):
        return pltpu.make_async_copy(_row_tile(yb_hbm, d), _row_tile(ybuf.at[sl, k], r), sems.at[sl])

    def gather(step, sl):
        def issue(c, carry):
            for u in range(DMA_UNROLL):
                r = c * DMA_UNROLL + u
                for k in range(TOP_K):
                    row_copy(dest_ref[k * n_tok + step * tc + r], sl, k, r).start(priority=k)
            return carry
        lax.fori_loop(0, tc // DMA_UNROLL, issue, 0)

    @pl.when(i == 0)
    def _():
        gather(0, 0)

    @pl.when(i + 1 < pl.num_programs(0))
    def _():
        gather(i + 1, 1 - slot)

    def drain(c, carry):
        for u in range(DMA_UNROLL * TOP_K):
            row_copy(0, slot, 0, 0).wait()
        return carry
    lax.fori_loop(0, tc // DMA_UNROLL, drain, 0)

    def rows(k):
        return jnp.concatenate(
            [ybuf[slot, k, pl.ds(s, tc, stride=ROW_TILES), :] for s in range(ROW_TILES)], axis=-1)

    route = route_ref[...]
    y = route[:, 2:3] * rows(0) + route[:, 3:4] * rows(1)
    x2 = x1_ref[...] + mod_ref[0, 5:6, :] * y
    ms = jnp.mean(x2 * x2, axis=-1, keepdims=True)
    o_ref[...] = x2 * lax.rsqrt(ms + NORM_EPS) * g_ref[...]


def _combine(dest, yb, x1_2d, route, mod3, final_g, seq):
    t_all, d = x1_2d.shape
    tc = min(COMBINE_TILE, seq)
    per_seq = seq // tc
    return pl.pallas_call(
        functools.partial(_combine_kernel, tc=tc, n_tok=t_all),
        grid_spec=pltpu.PrefetchScalarGridSpec(
            num_scalar_prefetch=1,
            grid=(t_all // tc,),
            in_specs=[
                pl.BlockSpec(memory_space=pl.ANY),
                pl.BlockSpec((tc, d), lambda i, dest: (i, 0)),
                pl.BlockSpec((tc, LANES), lambda i, dest: (i, 0)),
                pl.BlockSpec((1, 6, d), lambda i, dest: (i // per_seq, 0, 0)),
                pl.BlockSpec((1, d), lambda i, dest: (0, 0)),
            ],
            out_specs=pl.BlockSpec((tc, d), lambda i, dest: (i, 0)),
            scratch_shapes=[
                pltpu.VMEM((2, TOP_K, tc * ROW_TILES, LANES), F32),
                pltpu.SemaphoreType.DMA((2,)),
            ],
        ),
        out_shape=jax.ShapeDtypeStruct((t_all, d), F32),
        compiler_params=pltpu.CompilerParams(
            dimension_semantics=("arbitrary",), vmem_limit_bytes=VMEM_LIMIT),
        name="combine",
    )(dest, yb, x1_2d, route, mod3, final_g.reshape(1, d))


def kernel(x, c, ctx, c_ctx, ada_w, ada_b, norm1_g, norm2_g, w_in, rnn_conv_w, rnn_conv_b, rg_wa, rg_ba,
           rg_wx, rg_bx, rg_lambda, sc_conv_w, w_out, router_group_w, router_group_b, router_exp_w,
           router_exp_b, exp_w_gate, exp_w_up, exp_w_down, final_norm_g):
    bn, seq, d = x.shape
    assert d == D_MODEL and bn < MOD_ROWS and ada_w.shape[0] == 1
    t_all = bn * seq

    cc = jnp.concatenate([c, c_ctx[None], jnp.zeros((MOD_ROWS - bn - 1, d), F32)], axis=0)
    mod3 = _modulation(cc, ada_w[0], ada_b[0]).reshape(MOD_ROWS, 6, d)

    xr, gr, u, bg = _inproj(x, mod3, None, norm1_g[0], w_in[0], latent=True)
    ctx_len = ctx.shape[1]
    (xr_c,) = _inproj(ctx.reshape(1, bn * ctx_len, d), mod3, bn, norm1_g[0], w_in[0], latent=False)
    xr_c = xr_c.reshape(bn, ctx_len, D_RNN)

    wg, bgate = _gate_weights(rg_wa[0], rg_ba[0], rg_wx[0], rg_bx[0])
    assert D_RNN == D_CONV
    y_rnn, y_conv = _mixers(xr, xr_c, gr, rnn_conv_w[0], rnn_conv_b[0], wg, bgate, rg_lambda[0], u, bg,
                            sc_conv_w[0])

    wr = jnp.zeros((d, LANES), F32)
    wr = wr.at[:, :N_GROUPS].set(router_group_w[0]).at[:, EXPERT_LANE0:EXPERT_LANE0 + N_EXPERTS].set(router_exp_w[0])
    br = jnp.zeros((1, LANES), F32)
    br = br.at[0, :N_GROUPS].set(router_group_b[0]).at[0, EXPERT_LANE0:EXPERT_LANE0 + N_EXPERTS].set(router_exp_b[0])
    x1, mt, route, cnt, route_t = _outproj(x, y_rnn, y_conv, w_out[0], mod3, norm2_g[0], wr.astype(BF16), br)

    n_assign = t_all * TOP_K
    n_blocks = (n_assign + N_EXPERTS * (MOE_BLK - 1) + MOE_BLK - 1) // MOE_BLK
    counts = cnt[0, EXPERT_LANE0:EXPERT_LANE0 + N_EXPERTS].astype(jnp.int32)
    pcounts = (counts + MOE_BLK - 1) // MOE_BLK * MOE_BLK
    pends = jnp.cumsum(pcounts)
    pstarts = pends - pcounts
    experts = route_t[0:TOP_K].astype(jnp.int32)
    ranks = route_t[4:4 + TOP_K].astype(jnp.int32)
    onehot = experts[None] == jnp.arange(N_EXPERTS, dtype=jnp.int32)[:, None, None]
    dest = (ranks + jnp.sum(jnp.where(onehot, pstarts[:, None, None], 0), axis=0)).reshape(n_assign)
    n_used = (pends[-1] // MOE_BLK).astype(jnp.int32)
    blk_start = jnp.arange(n_blocks, dtype=jnp.int32) * MOE_BLK
    block_e = jnp.minimum(jnp.sum(blk_start[:, None] >= pends[None, :], axis=1), N_EXPERTS - 1)
    last_e = jnp.max(jnp.where(counts > 0, jnp.arange(N_EXPERTS, dtype=jnp.int32), 0))
    block_e = jnp.where(blk_start < pends[-1], block_e, last_e).astype(jnp.int32)
    eids = jnp.arange(N_EXPERTS, dtype=jnp.int32)
    used = counts > 0
    slot_of_e = (jnp.cumsum(used.astype(jnp.int32)) - 1) % 2
    later = jnp.where(used[None, :] & (eids[None, :] > eids[:, None]), eids[None, :], N_EXPERTS)
    next_of_e = jnp.min(later, axis=1)
    next_of_e = jnp.where(next_of_e == N_EXPERTS, -1, next_of_e)
    is_e = block_e[:, None] == eids[None, :]
    weight_slot = jnp.sum(jnp.where(is_e, slot_of_e[None, :], 0), axis=1).astype(jnp.int32)
    next_e = jnp.sum(jnp.where(is_e, next_of_e[None, :], 0), axis=1).astype(jnp.int32)
    rows_end = jnp.sum(jnp.where(is_e, (pstarts + counts)[None, :], 0), axis=1)
    block_valid = jnp.clip(rows_end - blk_start, 0, MOE_BLK).astype(jnp.int32)

    n_slots = n_blocks * MOE_BLK
    slot_asg = _slotmap(dest, n_slots)
    yb = _experts(block_e, weight_slot, next_e, block_valid, n_used.reshape(1), slot_asg, mt, exp_w_gate[0],
                  exp_w_up[0], exp_w_down[0], n_blocks)
    out = _combine(dest, yb, x1.reshape(t_all, d), route, mod3, final_norm_g, seq)
    return out.reshape(bn, seq, d)
```

```python
import functools

import jax
import jax.numpy as jnp
from jax import lax
from jax.experimental import pallas as pl
from jax.experimental.pallas import tpu as pltpu

F32 = jnp.float32
BF16 = jnp.bfloat16

D_MODEL = 1024
D_RNN = 512
D_CONV = 512
D_CONV_H = D_CONV // 2
RNN_HEADS = 8
RNN_HEAD_DIM = D_RNN // RNN_HEADS
GRID_W = 64
RG_C = 8.0
N_GROUPS = 4
EXPERTS_PER_GROUP = 8
N_EXPERTS = N_GROUPS * EXPERTS_PER_GROUP
TOP_K = 2
D_EXPERT = 512
NORM_EPS = 1e-6
F32_TINY = 1.1754944e-38

LANES = 128
SUBLANES = 8
ROW_TILES = D_MODEL // LANES
PACK_TILES = ROW_TILES // 2
N_LANE_GROUPS = D_RNN // LANES
EXPERT_LANE0 = N_GROUPS

MOD_ROWS = 16
MOD_TN = 1536
INPROJ_TILE = 1024
OUTPROJ_TILE = 512
COEFF_ROWS = 2048
SCAN_UNROLL = 64
MOE_BLK = 512
MXU_TILE = 256
COMBINE_TILE = 256
DMA_UNROLL = 256
VMEM_LIMIT = 48 * 1024 * 1024
EXPERT_VMEM_LIMIT = 58 * 1024 * 1024


def _dot(a, b):
    return jnp.dot(a, b, preferred_element_type=F32)


def _split_bf16(x):
    hi = x.astype(BF16)
    lo = (x - hi.astype(F32)).astype(BF16)
    return hi, lo


def _mod_kernel(cc_ref, w_ref, b_ref, o_ref):
    s = cc_ref[...]
    s = s * jax.nn.sigmoid(s)
    s_hi, s_lo = _split_bf16(s)
    w_hi, w_lo = _split_bf16(w_ref[...])
    o_ref[...] = _dot(s_hi, w_hi) + _dot(s_lo, w_hi) + _dot(s_hi, w_lo) + b_ref[...]


def _modulation(cc, ada_w, ada_b):
    n = ada_w.shape[1]
    return pl.pallas_call(
        _mod_kernel,
        grid=(n // MOD_TN,),
        in_specs=[
            pl.BlockSpec((MOD_ROWS, D_MODEL), lambda j: (0, 0)),
            pl.BlockSpec((D_MODEL, MOD_TN), lambda j: (0, j)),
            pl.BlockSpec((1, MOD_TN), lambda j: (0, j)),
        ],
        out_specs=pl.BlockSpec((MOD_ROWS, MOD_TN), lambda j: (0, j)),
        out_shape=jax.ShapeDtypeStruct((MOD_ROWS, n), F32),
        compiler_params=pltpu.CompilerParams(vmem_limit_bytes=VMEM_LIMIT),
        name="mod",
    )(cc, ada_w, ada_b.reshape(1, n))


def _norm_mod(x, g, scale, shift):
    ms = jnp.mean(x * x, axis=-1, keepdims=True)
    y = x * lax.rsqrt(ms + NORM_EPS) * g
    return y * (1.0 + scale) + shift


def _inproj_kernel(x_ref, mod_ref, g_ref, w32_ref, *refs, latent):
    out_refs, w_ref = refs[:-1], refs[-1]

    @pl.when((pl.program_id(0) == 0) & (pl.program_id(1) == 0))
    def _():
        w_ref[...] = w32_ref[...].astype(BF16)

    h = _norm_mod(x_ref[0], g_ref[...], mod_ref[0, 1:2, :], mod_ref[0, 0:1, :])
    hb = h.astype(BF16)
    xr = _dot(hb, w_ref[:, 0:D_RNN])
    out_refs[0][0] = xr
    if latent:
        o = D_RNN
        out_refs[1][0] = jax.nn.gelu(_dot(hb, w_ref[:, o:o + D_RNN]), approximate=True)
        o += D_RNN
        v = _dot(hb, w_ref[:, o:o + D_CONV])
        out_refs[3][0] = _dot(hb, w_ref[:, o + D_CONV:o + 2 * D_CONV])
        cg = _dot(hb, w_ref[:, o + 2 * D_CONV:o + 3 * D_CONV])
        out_refs[2][0] = cg * v


def _inproj(x, mod3, mod_row, norm_g, w_in, latent):
    bn, n, d = x.shape
    tm = min(INPROJ_TILE, n)
    assert n % tm == 0
    n_out = 4 if latent else 1
    width = w_in.shape[1] if latent else D_RNN
    mod_map = (lambda b, i: (b, 0, 0)) if mod_row is None else (lambda b, i: (mod_row, 0, 0))
    return pl.pallas_call(
        functools.partial(_inproj_kernel, latent=latent),
        grid=(bn, n // tm),
        in_specs=[
            pl.BlockSpec((1, tm, d), lambda b, i: (b, i, 0)),
            pl.BlockSpec((1, 6, d), mod_map),
            pl.BlockSpec((1, d), lambda b, i: (0, 0)),
            pl.BlockSpec((d, width), lambda b, i: (0, 0), pipeline_mode=pl.Buffered(1)),
        ],
        out_specs=[pl.BlockSpec((1, tm, D_RNN), lambda b, i: (b, i, 0))] * n_out,
        out_shape=[jax.ShapeDtypeStruct((bn, n, D_RNN), F32)] * n_out,
        scratch_shapes=[pltpu.VMEM((d, width), BF16)],
        compiler_params=pltpu.CompilerParams(
            dimension_semantics=("arbitrary", "arbitrary"), vmem_limit_bytes=VMEM_LIMIT),
        name="inproj_lat" if latent else "inproj_ctx",
    )(x, mod3, norm_g.reshape(1, d), w_in)


def _shift_rows(x, k):
    n = x.shape[0]
    row = lax.broadcasted_iota(jnp.int32, x.shape, 0)
    rolled = pltpu.roll(x, k % n, axis=0)
    valid = (row >= k) if k > 0 else (row < n + k)
    return jnp.where(valid, rolled, 0.0)


def _scan_pitch(chunk):
    pitch = chunk + SUBLANES
    return pitch if (pitch // SUBLANES) % 2 else pitch + SUBLANES


def _rnn_kernel(xr_ref, xrc_ref, gr_ref, cw_ref, cb_ref, wg_ref, bg_ref, lam_ref, u_ref, bgc_ref, scw_ref,
                y_ref, yc_ref,
                xc_s, ap_f, bp_f, ap_b, bp_b, hl_f, al_f, hl_b, al_b, hp_f, hp_b, xpad, *, n_lat, n_ctx):
    _gconv_block(u_ref, bgc_ref, scw_ref, yc_ref)

    nl = -lam_ref[...]
    sp = jnp.maximum(nl, 0.0) + jnp.log1p(jnp.exp(-jnp.abs(nl)))
    c1 = (0.5 * RG_C) * sp
    cw = cw_ref[...]
    bias = cb_ref[...]
    wg = wg_ref[0]
    bg = bg_ref[0]
    dirs = ((ap_f, bp_f, hl_f, al_f, hp_f), (ap_b, bp_b, hl_b, al_b, hp_b))

    def conv_into(x, n):
        halo = jnp.zeros((SUBLANES, LANES), F32)
        xpad[pl.ds(0, SUBLANES), :] = halo
        xpad[pl.ds(SUBLANES, n), :] = x
        xpad[pl.ds(SUBLANES + n, SUBLANES), :] = halo
        xc_s[pl.ds(0, n), :] = (cw[0:1] * xpad[pl.ds(SUBLANES - 2, n), :]
                                + cw[1:2] * xpad[pl.ds(SUBLANES - 1, n), :] + cw[2:3] * x
                                + cw[3:4] * xpad[pl.ds(SUBLANES + 1, n), :]) + bias

    def coefficients(n):
        chunk = n // SUBLANES
        pitch = _scan_pitch(chunk)
        rows = max(chunk, min(n, COEFF_ROWS))
        per = rows // chunk

        def body(i, carry):
            xc = xc_s[pl.ds(pl.multiple_of(i * rows, SUBLANES), rows), :]
            gates = _dot(xc.astype(BF16), wg) + bg
            half_xc = 0.5 * xc
            for d in range(2):
                tr = jnp.tanh(gates[:, (2 * d) * LANES:(2 * d + 1) * LANES])
                ti = jnp.tanh(gates[:, (2 * d + 1) * LANES:(2 * d + 2) * LANES])
                neg_log_a = c1[d:d + 1] + c1[d:d + 1] * tr
                a = jnp.exp(-neg_log_a)
                y = jnp.tanh(neg_log_a) * (a * a + 1.0)
                b = (y * lax.rsqrt(jnp.maximum(y, F32_TINY))) * (half_xc + half_xc * ti)
                for k in range(per):
                    dst = pl.multiple_of((i * per + k) * pitch, SUBLANES)
                    dirs[d][0][pl.ds(dst, chunk), :] = a[k * chunk:(k + 1) * chunk]
                    dirs[d][1][pl.ds(dst, chunk), :] = b[k * chunk:(k + 1) * chunk]
            return carry

        if n == rows:
            body(0, 0)
        else:
            lax.fori_loop(0, n // rows, body, 0)

    def scan(n, h0_f, h0_b, keep):
        chunk = n // SUBLANES
        pitch = _scan_pitch(chunk)
        unroll = min(SCAN_UNROLL, chunk)
        assert chunk % unroll == 0

        def steps(jo, carry):
            h_f, a_f, h_b, a_b = carry
            for u in range(unroll):
                j = jo * unroll + u
                av = ap_f[pl.ds(j, SUBLANES, stride=pitch), :]
                h_f = av * h_f + bp_f[pl.ds(j, SUBLANES, stride=pitch), :]
                a_f = av * a_f
                jb = chunk - 1 - j
                av = ap_b[pl.ds(jb, SUBLANES, stride=pitch), :]
                h_b = av * h_b + bp_b[pl.ds(jb, SUBLANES, stride=pitch), :]
                a_b = av * a_b
                if keep:
                    o = pl.multiple_of(j * SUBLANES, SUBLANES)
                    hl_f[pl.ds(o, SUBLANES), :] = h_f
                    al_f[pl.ds(o, SUBLANES), :] = a_f
                    hl_b[pl.ds(o, SUBLANES), :] = h_b
                    al_b[pl.ds(o, SUBLANES), :] = a_b
            return h_f, a_f, h_b, a_b

        zeros = jnp.zeros((SUBLANES, LANES), F32)
        ones = jnp.ones((SUBLANES, LANES), F32)
        h_f, a_f, h_b, a_b = lax.fori_loop(0, chunk // unroll, steps, (zeros, ones, zeros, ones))

        in_f = [h0_f]
        for c in range(SUBLANES):
            in_f.append(a_f[c:c + 1] * in_f[c] + h_f[c:c + 1])
        in_b = [h0_b]
        for c in range(SUBLANES - 1, -1, -1):
            in_b.append(a_b[c:c + 1] * in_b[-1] + h_b[c:c + 1])
        if keep:
            hin_f = jnp.concatenate(in_f[:SUBLANES], axis=0)
            hin_b = jnp.concatenate(in_b[SUBLANES - 1::-1], axis=0)

            def fix(jo, carry):
                for u in range(unroll):
                    j = jo * unroll + u
                    o = pl.multiple_of(j * SUBLANES, SUBLANES)
                    hp_f[pl.ds(j, SUBLANES, stride=pitch), :] = (
                        hl_f[pl.ds(o, SUBLANES), :] + al_f[pl.ds(o, SUBLANES), :] * hin_f)
                    hp_b[pl.ds(chunk - 1 - j, SUBLANES, stride=pitch), :] = (
                        hl_b[pl.ds(o, SUBLANES), :] + al_b[pl.ds(o, SUBLANES), :] * hin_b)
                return carry
            lax.fori_loop(0, chunk // unroll, fix, 0)
        return in_f[SUBLANES], in_b[SUBLANES]

    zero = jnp.zeros((1, LANES), F32)
    conv_into(xrc_ref[0], n_ctx)
    coefficients(n_ctx)
    h0_f, h0_b = scan(n_ctx, zero, zero, keep=False)

    conv_into(xr_ref[0], n_lat)
    coefficients(n_lat)
    scan(n_lat, h0_f, h0_b, keep=True)

    chunk = n_lat // SUBLANES
    pitch = _scan_pitch(chunk)

    def emit(c, carry):
        src = pl.multiple_of(c * chunk, 2 * SUBLANES)
        dst = pl.multiple_of(c * pitch, SUBLANES)
        hsum = hp_f[pl.ds(dst, chunk), :] + hp_b[pl.ds(dst, chunk), :]
        y = gr_ref[0, pl.ds(src, chunk), :] * hsum
        y_ref[0, pl.ds(src, chunk), :] = y.astype(y_ref.dtype)
        return carry
    lax.fori_loop(0, SUBLANES, emit, 0)


def _mixers(xr, xr_c, gr, conv_w, conv_b, wg, bgate, lam, u, bg, sc_w):
    bn, n, _ = xr.shape
    n_ctx = xr_c.shape[1]
    assert n % (SUBLANES * SUBLANES) == 0 and n_ctx % (SUBLANES * SUBLANES) == 0 and n_ctx <= n
    pitched = SUBLANES * _scan_pitch(n // SUBLANES)
    seq_spec = pl.BlockSpec((1, n, LANES), lambda b, p: (b, 0, p))
    return pl.pallas_call(
        functools.partial(_rnn_kernel, n_lat=n, n_ctx=n_ctx),
        grid=(bn, N_LANE_GROUPS),
        in_specs=[
            seq_spec,
            pl.BlockSpec((1, n_ctx, LANES), lambda b, p: (b, 0, p)),
            seq_spec,
            pl.BlockSpec((4, LANES), lambda b, p: (0, p)),
            pl.BlockSpec((1, LANES), lambda b, p: (0, p)),
            pl.BlockSpec((1, LANES, 4 * LANES), lambda b, p: (p, 0, 0)),
            pl.BlockSpec((1, 1, 4 * LANES), lambda b, p: (p, 0, 0)),
            pl.BlockSpec((2, LANES), lambda b, p: (0, p)),
            seq_spec,
            seq_spec,
            pl.BlockSpec((3, LANES), lambda b, p: (0, p)),
        ],
        out_specs=[seq_spec, seq_spec],
        out_shape=[jax.ShapeDtypeStruct((bn, n, D_RNN), BF16)] * 2,
        scratch_shapes=[pltpu.VMEM((n, LANES), F32)]
        + [pltpu.VMEM((pitched, LANES), F32)] * 4
        + [pltpu.VMEM((n, LANES), F32)] * 4
        + [pltpu.VMEM((pitched, LANES), F32)] * 2
        + [pltpu.VMEM((n + 2 * SUBLANES, LANES), F32)],
        compiler_params=pltpu.CompilerParams(vmem_limit_bytes=VMEM_LIMIT),
        name="mixers",
    )(xr, xr_c, gr, conv_w, conv_b.reshape(1, D_RNN), wg, bgate, lam, u, bg, sc_w)


def _gate_weights(rg_wa, rg_ba, rg_wx, rg_bx):
    eye = jnp.eye(2, dtype=F32)
    blocks, biases = [], []
    for d in range(2):
        for w, bvec in ((rg_wa[d], rg_ba[d]), (rg_wx[d], rg_bx[d])):
            w4 = w.reshape(N_LANE_GROUPS, 2, RNN_HEAD_DIM, RNN_HEAD_DIM)
            bd = jnp.einsum("paij,ac->paicj", w4, eye).reshape(N_LANE_GROUPS, LANES, LANES)
            blocks.append(0.5 * bd)
            biases.append(0.5 * bvec.reshape(N_LANE_GROUPS, 1, LANES))
    return jnp.concatenate(blocks, axis=-1).astype(BF16), jnp.concatenate(biases, axis=-1)


def _gconv_block(u_ref, bg_ref, w_ref, y_ref):
    p = pl.program_id(1)
    u = u_ref[0]
    w = w_ref[...]

    @pl.when(p < D_CONV_H // LANES)
    def _():
        col = lax.broadcasted_iota(jnp.int32, u.shape, 0) % GRID_W
        left = jnp.where(col > 0, _shift_rows(u, 1), 0.0)
        right = jnp.where(col < GRID_W - 1, _shift_rows(u, -1), 0.0)
        y_ref[0] = (bg_ref[0] * (w[0:1] * left + w[1:2] * u + w[2:3] * right)).astype(y_ref.dtype)

    @pl.when(p >= D_CONV_H // LANES)
    def _():
        y_ref[0] = (bg_ref[0] * (w[0:1] * _shift_rows(u, GRID_W) + w[1:2] * u
                                 + w[2:3] * _shift_rows(u, -GRID_W))).astype(y_ref.dtype)


def _lane_max(x, mask):
    return jnp.max(jnp.where(mask, x, -jnp.inf), axis=-1, keepdims=True)


def _first_lane(cond, lane):
    return jnp.min(jnp.where(cond, lane, float(LANES)), axis=-1, keepdims=True)


def _outproj_kernel(x_ref, yr_ref, yc_ref, w32_ref, mod_ref, g_ref, wr_ref, br_ref,
                    x1_ref, mt_ref, route_ref, cnt_ref, rt_ref, carry, w_ref, m_s, *, tm):
    s = pl.program_id(0)

    @pl.when(s == 0)
    def _():
        carry[...] = jnp.zeros_like(carry)
        w_ref[...] = w32_ref[...].astype(BF16)
        m_s[...] = jnp.zeros_like(m_s)

    logits = _dot(m_s[...], wr_ref[...]) + br_ref[...]
    lane_i = lax.broadcasted_iota(jnp.int32, logits.shape, 1)
    lane = lane_i.astype(F32)
    is_grp = lane_i < N_GROUPS
    g_max = _lane_max(logits, is_grp)
    grp = _first_lane(is_grp & (logits == g_max), lane)
    p_g = 1.0 / jnp.sum(jnp.where(is_grp, jnp.exp(logits - g_max), 0.0), axis=-1, keepdims=True)
    lo_lane = EXPERT_LANE0 + grp * EXPERTS_PER_GROUP
    in_grp = (lane >= lo_lane) & (lane < lo_lane + EXPERTS_PER_GROUP)
    l1 = _lane_max(logits, in_grp)
    i1 = _first_lane(in_grp & (logits == l1), lane)
    rest = in_grp & (lane != i1)
    l2 = _lane_max(logits, rest)
    i2 = _first_lane(rest & (logits == l2), lane)
    r21 = jnp.exp(l2 - l1)
    gate1 = p_g / (1.0 + r21)
    gate2 = gate1 * r21

    oh1 = jnp.where(lane == i1, 1.0, 0.0)
    oh2 = jnp.where(lane == i2, 1.0, 0.0)
    both = (oh1 + oh2).astype(BF16)
    ti = lax.broadcasted_iota(jnp.int32, (tm, tm), 0)
    tj = lax.broadcasted_iota(jnp.int32, (tm, tm), 1)
    tri = jnp.where(tj < ti, 1.0, 0.0).astype(BF16)
    counts = carry[...]
    before = _dot(tri, both) + counts
    rank1 = jnp.sum(oh1 * before, axis=-1, keepdims=True)
    rank2 = jnp.sum(oh2 * before, axis=-1, keepdims=True)
    out = jnp.zeros(logits.shape, F32)
    for k, val in enumerate((i1 - EXPERT_LANE0, i2 - EXPERT_LANE0, gate1, gate2, rank1, rank2)):
        out = jnp.where(lane_i == k, val, out)
    route_ref[...] = out
    rt_ref[...] = out.T[0:SUBLANES, :]
    real = jnp.where(s > 0, 1.0, 0.0)
    total = counts + real * jnp.sum(oh1 + oh2, axis=0, keepdims=True)
    carry[...] = total
    cnt_ref[...] = total

    mix = _dot(yr_ref[0], w_ref[0:D_RNN, :]) + _dot(yc_ref[0], w_ref[D_RNN:, :])
    x1 = x_ref[0] + mod_ref[0, 2:3, :] * mix
    x1_ref[0] = x1
    m_new = _norm_mod(x1, g_ref[...], mod_ref[0, 4:5, :], mod_ref[0, 3:4, :])
    m_s[...] = m_new.astype(BF16)
    half = D_MODEL // 2
    packed = pltpu.pack_elementwise([m_new[:, :half], m_new[:, half:]], packed_dtype=BF16)
    for q in range(PACK_TILES):
        mt_ref[pl.ds(q, tm, stride=PACK_TILES), :] = packed[:, q * LANES:(q + 1) * LANES]


def _outproj(x, y_rnn, y_conv, w_out, mod3, norm_g, wr, br):
    bn, n, d = x.shape
    tm = min(OUTPROJ_TILE, n)
    nt = n // tm
    n_tiles = bn * nt
    t_all = bn * n

    def cur(s):
        return jnp.minimum(s, n_tiles - 1)

    def prev(s):
        return jnp.maximum(s - 1, 0)

    def seq_map(s):
        return (cur(s) // nt, cur(s) % nt, 0)

    const = lambda s: (0, 0)
    return pl.pallas_call(
        functools.partial(_outproj_kernel, tm=tm),
        grid=(n_tiles + 1,),
        in_specs=[
            pl.BlockSpec((1, tm, d), seq_map),
            pl.BlockSpec((1, tm, D_RNN), seq_map),
            pl.BlockSpec((1, tm, D_CONV), seq_map),
            pl.BlockSpec((D_RNN + D_CONV, d), const, pipeline_mode=pl.Buffered(1)),
            pl.BlockSpec((1, 6, d), lambda s: (cur(s) // nt, 0, 0)),
            pl.BlockSpec((1, d), const),
            pl.BlockSpec((d, LANES), const),
            pl.BlockSpec((1, LANES), const),
        ],
        out_specs=[
            pl.BlockSpec((1, tm, d), seq_map),
            pl.BlockSpec((tm * PACK_TILES, LANES), lambda s: (cur(s), 0)),
            pl.BlockSpec((tm, LANES), lambda s: (prev(s), 0)),
            pl.BlockSpec((1, LANES), const),
            pl.BlockSpec((SUBLANES, tm), lambda s: (0, prev(s))),
        ],
        out_shape=[
            jax.ShapeDtypeStruct((bn, n, d), F32),
            jax.ShapeDtypeStruct((t_all * PACK_TILES, LANES), jnp.uint32),
            jax.ShapeDtypeStruct((t_all, LANES), F32),
            jax.ShapeDtypeStruct((1, LANES), F32),
            jax.ShapeDtypeStruct((SUBLANES, t_all), F32),
        ],
        scratch_shapes=[pltpu.VMEM((1, LANES), F32), pltpu.VMEM((D_RNN + D_CONV, d), BF16),
                        pltpu.VMEM((tm, d), BF16)],
        compiler_params=pltpu.CompilerParams(
            dimension_semantics=("arbitrary",), vmem_limit_bytes=VMEM_LIMIT),
        name="outproj",
    )(x, y_rnn, y_conv, w_out, mod3, norm_g.reshape(1, d), wr, br)


def _row_tile(ref, row):
    return ref.at[pl.ds(pl.multiple_of(row * ROW_TILES, ROW_TILES), ROW_TILES)]


def _slotmap_kernel(dest_ref, zeros_hbm, asg_ref, sem):
    fill = pltpu.make_async_copy(zeros_hbm, asg_ref, sem)
    fill.start()
    fill.wait()

    def body(c, carry):
        for u in range(DMA_UNROLL):
            a = c * DMA_UNROLL + u
            asg_ref[dest_ref[a]] = a
        return carry
    lax.fori_loop(0, dest_ref.shape[0] // DMA_UNROLL, body, 0)


def _slotmap(dest, n_slots):
    return pl.pallas_call(
        _slotmap_kernel,
        in_specs=[pl.BlockSpec(memory_space=pltpu.SMEM), pl.BlockSpec(memory_space=pl.ANY)],
        out_specs=pl.BlockSpec(memory_space=pltpu.SMEM),
        out_shape=jax.ShapeDtypeStruct((n_slots,), jnp.int32),
        scratch_shapes=[pltpu.SemaphoreType.DMA],
        name="slotmap",
    )(dest, jnp.zeros((n_slots,), jnp.int32))


def _expert_kernel(be_ref, ws_ref, ne_ref, bv_ref, nu_ref, asg_hbm, m_ref, wg_hbm, wu_hbm, wd_hbm, yb_ref,
                   xbuf_a, xbuf_b, idx, isems, wbuf_g, wbuf_u, wbuf_d, wsems, wg_s, wu_s, wd_s):
    j = pl.program_id(0)
    n_used = nu_ref[0]
    last = n_used - 1
    n_tok = m_ref.shape[0] // PACK_TILES

    def idx_copy(blk, sl):
        return pltpu.make_async_copy(asg_hbm.at[blk], idx.at[sl], isems.at[sl])

    def copy_rows(buf, sl, r0, n):
        for u in range(n):
            asg = idx[sl, 0, r0 + u]
            tok = jnp.where(asg >= n_tok, asg - n_tok, asg)
            src = pl.multiple_of(tok * PACK_TILES, PACK_TILES)
            buf[pl.ds((r0 + u) * PACK_TILES, PACK_TILES), :] = m_ref[pl.ds(src, PACK_TILES), :]

    def unpack(buf, rows):
        halves = ([], [])
        for s in range(PACK_TILES):
            word = buf[pl.ds(s, rows, stride=PACK_TILES), :]
            for k in range(2):
                part = pltpu.unpack_elementwise(word, index=k, packed_dtype=BF16, unpacked_dtype=F32)
                halves[k].append(part.astype(BF16))
        return jnp.concatenate(halves[0] + halves[1], axis=-1)

    n_pieces = 2 * D_EXPERT // MXU_TILE + D_MODEL // MXU_TILE
    bounds = [(p * MOE_BLK) // n_pieces for p in range(n_pieces + 1)]

    def compute(cur, nxt, nxt_sl, rows):
        pieces = iter(zip(bounds[:-1], bounds[1:]))

        def dot_pieces(a, w_ref, n0):
            acc = _dot(a, w_ref[:, n0:n0 + MXU_TILE])
            r0, r1 = next(pieces)
            copy_rows(nxt, nxt_sl, r0, r1 - r0)
            return acc

        xb16 = unpack(cur, rows)
        acts = []
        for n0 in range(0, D_EXPERT, MXU_TILE):
            gate = dot_pieces(xb16, wg_s, n0)
            up = dot_pieces(xb16, wu_s, n0)
            acts.append(((gate * jax.nn.sigmoid(gate)) * up).astype(BF16))
        h = jnp.concatenate(acts, axis=-1)
        for n0 in range(0, D_MODEL, MXU_TILE):
            y = dot_pieces(h, wd_s, n0)
            for s in range(MXU_TILE // LANES):
                yb_ref[pl.ds(n0 // LANES + s, rows, stride=ROW_TILES), :] = (
                    y[:, s * LANES:(s + 1) * LANES])
        if rows < MOE_BLK:
            yb_ref[pl.ds(rows * ROW_TILES, (MOE_BLK - rows) * ROW_TILES), :] = jnp.zeros(
                ((MOE_BLK - rows) * ROW_TILES, LANES), F32)

    @pl.when(j >= n_used)
    def _():
        yb_ref[...] = jnp.zeros_like(yb_ref)

    @pl.when(j < n_used)
    def _():
        slot = j % 2
        other = 1 - slot

        @pl.when(j == 0)
        def _():
            idx_copy(0, 0).start()
            idx_copy(0, 0).wait()

            def body(c, carry):
                copy_rows(xbuf_a, 0, c * DMA_UNROLL, DMA_UNROLL)
                return carry
            lax.fori_loop(0, MOE_BLK // DMA_UNROLL, body, 0)
            idx_copy(jnp.minimum(1, last), 1).start()

        e = be_ref[j]
        wslot = ws_ref[j]

        def weight_copies(expert, sl):
            return [pltpu.make_async_copy(src.at[expert], dst.at[sl], wsems.at[sl])
                    for src, dst in ((wg_hbm, wbuf_g), (wu_hbm, wbuf_u), (wd_hbm, wbuf_d))]

        @pl.when(j == 0)
        def _():
            for cp in weight_copies(e, wslot):
                cp.start()

        @pl.when((j == 0) | (e != be_ref[jnp.maximum(j - 1, 0)]))
        def _():
            for cp in weight_copies(e, wslot):
                cp.wait()
            wg_s[...] = wbuf_g[wslot].astype(BF16)
            wu_s[...] = wbuf_u[wslot].astype(BF16)
            wd_s[...] = wbuf_d[wslot].astype(BF16)

            @pl.when(ne_ref[j] >= 0)
            def _():
                for cp in weight_copies(ne_ref[j], 1 - wslot):
                    cp.start()

        idx_copy(0, other).wait()

        short = bv_ref[j] <= MOE_BLK // 2
        for par, (cur, nxt) in enumerate(((xbuf_a, xbuf_b), (xbuf_b, xbuf_a))):
            @pl.when((slot == par) & jnp.logical_not(short))
            def _(cur=cur, nxt=nxt, par=par):
                compute(cur, nxt, 1 - par, MOE_BLK)

            @pl.when((slot == par) & short)
            def _(cur=cur, nxt=nxt, par=par):
                compute(cur, nxt, 1 - par, MOE_BLK // 2)

        @pl.when(j < last)
        def _():
            idx_copy(jnp.minimum(j + 2, last), slot).start()


def _experts(block_e, weight_slot, next_e, block_valid, n_used, slot_asg, mt, w_gate, w_up, w_down, n_blocks):
    return pl.pallas_call(
        _expert_kernel,
        grid_spec=pltpu.PrefetchScalarGridSpec(
            num_scalar_prefetch=5,
            grid=(n_blocks,),
            in_specs=[
                pl.BlockSpec(memory_space=pl.ANY),
                pl.BlockSpec(memory_space=pltpu.VMEM),
                pl.BlockSpec(memory_space=pl.ANY),
                pl.BlockSpec(memory_space=pl.ANY),
                pl.BlockSpec(memory_space=pl.ANY),
            ],
            out_specs=pl.BlockSpec((MOE_BLK * ROW_TILES, LANES), lambda j, *_: (j, 0)),
            scratch_shapes=[
                pltpu.VMEM((MOE_BLK * PACK_TILES, LANES), jnp.uint32),
                pltpu.VMEM((MOE_BLK * PACK_TILES, LANES), jnp.uint32),
                pltpu.SMEM((2, 1, MOE_BLK), jnp.int32),
                pltpu.SemaphoreType.DMA((2,)),
                pltpu.VMEM((2, D_MODEL, D_EXPERT), F32),
                pltpu.VMEM((2, D_MODEL, D_EXPERT), F32),
                pltpu.VMEM((2, D_EXPERT, D_MODEL), F32),
                pltpu.SemaphoreType.DMA((2,)),
                pltpu.VMEM((D_MODEL, D_EXPERT), BF16),
                pltpu.VMEM((D_MODEL, D_EXPERT), BF16),
                pltpu.VMEM((D_EXPERT, D_MODEL), BF16),
            ],
        ),
        out_shape=jax.ShapeDtypeStruct((n_blocks * MOE_BLK * ROW_TILES, LANES), F32),
        compiler_params=pltpu.CompilerParams(
            dimension_semantics=("arbitrary",), vmem_limit_bytes=EXPERT_VMEM_LIMIT),
        name="expert",
    )(block_e, weight_slot, next_e, block_valid, n_used, slot_asg.reshape(n_blocks, 1, MOE_BLK), mt, w_gate,
      w_up, w_down)


def _combine_kernel(dest_ref, yb_hbm, x1_ref, route_ref, mod_ref, g_ref, o_ref, ybuf, sems, *, tc, n_tok):
    i = pl.program_id(0)
    slot = i % 2

    def row_copy(d, sl, k, r):
        return pltpu.make_async_copy(_row_tile(yb_hbm, d), _row_tile(ybuf.at[sl, k], r), sems.at[sl])

    def gather(step, sl):
        def issue(c, carry):
            for u in range(DMA_UNROLL):
                r = c * DMA_UNROLL + u
                for k in range(TOP_K):
                    row_copy(dest_ref[k * n_tok + step * tc + r], sl, k, r).start(priority=k)
            return carry
        lax.fori_loop(0, tc // DMA_UNROLL, issue, 0)

    @pl.when(i == 0)
    def _():
        gather(0, 0)

    @pl.when(i + 1 < pl.num_programs(0))
    def _():
        gather(i + 1, 1 - slot)

    def drain(c, carry):
        for u in range(DMA_UNROLL * TOP_K):
            row_copy(0, slot, 0, 0).wait()
        return carry
    lax.fori_loop(0, tc // DMA_UNROLL, drain, 0)

    def rows(k):
        return jnp.concatenate(
            [ybuf[slot, k, pl.ds(s, tc, stride=ROW_TILES), :] for s in range(ROW_TILES)], axis=-1)

    route = route_ref[...]
    y = route[:, 2:3] * rows(0) + route[:, 3:4] * rows(1)
    x2 = x1_ref[...] + mod_ref[0, 5:6, :] * y
    ms = jnp.mean(x2 * x2, axis=-1, keepdims=True)
    o_ref[...] = x2 * lax.rsqrt(ms + NORM_EPS) * g_ref[...]


def _combine(dest, yb, x1_2d, route, mod3, final_g, seq):
    t_all, d = x1_2d.shape
    tc = min(COMBINE_TILE, seq)
    per_seq = seq // tc
    return pl.pallas_call(
        functools.partial(_combine_kernel, tc=tc, n_tok=t_all),
        grid_spec=pltpu.PrefetchScalarGridSpec(
            num_scalar_prefetch=1,
            grid=(t_all // tc,),
            in_specs=[
                pl.BlockSpec(memory_space=pl.ANY),
                pl.BlockSpec((tc, d), lambda i, dest: (i, 0)),
                pl.BlockSpec((tc, LANES), lambda i, dest: (i, 0)),
                pl.BlockSpec((1, 6, d), lambda i, dest: (i // per_seq, 0, 0)),
                pl.BlockSpec((1, d), lambda i, dest: (0, 0)),
            ],
            out_specs=pl.BlockSpec((tc, d), lambda i, dest: (i, 0)),
            scratch_shapes=[
                pltpu.VMEM((2, TOP_K, tc * ROW_TILES, LANES), F32),
                pltpu.SemaphoreType.DMA((2,)),
            ],
        ),
        out_shape=jax.ShapeDtypeStruct((t_all, d), F32),
        compiler_params=pltpu.CompilerParams(
            dimension_semantics=("arbitrary",), vmem_limit_bytes=VMEM_LIMIT),
        name="combine",
    )(dest, yb, x1_2d, route, mod3, final_g.reshape(1, d))


def kernel(x, c, ctx, c_ctx, ada_w, ada_b, norm1_g, norm2_g, w_in, rnn_conv_w, rnn_conv_b, rg_wa, rg_ba,
           rg_wx, rg_bx, rg_lambda, sc_conv_w, w_out, router_group_w, router_group_b, router_exp_w,
           router_exp_b, exp_w_gate, exp_w_up, exp_w_down, final_norm_g):
    bn, seq, d = x.shape
    assert d == D_MODEL and bn < MOD_ROWS and ada_w.shape[0] == 1
    t_all = bn * seq

    cc = jnp.concatenate([c, c_ctx[None], jnp.zeros((MOD_ROWS - bn - 1, d), F32)], axis=0)
    mod3 = _modulation(cc, ada_w[0], ada_b[0]).reshape(MOD_ROWS, 6, d)

    xr, gr, u, bg = _inproj(x, mod3, None, norm1_g[0], w_in[0], latent=True)
    ctx_len = ctx.shape[1]
    (xr_c,) = _inproj(ctx.reshape(1, bn * ctx_len, d), mod3, bn, norm1_g[0], w_in[0], latent=False)
    xr_c = xr_c.reshape(bn, ctx_len, D_RNN)

    wg, bgate = _gate_weights(rg_wa[0], rg_ba[0], rg_wx[0], rg_bx[0])
    assert D_RNN == D_CONV
    y_rnn, y_conv = _mixers(xr, xr_c, gr, rnn_conv_w[0], rnn_conv_b[0], wg, bgate, rg_lambda[0], u, bg,
                            sc_conv_w[0])

    wr = jnp.zeros((d, LANES), F32)
    wr = wr.at[:, :N_GROUPS].set(router_group_w[0]).at[:, EXPERT_LANE0:EXPERT_LANE0 + N_EXPERTS].set(router_exp_w[0])
    br = jnp.zeros((1, LANES), F32)
    br = br.at[0, :N_GROUPS].set(router_group_b[0]).at[0, EXPERT_LANE0:EXPERT_LANE0 + N_EXPERTS].set(router_exp_b[0])
    x1, mt, route, cnt, route_t = _outproj(x, y_rnn, y_conv, w_out[0], mod3, norm2_g[0], wr.astype(BF16), br)

    n_assign = t_all * TOP_K
    n_blocks = (n_assign + N_EXPERTS * (MOE_BLK - 1) + MOE_BLK - 1) // MOE_BLK
    counts = cnt[0, EXPERT_LANE0:EXPERT_LANE0 + N_EXPERTS].astype(jnp.int32)
    pcounts = (counts + MOE_BLK - 1) // MOE_BLK * MOE_BLK
    pends = jnp.cumsum(pcounts)
    pstarts = pends - pcounts
    experts = route_t[0:TOP_K].astype(jnp.int32)
    ranks = route_t[4:4 + TOP_K].astype(jnp.int32)
    onehot = experts[None] == jnp.arange(N_EXPERTS, dtype=jnp.int32)[:, None, None]
    dest = (ranks + jnp.sum(jnp.where(onehot, pstarts[:, None, None], 0), axis=0)).reshape(n_assign)
    n_used = (pends[-1] // MOE_BLK).astype(jnp.int32)
    blk_start = jnp.arange(n_blocks, dtype=jnp.int32) * MOE_BLK
    block_e = jnp.minimum(jnp.sum(blk_start[:, None] >= pends[None, :], axis=1), N_EXPERTS - 1)
    last_e = jnp.max(jnp.where(counts > 0, jnp.arange(N_EXPERTS, dtype=jnp.int32), 0))
    block_e = jnp.where(blk_start < pends[-1], block_e, last_e).astype(jnp.int32)
    eids = jnp.arange(N_EXPERTS, dtype=jnp.int32)
    used = counts > 0
    slot_of_e = (jnp.cumsum(used.astype(jnp.int32)) - 1) % 2
    later = jnp.where(used[None, :] & (eids[None, :] > eids[:, None]), eids[None, :], N_EXPERTS)
    next_of_e = jnp.min(later, axis=1)
    next_of_e = jnp.where(next_of_e == N_EXPERTS, -1, next_of_e)
    is_e = block_e[:, None] == eids[None, :]
    weight_slot = jnp.sum(jnp.where(is_e, slot_of_e[None, :], 0), axis=1).astype(jnp.int32)
    next_e = jnp.sum(jnp.where(is_e, next_of_e[None, :], 0), axis=1).astype(jnp.int32)
    rows_end = jnp.sum(jnp.where(is_e, (pstarts + counts)[None, :], 0), axis=1)
    block_valid = jnp.clip(rows_end - blk_start, 0, MOE_BLK).astype(jnp.int32)

    n_slots = n_blocks * MOE_BLK
    slot_asg = _slotmap(dest, n_slots)
    yb = _experts(block_e, weight_slot, next_e, block_valid, n_used.reshape(1), slot_asg, mt, exp_w_gate[0],
                  exp_w_up[0], exp_w_down[0], n_blocks)
    out = _combine(dest, yb, x1.reshape(t_all, d), route, mod3, final_norm_g, seq)
    return out.reshape(bn, seq, d)
```
